```python
import math
import jax, jax.numpy as jnp
from jax import lax
import numpy as np

D_MODEL = 2048
BATCH = 8
SEQ = 4096
DEPTH = 2

GROUP_WIDTH = D_MODEL // 4
MIX_WIDTH = 4 * GROUP_WIDTH
D_FF = 4 * D_MODEL
S5_CH = 16
S5_GROUPS = GROUP_WIDTH // S5_CH
S5_STATE = 64
S5_DT_MIN = 1e-3
S5_DT_MAX = 1e-1
RET_HEADS = 4
RET_QK = 64
RET_V = GROUP_WIDTH // RET_HEADS
RET_CHUNK = 128
SWA_HD = 64
SWA_HEADS = GROUP_WIDTH // SWA_HD
SWA_KV_HEADS = 2
WINDOW = 128
MLA_HEADS = 4
MLA_Q_RANK = 384
MLA_KV_RANK = 128
MLA_NOPE = 128
MLA_ROPE = 64
MLA_V = GROUP_WIDTH // MLA_HEADS
MLA_BLOCK = 128
ROPE_BASE = 10000.0
EPS = 1e-6
NEG = -1e30

IN_SPLITS = (
    GROUP_WIDTH,
    RET_HEADS * RET_QK,
    RET_HEADS * RET_QK,
    GROUP_WIDTH,
    GROUP_WIDTH,
    SWA_HEADS * SWA_HD,
    SWA_KV_HEADS * SWA_HD,
    SWA_KV_HEADS * SWA_HD,
    MLA_Q_RANK,
    MLA_KV_RANK,
    MLA_ROPE,
)
N_IN = sum(IN_SPLITS)
SPLIT_IDX = [int(v) for v in np.cumsum(IN_SPLITS)[:-1]]

kernel_name = 'hybrid_parallel_head_group_block'


def rms_norm(x, gain=None):
    xf = x.astype(jnp.float32)
    y = xf * lax.rsqrt(jnp.mean(xf * xf, axis=-1, keepdims=True) + EPS)
    if gain is not None:
        y = y * gain.astype(jnp.float32)
    return y.astype(x.dtype)


def rotary(x):
    seq, d = x.shape[1], x.shape[-1]
    inv = ROPE_BASE ** (-jnp.arange(0, d, 2, dtype=jnp.float32) / d)
    ang = jnp.arange(seq, dtype=jnp.float32)[:, None] * inv[None, :]
    cos = jnp.cos(ang)[None, :, None, :].astype(x.dtype)
    sin = jnp.sin(ang)[None, :, None, :].astype(x.dtype)
    x1, x2 = x[..., :d // 2], x[..., d // 2:]
    return jnp.concatenate([x1 * cos - x2 * sin, x1 * sin + x2 * cos], axis=-1)


def s5_mixer(u, lam_re, lam_im, log_dt, b_re, b_im, c_re, c_im, d_skip, glu_w, glu_b):
    f32 = jnp.float32
    bsz, seq, _ = u.shape
    uf = u.astype(f32).reshape(bsz, seq, S5_GROUPS, S5_CH)
    dt = jnp.exp(log_dt.astype(f32))[:, None]
    lr, li = lam_re.astype(f32), lam_im.astype(f32)
    mag = jnp.exp(lr * dt)
    ar, ai = mag * jnp.cos(li * dt), mag * jnp.sin(li * dt)
    den = lr * lr + li * li
    cr = ((ar - 1.0) * lr + ai * li) / den
    ci = (ai * lr - (ar - 1.0) * li) / den
    br, bi = b_re.astype(f32), b_im.astype(f32)
    bbar_r = cr[..., None] * br - ci[..., None] * bi
    bbar_i = cr[..., None] * bi + ci[..., None] * br
    bu_r = jnp.einsum('bsgh,gph->bsgp', uf, bbar_r)
    bu_i = jnp.einsum('bsgh,gph->bsgp', uf, bbar_i)
    a_r = jnp.broadcast_to(ar, bu_r.shape)
    a_i = jnp.broadcast_to(ai, bu_i.shape)

    def combine(e1, e2):
        a1r, a1i, b1r, b1i = e1
        a2r, a2i, b2r, b2i = e2
        return (a2r * a1r - a2i * a1i, a2r * a1i + a2i * a1r,
                a2r * b1r - a2i * b1i + b2r, a2r * b1i + a2i * b1r + b2i)

    _, _, st_r, st_i = lax.associative_scan(combine, (a_r, a_i, bu_r, bu_i), axis=1)
    y = (jnp.einsum('bsgp,ghp->bsgh', st_r, c_re.astype(f32))
         - jnp.einsum('bsgp,ghp->bsgh', st_i, c_im.astype(f32))
         + d_skip.astype(f32) * uf)
    z = jax.nn.gelu(y.reshape(bsz, seq, GROUP_WIDTH)).astype(u.dtype)
    return z * jax.nn.sigmoid(z @ glu_w + glu_b)


def retention_mixer(q, k, v, g):
    f32 = jnp.float32
    bsz, seq = q.shape[:2]
    nck = seq // RET_CHUNK
    q = rotary(q.astype(f32))
    k = rotary(k.astype(f32)) * (RET_QK ** -0.5)
    v = v.astype(f32)
    log_gamma = jnp.log1p(-(2.0 ** (-5.0 - jnp.arange(RET_HEADS, dtype=f32))))
    idx = jnp.arange(RET_CHUNK, dtype=f32)
    rel = idx[:, None] - idx[None, :]
    decay_intra = jnp.where(rel >= 0, jnp.exp(log_gamma[:, None, None] * jnp.maximum(rel, 0.0)), 0.0)
    zeta = jnp.exp(log_gamma[:, None] * (RET_CHUNK - 1.0 - idx))
    xi = jnp.exp(log_gamma[:, None] * (idx + 1.0))
    gamma_chunk = jnp.exp(log_gamma * RET_CHUNK)
    qc = q.reshape(bsz, nck, RET_CHUNK, RET_HEADS, RET_QK)
    kc = k.reshape(bsz, nck, RET_CHUNK, RET_HEADS, RET_QK)
    vc = v.reshape(bsz, nck, RET_CHUNK, RET_HEADS, RET_V)
    s = jnp.einsum('bnchd,bnmhd->bnhcm', qc, kc) * decay_intra
    o_intra = jnp.einsum('bnhcm,bnmhv->bnchv', s, vc)
    kv = jnp.einsum('bnmhd,hm,bnmhv->nbhdv', kc, zeta, vc)

    def step(state, kv_n):
        return gamma_chunk[None, :, None, None] * state + kv_n, state

    _, prev = lax.scan(step, jnp.zeros_like(kv[0]), kv)
    o_cross = jnp.einsum('bnchd,nbhdv->bnchv', qc, prev) * xi.T[None, None, :, :, None]
    o = (o_intra + o_cross).reshape(bsz, seq, RET_HEADS, RET_V)
    o = o * lax.rsqrt(jnp.mean(o * o, axis=-1, keepdims=True) + EPS)
    return o.reshape(bsz, seq, GROUP_WIDTH).astype(g.dtype) * jax.nn.silu(g)


def swa_mixer(q, k, v, sinks):
    f32 = jnp.float32
    bsz, seq = q.shape[:2]
    nb = seq // WINDOW
    grp = SWA_HEADS // SWA_KV_HEADS
    qb = q.reshape(bsz, nb, WINDOW, SWA_KV_HEADS, grp, SWA_HD)
    pad = ((0, 0), (WINDOW, 0), (0, 0), (0, 0))
    kp = jnp.pad(k, pad).reshape(bsz, nb + 1, WINDOW, SWA_KV_HEADS, SWA_HD)
    vp = jnp.pad(v, pad).reshape(bsz, nb + 1, WINDOW, SWA_KV_HEADS, SWA_HD)
    kb = jnp.concatenate([kp[:, :-1], kp[:, 1:]], axis=2)
    vb = jnp.concatenate([vp[:, :-1], vp[:, 1:]], axis=2)
    s = jnp.einsum('bnqkgd,bnjkd->bnkgqj', qb, kb).astype(f32) * (SWA_HD ** -0.5)
    r = jnp.arange(WINDOW)[:, None]
    j = jnp.arange(2 * WINDOW)[None, :]
    dist = r + WINDOW - j
    blk = jnp.arange(nb)[:, None, None]
    valid = (dist >= 0) & (dist < WINDOW) & (blk * WINDOW + j - WINDOW >= 0)
    s = jnp.where(valid[None, :, None, None], s, NEG)
    sink = sinks.astype(f32).reshape(SWA_KV_HEADS, grp)[None, None, :, :, None, None]
    m = jnp.maximum(jnp.max(s, axis=-1, keepdims=True), sink)
    p = jnp.exp(s - m)
    denom = jnp.sum(p, axis=-1, keepdims=True) + jnp.exp(sink - m)
    o = jnp.einsum('bnkgqj,bnjkd->bnqkgd', (p / denom).astype(v.dtype), vb)
    return o.reshape(bsz, seq, GROUP_WIDTH)


def mla_mixer(c_q, c_kv, k_rope, q_norm, kv_norm, w_uq, w_ukv):
    f32 = jnp.float32
    bsz, seq = c_q.shape[:2]
    q = (rms_norm(c_q, q_norm) @ w_uq).reshape(bsz, seq, MLA_HEADS, MLA_NOPE + MLA_ROPE)
    q_nope, q_rope = q[..., :MLA_NOPE], rotary(q[..., MLA_NOPE:])
    kv = (rms_norm(c_kv, kv_norm) @ w_ukv).reshape(bsz, seq, MLA_HEADS, MLA_NOPE + MLA_V)
    k_nope, v = kv[..., :MLA_NOPE], kv[..., MLA_NOPE:]
    k_r = rotary(k_rope[:, :, None, :])[:, :, 0]
    scale = (MLA_NOPE + MLA_ROPE) ** -0.5
    nb = seq // MLA_BLOCK
    k_pos = jnp.arange(seq)

    def block(args):
        i, qn, qr = args
        s = (jnp.einsum('bqhd,bkhd->bhqk', qn, k_nope)
             + jnp.einsum('bqhr,bkr->bhqk', qr, k_r)).astype(f32) * scale
        q_pos = i * MLA_BLOCK + jnp.arange(MLA_BLOCK)
        s = jnp.where(k_pos[None, :] <= q_pos[:, None], s, NEG)
        p = jax.nn.softmax(s, axis=-1).astype(v.dtype)
        return jnp.einsum('bhqk,bkhv->bqhv', p, v)

    qn_b = q_nope.reshape(bsz, nb, MLA_BLOCK, MLA_HEADS, MLA_NOPE).transpose(1, 0, 2, 3, 4)
    qr_b = q_rope.reshape(bsz, nb, MLA_BLOCK, MLA_HEADS, MLA_ROPE).transpose(1, 0, 2, 3, 4)
    o = lax.map(block, (jnp.arange(nb), qn_b, qr_b))
    return o.transpose(1, 0, 2, 3, 4).reshape(bsz, seq, GROUP_WIDTH)


def _fwd_setup_inputs(seed: int = 0) -> dict:
    key = jax.random.key(seed)
    ks = jax.random.split(key, 32)
    f32 = jnp.float32
    nrm = lambda k, shape, scale: scale * jax.random.normal(k, shape, f32)
    L, D, G, P, H, GW = DEPTH, D_MODEL, S5_GROUPS, S5_STATE, S5_CH, GROUP_WIDTH
    return {
        'x': nrm(ks[0], (BATCH, SEQ, D), 1.0),
        'c': nrm(ks[1], (BATCH, D), 1.0),
        'norm1_g': 1.0 + nrm(ks[2], (L, D), 0.02),
        'norm2_g': 1.0 + nrm(ks[3], (L, D), 0.02),
        'ada_w': nrm(ks[4], (L, D, 6 * D), 0.5 * D ** -0.5),
        'ada_b': nrm(ks[5], (L, 6 * D), 0.02),
        'w_in': nrm(ks[6], (L, D, N_IN), D ** -0.5),
        's5_lambda_re': -0.5 + nrm(ks[7], (L, G, P), 0.01),
        's5_lambda_im': math.pi * jnp.arange(P, dtype=f32) + nrm(ks[8], (L, G, P), 0.01),
        's5_log_dt': jax.random.uniform(ks[9], (L, G), f32, math.log(S5_DT_MIN), math.log(S5_DT_MAX)),
        's5_b_re': nrm(ks[10], (L, G, P, H), (2 * H) ** -0.5),
        's5_b_im': nrm(ks[11], (L, G, P, H), (2 * H) ** -0.5),
        's5_c_re': nrm(ks[12], (L, G, H, P), (2 * P) ** -0.5),
        's5_c_im': nrm(ks[13], (L, G, H, P), (2 * P) ** -0.5),
        's5_d': nrm(ks[14], (L, G, H), 1.0),
        's5_glu_w': nrm(ks[15], (L, GW, GW), GW ** -0.5),
        's5_glu_b': nrm(ks[16], (L, GW), 0.01),
        'swa_sinks': nrm(ks[17], (L, SWA_HEADS), 0.5),
        'mla_q_norm': 1.0 + nrm(ks[18], (L, MLA_Q_RANK), 0.02),
        'mla_kv_norm': 1.0 + nrm(ks[19], (L, MLA_KV_RANK), 0.02),
        'mla_w_uq': nrm(ks[20], (L, MLA_Q_RANK, MLA_HEADS * (MLA_NOPE + MLA_ROPE)), MLA_Q_RANK ** -0.5),
        'mla_w_ukv': nrm(ks[21], (L, MLA_KV_RANK, MLA_HEADS * (MLA_NOPE + MLA_V)), MLA_KV_RANK ** -0.5),
        'w_out': nrm(ks[22], (L, MIX_WIDTH, D), MIX_WIDTH ** -0.5),
        'mlp_w1': nrm(ks[23], (L, D, D_FF), D ** -0.5),
        'mlp_w2': nrm(ks[24], (L, D_FF, D), D_FF ** -0.5),
        'final_norm_g': 1.0 + nrm(ks[25], (D,), 0.02),
    }


def _fwd_reference(x, c, norm1_g, norm2_g, ada_w, ada_b, w_in, s5_lambda_re, s5_lambda_im, s5_log_dt,
              s5_b_re, s5_b_im, s5_c_re, s5_c_im, s5_d, s5_glu_w, s5_glu_b, swa_sinks,
              mla_q_norm, mla_kv_norm, mla_w_uq, mla_w_ukv, w_out, mlp_w1, mlp_w2, final_norm_g):
    bsz, seq, _ = x.shape
    h = x
    c_act = jax.nn.silu(c)
    for l in range(DEPTH):
        mod = c_act @ ada_w[l] + ada_b[l]
        sh1, sc1, gt1, sh2, sc2, gt2 = [m[:, None, :] for m in jnp.split(mod, 6, axis=-1)]
        a = rms_norm(h, norm1_g[l]) * (1 + sc1) + sh1
        proj = a @ w_in[l]
        (u_s5, r_q, r_k, r_v, r_g, w_q, w_k, w_v,
         m_cq, m_ckv, m_kr) = jnp.split(proj, SPLIT_IDX, axis=-1)
        y_s5 = s5_mixer(u_s5, s5_lambda_re[l], s5_lambda_im[l], s5_log_dt[l], s5_b_re[l], s5_b_im[l],
                        s5_c_re[l], s5_c_im[l], s5_d[l], s5_glu_w[l], s5_glu_b[l])
        y_ret = retention_mixer(r_q.reshape(bsz, seq, RET_HEADS, RET_QK),
                                r_k.reshape(bsz, seq, RET_HEADS, RET_QK),
                                r_v.reshape(bsz, seq, RET_HEADS, RET_V), r_g)
        y_swa = swa_mixer(w_q.reshape(bsz, seq, SWA_HEADS, SWA_HD),
                          w_k.reshape(bsz, seq, SWA_KV_HEADS, SWA_HD),
                          w_v.reshape(bsz, seq, SWA_KV_HEADS, SWA_HD), swa_sinks[l])
        y_mla = mla_mixer(m_cq, m_ckv, m_kr, mla_q_norm[l], mla_kv_norm[l], mla_w_uq[l], mla_w_ukv[l])
        mixed = jnp.concatenate([y_s5, y_ret, y_swa, y_mla], axis=-1) @ w_out[l]
        h = h + gt1 * mixed
        a = rms_norm(h, norm2_g[l]) * (1 + sc2) + sh2
        h = h + gt2 * (jnp.square(jax.nn.relu(a @ mlp_w1[l])) @ mlp_w2[l])
    return rms_norm(h, final_norm_g)


import jax as _jax
import jax.numpy as _jnp

TWIN_FORMAT = 'train_step'
FWD_PARAMS = ['x', 'c', 'norm1_g', 'norm2_g', 'ada_w', 'ada_b', 'w_in', 's5_lambda_re', 's5_lambda_im', 's5_log_dt', 's5_b_re', 's5_b_im', 's5_c_re', 's5_c_im', 's5_d', 's5_glu_w', 's5_glu_b', 'swa_sinks', 'mla_q_norm', 'mla_kv_norm', 'mla_w_uq', 'mla_w_ukv', 'w_out', 'mlp_w1', 'mlp_w2', 'final_norm_g']
TWIN_WEIGHTS = ['norm1_g', 'norm2_g', 'ada_w', 'ada_b', 'w_in', 's5_lambda_re', 's5_lambda_im', 's5_log_dt', 's5_b_re', 's5_b_im', 's5_c_re', 's5_c_im', 's5_d', 's5_glu_w', 's5_glu_b', 'swa_sinks', 'mla_q_norm', 'mla_kv_norm', 'mla_w_uq', 'mla_w_ukv', 'w_out', 'mlp_w1', 'mlp_w2', 'final_norm_g']
TWIN_DIFF_INPUT = 'x'
TWIN_INPUTS = ['x', 'c', 'norm1_g', 'norm2_g', 'ada_w', 'ada_b', 'w_in', 's5_lambda_re', 's5_lambda_im', 's5_log_dt', 's5_b_re', 's5_b_im', 's5_c_re', 's5_c_im', 's5_d', 's5_glu_w', 's5_glu_b', 'swa_sinks', 'mla_q_norm', 'mla_kv_norm', 'mla_w_uq', 'mla_w_ukv', 'w_out', 'mlp_w1', 'mlp_w2', 'final_norm_g', 'loss_target', 'm_norm1_g', 'm_norm2_g', 'm_ada_w', 'm_ada_b', 'm_w_in', 'm_s5_lambda_re', 'm_s5_lambda_im', 'm_s5_log_dt', 'm_s5_b_re', 'm_s5_b_im', 'm_s5_c_re', 'm_s5_c_im', 'm_s5_d', 'm_s5_glu_w', 'm_s5_glu_b', 'm_swa_sinks', 'm_mla_q_norm', 'm_mla_kv_norm', 'm_mla_w_uq', 'm_mla_w_ukv', 'm_w_out', 'm_mlp_w1', 'm_mlp_w2', 'm_final_norm_g', 'v_norm1_g', 'v_norm2_g', 'v_ada_w', 'v_ada_b', 'v_w_in', 'v_s5_lambda_re', 'v_s5_lambda_im', 'v_s5_log_dt', 'v_s5_b_re', 'v_s5_b_im', 'v_s5_c_re', 'v_s5_c_im', 'v_s5_d', 'v_s5_glu_w', 'v_s5_glu_b', 'v_swa_sinks', 'v_mla_q_norm', 'v_mla_kv_norm', 'v_mla_w_uq', 'v_mla_w_ukv', 'v_w_out', 'v_mlp_w1', 'v_mlp_w2', 'v_final_norm_g']
TWIN_OUTPUTS = ['loss', 'grad_x', 'grad_norm1_g', 'grad_norm2_g', 'grad_ada_w', 'grad_ada_b', 'grad_w_in', 'grad_s5_lambda_re', 'grad_s5_lambda_im', 'grad_s5_log_dt', 'grad_s5_b_re', 'grad_s5_b_im', 'grad_s5_c_re', 'grad_s5_c_im', 'grad_s5_d', 'grad_s5_glu_w', 'grad_s5_glu_b', 'grad_swa_sinks', 'grad_mla_q_norm', 'grad_mla_kv_norm', 'grad_mla_w_uq', 'grad_mla_w_ukv', 'grad_w_out', 'grad_mlp_w1', 'grad_mlp_w2', 'grad_final_norm_g', 'delta_norm1_g', 'delta_norm2_g', 'delta_ada_w', 'delta_ada_b', 'delta_w_in', 'delta_s5_lambda_re', 'delta_s5_lambda_im', 'delta_s5_log_dt', 'delta_s5_b_re', 'delta_s5_b_im', 'delta_s5_c_re', 'delta_s5_c_im', 'delta_s5_d', 'delta_s5_glu_w', 'delta_s5_glu_b', 'delta_swa_sinks', 'delta_mla_q_norm', 'delta_mla_kv_norm', 'delta_mla_w_uq', 'delta_mla_w_ukv', 'delta_w_out', 'delta_mlp_w1', 'delta_mlp_w2', 'delta_final_norm_g', 'new_m_norm1_g', 'new_m_norm2_g', 'new_m_ada_w', 'new_m_ada_b', 'new_m_w_in', 'new_m_s5_lambda_re', 'new_m_s5_lambda_im', 'new_m_s5_log_dt', 'new_m_s5_b_re', 'new_m_s5_b_im', 'new_m_s5_c_re', 'new_m_s5_c_im', 'new_m_s5_d', 'new_m_s5_glu_w', 'new_m_s5_glu_b', 'new_m_swa_sinks', 'new_m_mla_q_norm', 'new_m_mla_kv_norm', 'new_m_mla_w_uq', 'new_m_mla_w_ukv', 'new_m_w_out', 'new_m_mlp_w1', 'new_m_mlp_w2', 'new_m_final_norm_g', 'new_v_norm1_g', 'new_v_norm2_g', 'new_v_ada_w', 'new_v_ada_b', 'new_v_w_in', 'new_v_s5_lambda_re', 'new_v_s5_lambda_im', 'new_v_s5_log_dt', 'new_v_s5_b_re', 'new_v_s5_b_im', 'new_v_s5_c_re', 'new_v_s5_c_im', 'new_v_s5_d', 'new_v_s5_glu_w', 'new_v_s5_glu_b', 'new_v_swa_sinks', 'new_v_mla_q_norm', 'new_v_mla_kv_norm', 'new_v_mla_w_uq', 'new_v_mla_w_ukv', 'new_v_w_out', 'new_v_mlp_w1', 'new_v_mlp_w2', 'new_v_final_norm_g']
TWIN_LEAF_KINDS = {'loss': 'loss', 'grad_x': 'grad_x', 'grad_norm1_g': 'grad_w', 'grad_norm2_g': 'grad_w', 'grad_ada_w': 'grad_w', 'grad_ada_b': 'grad_w', 'grad_w_in': 'grad_w', 'grad_s5_lambda_re': 'grad_w', 'grad_s5_lambda_im': 'grad_w', 'grad_s5_log_dt': 'grad_w', 'grad_s5_b_re': 'grad_w', 'grad_s5_b_im': 'grad_w', 'grad_s5_c_re': 'grad_w', 'grad_s5_c_im': 'grad_w', 'grad_s5_d': 'grad_w', 'grad_s5_glu_w': 'grad_w', 'grad_s5_glu_b': 'grad_w', 'grad_swa_sinks': 'grad_w', 'grad_mla_q_norm': 'grad_w', 'grad_mla_kv_norm': 'grad_w', 'grad_mla_w_uq': 'grad_w', 'grad_mla_w_ukv': 'grad_w', 'grad_w_out': 'grad_w', 'grad_mlp_w1': 'grad_w', 'grad_mlp_w2': 'grad_w', 'grad_final_norm_g': 'grad_w', 'delta_norm1_g': 'delta_w', 'delta_norm2_g': 'delta_w', 'delta_ada_w': 'delta_w', 'delta_ada_b': 'delta_w', 'delta_w_in': 'delta_w', 'delta_s5_lambda_re': 'delta_w', 'delta_s5_lambda_im': 'delta_w', 'delta_s5_log_dt': 'delta_w', 'delta_s5_b_re': 'delta_w', 'delta_s5_b_im': 'delta_w', 'delta_s5_c_re': 'delta_w', 'delta_s5_c_im': 'delta_w', 'delta_s5_d': 'delta_w', 'delta_s5_glu_w': 'delta_w', 'delta_s5_glu_b': 'delta_w', 'delta_swa_sinks': 'delta_w', 'delta_mla_q_norm': 'delta_w', 'delta_mla_kv_norm': 'delta_w', 'delta_mla_w_uq': 'delta_w', 'delta_mla_w_ukv': 'delta_w', 'delta_w_out': 'delta_w', 'delta_mlp_w1': 'delta_w', 'delta_mlp_w2': 'delta_w', 'delta_final_norm_g': 'delta_w', 'new_m_norm1_g': 'new_m', 'new_m_norm2_g': 'new_m', 'new_m_ada_w': 'new_m', 'new_m_ada_b': 'new_m', 'new_m_w_in': 'new_m', 'new_m_s5_lambda_re': 'new_m', 'new_m_s5_lambda_im': 'new_m', 'new_m_s5_log_dt': 'new_m', 'new_m_s5_b_re': 'new_m', 'new_m_s5_b_im': 'new_m', 'new_m_s5_c_re': 'new_m', 'new_m_s5_c_im': 'new_m', 'new_m_s5_d': 'new_m', 'new_m_s5_glu_w': 'new_m', 'new_m_s5_glu_b': 'new_m', 'new_m_swa_sinks': 'new_m', 'new_m_mla_q_norm': 'new_m', 'new_m_mla_kv_norm': 'new_m', 'new_m_mla_w_uq': 'new_m', 'new_m_mla_w_ukv': 'new_m', 'new_m_w_out': 'new_m', 'new_m_mlp_w1': 'new_m', 'new_m_mlp_w2': 'new_m', 'new_m_final_norm_g': 'new_m', 'new_v_norm1_g': 'new_v', 'new_v_norm2_g': 'new_v', 'new_v_ada_w': 'new_v', 'new_v_ada_b': 'new_v', 'new_v_w_in': 'new_v', 'new_v_s5_lambda_re': 'new_v', 'new_v_s5_lambda_im': 'new_v', 'new_v_s5_log_dt': 'new_v', 'new_v_s5_b_re': 'new_v', 'new_v_s5_b_im': 'new_v', 'new_v_s5_c_re': 'new_v', 'new_v_s5_c_im': 'new_v', 'new_v_s5_d': 'new_v', 'new_v_s5_glu_w': 'new_v', 'new_v_s5_glu_b': 'new_v', 'new_v_swa_sinks': 'new_v', 'new_v_mla_q_norm': 'new_v', 'new_v_mla_kv_norm': 'new_v', 'new_v_mla_w_uq': 'new_v', 'new_v_mla_w_ukv': 'new_v', 'new_v_w_out': 'new_v', 'new_v_mlp_w1': 'new_v', 'new_v_mlp_w2': 'new_v', 'new_v_final_norm_g': 'new_v'}


def _forward(args):
    return _fwd_reference(*[args[k] for k in FWD_PARAMS])


def _output_shape():
    def fwd():
        inp = _fwd_setup_inputs(0)
        return _fwd_reference(*[inp[k] for k in FWD_PARAMS])
    out = _jax.eval_shape(fwd)
    return out.shape, out.dtype

N_MICROBATCH = 1
ADAM_LR = 0.001
ADAM_B1 = 0.9
ADAM_B2 = 0.999
ADAM_EPS = 1e-08
ADAM_WD = 0.01
ADAM_STEP = 10
PER_EXAMPLE_BATCH_AXIS = {'x': 0, 'c': 0, 'loss_target': 0}
SHARED_INPUTS = []
_WEIGHT_DTYPES = {'norm1_g': _jnp.float32, 'norm2_g': _jnp.float32, 'ada_w': _jnp.float32, 'ada_b': _jnp.float32, 'w_in': _jnp.float32, 's5_lambda_re': _jnp.float32, 's5_lambda_im': _jnp.float32, 's5_log_dt': _jnp.float32, 's5_b_re': _jnp.float32, 's5_b_im': _jnp.float32, 's5_c_re': _jnp.float32, 's5_c_im': _jnp.float32, 's5_d': _jnp.float32, 's5_glu_w': _jnp.float32, 's5_glu_b': _jnp.float32, 'swa_sinks': _jnp.float32, 'mla_q_norm': _jnp.float32, 'mla_kv_norm': _jnp.float32, 'mla_w_uq': _jnp.float32, 'mla_w_ukv': _jnp.float32, 'w_out': _jnp.float32, 'mlp_w1': _jnp.float32, 'mlp_w2': _jnp.float32, 'final_norm_g': _jnp.float32}
MOMENT_SCALE = {'norm1_g': 1.860149e-02, 'norm2_g': 3.777314e-02, 'ada_w': 3.850120e-02, 'ada_b': 7.110232e-02, 'w_in': 1.605193e-02, 's5_lambda_re': 6.447327e-04, 's5_lambda_im': 5.900234e-04, 's5_log_dt': 2.533074e-01, 's5_b_re': 4.048669e-04, 's5_b_im': 4.352607e-04, 's5_c_re': 8.296774e-04, 's5_c_im': 8.121968e-04, 's5_d': 1.146581e-02, 's5_glu_w': 3.239383e-03, 's5_glu_b': 4.798188e-03, 'swa_sinks': 5.440729e-03, 'mla_q_norm': 5.071533e-03, 'mla_kv_norm': 2.048149e-02, 'mla_w_uq': 3.680759e-03, 'mla_w_ukv': 7.501554e-03, 'w_out': 1.220886e-02, 'mlp_w1': 1.977655e-02, 'mlp_w2': 3.676698e-02, 'final_norm_g': 1.608935e+01}


def _to_microbatches(a, axis):
    t = _jnp.moveaxis(a, axis, 0)
    t = t.reshape((N_MICROBATCH, t.shape[0] // N_MICROBATCH) + t.shape[1:])
    return _jnp.moveaxis(t, 1, axis + 1)


def setup_inputs(seed: int = 0) -> dict:
    inp = _fwd_setup_inputs(seed)
    key = _jax.random.fold_in(_jax.random.key(seed), 7919)
    shape, _ = _output_shape()
    out = dict(inp)
    out["loss_target"] = _jax.random.normal(_jax.random.fold_in(key, 0), shape, _jnp.float32)
    for i, name in enumerate(TWIN_WEIGHTS):
        w = inp[name].astype(_jnp.float32)
        if MOMENT_SCALE is None:
            s = _jnp.sqrt(_jnp.mean(_jnp.square(w)) + 1e-30)
        else:
            s = MOMENT_SCALE[name]
        km, kv = _jax.random.split(_jax.random.fold_in(key, i + 1))
        out[name] = w
        out["m_" + name] = s * _jax.random.normal(km, w.shape, _jnp.float32)
        out["v_" + name] = (s * s) * _jax.random.uniform(kv, w.shape, _jnp.float32, 0.5, 1.5)
    if N_MICROBATCH > 1:
        for name, axis in PER_EXAMPLE_BATCH_AXIS.items():
            out[name] = _to_microbatches(out[name], axis)
    return {'x': out['x'], 'c': out['c'], 'norm1_g': out['norm1_g'], 'norm2_g': out['norm2_g'], 'ada_w': out['ada_w'], 'ada_b': out['ada_b'], 'w_in': out['w_in'], 's5_lambda_re': out['s5_lambda_re'], 's5_lambda_im': out['s5_lambda_im'], 's5_log_dt': out['s5_log_dt'], 's5_b_re': out['s5_b_re'], 's5_b_im': out['s5_b_im'], 's5_c_re': out['s5_c_re'], 's5_c_im': out['s5_c_im'], 's5_d': out['s5_d'], 's5_glu_w': out['s5_glu_w'], 's5_glu_b': out['s5_glu_b'], 'swa_sinks': out['swa_sinks'], 'mla_q_norm': out['mla_q_norm'], 'mla_kv_norm': out['mla_kv_norm'], 'mla_w_uq': out['mla_w_uq'], 'mla_w_ukv': out['mla_w_ukv'], 'w_out': out['w_out'], 'mlp_w1': out['mlp_w1'], 'mlp_w2': out['mlp_w2'], 'final_norm_g': out['final_norm_g'], 'loss_target': out['loss_target'], 'm_norm1_g': out['m_norm1_g'], 'm_norm2_g': out['m_norm2_g'], 'm_ada_w': out['m_ada_w'], 'm_ada_b': out['m_ada_b'], 'm_w_in': out['m_w_in'], 'm_s5_lambda_re': out['m_s5_lambda_re'], 'm_s5_lambda_im': out['m_s5_lambda_im'], 'm_s5_log_dt': out['m_s5_log_dt'], 'm_s5_b_re': out['m_s5_b_re'], 'm_s5_b_im': out['m_s5_b_im'], 'm_s5_c_re': out['m_s5_c_re'], 'm_s5_c_im': out['m_s5_c_im'], 'm_s5_d': out['m_s5_d'], 'm_s5_glu_w': out['m_s5_glu_w'], 'm_s5_glu_b': out['m_s5_glu_b'], 'm_swa_sinks': out['m_swa_sinks'], 'm_mla_q_norm': out['m_mla_q_norm'], 'm_mla_kv_norm': out['m_mla_kv_norm'], 'm_mla_w_uq': out['m_mla_w_uq'], 'm_mla_w_ukv': out['m_mla_w_ukv'], 'm_w_out': out['m_w_out'], 'm_mlp_w1': out['m_mlp_w1'], 'm_mlp_w2': out['m_mlp_w2'], 'm_final_norm_g': out['m_final_norm_g'], 'v_norm1_g': out['v_norm1_g'], 'v_norm2_g': out['v_norm2_g'], 'v_ada_w': out['v_ada_w'], 'v_ada_b': out['v_ada_b'], 'v_w_in': out['v_w_in'], 'v_s5_lambda_re': out['v_s5_lambda_re'], 'v_s5_lambda_im': out['v_s5_lambda_im'], 'v_s5_log_dt': out['v_s5_log_dt'], 'v_s5_b_re': out['v_s5_b_re'], 'v_s5_b_im': out['v_s5_b_im'], 'v_s5_c_re': out['v_s5_c_re'], 'v_s5_c_im': out['v_s5_c_im'], 'v_s5_d': out['v_s5_d'], 'v_s5_glu_w': out['v_s5_glu_w'], 'v_s5_glu_b': out['v_s5_glu_b'], 'v_swa_sinks': out['v_swa_sinks'], 'v_mla_q_norm': out['v_mla_q_norm'], 'v_mla_kv_norm': out['v_mla_kv_norm'], 'v_mla_w_uq': out['v_mla_w_uq'], 'v_mla_w_ukv': out['v_mla_w_ukv'], 'v_w_out': out['v_w_out'], 'v_mlp_w1': out['v_mlp_w1'], 'v_mlp_w2': out['v_mlp_w2'], 'v_final_norm_g': out['v_final_norm_g']}


def _loss(weights, diff, rest, loss_target):
    with _jax.named_scope("forward"):
        args = {**rest, TWIN_DIFF_INPUT: diff, **{k: w.astype(_WEIGHT_DTYPES[k]) for k, w in weights.items()}}
        y = _forward(args)
    with _jax.named_scope("loss_head"):
        err = _jnp.square(y.astype(_jnp.float32) - loss_target)
        return 0.5 * _jnp.sum(_jnp.mean(err, axis=-1)) if err.ndim else 0.5 * err


def _adamw(w, g, m, v):
    m = ADAM_B1 * m + (1.0 - ADAM_B1) * g
    v = ADAM_B2 * v + (1.0 - ADAM_B2) * _jnp.square(g)
    m_hat = m / (1.0 - ADAM_B1 ** ADAM_STEP)
    v_hat = v / (1.0 - ADAM_B2 ** ADAM_STEP)
    delta = -ADAM_LR * (m_hat / (_jnp.sqrt(v_hat) + ADAM_EPS) + ADAM_WD * w)
    return delta, m, v


def reference(x, c, norm1_g, norm2_g, ada_w, ada_b, w_in, s5_lambda_re, s5_lambda_im, s5_log_dt, s5_b_re, s5_b_im, s5_c_re, s5_c_im, s5_d, s5_glu_w, s5_glu_b, swa_sinks, mla_q_norm, mla_kv_norm, mla_w_uq, mla_w_ukv, w_out, mlp_w1, mlp_w2, final_norm_g, loss_target, m_norm1_g, m_norm2_g, m_ada_w, m_ada_b, m_w_in, m_s5_lambda_re, m_s5_lambda_im, m_s5_log_dt, m_s5_b_re, m_s5_b_im, m_s5_c_re, m_s5_c_im, m_s5_d, m_s5_glu_w, m_s5_glu_b, m_swa_sinks, m_mla_q_norm, m_mla_kv_norm, m_mla_w_uq, m_mla_w_ukv, m_w_out, m_mlp_w1, m_mlp_w2, m_final_norm_g, v_norm1_g, v_norm2_g, v_ada_w, v_ada_b, v_w_in, v_s5_lambda_re, v_s5_lambda_im, v_s5_log_dt, v_s5_b_re, v_s5_b_im, v_s5_c_re, v_s5_c_im, v_s5_d, v_s5_glu_w, v_s5_glu_b, v_swa_sinks, v_mla_q_norm, v_mla_kv_norm, v_mla_w_uq, v_mla_w_ukv, v_w_out, v_mlp_w1, v_mlp_w2, v_final_norm_g):
    given = dict(x=x, c=c, norm1_g=norm1_g, norm2_g=norm2_g, ada_w=ada_w, ada_b=ada_b, w_in=w_in, s5_lambda_re=s5_lambda_re, s5_lambda_im=s5_lambda_im, s5_log_dt=s5_log_dt, s5_b_re=s5_b_re, s5_b_im=s5_b_im, s5_c_re=s5_c_re, s5_c_im=s5_c_im, s5_d=s5_d, s5_glu_w=s5_glu_w, s5_glu_b=s5_glu_b, swa_sinks=swa_sinks, mla_q_norm=mla_q_norm, mla_kv_norm=mla_kv_norm, mla_w_uq=mla_w_uq, mla_w_ukv=mla_w_ukv, w_out=w_out, mlp_w1=mlp_w1, mlp_w2=mlp_w2, final_norm_g=final_norm_g, loss_target=loss_target, m_norm1_g=m_norm1_g, m_norm2_g=m_norm2_g, m_ada_w=m_ada_w, m_ada_b=m_ada_b, m_w_in=m_w_in, m_s5_lambda_re=m_s5_lambda_re, m_s5_lambda_im=m_s5_lambda_im, m_s5_log_dt=m_s5_log_dt, m_s5_b_re=m_s5_b_re, m_s5_b_im=m_s5_b_im, m_s5_c_re=m_s5_c_re, m_s5_c_im=m_s5_c_im, m_s5_d=m_s5_d, m_s5_glu_w=m_s5_glu_w, m_s5_glu_b=m_s5_glu_b, m_swa_sinks=m_swa_sinks, m_mla_q_norm=m_mla_q_norm, m_mla_kv_norm=m_mla_kv_norm, m_mla_w_uq=m_mla_w_uq, m_mla_w_ukv=m_mla_w_ukv, m_w_out=m_w_out, m_mlp_w1=m_mlp_w1, m_mlp_w2=m_mlp_w2, m_final_norm_g=m_final_norm_g, v_norm1_g=v_norm1_g, v_norm2_g=v_norm2_g, v_ada_w=v_ada_w, v_ada_b=v_ada_b, v_w_in=v_w_in, v_s5_lambda_re=v_s5_lambda_re, v_s5_lambda_im=v_s5_lambda_im, v_s5_log_dt=v_s5_log_dt, v_s5_b_re=v_s5_b_re, v_s5_b_im=v_s5_b_im, v_s5_c_re=v_s5_c_re, v_s5_c_im=v_s5_c_im, v_s5_d=v_s5_d, v_s5_glu_w=v_s5_glu_w, v_s5_glu_b=v_s5_glu_b, v_swa_sinks=v_swa_sinks, v_mla_q_norm=v_mla_q_norm, v_mla_kv_norm=v_mla_kv_norm, v_mla_w_uq=v_mla_w_uq, v_mla_w_ukv=v_mla_w_ukv, v_w_out=v_w_out, v_mlp_w1=v_mlp_w1, v_mlp_w2=v_mlp_w2, v_final_norm_g=v_final_norm_g)
    weights = {n: given[n] for n in TWIN_WEIGHTS}
    shared = {n: given[n] for n in SHARED_INPUTS}
    per_example = {n: given[n] for n in ['x', 'c']}
    grad_fn = _jax.value_and_grad(_loss, argnums=(0, 1))

    def one_microbatch(ex, loss_target):
        ex = dict(ex)
        diff = ex.pop(TWIN_DIFF_INPUT)
        return grad_fn(weights, diff, {**shared, **ex}, loss_target)

    if N_MICROBATCH == 1:
        loss, (grad_w, grad_x) = one_microbatch(per_example, given["loss_target"])
    else:
        def body(carry, xs):
            loss_sum, grad_sum = carry
            l_k, (gw_k, gx_k) = one_microbatch(xs[0], xs[1])
            with _jax.named_scope("update"):
                return (loss_sum + l_k, _jax.tree.map(_jnp.add, grad_sum, gw_k)), gx_k

        init = (_jnp.zeros((), _jnp.float32), _jax.tree.map(_jnp.zeros_like, weights))
        (loss, grad_w), grad_x = _jax.lax.scan(body, init, (per_example, given["loss_target"]))
    with _jax.named_scope("update"):
        delta_w, new_m, new_v = {}, {}, {}
        for n in TWIN_WEIGHTS:
            delta_w[n], new_m[n], new_v[n] = _adamw(weights[n], grad_w[n], given["m_" + n], given["v_" + n])
    return (loss, grad_x, *[grad_w[n] for n in TWIN_WEIGHTS], *[delta_w[n] for n in TWIN_WEIGHTS],
            *[new_m[n] for n in TWIN_WEIGHTS], *[new_v[n] for n in TWIN_WEIGHTS])
```

```python
import functools
import math

import numpy as np
import jax
import jax.numpy as jnp
from jax import lax
from jax.experimental import pallas as pl
from jax.experimental.pallas import tpu as pltpu

F32 = jnp.float32
BF16 = jnp.bfloat16
_MXU_DTYPE = jnp.bfloat16

N_DEV = 8
D_MODEL = 2048
DEPTH = 2
GROUP_WIDTH = 512
D_FF = 8192
S5_CH, S5_GROUPS, S5_STATE = 16, 32, 64
S5_WIDTH = S5_GROUPS * S5_STATE
RET_HEADS, RET_QK, RET_V, RET_CHUNK = 4, 64, 128, 128
SWA_HD, SWA_HEADS, SWA_KV_HEADS, WINDOW = 64, 8, 2, 128
MLA_HEADS, MLA_Q_RANK, MLA_KV_RANK, MLA_NOPE, MLA_ROPE, MLA_V = 4, 384, 128, 128, 64, 128
ROPE_BASE = 10000.0
EPS = 1e-6
NEG = -1e30
N_IN = 3392
ADAM_LR, ADAM_B1, ADAM_B2, ADAM_EPS, ADAM_WD, ADAM_STEP = 0.001, 0.9, 0.999, 1e-08, 0.01, 10

VMEM_LIMIT_BYTES = 52 * 1024 * 1024
MESH_ID = pl.DeviceIdType.MESH
_ANY = pl.BlockSpec(memory_space=pl.ANY)
_SMEM = pl.BlockSpec(memory_space=pltpu.SMEM)


def _params(sem):
    return pltpu.CompilerParams(dimension_semantics=sem, vmem_limit_bytes=VMEM_LIMIT_BYTES)


_DIMS = {"nn": (((1,), (0,)), ((), ())), "nt": (((1,), (1,)), ((), ())), "tn": (((0,), (0,)), ((), ()))}


def _dot(a, b, mode="nn"):
    return lax.dot_general(a.astype(_MXU_DTYPE), b.astype(_MXU_DTYPE), _DIMS[mode], preferred_element_type=F32)


def _mm(a, b, mode, tm, tn, tk, out_dtype=F32, name="mm", b_off=0, n=None):
    if mode == "tn":
        kdim, m = a.shape
    else:
        m, kdim = a.shape
    if n is None:
        n = b.shape[0] if mode == "nt" else b.shape[1]
    tm, tn, tk = min(tm, m), min(tn, n), min(tk, kdim)
    assert m % tm == 0 and n % tn == 0 and kdim % tk == 0, (name, a.shape, b.shape, tm, tn, tk)
    nk = kdim // tk
    a_spec = pl.BlockSpec((tk, tm), lambda i, j, k: (k, i)) if mode == "tn" else pl.BlockSpec((tm, tk), lambda i, j, k: (i, k))
    if mode == "nt":
        b_spec = pl.BlockSpec((tn, tk), lambda i, j, k: (j + b_off, k))
    else:
        b_spec = pl.BlockSpec((tk, tn), lambda i, j, k: (k, j + b_off))

    def body(a_ref, b_ref, o_ref, *acc):
        part = _dot(a_ref[...], b_ref[...], mode)
        if nk == 1:
            o_ref[...] = part.astype(out_dtype)
        else:
            (acc_ref,) = acc
            k = pl.program_id(2)

            @pl.when(k == 0)
            def _():
                acc_ref[...] = part

            @pl.when(k > 0)
            def _():
                acc_ref[...] += part

            @pl.when(k == nk - 1)
            def _():
                o_ref[...] = acc_ref[...].astype(out_dtype)

    return pl.pallas_call(
        body, name=name, grid=(m // tm, n // tn, nk), in_specs=[a_spec, b_spec],
        out_specs=pl.BlockSpec((tm, tn), lambda i, j, k: (i, j)), out_shape=jax.ShapeDtypeStruct((m, n), out_dtype),
        scratch_shapes=[] if nk == 1 else [pltpu.VMEM((tm, tn), F32)],
        compiler_params=_params(("parallel", "parallel", "arbitrary")),
    )(a, b)


SUBLANES = 8


def _row_tile(rows, target):
    best = None
    for cand in range(SUBLANES, min(rows, target) + 1, SUBLANES):
        if rows % cand == 0:
            best = cand
    return best or rows


def _ew(fn, ins, outs, tt, name):
    t = [a.shape[0] for a, tiled in ins if tiled][0]
    tt = _row_tile(t, tt)
    n_in = len(ins)
    in_specs = [pl.BlockSpec((tt, a.shape[1]), lambda i: (i, 0)) if tiled else pl.BlockSpec(a.shape, lambda i: (0, 0))
                for a, tiled in ins]
    out_specs, out_shapes = [], []
    for w, dt, kind in outs:
        if kind == "tile":
            out_specs.append(pl.BlockSpec((tt, w), lambda i: (i, 0)))
            out_shapes.append(jax.ShapeDtypeStruct((t, w), dt))
        else:
            out_specs.append(pl.BlockSpec((1, w), lambda i: (0, 0)))
            out_shapes.append(jax.ShapeDtypeStruct((1, w), F32))
    has_acc = any(kind == "acc" for _, _, kind in outs)

    def body(*refs):
        vals = fn(*[r[...] for r in refs[:n_in]])
        if not isinstance(vals, (tuple, list)):
            vals = (vals,)
        i = pl.program_id(0)
        for o_ref, v, (w, dt, kind) in zip(refs[n_in:], vals, outs):
            if kind == "tile":
                o_ref[...] = v.astype(dt)
            else:
                @pl.when(i == 0)
                def _(o_ref=o_ref, v=v):
                    o_ref[...] = v.astype(F32)

                @pl.when(i > 0)
                def _(o_ref=o_ref, v=v):
                    o_ref[...] += v.astype(F32)

    res = pl.pallas_call(
        body, name=name, grid=(t // tt,), in_specs=in_specs, out_specs=out_specs, out_shape=out_shapes,
        compiler_params=_params(("arbitrary" if has_acc else "parallel",)),
    )(*[a for a, _ in ins])
    return res


def _whole(fn, ins, outs, name):
    def body(*refs):
        vals = fn(*[r[...] for r in refs[:len(ins)]])
        if not isinstance(vals, (tuple, list)):
            vals = (vals,)
        for o_ref, v in zip(refs[len(ins):], vals):
            o_ref[...] = v.astype(o_ref.dtype)

    return pl.pallas_call(body, name=name, out_shape=[jax.ShapeDtypeStruct(s, dt) for s, dt in outs])(*ins)


def _rms(x):
    return x * lax.rsqrt(jnp.mean(x * x, axis=-1, keepdims=True) + EPS)


def _norm_mod_fn(h, g, sc, sh):
    return (_rms(h) * g) * (1.0 + sc) + sh


def _rms_gain_fn(x, g):
    return _rms(x) * g


def _gate_add_fn(h, y, gt):
    return h + gt * y


def _relu2_fn(x):
    return jnp.square(jnp.maximum(x, 0.0))


def _s5_act_fn(ypre, u, dskip):
    return jax.nn.gelu(ypre + dskip * u)


def _s5_glu_fn(z, zz, b):
    return z * jax.nn.sigmoid(zz + b)


def _ret_gate_fn(o, g):
    return _rms(o) * (g * jax.nn.sigmoid(g))


def _final_fn(h, g, tgt):
    err = _rms(h) * g - tgt
    return 0.5 * jnp.sum(jnp.mean(err * err, axis=-1, keepdims=True), axis=0, keepdims=True)


def _vjp_block(fn, n_args):
    def bwd(*vals):
        _, pull = jax.vjp(fn, *vals[:n_args])
        return pull(vals[n_args])
    return bwd


def _rope_tables(t):
    d = RET_QK
    inv = ROPE_BASE ** (-jnp.arange(0, d, 2, dtype=F32) / d)
    ang = jnp.arange(t, dtype=F32)[:, None] * inv[None, :]
    cos, sin = jnp.cos(ang), jnp.sin(ang)
    swap = np.zeros((d, d), np.float32)
    swap[np.arange(d), (np.arange(d) + d // 2) % d] = 1.0
    return jnp.concatenate([cos, cos], -1), jnp.concatenate([-sin, sin], -1), jnp.asarray(swap)


def _rope(x, cos2, sin2, swap, name):
    nh, t, d = x.shape
    tt = min(t, 1024)

    def body(x_ref, c_ref, s_ref, p_ref, o_ref):
        xv = x_ref[0]
        sw = lax.dot_general(xv, p_ref[...], _DIMS["nn"], precision=lax.Precision.HIGHEST, preferred_element_type=F32)
        o_ref[0] = xv * c_ref[...] + sw * s_ref[...]

    return pl.pallas_call(
        body, name=name, grid=(nh, t // tt),
        in_specs=[pl.BlockSpec((1, tt, d), lambda h, i: (h, i, 0)), pl.BlockSpec((tt, d), lambda h, i: (i, 0)),
                  pl.BlockSpec((tt, d), lambda h, i: (i, 0)), pl.BlockSpec((d, d), lambda h, i: (0, 0))],
        out_specs=pl.BlockSpec((1, tt, d), lambda h, i: (h, i, 0)), out_shape=jax.ShapeDtypeStruct(x.shape, F32),
        compiler_params=_params(("parallel", "parallel")),
    )(x, cos2, sin2, swap)


SCAN_ROWS, SCAN_LANES = 256, 512


def _cmul(ar, ai, br, bi):
    return ar * br - ai * bi, ar * bi + ai * br


def _scan_chunk(xr, xi, ar, ai, reverse):
    rows = xr.shape[0]
    row = lax.broadcasted_iota(jnp.int32, xr.shape, 0)
    pr, pi = ar, ai
    k = 1
    while k < rows:
        if reverse:
            sr, si = pltpu.roll(xr, rows - k, 0), pltpu.roll(xi, rows - k, 0)
            keep = row < rows - k
        else:
            sr, si = pltpu.roll(xr, k, 0), pltpu.roll(xi, k, 0)
            keep = row >= k
        tr, ti = _cmul(pr, pi, sr, si)
        xr = xr + jnp.where(keep, tr, 0.0)
        xi = xi + jnp.where(keep, ti, 0.0)
        pr, pi = _cmul(pr, pi, pr, pi)
        k *= 2
    return xr, xi


def _s5_scan_fwd(bu, a_r, a_i, name):
    t = bu.shape[0]
    rows = min(SCAN_ROWS, t)
    nl = S5_WIDTH // SCAN_LANES

    def body(br_ref, bi_ref, ar_ref, ai_ref, or_ref, oi_ref, cr_ref, ci_ref):
        i = pl.program_id(1)

        @pl.when(i == 0)
        def _():
            cr_ref[...] = jnp.zeros_like(cr_ref)
            ci_ref[...] = jnp.zeros_like(ci_ref)

        ar, ai = ar_ref[...], ai_ref[...]
        xr, xi = br_ref[...], bi_ref[...]
        inr, ini = _cmul(ar, ai, cr_ref[...], ci_ref[...])
        first = lax.broadcasted_iota(jnp.int32, xr.shape, 0) == 0
        xr = xr + jnp.where(first, inr, 0.0)
        xi = xi + jnp.where(first, ini, 0.0)
        xr, xi = _scan_chunk(xr, xi, ar, ai, reverse=False)
        or_ref[...] = xr
        oi_ref[...] = xi
        cr_ref[...] = xr[rows - 1:rows, :]
        ci_ref[...] = xi[rows - 1:rows, :]

    blk = lambda off: pl.BlockSpec((rows, SCAN_LANES), lambda j, i: (i, j + off))
    par = pl.BlockSpec((1, SCAN_LANES), lambda j, i: (0, j))
    st_r, st_i = pl.pallas_call(
        body, name=name, grid=(nl, t // rows), in_specs=[blk(0), blk(nl), par, par], out_specs=[blk(0), blk(0)],
        out_shape=[jax.ShapeDtypeStruct((t, S5_WIDTH), F32)] * 2,
        scratch_shapes=[pltpu.VMEM((1, SCAN_LANES), F32)] * 2, compiler_params=_params(("parallel", "arbitrary")),
    )(bu, bu, a_r, a_i)
    return st_r, st_i


def _s5_scan_bwd(dst, st_r, st_i, a_r, a_i, name):
    t = dst.shape[0]
    rows = min(SCAN_ROWS, t)
    nl = S5_WIDTH // SCAN_LANES
    nc = t // rows

    def body(dr_ref, di_ref, xr_ref, xi_ref, ar_ref, ai_ref, gr_ref, gi_ref, dar_ref, dai_ref, cr_ref, ci_ref):
        i = pl.program_id(1)

        @pl.when(i == 0)
        def _():
            cr_ref[...] = jnp.zeros_like(cr_ref)
            ci_ref[...] = jnp.zeros_like(ci_ref)
            dar_ref[...] = jnp.zeros_like(dar_ref)
            dai_ref[...] = jnp.zeros_like(dai_ref)

        ar, ai = ar_ref[...], ai_ref[...]
        cr, ci = cr_ref[...], ci_ref[...]
        gr, gi = dr_ref[...], di_ref[...]
        row = lax.broadcasted_iota(jnp.int32, gr.shape, 0)
        last = row == rows - 1
        inr, ini = _cmul(ar, -ai, cr, ci)
        gr = gr + jnp.where(last, inr, 0.0)
        gi = gi + jnp.where(last, ini, 0.0)
        gr, gi = _scan_chunk(gr, gi, ar, -ai, reverse=True)
        gr_ref[...] = gr
        gi_ref[...] = gi
        nr = jnp.where(last, cr, pltpu.roll(gr, rows - 1, 0))
        ni = jnp.where(last, ci, pltpu.roll(gi, rows - 1, 0))
        xr, xi = xr_ref[...], xi_ref[...]
        dar_ref[...] += jnp.sum(nr * xr + ni * xi, axis=0, keepdims=True)
        dai_ref[...] += jnp.sum(ni * xr - nr * xi, axis=0, keepdims=True)
        cr_ref[...] = gr[0:1, :]
        ci_ref[...] = gi[0:1, :]

    blk = lambda off: pl.BlockSpec((rows, SCAN_LANES), lambda j, i: (nc - 1 - i, j + off))
    par = pl.BlockSpec((1, SCAN_LANES), lambda j, i: (0, j))
    return pl.pallas_call(
        body, name=name, grid=(nl, nc), in_specs=[blk(0), blk(nl), blk(0), blk(0), par, par],
        out_specs=[blk(0), blk(0), par, par],
        out_shape=[jax.ShapeDtypeStruct((t, S5_WIDTH), F32)] * 2 + [jax.ShapeDtypeStruct((1, S5_WIDTH), F32)] * 2,
        scratch_shapes=[pltpu.VMEM((1, SCAN_LANES), F32)] * 2, compiler_params=_params(("parallel", "arbitrary")),
    )(dst, dst, st_r, st_i, a_r, a_i)


def _s5_prep(lam_re, lam_im, log_dt, b_re, b_im, c_re, c_im, d_skip):
    dt = jnp.exp(log_dt)[:, None]
    mag = jnp.exp(lam_re * dt)
    ar, ai = mag * jnp.cos(lam_im * dt), mag * jnp.sin(lam_im * dt)
    den = lam_re * lam_re + lam_im * lam_im
    cr = ((ar - 1.0) * lam_re + ai * lam_im) / den
    ci = (ai * lam_re - (ar - 1.0) * lam_im) / den
    bbar_r = cr[..., None] * b_re - ci[..., None] * b_im
    bbar_i = cr[..., None] * b_im + ci[..., None] * b_re
    eye = jnp.eye(S5_GROUPS, dtype=F32)

    def bdiag(m):
        g, a, b = m.shape
        return (eye[:, None, :, None] * m[:, :, None, :]).reshape(g * a, g * b)

    b_cat = jnp.concatenate([bdiag(bbar_r.transpose(0, 2, 1)), bdiag(bbar_i.transpose(0, 2, 1))], axis=1)
    c_cat = jnp.concatenate([bdiag(c_re.transpose(0, 2, 1)), -bdiag(c_im.transpose(0, 2, 1))], axis=0)
    return b_cat, c_cat, d_skip.reshape(1, GROUP_WIDTH), ar.reshape(1, S5_WIDTH), ai.reshape(1, S5_WIDTH)


def _ret_consts(lgam):
    c = RET_CHUNK
    r = lax.broadcasted_iota(jnp.int32, (c, c), 0)
    m = lax.broadcasted_iota(jnp.int32, (c, c), 1)
    rel = (r - m).astype(F32)
    decay = jnp.where(rel >= 0, jnp.exp(lgam * jnp.maximum(rel, 0.0)), 0.0)
    idx = lax.broadcasted_iota(jnp.int32, (c, 1), 0).astype(F32)
    zeta = jnp.exp(lgam * (c - 1.0 - idx))
    xi = jnp.exp(lgam * (idx + 1.0))
    return decay, zeta, xi, jnp.exp(lgam * c)


def _ret_specs(t):
    qk = lambda off: pl.BlockSpec((1, t, RET_QK), lambda h: (h + off, 0, 0))
    col = lambda off: pl.BlockSpec((t, RET_V), lambda h: (0, h + off))
    return qk, col


def _ret_fwd(qk, p_ret, lgam, name):
    t = qk.shape[1]
    nck = t // RET_CHUNK
    qk_spec, col = _ret_specs(t)

    def body(lg_ref, q_ref, k_ref, v_ref, g_ref, o_ref, y_ref):
        decay, zeta, xi, gam = _ret_consts(lg_ref[pl.program_id(0)])

        def step(n, state):
            sl = pl.ds(pl.multiple_of(n * RET_CHUNK, RET_CHUNK), RET_CHUNK)
            q, k, v = q_ref[0, sl, :], k_ref[0, sl, :] * (RET_QK ** -0.5), v_ref[sl, :]
            s = _dot(q, k, "nt") * decay
            o = _dot(s, v) + _dot(q, state) * xi
            o_ref[sl, :] = o
            y_ref[sl, :] = _ret_gate_fn(o, g_ref[sl, :]).astype(y_ref.dtype)
            return gam * state + _dot(k, zeta * v, "tn")

        lax.fori_loop(0, nck, step, jnp.zeros((RET_QK, RET_V), F32))

    return pl.pallas_call(
        body, name=name, grid=(RET_HEADS,), in_specs=[_SMEM, qk_spec(0), qk_spec(RET_HEADS), col(4), col(8)],
        out_specs=[col(0), col(0)],
        out_shape=[jax.ShapeDtypeStruct((t, GROUP_WIDTH), F32), jax.ShapeDtypeStruct((t, GROUP_WIDTH), BF16)],
        compiler_params=_params(("parallel",)),
    )(lgam, qk, qk, p_ret, p_ret)


def _ret_bwd(qk, p_ret, o_all, dy, lgam, name):
    t = qk.shape[1]
    nck = t // RET_CHUNK
    qk_spec, col = _ret_specs(t)
    gate_bwd = _vjp_block(_ret_gate_fn, 2)

    def body(lg_ref, q_ref, k_ref, v_ref, g_ref, o_ref, dy_ref, dq_ref, dk_ref, dv_ref, dg_ref, st_ref):
        decay, zeta, xi, gam = _ret_consts(lg_ref[pl.program_id(0)])
        scale = RET_QK ** -0.5

        def fstep(n, state):
            sl = pl.ds(pl.multiple_of(n * RET_CHUNK, RET_CHUNK), RET_CHUNK)
            st_ref[n] = state
            return gam * state + _dot(k_ref[0, sl, :] * scale, zeta * v_ref[sl, :], "tn")

        lax.fori_loop(0, nck, fstep, jnp.zeros((RET_QK, RET_V), F32))

        def bstep(r, grad_state):
            n = nck - 1 - r
            sl = pl.ds(pl.multiple_of(n * RET_CHUNK, RET_CHUNK), RET_CHUNK)
            q, k, v = q_ref[0, sl, :], k_ref[0, sl, :] * scale, v_ref[sl, :]
            d_o, dg = gate_bwd(o_ref[sl, :], g_ref[sl, :], dy_ref[sl, :])
            dg_ref[sl, :] = dg
            s = _dot(q, k, "nt") * decay
            ds = _dot(d_o, v, "nt") * decay
            xdo = xi * d_o
            dq_ref[0, sl, :] = _dot(ds, k) + _dot(xdo, st_ref[n], "nt")
            dk_ref[0, sl, :] = (_dot(ds, q, "tn") + _dot(zeta * v, grad_state, "nt")) * scale
            dv_ref[sl, :] = _dot(s, d_o, "tn") + zeta * _dot(k, grad_state)
            return gam * grad_state + _dot(q, xdo, "tn")

        lax.fori_loop(0, nck, bstep, jnp.zeros((RET_QK, RET_V), F32))

    hd = pl.BlockSpec((1, t, RET_QK), lambda h: (h, 0, 0))
    return pl.pallas_call(
        body, name=name, grid=(RET_HEADS,),
        in_specs=[_SMEM, qk_spec(0), qk_spec(RET_HEADS), col(4), col(8), col(0), col(0)],
        out_specs=[hd, hd, col(0), col(0)],
        out_shape=[jax.ShapeDtypeStruct((RET_HEADS, t, RET_QK), F32)] * 2 + [jax.ShapeDtypeStruct((t, GROUP_WIDTH), F32)] * 2,
        scratch_shapes=[pltpu.VMEM((nck, RET_QK, RET_V), F32)], compiler_params=_params(("parallel",)),
    )(lgam, qk, qk, p_ret, p_ret, o_all, dy)


def _swa_mask(n):
    r = lax.broadcasted_iota(jnp.int32, (WINDOW, 2 * WINDOW), 0)
    j = lax.broadcasted_iota(jnp.int32, (WINDOW, 2 * WINDOW), 1)
    dist = r + WINDOW - j
    return (dist >= 0) & (dist < WINDOW) & (n * WINDOW + j - WINDOW >= 0)


def _swa_fwd(qkv, sinks, name):
    t = qkv.shape[1]
    nb = t // WINDOW
    grp = SWA_HEADS // SWA_KV_HEADS

    def body(sink_ref, q_ref, k_ref, v_ref, o_ref, lse_ref, kp_ref, vp_ref):
        sink = sink_ref[pl.program_id(0)]
        zero = jnp.zeros((WINDOW, SWA_HD), F32)
        kp_ref[0:WINDOW, :] = zero
        vp_ref[0:WINDOW, :] = zero
        kp_ref[WINDOW:, :] = k_ref[0]
        vp_ref[WINDOW:, :] = v_ref[0]

        def step(n, carry):
            sl = pl.ds(pl.multiple_of(n * WINDOW, WINDOW), WINDOW)
            win = pl.ds(pl.multiple_of(n * WINDOW, WINDOW), 2 * WINDOW)
            s = _dot(q_ref[0, sl, :], kp_ref[win, :], "nt") * (SWA_HD ** -0.5)
            s = jnp.where(_swa_mask(n), s, NEG)
            m = jnp.maximum(jnp.max(s, axis=-1, keepdims=True), sink)
            p = jnp.exp(s - m)
            den = jnp.sum(p, axis=-1, keepdims=True) + jnp.exp(sink - m)
            o_ref[0, sl, :] = _dot(p / den, vp_ref[win, :])
            lse_ref[0, sl, :] = m + jnp.log(den)
            return carry

        lax.fori_loop(0, nb, step, 0)

    hd = lambda f: pl.BlockSpec((1, t, SWA_HD), f)
    return pl.pallas_call(
        body, name=name, grid=(SWA_HEADS,),
        in_specs=[_SMEM, hd(lambda h: (h, 0, 0)), hd(lambda h: (SWA_HEADS + h // grp, 0, 0)),
                  hd(lambda h: (SWA_HEADS + SWA_KV_HEADS + h // grp, 0, 0))],
        out_specs=[hd(lambda h: (h, 0, 0)), pl.BlockSpec((1, t, 1), lambda h: (h, 0, 0))],
        out_shape=[jax.ShapeDtypeStruct((SWA_HEADS, t, SWA_HD), F32), jax.ShapeDtypeStruct((SWA_HEADS, t, 1), F32)],
        scratch_shapes=[pltpu.VMEM((t + WINDOW, SWA_HD), F32)] * 2, compiler_params=_params(("parallel",)),
    )(sinks, qkv, qkv, qkv)


def _swa_bwd(qkv, o, lse, d_o, sinks, name):
    t = qkv.shape[1]
    nb = t // WINDOW
    grp = SWA_HEADS // SWA_KV_HEADS

    def body(sink_ref, q_ref, k_ref, v_ref, o_ref, lse_ref, do_ref, dq_ref, dk_ref, dv_ref, dsink_ref,
             kp_ref, vp_ref, dkp_ref, dvp_ref):
        kv, g = pl.program_id(0), pl.program_id(1)
        sink = sink_ref[kv * grp + g]

        @pl.when(g == 0)
        def _():
            zero = jnp.zeros((WINDOW, SWA_HD), F32)
            kp_ref[0:WINDOW, :] = zero
            vp_ref[0:WINDOW, :] = zero
            kp_ref[WINDOW:, :] = k_ref[0]
            vp_ref[WINDOW:, :] = v_ref[0]
            dkp_ref[...] = jnp.zeros_like(dkp_ref)
            dvp_ref[...] = jnp.zeros_like(dvp_ref)

        def step(n, dsink):
            sl = pl.ds(pl.multiple_of(n * WINDOW, WINDOW), WINDOW)
            win = pl.ds(pl.multiple_of(n * WINDOW, WINDOW), 2 * WINDOW)
            q, dout, lse_n = q_ref[0, sl, :], do_ref[0, sl, :], lse_ref[0, sl, :]
            s = _dot(q, kp_ref[win, :], "nt") * (SWA_HD ** -0.5)
            s = jnp.where(_swa_mask(n), s, NEG)
            p = jnp.exp(s - lse_n)
            delta = jnp.sum(dout * o_ref[0, sl, :], axis=-1, keepdims=True)
            ds = p * (_dot(dout, vp_ref[win, :], "nt") - delta)
            dq_ref[0, sl, :] = _dot(ds, kp_ref[win, :]) * (SWA_HD ** -0.5)
            dkp_ref[win, :] += _dot(ds, q, "tn") * (SWA_HD ** -0.5)
            dvp_ref[win, :] += _dot(p, dout, "tn")
            return dsink - jnp.sum(jnp.exp(sink - lse_n) * delta, axis=0, keepdims=True)

        dsink = lax.fori_loop(0, nb, step, jnp.zeros((1, 1), F32))
        dsink_ref[0] = jnp.broadcast_to(dsink, (1, 128))

        @pl.when(g == grp - 1)
        def _():
            dk_ref[0] = dkp_ref[WINDOW:, :]
            dv_ref[0] = dvp_ref[WINDOW:, :]

    hd = lambda f: pl.BlockSpec((1, t, SWA_HD), f)
    qh = lambda kv, g: (kv * grp + g, 0, 0)
    dq, dk, dv, dsink = pl.pallas_call(
        body, name=name, grid=(SWA_KV_HEADS, grp),
        in_specs=[_SMEM, hd(qh), hd(lambda kv, g: (SWA_HEADS + kv, 0, 0)), hd(lambda kv, g: (SWA_HEADS + SWA_KV_HEADS + kv, 0, 0)),
                  hd(qh), pl.BlockSpec((1, t, 1), qh), hd(qh)],
        out_specs=[hd(qh), hd(lambda kv, g: (kv, 0, 0)), hd(lambda kv, g: (kv, 0, 0)), pl.BlockSpec((1, 1, 128), qh)],
        out_shape=[jax.ShapeDtypeStruct((SWA_HEADS, t, SWA_HD), F32), jax.ShapeDtypeStruct((SWA_KV_HEADS, t, SWA_HD), F32),
                   jax.ShapeDtypeStruct((SWA_KV_HEADS, t, SWA_HD), F32), jax.ShapeDtypeStruct((SWA_HEADS, 1, 128), F32)],
        scratch_shapes=[pltpu.VMEM((t + WINDOW, SWA_HD), F32)] * 4, compiler_params=_params(("parallel", "arbitrary")),
    )(sinks, qkv, qkv, qkv, o, lse, d_o)
    return jnp.concatenate([dq, dk, dv], axis=0), dsink[:, 0, 0]


MLA_BLOCK = 512
MLA_SCALE = (MLA_NOPE + MLA_ROPE) ** -0.5


def _mla_scores(qn, qr, kn, kr, i, j, blk):
    s = (_dot(qn, kn, "nt") + _dot(qr, kr, "nt")) * MLA_SCALE
    qpos = i * blk + lax.broadcasted_iota(jnp.int32, (blk, blk), 0)
    kpos = j * blk + lax.broadcasted_iota(jnp.int32, (blk, blk), 1)
    return jnp.where(kpos <= qpos, s, NEG)


def _mla_fwd(qn, rot, kv, name):
    t = qn.shape[1]
    blk = min(MLA_BLOCK, t)
    nb = t // blk

    def body(qn_ref, qr_ref, kn_ref, kr_ref, v_ref, o_ref, lse_ref, m_ref, l_ref, acc_ref):
        i, j = pl.program_id(1), pl.program_id(2)

        @pl.when(j == 0)
        def _():
            m_ref[...] = jnp.full_like(m_ref, NEG)
            l_ref[...] = jnp.zeros_like(l_ref)
            acc_ref[...] = jnp.zeros_like(acc_ref)

        @pl.when(j <= i)
        def _():
            s = _mla_scores(qn_ref[0], qr_ref[0], kn_ref[...], kr_ref[0], i, j, blk)
            m_new = jnp.maximum(m_ref[...], jnp.max(s, axis=-1, keepdims=True))
            alpha = jnp.exp(m_ref[...] - m_new)
            p = jnp.exp(s - m_new)
            l_ref[...] = alpha * l_ref[...] + jnp.sum(p, axis=-1, keepdims=True)
            acc_ref[...] = alpha * acc_ref[...] + _dot(p, v_ref[...])
            m_ref[...] = m_new

        @pl.when(j == nb - 1)
        def _():
            o_ref[...] = acc_ref[...] / l_ref[...]
            lse_ref[0] = m_ref[...] + jnp.log(l_ref[...])

    jc = lambda i, j: jnp.minimum(i, j)
    return pl.pallas_call(
        body, name=name, grid=(MLA_HEADS, nb, nb),
        in_specs=[pl.BlockSpec((1, blk, MLA_NOPE), lambda h, i, j: (h, i, 0)), pl.BlockSpec((1, blk, MLA_ROPE), lambda h, i, j: (h, i, 0)),
                  pl.BlockSpec((blk, MLA_NOPE), lambda h, i, j: (jc(i, j), 2 * h)),
                  pl.BlockSpec((1, blk, MLA_ROPE), lambda h, i, j: (MLA_HEADS, jc(i, j), 0)),
                  pl.BlockSpec((blk, MLA_V), lambda h, i, j: (jc(i, j), 2 * h + 1))],
        out_specs=[pl.BlockSpec((blk, MLA_V), lambda h, i, j: (i, h)), pl.BlockSpec((1, blk, 1), lambda h, i, j: (h, i, 0))],
        out_shape=[jax.ShapeDtypeStruct((t, GROUP_WIDTH), F32), jax.ShapeDtypeStruct((MLA_HEADS, t, 1), F32)],
        scratch_shapes=[pltpu.VMEM((blk, 1), F32), pltpu.VMEM((blk, 1), F32), pltpu.VMEM((blk, MLA_V), F32)],
        compiler_params=_params(("parallel", "parallel", "arbitrary")),
    )(qn, rot, kv, rot, kv)


def _mla_bwd_q(qn, rot, kv, o, lse, d_o, name):
    t = qn.shape[1]
    blk = min(MLA_BLOCK, t)
    nb = t // blk

    def body(qn_ref, qr_ref, kn_ref, kr_ref, v_ref, o_ref, lse_ref, do_ref, dqn_ref, dqr_ref, an_ref, ar_ref):
        i, j = pl.program_id(1), pl.program_id(2)

        @pl.when(j == 0)
        def _():
            an_ref[...] = jnp.zeros_like(an_ref)
            ar_ref[...] = jnp.zeros_like(ar_ref)

        @pl.when(j <= i)
        def _():
            s = _mla_scores(qn_ref[0], qr_ref[0], kn_ref[...], kr_ref[0], i, j, blk)
            p = jnp.exp(s - lse_ref[0])
            dout = do_ref[...]
            delta = jnp.sum(dout * o_ref[...], axis=-1, keepdims=True)
            ds = p * (_dot(dout, v_ref[...], "nt") - delta) * MLA_SCALE
            an_ref[...] += _dot(ds, kn_ref[...])
            ar_ref[...] += _dot(ds, kr_ref[0])

        @pl.when(j == nb - 1)
        def _():
            dqn_ref[0] = an_ref[...]
            dqr_ref[0] = ar_ref[...]

    jc = lambda i, j: jnp.minimum(i, j)
    return pl.pallas_call(
        body, name=name, grid=(MLA_HEADS, nb, nb),
        in_specs=[pl.BlockSpec((1, blk, MLA_NOPE), lambda h, i, j: (h, i, 0)), pl.BlockSpec((1, blk, MLA_ROPE), lambda h, i, j: (h, i, 0)),
                  pl.BlockSpec((blk, MLA_NOPE), lambda h, i, j: (jc(i, j), 2 * h)),
                  pl.BlockSpec((1, blk, MLA_ROPE), lambda h, i, j: (MLA_HEADS, jc(i, j), 0)),
                  pl.BlockSpec((blk, MLA_V), lambda h, i, j: (jc(i, j), 2 * h + 1)),
                  pl.BlockSpec((blk, MLA_V), lambda h, i, j: (i, h)), pl.BlockSpec((1, blk, 1), lambda h, i, j: (h, i, 0)),
                  pl.BlockSpec((blk, MLA_V), lambda h, i, j: (i, h))],
        out_specs=[pl.BlockSpec((1, blk, MLA_NOPE), lambda h, i, j: (h, i, 0)), pl.BlockSpec((1, blk, MLA_ROPE), lambda h, i, j: (h, i, 0))],
        out_shape=[jax.ShapeDtypeStruct((MLA_HEADS, t, MLA_NOPE), F32), jax.ShapeDtypeStruct((MLA_HEADS, t, MLA_ROPE), F32)],
        scratch_shapes=[pltpu.VMEM((blk, MLA_NOPE), F32), pltpu.VMEM((blk, MLA_ROPE), F32)],
        compiler_params=_params(("parallel", "parallel", "arbitrary")),
    )(qn, rot, kv, rot, kv, o, lse, d_o)


def _mla_bwd_kv(qn, rot, kv, o, lse, d_o, name):
    t = qn.shape[1]
    blk = min(MLA_BLOCK, t)
    nb = t // blk

    def body(qn_ref, qr_ref, kn_ref, kr_ref, v_ref, o_ref, lse_ref, do_ref, dkn_ref, dv_ref, dkr_ref, akn_ref, av_ref):
        j, h, i = pl.program_id(0), pl.program_id(1), pl.program_id(2)

        @pl.when((h == 0) & (i == 0))
        def _():
            dkr_ref[...] = jnp.zeros_like(dkr_ref)

        @pl.when(i == 0)
        def _():
            akn_ref[...] = jnp.zeros_like(akn_ref)
            av_ref[...] = jnp.zeros_like(av_ref)

        @pl.when(i >= j)
        def _():
            s = _mla_scores(qn_ref[0], qr_ref[0], kn_ref[...], kr_ref[0], i, j, blk)
            p = jnp.exp(s - lse_ref[0])
            dout = do_ref[...]
            delta = jnp.sum(dout * o_ref[...], axis=-1, keepdims=True)
            ds = p * (_dot(dout, v_ref[...], "nt") - delta) * MLA_SCALE
            av_ref[...] += _dot(p, dout, "tn")
            akn_ref[...] += _dot(ds, qn_ref[0], "tn")
            dkr_ref[...] += _dot(ds, qr_ref[0], "tn")

        @pl.when(i == nb - 1)
        def _():
            dkn_ref[...] = akn_ref[...]
            dv_ref[...] = av_ref[...]

    ic = lambda j, i: jnp.maximum(i, j)
    dkn, dv, dkr = pl.pallas_call(
        body, name=name, grid=(nb, MLA_HEADS, nb),
        in_specs=[pl.BlockSpec((1, blk, MLA_NOPE), lambda j, h, i: (h, ic(j, i), 0)), pl.BlockSpec((1, blk, MLA_ROPE), lambda j, h, i: (h, ic(j, i), 0)),
                  pl.BlockSpec((blk, MLA_NOPE), lambda j, h, i: (j, 2 * h)),
                  pl.BlockSpec((1, blk, MLA_ROPE), lambda j, h, i: (MLA_HEADS, j, 0)),
                  pl.BlockSpec((blk, MLA_V), lambda j, h, i: (j, 2 * h + 1)),
                  pl.BlockSpec((blk, MLA_V), lambda j, h, i: (ic(j, i), h)), pl.BlockSpec((1, blk, 1), lambda j, h, i: (h, ic(j, i), 0)),
                  pl.BlockSpec((blk, MLA_V), lambda j, h, i: (ic(j, i), h))],
        out_specs=[pl.BlockSpec((blk, MLA_NOPE), lambda j, h, i: (j, h)), pl.BlockSpec((blk, MLA_V), lambda j, h, i: (j, h)),
                   pl.BlockSpec((blk, MLA_ROPE), lambda j, h, i: (j, 0))],
        out_shape=[jax.ShapeDtypeStruct((t, MLA_HEADS * MLA_NOPE), F32), jax.ShapeDtypeStruct((t, MLA_HEADS * MLA_V), F32),
                   jax.ShapeDtypeStruct((t, MLA_ROPE), F32)],
        scratch_shapes=[pltpu.VMEM((blk, MLA_NOPE), F32), pltpu.VMEM((blk, MLA_V), F32)],
        compiler_params=_params(("parallel", "arbitrary", "arbitrary")),
    )(qn, rot, kv, rot, kv, o, lse, d_o)
    dkv = jnp.stack([dkn.reshape(t, MLA_HEADS, MLA_NOPE), dv.reshape(t, MLA_HEADS, MLA_V)], axis=2).reshape(t, 2 * MLA_HEADS * MLA_NOPE)
    return dkv, dkr


def _place():
    return lax.axis_index("x"), lax.axis_index("y"), lax.axis_index("c")


def _all_gather(arrs, name):
    n = len(arrs)

    def body(*refs):
        x_refs, o_refs = refs[:n], refs[n:2 * n]
        send_sems, recv_sems, local_sems = refs[2 * n:]
        x, y, c = _place()
        me, sibling = (x, y, c), (x, y, 1 - c)
        chips = [(1 - x, y), (x, 1 - y), (1 - x, 1 - y)]

        def slot(a, p):
            return o_refs[a].at[4 * p[0] + 2 * p[1] + p[2]]

        def copy(a, k, block, to, src=None):
            return pltpu.make_async_remote_copy(
                src_ref=slot(a, block) if src is None else src, dst_ref=slot(a, block),
                send_sem=send_sems.at[a, k], recv_sem=recv_sems.at[a, k], device_id=to, device_id_type=MESH_ID)

        mine = [pltpu.make_async_copy(x_refs[a], slot(a, me), local_sems.at[a]) for a in range(n)]
        for cp in mine:
            cp.start()
        first = []
        for a in range(n):
            first.append(copy(a, 0, me, sibling, src=x_refs[a]))
            first += [copy(a, 1 + j, me, (*chip, c), src=x_refs[a]) for j, chip in enumerate(chips)]
        for cp in first:
            cp.start()
        passed = []
        for j, chip in enumerate(chips):
            for a in range(n):
                copy(a, 1 + j, (*chip, c), me).wait_recv()
                cp = copy(a, 4 + j, (*chip, c), sibling)
                cp.start()
                passed.append(cp)
        for a in range(n):
            copy(a, 0, sibling, me).wait_recv()
            for j, chip in enumerate(chips):
                copy(a, 4 + j, (*chip, 1 - c), me).wait_recv()
        for cp in first + passed:
            cp.wait_send()
        for cp in mine:
            cp.wait()

    return pl.pallas_call(
        body, name=name, in_specs=[_ANY] * n, out_specs=[_ANY] * n,
        out_shape=[jax.ShapeDtypeStruct((N_DEV,) + a.shape, a.dtype) for a in arrs],
        scratch_shapes=[pltpu.SemaphoreType.DMA((n, 7)), pltpu.SemaphoreType.DMA((n, 7)), pltpu.SemaphoreType.DMA((n,))],
    )(*arrs)


def _scatter_core(grads, name):
    n = len(grads)

    def body(*refs):
        g_refs, own_refs, got_refs = refs[:n], refs[n:2 * n], refs[2 * n:3 * n]
        send_sems, recv_sems, local_sems = refs[3 * n:]
        x, y, c = _place()
        sends, locals_ = [], []
        for a in range(n):
            for q in range(4):
                locals_.append(pltpu.make_async_copy(g_refs[a].at[2 * q + c], own_refs[a].at[q], local_sems.at[a, q]))
                sends.append(pltpu.make_async_remote_copy(
                    src_ref=g_refs[a].at[2 * q + 1 - c], dst_ref=got_refs[a].at[q], send_sem=send_sems.at[a, q],
                    recv_sem=recv_sems.at[a, q], device_id=(x, y, 1 - c), device_id_type=MESH_ID))
        for cp in sends + locals_:
            cp.start()
        for cp in sends:
            cp.wait()
        for cp in locals_:
            cp.wait()

    half = [jax.ShapeDtypeStruct((4,) + g.shape[1:], g.dtype) for g in grads]
    res = pl.pallas_call(
        body, name=name, in_specs=[_ANY] * n, out_specs=[_ANY] * (2 * n), out_shape=half + half,
        scratch_shapes=[pltpu.SemaphoreType.DMA((n, 4)), pltpu.SemaphoreType.DMA((n, 4)), pltpu.SemaphoreType.DMA((n, 4))],
    )(*grads)
    return res[:n], res[n:]


def _scatter_chips(parts, name):
    n = len(parts)

    def body(*refs):
        p_refs, o_refs = refs[:n], refs[n:2 * n]
        send_sems, recv_sems, local_sems = refs[2 * n:]
        x, y, c = _place()
        mine = 2 * x + y
        chips = [(1 - x, y), (x, 1 - y), (1 - x, 1 - y)]
        sends, recvs, locals_ = [], [], []
        for a in range(n):
            locals_.append(pltpu.make_async_copy(p_refs[a].at[mine], o_refs[a].at[mine], local_sems.at[a]))
            for j, (px, py) in enumerate(chips):
                theirs = 2 * px + py
                sends.append(pltpu.make_async_remote_copy(
                    src_ref=p_refs[a].at[theirs], dst_ref=o_refs[a].at[mine], send_sem=send_sems.at[a, j],
                    recv_sem=recv_sems.at[a, j], device_id=(px, py, c), device_id_type=MESH_ID))
                recvs.append(pltpu.make_async_remote_copy(
                    src_ref=p_refs[a].at[theirs], dst_ref=o_refs[a].at[theirs], send_sem=send_sems.at[a, j],
                    recv_sem=recv_sems.at[a, j], device_id=(px, py, c), device_id_type=MESH_ID))
        for cp in sends + locals_:
            cp.start()
        for cp in sends:
            cp.wait_send()
        for cp in recvs:
            cp.wait_recv()
        for cp in locals_:
            cp.wait()

    return pl.pallas_call(
        body, name=name, in_specs=[_ANY] * n, out_specs=[_ANY] * n,
        out_shape=[jax.ShapeDtypeStruct(p.shape, p.dtype) for p in parts],
        scratch_shapes=[pltpu.SemaphoreType.DMA((n, 3)), pltpu.SemaphoreType.DMA((n, 3)), pltpu.SemaphoreType.DMA((n,))],
    )(*parts)


def _adamw_fn(w, g, m, v):
    m = ADAM_B1 * m + (1.0 - ADAM_B1) * g
    v = ADAM_B2 * v + (1.0 - ADAM_B2) * jnp.square(g)
    m_hat = m / (1.0 - ADAM_B1 ** ADAM_STEP)
    v_hat = v / (1.0 - ADAM_B2 ** ADAM_STEP)
    delta = -ADAM_LR * (m_hat / (jnp.sqrt(v_hat) + ADAM_EPS) + ADAM_WD * w)
    return delta, m, v


def _adamw_sum4_fn(w, m, v, g0, g1, g2, g3):
    g = ((g0 + g1) + g2) + g3
    return (g,) + _adamw_fn(w, g, m, v)


def _as2d(a):
    return a.reshape(-1, a.shape[-1])


def _adamw_big(w, m, v, parts, name):
    shape = w.shape
    w2, m2, v2 = _as2d(w), _as2d(m), _as2d(v)
    g4 = [_as2d(parts[:, q]) for q in range(4)]
    cols = w2.shape[1]
    outs = _ew(_adamw_sum4_fn, [(a, True) for a in (w2, m2, v2, *g4)], [(cols, F32, "tile")] * 4, 256, name)
    return [o.reshape(shape) for o in outs]


def _adamw_shard(w, g, m, v, name):
    shape = w.shape
    ins = [_as2d(a) for a in (w, g, m, v)]
    cols = ins[0].shape[1]
    outs = _ew(_adamw_fn, [(a, True) for a in ins], [(cols, F32, "tile")] * 3, 256, name)
    return [o.reshape(shape) for o in outs]


def _adamw_small(ws, gs, ms, vs, name):
    shapes = [w.shape for w in ws]
    flat = lambda a: a.reshape(-1, 128) if a.size % 128 == 0 else a.reshape(1, -1)
    ins = [flat(a) for grp in zip(ws, gs, ms, vs) for a in grp]
    k = len(ws)

    def fn(*vals):
        out = []
        for i in range(k):
            out += list(_adamw_fn(*vals[4 * i:4 * i + 4]))
        return out

    outs = _whole(fn, ins, [(ins[4 * (i // 3)].shape, F32) for i in range(3 * k)], name)
    deltas = [outs[3 * i].reshape(shapes[i]) for i in range(k)]
    new_m = [outs[3 * i + 1].reshape(shapes[i]) for i in range(k)]
    new_v = [outs[3 * i + 2].reshape(shapes[i]) for i in range(k)]
    return deltas, new_m, new_v


def _sum8(stacked, name):
    def fn(a):
        s = a[0:1]
        for i in range(1, N_DEV):
            s = s + a[i:i + 1]
        return s
    w = stacked.shape[1]
    tw = 8192
    if w % tw:
        return _whole(fn, [stacked], [((1, w), F32)], name)[0]

    def body(a_ref, o_ref):
        o_ref[...] = fn(a_ref[...])

    return pl.pallas_call(body, name=name, grid=(w // tw,), in_specs=[pl.BlockSpec((N_DEV, tw), lambda i: (0, i))],
                          out_specs=pl.BlockSpec((1, tw), lambda i: (0, i)), out_shape=jax.ShapeDtypeStruct((1, w), F32))(stacked)


ROWS = 512


def _split_heads(p, nh):
    t = p.shape[0]
    return p.reshape(t, nh, p.shape[1] // nh).transpose(1, 0, 2)


def _merge_heads(p):
    nh, t, d = p.shape
    return p.transpose(1, 0, 2).reshape(t, nh * d)


def _layer_fwd(h, mod, w, small, rope, l):
    sh1, sc1, gt1, sh2, sc2, gt2 = mod
    cos2, sin2, swap = rope
    t = h.shape[0]
    nm = lambda s: f"l{l}_{s}"
    a1 = _ew(_norm_mod_fn, [(h, True), (small["norm1_g"], False), (sc1, False), (sh1, False)], [(D_MODEL, BF16, "tile")], ROWS, nm("norm1"))[0]
    p_s5 = _mm(a1, w["w_in_t"], "nt", 512, 512, 2048, name=nm("proj_s5"), n=512)
    p_ret = _mm(a1, w["w_in_t"], "nt", 512, 512, 2048, name=nm("proj_ret"), b_off=1, n=1536)
    p_swa = _mm(a1, w["w_in_t"], "nt", 512, 256, 2048, name=nm("proj_swa"), b_off=8, n=768)
    p_mla = _mm(a1, w["w_in_t"][2816:], "nt", 512, 576, 2048, name=nm("proj_mla"))
    b_cat, c_cat, dskip, a_r, a_i = small["s5"]
    bu = _mm(p_s5, b_cat, "nn", 512, 1024, 512, name=nm("s5_bu"))
    st_r, st_i = _s5_scan_fwd(bu, a_r, a_i, nm("s5_scan"))
    st = jnp.concatenate([st_r, st_i], axis=1)
    ypre = _mm(st, c_cat, "nn", 512, 512, 1024, name=nm("s5_y"))
    z = _ew(_s5_act_fn, [(ypre, True), (p_s5, True), (dskip, False)], [(GROUP_WIDTH, F32, "tile")], ROWS, nm("s5_act"))[0]
    zz = _mm(z, w["glu_w"], "nn", 512, 512, 512, name=nm("s5_zz"))
    y_s5 = _ew(_s5_glu_fn, [(z, True), (zz, True), (small["s5_glu_b"], False)], [(GROUP_WIDTH, BF16, "tile")], ROWS, nm("s5_glu"))[0]
    qk_ret = _rope(_split_heads(p_ret[:, :512], 8), cos2, sin2, swap, nm("ret_rope"))
    o_ret, y_ret = _ret_fwd(qk_ret, p_ret, small["ret_lgam"], nm("ret"))
    qkv_swa = _split_heads(p_swa, 12)
    o_swa, lse_swa = _swa_fwd(qkv_swa, small["swa_sinks"], nm("swa"))
    y_swa = _merge_heads(o_swa).astype(BF16)
    cq, ckv, kr = p_mla[:, :MLA_Q_RANK], p_mla[:, MLA_Q_RANK:MLA_Q_RANK + MLA_KV_RANK], p_mla[:, MLA_Q_RANK + MLA_KV_RANK:]
    cqn = _ew(_rms_gain_fn, [(cq, True), (small["mla_q_norm"], False)], [(MLA_Q_RANK, BF16, "tile")], ROWS, nm("mla_qnorm"))[0]
    ckvn = _ew(_rms_gain_fn, [(ckv, True), (small["mla_kv_norm"], False)], [(MLA_KV_RANK, BF16, "tile")], ROWS, nm("mla_kvnorm"))[0]
    q_full = _mm(cqn, w["w_uq_t"], "nt", 512, 768, 384, name=nm("mla_q"))
    kv_full = _mm(ckvn, w["w_ukv_t"], "nt", 512, 1024, 128, name=nm("mla_kv"))
    q4 = q_full.reshape(t, MLA_HEADS, MLA_NOPE + MLA_ROPE)
    qn = q4[:, :, :MLA_NOPE].transpose(1, 0, 2)
    rot_in = jnp.concatenate([q4[:, :, MLA_NOPE:].transpose(1, 0, 2), kr[None]], axis=0)
    rot = _rope(rot_in, cos2, sin2, swap, nm("mla_rope"))
    o_mla, lse_mla = _mla_fwd(qn, rot, kv_full, nm("mla"))
    cat = jnp.concatenate([y_s5, y_ret, y_swa, o_mla.astype(BF16)], axis=1)
    mixed = _mm(cat, w["w_out"], "nn", 512, 1024, 2048, name=nm("out_proj"))
    h1 = _ew(_gate_add_fn, [(h, True), (mixed, True), (gt1, False)], [(D_MODEL, F32, "tile")], ROWS, nm("res1"))[0]
    a2 = _ew(_norm_mod_fn, [(h1, True), (small["norm2_g"], False), (sc2, False), (sh2, False)], [(D_MODEL, BF16, "tile")], ROWS, nm("norm2"))[0]
    hid = _mm(a2, w["w1_t"], "nt", 1024, 1024, 2048, name=nm("mlp1"))
    act = _ew(_relu2_fn, [(hid, True)], [(D_FF, BF16, "tile")], 256, nm("relu2"))[0]
    mo = _mm(act, w["w2"], "nn", 1024, 1024, 2048, name=nm("mlp2"))
    h2 = _ew(_gate_add_fn, [(h1, True), (mo, True), (gt2, False)], [(D_MODEL, F32, "tile")], ROWS, nm("res2"))[0]
    saved = dict(h=h, a1=a1, p_s5=p_s5, p_ret=p_ret, st=st, ypre=ypre, z=z, zz=zz, qk_ret=qk_ret, o_ret=o_ret,
                 qkv_swa=qkv_swa, o_swa=o_swa, lse_swa=lse_swa, cq=cq, ckv=ckv, cqn=cqn, ckvn=ckvn, qn=qn, rot=rot,
                 kv_full=kv_full, o_mla=o_mla, lse_mla=lse_mla, cat=cat, mixed=mixed, h1=h1, a2=a2, hid=hid, act=act, mo=mo)
    return h2, saved


def _layer_bwd(dh2, mod, w, small, rope, s, l):
    sh1, sc1, gt1, sh2, sc2, gt2 = mod
    cos2, sin2, swap = rope
    t = dh2.shape[0]
    nm = lambda n: f"l{l}_{n}_bwd"
    gb, gs = {}, {}
    row = (D_MODEL, F32, "acc")
    dmo, dgt2 = _ew(lambda d, y, gt: (d * gt, jnp.sum(d * y, axis=0, keepdims=True)),
                    [(dh2, True), (s["mo"], True), (gt2, False)], [(D_MODEL, BF16, "tile"), row], ROWS, nm("res2"))
    dact = _mm(dmo, w["w2"], "nt", 1024, 1024, 2048, name=nm("mlp2_x"))
    gb["w2"] = _mm(s["act"], dmo, "tn", 1024, 1024, 1024, name=nm("mlp2_w"))
    dhid = _ew(lambda x, d: d * 2.0 * jnp.maximum(x, 0.0), [(s["hid"], True), (dact, True)], [(D_FF, BF16, "tile")], 256, nm("relu2"))[0]
    da2 = _mm(dhid, w["w1_t"], "nn", 1024, 1024, 2048, name=nm("mlp1_x"))
    gb["w1_t"] = _mm(dhid, s["a2"], "tn", 1024, 1024, 1024, name=nm("mlp1_w"))

    def norm_bwd(hh, g, sc, sh, da, dres):
        dh_, dg, dsc, dsh = _vjp_block(_norm_mod_fn, 4)(hh, g, sc, sh, da)
        return dh_ + dres, dg, dsc, dsh

    dh1, gs["norm2_g"], dsc2, dsh2 = _ew(norm_bwd, [(s["h1"], True), (small["norm2_g"], False), (sc2, False), (sh2, False), (da2, True), (dh2, True)],
                                         [(D_MODEL, F32, "tile"), row, row, row], ROWS, nm("norm2"))
    dmixed, dgt1 = _ew(lambda d, y, gt: (d * gt, jnp.sum(d * y, axis=0, keepdims=True)),
                       [(dh1, True), (s["mixed"], True), (gt1, False)], [(D_MODEL, BF16, "tile"), row], ROWS, nm("res1"))
    dcat = _mm(dmixed, w["w_out"], "nt", 512, 1024, 2048, name=nm("out_proj_x"))
    gb["w_out"] = _mm(s["cat"], dmixed, "tn", 1024, 1024, 1024, name=nm("out_proj_w"))
    dy_s5, dy_ret, dy_swa, dy_mla = (dcat[:, i * GROUP_WIDTH:(i + 1) * GROUP_WIDTH] for i in range(4))
    b_cat, c_cat, dskip, a_r, a_i = small["s5"]
    gw = (GROUP_WIDTH, F32, "tile")
    gacc = (GROUP_WIDTH, F32, "acc")
    dz_a, dzz, gs["s5_glu_b"] = _ew(_vjp_block(_s5_glu_fn, 3), [(s["z"], True), (s["zz"], True), (small["s5_glu_b"], False), (dy_s5, True)],
                                    [gw, gw, gacc], ROWS, nm("s5_glu"))
    dz_b = _mm(dzz, w["glu_w"], "nt", 512, 512, 512, name=nm("s5_zz_x"))
    gb["glu_w"] = _mm(s["z"], dzz, "tn", 512, 512, 1024, name=nm("s5_zz_w"))

    def act_bwd(ypre, u, dsk, dza, dzb):
        return _vjp_block(_s5_act_fn, 3)(ypre, u, dsk, dza + dzb)

    dypre, du_a, g_dskip = _ew(act_bwd, [(s["ypre"], True), (s["p_s5"], True), (dskip, False), (dz_a, True), (dz_b, True)],
                               [gw, gw, gacc], ROWS, nm("s5_act"))
    dst = _mm(dypre, c_cat, "nt", 512, 1024, 512, name=nm("s5_y_x"))
    g_ccat = _mm(s["st"], dypre, "tn", 1024, 512, 1024, name=nm("s5_y_w"))
    dbu_r, dbu_i, g_ar, g_ai = _s5_scan_bwd(dst, s["st"][:, :S5_WIDTH], s["st"][:, S5_WIDTH:], a_r, a_i, nm("s5_scan"))
    dbu = jnp.concatenate([dbu_r, dbu_i], axis=1)
    du_b = _mm(dbu, b_cat, "nt", 512, 512, 1024, name=nm("s5_bu_x"))
    g_bcat = _mm(s["p_s5"], dbu, "tn", 512, 1024, 1024, name=nm("s5_bu_w"))
    gs["s5"] = (g_bcat, g_ccat, g_dskip, g_ar, g_ai)
    dqk_rot, dk_rot, dv_ret, dg_ret = _ret_bwd(s["qk_ret"], s["p_ret"], s["o_ret"], dy_ret, small["ret_lgam"], nm("ret"))
    dqk = _rope(jnp.concatenate([dqk_rot, dk_rot], axis=0), cos2, -sin2, swap, nm("ret_rope"))
    dqkv_swa, gs["swa_sinks"] = _swa_bwd(s["qkv_swa"], s["o_swa"], s["lse_swa"], _split_heads(dy_swa, SWA_HEADS), small["swa_sinks"], nm("swa"))
    dqn, dqr = _mla_bwd_q(s["qn"], s["rot"], s["kv_full"], s["o_mla"], s["lse_mla"], dy_mla, nm("mla_q_att"))
    dkv_full, dkr_rot = _mla_bwd_kv(s["qn"], s["rot"], s["kv_full"], s["o_mla"], s["lse_mla"], dy_mla, nm("mla_kv_att"))
    drot = _rope(jnp.concatenate([dqr, dkr_rot[None]], axis=0), cos2, -sin2, swap, nm("mla_rope"))
    dq_full = jnp.concatenate([dqn.transpose(1, 0, 2), drot[:MLA_HEADS].transpose(1, 0, 2)], axis=2).reshape(t, MLA_HEADS * (MLA_NOPE + MLA_ROPE))
    dkr = drot[MLA_HEADS]
    dcqn = _mm(dq_full, w["w_uq_t"], "nn", 512, 384, 768, name=nm("mla_q_x"))
    gb["w_uq_t"] = _mm(dq_full, s["cqn"], "tn", 768, 384, 1024, name=nm("mla_q_w"))
    dckvn = _mm(dkv_full, w["w_ukv_t"], "nn", 512, 128, 1024, name=nm("mla_kv_x"))
    gb["w_ukv_t"] = _mm(dkv_full, s["ckvn"], "tn", 1024, 128, 1024, name=nm("mla_kv_w"))
    dcq, gs["mla_q_norm"] = _ew(_vjp_block(_rms_gain_fn, 2), [(s["cq"], True), (small["mla_q_norm"], False), (dcqn, True)],
                                [(MLA_Q_RANK, F32, "tile"), (MLA_Q_RANK, F32, "acc")], ROWS, nm("mla_qnorm"))
    dckv, gs["mla_kv_norm"] = _ew(_vjp_block(_rms_gain_fn, 2), [(s["ckv"], True), (small["mla_kv_norm"], False), (dckvn, True)],
                                  [(MLA_KV_RANK, F32, "tile"), (MLA_KV_RANK, F32, "acc")], ROWS, nm("mla_kvnorm"))
    du = _ew(lambda a, b: a + b, [(du_a, True), (du_b, True)], [(GROUP_WIDTH, BF16, "tile")], ROWS, nm("s5_du"))[0]
    bf = lambda a: a.astype(BF16)
    dproj = jnp.concatenate([du, bf(_merge_heads(dqk)), bf(dv_ret), bf(dg_ret), bf(_merge_heads(dqkv_swa)), bf(dcq), bf(dckv), bf(dkr)], axis=1)
    da1 = _mm(dproj, w["w_in_t"], "nn", 512, 1024, N_IN, name=nm("proj_x"))
    gb["w_in_t"] = _mm(dproj, s["a1"], "tn", N_IN, 512, 512, name=nm("proj_w"))
    dh, gs["norm1_g"], dsc1, dsh1 = _ew(norm_bwd, [(s["h"], True), (small["norm1_g"], False), (sc1, False), (sh1, False), (da1, True), (dh1, True)],
                                        [(D_MODEL, F32, "tile"), row, row, row], ROWS, nm("norm1"))
    dmod = jnp.concatenate([dsh1, dsc1, dgt1, dsh2, dsc2, dgt2], axis=1)
    return dh, gb, gs, dmod


BIG = ("w_in_t", "w1_t", "w_uq_t", "w_ukv_t", "w_out", "w2", "glu_w")
S5_NAMES = ("s5_lambda_re", "s5_lambda_im", "s5_log_dt", "s5_b_re", "s5_b_im", "s5_c_re", "s5_c_im", "s5_d")


def kernel(x, c, norm1_g, norm2_g, ada_w, ada_b, w_in, s5_lambda_re, s5_lambda_im, s5_log_dt, s5_b_re, s5_b_im, s5_c_re, s5_c_im, s5_d, s5_glu_w, s5_glu_b, swa_sinks, mla_q_norm, mla_kv_norm, mla_w_uq, mla_w_ukv, w_out, mlp_w1, mlp_w2, final_norm_g, loss_target, m_norm1_g, m_norm2_g, m_ada_w, m_ada_b, m_w_in, m_s5_lambda_re, m_s5_lambda_im, m_s5_log_dt, m_s5_b_re, m_s5_b_im, m_s5_c_re, m_s5_c_im, m_s5_d, m_s5_glu_w, m_s5_glu_b, m_swa_sinks, m_mla_q_norm, m_mla_kv_norm, m_mla_w_uq, m_mla_w_ukv, m_w_out, m_mlp_w1, m_mlp_w2, m_final_norm_g, v_norm1_g, v_norm2_g, v_ada_w, v_ada_b, v_w_in, v_s5_lambda_re, v_s5_lambda_im, v_s5_log_dt, v_s5_b_re, v_s5_b_im, v_s5_c_re, v_s5_c_im, v_s5_d, v_s5_glu_w, v_s5_glu_b, v_swa_sinks, v_mla_q_norm, v_mla_kv_norm, v_mla_w_uq, v_mla_w_ukv, v_w_out, v_mlp_w1, v_mlp_w2, v_final_norm_g):
    names = ["norm1_g", "norm2_g", "ada_w", "ada_b", "w_in", "s5_lambda_re", "s5_lambda_im", "s5_log_dt", "s5_b_re", "s5_b_im",
             "s5_c_re", "s5_c_im", "s5_d", "s5_glu_w", "s5_glu_b", "swa_sinks", "mla_q_norm", "mla_kv_norm", "mla_w_uq",
             "mla_w_ukv", "w_out", "mlp_w1", "mlp_w2", "final_norm_g"]
    env = locals()
    wts = {n: env[n] for n in names}
    mom = {n: env["m_" + n] for n in names}
    var = {n: env["v_" + n] for n in names}
    t = x.shape[1]
    me = 4 * lax.axis_index("x") + 2 * lax.axis_index("y") + lax.axis_index("c")
    rope = _rope_tables(t)
    ret_lgam = jnp.log1p(-(2.0 ** (-5.0 - jnp.arange(RET_HEADS, dtype=F32))))

    tr = lambda a: a.transpose(0, 2, 1)
    shard = {"w_in_t": tr(w_in), "w1_t": tr(mlp_w1), "w_uq_t": tr(mla_w_uq), "w_ukv_t": tr(mla_w_ukv),
             "w_out": w_out, "w2": mlp_w2, "glu_w": s5_glu_w}
    to_send = [shard[k][l].astype(BF16) for l in range(DEPTH) for k in BIG]
    gathered = _all_gather(to_send + [c], "gather_weights")
    c_all = gathered[-1].reshape(N_DEV, D_MODEL)
    big = [{k: gathered[l * len(BIG) + i].reshape(-1, shard[k].shape[2]) for i, k in enumerate(BIG)} for l in range(DEPTH)]

    c_act = _whole(lambda v: v * jax.nn.sigmoid(v), [c_all], [((N_DEV, D_MODEL), F32)], "cond_silu")[0]
    c_pad = jnp.concatenate([c_act, jnp.zeros((128 - N_DEV, D_MODEL), F32)], axis=0)
    cols = ada_w.shape[2]
    mod_part = [_mm(c_pad, ada_w[l], "nn", 128, cols, 512, name=f"l{l}_mod")[:N_DEV] for l in range(DEPTH)]
    mod_all = _all_gather([jnp.stack(mod_part)], "gather_mod")[0]
    mod_rows = lax.dynamic_index_in_dim(mod_all, me, axis=2, keepdims=False)
    mods = []
    for l in range(DEPTH):
        row = mod_rows[:, l].reshape(1, 6 * D_MODEL) + ada_b[l][None]
        mods.append([row[:, i * D_MODEL:(i + 1) * D_MODEL] for i in range(6)])

    smalls, s5_pulls = [], []
    for l in range(DEPTH):
        s5_ops, pull = jax.vjp(_s5_prep, *[wts[n][l] for n in S5_NAMES])
        s5_pulls.append(pull)
        smalls.append(dict(norm1_g=norm1_g[l][None], norm2_g=norm2_g[l][None], s5=s5_ops, s5_glu_b=s5_glu_b[l][None],
                           swa_sinks=swa_sinks[l], mla_q_norm=mla_q_norm[l][None], mla_kv_norm=mla_kv_norm[l][None], ret_lgam=ret_lgam))
    h = x[0]
    saved = []
    for l in range(DEPTH):
        h, s = _layer_fwd(h, mods[l], big[l], smalls[l], rope, l)
        saved.append(s)

    fg = final_norm_g[None]
    tgt = loss_target[0]
    loss_local = _ew(_final_fn, [(h, True), (fg, False), (tgt, True)], [(1, F32, "acc")], ROWS, "loss")[0]

    def final_bwd(hh, g, tg):
        dh_, dg, _ = _vjp_block(_final_fn, 3)(hh, g, tg, jnp.ones((1, 1), F32))
        return dh_, dg

    dh, g_final = _ew(final_bwd, [(h, True), (fg, False), (tgt, True)], [(D_MODEL, F32, "tile"), (D_MODEL, F32, "acc")], ROWS, "loss_bwd")
    loss = lax.psum(loss_local[0, 0], ("x", "y", "c"))

    g_big, g_small, dmods = [None] * DEPTH, [None] * DEPTH, [None] * DEPTH
    for l in reversed(range(DEPTH)):
        dh, g_big[l], g_small[l], dmods[l] = _layer_bwd(dh, mods[l], big[l], smalls[l], rope, saved[l], l)
    grad_x = dh[None]

    small_parts = []
    for l in range(DEPTH):
        gs = g_small[l]
        s5g = s5_pulls[l](gs["s5"])
        small_parts += [gs["norm1_g"], gs["norm2_g"], *s5g, gs["s5_glu_b"], gs["swa_sinks"], gs["mla_q_norm"], gs["mla_kv_norm"]]
    small_parts += [g_final, *dmods]
    sizes = [int(np.prod(p.shape)) for p in small_parts]
    flat = jnp.concatenate([p.reshape(1, -1) for p in small_parts], axis=1)
    pad = (-flat.shape[1]) % 8192
    flat = jnp.pad(flat, ((0, 0), (0, pad)))
    flat_all = _all_gather([flat], "gather_small_grads")[0].reshape(N_DEV, -1)
    summed = _sum8(flat_all, "sum_small_grads")
    pieces, off = [], 0
    for sz in sizes:
        pieces.append(summed[0, off:off + sz])
        off += sz
    small_names = ["norm1_g", "norm2_g", *S5_NAMES, "s5_glu_b", "swa_sinks", "mla_q_norm", "mla_kv_norm"]
    per_layer = len(small_names)
    grads = {}
    for i, n in enumerate(small_names):
        grads[n] = jnp.stack([pieces[l * per_layer + i].reshape(wts[n].shape[1:]) for l in range(DEPTH)])
    grads["final_norm_g"] = pieces[DEPTH * per_layer]
    grads["ada_b"] = jnp.stack([pieces[DEPTH * per_layer + 1 + l] for l in range(DEPTH)])

    mod_off = sum(sizes[:DEPTH * per_layer + 1])
    dmod_all = flat_all[:, mod_off:mod_off + DEPTH * 6 * D_MODEL].reshape(N_DEV, DEPTH, N_DEV, cols)
    dmod_mine = lax.dynamic_index_in_dim(dmod_all, me, axis=2, keepdims=False).transpose(1, 0, 2)
    dmod_pad = jnp.concatenate([dmod_mine, jnp.zeros((DEPTH, 128 - N_DEV, cols), F32)], axis=1)
    grads["ada_w"] = jnp.stack([_mm(c_pad, dmod_pad[l], "tn", 512, cols, 128, name=f"l{l}_ada_w_grad") for l in range(DEPTH)])

    g_list = [g_big[l][k].reshape(N_DEV, -1, g_big[l][k].shape[1]) for l in range(DEPTH) for k in BIG]
    own, got = _scatter_core(g_list, "scatter_core")
    halves = [_ew(lambda a, b: a + b, [(_as2d(o), True), (_as2d(g), True)], [(o.shape[-1], F32, "tile")], 256, f"core_sum_{i}")[0].reshape(o.shape)
              for i, (o, g) in enumerate(zip(own, got))]
    parts = _scatter_chips(halves, "scatter_chips")

    out_g, out_d, out_m, out_v = dict(grads), {}, {}, {}
    orig = {"w_in_t": "w_in", "w1_t": "mlp_w1", "w_uq_t": "mla_w_uq", "w_ukv_t": "mla_w_ukv", "w_out": "w_out", "w2": "mlp_w2", "glu_w": "s5_glu_w"}
    for i, k in enumerate(BIG):
        n = orig[k]
        transposed = k.endswith("_t")
        p = jnp.stack([parts[l * len(BIG) + i] for l in range(DEPTH)])
        fix = tr if transposed else (lambda a: a)
        g, d, nm_, nv_ = _adamw_big(fix(wts[n]), fix(mom[n]), fix(var[n]), p, f"adamw_{n}")
        out_g[n], out_d[n], out_m[n], out_v[n] = fix(g), fix(d), fix(nm_), fix(nv_)
    n = "ada_w"
    out_d[n], out_m[n], out_v[n] = _adamw_shard(wts[n], grads[n], mom[n], var[n], "adamw_ada_w")
    small_all = small_names + ["ada_b", "final_norm_g"]
    ds, ms, vs = _adamw_small([wts[n] for n in small_all], [grads[n] for n in small_all], [mom[n] for n in small_all],
                              [var[n] for n in small_all], "adamw_small")
    for n, d, m_, v_ in zip(small_all, ds, ms, vs):
        out_d[n], out_m[n], out_v[n] = d, m_, v_
    return (loss, grad_x, *[out_g[n] for n in names], *[out_d[n] for n in names], *[out_m[n] for n in names], *[out_v[n] for n in names])
```

```python
import functools
import math

import numpy as np
import jax
import jax.numpy as jnp
from jax import lax
from jax.experimental import pallas as pl
from jax.experimental.pallas import tpu as pltpu

F32 = jnp.float32
BF16 = jnp.bfloat16
_MXU_DTYPE = jnp.bfloat16

N_DEV = 8
D_MODEL = 2048
DEPTH = 2
GROUP_WIDTH = 512
D_FF = 8192
S5_CH, S5_GROUPS, S5_STATE = 16, 32, 64
S5_WIDTH = S5_GROUPS * S5_STATE
RET_HEADS, RET_QK, RET_V, RET_CHUNK = 4, 64, 128, 128
SWA_HD, SWA_HEADS, SWA_KV_HEADS, WINDOW = 64, 8, 2, 128
MLA_HEADS, MLA_Q_RANK, MLA_KV_RANK, MLA_NOPE, MLA_ROPE, MLA_V = 4, 384, 128, 128, 64, 128
ROPE_BASE = 10000.0
EPS = 1e-6
NEG = -1e30
N_IN = 3392
ADAM_LR, ADAM_B1, ADAM_B2, ADAM_EPS, ADAM_WD, ADAM_STEP = 0.001, 0.9, 0.999, 1e-08, 0.01, 10

VMEM_LIMIT_BYTES = 52 * 1024 * 1024
MESH_ID = pl.DeviceIdType.MESH
_ANY = pl.BlockSpec(memory_space=pl.ANY)
_SMEM = pl.BlockSpec(memory_space=pltpu.SMEM)


def _params(sem):
    return pltpu.CompilerParams(dimension_semantics=sem, vmem_limit_bytes=VMEM_LIMIT_BYTES)


_DIMS = {"nn": (((1,), (0,)), ((), ())), "nt": (((1,), (1,)), ((), ())), "tn": (((0,), (0,)), ((), ()))}


def _dot(a, b, mode="nn"):
    return lax.dot_general(a.astype(_MXU_DTYPE), b.astype(_MXU_DTYPE), _DIMS[mode], preferred_element_type=F32)


def _mm(a, b, mode, tm, tn, tk, out_dtype=F32, name="mm", b_off=0, n=None):
    if mode == "tn":
        kdim, m = a.shape
    else:
        m, kdim = a.shape
    if n is None:
        n = b.shape[0] if mode == "nt" else b.shape[1]
    tm, tn, tk = min(tm, m), min(tn, n), min(tk, kdim)
    assert m % tm == 0 and n % tn == 0 and kdim % tk == 0, (name, a.shape, b.shape, tm, tn, tk)
    nk = kdim // tk
    a_spec = pl.BlockSpec((tk, tm), lambda i, j, k: (k, i)) if mode == "tn" else pl.BlockSpec((tm, tk), lambda i, j, k: (i, k))
    if mode == "nt":
        b_spec = pl.BlockSpec((tn, tk), lambda i, j, k: (j + b_off, k))
    else:
        b_spec = pl.BlockSpec((tk, tn), lambda i, j, k: (k, j + b_off))

    def body(a_ref, b_ref, o_ref, *acc):
        part = _dot(a_ref[...], b_ref[...], mode)
        if nk == 1:
            o_ref[...] = part.astype(out_dtype)
        else:
            (acc_ref,) = acc
            k = pl.program_id(2)

            @pl.when(k == 0)
            def _():
                acc_ref[...] = part

            @pl.when(k > 0)
            def _():
                acc_ref[...] += part

            @pl.when(k == nk - 1)
            def _():
                o_ref[...] = acc_ref[...].astype(out_dtype)

    return pl.pallas_call(
        body, name=name, grid=(m // tm, n // tn, nk), in_specs=[a_spec, b_spec],
        out_specs=pl.BlockSpec((tm, tn), lambda i, j, k: (i, j)), out_shape=jax.ShapeDtypeStruct((m, n), out_dtype),
        scratch_shapes=[] if nk == 1 else [pltpu.VMEM((tm, tn), F32)],
        compiler_params=_params(("parallel", "parallel", "arbitrary")),
    )(a, b)


SUBLANES = 8


def _row_tile(rows, target, mult=SUBLANES):
    best = None
    for cand in range(mult, min(rows, target) + 1, mult):
        if rows % cand == 0:
            best = cand
    return best or rows


def _ew(fn, ins, outs, tt, name):
    t = [a.shape[0] for a, tiled in ins if tiled][0]
    tt = _row_tile(t, tt)
    n_in = len(ins)
    in_specs = [pl.BlockSpec((tt, a.shape[1]), lambda i: (i, 0)) if tiled else pl.BlockSpec(a.shape, lambda i: (0, 0))
                for a, tiled in ins]
    out_specs, out_shapes = [], []
    for w, dt, kind in outs:
        if kind == "tile":
            out_specs.append(pl.BlockSpec((tt, w), lambda i: (i, 0)))
            out_shapes.append(jax.ShapeDtypeStruct((t, w), dt))
        else:
            out_specs.append(pl.BlockSpec((1, w), lambda i: (0, 0)))
            out_shapes.append(jax.ShapeDtypeStruct((1, w), F32))
    has_acc = any(kind == "acc" for _, _, kind in outs)

    def body(*refs):
        vals = fn(*[r[...] for r in refs[:n_in]])
        if not isinstance(vals, (tuple, list)):
            vals = (vals,)
        i = pl.program_id(0)
        for o_ref, v, (w, dt, kind) in zip(refs[n_in:], vals, outs):
            if kind == "tile":
                o_ref[...] = v.astype(dt)
            else:
                @pl.when(i == 0)
                def _(o_ref=o_ref, v=v):
                    o_ref[...] = v.astype(F32)

                @pl.when(i > 0)
                def _(o_ref=o_ref, v=v):
                    o_ref[...] += v.astype(F32)

    res = pl.pallas_call(
        body, name=name, grid=(t // tt,), in_specs=in_specs, out_specs=out_specs, out_shape=out_shapes,
        compiler_params=_params(("arbitrary" if has_acc else "parallel",)),
    )(*[a for a, _ in ins])
    return res


def _whole(fn, ins, outs, name):
    def body(*refs):
        vals = fn(*[r[...] for r in refs[:len(ins)]])
        if not isinstance(vals, (tuple, list)):
            vals = (vals,)
        for o_ref, v in zip(refs[len(ins):], vals):
            o_ref[...] = v.astype(o_ref.dtype)

    return pl.pallas_call(body, name=name, out_shape=[jax.ShapeDtypeStruct(s, dt) for s, dt in outs])(*ins)


def _rms(x):
    return x * lax.rsqrt(jnp.mean(x * x, axis=-1, keepdims=True) + EPS)


def _norm_mod_fn(h, g, sc, sh):
    return (_rms(h) * g) * (1.0 + sc) + sh


def _rms_gain_fn(x, g):
    return _rms(x) * g


def _gate_add_fn(h, y, gt):
    return h + gt * y


def _relu2_fn(x):
    return jnp.square(jnp.maximum(x, 0.0))


def _s5_act_fn(ypre, u, dskip):
    return jax.nn.gelu(ypre + dskip * u)


def _s5_glu_fn(z, zz, b):
    return z * jax.nn.sigmoid(zz + b)


def _ret_gate_fn(o, g):
    return _rms(o) * (g * jax.nn.sigmoid(g))


def _final_fn(h, g, tgt):
    err = _rms(h) * g - tgt
    return 0.5 * jnp.sum(jnp.mean(err * err, axis=-1, keepdims=True), axis=0, keepdims=True)


def _vjp_block(fn, n_args):
    def bwd(*vals):
        _, pull = jax.vjp(fn, *vals[:n_args])
        return pull(vals[n_args])
    return bwd


def _rope_tables(t):
    d = RET_QK
    inv = ROPE_BASE ** (-jnp.arange(0, d, 2, dtype=F32) / d)
    ang = jnp.arange(t, dtype=F32)[:, None] * inv[None, :]
    cos, sin = jnp.cos(ang), jnp.sin(ang)
    swap = np.zeros((d, d), np.float32)
    swap[np.arange(d), (np.arange(d) + d // 2) % d] = 1.0
    return jnp.concatenate([cos, cos], -1), jnp.concatenate([-sin, sin], -1), jnp.asarray(swap)


def _rope(x, cos2, sin2, swap, name):
    nh, t, d = x.shape
    tt = min(t, 1024)

    def body(x_ref, c_ref, s_ref, p_ref, o_ref):
        xv = x_ref[0]
        sw = lax.dot_general(xv, p_ref[...], _DIMS["nn"], precision=lax.Precision.HIGHEST, preferred_element_type=F32)
        o_ref[0] = xv * c_ref[...] + sw * s_ref[...]

    return pl.pallas_call(
        body, name=name, grid=(nh, t // tt),
        in_specs=[pl.BlockSpec((1, tt, d), lambda h, i: (h, i, 0)), pl.BlockSpec((tt, d), lambda h, i: (i, 0)),
                  pl.BlockSpec((tt, d), lambda h, i: (i, 0)), pl.BlockSpec((d, d), lambda h, i: (0, 0))],
        out_specs=pl.BlockSpec((1, tt, d), lambda h, i: (h, i, 0)), out_shape=jax.ShapeDtypeStruct(x.shape, F32),
        compiler_params=_params(("parallel", "parallel")),
    )(x, cos2, sin2, swap)


SCAN_ROWS, SCAN_LANES = 256, 512


def _cmul(ar, ai, br, bi):
    return ar * br - ai * bi, ar * bi + ai * br


def _scan_chunk(xr, xi, ar, ai, reverse):
    rows = xr.shape[0]
    row = lax.broadcasted_iota(jnp.int32, xr.shape, 0)
    pr, pi = ar, ai
    k = 1
    while k < rows:
        if reverse:
            sr, si = pltpu.roll(xr, rows - k, 0), pltpu.roll(xi, rows - k, 0)
            keep = row < rows - k
        else:
            sr, si = pltpu.roll(xr, k, 0), pltpu.roll(xi, k, 0)
            keep = row >= k
        tr, ti = _cmul(pr, pi, sr, si)
        xr = xr + jnp.where(keep, tr, 0.0)
        xi = xi + jnp.where(keep, ti, 0.0)
        pr, pi = _cmul(pr, pi, pr, pi)
        k *= 2
    return xr, xi


def _s5_scan_fwd(bu, a_r, a_i, name):
    t = bu.shape[0]
    rows = min(SCAN_ROWS, t)
    nl = S5_WIDTH // SCAN_LANES

    def body(br_ref, bi_ref, ar_ref, ai_ref, or_ref, oi_ref, cr_ref, ci_ref):
        i = pl.program_id(1)

        @pl.when(i == 0)
        def _():
            cr_ref[...] = jnp.zeros_like(cr_ref)
            ci_ref[...] = jnp.zeros_like(ci_ref)

        ar, ai = ar_ref[...], ai_ref[...]
        xr, xi = br_ref[...], bi_ref[...]
        inr, ini = _cmul(ar, ai, cr_ref[...], ci_ref[...])
        first = lax.broadcasted_iota(jnp.int32, xr.shape, 0) == 0
        xr = xr + jnp.where(first, inr, 0.0)
        xi = xi + jnp.where(first, ini, 0.0)
        xr, xi = _scan_chunk(xr, xi, ar, ai, reverse=False)
        or_ref[...] = xr
        oi_ref[...] = xi
        cr_ref[...] = xr[rows - 1:rows, :]
        ci_ref[...] = xi[rows - 1:rows, :]

    blk = lambda off: pl.BlockSpec((rows, SCAN_LANES), lambda j, i: (i, j + off))
    par = pl.BlockSpec((1, SCAN_LANES), lambda j, i: (0, j))
    st_r, st_i = pl.pallas_call(
        body, name=name, grid=(nl, t // rows), in_specs=[blk(0), blk(nl), par, par], out_specs=[blk(0), blk(0)],
        out_shape=[jax.ShapeDtypeStruct((t, S5_WIDTH), F32)] * 2,
        scratch_shapes=[pltpu.VMEM((1, SCAN_LANES), F32)] * 2, compiler_params=_params(("parallel", "arbitrary")),
    )(bu, bu, a_r, a_i)
    return st_r, st_i


def _s5_scan_bwd(dst, st_r, st_i, a_r, a_i, name):
    t = dst.shape[0]
    rows = min(SCAN_ROWS, t)
    nl = S5_WIDTH // SCAN_LANES
    nc = t // rows

    def body(dr_ref, di_ref, xr_ref, xi_ref, ar_ref, ai_ref, gr_ref, gi_ref, dar_ref, dai_ref, cr_ref, ci_ref):
        i = pl.program_id(1)

        @pl.when(i == 0)
        def _():
            cr_ref[...] = jnp.zeros_like(cr_ref)
            ci_ref[...] = jnp.zeros_like(ci_ref)
            dar_ref[...] = jnp.zeros_like(dar_ref)
            dai_ref[...] = jnp.zeros_like(dai_ref)

        ar, ai = ar_ref[...], ai_ref[...]
        cr, ci = cr_ref[...], ci_ref[...]
        gr, gi = dr_ref[...], di_ref[...]
        row = lax.broadcasted_iota(jnp.int32, gr.shape, 0)
        last = row == rows - 1
        inr, ini = _cmul(ar, -ai, cr, ci)
        gr = gr + jnp.where(last, inr, 0.0)
        gi = gi + jnp.where(last, ini, 0.0)
        gr, gi = _scan_chunk(gr, gi, ar, -ai, reverse=True)
        gr_ref[...] = gr
        gi_ref[...] = gi
        nr = jnp.where(last, cr, pltpu.roll(gr, rows - 1, 0))
        ni = jnp.where(last, ci, pltpu.roll(gi, rows - 1, 0))
        xr, xi = xr_ref[...], xi_ref[...]
        dar_ref[...] += jnp.sum(nr * xr + ni * xi, axis=0, keepdims=True)
        dai_ref[...] += jnp.sum(ni * xr - nr * xi, axis=0, keepdims=True)
        cr_ref[...] = gr[0:1, :]
        ci_ref[...] = gi[0:1, :]

    blk = lambda off: pl.BlockSpec((rows, SCAN_LANES), lambda j, i: (nc - 1 - i, j + off))
    par = pl.BlockSpec((1, SCAN_LANES), lambda j, i: (0, j))
    return pl.pallas_call(
        body, name=name, grid=(nl, nc), in_specs=[blk(0), blk(nl), blk(0), blk(0), par, par],
        out_specs=[blk(0), blk(0), par, par],
        out_shape=[jax.ShapeDtypeStruct((t, S5_WIDTH), F32)] * 2 + [jax.ShapeDtypeStruct((1, S5_WIDTH), F32)] * 2,
        scratch_shapes=[pltpu.VMEM((1, SCAN_LANES), F32)] * 2, compiler_params=_params(("parallel", "arbitrary")),
    )(dst, dst, st_r, st_i, a_r, a_i)


def _s5_prep(lam_re, lam_im, log_dt, b_re, b_im, c_re, c_im, d_skip):
    dt = jnp.exp(log_dt)[:, None]
    mag = jnp.exp(lam_re * dt)
    ar, ai = mag * jnp.cos(lam_im * dt), mag * jnp.sin(lam_im * dt)
    den = lam_re * lam_re + lam_im * lam_im
    cr = ((ar - 1.0) * lam_re + ai * lam_im) / den
    ci = (ai * lam_re - (ar - 1.0) * lam_im) / den
    bbar_r = cr[..., None] * b_re - ci[..., None] * b_im
    bbar_i = cr[..., None] * b_im + ci[..., None] * b_re
    eye = jnp.eye(S5_GROUPS, dtype=F32)

    def bdiag(m):
        g, a, b = m.shape
        return (eye[:, None, :, None] * m[:, :, None, :]).reshape(g * a, g * b)

    b_cat = jnp.concatenate([bdiag(bbar_r.transpose(0, 2, 1)), bdiag(bbar_i.transpose(0, 2, 1))], axis=1)
    c_cat = jnp.concatenate([bdiag(c_re.transpose(0, 2, 1)), -bdiag(c_im.transpose(0, 2, 1))], axis=0)
    return b_cat, c_cat, d_skip.reshape(1, GROUP_WIDTH), ar.reshape(1, S5_WIDTH), ai.reshape(1, S5_WIDTH)


def _ret_consts(lgam):
    c = RET_CHUNK
    r = lax.broadcasted_iota(jnp.int32, (c, c), 0)
    m = lax.broadcasted_iota(jnp.int32, (c, c), 1)
    rel = (r - m).astype(F32)
    decay = jnp.where(rel >= 0, jnp.exp(lgam * jnp.maximum(rel, 0.0)), 0.0)
    idx = lax.broadcasted_iota(jnp.int32, (c, 1), 0).astype(F32)
    zeta = jnp.exp(lgam * (c - 1.0 - idx))
    xi = jnp.exp(lgam * (idx + 1.0))
    return decay, zeta, xi, jnp.exp(lgam * c)


def _ret_specs(t):
    qk = lambda off: pl.BlockSpec((1, t, RET_QK), lambda h: (h + off, 0, 0))
    col = lambda off: pl.BlockSpec((t, RET_V), lambda h: (0, h + off))
    return qk, col


def _ret_fwd(qk, p_ret, lgam, name):
    t = qk.shape[1]
    nck = t // RET_CHUNK
    qk_spec, col = _ret_specs(t)

    def body(lg_ref, q_ref, k_ref, v_ref, g_ref, o_ref, y_ref):
        decay, zeta, xi, gam = _ret_consts(lg_ref[pl.program_id(0)])

        def step(n, state):
            sl = pl.ds(pl.multiple_of(n * RET_CHUNK, RET_CHUNK), RET_CHUNK)
            q, k, v = q_ref[0, sl, :], k_ref[0, sl, :] * (RET_QK ** -0.5), v_ref[sl, :]
            s = _dot(q, k, "nt") * decay
            o = _dot(s, v) + _dot(q, state) * xi
            o_ref[sl, :] = o
            y_ref[sl, :] = _ret_gate_fn(o, g_ref[sl, :]).astype(y_ref.dtype)
            return gam * state + _dot(k, zeta * v, "tn")

        lax.fori_loop(0, nck, step, jnp.zeros((RET_QK, RET_V), F32))

    return pl.pallas_call(
        body, name=name, grid=(RET_HEADS,), in_specs=[_SMEM, qk_spec(0), qk_spec(RET_HEADS), col(4), col(8)],
        out_specs=[col(0), col(0)],
        out_shape=[jax.ShapeDtypeStruct((t, GROUP_WIDTH), F32), jax.ShapeDtypeStruct((t, GROUP_WIDTH), BF16)],
        compiler_params=_params(("parallel",)),
    )(lgam, qk, qk, p_ret, p_ret)


def _ret_bwd(qk, p_ret, o_all, dy, lgam, name):
    t = qk.shape[1]
    nck = t // RET_CHUNK
    qk_spec, col = _ret_specs(t)
    gate_bwd = _vjp_block(_ret_gate_fn, 2)

    def body(lg_ref, q_ref, k_ref, v_ref, g_ref, o_ref, dy_ref, dq_ref, dk_ref, dv_ref, dg_ref, st_ref):
        decay, zeta, xi, gam = _ret_consts(lg_ref[pl.program_id(0)])
        scale = RET_QK ** -0.5

        def fstep(n, state):
            sl = pl.ds(pl.multiple_of(n * RET_CHUNK, RET_CHUNK), RET_CHUNK)
            st_ref[n] = state
            return gam * state + _dot(k_ref[0, sl, :] * scale, zeta * v_ref[sl, :], "tn")

        lax.fori_loop(0, nck, fstep, jnp.zeros((RET_QK, RET_V), F32))

        def bstep(r, grad_state):
            n = nck - 1 - r
            sl = pl.ds(pl.multiple_of(n * RET_CHUNK, RET_CHUNK), RET_CHUNK)
            q, k, v = q_ref[0, sl, :], k_ref[0, sl, :] * scale, v_ref[sl, :]
            d_o, dg = gate_bwd(o_ref[sl, :], g_ref[sl, :], dy_ref[sl, :])
            dg_ref[sl, :] = dg
            s = _dot(q, k, "nt") * decay
            ds = _dot(d_o, v, "nt") * decay
            xdo = xi * d_o
            dq_ref[0, sl, :] = _dot(ds, k) + _dot(xdo, st_ref[n], "nt")
            dk_ref[0, sl, :] = (_dot(ds, q, "tn") + _dot(zeta * v, grad_state, "nt")) * scale
            dv_ref[sl, :] = _dot(s, d_o, "tn") + zeta * _dot(k, grad_state)
            return gam * grad_state + _dot(q, xdo, "tn")

        lax.fori_loop(0, nck, bstep, jnp.zeros((RET_QK, RET_V), F32))

    hd = pl.BlockSpec((1, t, RET_QK), lambda h: (h, 0, 0))
    return pl.pallas_call(
        body, name=name, grid=(RET_HEADS,),
        in_specs=[_SMEM, qk_spec(0), qk_spec(RET_HEADS), col(4), col(8), col(0), col(0)],
        out_specs=[hd, hd, col(0), col(0)],
        out_shape=[jax.ShapeDtypeStruct((RET_HEADS, t, RET_QK), F32)] * 2 + [jax.ShapeDtypeStruct((t, GROUP_WIDTH), F32)] * 2,
        scratch_shapes=[pltpu.VMEM((nck, RET_QK, RET_V), F32)], compiler_params=_params(("parallel",)),
    )(lgam, qk, qk, p_ret, p_ret, o_all, dy)


def _swa_mask(n):
    r = lax.broadcasted_iota(jnp.int32, (WINDOW, 2 * WINDOW), 0)
    j = lax.broadcasted_iota(jnp.int32, (WINDOW, 2 * WINDOW), 1)
    dist = r + WINDOW - j
    return (dist >= 0) & (dist < WINDOW) & (n * WINDOW + j - WINDOW >= 0)


def _swa_fwd(qkv, sinks, name):
    t = qkv.shape[1]
    nb = t // WINDOW
    grp = SWA_HEADS // SWA_KV_HEADS

    def body(sink_ref, q_ref, k_ref, v_ref, o_ref, lse_ref, kp_ref, vp_ref):
        sink = sink_ref[pl.program_id(0)]
        zero = jnp.zeros((WINDOW, SWA_HD), F32)
        kp_ref[0:WINDOW, :] = zero
        vp_ref[0:WINDOW, :] = zero
        kp_ref[WINDOW:, :] = k_ref[0]
        vp_ref[WINDOW:, :] = v_ref[0]

        def step(n, carry):
            sl = pl.ds(pl.multiple_of(n * WINDOW, WINDOW), WINDOW)
            win = pl.ds(pl.multiple_of(n * WINDOW, WINDOW), 2 * WINDOW)
            s = _dot(q_ref[0, sl, :], kp_ref[win, :], "nt") * (SWA_HD ** -0.5)
            s = jnp.where(_swa_mask(n), s, NEG)
            m = jnp.maximum(jnp.max(s, axis=-1, keepdims=True), sink)
            p = jnp.exp(s - m)
            den = jnp.sum(p, axis=-1, keepdims=True) + jnp.exp(sink - m)
            o_ref[0, sl, :] = _dot(p / den, vp_ref[win, :])
            lse_ref[0, sl, :] = m + jnp.log(den)
            return carry

        lax.fori_loop(0, nb, step, 0)

    hd = lambda f: pl.BlockSpec((1, t, SWA_HD), f)
    return pl.pallas_call(
        body, name=name, grid=(SWA_HEADS,),
        in_specs=[_SMEM, hd(lambda h: (h, 0, 0)), hd(lambda h: (SWA_HEADS + h // grp, 0, 0)),
                  hd(lambda h: (SWA_HEADS + SWA_KV_HEADS + h // grp, 0, 0))],
        out_specs=[hd(lambda h: (h, 0, 0)), pl.BlockSpec((1, t, 1), lambda h: (h, 0, 0))],
        out_shape=[jax.ShapeDtypeStruct((SWA_HEADS, t, SWA_HD), F32), jax.ShapeDtypeStruct((SWA_HEADS, t, 1), F32)],
        scratch_shapes=[pltpu.VMEM((t + WINDOW, SWA_HD), F32)] * 2, compiler_params=_params(("parallel",)),
    )(sinks, qkv, qkv, qkv)


def _swa_bwd(qkv, o, lse, d_o, sinks, name):
    t = qkv.shape[1]
    nb = t // WINDOW
    grp = SWA_HEADS // SWA_KV_HEADS

    def body(sink_ref, q_ref, k_ref, v_ref, o_ref, lse_ref, do_ref, dq_ref, dk_ref, dv_ref, dsink_ref,
             kp_ref, vp_ref, dkp_ref, dvp_ref):
        kv, g = pl.program_id(0), pl.program_id(1)
        sink = sink_ref[kv * grp + g]

        @pl.when(g == 0)
        def _():
            zero = jnp.zeros((WINDOW, SWA_HD), F32)
            kp_ref[0:WINDOW, :] = zero
            vp_ref[0:WINDOW, :] = zero
            kp_ref[WINDOW:, :] = k_ref[0]
            vp_ref[WINDOW:, :] = v_ref[0]
            dkp_ref[...] = jnp.zeros_like(dkp_ref)
            dvp_ref[...] = jnp.zeros_like(dvp_ref)

        def step(n, dsink):
            sl = pl.ds(pl.multiple_of(n * WINDOW, WINDOW), WINDOW)
            win = pl.ds(pl.multiple_of(n * WINDOW, WINDOW), 2 * WINDOW)
            q, dout, lse_n = q_ref[0, sl, :], do_ref[0, sl, :], lse_ref[0, sl, :]
            s = _dot(q, kp_ref[win, :], "nt") * (SWA_HD ** -0.5)
            s = jnp.where(_swa_mask(n), s, NEG)
            p = jnp.exp(s - lse_n)
            delta = jnp.sum(dout * o_ref[0, sl, :], axis=-1, keepdims=True)
            ds = p * (_dot(dout, vp_ref[win, :], "nt") - delta)
            dq_ref[0, sl, :] = _dot(ds, kp_ref[win, :]) * (SWA_HD ** -0.5)
            dkp_ref[win, :] += _dot(ds, q, "tn") * (SWA_HD ** -0.5)
            dvp_ref[win, :] += _dot(p, dout, "tn")
            return dsink - jnp.sum(jnp.exp(sink - lse_n) * delta, axis=0, keepdims=True)

        dsink = lax.fori_loop(0, nb, step, jnp.zeros((1, 1), F32))
        dsink_ref[0] = jnp.broadcast_to(dsink, (1, 128))

        @pl.when(g == grp - 1)
        def _():
            dk_ref[0] = dkp_ref[WINDOW:, :]
            dv_ref[0] = dvp_ref[WINDOW:, :]

    hd = lambda f: pl.BlockSpec((1, t, SWA_HD), f)
    qh = lambda kv, g: (kv * grp + g, 0, 0)
    dq, dk, dv, dsink = pl.pallas_call(
        body, name=name, grid=(SWA_KV_HEADS, grp),
        in_specs=[_SMEM, hd(qh), hd(lambda kv, g: (SWA_HEADS + kv, 0, 0)), hd(lambda kv, g: (SWA_HEADS + SWA_KV_HEADS + kv, 0, 0)),
                  hd(qh), pl.BlockSpec((1, t, 1), qh), hd(qh)],
        out_specs=[hd(qh), hd(lambda kv, g: (kv, 0, 0)), hd(lambda kv, g: (kv, 0, 0)), pl.BlockSpec((1, 1, 128), qh)],
        out_shape=[jax.ShapeDtypeStruct((SWA_HEADS, t, SWA_HD), F32), jax.ShapeDtypeStruct((SWA_KV_HEADS, t, SWA_HD), F32),
                   jax.ShapeDtypeStruct((SWA_KV_HEADS, t, SWA_HD), F32), jax.ShapeDtypeStruct((SWA_HEADS, 1, 128), F32)],
        scratch_shapes=[pltpu.VMEM((t + WINDOW, SWA_HD), F32)] * 4, compiler_params=_params(("parallel", "arbitrary")),
    )(sinks, qkv, qkv, qkv, o, lse, d_o)
    return jnp.concatenate([dq, dk, dv], axis=0), dsink[:, 0, 0]


MLA_BLOCK = 512
MLA_SCALE = (MLA_NOPE + MLA_ROPE) ** -0.5


def _mla_scores(qn, qr, kn, kr, i, j, blk):
    s = (_dot(qn, kn, "nt") + _dot(qr, kr, "nt")) * MLA_SCALE
    qpos = i * blk + lax.broadcasted_iota(jnp.int32, (blk, blk), 0)
    kpos = j * blk + lax.broadcasted_iota(jnp.int32, (blk, blk), 1)
    return jnp.where(kpos <= qpos, s, NEG)


def _mla_fwd(qn, rot, kv, name):
    t = qn.shape[1]
    blk = min(MLA_BLOCK, t)
    nb = t // blk

    def body(qn_ref, qr_ref, kn_ref, kr_ref, v_ref, o_ref, lse_ref, m_ref, l_ref, acc_ref):
        i, j = pl.program_id(1), pl.program_id(2)

        @pl.when(j == 0)
        def _():
            m_ref[...] = jnp.full_like(m_ref, NEG)
            l_ref[...] = jnp.zeros_like(l_ref)
            acc_ref[...] = jnp.zeros_like(acc_ref)

        @pl.when(j <= i)
        def _():
            s = _mla_scores(qn_ref[0], qr_ref[0], kn_ref[...], kr_ref[0], i, j, blk)
            m_new = jnp.maximum(m_ref[...], jnp.max(s, axis=-1, keepdims=True))
            alpha = jnp.exp(m_ref[...] - m_new)
            p = jnp.exp(s - m_new)
            l_ref[...] = alpha * l_ref[...] + jnp.sum(p, axis=-1, keepdims=True)
            acc_ref[...] = alpha * acc_ref[...] + _dot(p, v_ref[...])
            m_ref[...] = m_new

        @pl.when(j == nb - 1)
        def _():
            o_ref[...] = acc_ref[...] / l_ref[...]
            lse_ref[0] = m_ref[...] + jnp.log(l_ref[...])

    jc = lambda i, j: jnp.minimum(i, j)
    return pl.pallas_call(
        body, name=name, grid=(MLA_HEADS, nb, nb),
        in_specs=[pl.BlockSpec((1, blk, MLA_NOPE), lambda h, i, j: (h, i, 0)), pl.BlockSpec((1, blk, MLA_ROPE), lambda h, i, j: (h, i, 0)),
                  pl.BlockSpec((blk, MLA_NOPE), lambda h, i, j: (jc(i, j), 2 * h)),
                  pl.BlockSpec((1, blk, MLA_ROPE), lambda h, i, j: (MLA_HEADS, jc(i, j), 0)),
                  pl.BlockSpec((blk, MLA_V), lambda h, i, j: (jc(i, j), 2 * h + 1))],
        out_specs=[pl.BlockSpec((blk, MLA_V), lambda h, i, j: (i, h)), pl.BlockSpec((1, blk, 1), lambda h, i, j: (h, i, 0))],
        out_shape=[jax.ShapeDtypeStruct((t, GROUP_WIDTH), F32), jax.ShapeDtypeStruct((MLA_HEADS, t, 1), F32)],
        scratch_shapes=[pltpu.VMEM((blk, 1), F32), pltpu.VMEM((blk, 1), F32), pltpu.VMEM((blk, MLA_V), F32)],
        compiler_params=_params(("parallel", "parallel", "arbitrary")),
    )(qn, rot, kv, rot, kv)


def _mla_bwd_q(qn, rot, kv, o, lse, d_o, name):
    t = qn.shape[1]
    blk = min(MLA_BLOCK, t)
    nb = t // blk

    def body(qn_ref, qr_ref, kn_ref, kr_ref, v_ref, o_ref, lse_ref, do_ref, dqn_ref, dqr_ref, an_ref, ar_ref):
        i, j = pl.program_id(1), pl.program_id(2)

        @pl.when(j == 0)
        def _():
            an_ref[...] = jnp.zeros_like(an_ref)
            ar_ref[...] = jnp.zeros_like(ar_ref)

        @pl.when(j <= i)
        def _():
            s = _mla_scores(qn_ref[0], qr_ref[0], kn_ref[...], kr_ref[0], i, j, blk)
            p = jnp.exp(s - lse_ref[0])
            dout = do_ref[...]
            delta = jnp.sum(dout * o_ref[...], axis=-1, keepdims=True)
            ds = p * (_dot(dout, v_ref[...], "nt") - delta) * MLA_SCALE
            an_ref[...] += _dot(ds, kn_ref[...])
            ar_ref[...] += _dot(ds, kr_ref[0])

        @pl.when(j == nb - 1)
        def _():
            dqn_ref[0] = an_ref[...]
            dqr_ref[0] = ar_ref[...]

    jc = lambda i, j: jnp.minimum(i, j)
    return pl.pallas_call(
        body, name=name, grid=(MLA_HEADS, nb, nb),
        in_specs=[pl.BlockSpec((1, blk, MLA_NOPE), lambda h, i, j: (h, i, 0)), pl.BlockSpec((1, blk, MLA_ROPE), lambda h, i, j: (h, i, 0)),
                  pl.BlockSpec((blk, MLA_NOPE), lambda h, i, j: (jc(i, j), 2 * h)),
                  pl.BlockSpec((1, blk, MLA_ROPE), lambda h, i, j: (MLA_HEADS, jc(i, j), 0)),
                  pl.BlockSpec((blk, MLA_V), lambda h, i, j: (jc(i, j), 2 * h + 1)),
                  pl.BlockSpec((blk, MLA_V), lambda h, i, j: (i, h)), pl.BlockSpec((1, blk, 1), lambda h, i, j: (h, i, 0)),
                  pl.BlockSpec((blk, MLA_V), lambda h, i, j: (i, h))],
        out_specs=[pl.BlockSpec((1, blk, MLA_NOPE), lambda h, i, j: (h, i, 0)), pl.BlockSpec((1, blk, MLA_ROPE), lambda h, i, j: (h, i, 0))],
        out_shape=[jax.ShapeDtypeStruct((MLA_HEADS, t, MLA_NOPE), F32), jax.ShapeDtypeStruct((MLA_HEADS, t, MLA_ROPE), F32)],
        scratch_shapes=[pltpu.VMEM((blk, MLA_NOPE), F32), pltpu.VMEM((blk, MLA_ROPE), F32)],
        compiler_params=_params(("parallel", "parallel", "arbitrary")),
    )(qn, rot, kv, rot, kv, o, lse, d_o)


def _mla_bwd_kv(qn, rot, kv, o, lse, d_o, name):
    t = qn.shape[1]
    blk = min(MLA_BLOCK, t)
    nb = t // blk

    def body(qn_ref, qr_ref, kn_ref, kr_ref, v_ref, o_ref, lse_ref, do_ref, dkn_ref, dv_ref, dkr_ref, akn_ref, av_ref):
        j, h, i = pl.program_id(0), pl.program_id(1), pl.program_id(2)

        @pl.when((h == 0) & (i == 0))
        def _():
            dkr_ref[...] = jnp.zeros_like(dkr_ref)

        @pl.when(i == 0)
        def _():
            akn_ref[...] = jnp.zeros_like(akn_ref)
            av_ref[...] = jnp.zeros_like(av_ref)

        @pl.when(i >= j)
        def _():
            s = _mla_scores(qn_ref[0], qr_ref[0], kn_ref[...], kr_ref[0], i, j, blk)
            p = jnp.exp(s - lse_ref[0])
            dout = do_ref[...]
            delta = jnp.sum(dout * o_ref[...], axis=-1, keepdims=True)
            ds = p * (_dot(dout, v_ref[...], "nt") - delta) * MLA_SCALE
            av_ref[...] += _dot(p, dout, "tn")
            akn_ref[...] += _dot(ds, qn_ref[0], "tn")
            dkr_ref[...] += _dot(ds, qr_ref[0], "tn")

        @pl.when(i == nb - 1)
        def _():
            dkn_ref[...] = akn_ref[...]
            dv_ref[...] = av_ref[...]

    ic = lambda j, i: jnp.maximum(i, j)
    dkn, dv, dkr = pl.pallas_call(
        body, name=name, grid=(nb, MLA_HEADS, nb),
        in_specs=[pl.BlockSpec((1, blk, MLA_NOPE), lambda j, h, i: (h, ic(j, i), 0)), pl.BlockSpec((1, blk, MLA_ROPE), lambda j, h, i: (h, ic(j, i), 0)),
                  pl.BlockSpec((blk, MLA_NOPE), lambda j, h, i: (j, 2 * h)),
                  pl.BlockSpec((1, blk, MLA_ROPE), lambda j, h, i: (MLA_HEADS, j, 0)),
                  pl.BlockSpec((blk, MLA_V), lambda j, h, i: (j, 2 * h + 1)),
                  pl.BlockSpec((blk, MLA_V), lambda j, h, i: (ic(j, i), h)), pl.BlockSpec((1, blk, 1), lambda j, h, i: (h, ic(j, i), 0)),
                  pl.BlockSpec((blk, MLA_V), lambda j, h, i: (ic(j, i), h))],
        out_specs=[pl.BlockSpec((blk, MLA_NOPE), lambda j, h, i: (j, h)), pl.BlockSpec((blk, MLA_V), lambda j, h, i: (j, h)),
                   pl.BlockSpec((blk, MLA_ROPE), lambda j, h, i: (j, 0))],
        out_shape=[jax.ShapeDtypeStruct((t, MLA_HEADS * MLA_NOPE), F32), jax.ShapeDtypeStruct((t, MLA_HEADS * MLA_V), F32),
                   jax.ShapeDtypeStruct((t, MLA_ROPE), F32)],
        scratch_shapes=[pltpu.VMEM((blk, MLA_NOPE), F32), pltpu.VMEM((blk, MLA_V), F32)],
        compiler_params=_params(("parallel", "arbitrary", "arbitrary")),
    )(qn, rot, kv, rot, kv, o, lse, d_o)
    dkv = jnp.stack([dkn.reshape(t, MLA_HEADS, MLA_NOPE), dv.reshape(t, MLA_HEADS, MLA_V)], axis=2).reshape(t, 2 * MLA_HEADS * MLA_NOPE)
    return dkv, dkr


def _place():
    return lax.axis_index("x"), lax.axis_index("y"), lax.axis_index("c")


def _all_gather(arrs, name):
    n = len(arrs)

    def body(*refs):
        x_refs, o_refs = refs[:n], refs[n:2 * n]
        send_sems, recv_sems, local_sems = refs[2 * n:]
        x, y, c = _place()
        me, sibling = (x, y, c), (x, y, 1 - c)
        chips = [(1 - x, y), (x, 1 - y), (1 - x, 1 - y)]

        def slot(a, p):
            return o_refs[a].at[4 * p[0] + 2 * p[1] + p[2]]

        def copy(a, k, block, to, src=None):
            return pltpu.make_async_remote_copy(
                src_ref=slot(a, block) if src is None else src, dst_ref=slot(a, block),
                send_sem=send_sems.at[a, k], recv_sem=recv_sems.at[a, k], device_id=to, device_id_type=MESH_ID)

        mine = [pltpu.make_async_copy(x_refs[a], slot(a, me), local_sems.at[a]) for a in range(n)]
        for cp in mine:
            cp.start()
        first = []
        for a in range(n):
            first.append(copy(a, 0, me, sibling, src=x_refs[a]))
            first += [copy(a, 1 + j, me, (*chip, c), src=x_refs[a]) for j, chip in enumerate(chips)]
        for cp in first:
            cp.start()
        passed = []
        for j, chip in enumerate(chips):
            for a in range(n):
                copy(a, 1 + j, (*chip, c), me).wait_recv()
                cp = copy(a, 4 + j, (*chip, c), sibling)
                cp.start()
                passed.append(cp)
        for a in range(n):
            copy(a, 0, sibling, me).wait_recv()
            for j, chip in enumerate(chips):
                copy(a, 4 + j, (*chip, 1 - c), me).wait_recv()
        for cp in first + passed:
            cp.wait_send()
        for cp in mine:
            cp.wait()

    return pl.pallas_call(
        body, name=name, in_specs=[_ANY] * n, out_specs=[_ANY] * n,
        out_shape=[jax.ShapeDtypeStruct((N_DEV,) + a.shape, a.dtype) for a in arrs],
        scratch_shapes=[pltpu.SemaphoreType.DMA((n, 7)), pltpu.SemaphoreType.DMA((n, 7)), pltpu.SemaphoreType.DMA((n,))],
    )(*arrs)


def _scatter_core(grads, name):
    n = len(grads)

    def body(*refs):
        g_refs, got_refs = refs[:n], refs[n:2 * n]
        send_sems, recv_sems = refs[2 * n:]
        x, y, c = _place()
        sends = [pltpu.make_async_remote_copy(
            src_ref=g_refs[a].at[2 * q + 1 - c], dst_ref=got_refs[a].at[q], send_sem=send_sems.at[a, q],
            recv_sem=recv_sems.at[a, q], device_id=(x, y, 1 - c), device_id_type=MESH_ID) for a in range(n) for q in range(4)]
        for cp in sends:
            cp.start()
        for cp in sends:
            cp.wait()

    return pl.pallas_call(
        body, name=name, in_specs=[_ANY] * n, out_specs=[_ANY] * n,
        out_shape=[jax.ShapeDtypeStruct((4,) + g.shape[1:], g.dtype) for g in grads],
        scratch_shapes=[pltpu.SemaphoreType.DMA((n, 4)), pltpu.SemaphoreType.DMA((n, 4))],
    )(*grads)


def _scatter_chips(parts, name):
    n = len(parts)

    def body(*refs):
        p_refs, o_refs = refs[:n], refs[n:2 * n]
        send_sems, recv_sems = refs[2 * n:]
        x, y, c = _place()
        chips = [(1 - x, y), (x, 1 - y), (1 - x, 1 - y)]
        sends = [pltpu.make_async_remote_copy(
            src_ref=p_refs[a].at[2 * px + py], dst_ref=o_refs[a].at[j], send_sem=send_sems.at[a, j],
            recv_sem=recv_sems.at[a, j], device_id=(px, py, c), device_id_type=MESH_ID)
            for a in range(n) for j, (px, py) in enumerate(chips)]
        for cp in sends:
            cp.start()
        for cp in sends:
            cp.wait()

    return pl.pallas_call(
        body, name=name, in_specs=[_ANY] * n, out_specs=[_ANY] * n,
        out_shape=[jax.ShapeDtypeStruct((3,) + p.shape[1:], p.dtype) for p in parts],
        scratch_shapes=[pltpu.SemaphoreType.DMA((n, 3)), pltpu.SemaphoreType.DMA((n, 3))],
    )(*parts)


def _pick_sum(picked, rest, index, pick_of, out_dtype, name):
    nq, r, cdim = rest.shape
    one = nq == 3
    tr = _row_tile(r, 512, 16)
    tc = 512 if cdim % 512 == 0 else cdim
    grid = (1 if one else nq, r // tr, cdim // tc)

    def body(i_ref, p_ref, r_ref, o_ref):
        acc = p_ref[0].astype(F32)
        if one:
            for j in range(3):
                acc = acc + r_ref[j].astype(F32)
            o_ref[...] = acc.astype(out_dtype)
        else:
            o_ref[0] = (acc + r_ref[0].astype(F32)).astype(out_dtype)

    spec = pltpu.PrefetchScalarGridSpec(
        num_scalar_prefetch=1, grid=grid,
        in_specs=[pl.BlockSpec((1, tr, tc), lambda q, i, j, i_ref: (pick_of(q, i_ref[0]), i, j)),
                  pl.BlockSpec((3, tr, tc), lambda q, i, j, i_ref: (0, i, j)) if one else pl.BlockSpec((1, tr, tc), lambda q, i, j, i_ref: (q, i, j))],
        out_specs=pl.BlockSpec((tr, tc), lambda q, i, j, i_ref: (i, j)) if one else pl.BlockSpec((1, tr, tc), lambda q, i, j, i_ref: (q, i, j)))
    return pl.pallas_call(
        body, name=name, grid_spec=spec,
        out_shape=jax.ShapeDtypeStruct((r, cdim) if one else (nq, r, cdim), out_dtype),
        compiler_params=_params(("parallel", "parallel", "parallel")),
    )(index.astype(jnp.int32).reshape(1), picked, rest)


def _adamw_fn(w, g, m, v):
    m = ADAM_B1 * m + (1.0 - ADAM_B1) * g
    v = ADAM_B2 * v + (1.0 - ADAM_B2) * jnp.square(g)
    m_hat = m / (1.0 - ADAM_B1 ** ADAM_STEP)
    v_hat = v / (1.0 - ADAM_B2 ** ADAM_STEP)
    delta = -ADAM_LR * (m_hat / (jnp.sqrt(v_hat) + ADAM_EPS) + ADAM_WD * w)
    return delta, m, v


def _as2d(a):
    return a.reshape(-1, a.shape[-1])


def _adamw_shard(w, g, m, v, name):
    shape = w.shape
    ins = [_as2d(a) for a in (w, g, m, v)]
    cols = ins[0].shape[1]
    outs = _ew(_adamw_fn, [(a, True) for a in ins], [(cols, F32, "tile")] * 3, 256, name)
    return [o.reshape(shape) for o in outs]


def _adamw_small(ws, gs, ms, vs, name):
    shapes = [w.shape for w in ws]
    flat = lambda a: a.reshape(-1, 128) if a.size % 128 == 0 else a.reshape(1, -1)
    ins = [flat(a) for grp in zip(ws, gs, ms, vs) for a in grp]
    k = len(ws)

    def fn(*vals):
        out = []
        for i in range(k):
            out += list(_adamw_fn(*vals[4 * i:4 * i + 4]))
        return out

    outs = _whole(fn, ins, [(ins[4 * (i // 3)].shape, F32) for i in range(3 * k)], name)
    deltas = [outs[3 * i].reshape(shapes[i]) for i in range(k)]
    new_m = [outs[3 * i + 1].reshape(shapes[i]) for i in range(k)]
    new_v = [outs[3 * i + 2].reshape(shapes[i]) for i in range(k)]
    return deltas, new_m, new_v


def _sum8(stacked, name):
    def fn(a):
        s = a[0:1]
        for i in range(1, N_DEV):
            s = s + a[i:i + 1]
        return s
    w = stacked.shape[1]
    tw = 8192
    if w % tw:
        return _whole(fn, [stacked], [((1, w), F32)], name)[0]

    def body(a_ref, o_ref):
        o_ref[...] = fn(a_ref[...])

    return pl.pallas_call(body, name=name, grid=(w // tw,), in_specs=[pl.BlockSpec((N_DEV, tw), lambda i: (0, i))],
                          out_specs=pl.BlockSpec((1, tw), lambda i: (0, i)), out_shape=jax.ShapeDtypeStruct((1, w), F32))(stacked)


ROWS = 512


def _split_heads(p, nh):
    t = p.shape[0]
    return p.reshape(t, nh, p.shape[1] // nh).transpose(1, 0, 2)


def _merge_heads(p):
    nh, t, d = p.shape
    return p.transpose(1, 0, 2).reshape(t, nh * d)


def _layer_fwd(h, mod, w, small, rope, l):
    sh1, sc1, gt1, sh2, sc2, gt2 = mod
    cos2, sin2, swap = rope
    t = h.shape[0]
    nm = lambda s: f"l{l}_{s}"
    a1 = _ew(_norm_mod_fn, [(h, True), (small["norm1_g"], False), (sc1, False), (sh1, False)], [(D_MODEL, BF16, "tile")], ROWS, nm("norm1"))[0]
    p_s5 = _mm(a1, w["w_in_t"], "nt", 512, 512, 2048, name=nm("proj_s5"), n=512)
    p_ret = _mm(a1, w["w_in_t"], "nt", 512, 512, 2048, name=nm("proj_ret"), b_off=1, n=1536)
    p_swa = _mm(a1, w["w_in_t"], "nt", 512, 256, 2048, name=nm("proj_swa"), b_off=8, n=768)
    p_mla = _mm(a1, w["w_in_t"][2816:], "nt", 512, 576, 2048, name=nm("proj_mla"))
    b_cat, c_cat, dskip, a_r, a_i = small["s5"]
    bu = _mm(p_s5, b_cat, "nn", 512, 1024, 512, name=nm("s5_bu"))
    st_r, st_i = _s5_scan_fwd(bu, a_r, a_i, nm("s5_scan"))
    st = jnp.concatenate([st_r, st_i], axis=1)
    ypre = _mm(st, c_cat, "nn", 512, 512, 1024, name=nm("s5_y"))
    z = _ew(_s5_act_fn, [(ypre, True), (p_s5, True), (dskip, False)], [(GROUP_WIDTH, F32, "tile")], ROWS, nm("s5_act"))[0]
    zz = _mm(z, w["glu_w"], "nn", 512, 512, 512, name=nm("s5_zz"))
    y_s5 = _ew(_s5_glu_fn, [(z, True), (zz, True), (small["s5_glu_b"], False)], [(GROUP_WIDTH, BF16, "tile")], ROWS, nm("s5_glu"))[0]
    qk_ret = _rope(_split_heads(p_ret[:, :512], 8), cos2, sin2, swap, nm("ret_rope"))
    o_ret, y_ret = _ret_fwd(qk_ret, p_ret, small["ret_lgam"], nm("ret"))
    qkv_swa = _split_heads(p_swa, 12)
    o_swa, lse_swa = _swa_fwd(qkv_swa, small["swa_sinks"], nm("swa"))
    y_swa = _merge_heads(o_swa).astype(BF16)
    cq, ckv, kr = p_mla[:, :MLA_Q_RANK], p_mla[:, MLA_Q_RANK:MLA_Q_RANK + MLA_KV_RANK], p_mla[:, MLA_Q_RANK + MLA_KV_RANK:]
    cqn = _ew(_rms_gain_fn, [(cq, True), (small["mla_q_norm"], False)], [(MLA_Q_RANK, BF16, "tile")], ROWS, nm("mla_qnorm"))[0]
    ckvn = _ew(_rms_gain_fn, [(ckv, True), (small["mla_kv_norm"], False)], [(MLA_KV_RANK, BF16, "tile")], ROWS, nm("mla_kvnorm"))[0]
    q_full = _mm(cqn, w["w_uq_t"], "nt", 512, 768, 384, name=nm("mla_q"))
    kv_full = _mm(ckvn, w["w_ukv_t"], "nt", 512, 1024, 128, name=nm("mla_kv"))
    q4 = q_full.reshape(t, MLA_HEADS, MLA_NOPE + MLA_ROPE)
    qn = q4[:, :, :MLA_NOPE].transpose(1, 0, 2)
    rot_in = jnp.concatenate([q4[:, :, MLA_NOPE:].transpose(1, 0, 2), kr[None]], axis=0)
    rot = _rope(rot_in, cos2, sin2, swap, nm("mla_rope"))
    o_mla, lse_mla = _mla_fwd(qn, rot, kv_full, nm("mla"))
    cat = jnp.concatenate([y_s5, y_ret, y_swa, o_mla.astype(BF16)], axis=1)
    mixed = _mm(cat, w["w_out"], "nn", 512, 1024, 2048, name=nm("out_proj"))
    h1 = _ew(_gate_add_fn, [(h, True), (mixed, True), (gt1, False)], [(D_MODEL, F32, "tile")], ROWS, nm("res1"))[0]
    a2 = _ew(_norm_mod_fn, [(h1, True), (small["norm2_g"], False), (sc2, False), (sh2, False)], [(D_MODEL, BF16, "tile")], ROWS, nm("norm2"))[0]
    hid = _mm(a2, w["w1_t"], "nt", 1024, 1024, 2048, name=nm("mlp1"))
    act = _ew(_relu2_fn, [(hid, True)], [(D_FF, BF16, "tile")], 256, nm("relu2"))[0]
    mo = _mm(act, w["w2"], "nn", 1024, 1024, 2048, name=nm("mlp2"))
    h2 = _ew(_gate_add_fn, [(h1, True), (mo, True), (gt2, False)], [(D_MODEL, F32, "tile")], ROWS, nm("res2"))[0]
    saved = dict(h=h, a1=a1, p_s5=p_s5, p_ret=p_ret, st=st, ypre=ypre, z=z, zz=zz, qk_ret=qk_ret, o_ret=o_ret,
                 qkv_swa=qkv_swa, o_swa=o_swa, lse_swa=lse_swa, cq=cq, ckv=ckv, cqn=cqn, ckvn=ckvn, qn=qn, rot=rot,
                 kv_full=kv_full, o_mla=o_mla, lse_mla=lse_mla, cat=cat, mixed=mixed, h1=h1, a2=a2, hid=hid, act=act, mo=mo)
    return h2, saved


def _layer_bwd(dh2, mod, w, small, rope, s, l):
    sh1, sc1, gt1, sh2, sc2, gt2 = mod
    cos2, sin2, swap = rope
    t = dh2.shape[0]
    nm = lambda n: f"l{l}_{n}_bwd"
    gb, gs = {}, {}
    row = (D_MODEL, F32, "acc")
    dmo, dgt2 = _ew(lambda d, y, gt: (d * gt, jnp.sum(d * y, axis=0, keepdims=True)),
                    [(dh2, True), (s["mo"], True), (gt2, False)], [(D_MODEL, BF16, "tile"), row], ROWS, nm("res2"))
    dact = _mm(dmo, w["w2"], "nt", 1024, 1024, 2048, name=nm("mlp2_x"))
    gb["w2"] = _mm(s["act"], dmo, "tn", 1024, 1024, 1024, BF16, name=nm("mlp2_w"))
    dhid = _ew(lambda x, d: d * 2.0 * jnp.maximum(x, 0.0), [(s["hid"], True), (dact, True)], [(D_FF, BF16, "tile")], 256, nm("relu2"))[0]
    da2 = _mm(dhid, w["w1_t"], "nn", 1024, 1024, 2048, name=nm("mlp1_x"))
    gb["w1_t"] = _mm(dhid, s["a2"], "tn", 1024, 1024, 1024, BF16, name=nm("mlp1_w"))

    def norm_bwd(hh, g, sc, sh, da, dres):
        dh_, dg, dsc, dsh = _vjp_block(_norm_mod_fn, 4)(hh, g, sc, sh, da)
        return dh_ + dres, dg, dsc, dsh

    dh1, gs["norm2_g"], dsc2, dsh2 = _ew(norm_bwd, [(s["h1"], True), (small["norm2_g"], False), (sc2, False), (sh2, False), (da2, True), (dh2, True)],
                                         [(D_MODEL, F32, "tile"), row, row, row], ROWS, nm("norm2"))
    dmixed, dgt1 = _ew(lambda d, y, gt: (d * gt, jnp.sum(d * y, axis=0, keepdims=True)),
                       [(dh1, True), (s["mixed"], True), (gt1, False)], [(D_MODEL, BF16, "tile"), row], ROWS, nm("res1"))
    dcat = _mm(dmixed, w["w_out"], "nt", 512, 1024, 2048, name=nm("out_proj_x"))
    gb["w_out"] = _mm(s["cat"], dmixed, "tn", 1024, 1024, 1024, BF16, name=nm("out_proj_w"))
    dy_s5, dy_ret, dy_swa, dy_mla = (dcat[:, i * GROUP_WIDTH:(i + 1) * GROUP_WIDTH] for i in range(4))
    b_cat, c_cat, dskip, a_r, a_i = small["s5"]
    gw = (GROUP_WIDTH, F32, "tile")
    gacc = (GROUP_WIDTH, F32, "acc")
    dz_a, dzz, gs["s5_glu_b"] = _ew(_vjp_block(_s5_glu_fn, 3), [(s["z"], True), (s["zz"], True), (small["s5_glu_b"], False), (dy_s5, True)],
                                    [gw, gw, gacc], ROWS, nm("s5_glu"))
    dz_b = _mm(dzz, w["glu_w"], "nt", 512, 512, 512, name=nm("s5_zz_x"))
    gb["glu_w"] = _mm(s["z"], dzz, "tn", 512, 512, 1024, BF16, name=nm("s5_zz_w"))

    def act_bwd(ypre, u, dsk, dza, dzb):
        return _vjp_block(_s5_act_fn, 3)(ypre, u, dsk, dza + dzb)

    dypre, du_a, g_dskip = _ew(act_bwd, [(s["ypre"], True), (s["p_s5"], True), (dskip, False), (dz_a, True), (dz_b, True)],
                               [gw, gw, gacc], ROWS, nm("s5_act"))
    dst = _mm(dypre, c_cat, "nt", 512, 1024, 512, name=nm("s5_y_x"))
    g_ccat = _mm(s["st"], dypre, "tn", 1024, 512, 1024, name=nm("s5_y_w"))
    dbu_r, dbu_i, g_ar, g_ai = _s5_scan_bwd(dst, s["st"][:, :S5_WIDTH], s["st"][:, S5_WIDTH:], a_r, a_i, nm("s5_scan"))
    dbu = jnp.concatenate([dbu_r, dbu_i], axis=1)
    du_b = _mm(dbu, b_cat, "nt", 512, 512, 1024, name=nm("s5_bu_x"))
    g_bcat = _mm(s["p_s5"], dbu, "tn", 512, 1024, 1024, name=nm("s5_bu_w"))
    gs["s5"] = (g_bcat, g_ccat, g_dskip, g_ar, g_ai)
    dqk_rot, dk_rot, dv_ret, dg_ret = _ret_bwd(s["qk_ret"], s["p_ret"], s["o_ret"], dy_ret, small["ret_lgam"], nm("ret"))
    dqk = _rope(jnp.concatenate([dqk_rot, dk_rot], axis=0), cos2, -sin2, swap, nm("ret_rope"))
    dqkv_swa, gs["swa_sinks"] = _swa_bwd(s["qkv_swa"], s["o_swa"], s["lse_swa"], _split_heads(dy_swa, SWA_HEADS), small["swa_sinks"], nm("swa"))
    dqn, dqr = _mla_bwd_q(s["qn"], s["rot"], s["kv_full"], s["o_mla"], s["lse_mla"], dy_mla, nm("mla_q_att"))
    dkv_full, dkr_rot = _mla_bwd_kv(s["qn"], s["rot"], s["kv_full"], s["o_mla"], s["lse_mla"], dy_mla, nm("mla_kv_att"))
    drot = _rope(jnp.concatenate([dqr, dkr_rot[None]], axis=0), cos2, -sin2, swap, nm("mla_rope"))
    dq_full = jnp.concatenate([dqn.transpose(1, 0, 2), drot[:MLA_HEADS].transpose(1, 0, 2)], axis=2).reshape(t, MLA_HEADS * (MLA_NOPE + MLA_ROPE))
    dkr = drot[MLA_HEADS]
    dcqn = _mm(dq_full, w["w_uq_t"], "nn", 512, 384, 768, name=nm("mla_q_x"))
    gb["w_uq_t"] = _mm(dq_full, s["cqn"], "tn", 768, 384, 1024, BF16, name=nm("mla_q_w"))
    dckvn = _mm(dkv_full, w["w_ukv_t"], "nn", 512, 128, 1024, name=nm("mla_kv_x"))
    gb["w_ukv_t"] = _mm(dkv_full, s["ckvn"], "tn", 1024, 128, 1024, BF16, name=nm("mla_kv_w"))
    dcq, gs["mla_q_norm"] = _ew(_vjp_block(_rms_gain_fn, 2), [(s["cq"], True), (small["mla_q_norm"], False), (dcqn, True)],
                                [(MLA_Q_RANK, F32, "tile"), (MLA_Q_RANK, F32, "acc")], ROWS, nm("mla_qnorm"))
    dckv, gs["mla_kv_norm"] = _ew(_vjp_block(_rms_gain_fn, 2), [(s["ckv"], True), (small["mla_kv_norm"], False), (dckvn, True)],
                                  [(MLA_KV_RANK, F32, "tile"), (MLA_KV_RANK, F32, "acc")], ROWS, nm("mla_kvnorm"))
    du = _ew(lambda a, b: a + b, [(du_a, True), (du_b, True)], [(GROUP_WIDTH, BF16, "tile")], ROWS, nm("s5_du"))[0]
    bf = lambda a: a.astype(BF16)
    dproj = jnp.concatenate([du, bf(_merge_heads(dqk)), bf(dv_ret), bf(dg_ret), bf(_merge_heads(dqkv_swa)), bf(dcq), bf(dckv), bf(dkr)], axis=1)
    da1 = _mm(dproj, w["w_in_t"], "nn", 512, 1024, N_IN, name=nm("proj_x"))
    gb["w_in_t"] = _mm(dproj, s["a1"], "tn", N_IN, 512, 512, BF16, name=nm("proj_w"))
    dh, gs["norm1_g"], dsc1, dsh1 = _ew(norm_bwd, [(s["h"], True), (small["norm1_g"], False), (sc1, False), (sh1, False), (da1, True), (dh1, True)],
                                        [(D_MODEL, F32, "tile"), row, row, row], ROWS, nm("norm1"))
    dmod = jnp.concatenate([dsh1, dsc1, dgt1, dsh2, dsc2, dgt2], axis=1)
    return dh, gb, gs, dmod


BIG = ("w_in_t", "w1_t", "w_uq_t", "w_ukv_t", "w_out", "w2", "glu_w")
S5_NAMES = ("s5_lambda_re", "s5_lambda_im", "s5_log_dt", "s5_b_re", "s5_b_im", "s5_c_re", "s5_c_im", "s5_d")


def kernel(x, c, norm1_g, norm2_g, ada_w, ada_b, w_in, s5_lambda_re, s5_lambda_im, s5_log_dt, s5_b_re, s5_b_im, s5_c_re, s5_c_im, s5_d, s5_glu_w, s5_glu_b, swa_sinks, mla_q_norm, mla_kv_norm, mla_w_uq, mla_w_ukv, w_out, mlp_w1, mlp_w2, final_norm_g, loss_target, m_norm1_g, m_norm2_g, m_ada_w, m_ada_b, m_w_in, m_s5_lambda_re, m_s5_lambda_im, m_s5_log_dt, m_s5_b_re, m_s5_b_im, m_s5_c_re, m_s5_c_im, m_s5_d, m_s5_glu_w, m_s5_glu_b, m_swa_sinks, m_mla_q_norm, m_mla_kv_norm, m_mla_w_uq, m_mla_w_ukv, m_w_out, m_mlp_w1, m_mlp_w2, m_final_norm_g, v_norm1_g, v_norm2_g, v_ada_w, v_ada_b, v_w_in, v_s5_lambda_re, v_s5_lambda_im, v_s5_log_dt, v_s5_b_re, v_s5_b_im, v_s5_c_re, v_s5_c_im, v_s5_d, v_s5_glu_w, v_s5_glu_b, v_swa_sinks, v_mla_q_norm, v_mla_kv_norm, v_mla_w_uq, v_mla_w_ukv, v_w_out, v_mlp_w1, v_mlp_w2, v_final_norm_g):
    names = ["norm1_g", "norm2_g", "ada_w", "ada_b", "w_in", "s5_lambda_re", "s5_lambda_im", "s5_log_dt", "s5_b_re", "s5_b_im",
             "s5_c_re", "s5_c_im", "s5_d", "s5_glu_w", "s5_glu_b", "swa_sinks", "mla_q_norm", "mla_kv_norm", "mla_w_uq",
             "mla_w_ukv", "w_out", "mlp_w1", "mlp_w2", "final_norm_g"]
    env = locals()
    wts = {n: env[n] for n in names}
    mom = {n: env["m_" + n] for n in names}
    var = {n: env["v_" + n] for n in names}
    t = x.shape[1]
    me = 4 * lax.axis_index("x") + 2 * lax.axis_index("y") + lax.axis_index("c")
    rope = _rope_tables(t)
    ret_lgam = jnp.log1p(-(2.0 ** (-5.0 - jnp.arange(RET_HEADS, dtype=F32))))

    tr = lambda a: a.transpose(0, 2, 1)
    shard = {"w_in_t": tr(w_in), "w1_t": tr(mlp_w1), "w_uq_t": tr(mla_w_uq), "w_ukv_t": tr(mla_w_ukv),
             "w_out": w_out, "w2": mlp_w2, "glu_w": s5_glu_w}
    to_send = [shard[k][l].astype(BF16) for l in range(DEPTH) for k in BIG]
    gathered = _all_gather(to_send + [c], "gather_weights")
    c_all = gathered[-1].reshape(N_DEV, D_MODEL)
    big = [{k: gathered[l * len(BIG) + i].reshape(-1, shard[k].shape[2]) for i, k in enumerate(BIG)} for l in range(DEPTH)]

    c_act = _whole(lambda v: v * jax.nn.sigmoid(v), [c_all], [((N_DEV, D_MODEL), F32)], "cond_silu")[0]
    c_pad = jnp.concatenate([c_act, jnp.zeros((128 - N_DEV, D_MODEL), F32)], axis=0)
    cols = ada_w.shape[2]
    mod_part = [_mm(c_pad, ada_w[l], "nn", 128, cols, 512, name=f"l{l}_mod")[:N_DEV] for l in range(DEPTH)]
    mod_all = _all_gather([jnp.stack(mod_part)], "gather_mod")[0]
    mod_rows = lax.dynamic_index_in_dim(mod_all, me, axis=2, keepdims=False)
    mods = []
    for l in range(DEPTH):
        row = mod_rows[:, l].reshape(1, 6 * D_MODEL) + ada_b[l][None]
        mods.append([row[:, i * D_MODEL:(i + 1) * D_MODEL] for i in range(6)])

    smalls, s5_pulls = [], []
    for l in range(DEPTH):
        s5_ops, pull = jax.vjp(_s5_prep, *[wts[n][l] for n in S5_NAMES])
        s5_pulls.append(pull)
        smalls.append(dict(norm1_g=norm1_g[l][None], norm2_g=norm2_g[l][None], s5=s5_ops, s5_glu_b=s5_glu_b[l][None],
                           swa_sinks=swa_sinks[l], mla_q_norm=mla_q_norm[l][None], mla_kv_norm=mla_kv_norm[l][None], ret_lgam=ret_lgam))
    h = x[0]
    saved = []
    for l in range(DEPTH):
        h, s = _layer_fwd(h, mods[l], big[l], smalls[l], rope, l)
        saved.append(s)

    fg = final_norm_g[None]
    tgt = loss_target[0]
    loss_local = _ew(_final_fn, [(h, True), (fg, False), (tgt, True)], [(1, F32, "acc")], ROWS, "loss")[0]

    def final_bwd(hh, g, tg):
        dh_, dg, _ = _vjp_block(_final_fn, 3)(hh, g, tg, jnp.ones((1, 1), F32))
        return dh_, dg

    dh, g_final = _ew(final_bwd, [(h, True), (fg, False), (tgt, True)], [(D_MODEL, F32, "tile"), (D_MODEL, F32, "acc")], ROWS, "loss_bwd")
    loss = lax.psum(loss_local[0, 0], ("x", "y", "c"))

    g_big, g_small, dmods = [None] * DEPTH, [None] * DEPTH, [None] * DEPTH
    for l in reversed(range(DEPTH)):
        dh, g_big[l], g_small[l], dmods[l] = _layer_bwd(dh, mods[l], big[l], smalls[l], rope, saved[l], l)
    grad_x = dh[None]

    small_parts = []
    for l in range(DEPTH):
        gs = g_small[l]
        s5g = s5_pulls[l](gs["s5"])
        small_parts += [gs["norm1_g"], gs["norm2_g"], *s5g, gs["s5_glu_b"], gs["swa_sinks"], gs["mla_q_norm"], gs["mla_kv_norm"]]
    small_parts += [g_final, *dmods]
    sizes = [int(np.prod(p.shape)) for p in small_parts]
    flat = jnp.concatenate([p.reshape(1, -1) for p in small_parts], axis=1)
    pad = (-flat.shape[1]) % 8192
    flat = jnp.pad(flat, ((0, 0), (0, pad)))
    flat_all = _all_gather([flat], "gather_small_grads")[0].reshape(N_DEV, -1)
    summed = _sum8(flat_all, "sum_small_grads")
    pieces, off = [], 0
    for sz in sizes:
        pieces.append(summed[0, off:off + sz])
        off += sz
    small_names = ["norm1_g", "norm2_g", *S5_NAMES, "s5_glu_b", "swa_sinks", "mla_q_norm", "mla_kv_norm"]
    per_layer = len(small_names)
    grads = {}
    for i, n in enumerate(small_names):
        grads[n] = jnp.stack([pieces[l * per_layer + i].reshape(wts[n].shape[1:]) for l in range(DEPTH)])
    grads["final_norm_g"] = pieces[DEPTH * per_layer]
    grads["ada_b"] = jnp.stack([pieces[DEPTH * per_layer + 1 + l] for l in range(DEPTH)])

    mod_off = sum(sizes[:DEPTH * per_layer + 1])
    dmod_all = flat_all[:, mod_off:mod_off + DEPTH * 6 * D_MODEL].reshape(N_DEV, DEPTH, N_DEV, cols)
    dmod_mine = lax.dynamic_index_in_dim(dmod_all, me, axis=2, keepdims=False).transpose(1, 0, 2)
    dmod_pad = jnp.concatenate([dmod_mine, jnp.zeros((DEPTH, 128 - N_DEV, cols), F32)], axis=1)
    grads["ada_w"] = jnp.stack([_mm(c_pad, dmod_pad[l], "tn", 512, cols, 128, name=f"l{l}_ada_w_grad") for l in range(DEPTH)])

    g_list = [g_big[l][k].reshape(N_DEV, -1, g_big[l][k].shape[1]) for l in range(DEPTH) for k in BIG]
    core, chip = lax.axis_index("c"), 2 * lax.axis_index("x") + lax.axis_index("y")
    got = _scatter_core(g_list, "scatter_core")
    halves = [_pick_sum(g, o, core, lambda q, c: 2 * q + c, BF16, f"core_sum_{i}") for i, (g, o) in enumerate(zip(g_list, got))]
    landed = _scatter_chips(halves, "scatter_chips")
    g_shard = [_pick_sum(h_, o, chip, lambda q, m: m, F32, f"chip_sum_{i}") for i, (h_, o) in enumerate(zip(halves, landed))]

    out_g, out_d, out_m, out_v = dict(grads), {}, {}, {}
    orig = {"w_in_t": "w_in", "w1_t": "mlp_w1", "w_uq_t": "mla_w_uq", "w_ukv_t": "mla_w_ukv", "w_out": "w_out", "w2": "mlp_w2", "glu_w": "s5_glu_w"}
    for i, k in enumerate(BIG):
        out_g[orig[k]] = jnp.stack([g_shard[l * len(BIG) + i] for l in range(DEPTH)])
        if k.endswith("_t"):
            out_g[orig[k]] = tr(out_g[orig[k]])
    for n in [*orig.values(), "ada_w"]:
        out_d[n], out_m[n], out_v[n] = _adamw_shard(wts[n], out_g[n], mom[n], var[n], f"adamw_{n}")
    small_all = small_names + ["ada_b", "final_norm_g"]
    ds, ms, vs = _adamw_small([wts[n] for n in small_all], [grads[n] for n in small_all], [mom[n] for n in small_all],
                              [var[n] for n in small_all], "adamw_small")
    for n, d, m_, v_ in zip(small_all, ds, ms, vs):
        out_d[n], out_m[n], out_v[n] = d, m_, v_
    return (loss, grad_x, *[out_g[n] for n in names], *[out_d[n] for n in names], *[out_m[n] for n in names], *[out_v[n] for n in names])
```

```python
import functools
import math

import numpy as np
import jax
import jax.numpy as jnp
from jax import lax
from jax.experimental import pallas as pl
from jax.experimental.pallas import tpu as pltpu

F32 = jnp.float32
BF16 = jnp.bfloat16
_MXU_DTYPE = jnp.bfloat16

N_DEV = 8
D_MODEL = 2048
DEPTH = 2
GROUP_WIDTH = 512
D_FF = 8192
S5_CH, S5_GROUPS, S5_STATE = 16, 32, 64
S5_WIDTH = S5_GROUPS * S5_STATE
RET_HEADS, RET_QK, RET_V, RET_CHUNK = 4, 64, 128, 128
SWA_HD, SWA_HEADS, SWA_KV_HEADS, WINDOW = 64, 8, 2, 128
MLA_HEADS, MLA_Q_RANK, MLA_KV_RANK, MLA_NOPE, MLA_ROPE, MLA_V = 4, 384, 128, 128, 64, 128
ROPE_BASE = 10000.0
EPS = 1e-6
NEG = -1e30
N_IN = 3392
ADAM_LR, ADAM_B1, ADAM_B2, ADAM_EPS, ADAM_WD, ADAM_STEP = 0.001, 0.9, 0.999, 1e-08, 0.01, 10

VMEM_LIMIT_BYTES = 52 * 1024 * 1024
MESH_ID = pl.DeviceIdType.MESH
_ANY = pl.BlockSpec(memory_space=pl.ANY)
_SMEM = pl.BlockSpec(memory_space=pltpu.SMEM)


def _params(sem):
    return pltpu.CompilerParams(dimension_semantics=sem, vmem_limit_bytes=VMEM_LIMIT_BYTES)


_DIMS = {"nn": (((1,), (0,)), ((), ())), "nt": (((1,), (1,)), ((), ())), "tn": (((0,), (0,)), ((), ()))}


def _dot(a, b, mode="nn"):
    return lax.dot_general(a.astype(_MXU_DTYPE), b.astype(_MXU_DTYPE), _DIMS[mode], preferred_element_type=F32)


def _mm(a, b, mode, tm, tn, tk, out_dtype=F32, name="mm", b_off=0, n=None, pair=None, epi=None, epi_ins=(), epi_outs=None):
    if mode == "tn":
        kdim, m = a.shape
    else:
        m, kdim = a.shape
    if n is None:
        n = b.shape[0] if mode == "nt" else b.shape[1]
    tm, tn, tk = min(tm, m), min(tn, n), min(tk, kdim)
    assert m % tm == 0 and n % tn == 0 and kdim % tk == 0, (name, a.shape, b.shape, tm, tn, tk)
    nk = kdim // tk
    a_spec = pl.BlockSpec((tk, tm), lambda i, j, k: (k, i)) if mode == "tn" else pl.BlockSpec((tm, tk), lambda i, j, k: (i, k))
    if mode == "nt":
        b_spec = pl.BlockSpec((tn, tk), lambda i, j, k: (j + b_off, k))
    else:
        b_spec = pl.BlockSpec((tk, tn), lambda i, j, k: (k, j + b_off))
    o_spec = pl.BlockSpec((tm, tn), lambda i, j, k: (i, j))
    n_mm = 2 if pair is None else 4
    out_dtypes = [out_dtype] if epi is None else list(epi_outs)

    def body(*refs):
        ins, extra = refs[:n_mm], refs[n_mm:n_mm + len(epi_ins)]
        outs = refs[n_mm + len(epi_ins):n_mm + len(epi_ins) + len(out_dtypes)]
        part = _dot(ins[0][...], ins[1][...], mode)
        if pair is not None:
            part = part + _dot(ins[2][...], ins[3][...], mode)

        def finish(acc):
            vals = (acc,) if epi is None else epi(acc, *[r[...] for r in extra])
            for o_ref, v, dt in zip(outs, vals, out_dtypes):
                o_ref[...] = v.astype(dt)

        if nk == 1:
            finish(part)
        else:
            acc_ref = refs[-1]
            k = pl.program_id(2)

            @pl.when(k == 0)
            def _():
                acc_ref[...] = part

            @pl.when(k > 0)
            def _():
                acc_ref[...] += part

            @pl.when(k == nk - 1)
            def _():
                finish(acc_ref[...])

    operands = [a, b] + ([] if pair is None else list(pair)) + list(epi_ins)
    res = pl.pallas_call(
        body, name=name, grid=(m // tm, n // tn, nk),
        in_specs=[a_spec, b_spec] * (n_mm // 2) + [o_spec] * len(epi_ins),
        out_specs=[o_spec] * len(out_dtypes), out_shape=[jax.ShapeDtypeStruct((m, n), dt) for dt in out_dtypes],
        scratch_shapes=[] if nk == 1 else [pltpu.VMEM((tm, tn), F32)],
        compiler_params=_params(("parallel", "parallel", "arbitrary")),
    )(*operands)
    return res[0] if epi is None else res


SUBLANES = 8


def _row_tile(rows, target, mult=SUBLANES):
    best = None
    for cand in range(mult, min(rows, target) + 1, mult):
        if rows % cand == 0:
            best = cand
    return best or rows


def _ew(fn, ins, outs, tt, name):
    t = [a.shape[0] for a, tiled in ins if tiled][0]
    tt = _row_tile(t, tt)
    n_in = len(ins)
    in_specs = [pl.BlockSpec((tt, a.shape[1]), lambda i: (i, 0)) if tiled else pl.BlockSpec(a.shape, lambda i: (0, 0))
                for a, tiled in ins]
    out_specs, out_shapes = [], []
    for w, dt, kind in outs:
        if kind == "tile":
            out_specs.append(pl.BlockSpec((tt, w), lambda i: (i, 0)))
            out_shapes.append(jax.ShapeDtypeStruct((t, w), dt))
        else:
            out_specs.append(pl.BlockSpec((1, w), lambda i: (0, 0)))
            out_shapes.append(jax.ShapeDtypeStruct((1, w), F32))
    has_acc = any(kind == "acc" for _, _, kind in outs)

    def body(*refs):
        vals = fn(*[r[...] for r in refs[:n_in]])
        if not isinstance(vals, (tuple, list)):
            vals = (vals,)
        i = pl.program_id(0)
        for o_ref, v, (w, dt, kind) in zip(refs[n_in:], vals, outs):
            if kind == "tile":
                o_ref[...] = v.astype(dt)
            else:
                @pl.when(i == 0)
                def _(o_ref=o_ref, v=v):
                    o_ref[...] = v.astype(F32)

                @pl.when(i > 0)
                def _(o_ref=o_ref, v=v):
                    o_ref[...] += v.astype(F32)

    res = pl.pallas_call(
        body, name=name, grid=(t // tt,), in_specs=in_specs, out_specs=out_specs, out_shape=out_shapes,
        compiler_params=_params(("arbitrary" if has_acc else "parallel",)),
    )(*[a for a, _ in ins])
    return res


def _whole(fn, ins, outs, name):
    def body(*refs):
        vals = fn(*[r[...] for r in refs[:len(ins)]])
        if not isinstance(vals, (tuple, list)):
            vals = (vals,)
        for o_ref, v in zip(refs[len(ins):], vals):
            o_ref[...] = v.astype(o_ref.dtype)

    return pl.pallas_call(body, name=name, out_shape=[jax.ShapeDtypeStruct(s, dt) for s, dt in outs])(*ins)


def _rms(x):
    return x * lax.rsqrt(jnp.mean(x * x, axis=-1, keepdims=True) + EPS)


def _norm_mod_fn(h, g, sc, sh):
    return (_rms(h) * g) * (1.0 + sc) + sh


def _rms_gain_fn(x, g):
    return _rms(x) * g


def _gate_add_fn(h, y, gt):
    return h + gt * y


def _relu2_fn(x):
    return jnp.square(jnp.maximum(x, 0.0))


def _s5_act_fn(ypre, u, dskip):
    return jax.nn.gelu(ypre + dskip * u)


def _s5_glu_fn(z, zz, b):
    return z * jax.nn.sigmoid(zz + b)


def _ret_gate_fn(o, g):
    return _rms(o) * (g * jax.nn.sigmoid(g))


def _final_fn(h, g, tgt):
    err = _rms(h) * g - tgt
    return 0.5 * jnp.sum(jnp.mean(err * err, axis=-1, keepdims=True), axis=0, keepdims=True)


def _vjp_block(fn, n_args):
    def bwd(*vals):
        _, pull = jax.vjp(fn, *vals[:n_args])
        return pull(vals[n_args])
    return bwd


def _rope_tables(t):
    d = RET_QK
    inv = ROPE_BASE ** (-jnp.arange(0, d, 2, dtype=F32) / d)
    ang = jnp.arange(t, dtype=F32)[:, None] * inv[None, :]
    cos, sin = jnp.cos(ang), jnp.sin(ang)
    cos2, sin2 = jnp.concatenate([cos, cos], -1), jnp.concatenate([-sin, sin], -1)
    ret = (jnp.tile(cos2, (1, 8)), jnp.tile(sin2, (1, 8)))
    one, zero = jnp.ones((t, MLA_NOPE), F32), jnp.zeros((t, MLA_NOPE), F32)
    mla_c = jnp.concatenate([jnp.tile(jnp.concatenate([one, cos2], -1), (1, MLA_HEADS)), cos2, one[:, :d]], -1)
    mla_s = jnp.concatenate([jnp.tile(jnp.concatenate([zero, sin2], -1), (1, MLA_HEADS)), sin2, zero[:, :d]], -1)
    return ret, (mla_c, mla_s)


def _rope_fn(x, c, s, sign):
    w = x.shape[1]
    lane = lax.broadcasted_iota(jnp.int32, x.shape, 1)
    swapped = jnp.where((lane & 63) < 32, pltpu.roll(x, w - 32, 1), pltpu.roll(x, 32, 1))
    return x * c + swapped * (sign * s)


def _rope(x, tables, name, inverse=False, out_dtype=F32):
    c, s = tables
    fn = functools.partial(_rope_fn, sign=-1.0 if inverse else 1.0)
    return _ew(fn, [(x, True), (c, True), (s, True)], [(x.shape[1], out_dtype, "tile")], ROWS, name)[0]


SCAN_ROWS, SCAN_LANES = 256, 512


def _cmul(ar, ai, br, bi):
    return ar * br - ai * bi, ar * bi + ai * br


def _scan_chunk(xr, xi, ar, ai, reverse):
    rows = xr.shape[0]
    row = lax.broadcasted_iota(jnp.int32, xr.shape, 0)
    pr, pi = ar, ai
    k = 1
    while k < rows:
        if reverse:
            sr, si = pltpu.roll(xr, rows - k, 0), pltpu.roll(xi, rows - k, 0)
            keep = row < rows - k
        else:
            sr, si = pltpu.roll(xr, k, 0), pltpu.roll(xi, k, 0)
            keep = row >= k
        tr, ti = _cmul(pr, pi, sr, si)
        xr = xr + jnp.where(keep, tr, 0.0)
        xi = xi + jnp.where(keep, ti, 0.0)
        pr, pi = _cmul(pr, pi, pr, pi)
        k *= 2
    return xr, xi


def _s5_scan_fwd(bu, a_r, a_i, name):
    t = bu.shape[0]
    rows = min(SCAN_ROWS, t)
    nl = S5_WIDTH // SCAN_LANES

    def body(br_ref, bi_ref, ar_ref, ai_ref, or_ref, oi_ref, cr_ref, ci_ref):
        i = pl.program_id(1)

        @pl.when(i == 0)
        def _():
            cr_ref[...] = jnp.zeros_like(cr_ref)
            ci_ref[...] = jnp.zeros_like(ci_ref)

        ar, ai = ar_ref[...], ai_ref[...]
        xr, xi = br_ref[...], bi_ref[...]
        inr, ini = _cmul(ar, ai, cr_ref[...], ci_ref[...])
        first = lax.broadcasted_iota(jnp.int32, xr.shape, 0) == 0
        xr = xr + jnp.where(first, inr, 0.0)
        xi = xi + jnp.where(first, ini, 0.0)
        xr, xi = _scan_chunk(xr, xi, ar, ai, reverse=False)
        or_ref[...] = xr
        oi_ref[...] = xi
        cr_ref[...] = xr[rows - 1:rows, :]
        ci_ref[...] = xi[rows - 1:rows, :]

    blk = lambda off: pl.BlockSpec((rows, SCAN_LANES), lambda j, i: (i, j + off))
    par = pl.BlockSpec((1, SCAN_LANES), lambda j, i: (0, j))
    st_r, st_i = pl.pallas_call(
        body, name=name, grid=(nl, t // rows), in_specs=[blk(0), blk(nl), par, par], out_specs=[blk(0), blk(0)],
        out_shape=[jax.ShapeDtypeStruct((t, S5_WIDTH), F32)] * 2,
        scratch_shapes=[pltpu.VMEM((1, SCAN_LANES), F32)] * 2, compiler_params=_params(("parallel", "arbitrary")),
    )(bu, bu, a_r, a_i)
    return st_r, st_i


def _s5_scan_bwd(dst, st_r, st_i, a_r, a_i, name):
    t = dst.shape[0]
    rows = min(SCAN_ROWS, t)
    nl = S5_WIDTH // SCAN_LANES
    nc = t // rows

    def body(dr_ref, di_ref, xr_ref, xi_ref, ar_ref, ai_ref, gr_ref, gi_ref, dar_ref, dai_ref, cr_ref, ci_ref):
        i = pl.program_id(1)

        @pl.when(i == 0)
        def _():
            cr_ref[...] = jnp.zeros_like(cr_ref)
            ci_ref[...] = jnp.zeros_like(ci_ref)
            dar_ref[...] = jnp.zeros_like(dar_ref)
            dai_ref[...] = jnp.zeros_like(dai_ref)

        ar, ai = ar_ref[...], ai_ref[...]
        cr, ci = cr_ref[...], ci_ref[...]
        gr, gi = dr_ref[...], di_ref[...]
        row = lax.broadcasted_iota(jnp.int32, gr.shape, 0)
        last = row == rows - 1
        inr, ini = _cmul(ar, -ai, cr, ci)
        gr = gr + jnp.where(last, inr, 0.0)
        gi = gi + jnp.where(last, ini, 0.0)
        gr, gi = _scan_chunk(gr, gi, ar, -ai, reverse=True)
        gr_ref[...] = gr
        gi_ref[...] = gi
        nr = jnp.where(last, cr, pltpu.roll(gr, rows - 1, 0))
        ni = jnp.where(last, ci, pltpu.roll(gi, rows - 1, 0))
        xr, xi = xr_ref[...], xi_ref[...]
        dar_ref[...] += jnp.sum(nr * xr + ni * xi, axis=0, keepdims=True)
        dai_ref[...] += jnp.sum(ni * xr - nr * xi, axis=0, keepdims=True)
        cr_ref[...] = gr[0:1, :]
        ci_ref[...] = gi[0:1, :]

    blk = lambda off: pl.BlockSpec((rows, SCAN_LANES), lambda j, i: (nc - 1 - i, j + off))
    par = pl.BlockSpec((1, SCAN_LANES), lambda j, i: (0, j))
    return pl.pallas_call(
        body, name=name, grid=(nl, nc), in_specs=[blk(0), blk(nl), blk(0), blk(0), par, par],
        out_specs=[blk(0), blk(0), par, par],
        out_shape=[jax.ShapeDtypeStruct((t, S5_WIDTH), F32)] * 2 + [jax.ShapeDtypeStruct((1, S5_WIDTH), F32)] * 2,
        scratch_shapes=[pltpu.VMEM((1, SCAN_LANES), F32)] * 2, compiler_params=_params(("parallel", "arbitrary")),
    )(dst, dst, st_r, st_i, a_r, a_i)


def _s5_prep(lam_re, lam_im, log_dt, b_re, b_im, c_re, c_im, d_skip):
    dt = jnp.exp(log_dt)[:, None]
    mag = jnp.exp(lam_re * dt)
    ar, ai = mag * jnp.cos(lam_im * dt), mag * jnp.sin(lam_im * dt)
    den = lam_re * lam_re + lam_im * lam_im
    cr = ((ar - 1.0) * lam_re + ai * lam_im) / den
    ci = (ai * lam_re - (ar - 1.0) * lam_im) / den
    bbar_r = cr[..., None] * b_re - ci[..., None] * b_im
    bbar_i = cr[..., None] * b_im + ci[..., None] * b_re
    eye = jnp.eye(S5_GROUPS, dtype=F32)

    def bdiag(m):
        g, a, b = m.shape
        return (eye[:, None, :, None] * m[:, :, None, :]).reshape(g * a, g * b)

    b_cat = jnp.concatenate([bdiag(bbar_r.transpose(0, 2, 1)), bdiag(bbar_i.transpose(0, 2, 1))], axis=1)
    c_cat = jnp.concatenate([bdiag(c_re.transpose(0, 2, 1)), -bdiag(c_im.transpose(0, 2, 1))], axis=0)
    return b_cat, c_cat, d_skip.reshape(1, GROUP_WIDTH), ar.reshape(1, S5_WIDTH), ai.reshape(1, S5_WIDTH)


def _ret_consts(lgam):
    c = RET_CHUNK
    r = lax.broadcasted_iota(jnp.int32, (c, c), 0)
    m = lax.broadcasted_iota(jnp.int32, (c, c), 1)
    rel = (r - m).astype(F32)
    decay = jnp.where(rel >= 0, jnp.exp(lgam * jnp.maximum(rel, 0.0)), 0.0)
    idx = lax.broadcasted_iota(jnp.int32, (c, 1), 0).astype(F32)
    zeta = jnp.exp(lgam * (c - 1.0 - idx))
    xi = jnp.exp(lgam * (idx + 1.0))
    return decay, zeta, xi, jnp.exp(lgam * c)


def _ret_specs(t):
    qk = lambda off: pl.BlockSpec((1, t, RET_QK), lambda h: (h + off, 0, 0))
    col = lambda off: pl.BlockSpec((t, RET_V), lambda h: (0, h + off))
    return qk, col


def _ret_fwd(qk, p_ret, lgam, name):
    t = qk.shape[1]
    nck = t // RET_CHUNK
    qk_spec, col = _ret_specs(t)

    def body(lg_ref, q_ref, k_ref, v_ref, g_ref, o_ref, y_ref):
        decay, zeta, xi, gam = _ret_consts(lg_ref[pl.program_id(0)])

        def step(n, state):
            sl = pl.ds(pl.multiple_of(n * RET_CHUNK, RET_CHUNK), RET_CHUNK)
            q, k, v = q_ref[0, sl, :], k_ref[0, sl, :] * (RET_QK ** -0.5), v_ref[sl, :]
            s = _dot(q, k, "nt") * decay
            o = _dot(s, v) + _dot(q, state) * xi
            o_ref[sl, :] = o
            y_ref[sl, :] = _ret_gate_fn(o, g_ref[sl, :]).astype(y_ref.dtype)
            return gam * state + _dot(k, zeta * v, "tn")

        lax.fori_loop(0, nck, step, jnp.zeros((RET_QK, RET_V), F32))

    return pl.pallas_call(
        body, name=name, grid=(RET_HEADS,), in_specs=[_SMEM, qk_spec(0), qk_spec(RET_HEADS), col(4), col(8)],
        out_specs=[col(0), col(0)],
        out_shape=[jax.ShapeDtypeStruct((t, GROUP_WIDTH), F32), jax.ShapeDtypeStruct((t, GROUP_WIDTH), BF16)],
        compiler_params=_params(("parallel",)),
    )(lgam, qk, qk, p_ret, p_ret)


def _ret_bwd(qk, p_ret, o_all, dy, lgam, name):
    t = qk.shape[1]
    nck = t // RET_CHUNK
    qk_spec, col = _ret_specs(t)
    gate_bwd = _vjp_block(_ret_gate_fn, 2)

    def body(lg_ref, q_ref, k_ref, v_ref, g_ref, o_ref, dy_ref, dq_ref, dk_ref, dv_ref, dg_ref, st_ref):
        decay, zeta, xi, gam = _ret_consts(lg_ref[pl.program_id(0)])
        scale = RET_QK ** -0.5

        def fstep(n, state):
            sl = pl.ds(pl.multiple_of(n * RET_CHUNK, RET_CHUNK), RET_CHUNK)
            st_ref[n] = state
            return gam * state + _dot(k_ref[0, sl, :] * scale, zeta * v_ref[sl, :], "tn")

        lax.fori_loop(0, nck, fstep, jnp.zeros((RET_QK, RET_V), F32))

        def bstep(r, grad_state):
            n = nck - 1 - r
            sl = pl.ds(pl.multiple_of(n * RET_CHUNK, RET_CHUNK), RET_CHUNK)
            q, k, v = q_ref[0, sl, :], k_ref[0, sl, :] * scale, v_ref[sl, :]
            d_o, dg = gate_bwd(o_ref[sl, :], g_ref[sl, :], dy_ref[sl, :])
            dg_ref[sl, :] = dg
            s = _dot(q, k, "nt") * decay
            ds = _dot(d_o, v, "nt") * decay
            xdo = xi * d_o
            dq_ref[0, sl, :] = _dot(ds, k) + _dot(xdo, st_ref[n], "nt")
            dk_ref[0, sl, :] = (_dot(ds, q, "tn") + _dot(zeta * v, grad_state, "nt")) * scale
            dv_ref[sl, :] = _dot(s, d_o, "tn") + zeta * _dot(k, grad_state)
            return gam * grad_state + _dot(q, xdo, "tn")

        lax.fori_loop(0, nck, bstep, jnp.zeros((RET_QK, RET_V), F32))

    hd = pl.BlockSpec((1, t, RET_QK), lambda h: (h, 0, 0))
    return pl.pallas_call(
        body, name=name, grid=(RET_HEADS,),
        in_specs=[_SMEM, qk_spec(0), qk_spec(RET_HEADS), col(4), col(8), col(0), col(0)],
        out_specs=[hd, hd, col(0), col(0)],
        out_shape=[jax.ShapeDtypeStruct((RET_HEADS, t, RET_QK), F32)] * 2 + [jax.ShapeDtypeStruct((t, GROUP_WIDTH), F32)] * 2,
        scratch_shapes=[pltpu.VMEM((nck, RET_QK, RET_V), F32)], compiler_params=_params(("parallel",)),
    )(lgam, qk, qk, p_ret, p_ret, o_all, dy)


SWA_GROUP = SWA_HEADS // SWA_KV_HEADS
SWA_SCALE = SWA_HD ** -0.5


def _swa_mask(n):
    rows = SWA_GROUP * WINDOW
    r = lax.broadcasted_iota(jnp.int32, (rows, 2 * WINDOW), 0) & (WINDOW - 1)
    j = lax.broadcasted_iota(jnp.int32, (rows, 2 * WINDOW), 1)
    dist = r + WINDOW - j
    return (dist >= 0) & (dist < WINDOW) & (n * WINDOW + j - WINDOW >= 0)


def _swa_sink_rows(sink_ref, kv):
    row = lax.broadcasted_iota(jnp.int32, (SWA_GROUP * WINDOW, 1), 0)
    sink = jnp.zeros((SWA_GROUP * WINDOW, 1), F32)
    for g in range(SWA_GROUP):
        sink = jnp.where(row >= g * WINDOW, sink_ref[kv * SWA_GROUP + g], sink)
    return sink


def _swa_pad_keys(n, k_ref, v_ref, kp_ref, vp_ref):
    @pl.when(n == 0)
    def _():
        zero = jnp.zeros((WINDOW, SWA_HD), F32)
        kp_ref[0:WINDOW, :] = zero
        vp_ref[0:WINDOW, :] = zero
        kp_ref[WINDOW:, :] = k_ref[0]
        vp_ref[WINDOW:, :] = v_ref[0]


def _swa_specs(t):
    blk = lambda w: pl.BlockSpec((SWA_GROUP, WINDOW, w), lambda kv, n: (kv, n, 0))
    kspec = lambda off: pl.BlockSpec((1, t, SWA_HD), lambda kv, n: (SWA_HEADS + off + kv, 0, 0))
    return blk, kspec


def _swa_fwd(qkv, sinks, name):
    t = qkv.shape[1]
    rows = SWA_GROUP * WINDOW
    blk, kspec = _swa_specs(t)

    def body(sink_ref, q_ref, k_ref, v_ref, o_ref, lse_ref, kp_ref, vp_ref):
        kv, n = pl.program_id(0), pl.program_id(1)
        _swa_pad_keys(n, k_ref, v_ref, kp_ref, vp_ref)
        win = pl.ds(pl.multiple_of(n * WINDOW, WINDOW), 2 * WINDOW)
        sink = _swa_sink_rows(sink_ref, kv)
        s = _dot(q_ref[...].reshape(rows, SWA_HD), kp_ref[win, :], "nt") * SWA_SCALE
        s = jnp.where(_swa_mask(n), s, NEG)
        m = jnp.maximum(jnp.max(s, axis=-1, keepdims=True), sink)
        p = jnp.exp(s - m)
        den = jnp.sum(p, axis=-1, keepdims=True) + jnp.exp(sink - m)
        o_ref[...] = _dot(p / den, vp_ref[win, :]).reshape(SWA_GROUP, WINDOW, SWA_HD)
        lse_ref[...] = (m + jnp.log(den)).reshape(SWA_GROUP, WINDOW, 1)

    return pl.pallas_call(
        body, name=name, grid=(SWA_KV_HEADS, t // WINDOW),
        in_specs=[_SMEM, blk(SWA_HD), kspec(0), kspec(SWA_KV_HEADS)], out_specs=[blk(SWA_HD), blk(1)],
        out_shape=[jax.ShapeDtypeStruct((SWA_HEADS, t, SWA_HD), F32), jax.ShapeDtypeStruct((SWA_HEADS, t, 1), F32)],
        scratch_shapes=[pltpu.VMEM((t + WINDOW, SWA_HD), F32)] * 2, compiler_params=_params(("parallel", "arbitrary")),
    )(sinks, qkv, qkv, qkv)


def _swa_bwd(qkv, o, lse, d_o, sinks, name):
    t = qkv.shape[1]
    nb = t // WINDOW
    rows = SWA_GROUP * WINDOW
    blk, kspec = _swa_specs(t)

    def body(sink_ref, q_ref, k_ref, v_ref, o_ref, lse_ref, do_ref, dq_ref, dk_ref, dv_ref, dsink_ref,
             kp_ref, vp_ref, dkp_ref, dvp_ref):
        kv, n = pl.program_id(0), pl.program_id(1)
        _swa_pad_keys(n, k_ref, v_ref, kp_ref, vp_ref)

        @pl.when(n == 0)
        def _():
            dkp_ref[...] = jnp.zeros_like(dkp_ref)
            dvp_ref[...] = jnp.zeros_like(dvp_ref)
            dsink_ref[...] = jnp.zeros_like(dsink_ref)

        win = pl.ds(pl.multiple_of(n * WINDOW, WINDOW), 2 * WINDOW)
        sink = _swa_sink_rows(sink_ref, kv)
        q, dout = q_ref[...].reshape(rows, SWA_HD), do_ref[...].reshape(rows, SWA_HD)
        lse_n = lse_ref[...].reshape(rows, 1)
        s = _dot(q, kp_ref[win, :], "nt") * SWA_SCALE
        s = jnp.where(_swa_mask(n), s, NEG)
        p = jnp.exp(s - lse_n)
        delta = jnp.sum(dout * o_ref[...].reshape(rows, SWA_HD), axis=-1, keepdims=True)
        ds = p * (_dot(dout, vp_ref[win, :], "nt") - delta)
        dq_ref[...] = (_dot(ds, kp_ref[win, :]) * SWA_SCALE).reshape(SWA_GROUP, WINDOW, SWA_HD)
        dkp_ref[win, :] += _dot(ds, q, "tn") * SWA_SCALE
        dvp_ref[win, :] += _dot(p, dout, "tn")
        term = jnp.exp(sink - lse_n) * delta
        head = lax.broadcasted_iota(jnp.int32, (SWA_GROUP, 128), 0)
        acc = jnp.zeros((SWA_GROUP, 128), F32)
        for g in range(SWA_GROUP):
            acc = jnp.where(head == g, jnp.sum(term[g * WINDOW:(g + 1) * WINDOW], axis=0, keepdims=True), acc)
        dsink_ref[0] -= acc

        @pl.when(n == nb - 1)
        def _():
            dk_ref[0] = dkp_ref[WINDOW:, :]
            dv_ref[0] = dvp_ref[WINDOW:, :]

    kout = pl.BlockSpec((1, t, SWA_HD), lambda kv, n: (kv, 0, 0))
    dq, dk, dv, dsink = pl.pallas_call(
        body, name=name, grid=(SWA_KV_HEADS, nb),
        in_specs=[_SMEM, blk(SWA_HD), kspec(0), kspec(SWA_KV_HEADS), blk(SWA_HD), blk(1), blk(SWA_HD)],
        out_specs=[blk(SWA_HD), kout, kout, pl.BlockSpec((1, SWA_GROUP, 128), lambda kv, n: (kv, 0, 0))],
        out_shape=[jax.ShapeDtypeStruct((SWA_HEADS, t, SWA_HD), F32), jax.ShapeDtypeStruct((SWA_KV_HEADS, t, SWA_HD), F32),
                   jax.ShapeDtypeStruct((SWA_KV_HEADS, t, SWA_HD), F32), jax.ShapeDtypeStruct((SWA_KV_HEADS, SWA_GROUP, 128), F32)],
        scratch_shapes=[pltpu.VMEM((t + WINDOW, SWA_HD), F32)] * 4, compiler_params=_params(("parallel", "arbitrary")),
    )(sinks, qkv, qkv, qkv, o, lse, d_o)
    return jnp.concatenate([dq, dk, dv], axis=0), dsink[:, :, 0].reshape(SWA_HEADS)


MLA_SCALE = (MLA_NOPE + MLA_ROPE) ** -0.5
MLA_TILE = 256


def _mla_diag(s):
    r = lax.broadcasted_iota(jnp.int32, s.shape, 0)
    c = lax.broadcasted_iota(jnp.int32, s.shape, 1)
    return jnp.where(c <= r, s, NEG)


def _mla_specs(t, tile):
    whole = lambda w, off: pl.BlockSpec((t, w), lambda h, i: (0, 2 * h + off))
    head = lambda w: pl.BlockSpec((1, t, w), lambda h, i: (h, 0, 0))
    key_rope = pl.BlockSpec((1, t, MLA_ROPE), lambda h, i: (MLA_HEADS, 0, 0))
    tile_of = lambda w: pl.BlockSpec((1, tile, w), lambda h, i: (h, i, 0))
    return whole, head, key_rope, tile_of


def _mla_attend(qn, rot, kv, name):
    t = qn.shape[1]
    tile = min(MLA_TILE, t)
    whole, head, key_rope, tile_of = _mla_specs(t, tile)

    def body(qn_ref, qr_ref, kn_ref, kr_ref, v_ref, o_ref, lse_ref, m_ref, l_ref, acc_ref):
        i = pl.program_id(1)
        qn_b, qr_b = qn_ref[0], qr_ref[0]

        def rows(j):
            return pl.ds(pl.multiple_of(j * tile, tile), tile)

        def scores(j):
            return (_dot(qn_b, kn_ref[rows(j), :], "nt") + _dot(qr_b, kr_ref[0, rows(j), :], "nt")) * MLA_SCALE

        def update(s, j):
            m_old = m_ref[...]
            m_new = jnp.maximum(m_old, jnp.max(s, axis=-1, keepdims=True))
            alpha = jnp.exp(m_old - m_new)
            p = jnp.exp(s - m_new)
            l_ref[...] = alpha * l_ref[...] + jnp.sum(p, axis=-1, keepdims=True)
            acc_ref[...] = alpha * acc_ref[...] + _dot(p, v_ref[rows(j), :])
            m_ref[...] = m_new

        m_ref[...] = jnp.full_like(m_ref, NEG)
        l_ref[...] = jnp.zeros_like(l_ref)
        acc_ref[...] = jnp.zeros_like(acc_ref)

        def step(j, s_cur):
            s_next = scores(j + 1)
            update(s_cur, j)
            return s_next

        s_diag = lax.fori_loop(0, i, step, scores(0))
        update(_mla_diag(s_diag), i)
        o_ref[...] = acc_ref[...] / l_ref[...]
        lse_ref[0] = m_ref[...] + jnp.log(l_ref[...])

    return pl.pallas_call(
        body, name=name, grid=(MLA_HEADS, t // tile),
        in_specs=[tile_of(MLA_NOPE), tile_of(MLA_ROPE), whole(MLA_NOPE, 0), key_rope, whole(MLA_V, 1)],
        out_specs=[pl.BlockSpec((tile, MLA_V), lambda h, i: (i, h)), tile_of(1)],
        out_shape=[jax.ShapeDtypeStruct((t, GROUP_WIDTH), F32), jax.ShapeDtypeStruct((MLA_HEADS, t, 1), F32)],
        scratch_shapes=[pltpu.VMEM((tile, 1), F32), pltpu.VMEM((tile, 1), F32), pltpu.VMEM((tile, MLA_V), F32)],
        compiler_params=_params(("parallel", "parallel")),
    )(qn, rot, kv, rot, kv)


def _mla_delta(o, d_o, name):
    t = o.shape[0]
    tt = min(ROWS, t)

    def body(o_ref, do_ref, d_ref):
        d_ref[0] = jnp.sum(o_ref[...] * do_ref[...], axis=-1, keepdims=True)

    blk = pl.BlockSpec((tt, MLA_V), lambda h, i: (i, h))
    return pl.pallas_call(
        body, name=name, grid=(MLA_HEADS, t // tt), in_specs=[blk, blk],
        out_specs=pl.BlockSpec((1, tt, 1), lambda h, i: (h, i, 0)), out_shape=jax.ShapeDtypeStruct((MLA_HEADS, t, 1), F32),
        compiler_params=_params(("parallel", "parallel")),
    )(o, d_o)


def _mla_attend_bwd(qn, rot, kv, lse, delta, d_o, name):
    t = qn.shape[1]
    tile = min(MLA_TILE, t)
    nt = t // tile
    whole, head, key_rope, tile_of = _mla_specs(t, tile)

    def body(qn_ref, qr_ref, kn_ref, kr_ref, v_ref, lse_ref, dl_ref, do_ref, dqn_ref, dqr_ref, dkn_ref, dv_ref, dkr_ref):
        j = pl.program_id(1)

        @pl.when(j == 0)
        def _():
            dqn_ref[...] = jnp.zeros_like(dqn_ref)
            dqr_ref[...] = jnp.zeros_like(dqr_ref)

        dkn_ref[...] = jnp.zeros_like(dkn_ref)
        dv_ref[...] = jnp.zeros_like(dv_ref)
        dkr_ref[...] = jnp.zeros_like(dkr_ref)
        kn_b, kr_b, v_b = kn_ref[...], kr_ref[0], v_ref[...]

        def block(i, diagonal):
            sl = pl.ds(pl.multiple_of(i * tile, tile), tile)
            qn_b, qr_b, dout = qn_ref[0, sl, :], qr_ref[0, sl, :], do_ref[sl, :]
            s = (_dot(qn_b, kn_b, "nt") + _dot(qr_b, kr_b, "nt")) * MLA_SCALE
            if diagonal:
                s = _mla_diag(s)
            p = jnp.exp(s - lse_ref[0, sl, :])
            ds = p * (_dot(dout, v_b, "nt") - dl_ref[0, sl, :]) * MLA_SCALE
            dv_ref[...] += _dot(p, dout, "tn")
            dkn_ref[...] += _dot(ds, qn_b, "tn")
            dkr_ref[0] += _dot(ds, qr_b, "tn")
            dqn_ref[0, sl, :] += _dot(ds, kn_b)
            dqr_ref[0, sl, :] += _dot(ds, kr_b)

        block(j, True)

        def step(i, carry):
            block(i, False)
            return carry

        lax.fori_loop(j + 1, nt, step, 0)

    key_tile = lambda w, off: pl.BlockSpec((tile, w), lambda h, j: (j, 2 * h + off))
    out_tile = pl.BlockSpec((tile, MLA_V), lambda h, j: (j, h))
    return pl.pallas_call(
        body, name=name, grid=(MLA_HEADS, nt),
        in_specs=[head(MLA_NOPE), head(MLA_ROPE), key_tile(MLA_NOPE, 0), pl.BlockSpec((1, tile, MLA_ROPE), lambda h, j: (MLA_HEADS, j, 0)),
                  key_tile(MLA_V, 1), head(1), head(1), pl.BlockSpec((t, MLA_V), lambda h, j: (0, h))],
        out_specs=[head(MLA_NOPE), head(MLA_ROPE), out_tile, out_tile, tile_of(MLA_ROPE)],
        out_shape=[jax.ShapeDtypeStruct((MLA_HEADS, t, MLA_NOPE), F32), jax.ShapeDtypeStruct((MLA_HEADS, t, MLA_ROPE), F32),
                   jax.ShapeDtypeStruct((t, MLA_HEADS * MLA_NOPE), F32), jax.ShapeDtypeStruct((t, MLA_HEADS * MLA_V), F32),
                   jax.ShapeDtypeStruct((MLA_HEADS, t, MLA_ROPE), F32)],
        compiler_params=_params(("parallel", "arbitrary")),
    )(qn, rot, kv, rot, kv, lse, delta, d_o)


def _place():
    return lax.axis_index("x"), lax.axis_index("y"), lax.axis_index("c")


def _all_gather(arrs, name):
    n = len(arrs)

    def body(*refs):
        x_refs, o_refs = refs[:n], refs[n:2 * n]
        send_sems, recv_sems, local_sems = refs[2 * n:]
        x, y, c = _place()
        me, sibling = (x, y, c), (x, y, 1 - c)
        chips = [(1 - x, y), (x, 1 - y), (1 - x, 1 - y)]

        def slot(a, p):
            return o_refs[a].at[4 * p[0] + 2 * p[1] + p[2]]

        def copy(a, k, block, to, src=None):
            return pltpu.make_async_remote_copy(
                src_ref=slot(a, block) if src is None else src, dst_ref=slot(a, block),
                send_sem=send_sems.at[a, k], recv_sem=recv_sems.at[a, k], device_id=to, device_id_type=MESH_ID)

        mine = [pltpu.make_async_copy(x_refs[a], slot(a, me), local_sems.at[a]) for a in range(n)]
        for cp in mine:
            cp.start()
        first = []
        for a in range(n):
            first.append(copy(a, 0, me, sibling, src=x_refs[a]))
            first += [copy(a, 1 + j, me, (*chip, c), src=x_refs[a]) for j, chip in enumerate(chips)]
        for cp in first:
            cp.start()
        passed = []
        for j, chip in enumerate(chips):
            for a in range(n):
                copy(a, 1 + j, (*chip, c), me).wait_recv()
                cp = copy(a, 4 + j, (*chip, c), sibling)
                cp.start()
                passed.append(cp)
        for a in range(n):
            copy(a, 0, sibling, me).wait_recv()
            for j, chip in enumerate(chips):
                copy(a, 4 + j, (*chip, 1 - c), me).wait_recv()
        for cp in first + passed:
            cp.wait_send()
        for cp in mine:
            cp.wait()

    return pl.pallas_call(
        body, name=name, in_specs=[_ANY] * n, out_specs=[_ANY] * n,
        out_shape=[jax.ShapeDtypeStruct((N_DEV,) + a.shape, a.dtype) for a in arrs],
        scratch_shapes=[pltpu.SemaphoreType.DMA((n, 7)), pltpu.SemaphoreType.DMA((n, 7)), pltpu.SemaphoreType.DMA((n,))],
    )(*arrs)


def _scatter_core(grads, name):
    n = len(grads)

    def body(*refs):
        g_refs, got_refs = refs[:n], refs[n:2 * n]
        send_sems, recv_sems = refs[2 * n:]
        x, y, c = _place()
        sends = [pltpu.make_async_remote_copy(
            src_ref=g_refs[a].at[2 * q + 1 - c], dst_ref=got_refs[a].at[q], send_sem=send_sems.at[a, q],
            recv_sem=recv_sems.at[a, q], device_id=(x, y, 1 - c), device_id_type=MESH_ID) for a in range(n) for q in range(4)]
        for cp in sends:
            cp.start()
        for cp in sends:
            cp.wait()

    return pl.pallas_call(
        body, name=name, in_specs=[_ANY] * n, out_specs=[_ANY] * n,
        out_shape=[jax.ShapeDtypeStruct((4,) + g.shape[1:], g.dtype) for g in grads],
        scratch_shapes=[pltpu.SemaphoreType.DMA((n, 4)), pltpu.SemaphoreType.DMA((n, 4))],
    )(*grads)


def _scatter_chips(parts, name):
    n = len(parts)

    def body(*refs):
        p_refs, o_refs = refs[:n], refs[n:2 * n]
        send_sems, recv_sems = refs[2 * n:]
        x, y, c = _place()
        chips = [(1 - x, y), (x, 1 - y), (1 - x, 1 - y)]
        sends = [pltpu.make_async_remote_copy(
            src_ref=p_refs[a].at[2 * px + py], dst_ref=o_refs[a].at[j], send_sem=send_sems.at[a, j],
            recv_sem=recv_sems.at[a, j], device_id=(px, py, c), device_id_type=MESH_ID)
            for a in range(n) for j, (px, py) in enumerate(chips)]
        for cp in sends:
            cp.start()
        for cp in sends:
            cp.wait()

    return pl.pallas_call(
        body, name=name, in_specs=[_ANY] * n, out_specs=[_ANY] * n,
        out_shape=[jax.ShapeDtypeStruct((3,) + p.shape[1:], p.dtype) for p in parts],
        scratch_shapes=[pltpu.SemaphoreType.DMA((n, 3)), pltpu.SemaphoreType.DMA((n, 3))],
    )(*parts)


def _pick_sum(picked, rest, index, pick_of, out_dtype, name):
    nq, r, cdim = rest.shape
    one = nq == 3
    tr = _row_tile(r, 512, 16)
    tc = 512 if cdim % 512 == 0 else cdim
    grid = (1 if one else nq, r // tr, cdim // tc)

    def body(i_ref, p_ref, r_ref, o_ref):
        acc = p_ref[0].astype(F32)
        if one:
            for j in range(3):
                acc = acc + r_ref[j].astype(F32)
            o_ref[...] = acc.astype(out_dtype)
        else:
            o_ref[0] = (acc + r_ref[0].astype(F32)).astype(out_dtype)

    spec = pltpu.PrefetchScalarGridSpec(
        num_scalar_prefetch=1, grid=grid,
        in_specs=[pl.BlockSpec((1, tr, tc), lambda q, i, j, i_ref: (pick_of(q, i_ref[0]), i, j)),
                  pl.BlockSpec((3, tr, tc), lambda q, i, j, i_ref: (0, i, j)) if one else pl.BlockSpec((1, tr, tc), lambda q, i, j, i_ref: (q, i, j))],
        out_specs=pl.BlockSpec((tr, tc), lambda q, i, j, i_ref: (i, j)) if one else pl.BlockSpec((1, tr, tc), lambda q, i, j, i_ref: (q, i, j)))
    return pl.pallas_call(
        body, name=name, grid_spec=spec,
        out_shape=jax.ShapeDtypeStruct((r, cdim) if one else (nq, r, cdim), out_dtype),
        compiler_params=_params(("parallel", "parallel", "parallel")),
    )(index.astype(jnp.int32).reshape(1), picked, rest)


def _adamw_fn(w, g, m, v):
    m = ADAM_B1 * m + (1.0 - ADAM_B1) * g
    v = ADAM_B2 * v + (1.0 - ADAM_B2) * jnp.square(g)
    m_hat = m / (1.0 - ADAM_B1 ** ADAM_STEP)
    v_hat = v / (1.0 - ADAM_B2 ** ADAM_STEP)
    delta = -ADAM_LR * (m_hat / (jnp.sqrt(v_hat) + ADAM_EPS) + ADAM_WD * w)
    return delta, m, v


def _as2d(a):
    return a.reshape(-1, a.shape[-1])


def _adamw_shard(w, g, m, v, name):
    shape = w.shape
    ins = [_as2d(a) for a in (w, g, m, v)]
    cols = ins[0].shape[1]
    outs = _ew(_adamw_fn, [(a, True) for a in ins], [(cols, F32, "tile")] * 3, 256, name)
    return [o.reshape(shape) for o in outs]


def _adamw_small(ws, gs, ms, vs, name):
    shapes = [w.shape for w in ws]
    flat = lambda a: a.reshape(-1, 128) if a.size % 128 == 0 else a.reshape(1, -1)
    ins = [flat(a) for grp in zip(ws, gs, ms, vs) for a in grp]
    k = len(ws)

    def fn(*vals):
        out = []
        for i in range(k):
            out += list(_adamw_fn(*vals[4 * i:4 * i + 4]))
        return out

    outs = _whole(fn, ins, [(ins[4 * (i // 3)].shape, F32) for i in range(3 * k)], name)
    deltas = [outs[3 * i].reshape(shapes[i]) for i in range(k)]
    new_m = [outs[3 * i + 1].reshape(shapes[i]) for i in range(k)]
    new_v = [outs[3 * i + 2].reshape(shapes[i]) for i in range(k)]
    return deltas, new_m, new_v


def _sum8(stacked, name):
    def fn(a):
        s = a[0:1]
        for i in range(1, N_DEV):
            s = s + a[i:i + 1]
        return s
    w = stacked.shape[1]
    tw = 8192
    if w % tw:
        return _whole(fn, [stacked], [((1, w), F32)], name)[0]

    def body(a_ref, o_ref):
        o_ref[...] = fn(a_ref[...])

    return pl.pallas_call(body, name=name, grid=(w // tw,), in_specs=[pl.BlockSpec((N_DEV, tw), lambda i: (0, i))],
                          out_specs=pl.BlockSpec((1, tw), lambda i: (0, i)), out_shape=jax.ShapeDtypeStruct((1, w), F32))(stacked)


ROWS = 512


def _split_heads(p, nh):
    t = p.shape[0]
    return p.reshape(t, nh, p.shape[1] // nh).transpose(1, 0, 2)


def _merge_heads(p):
    nh, t, d = p.shape
    return p.transpose(1, 0, 2).reshape(t, nh * d)


def _layer_fwd(h, mod, w, small, rope, l):
    sh1, sc1, gt1, sh2, sc2, gt2 = mod
    rope_ret, rope_mla = rope
    t = h.shape[0]
    nm = lambda s: f"l{l}_{s}"
    a1 = _ew(_norm_mod_fn, [(h, True), (small["norm1_g"], False), (sc1, False), (sh1, False)], [(D_MODEL, BF16, "tile")], ROWS, nm("norm1"))[0]
    p_s5 = _mm(a1, w["w_in_t"], "nt", 512, 512, 2048, name=nm("proj_s5"), n=512)
    p_ret = _mm(a1, w["w_in_t"], "nt", 512, 512, 2048, name=nm("proj_ret"), b_off=1, n=1536)
    p_swa = _mm(a1, w["w_in_t"], "nt", 512, 256, 2048, name=nm("proj_swa"), b_off=8, n=768)
    p_mla = _mm(a1, w["w_in_t"][2816:], "nt", 512, 576, 2048, name=nm("proj_mla"))
    b_cat, c_cat, dskip, a_r, a_i = small["s5"]
    bu = _mm(p_s5, b_cat, "nn", 512, 1024, 512, name=nm("s5_bu"))
    st_r, st_i = _s5_scan_fwd(bu, a_r, a_i, nm("s5_scan"))
    ypre = _mm(st_r, c_cat[:S5_WIDTH], "nn", 512, 512, 1024, name=nm("s5_y"), pair=(st_i, c_cat[S5_WIDTH:]))
    z = _ew(_s5_act_fn, [(ypre, True), (p_s5, True), (dskip, False)], [(GROUP_WIDTH, F32, "tile")], ROWS, nm("s5_act"))[0]
    zz = _mm(z, w["glu_w"], "nn", 512, 512, 512, name=nm("s5_zz"))
    y_s5 = _ew(_s5_glu_fn, [(z, True), (zz, True), (small["s5_glu_b"], False)], [(GROUP_WIDTH, BF16, "tile")], ROWS, nm("s5_glu"))[0]
    qk_ret = _split_heads(_rope(p_ret[:, :2 * RET_HEADS * RET_QK], rope_ret, nm("ret_rope")), 2 * RET_HEADS)
    o_ret, y_ret = _ret_fwd(qk_ret, p_ret, small["ret_lgam"], nm("ret"))
    qkv_swa = _split_heads(p_swa, 12)
    o_swa, lse_swa = _swa_fwd(qkv_swa, small["swa_sinks"], nm("swa"))
    y_swa = _merge_heads(o_swa).astype(BF16)
    cq, ckv, kr = p_mla[:, :MLA_Q_RANK], p_mla[:, MLA_Q_RANK:MLA_Q_RANK + MLA_KV_RANK], p_mla[:, MLA_Q_RANK + MLA_KV_RANK:]
    cqn = _ew(_rms_gain_fn, [(cq, True), (small["mla_q_norm"], False)], [(MLA_Q_RANK, BF16, "tile")], ROWS, nm("mla_qnorm"))[0]
    ckvn = _ew(_rms_gain_fn, [(ckv, True), (small["mla_kv_norm"], False)], [(MLA_KV_RANK, BF16, "tile")], ROWS, nm("mla_kvnorm"))[0]
    q_full = _mm(cqn, w["w_uq_t"], "nt", 512, 768, 384, name=nm("mla_q"))
    kv_full = _mm(ckvn, w["w_ukv_t"], "nt", 512, 1024, 128, BF16, name=nm("mla_kv"))
    nq = q_full.shape[1]
    roped = _rope(jnp.concatenate([q_full, kr, jnp.zeros_like(kr)], axis=1), rope_mla, nm("mla_rope"), out_dtype=BF16)
    q4 = roped[:, :nq].reshape(t, MLA_HEADS, MLA_NOPE + MLA_ROPE)
    qn = q4[:, :, :MLA_NOPE].transpose(1, 0, 2)
    rot = jnp.concatenate([q4[:, :, MLA_NOPE:].transpose(1, 0, 2), roped[None, :, nq:nq + MLA_ROPE]], axis=0)
    o_mla, lse_mla = _mla_attend(qn, rot, kv_full, nm("mla"))
    cat = jnp.concatenate([y_s5, y_ret, y_swa, o_mla.astype(BF16)], axis=1)
    mixed = _mm(cat, w["w_out"], "nn", 512, 1024, 2048, name=nm("out_proj"))
    h1 = _ew(_gate_add_fn, [(h, True), (mixed, True), (gt1, False)], [(D_MODEL, F32, "tile")], ROWS, nm("res1"))[0]
    a2 = _ew(_norm_mod_fn, [(h1, True), (small["norm2_g"], False), (sc2, False), (sh2, False)], [(D_MODEL, BF16, "tile")], ROWS, nm("norm2"))[0]
    hid, act = _mm(a2, w["w1_t"], "nt", 1024, 1024, 2048, name=nm("mlp1"), epi=lambda acc: (acc, _relu2_fn(acc)), epi_outs=[F32, BF16])
    mo = _mm(act, w["w2"], "nn", 1024, 1024, 2048, name=nm("mlp2"))
    h2 = _ew(_gate_add_fn, [(h1, True), (mo, True), (gt2, False)], [(D_MODEL, F32, "tile")], ROWS, nm("res2"))[0]
    saved = dict(h=h, a1=a1, p_s5=p_s5, p_ret=p_ret, st_r=st_r, st_i=st_i, ypre=ypre, z=z, zz=zz, qk_ret=qk_ret, o_ret=o_ret,
                 qkv_swa=qkv_swa, o_swa=o_swa, lse_swa=lse_swa, cq=cq, ckv=ckv, cqn=cqn, ckvn=ckvn, qn=qn, rot=rot,
                 kv_full=kv_full, o_mla=o_mla, lse_mla=lse_mla, cat=cat, mixed=mixed, h1=h1, a2=a2, hid=hid, act=act, mo=mo)
    return h2, saved


def _layer_bwd(dh2, mod, w, small, rope, s, l):
    sh1, sc1, gt1, sh2, sc2, gt2 = mod
    rope_ret, rope_mla = rope
    t = dh2.shape[0]
    nm = lambda n: f"l{l}_{n}_bwd"
    gb, gs = {}, {}
    row = (D_MODEL, F32, "acc")
    dmo, dgt2 = _ew(lambda d, y, gt: (d * gt, jnp.sum(d * y, axis=0, keepdims=True)),
                    [(dh2, True), (s["mo"], True), (gt2, False)], [(D_MODEL, BF16, "tile"), row], ROWS, nm("res2"))
    dhid = _mm(dmo, w["w2"], "nt", 1024, 1024, 2048, name=nm("mlp2_x"), epi=lambda acc, x: (acc * 2.0 * jnp.maximum(x, 0.0),),
               epi_ins=[s["hid"]], epi_outs=[BF16])[0]
    gb["w2"] = _mm(s["act"], dmo, "tn", 1024, 1024, 1024, BF16, name=nm("mlp2_w"))
    da2 = _mm(dhid, w["w1_t"], "nn", 1024, 1024, 2048, name=nm("mlp1_x"))
    gb["w1_t"] = _mm(dhid, s["a2"], "tn", 1024, 1024, 1024, BF16, name=nm("mlp1_w"))

    def norm_bwd(hh, g, sc, sh, da, dres):
        dh_, dg, dsc, dsh = _vjp_block(_norm_mod_fn, 4)(hh, g, sc, sh, da)
        return dh_ + dres, dg, dsc, dsh

    dh1, gs["norm2_g"], dsc2, dsh2 = _ew(norm_bwd, [(s["h1"], True), (small["norm2_g"], False), (sc2, False), (sh2, False), (da2, True), (dh2, True)],
                                         [(D_MODEL, F32, "tile"), row, row, row], ROWS, nm("norm2"))
    dmixed, dgt1 = _ew(lambda d, y, gt: (d * gt, jnp.sum(d * y, axis=0, keepdims=True)),
                       [(dh1, True), (s["mixed"], True), (gt1, False)], [(D_MODEL, BF16, "tile"), row], ROWS, nm("res1"))
    dcat = _mm(dmixed, w["w_out"], "nt", 512, 1024, 2048, name=nm("out_proj_x"))
    gb["w_out"] = _mm(s["cat"], dmixed, "tn", 1024, 1024, 1024, BF16, name=nm("out_proj_w"))
    dy_s5, dy_ret, dy_swa, dy_mla = (dcat[:, i * GROUP_WIDTH:(i + 1) * GROUP_WIDTH] for i in range(4))
    b_cat, c_cat, dskip, a_r, a_i = small["s5"]
    gw = (GROUP_WIDTH, F32, "tile")
    gacc = (GROUP_WIDTH, F32, "acc")
    dz_a, dzz, gs["s5_glu_b"] = _ew(_vjp_block(_s5_glu_fn, 3), [(s["z"], True), (s["zz"], True), (small["s5_glu_b"], False), (dy_s5, True)],
                                    [gw, gw, gacc], ROWS, nm("s5_glu"))
    dz_b = _mm(dzz, w["glu_w"], "nt", 512, 512, 512, name=nm("s5_zz_x"))
    gb["glu_w"] = _mm(s["z"], dzz, "tn", 512, 512, 1024, BF16, name=nm("s5_zz_w"))

    def act_bwd(ypre, u, dsk, dza, dzb):
        return _vjp_block(_s5_act_fn, 3)(ypre, u, dsk, dza + dzb)

    dypre, du_a, g_dskip = _ew(act_bwd, [(s["ypre"], True), (s["p_s5"], True), (dskip, False), (dz_a, True), (dz_b, True)],
                               [gw, gw, gacc], ROWS, nm("s5_act"))
    dst = _mm(dypre, c_cat, "nt", 512, 1024, 512, name=nm("s5_y_x"))
    g_ccat = jnp.concatenate([_mm(s["st_r"], dypre, "tn", 1024, 512, 1024, name=nm("s5_y_w_re")),
                              _mm(s["st_i"], dypre, "tn", 1024, 512, 1024, name=nm("s5_y_w_im"))], axis=0)
    dbu_r, dbu_i, g_ar, g_ai = _s5_scan_bwd(dst, s["st_r"], s["st_i"], a_r, a_i, nm("s5_scan"))
    du_b = _mm(dbu_r, b_cat[:, :S5_WIDTH], "nt", 512, 512, 1024, name=nm("s5_bu_x"), pair=(dbu_i, b_cat[:, S5_WIDTH:]))
    g_bcat = jnp.concatenate([_mm(s["p_s5"], dbu_r, "tn", 512, 1024, 1024, name=nm("s5_bu_w_re")),
                              _mm(s["p_s5"], dbu_i, "tn", 512, 1024, 1024, name=nm("s5_bu_w_im"))], axis=1)
    gs["s5"] = (g_bcat, g_ccat, g_dskip, g_ar, g_ai)
    dqk_rot, dk_rot, dv_ret, dg_ret = _ret_bwd(s["qk_ret"], s["p_ret"], s["o_ret"], dy_ret, small["ret_lgam"], nm("ret"))
    dqk = _rope(_merge_heads(jnp.concatenate([dqk_rot, dk_rot], axis=0)), rope_ret, nm("ret_rope"), inverse=True)
    dqkv_swa, gs["swa_sinks"] = _swa_bwd(s["qkv_swa"], s["o_swa"], s["lse_swa"], _split_heads(dy_swa, SWA_HEADS), small["swa_sinks"], nm("swa"))
    delta = _mla_delta(s["o_mla"], dy_mla, nm("mla_delta"))
    dqn, dqr, dkn, dv_mla, dkr_heads = _mla_attend_bwd(s["qn"], s["rot"], s["kv_full"], s["lse_mla"], delta, dy_mla, nm("mla_att"))
    dkv_full = jnp.stack([dkn.reshape(t, MLA_HEADS, MLA_NOPE), dv_mla.reshape(t, MLA_HEADS, MLA_V)], axis=2).reshape(t, 2 * MLA_HEADS * MLA_NOPE)
    dkr_rot = _ew(lambda a, b, c, d: a + b + c + d, [(dkr_heads[i], True) for i in range(MLA_HEADS)], [(MLA_ROPE, F32, "tile")], ROWS, nm("mla_dkr"))[0]
    nq = MLA_HEADS * (MLA_NOPE + MLA_ROPE)
    dq_rot = jnp.concatenate([dqn.transpose(1, 0, 2), dqr.transpose(1, 0, 2)], axis=2).reshape(t, nq)
    droped = _rope(jnp.concatenate([dq_rot, dkr_rot, jnp.zeros_like(dkr_rot)], axis=1), rope_mla, nm("mla_rope"), inverse=True)
    dq_full, dkr = droped[:, :nq], droped[:, nq:nq + MLA_ROPE]
    dcqn = _mm(dq_full, w["w_uq_t"], "nn", 512, 384, 768, name=nm("mla_q_x"))
    gb["w_uq_t"] = _mm(dq_full, s["cqn"], "tn", 768, 384, 1024, BF16, name=nm("mla_q_w"))
    dckvn = _mm(dkv_full, w["w_ukv_t"], "nn", 512, 128, 1024, name=nm("mla_kv_x"))
    gb["w_ukv_t"] = _mm(dkv_full, s["ckvn"], "tn", 1024, 128, 1024, BF16, name=nm("mla_kv_w"))
    dcq, gs["mla_q_norm"] = _ew(_vjp_block(_rms_gain_fn, 2), [(s["cq"], True), (small["mla_q_norm"], False), (dcqn, True)],
                                [(MLA_Q_RANK, F32, "tile"), (MLA_Q_RANK, F32, "acc")], ROWS, nm("mla_qnorm"))
    dckv, gs["mla_kv_norm"] = _ew(_vjp_block(_rms_gain_fn, 2), [(s["ckv"], True), (small["mla_kv_norm"], False), (dckvn, True)],
                                  [(MLA_KV_RANK, F32, "tile"), (MLA_KV_RANK, F32, "acc")], ROWS, nm("mla_kvnorm"))
    du = _ew(lambda a, b: a + b, [(du_a, True), (du_b, True)], [(GROUP_WIDTH, BF16, "tile")], ROWS, nm("s5_du"))[0]
    bf = lambda a: a.astype(BF16)
    dproj = jnp.concatenate([du, bf(dqk), bf(dv_ret), bf(dg_ret), bf(_merge_heads(dqkv_swa)), bf(dcq), bf(dckv), bf(dkr)], axis=1)
    da1 = _mm(dproj, w["w_in_t"], "nn", 512, 1024, N_IN, name=nm("proj_x"))
    gb["w_in_t"] = _mm(dproj, s["a1"], "tn", N_IN, 512, 512, BF16, name=nm("proj_w"))
    dh, gs["norm1_g"], dsc1, dsh1 = _ew(norm_bwd, [(s["h"], True), (small["norm1_g"], False), (sc1, False), (sh1, False), (da1, True), (dh1, True)],
                                        [(D_MODEL, F32, "tile"), row, row, row], ROWS, nm("norm1"))
    dmod = jnp.concatenate([dsh1, dsc1, dgt1, dsh2, dsc2, dgt2], axis=1)
    return dh, gb, gs, dmod


BIG = ("w_in_t", "w1_t", "w_uq_t", "w_ukv_t", "w_out", "w2", "glu_w")
S5_NAMES = ("s5_lambda_re", "s5_lambda_im", "s5_log_dt", "s5_b_re", "s5_b_im", "s5_c_re", "s5_c_im", "s5_d")


def kernel(x, c, norm1_g, norm2_g, ada_w, ada_b, w_in, s5_lambda_re, s5_lambda_im, s5_log_dt, s5_b_re, s5_b_im, s5_c_re, s5_c_im, s5_d, s5_glu_w, s5_glu_b, swa_sinks, mla_q_norm, mla_kv_norm, mla_w_uq, mla_w_ukv, w_out, mlp_w1, mlp_w2, final_norm_g, loss_target, m_norm1_g, m_norm2_g, m_ada_w, m_ada_b, m_w_in, m_s5_lambda_re, m_s5_lambda_im, m_s5_log_dt, m_s5_b_re, m_s5_b_im, m_s5_c_re, m_s5_c_im, m_s5_d, m_s5_glu_w, m_s5_glu_b, m_swa_sinks, m_mla_q_norm, m_mla_kv_norm, m_mla_w_uq, m_mla_w_ukv, m_w_out, m_mlp_w1, m_mlp_w2, m_final_norm_g, v_norm1_g, v_norm2_g, v_ada_w, v_ada_b, v_w_in, v_s5_lambda_re, v_s5_lambda_im, v_s5_log_dt, v_s5_b_re, v_s5_b_im, v_s5_c_re, v_s5_c_im, v_s5_d, v_s5_glu_w, v_s5_glu_b, v_swa_sinks, v_mla_q_norm, v_mla_kv_norm, v_mla_w_uq, v_mla_w_ukv, v_w_out, v_mlp_w1, v_mlp_w2, v_final_norm_g):
    names = ["norm1_g", "norm2_g", "ada_w", "ada_b", "w_in", "s5_lambda_re", "s5_lambda_im", "s5_log_dt", "s5_b_re", "s5_b_im",
             "s5_c_re", "s5_c_im", "s5_d", "s5_glu_w", "s5_glu_b", "swa_sinks", "mla_q_norm", "mla_kv_norm", "mla_w_uq",
             "mla_w_ukv", "w_out", "mlp_w1", "mlp_w2", "final_norm_g"]
    env = locals()
    wts = {n: env[n] for n in names}
    mom = {n: env["m_" + n] for n in names}
    var = {n: env["v_" + n] for n in names}
    t = x.shape[1]
    me = 4 * lax.axis_index("x") + 2 * lax.axis_index("y") + lax.axis_index("c")
    rope = _rope_tables(t)
    ret_lgam = jnp.log1p(-(2.0 ** (-5.0 - jnp.arange(RET_HEADS, dtype=F32))))

    tr = lambda a: a.transpose(0, 2, 1)
    shard = {"w_in_t": tr(w_in), "w1_t": tr(mlp_w1), "w_uq_t": tr(mla_w_uq), "w_ukv_t": tr(mla_w_ukv),
             "w_out": w_out, "w2": mlp_w2, "glu_w": s5_glu_w}
    to_send = [shard[k][l].astype(BF16) for l in range(DEPTH) for k in BIG]
    gathered = _all_gather(to_send + [c], "gather_weights")
    c_all = gathered[-1].reshape(N_DEV, D_MODEL)
    big = [{k: gathered[l * len(BIG) + i].reshape(-1, shard[k].shape[2]) for i, k in enumerate(BIG)} for l in range(DEPTH)]

    c_act = _whole(lambda v: v * jax.nn.sigmoid(v), [c_all], [((N_DEV, D_MODEL), F32)], "cond_silu")[0]
    c_pad = jnp.concatenate([c_act, jnp.zeros((128 - N_DEV, D_MODEL), F32)], axis=0)
    cols = ada_w.shape[2]
    mod_part = [_mm(c_pad, ada_w[l], "nn", 128, cols, 512, name=f"l{l}_mod")[:N_DEV] for l in range(DEPTH)]
    mod_all = _all_gather([jnp.stack(mod_part)], "gather_mod")[0]
    mod_rows = lax.dynamic_index_in_dim(mod_all, me, axis=2, keepdims=False)
    mods = []
    for l in range(DEPTH):
        row = mod_rows[:, l].reshape(1, 6 * D_MODEL) + ada_b[l][None]
        mods.append([row[:, i * D_MODEL:(i + 1) * D_MODEL] for i in range(6)])

    smalls, s5_pulls = [], []
    for l in range(DEPTH):
        s5_ops, pull = jax.vjp(_s5_prep, *[wts[n][l] for n in S5_NAMES])
        s5_pulls.append(pull)
        smalls.append(dict(norm1_g=norm1_g[l][None], norm2_g=norm2_g[l][None], s5=s5_ops, s5_glu_b=s5_glu_b[l][None],
                           swa_sinks=swa_sinks[l], mla_q_norm=mla_q_norm[l][None], mla_kv_norm=mla_kv_norm[l][None], ret_lgam=ret_lgam))
    h = x[0]
    saved = []
    for l in range(DEPTH):
        h, s = _layer_fwd(h, mods[l], big[l], smalls[l], rope, l)
        saved.append(s)

    fg = final_norm_g[None]
    tgt = loss_target[0]
    loss_local = _ew(_final_fn, [(h, True), (fg, False), (tgt, True)], [(1, F32, "acc")], ROWS, "loss")[0]

    def final_bwd(hh, g, tg):
        dh_, dg, _ = _vjp_block(_final_fn, 3)(hh, g, tg, jnp.ones((1, 1), F32))
        return dh_, dg

    dh, g_final = _ew(final_bwd, [(h, True), (fg, False), (tgt, True)], [(D_MODEL, F32, "tile"), (D_MODEL, F32, "acc")], ROWS, "loss_bwd")
    loss = lax.psum(loss_local[0, 0], ("x", "y", "c"))

    g_big, g_small, dmods = [None] * DEPTH, [None] * DEPTH, [None] * DEPTH
    for l in reversed(range(DEPTH)):
        dh, g_big[l], g_small[l], dmods[l] = _layer_bwd(dh, mods[l], big[l], smalls[l], rope, saved[l], l)
    grad_x = dh[None]

    small_parts = []
    for l in range(DEPTH):
        gs = g_small[l]
        s5g = s5_pulls[l](gs["s5"])
        small_parts += [gs["norm1_g"], gs["norm2_g"], *s5g, gs["s5_glu_b"], gs["swa_sinks"], gs["mla_q_norm"], gs["mla_kv_norm"]]
    small_parts += [g_final, *dmods]
    sizes = [int(np.prod(p.shape)) for p in small_parts]
    flat = jnp.concatenate([p.reshape(1, -1) for p in small_parts], axis=1)
    pad = (-flat.shape[1]) % 8192
    flat = jnp.pad(flat, ((0, 0), (0, pad)))
    flat_all = _all_gather([flat], "gather_small_grads")[0].reshape(N_DEV, -1)
    summed = _sum8(flat_all, "sum_small_grads")
    pieces, off = [], 0
    for sz in sizes:
        pieces.append(summed[0, off:off + sz])
        off += sz
    small_names = ["norm1_g", "norm2_g", *S5_NAMES, "s5_glu_b", "swa_sinks", "mla_q_norm", "mla_kv_norm"]
    per_layer = len(small_names)
    grads = {}
    for i, n in enumerate(small_names):
        grads[n] = jnp.stack([pieces[l * per_layer + i].reshape(wts[n].shape[1:]) for l in range(DEPTH)])
    grads["final_norm_g"] = pieces[DEPTH * per_layer]
    grads["ada_b"] = jnp.stack([pieces[DEPTH * per_layer + 1 + l] for l in range(DEPTH)])

    mod_off = sum(sizes[:DEPTH * per_layer + 1])
    dmod_all = flat_all[:, mod_off:mod_off + DEPTH * 6 * D_MODEL].reshape(N_DEV, DEPTH, N_DEV, cols)
    dmod_mine = lax.dynamic_index_in_dim(dmod_all, me, axis=2, keepdims=False).transpose(1, 0, 2)
    dmod_pad = jnp.concatenate([dmod_mine, jnp.zeros((DEPTH, 128 - N_DEV, cols), F32)], axis=1)
    grads["ada_w"] = jnp.stack([_mm(c_pad, dmod_pad[l], "tn", 512, cols, 128, name=f"l{l}_ada_w_grad") for l in range(DEPTH)])

    g_list = [g_big[l][k].reshape(N_DEV, -1, g_big[l][k].shape[1]) for l in range(DEPTH) for k in BIG]
    core, chip = lax.axis_index("c"), 2 * lax.axis_index("x") + lax.axis_index("y")
    got = _scatter_core(g_list, "scatter_core")
    halves = [_pick_sum(g, o, core, lambda q, c: 2 * q + c, BF16, f"core_sum_{i}") for i, (g, o) in enumerate(zip(g_list, got))]
    landed = _scatter_chips(halves, "scatter_chips")
    g_shard = [_pick_sum(h_, o, chip, lambda q, m: m, F32, f"chip_sum_{i}") for i, (h_, o) in enumerate(zip(halves, landed))]

    out_g, out_d, out_m, out_v = dict(grads), {}, {}, {}
    orig = {"w_in_t": "w_in", "w1_t": "mlp_w1", "w_uq_t": "mla_w_uq", "w_ukv_t": "mla_w_ukv", "w_out": "w_out", "w2": "mlp_w2", "glu_w": "s5_glu_w"}
    for i, k in enumerate(BIG):
        out_g[orig[k]] = jnp.stack([g_shard[l * len(BIG) + i] for l in range(DEPTH)])
        if k.endswith("_t"):
            out_g[orig[k]] = tr(out_g[orig[k]])
    for n in [*orig.values(), "ada_w"]:
        out_d[n], out_m[n], out_v[n] = _adamw_shard(wts[n], out_g[n], mom[n], var[n], f"adamw_{n}")
    small_all = small_names + ["ada_b", "final_norm_g"]
    ds, ms, vs = _adamw_small([wts[n] for n in small_all], [grads[n] for n in small_all], [mom[n] for n in small_all],
                              [var[n] for n in small_all], "adamw_small")
    for n, d, m_, v_ in zip(small_all, ds, ms, vs):
        out_d[n], out_m[n], out_v[n] = d, m_, v_
    return (loss, grad_x, *[out_g[n] for n in names], *[out_d[n] for n in names], *[out_m[n] for n in names], *[out_v[n] for n in names])
```

```python
import functools
import math

import numpy as np
import jax
import jax.numpy as jnp
from jax import lax
from jax.experimental import pallas as pl
from jax.experimental.pallas import tpu as pltpu

F32 = jnp.float32
BF16 = jnp.bfloat16
_MXU_DTYPE = jnp.bfloat16

N_DEV = 8
D_MODEL = 2048
DEPTH = 2
GROUP_WIDTH = 512
D_FF = 8192
S5_CH, S5_GROUPS, S5_STATE = 16, 32, 64
S5_WIDTH = S5_GROUPS * S5_STATE
RET_HEADS, RET_QK, RET_V, RET_CHUNK = 4, 64, 128, 128
SWA_HD, SWA_HEADS, SWA_KV_HEADS, WINDOW = 64, 8, 2, 128
MLA_HEADS, MLA_Q_RANK, MLA_KV_RANK, MLA_NOPE, MLA_ROPE, MLA_V = 4, 384, 128, 128, 64, 128
ROPE_BASE = 10000.0
EPS = 1e-6
NEG = -1e30
N_IN = 3392
ADAM_LR, ADAM_B1, ADAM_B2, ADAM_EPS, ADAM_WD, ADAM_STEP = 0.001, 0.9, 0.999, 1e-08, 0.01, 10

VMEM_LIMIT_BYTES = 52 * 1024 * 1024
MESH_ID = pl.DeviceIdType.MESH
_ANY = pl.BlockSpec(memory_space=pl.ANY)
_SMEM = pl.BlockSpec(memory_space=pltpu.SMEM)


def _params(sem):
    return pltpu.CompilerParams(dimension_semantics=sem, vmem_limit_bytes=VMEM_LIMIT_BYTES)


_DIMS = {"nn": (((1,), (0,)), ((), ())), "nt": (((1,), (1,)), ((), ())), "tn": (((0,), (0,)), ((), ()))}


def _dot(a, b, mode="nn"):
    return lax.dot_general(a.astype(_MXU_DTYPE), b.astype(_MXU_DTYPE), _DIMS[mode], preferred_element_type=F32)


def _mm(a, b, mode, tm, tn, tk, out_dtype=F32, name="mm", b_off=0, n=None, pair=None, epi=None, epi_ins=(), epi_outs=None):
    if mode == "tn":
        kdim, m = a.shape
    else:
        m, kdim = a.shape
    if n is None:
        n = b.shape[0] if mode == "nt" else b.shape[1]
    tm, tn, tk = min(tm, m), min(tn, n), min(tk, kdim)
    assert m % tm == 0 and n % tn == 0 and kdim % tk == 0, (name, a.shape, b.shape, tm, tn, tk)
    nk = kdim // tk
    a_spec = pl.BlockSpec((tk, tm), lambda i, j, k: (k, i)) if mode == "tn" else pl.BlockSpec((tm, tk), lambda i, j, k: (i, k))
    if mode == "nt":
        b_spec = pl.BlockSpec((tn, tk), lambda i, j, k: (j + b_off, k))
    else:
        b_spec = pl.BlockSpec((tk, tn), lambda i, j, k: (k, j + b_off))
    o_spec = pl.BlockSpec((tm, tn), lambda i, j, k: (i, j))
    n_mm = 2 if pair is None else 4
    out_dtypes = [out_dtype] if epi is None else list(epi_outs)

    def body(*refs):
        ins, extra = refs[:n_mm], refs[n_mm:n_mm + len(epi_ins)]
        outs = refs[n_mm + len(epi_ins):n_mm + len(epi_ins) + len(out_dtypes)]
        part = _dot(ins[0][...], ins[1][...], mode)
        if pair is not None:
            part = part + _dot(ins[2][...], ins[3][...], mode)

        def finish(acc):
            vals = (acc,) if epi is None else epi(acc, *[r[...] for r in extra])
            for o_ref, v, dt in zip(outs, vals, out_dtypes):
                o_ref[...] = v.astype(dt)

        if nk == 1:
            finish(part)
        else:
            acc_ref = refs[-1]
            k = pl.program_id(2)

            @pl.when(k == 0)
            def _():
                acc_ref[...] = part

            @pl.when(k > 0)
            def _():
                acc_ref[...] += part

            @pl.when(k == nk - 1)
            def _():
                finish(acc_ref[...])

    operands = [a, b] + ([] if pair is None else list(pair)) + list(epi_ins)
    res = pl.pallas_call(
        body, name=name, grid=(m // tm, n // tn, nk),
        in_specs=[a_spec, b_spec] * (n_mm // 2) + [o_spec] * len(epi_ins),
        out_specs=[o_spec] * len(out_dtypes), out_shape=[jax.ShapeDtypeStruct((m, n), dt) for dt in out_dtypes],
        scratch_shapes=[] if nk == 1 else [pltpu.VMEM((tm, tn), F32)],
        compiler_params=_params(("parallel", "parallel", "arbitrary")),
    )(*operands)
    return res[0] if epi is None else res


SUBLANES = 8


def _row_tile(rows, target, mult=SUBLANES):
    best = None
    for cand in range(mult, min(rows, target) + 1, mult):
        if rows % cand == 0:
            best = cand
    return best or rows


def _ew(fn, ins, outs, tt, name):
    t = [a.shape[0] for a, tiled in ins if tiled][0]
    tt = _row_tile(t, tt)
    n_in = len(ins)
    in_specs = [pl.BlockSpec((tt, a.shape[1]), lambda i: (i, 0)) if tiled else pl.BlockSpec(a.shape, lambda i: (0, 0))
                for a, tiled in ins]
    out_specs, out_shapes = [], []
    for w, dt, kind in outs:
        if kind == "tile":
            out_specs.append(pl.BlockSpec((tt, w), lambda i: (i, 0)))
            out_shapes.append(jax.ShapeDtypeStruct((t, w), dt))
        else:
            out_specs.append(pl.BlockSpec((1, w), lambda i: (0, 0)))
            out_shapes.append(jax.ShapeDtypeStruct((1, w), F32))
    has_acc = any(kind == "acc" for _, _, kind in outs)

    def body(*refs):
        vals = fn(*[r[...] for r in refs[:n_in]])
        if not isinstance(vals, (tuple, list)):
            vals = (vals,)
        i = pl.program_id(0)
        for o_ref, v, (w, dt, kind) in zip(refs[n_in:], vals, outs):
            if kind == "tile":
                o_ref[...] = v.astype(dt)
            else:
                @pl.when(i == 0)
                def _(o_ref=o_ref, v=v):
                    o_ref[...] = v.astype(F32)

                @pl.when(i > 0)
                def _(o_ref=o_ref, v=v):
                    o_ref[...] += v.astype(F32)

    res = pl.pallas_call(
        body, name=name, grid=(t // tt,), in_specs=in_specs, out_specs=out_specs, out_shape=out_shapes,
        compiler_params=_params(("arbitrary" if has_acc else "parallel",)),
    )(*[a for a, _ in ins])
    return res


def _whole(fn, ins, outs, name):
    def body(*refs):
        vals = fn(*[r[...] for r in refs[:len(ins)]])
        if not isinstance(vals, (tuple, list)):
            vals = (vals,)
        for o_ref, v in zip(refs[len(ins):], vals):
            o_ref[...] = v.astype(o_ref.dtype)

    return pl.pallas_call(body, name=name, out_shape=[jax.ShapeDtypeStruct(s, dt) for s, dt in outs])(*ins)


def _rms(x):
    return x * lax.rsqrt(jnp.mean(x * x, axis=-1, keepdims=True) + EPS)


def _norm_mod_fn(h, g, sc, sh):
    return (_rms(h) * g) * (1.0 + sc) + sh


def _rms_gain_fn(x, g):
    return _rms(x) * g


def _gate_add_fn(h, y, gt):
    return h + gt * y


def _relu2_fn(x):
    return jnp.square(jnp.maximum(x, 0.0))


def _s5_act_fn(ypre, u, dskip):
    return jax.nn.gelu(ypre + dskip * u)


def _s5_glu_fn(z, zz, b):
    return z * jax.nn.sigmoid(zz + b)


def _ret_gate_fn(o, g):
    return _rms(o) * (g * jax.nn.sigmoid(g))


def _final_fn(h, g, tgt):
    err = _rms(h) * g - tgt
    return 0.5 * jnp.sum(jnp.mean(err * err, axis=-1, keepdims=True), axis=0, keepdims=True)


def _vjp_block(fn, n_args):
    def bwd(*vals):
        _, pull = jax.vjp(fn, *vals[:n_args])
        return pull(vals[n_args])
    return bwd


def _rope_tables(t):
    d = RET_QK
    inv = ROPE_BASE ** (-jnp.arange(0, d, 2, dtype=F32) / d)
    ang = jnp.arange(t, dtype=F32)[:, None] * inv[None, :]
    cos, sin = jnp.cos(ang), jnp.sin(ang)
    cos2, sin2 = jnp.concatenate([cos, cos], -1), jnp.concatenate([-sin, sin], -1)
    ret = (jnp.tile(cos2, (1, 8)), jnp.tile(sin2, (1, 8)))
    one, zero = jnp.ones((t, MLA_NOPE), F32), jnp.zeros((t, MLA_NOPE), F32)
    mla_c = jnp.concatenate([jnp.tile(jnp.concatenate([one, cos2], -1), (1, MLA_HEADS)), cos2, one[:, :d]], -1)
    mla_s = jnp.concatenate([jnp.tile(jnp.concatenate([zero, sin2], -1), (1, MLA_HEADS)), sin2, zero[:, :d]], -1)
    return ret, (mla_c, mla_s)


def _rope_fn(x, c, s, sign):
    w = x.shape[1]
    lane = lax.broadcasted_iota(jnp.int32, x.shape, 1)
    swapped = jnp.where((lane & 63) < 32, pltpu.roll(x, w - 32, 1), pltpu.roll(x, 32, 1))
    return x * c + swapped * (sign * s)


def _rope(x, tables, name, inverse=False, out_dtype=F32):
    c, s = tables
    fn = functools.partial(_rope_fn, sign=-1.0 if inverse else 1.0)
    return _ew(fn, [(x, True), (c, True), (s, True)], [(x.shape[1], out_dtype, "tile")], ROWS, name)[0]


SCAN_ROWS, SCAN_LANES = 256, 512


def _cmul(ar, ai, br, bi):
    return ar * br - ai * bi, ar * bi + ai * br


def _scan_chunk(xr, xi, ar, ai, reverse):
    rows = xr.shape[0]
    row = lax.broadcasted_iota(jnp.int32, xr.shape, 0)
    pr, pi = ar, ai
    k = 1
    while k < rows:
        if reverse:
            sr, si = pltpu.roll(xr, rows - k, 0), pltpu.roll(xi, rows - k, 0)
            keep = row < rows - k
        else:
            sr, si = pltpu.roll(xr, k, 0), pltpu.roll(xi, k, 0)
            keep = row >= k
        tr, ti = _cmul(pr, pi, sr, si)
        xr = xr + jnp.where(keep, tr, 0.0)
        xi = xi + jnp.where(keep, ti, 0.0)
        pr, pi = _cmul(pr, pi, pr, pi)
        k *= 2
    return xr, xi


def _s5_scan_fwd(bu, a_r, a_i, name):
    t = bu.shape[0]
    rows = min(SCAN_ROWS, t)
    nl = S5_WIDTH // SCAN_LANES

    def body(br_ref, bi_ref, ar_ref, ai_ref, or_ref, oi_ref, cr_ref, ci_ref):
        i = pl.program_id(1)

        @pl.when(i == 0)
        def _():
            cr_ref[...] = jnp.zeros_like(cr_ref)
            ci_ref[...] = jnp.zeros_like(ci_ref)

        ar, ai = ar_ref[...], ai_ref[...]
        xr, xi = br_ref[...], bi_ref[...]
        inr, ini = _cmul(ar, ai, cr_ref[...], ci_ref[...])
        first = lax.broadcasted_iota(jnp.int32, xr.shape, 0) == 0
        xr = xr + jnp.where(first, inr, 0.0)
        xi = xi + jnp.where(first, ini, 0.0)
        xr, xi = _scan_chunk(xr, xi, ar, ai, reverse=False)
        or_ref[...] = xr
        oi_ref[...] = xi
        cr_ref[...] = xr[rows - 1:rows, :]
        ci_ref[...] = xi[rows - 1:rows, :]

    blk = lambda off: pl.BlockSpec((rows, SCAN_LANES), lambda j, i: (i, j + off))
    par = pl.BlockSpec((1, SCAN_LANES), lambda j, i: (0, j))
    st_r, st_i = pl.pallas_call(
        body, name=name, grid=(nl, t // rows), in_specs=[blk(0), blk(nl), par, par], out_specs=[blk(0), blk(0)],
        out_shape=[jax.ShapeDtypeStruct((t, S5_WIDTH), F32)] * 2,
        scratch_shapes=[pltpu.VMEM((1, SCAN_LANES), F32)] * 2, compiler_params=_params(("parallel", "arbitrary")),
    )(bu, bu, a_r, a_i)
    return st_r, st_i


def _s5_scan_bwd(dst, st_r, st_i, a_r, a_i, name):
    t = dst.shape[0]
    rows = min(SCAN_ROWS, t)
    nl = S5_WIDTH // SCAN_LANES
    nc = t // rows

    def body(dr_ref, di_ref, xr_ref, xi_ref, ar_ref, ai_ref, gr_ref, gi_ref, dar_ref, dai_ref, cr_ref, ci_ref):
        i = pl.program_id(1)

        @pl.when(i == 0)
        def _():
            cr_ref[...] = jnp.zeros_like(cr_ref)
            ci_ref[...] = jnp.zeros_like(ci_ref)
            dar_ref[...] = jnp.zeros_like(dar_ref)
            dai_ref[...] = jnp.zeros_like(dai_ref)

        ar, ai = ar_ref[...], ai_ref[...]
        cr, ci = cr_ref[...], ci_ref[...]
        gr, gi = dr_ref[...], di_ref[...]
        row = lax.broadcasted_iota(jnp.int32, gr.shape, 0)
        last = row == rows - 1
        inr, ini = _cmul(ar, -ai, cr, ci)
        gr = gr + jnp.where(last, inr, 0.0)
        gi = gi + jnp.where(last, ini, 0.0)
        gr, gi = _scan_chunk(gr, gi, ar, -ai, reverse=True)
        gr_ref[...] = gr
        gi_ref[...] = gi
        nr = jnp.where(last, cr, pltpu.roll(gr, rows - 1, 0))
        ni = jnp.where(last, ci, pltpu.roll(gi, rows - 1, 0))
        xr, xi = xr_ref[...], xi_ref[...]
        dar_ref[...] += jnp.sum(nr * xr + ni * xi, axis=0, keepdims=True)
        dai_ref[...] += jnp.sum(ni * xr - nr * xi, axis=0, keepdims=True)
        cr_ref[...] = gr[0:1, :]
        ci_ref[...] = gi[0:1, :]

    blk = lambda off: pl.BlockSpec((rows, SCAN_LANES), lambda j, i: (nc - 1 - i, j + off))
    par = pl.BlockSpec((1, SCAN_LANES), lambda j, i: (0, j))
    return pl.pallas_call(
        body, name=name, grid=(nl, nc), in_specs=[blk(0), blk(nl), blk(0), blk(0), par, par],
        out_specs=[blk(0), blk(0), par, par],
        out_shape=[jax.ShapeDtypeStruct((t, S5_WIDTH), F32)] * 2 + [jax.ShapeDtypeStruct((1, S5_WIDTH), F32)] * 2,
        scratch_shapes=[pltpu.VMEM((1, SCAN_LANES), F32)] * 2, compiler_params=_params(("parallel", "arbitrary")),
    )(dst, dst, st_r, st_i, a_r, a_i)


def _s5_prep(lam_re, lam_im, log_dt, b_re, b_im, c_re, c_im, d_skip):
    dt = jnp.exp(log_dt)[:, None]
    mag = jnp.exp(lam_re * dt)
    ar, ai = mag * jnp.cos(lam_im * dt), mag * jnp.sin(lam_im * dt)
    den = lam_re * lam_re + lam_im * lam_im
    cr = ((ar - 1.0) * lam_re + ai * lam_im) / den
    ci = (ai * lam_re - (ar - 1.0) * lam_im) / den
    bbar_r = cr[..., None] * b_re - ci[..., None] * b_im
    bbar_i = cr[..., None] * b_im + ci[..., None] * b_re
    eye = jnp.eye(S5_GROUPS, dtype=F32)

    def bdiag(m):
        g, a, b = m.shape
        return (eye[:, None, :, None] * m[:, :, None, :]).reshape(g * a, g * b)

    b_cat = jnp.concatenate([bdiag(bbar_r.transpose(0, 2, 1)), bdiag(bbar_i.transpose(0, 2, 1))], axis=1)
    c_cat = jnp.concatenate([bdiag(c_re.transpose(0, 2, 1)), -bdiag(c_im.transpose(0, 2, 1))], axis=0)
    return b_cat, c_cat, d_skip.reshape(1, GROUP_WIDTH), ar.reshape(1, S5_WIDTH), ai.reshape(1, S5_WIDTH)


def _ret_consts(lgam):
    c = RET_CHUNK
    r = lax.broadcasted_iota(jnp.int32, (c, c), 0)
    m = lax.broadcasted_iota(jnp.int32, (c, c), 1)
    rel = (r - m).astype(F32)
    decay = jnp.where(rel >= 0, jnp.exp(lgam * jnp.maximum(rel, 0.0)), 0.0)
    idx = lax.broadcasted_iota(jnp.int32, (c, 1), 0).astype(F32)
    zeta = jnp.exp(lgam * (c - 1.0 - idx))
    xi = jnp.exp(lgam * (idx + 1.0))
    return decay, zeta, xi, jnp.exp(lgam * c)


def _ret_specs(t):
    qk = lambda off: pl.BlockSpec((1, t, RET_QK), lambda h: (h + off, 0, 0))
    col = lambda off: pl.BlockSpec((t, RET_V), lambda h: (0, h + off))
    return qk, col


def _ret_fwd(qk, p_ret, lgam, name):
    t = qk.shape[1]
    nck = t // RET_CHUNK
    qk_spec, col = _ret_specs(t)

    def body(lg_ref, q_ref, k_ref, v_ref, g_ref, o_ref, y_ref):
        decay, zeta, xi, gam = _ret_consts(lg_ref[pl.program_id(0)])

        def step(n, state):
            sl = pl.ds(pl.multiple_of(n * RET_CHUNK, RET_CHUNK), RET_CHUNK)
            q, k, v = q_ref[0, sl, :], k_ref[0, sl, :] * (RET_QK ** -0.5), v_ref[sl, :]
            s = _dot(q, k, "nt") * decay
            o = _dot(s, v) + _dot(q, state) * xi
            o_ref[sl, :] = o
            y_ref[sl, :] = _ret_gate_fn(o, g_ref[sl, :]).astype(y_ref.dtype)
            return gam * state + _dot(k, zeta * v, "tn")

        lax.fori_loop(0, nck, step, jnp.zeros((RET_QK, RET_V), F32))

    return pl.pallas_call(
        body, name=name, grid=(RET_HEADS,), in_specs=[_SMEM, qk_spec(0), qk_spec(RET_HEADS), col(4), col(8)],
        out_specs=[col(0), col(0)],
        out_shape=[jax.ShapeDtypeStruct((t, GROUP_WIDTH), F32), jax.ShapeDtypeStruct((t, GROUP_WIDTH), BF16)],
        compiler_params=_params(("parallel",)),
    )(lgam, qk, qk, p_ret, p_ret)


def _ret_bwd(qk, p_ret, o_all, dy, lgam, name):
    t = qk.shape[1]
    nck = t // RET_CHUNK
    qk_spec, col = _ret_specs(t)
    gate_bwd = _vjp_block(_ret_gate_fn, 2)

    def body(lg_ref, q_ref, k_ref, v_ref, g_ref, o_ref, dy_ref, dq_ref, dk_ref, dv_ref, dg_ref, st_ref):
        decay, zeta, xi, gam = _ret_consts(lg_ref[pl.program_id(0)])
        scale = RET_QK ** -0.5

        def fstep(n, state):
            sl = pl.ds(pl.multiple_of(n * RET_CHUNK, RET_CHUNK), RET_CHUNK)
            st_ref[n] = state
            return gam * state + _dot(k_ref[0, sl, :] * scale, zeta * v_ref[sl, :], "tn")

        lax.fori_loop(0, nck, fstep, jnp.zeros((RET_QK, RET_V), F32))

        def bstep(r, grad_state):
            n = nck - 1 - r
            sl = pl.ds(pl.multiple_of(n * RET_CHUNK, RET_CHUNK), RET_CHUNK)
            q, k, v = q_ref[0, sl, :], k_ref[0, sl, :] * scale, v_ref[sl, :]
            d_o, dg = gate_bwd(o_ref[sl, :], g_ref[sl, :], dy_ref[sl, :])
            dg_ref[sl, :] = dg
            s = _dot(q, k, "nt") * decay
            ds = _dot(d_o, v, "nt") * decay
            xdo = xi * d_o
            dq_ref[0, sl, :] = _dot(ds, k) + _dot(xdo, st_ref[n], "nt")
            dk_ref[0, sl, :] = (_dot(ds, q, "tn") + _dot(zeta * v, grad_state, "nt")) * scale
            dv_ref[sl, :] = _dot(s, d_o, "tn") + zeta * _dot(k, grad_state)
            return gam * grad_state + _dot(q, xdo, "tn")

        lax.fori_loop(0, nck, bstep, jnp.zeros((RET_QK, RET_V), F32))

    hd = pl.BlockSpec((1, t, RET_QK), lambda h: (h, 0, 0))
    return pl.pallas_call(
        body, name=name, grid=(RET_HEADS,),
        in_specs=[_SMEM, qk_spec(0), qk_spec(RET_HEADS), col(4), col(8), col(0), col(0)],
        out_specs=[hd, hd, col(0), col(0)],
        out_shape=[jax.ShapeDtypeStruct((RET_HEADS, t, RET_QK), F32)] * 2 + [jax.ShapeDtypeStruct((t, GROUP_WIDTH), F32)] * 2,
        scratch_shapes=[pltpu.VMEM((nck, RET_QK, RET_V), F32)], compiler_params=_params(("parallel",)),
    )(lgam, qk, qk, p_ret, p_ret, o_all, dy)


SWA_GROUP = SWA_HEADS // SWA_KV_HEADS
SWA_SCALE = SWA_HD ** -0.5


def _swa_mask(n):
    rows = SWA_GROUP * WINDOW
    r = lax.broadcasted_iota(jnp.int32, (rows, 2 * WINDOW), 0) & (WINDOW - 1)
    j = lax.broadcasted_iota(jnp.int32, (rows, 2 * WINDOW), 1)
    dist = r + WINDOW - j
    return (dist >= 0) & (dist < WINDOW) & (n * WINDOW + j - WINDOW >= 0)


def _swa_sink_rows(sink_ref, kv):
    row = lax.broadcasted_iota(jnp.int32, (SWA_GROUP * WINDOW, 1), 0)
    sink = jnp.zeros((SWA_GROUP * WINDOW, 1), F32)
    for g in range(SWA_GROUP):
        sink = jnp.where(row >= g * WINDOW, sink_ref[kv * SWA_GROUP + g], sink)
    return sink


def _swa_pad_keys(n, k_ref, v_ref, kp_ref, vp_ref):
    @pl.when(n == 0)
    def _():
        zero = jnp.zeros((WINDOW, SWA_HD), F32)
        kp_ref[0:WINDOW, :] = zero
        vp_ref[0:WINDOW, :] = zero
        kp_ref[WINDOW:, :] = k_ref[0]
        vp_ref[WINDOW:, :] = v_ref[0]


def _swa_specs(t):
    blk = lambda w: pl.BlockSpec((SWA_GROUP, WINDOW, w), lambda kv, n: (kv, n, 0))
    kspec = lambda off: pl.BlockSpec((1, t, SWA_HD), lambda kv, n: (SWA_HEADS + off + kv, 0, 0))
    return blk, kspec


def _swa_fwd(qkv, sinks, name):
    t = qkv.shape[1]
    rows = SWA_GROUP * WINDOW
    blk, kspec = _swa_specs(t)

    def body(sink_ref, q_ref, k_ref, v_ref, o_ref, lse_ref, kp_ref, vp_ref):
        kv, n = pl.program_id(0), pl.program_id(1)
        _swa_pad_keys(n, k_ref, v_ref, kp_ref, vp_ref)
        win = pl.ds(pl.multiple_of(n * WINDOW, WINDOW), 2 * WINDOW)
        sink = _swa_sink_rows(sink_ref, kv)
        s = _dot(q_ref[...].reshape(rows, SWA_HD), kp_ref[win, :], "nt") * SWA_SCALE
        s = jnp.where(_swa_mask(n), s, NEG)
        m = jnp.maximum(jnp.max(s, axis=-1, keepdims=True), sink)
        p = jnp.exp(s - m)
        den = jnp.sum(p, axis=-1, keepdims=True) + jnp.exp(sink - m)
        o_ref[...] = _dot(p / den, vp_ref[win, :]).reshape(SWA_GROUP, WINDOW, SWA_HD)
        lse_ref[...] = (m + jnp.log(den)).reshape(SWA_GROUP, WINDOW, 1)

    return pl.pallas_call(
        body, name=name, grid=(SWA_KV_HEADS, t // WINDOW),
        in_specs=[_SMEM, blk(SWA_HD), kspec(0), kspec(SWA_KV_HEADS)], out_specs=[blk(SWA_HD), blk(1)],
        out_shape=[jax.ShapeDtypeStruct((SWA_HEADS, t, SWA_HD), F32), jax.ShapeDtypeStruct((SWA_HEADS, t, 1), F32)],
        scratch_shapes=[pltpu.VMEM((t + WINDOW, SWA_HD), F32)] * 2, compiler_params=_params(("parallel", "arbitrary")),
    )(sinks, qkv, qkv, qkv)


def _swa_bwd(qkv, o, lse, d_o, sinks, name):
    t = qkv.shape[1]
    nb = t // WINDOW
    rows = SWA_GROUP * WINDOW
    blk, kspec = _swa_specs(t)

    def body(sink_ref, q_ref, k_ref, v_ref, o_ref, lse_ref, do_ref, dq_ref, dk_ref, dv_ref, dsink_ref,
             kp_ref, vp_ref, dkp_ref, dvp_ref):
        kv, n = pl.program_id(0), pl.program_id(1)
        _swa_pad_keys(n, k_ref, v_ref, kp_ref, vp_ref)

        @pl.when(n == 0)
        def _():
            dkp_ref[...] = jnp.zeros_like(dkp_ref)
            dvp_ref[...] = jnp.zeros_like(dvp_ref)
            dsink_ref[...] = jnp.zeros_like(dsink_ref)

        win = pl.ds(pl.multiple_of(n * WINDOW, WINDOW), 2 * WINDOW)
        sink = _swa_sink_rows(sink_ref, kv)
        q, dout = q_ref[...].reshape(rows, SWA_HD), do_ref[...].reshape(rows, SWA_HD)
        lse_n = lse_ref[...].reshape(rows, 1)
        s = _dot(q, kp_ref[win, :], "nt") * SWA_SCALE
        s = jnp.where(_swa_mask(n), s, NEG)
        p = jnp.exp(s - lse_n)
        delta = jnp.sum(dout * o_ref[...].reshape(rows, SWA_HD), axis=-1, keepdims=True)
        ds = p * (_dot(dout, vp_ref[win, :], "nt") - delta)
        dq_ref[...] = (_dot(ds, kp_ref[win, :]) * SWA_SCALE).reshape(SWA_GROUP, WINDOW, SWA_HD)
        dkp_ref[win, :] += _dot(ds, q, "tn") * SWA_SCALE
        dvp_ref[win, :] += _dot(p, dout, "tn")
        term = jnp.exp(sink - lse_n) * delta
        head = lax.broadcasted_iota(jnp.int32, (SWA_GROUP, 128), 0)
        acc = jnp.zeros((SWA_GROUP, 128), F32)
        for g in range(SWA_GROUP):
            acc = jnp.where(head == g, jnp.sum(term[g * WINDOW:(g + 1) * WINDOW], axis=0, keepdims=True), acc)
        dsink_ref[0] -= acc

        @pl.when(n == nb - 1)
        def _():
            dk_ref[0] = dkp_ref[WINDOW:, :]
            dv_ref[0] = dvp_ref[WINDOW:, :]

    kout = pl.BlockSpec((1, t, SWA_HD), lambda kv, n: (kv, 0, 0))
    dq, dk, dv, dsink = pl.pallas_call(
        body, name=name, grid=(SWA_KV_HEADS, nb),
        in_specs=[_SMEM, blk(SWA_HD), kspec(0), kspec(SWA_KV_HEADS), blk(SWA_HD), blk(1), blk(SWA_HD)],
        out_specs=[blk(SWA_HD), kout, kout, pl.BlockSpec((1, SWA_GROUP, 128), lambda kv, n: (kv, 0, 0))],
        out_shape=[jax.ShapeDtypeStruct((SWA_HEADS, t, SWA_HD), F32), jax.ShapeDtypeStruct((SWA_KV_HEADS, t, SWA_HD), F32),
                   jax.ShapeDtypeStruct((SWA_KV_HEADS, t, SWA_HD), F32), jax.ShapeDtypeStruct((SWA_KV_HEADS, SWA_GROUP, 128), F32)],
        scratch_shapes=[pltpu.VMEM((t + WINDOW, SWA_HD), F32)] * 4, compiler_params=_params(("parallel", "arbitrary")),
    )(sinks, qkv, qkv, qkv, o, lse, d_o)
    return jnp.concatenate([dq, dk, dv], axis=0), dsink[:, :, 0].reshape(SWA_HEADS)


MLA_SCALE = (MLA_NOPE + MLA_ROPE) ** -0.5
MLA_TILE = 256


def _mla_diag(s):
    r = lax.broadcasted_iota(jnp.int32, s.shape, 0)
    c = lax.broadcasted_iota(jnp.int32, s.shape, 1)
    return jnp.where(c <= r, s, NEG)


def _mla_specs(t, tile):
    whole = lambda w, off: pl.BlockSpec((t, w), lambda h, i: (0, 2 * h + off))
    head = lambda w: pl.BlockSpec((1, t, w), lambda h, i: (h, 0, 0))
    key_rope = pl.BlockSpec((1, t, MLA_ROPE), lambda h, i: (MLA_HEADS, 0, 0))
    tile_of = lambda w: pl.BlockSpec((1, tile, w), lambda h, i: (h, i, 0))
    return whole, head, key_rope, tile_of


def _mla_attend(qn, rot, kv, name):
    t = qn.shape[1]
    tile = min(MLA_TILE, t)
    whole, head, key_rope, tile_of = _mla_specs(t, tile)

    def body(qn_ref, qr_ref, kn_ref, kr_ref, v_ref, o_ref, lse_ref, m_ref, l_ref, acc_ref):
        i = pl.program_id(1)
        qn_b, qr_b = qn_ref[0], qr_ref[0]

        def rows(j):
            return pl.ds(pl.multiple_of(j * tile, tile), tile)

        def scores(j):
            return (_dot(qn_b, kn_ref[rows(j), :], "nt") + _dot(qr_b, kr_ref[0, rows(j), :], "nt")) * MLA_SCALE

        def update(s, j):
            m_old = m_ref[...]
            m_new = jnp.maximum(m_old, jnp.max(s, axis=-1, keepdims=True))
            alpha = jnp.exp(m_old - m_new)
            p = jnp.exp(s - m_new)
            l_ref[...] = alpha * l_ref[...] + jnp.sum(p, axis=-1, keepdims=True)
            acc_ref[...] = alpha * acc_ref[...] + _dot(p, v_ref[rows(j), :])
            m_ref[...] = m_new

        m_ref[...] = jnp.full_like(m_ref, NEG)
        l_ref[...] = jnp.zeros_like(l_ref)
        acc_ref[...] = jnp.zeros_like(acc_ref)

        def step(j, s_cur):
            s_next = scores(j + 1)
            update(s_cur, j)
            return s_next

        s_diag = lax.fori_loop(0, i, step, scores(0))
        update(_mla_diag(s_diag), i)
        o_ref[...] = acc_ref[...] / l_ref[...]
        lse_ref[0] = m_ref[...] + jnp.log(l_ref[...])

    return pl.pallas_call(
        body, name=name, grid=(MLA_HEADS, t // tile),
        in_specs=[tile_of(MLA_NOPE), tile_of(MLA_ROPE), whole(MLA_NOPE, 0), key_rope, whole(MLA_V, 1)],
        out_specs=[pl.BlockSpec((tile, MLA_V), lambda h, i: (i, h)), tile_of(1)],
        out_shape=[jax.ShapeDtypeStruct((t, GROUP_WIDTH), F32), jax.ShapeDtypeStruct((MLA_HEADS, t, 1), F32)],
        scratch_shapes=[pltpu.VMEM((tile, 1), F32), pltpu.VMEM((tile, 1), F32), pltpu.VMEM((tile, MLA_V), F32)],
        compiler_params=_params(("parallel", "parallel")),
    )(qn, rot, kv, rot, kv)


def _mla_delta(o, d_o, name):
    t = o.shape[0]
    tt = min(ROWS, t)

    def body(o_ref, do_ref, d_ref):
        d_ref[0] = jnp.sum(o_ref[...] * do_ref[...], axis=-1, keepdims=True)

    blk = pl.BlockSpec((tt, MLA_V), lambda h, i: (i, h))
    return pl.pallas_call(
        body, name=name, grid=(MLA_HEADS, t // tt), in_specs=[blk, blk],
        out_specs=pl.BlockSpec((1, tt, 1), lambda h, i: (h, i, 0)), out_shape=jax.ShapeDtypeStruct((MLA_HEADS, t, 1), F32),
        compiler_params=_params(("parallel", "parallel")),
    )(o, d_o)


def _mla_attend_bwd(qn, rot, kv, lse, delta, d_o, name):
    t = qn.shape[1]
    tile = min(MLA_TILE, t)
    nt = t // tile
    whole, head, key_rope, tile_of = _mla_specs(t, tile)

    def body(qn_ref, qr_ref, kn_ref, kr_ref, v_ref, lse_ref, dl_ref, do_ref, dqn_ref, dqr_ref, dkn_ref, dv_ref, dkr_ref):
        j = pl.program_id(1)

        @pl.when(j == 0)
        def _():
            dqn_ref[...] = jnp.zeros_like(dqn_ref)
            dqr_ref[...] = jnp.zeros_like(dqr_ref)

        dkn_ref[...] = jnp.zeros_like(dkn_ref)
        dv_ref[...] = jnp.zeros_like(dv_ref)
        dkr_ref[...] = jnp.zeros_like(dkr_ref)
        kn_b, kr_b, v_b = kn_ref[...], kr_ref[0], v_ref[...]

        def block(i, diagonal):
            sl = pl.ds(pl.multiple_of(i * tile, tile), tile)
            qn_b, qr_b, dout = qn_ref[0, sl, :], qr_ref[0, sl, :], do_ref[sl, :]
            s = (_dot(qn_b, kn_b, "nt") + _dot(qr_b, kr_b, "nt")) * MLA_SCALE
            if diagonal:
                s = _mla_diag(s)
            p = jnp.exp(s - lse_ref[0, sl, :])
            ds = p * (_dot(dout, v_b, "nt") - dl_ref[0, sl, :]) * MLA_SCALE
            dv_ref[...] += _dot(p, dout, "tn")
            dkn_ref[...] += _dot(ds, qn_b, "tn")
            dkr_ref[0] += _dot(ds, qr_b, "tn")
            dqn_ref[0, sl, :] += _dot(ds, kn_b)
            dqr_ref[0, sl, :] += _dot(ds, kr_b)

        block(j, True)

        def step(i, carry):
            block(i, False)
            return carry

        lax.fori_loop(j + 1, nt, step, 0)

    key_tile = lambda w, off: pl.BlockSpec((tile, w), lambda h, j: (j, 2 * h + off))
    out_tile = pl.BlockSpec((tile, MLA_V), lambda h, j: (j, h))
    return pl.pallas_call(
        body, name=name, grid=(MLA_HEADS, nt),
        in_specs=[head(MLA_NOPE), head(MLA_ROPE), key_tile(MLA_NOPE, 0), pl.BlockSpec((1, tile, MLA_ROPE), lambda h, j: (MLA_HEADS, j, 0)),
                  key_tile(MLA_V, 1), head(1), head(1), pl.BlockSpec((t, MLA_V), lambda h, j: (0, h))],
        out_specs=[head(MLA_NOPE), head(MLA_ROPE), out_tile, out_tile, tile_of(MLA_ROPE)],
        out_shape=[jax.ShapeDtypeStruct((MLA_HEADS, t, MLA_NOPE), F32), jax.ShapeDtypeStruct((MLA_HEADS, t, MLA_ROPE), F32),
                   jax.ShapeDtypeStruct((t, MLA_HEADS * MLA_NOPE), F32), jax.ShapeDtypeStruct((t, MLA_HEADS * MLA_V), F32),
                   jax.ShapeDtypeStruct((MLA_HEADS, t, MLA_ROPE), F32)],
        compiler_params=_params(("parallel", "arbitrary")),
    )(qn, rot, kv, rot, kv, lse, delta, d_o)


def _place():
    return lax.axis_index("x"), lax.axis_index("y"), lax.axis_index("c")


def _all_gather(arrs, name):
    n = len(arrs)

    def body(*refs):
        x_refs, o_refs = refs[:n], refs[n:2 * n]
        send_sems, recv_sems, local_sems = refs[2 * n:]
        x, y, c = _place()
        me, sibling = (x, y, c), (x, y, 1 - c)
        chips = [(1 - x, y), (x, 1 - y), (1 - x, 1 - y)]

        def slot(a, p):
            return o_refs[a].at[4 * p[0] + 2 * p[1] + p[2]]

        def copy(a, k, block, to, src=None):
            return pltpu.make_async_remote_copy(
                src_ref=slot(a, block) if src is None else src, dst_ref=slot(a, block),
                send_sem=send_sems.at[a, k], recv_sem=recv_sems.at[a, k], device_id=to, device_id_type=MESH_ID)

        mine = [pltpu.make_async_copy(x_refs[a], slot(a, me), local_sems.at[a]) for a in range(n)]
        for cp in mine:
            cp.start()
        first = []
        for a in range(n):
            first.append(copy(a, 0, me, sibling, src=x_refs[a]))
            first += [copy(a, 1 + j, me, (*chip, c), src=x_refs[a]) for j, chip in enumerate(chips)]
        for cp in first:
            cp.start()
        passed = []
        for j, chip in enumerate(chips):
            for a in range(n):
                copy(a, 1 + j, (*chip, c), me).wait_recv()
                cp = copy(a, 4 + j, (*chip, c), sibling)
                cp.start()
                passed.append(cp)
        for a in range(n):
            copy(a, 0, sibling, me).wait_recv()
            for j, chip in enumerate(chips):
                copy(a, 4 + j, (*chip, 1 - c), me).wait_recv()
        for cp in first + passed:
            cp.wait_send()
        for cp in mine:
            cp.wait()

    return pl.pallas_call(
        body, name=name, in_specs=[_ANY] * n, out_specs=[_ANY] * n,
        out_shape=[jax.ShapeDtypeStruct((N_DEV,) + a.shape, a.dtype) for a in arrs],
        scratch_shapes=[pltpu.SemaphoreType.DMA((n, 7)), pltpu.SemaphoreType.DMA((n, 7)), pltpu.SemaphoreType.DMA((n,))],
    )(*arrs)


def _scatter_core(grads, name):
    n = len(grads)

    def body(*refs):
        g_refs, got_refs = refs[:n], refs[n:2 * n]
        send_sems, recv_sems = refs[2 * n:]
        x, y, c = _place()
        sends = [pltpu.make_async_remote_copy(
            src_ref=g_refs[a].at[2 * q + 1 - c], dst_ref=got_refs[a].at[q], send_sem=send_sems.at[a, q],
            recv_sem=recv_sems.at[a, q], device_id=(x, y, 1 - c), device_id_type=MESH_ID) for a in range(n) for q in range(4)]
        for cp in sends:
            cp.start()
        for cp in sends:
            cp.wait()

    return pl.pallas_call(
        body, name=name, in_specs=[_ANY] * n, out_specs=[_ANY] * n,
        out_shape=[jax.ShapeDtypeStruct((4,) + g.shape[1:], g.dtype) for g in grads],
        scratch_shapes=[pltpu.SemaphoreType.DMA((n, 4)), pltpu.SemaphoreType.DMA((n, 4))],
    )(*grads)


def _scatter_chips(parts, name):
    n = len(parts)

    def body(*refs):
        p_refs, o_refs = refs[:n], refs[n:2 * n]
        send_sems, recv_sems = refs[2 * n:]
        x, y, c = _place()
        chips = [(1 - x, y), (x, 1 - y), (1 - x, 1 - y)]
        sends = [pltpu.make_async_remote_copy(
            src_ref=p_refs[a].at[2 * px + py], dst_ref=o_refs[a].at[j], send_sem=send_sems.at[a, j],
            recv_sem=recv_sems.at[a, j], device_id=(px, py, c), device_id_type=MESH_ID)
            for a in range(n) for j, (px, py) in enumerate(chips)]
        for cp in sends:
            cp.start()
        for cp in sends:
            cp.wait()

    return pl.pallas_call(
        body, name=name, in_specs=[_ANY] * n, out_specs=[_ANY] * n,
        out_shape=[jax.ShapeDtypeStruct((3,) + p.shape[1:], p.dtype) for p in parts],
        scratch_shapes=[pltpu.SemaphoreType.DMA((n, 3)), pltpu.SemaphoreType.DMA((n, 3))],
    )(*parts)


ALL_PEERS = [(k >> 2 & 1, k >> 1 & 1, k & 1) for k in range(1, N_DEV)]
CHIP_PEERS = [(1, 0, 0), (0, 1, 0), (1, 1, 0)]
_HBM = pl.BlockSpec(memory_space=pltpu.HBM)
_SEM = pl.BlockSpec(memory_space=pltpu.SEMAPHORE)
_EFFECT = pltpu.SideEffectType.DATAFLOW_SIDE_EFFECTING


def _push_copies(src_refs, land_refs, send_sems, recv_sems, peers, src_of, slot_of):
    place = _place()
    flip = lambda v, f: 1 - v if f else v
    return [pltpu.make_async_remote_copy(
        src_ref=src_of(src_refs[a], k), dst_ref=land_refs[a].at[slot_of(k)], send_sem=send_sems[a], recv_sem=recv_sems[a],
        device_id=tuple(flip(v, f) for v, f in zip(place, peer)), device_id_type=MESH_ID)
        for a in range(len(src_refs)) for k, peer in enumerate(peers)]


def _push_start(srcs, land_shapes, peers, src_of, slot_of, name):
    n = len(srcs)

    def body(*refs):
        src_refs, land_refs = refs[:n], refs[n:2 * n]
        send_sems, recv_sems = refs[2 * n:3 * n], refs[3 * n:4 * n]
        token = refs[-1]
        for cp in _push_copies(src_refs, land_refs, send_sems, recv_sems, peers, src_of, slot_of):
            cp.start()
        token[...] = jnp.zeros_like(token)

    sems = [pltpu.SemaphoreType.DMA(())] * (2 * n)
    lands = [pltpu.with_memory_space_constraint(lax.empty(s.shape, s.dtype), pltpu.HBM) for s in land_shapes]
    res = pl.pallas_call(
        body, name=name, in_specs=[_HBM] * (2 * n), out_specs=[_SEM] * (2 * n) + [_HBM] * (2 * n) + [pl.BlockSpec(memory_space=pltpu.VMEM)],
        out_shape=sems + [pltpu.HBM(s.shape, s.dtype) for s in srcs] + [pltpu.HBM(s.shape, s.dtype) for s in land_shapes]
        + [jax.ShapeDtypeStruct((8, 128), F32)],
        input_output_aliases={i: 2 * n + i for i in range(2 * n)},
        compiler_params=pltpu.CompilerParams(has_side_effects=_EFFECT),
    )(*[pltpu.with_memory_space_constraint(s, pltpu.HBM) for s in srcs], *lands)
    return list(res[:n]), list(res[n:2 * n]), list(res[2 * n:3 * n]), list(res[3 * n:4 * n]), res[-1]


def _push_wait(send_sems, recv_sems, srcs, lands, after, peers, src_of, slot_of, name):
    n = len(srcs)

    def body(*refs):
        src_refs, land_refs = refs[:n], refs[n:2 * n]
        s_sems, r_sems = refs[2 * n:3 * n], refs[3 * n:4 * n]
        copies = _push_copies(src_refs, land_refs, s_sems, r_sems, peers, src_of, slot_of)
        for cp in copies:
            cp.wait_send()
        for cp in copies:
            cp.wait_recv()

    res = pl.pallas_call(
        body, name=name, in_specs=[_HBM] * (2 * n) + [_SEM] * (2 * n) + [_ANY], out_specs=[_HBM] * (2 * n),
        out_shape=[pltpu.HBM(s.shape, s.dtype) for s in srcs] + [pltpu.HBM(s.shape, s.dtype) for s in lands],
        input_output_aliases={i: i for i in range(2 * n)},
        compiler_params=pltpu.CompilerParams(has_side_effects=_EFFECT),
    )(*srcs, *lands, *send_sems, *recv_sems, after)
    return list(res[:n]), list(res[n:])


def _pick_sum(picked, rest, index, pick_of, out_dtype, name):
    nq, r, cdim = rest.shape
    one = nq == 3
    tr = _row_tile(r, 512, 16)
    tc = 512 if cdim % 512 == 0 else cdim
    grid = (1 if one else nq, r // tr, cdim // tc)

    def body(i_ref, p_ref, r_ref, o_ref):
        acc = p_ref[0].astype(F32)
        if one:
            for j in range(3):
                acc = acc + r_ref[j].astype(F32)
            o_ref[...] = acc.astype(out_dtype)
        else:
            o_ref[0] = (acc + r_ref[0].astype(F32)).astype(out_dtype)

    spec = pltpu.PrefetchScalarGridSpec(
        num_scalar_prefetch=1, grid=grid,
        in_specs=[pl.BlockSpec((1, tr, tc), lambda q, i, j, i_ref: (pick_of(q, i_ref[0]), i, j)),
                  pl.BlockSpec((3, tr, tc), lambda q, i, j, i_ref: (0, i, j)) if one else pl.BlockSpec((1, tr, tc), lambda q, i, j, i_ref: (q, i, j))],
        out_specs=pl.BlockSpec((tr, tc), lambda q, i, j, i_ref: (i, j)) if one else pl.BlockSpec((1, tr, tc), lambda q, i, j, i_ref: (q, i, j)))
    return pl.pallas_call(
        body, name=name, grid_spec=spec,
        out_shape=jax.ShapeDtypeStruct((r, cdim) if one else (nq, r, cdim), out_dtype),
        compiler_params=_params(("parallel", "parallel", "parallel")),
    )(index.astype(jnp.int32).reshape(1), picked, rest)


def _adamw_fn(w, g, m, v):
    m = ADAM_B1 * m + (1.0 - ADAM_B1) * g
    v = ADAM_B2 * v + (1.0 - ADAM_B2) * jnp.square(g)
    m_hat = m / (1.0 - ADAM_B1 ** ADAM_STEP)
    v_hat = v / (1.0 - ADAM_B2 ** ADAM_STEP)
    delta = -ADAM_LR * (m_hat / (jnp.sqrt(v_hat) + ADAM_EPS) + ADAM_WD * w)
    return delta, m, v


def _as2d(a):
    return a.reshape(-1, a.shape[-1])


def _adamw_shard(w, g, m, v, name):
    shape = w.shape
    ins = [_as2d(a) for a in (w, g, m, v)]
    cols = ins[0].shape[1]
    outs = _ew(_adamw_fn, [(a, True) for a in ins], [(cols, F32, "tile")] * 3, 256, name)
    return [o.reshape(shape) for o in outs]


def _adamw_small(ws, gs, ms, vs, name):
    shapes = [w.shape for w in ws]
    flat = lambda a: a.reshape(-1, 128) if a.size % 128 == 0 else a.reshape(1, -1)
    ins = [flat(a) for grp in zip(ws, gs, ms, vs) for a in grp]
    k = len(ws)

    def fn(*vals):
        out = []
        for i in range(k):
            out += list(_adamw_fn(*vals[4 * i:4 * i + 4]))
        return out

    outs = _whole(fn, ins, [(ins[4 * (i // 3)].shape, F32) for i in range(3 * k)], name)
    deltas = [outs[3 * i].reshape(shapes[i]) for i in range(k)]
    new_m = [outs[3 * i + 1].reshape(shapes[i]) for i in range(k)]
    new_v = [outs[3 * i + 2].reshape(shapes[i]) for i in range(k)]
    return deltas, new_m, new_v


def _sum8(stacked, name):
    def fn(a):
        s = a[0:1]
        for i in range(1, N_DEV):
            s = s + a[i:i + 1]
        return s
    w = stacked.shape[1]
    tw = 8192
    if w % tw:
        return _whole(fn, [stacked], [((1, w), F32)], name)[0]

    def body(a_ref, o_ref):
        o_ref[...] = fn(a_ref[...])

    return pl.pallas_call(body, name=name, grid=(w // tw,), in_specs=[pl.BlockSpec((N_DEV, tw), lambda i: (0, i))],
                          out_specs=pl.BlockSpec((1, tw), lambda i: (0, i)), out_shape=jax.ShapeDtypeStruct((1, w), F32))(stacked)


ROWS = 512


def _split_heads(p, nh):
    t = p.shape[0]
    return p.reshape(t, nh, p.shape[1] // nh).transpose(1, 0, 2)


def _merge_heads(p):
    nh, t, d = p.shape
    return p.transpose(1, 0, 2).reshape(t, nh * d)


def _layer_fwd(h, mod, w, small, rope, l):
    sh1, sc1, gt1, sh2, sc2, gt2 = mod
    rope_ret, rope_mla = rope
    t = h.shape[0]
    nm = lambda s: f"l{l}_{s}"
    a1 = _ew(_norm_mod_fn, [(h, True), (small["norm1_g"], False), (sc1, False), (sh1, False)], [(D_MODEL, BF16, "tile")], ROWS, nm("norm1"))[0]
    p_s5 = _mm(a1, w["w_in_t"], "nt", 512, 512, 2048, name=nm("proj_s5"), n=512)
    p_ret = _mm(a1, w["w_in_t"], "nt", 512, 512, 2048, name=nm("proj_ret"), b_off=1, n=1536)
    p_swa = _mm(a1, w["w_in_t"], "nt", 512, 256, 2048, name=nm("proj_swa"), b_off=8, n=768)
    p_mla = _mm(a1, w["w_in_t"][2816:], "nt", 512, 576, 2048, name=nm("proj_mla"))
    b_cat, c_cat, dskip, a_r, a_i = small["s5"]
    bu = _mm(p_s5, b_cat, "nn", 512, 1024, 512, name=nm("s5_bu"))
    st_r, st_i = _s5_scan_fwd(bu, a_r, a_i, nm("s5_scan"))
    ypre = _mm(st_r, c_cat[:S5_WIDTH], "nn", 512, 512, 1024, name=nm("s5_y"), pair=(st_i, c_cat[S5_WIDTH:]))
    z = _ew(_s5_act_fn, [(ypre, True), (p_s5, True), (dskip, False)], [(GROUP_WIDTH, F32, "tile")], ROWS, nm("s5_act"))[0]
    zz = _mm(z, w["glu_w"], "nn", 512, 512, 512, name=nm("s5_zz"))
    y_s5 = _ew(_s5_glu_fn, [(z, True), (zz, True), (small["s5_glu_b"], False)], [(GROUP_WIDTH, BF16, "tile")], ROWS, nm("s5_glu"))[0]
    qk_ret = _split_heads(_rope(p_ret[:, :2 * RET_HEADS * RET_QK], rope_ret, nm("ret_rope")), 2 * RET_HEADS)
    o_ret, y_ret = _ret_fwd(qk_ret, p_ret, small["ret_lgam"], nm("ret"))
    qkv_swa = _split_heads(p_swa, 12)
    o_swa, lse_swa = _swa_fwd(qkv_swa, small["swa_sinks"], nm("swa"))
    y_swa = _merge_heads(o_swa).astype(BF16)
    cq, ckv, kr = p_mla[:, :MLA_Q_RANK], p_mla[:, MLA_Q_RANK:MLA_Q_RANK + MLA_KV_RANK], p_mla[:, MLA_Q_RANK + MLA_KV_RANK:]
    cqn = _ew(_rms_gain_fn, [(cq, True), (small["mla_q_norm"], False)], [(MLA_Q_RANK, BF16, "tile")], ROWS, nm("mla_qnorm"))[0]
    ckvn = _ew(_rms_gain_fn, [(ckv, True), (small["mla_kv_norm"], False)], [(MLA_KV_RANK, BF16, "tile")], ROWS, nm("mla_kvnorm"))[0]
    q_full = _mm(cqn, w["w_uq_t"], "nt", 512, 768, 384, name=nm("mla_q"))
    kv_full = _mm(ckvn, w["w_ukv_t"], "nt", 512, 1024, 128, BF16, name=nm("mla_kv"))
    nq = q_full.shape[1]
    roped = _rope(jnp.concatenate([q_full, kr, jnp.zeros_like(kr)], axis=1), rope_mla, nm("mla_rope"), out_dtype=BF16)
    q4 = roped[:, :nq].reshape(t, MLA_HEADS, MLA_NOPE + MLA_ROPE)
    qn = q4[:, :, :MLA_NOPE].transpose(1, 0, 2)
    rot = jnp.concatenate([q4[:, :, MLA_NOPE:].transpose(1, 0, 2), roped[None, :, nq:nq + MLA_ROPE]], axis=0)
    o_mla, lse_mla = _mla_attend(qn, rot, kv_full, nm("mla"))
    cat = jnp.concatenate([y_s5, y_ret, y_swa, o_mla.astype(BF16)], axis=1)
    mixed = _mm(cat, w["w_out"], "nn", 512, 1024, 2048, name=nm("out_proj"))
    h1 = _ew(_gate_add_fn, [(h, True), (mixed, True), (gt1, False)], [(D_MODEL, F32, "tile")], ROWS, nm("res1"))[0]
    a2 = _ew(_norm_mod_fn, [(h1, True), (small["norm2_g"], False), (sc2, False), (sh2, False)], [(D_MODEL, BF16, "tile")], ROWS, nm("norm2"))[0]
    hid, act = _mm(a2, w["w1_t"], "nt", 1024, 1024, 2048, name=nm("mlp1"), epi=lambda acc: (acc, _relu2_fn(acc)), epi_outs=[F32, BF16])
    mo = _mm(act, w["w2"], "nn", 1024, 1024, 2048, name=nm("mlp2"))
    h2 = _ew(_gate_add_fn, [(h1, True), (mo, True), (gt2, False)], [(D_MODEL, F32, "tile")], ROWS, nm("res2"))[0]
    saved = dict(h=h, a1=a1, p_s5=p_s5, p_ret=p_ret, st_r=st_r, st_i=st_i, ypre=ypre, z=z, zz=zz, qk_ret=qk_ret, o_ret=o_ret,
                 qkv_swa=qkv_swa, o_swa=o_swa, lse_swa=lse_swa, cq=cq, ckv=ckv, cqn=cqn, ckvn=ckvn, qn=qn, rot=rot,
                 kv_full=kv_full, o_mla=o_mla, lse_mla=lse_mla, cat=cat, mixed=mixed, h1=h1, a2=a2, hid=hid, act=act, mo=mo)
    return h2, saved


def _layer_bwd(dh2, mod, w, small, rope, s, l):
    sh1, sc1, gt1, sh2, sc2, gt2 = mod
    rope_ret, rope_mla = rope
    t = dh2.shape[0]
    nm = lambda n: f"l{l}_{n}_bwd"
    gb, gs = {}, {}
    row = (D_MODEL, F32, "acc")
    dmo, dgt2 = _ew(lambda d, y, gt: (d * gt, jnp.sum(d * y, axis=0, keepdims=True)),
                    [(dh2, True), (s["mo"], True), (gt2, False)], [(D_MODEL, BF16, "tile"), row], ROWS, nm("res2"))
    dhid = _mm(dmo, w["w2"], "nt", 1024, 1024, 2048, name=nm("mlp2_x"), epi=lambda acc, x: (acc * 2.0 * jnp.maximum(x, 0.0),),
               epi_ins=[s["hid"]], epi_outs=[BF16])[0]
    gb["w2"] = _mm(s["act"], dmo, "tn", 1024, 1024, 1024, BF16, name=nm("mlp2_w"))
    da2 = _mm(dhid, w["w1_t"], "nn", 1024, 1024, 2048, name=nm("mlp1_x"))
    gb["w1_t"] = _mm(dhid, s["a2"], "tn", 1024, 1024, 1024, BF16, name=nm("mlp1_w"))

    def norm_bwd(hh, g, sc, sh, da, dres):
        dh_, dg, dsc, dsh = _vjp_block(_norm_mod_fn, 4)(hh, g, sc, sh, da)
        return dh_ + dres, dg, dsc, dsh

    dh1, gs["norm2_g"], dsc2, dsh2 = _ew(norm_bwd, [(s["h1"], True), (small["norm2_g"], False), (sc2, False), (sh2, False), (da2, True), (dh2, True)],
                                         [(D_MODEL, F32, "tile"), row, row, row], ROWS, nm("norm2"))
    dmixed, dgt1 = _ew(lambda d, y, gt: (d * gt, jnp.sum(d * y, axis=0, keepdims=True)),
                       [(dh1, True), (s["mixed"], True), (gt1, False)], [(D_MODEL, BF16, "tile"), row], ROWS, nm("res1"))
    dcat = _mm(dmixed, w["w_out"], "nt", 512, 1024, 2048, name=nm("out_proj_x"))
    gb["w_out"] = _mm(s["cat"], dmixed, "tn", 1024, 1024, 1024, BF16, name=nm("out_proj_w"))
    dy_s5, dy_ret, dy_swa, dy_mla = (dcat[:, i * GROUP_WIDTH:(i + 1) * GROUP_WIDTH] for i in range(4))
    b_cat, c_cat, dskip, a_r, a_i = small["s5"]
    gw = (GROUP_WIDTH, F32, "tile")
    gacc = (GROUP_WIDTH, F32, "acc")
    dz_a, dzz, gs["s5_glu_b"] = _ew(_vjp_block(_s5_glu_fn, 3), [(s["z"], True), (s["zz"], True), (small["s5_glu_b"], False), (dy_s5, True)],
                                    [gw, gw, gacc], ROWS, nm("s5_glu"))
    dz_b = _mm(dzz, w["glu_w"], "nt", 512, 512, 512, name=nm("s5_zz_x"))
    gb["glu_w"] = _mm(s["z"], dzz, "tn", 512, 512, 1024, BF16, name=nm("s5_zz_w"))

    def act_bwd(ypre, u, dsk, dza, dzb):
        return _vjp_block(_s5_act_fn, 3)(ypre, u, dsk, dza + dzb)

    dypre, du_a, g_dskip = _ew(act_bwd, [(s["ypre"], True), (s["p_s5"], True), (dskip, False), (dz_a, True), (dz_b, True)],
                               [gw, gw, gacc], ROWS, nm("s5_act"))
    dst = _mm(dypre, c_cat, "nt", 512, 1024, 512, name=nm("s5_y_x"))
    g_ccat = jnp.concatenate([_mm(s["st_r"], dypre, "tn", 1024, 512, 1024, name=nm("s5_y_w_re")),
                              _mm(s["st_i"], dypre, "tn", 1024, 512, 1024, name=nm("s5_y_w_im"))], axis=0)
    dbu_r, dbu_i, g_ar, g_ai = _s5_scan_bwd(dst, s["st_r"], s["st_i"], a_r, a_i, nm("s5_scan"))
    du_b = _mm(dbu_r, b_cat[:, :S5_WIDTH], "nt", 512, 512, 1024, name=nm("s5_bu_x"), pair=(dbu_i, b_cat[:, S5_WIDTH:]))
    g_bcat = jnp.concatenate([_mm(s["p_s5"], dbu_r, "tn", 512, 1024, 1024, name=nm("s5_bu_w_re")),
                              _mm(s["p_s5"], dbu_i, "tn", 512, 1024, 1024, name=nm("s5_bu_w_im"))], axis=1)
    gs["s5"] = (g_bcat, g_ccat, g_dskip, g_ar, g_ai)
    dqk_rot, dk_rot, dv_ret, dg_ret = _ret_bwd(s["qk_ret"], s["p_ret"], s["o_ret"], dy_ret, small["ret_lgam"], nm("ret"))
    dqk = _rope(_merge_heads(jnp.concatenate([dqk_rot, dk_rot], axis=0)), rope_ret, nm("ret_rope"), inverse=True)
    dqkv_swa, gs["swa_sinks"] = _swa_bwd(s["qkv_swa"], s["o_swa"], s["lse_swa"], _split_heads(dy_swa, SWA_HEADS), small["swa_sinks"], nm("swa"))
    delta = _mla_delta(s["o_mla"], dy_mla, nm("mla_delta"))
    dqn, dqr, dkn, dv_mla, dkr_heads = _mla_attend_bwd(s["qn"], s["rot"], s["kv_full"], s["lse_mla"], delta, dy_mla, nm("mla_att"))
    dkv_full = jnp.stack([dkn.reshape(t, MLA_HEADS, MLA_NOPE), dv_mla.reshape(t, MLA_HEADS, MLA_V)], axis=2).reshape(t, 2 * MLA_HEADS * MLA_NOPE)
    dkr_rot = _ew(lambda a, b, c, d: a + b + c + d, [(dkr_heads[i], True) for i in range(MLA_HEADS)], [(MLA_ROPE, F32, "tile")], ROWS, nm("mla_dkr"))[0]
    nq = MLA_HEADS * (MLA_NOPE + MLA_ROPE)
    dq_rot = jnp.concatenate([dqn.transpose(1, 0, 2), dqr.transpose(1, 0, 2)], axis=2).reshape(t, nq)
    droped = _rope(jnp.concatenate([dq_rot, dkr_rot, jnp.zeros_like(dkr_rot)], axis=1), rope_mla, nm("mla_rope"), inverse=True)
    dq_full, dkr = droped[:, :nq], droped[:, nq:nq + MLA_ROPE]
    dcqn = _mm(dq_full, w["w_uq_t"], "nn", 512, 384, 768, name=nm("mla_q_x"))
    gb["w_uq_t"] = _mm(dq_full, s["cqn"], "tn", 768, 384, 1024, BF16, name=nm("mla_q_w"))
    dckvn = _mm(dkv_full, w["w_ukv_t"], "nn", 512, 128, 1024, name=nm("mla_kv_x"))
    gb["w_ukv_t"] = _mm(dkv_full, s["ckvn"], "tn", 1024, 128, 1024, BF16, name=nm("mla_kv_w"))
    dcq, gs["mla_q_norm"] = _ew(_vjp_block(_rms_gain_fn, 2), [(s["cq"], True), (small["mla_q_norm"], False), (dcqn, True)],
                                [(MLA_Q_RANK, F32, "tile"), (MLA_Q_RANK, F32, "acc")], ROWS, nm("mla_qnorm"))
    dckv, gs["mla_kv_norm"] = _ew(_vjp_block(_rms_gain_fn, 2), [(s["ckv"], True), (small["mla_kv_norm"], False), (dckvn, True)],
                                  [(MLA_KV_RANK, F32, "tile"), (MLA_KV_RANK, F32, "acc")], ROWS, nm("mla_kvnorm"))
    du = _ew(lambda a, b: a + b, [(du_a, True), (du_b, True)], [(GROUP_WIDTH, BF16, "tile")], ROWS, nm("s5_du"))[0]
    bf = lambda a: a.astype(BF16)
    dproj = jnp.concatenate([du, bf(dqk), bf(dv_ret), bf(dg_ret), bf(_merge_heads(dqkv_swa)), bf(dcq), bf(dckv), bf(dkr)], axis=1)
    da1 = _mm(dproj, w["w_in_t"], "nn", 512, 1024, N_IN, name=nm("proj_x"))
    gb["w_in_t"] = _mm(dproj, s["a1"], "tn", N_IN, 512, 512, BF16, name=nm("proj_w"))
    dh, gs["norm1_g"], dsc1, dsh1 = _ew(norm_bwd, [(s["h"], True), (small["norm1_g"], False), (sc1, False), (sh1, False), (da1, True), (dh1, True)],
                                        [(D_MODEL, F32, "tile"), row, row, row], ROWS, nm("norm1"))
    dmod = jnp.concatenate([dsh1, dsc1, dgt1, dsh2, dsc2, dgt2], axis=1)
    return dh, gb, gs, dmod


BIG = ("w_in_t", "w1_t", "w_uq_t", "w_ukv_t", "w_out", "w2", "glu_w")
S5_NAMES = ("s5_lambda_re", "s5_lambda_im", "s5_log_dt", "s5_b_re", "s5_b_im", "s5_c_re", "s5_c_im", "s5_d")


def kernel(x, c, norm1_g, norm2_g, ada_w, ada_b, w_in, s5_lambda_re, s5_lambda_im, s5_log_dt, s5_b_re, s5_b_im, s5_c_re, s5_c_im, s5_d, s5_glu_w, s5_glu_b, swa_sinks, mla_q_norm, mla_kv_norm, mla_w_uq, mla_w_ukv, w_out, mlp_w1, mlp_w2, final_norm_g, loss_target, m_norm1_g, m_norm2_g, m_ada_w, m_ada_b, m_w_in, m_s5_lambda_re, m_s5_lambda_im, m_s5_log_dt, m_s5_b_re, m_s5_b_im, m_s5_c_re, m_s5_c_im, m_s5_d, m_s5_glu_w, m_s5_glu_b, m_swa_sinks, m_mla_q_norm, m_mla_kv_norm, m_mla_w_uq, m_mla_w_ukv, m_w_out, m_mlp_w1, m_mlp_w2, m_final_norm_g, v_norm1_g, v_norm2_g, v_ada_w, v_ada_b, v_w_in, v_s5_lambda_re, v_s5_lambda_im, v_s5_log_dt, v_s5_b_re, v_s5_b_im, v_s5_c_re, v_s5_c_im, v_s5_d, v_s5_glu_w, v_s5_glu_b, v_swa_sinks, v_mla_q_norm, v_mla_kv_norm, v_mla_w_uq, v_mla_w_ukv, v_w_out, v_mlp_w1, v_mlp_w2, v_final_norm_g):
    names = ["norm1_g", "norm2_g", "ada_w", "ada_b", "w_in", "s5_lambda_re", "s5_lambda_im", "s5_log_dt", "s5_b_re", "s5_b_im",
             "s5_c_re", "s5_c_im", "s5_d", "s5_glu_w", "s5_glu_b", "swa_sinks", "mla_q_norm", "mla_kv_norm", "mla_w_uq",
             "mla_w_ukv", "w_out", "mlp_w1", "mlp_w2", "final_norm_g"]
    env = locals()
    wts = {n: env[n] for n in names}
    mom = {n: env["m_" + n] for n in names}
    var = {n: env["v_" + n] for n in names}
    t = x.shape[1]
    me = 4 * lax.axis_index("x") + 2 * lax.axis_index("y") + lax.axis_index("c")
    rope = _rope_tables(t)
    ret_lgam = jnp.log1p(-(2.0 ** (-5.0 - jnp.arange(RET_HEADS, dtype=F32))))

    tr = lambda a: a.transpose(0, 2, 1)
    shard = {"w_in_t": tr(w_in), "w1_t": tr(mlp_w1), "w_uq_t": tr(mla_w_uq), "w_ukv_t": tr(mla_w_ukv),
             "w_out": w_out, "w2": mlp_w2, "glu_w": s5_glu_w}
    to_send = [[shard[k][l].astype(BF16) for k in BIG] for l in range(DEPTH)]
    gathered = _all_gather(to_send[0] + [c], "gather_weights_l0")
    c_all = gathered[-1].reshape(N_DEV, D_MODEL)
    as_rows = lambda arrs: {k: a.reshape(-1, shard[k].shape[2]) for k, a in zip(BIG, arrs)}
    big = [as_rows(gathered[:len(BIG)]), None]
    own_slot = lambda k: 4 * lax.axis_index("x") + 2 * lax.axis_index("y") + lax.axis_index("c")
    gather1 = _push_start(to_send[1], [jax.ShapeDtypeStruct((N_DEV,) + a.shape, a.dtype) for a in to_send[1]], ALL_PEERS,
                          lambda ref, k: ref, own_slot, "gather_weights_l1_start")

    c_act = _whole(lambda v: v * jax.nn.sigmoid(v), [c_all], [((N_DEV, D_MODEL), F32)], "cond_silu")[0]
    c_pad = jnp.concatenate([c_act, jnp.zeros((128 - N_DEV, D_MODEL), F32)], axis=0)
    cols = ada_w.shape[2]
    mod_part = [_mm(c_pad, ada_w[l], "nn", 128, cols, 512, name=f"l{l}_mod")[:N_DEV] for l in range(DEPTH)]
    mod_all = _all_gather([jnp.stack(mod_part)], "gather_mod")[0]
    mod_rows = lax.dynamic_index_in_dim(mod_all, me, axis=2, keepdims=False)
    mods = []
    for l in range(DEPTH):
        row = mod_rows[:, l].reshape(1, 6 * D_MODEL) + ada_b[l][None]
        if l == 0:
            row = row + gather1[4][0, 0]
        mods.append([row[:, i * D_MODEL:(i + 1) * D_MODEL] for i in range(6)])

    smalls, s5_pulls = [], []
    for l in range(DEPTH):
        s5_ops, pull = jax.vjp(_s5_prep, *[wts[n][l] for n in S5_NAMES])
        s5_pulls.append(pull)
        smalls.append(dict(norm1_g=norm1_g[l][None], norm2_g=norm2_g[l][None], s5=s5_ops, s5_glu_b=s5_glu_b[l][None],
                           swa_sinks=swa_sinks[l], mla_q_norm=mla_q_norm[l][None], mla_kv_norm=mla_kv_norm[l][None], ret_lgam=ret_lgam))
    h = x[0]
    saved = []
    for l in range(DEPTH):
        if l == 1:
            sent, landed = _push_wait(gather1[0], gather1[1], gather1[2], gather1[3], h, ALL_PEERS, lambda ref, k: ref, own_slot,
                                      "gather_weights_l1_wait")
            big[1] = as_rows([lax.dynamic_update_index_in_dim(full, own, me, 0) for full, own in zip(landed, sent)])
        h, s = _layer_fwd(h, mods[l], big[l], smalls[l], rope, l)
        saved.append(s)

    fg = final_norm_g[None]
    tgt = loss_target[0]
    loss_local = _ew(_final_fn, [(h, True), (fg, False), (tgt, True)], [(1, F32, "acc")], ROWS, "loss")[0]

    def final_bwd(hh, g, tg):
        dh_, dg, _ = _vjp_block(_final_fn, 3)(hh, g, tg, jnp.ones((1, 1), F32))
        return dh_, dg

    dh, g_final = _ew(final_bwd, [(h, True), (fg, False), (tgt, True)], [(D_MODEL, F32, "tile"), (D_MODEL, F32, "acc")], ROWS, "loss_bwd")
    loss = lax.psum(loss_local[0, 0], ("x", "y", "c"))

    core, chip = lax.axis_index("c"), 2 * lax.axis_index("x") + lax.axis_index("y")

    def core_stage(g_layer, l):
        g_list = [g_layer[k].reshape(N_DEV, -1, g_layer[k].shape[1]) for k in BIG]
        got = _scatter_core(g_list, f"scatter_core_l{l}")
        return [_pick_sum(g, o, core, lambda q, c_: 2 * q + c_, BF16, f"l{l}_core_sum_{i}") for i, (g, o) in enumerate(zip(g_list, got))]

    def their_block(ref, k):
        x_, y_ = lax.axis_index("x"), lax.axis_index("y")
        dx, dy, _ = CHIP_PEERS[k]
        return ref.at[2 * (1 - x_ if dx else x_) + (1 - y_ if dy else y_)]

    g_small, dmods = [None] * DEPTH, [None] * DEPTH
    dh, g_big1, g_small[1], dmods[1] = _layer_bwd(dh, mods[1], big[1], smalls[1], rope, saved[1], 1)
    halves1 = core_stage(g_big1, 1)
    chips1 = _push_start(halves1, [jax.ShapeDtypeStruct((3,) + a.shape[1:], a.dtype) for a in halves1], CHIP_PEERS,
                         their_block, lambda k: k, "scatter_chips_l1_start")
    mods0 = [m + chips1[4][0, 0] for m in mods[0]]
    dh, g_big0, g_small[0], dmods[0] = _layer_bwd(dh, mods0, big[0], smalls[0], rope, saved[0], 0)
    grad_x = dh[None]
    halves0 = core_stage(g_big0, 0)
    halves1, landed1 = _push_wait(chips1[0], chips1[1], chips1[2], chips1[3], dh, CHIP_PEERS, their_block, lambda k: k, "scatter_chips_l1_wait")
    landed0 = _scatter_chips(halves0, "scatter_chips_l0")
    g_shard = [_pick_sum(h_, o, chip, lambda q, m_: m_, F32, f"chip_sum_{i}")
               for i, (h_, o) in enumerate(zip(list(halves0) + list(halves1), list(landed0) + list(landed1)))]

    small_parts = []
    for l in range(DEPTH):
        gs = g_small[l]
        s5g = s5_pulls[l](gs["s5"])
        small_parts += [gs["norm1_g"], gs["norm2_g"], *s5g, gs["s5_glu_b"], gs["swa_sinks"], gs["mla_q_norm"], gs["mla_kv_norm"]]
    small_parts += [g_final, *dmods]
    sizes = [int(np.prod(p.shape)) for p in small_parts]
    flat = jnp.concatenate([p.reshape(1, -1) for p in small_parts], axis=1)
    pad = (-flat.shape[1]) % 8192
    flat = jnp.pad(flat, ((0, 0), (0, pad)))
    flat_all = _all_gather([flat], "gather_small_grads")[0].reshape(N_DEV, -1)
    summed = _sum8(flat_all, "sum_small_grads")
    pieces, off = [], 0
    for sz in sizes:
        pieces.append(summed[0, off:off + sz])
        off += sz
    small_names = ["norm1_g", "norm2_g", *S5_NAMES, "s5_glu_b", "swa_sinks", "mla_q_norm", "mla_kv_norm"]
    per_layer = len(small_names)
    grads = {}
    for i, n in enumerate(small_names):
        grads[n] = jnp.stack([pieces[l * per_layer + i].reshape(wts[n].shape[1:]) for l in range(DEPTH)])
    grads["final_norm_g"] = pieces[DEPTH * per_layer]
    grads["ada_b"] = jnp.stack([pieces[DEPTH * per_layer + 1 + l] for l in range(DEPTH)])

    mod_off = sum(sizes[:DEPTH * per_layer + 1])
    dmod_all = flat_all[:, mod_off:mod_off + DEPTH * 6 * D_MODEL].reshape(N_DEV, DEPTH, N_DEV, cols)
    dmod_mine = lax.dynamic_index_in_dim(dmod_all, me, axis=2, keepdims=False).transpose(1, 0, 2)
    dmod_pad = jnp.concatenate([dmod_mine, jnp.zeros((DEPTH, 128 - N_DEV, cols), F32)], axis=1)
    grads["ada_w"] = jnp.stack([_mm(c_pad, dmod_pad[l], "tn", 512, cols, 128, name=f"l{l}_ada_w_grad") for l in range(DEPTH)])

    out_g, out_d, out_m, out_v = dict(grads), {}, {}, {}
    orig = {"w_in_t": "w_in", "w1_t": "mlp_w1", "w_uq_t": "mla_w_uq", "w_ukv_t": "mla_w_ukv", "w_out": "w_out", "w2": "mlp_w2", "glu_w": "s5_glu_w"}
    for i, k in enumerate(BIG):
        out_g[orig[k]] = jnp.stack([g_shard[l * len(BIG) + i] for l in range(DEPTH)])
        if k.endswith("_t"):
            out_g[orig[k]] = tr(out_g[orig[k]])
    for n in [*orig.values(), "ada_w"]:
        out_d[n], out_m[n], out_v[n] = _adamw_shard(wts[n], out_g[n], mom[n], var[n], f"adamw_{n}")
    small_all = small_names + ["ada_b", "final_norm_g"]
    ds, ms, vs = _adamw_small([wts[n] for n in small_all], [grads[n] for n in small_all], [mom[n] for n in small_all],
                              [var[n] for n in small_all], "adamw_small")
    for n, d, m_, v_ in zip(small_all, ds, ms, vs):
        out_d[n], out_m[n], out_v[n] = d, m_, v_
    return (loss, grad_x, *[out_g[n] for n in names], *[out_d[n] for n in names], *[out_m[n] for n in names], *[out_v[n] for n in names])
```

```python
import functools
import math

import numpy as np
import jax
import jax.numpy as jnp
from jax import lax
from jax.experimental import pallas as pl
from jax.experimental.pallas import tpu as pltpu

F32 = jnp.float32
BF16 = jnp.bfloat16
_MXU_DTYPE = jnp.bfloat16

N_DEV = 8
D_MODEL = 2048
DEPTH = 2
GROUP_WIDTH = 512
D_FF = 8192
S5_CH, S5_GROUPS, S5_STATE = 16, 32, 64
S5_WIDTH = S5_GROUPS * S5_STATE
RET_HEADS, RET_QK, RET_V, RET_CHUNK = 4, 64, 128, 128
SWA_HD, SWA_HEADS, SWA_KV_HEADS, WINDOW = 64, 8, 2, 128
MLA_HEADS, MLA_Q_RANK, MLA_KV_RANK, MLA_NOPE, MLA_ROPE, MLA_V = 4, 384, 128, 128, 64, 128
ROPE_BASE = 10000.0
EPS = 1e-6
NEG = -1e30
N_IN = 3392
ADAM_LR, ADAM_B1, ADAM_B2, ADAM_EPS, ADAM_WD, ADAM_STEP = 0.001, 0.9, 0.999, 1e-08, 0.01, 10

VMEM_LIMIT_BYTES = 52 * 1024 * 1024
MESH_ID = pl.DeviceIdType.MESH
_ANY = pl.BlockSpec(memory_space=pl.ANY)
_SMEM = pl.BlockSpec(memory_space=pltpu.SMEM)


def _params(sem):
    return pltpu.CompilerParams(dimension_semantics=sem, vmem_limit_bytes=VMEM_LIMIT_BYTES)


_DIMS = {"nn": (((1,), (0,)), ((), ())), "nt": (((1,), (1,)), ((), ())), "tn": (((0,), (0,)), ((), ()))}


def _dot(a, b, mode="nn"):
    return lax.dot_general(a.astype(_MXU_DTYPE), b.astype(_MXU_DTYPE), _DIMS[mode], preferred_element_type=F32)


def _mm(a, b, mode, tm, tn, tk, out_dtype=F32, name="mm", b_off=0, n=None, pair=None, epi=None, epi_ins=(), epi_outs=None):
    if mode == "tn":
        kdim, m = a.shape
    else:
        m, kdim = a.shape
    if n is None:
        n = b.shape[0] if mode == "nt" else b.shape[1]
    tm, tn, tk = min(tm, m), min(tn, n), min(tk, kdim)
    assert m % tm == 0 and n % tn == 0 and kdim % tk == 0, (name, a.shape, b.shape, tm, tn, tk)
    nk = kdim // tk
    a_spec = pl.BlockSpec((tk, tm), lambda i, j, k: (k, i)) if mode == "tn" else pl.BlockSpec((tm, tk), lambda i, j, k: (i, k))
    if mode == "nt":
        b_spec = pl.BlockSpec((tn, tk), lambda i, j, k: (j + b_off, k))
    else:
        b_spec = pl.BlockSpec((tk, tn), lambda i, j, k: (k, j + b_off))
    o_spec = pl.BlockSpec((tm, tn), lambda i, j, k: (i, j))
    n_mm = 2 if pair is None else 4
    out_dtypes = [out_dtype] if epi is None else list(epi_outs)

    def body(*refs):
        ins, extra = refs[:n_mm], refs[n_mm:n_mm + len(epi_ins)]
        outs = refs[n_mm + len(epi_ins):n_mm + len(epi_ins) + len(out_dtypes)]
        part = _dot(ins[0][...], ins[1][...], mode)
        if pair is not None:
            part = part + _dot(ins[2][...], ins[3][...], mode)

        def finish(acc):
            vals = (acc,) if epi is None else epi(acc, *[r[...] for r in extra])
            for o_ref, v, dt in zip(outs, vals, out_dtypes):
                o_ref[...] = v.astype(dt)

        if nk == 1:
            finish(part)
        else:
            acc_ref = refs[-1]
            k = pl.program_id(2)

            @pl.when(k == 0)
            def _():
                acc_ref[...] = part

            @pl.when(k > 0)
            def _():
                acc_ref[...] += part

            @pl.when(k == nk - 1)
            def _():
                finish(acc_ref[...])

    operands = [a, b] + ([] if pair is None else list(pair)) + list(epi_ins)
    res = pl.pallas_call(
        body, name=name, grid=(m // tm, n // tn, nk),
        in_specs=[a_spec, b_spec] * (n_mm // 2) + [o_spec] * len(epi_ins),
        out_specs=[o_spec] * len(out_dtypes), out_shape=[jax.ShapeDtypeStruct((m, n), dt) for dt in out_dtypes],
        scratch_shapes=[] if nk == 1 else [pltpu.VMEM((tm, tn), F32)],
        compiler_params=_params(("parallel", "parallel", "arbitrary")),
    )(*operands)
    return res[0] if epi is None else res


SUBLANES = 8


def _row_tile(rows, target, mult=SUBLANES):
    best = None
    for cand in range(mult, min(rows, target) + 1, mult):
        if rows % cand == 0:
            best = cand
    return best or rows


def _ew(fn, ins, outs, tt, name):
    t = [a.shape[0] for a, tiled in ins if tiled][0]
    tt = _row_tile(t, tt)
    n_in = len(ins)
    in_specs = [pl.BlockSpec((tt, a.shape[1]), lambda i: (i, 0)) if tiled else pl.BlockSpec(a.shape, lambda i: (0, 0))
                for a, tiled in ins]
    out_specs, out_shapes = [], []
    for w, dt, kind in outs:
        if kind == "tile":
            out_specs.append(pl.BlockSpec((tt, w), lambda i: (i, 0)))
            out_shapes.append(jax.ShapeDtypeStruct((t, w), dt))
        else:
            out_specs.append(pl.BlockSpec((1, w), lambda i: (0, 0)))
            out_shapes.append(jax.ShapeDtypeStruct((1, w), F32))
    has_acc = any(kind == "acc" for _, _, kind in outs)

    def body(*refs):
        vals = fn(*[r[...] for r in refs[:n_in]])
        if not isinstance(vals, (tuple, list)):
            vals = (vals,)
        i = pl.program_id(0)
        for o_ref, v, (w, dt, kind) in zip(refs[n_in:], vals, outs):
            if kind == "tile":
                o_ref[...] = v.astype(dt)
            else:
                @pl.when(i == 0)
                def _(o_ref=o_ref, v=v):
                    o_ref[...] = v.astype(F32)

                @pl.when(i > 0)
                def _(o_ref=o_ref, v=v):
                    o_ref[...] += v.astype(F32)

    res = pl.pallas_call(
        body, name=name, grid=(t // tt,), in_specs=in_specs, out_specs=out_specs, out_shape=out_shapes,
        compiler_params=_params(("arbitrary" if has_acc else "parallel",)),
    )(*[a for a, _ in ins])
    return res


def _whole(fn, ins, outs, name):
    def body(*refs):
        vals = fn(*[r[...] for r in refs[:len(ins)]])
        if not isinstance(vals, (tuple, list)):
            vals = (vals,)
        for o_ref, v in zip(refs[len(ins):], vals):
            o_ref[...] = v.astype(o_ref.dtype)

    return pl.pallas_call(body, name=name, out_shape=[jax.ShapeDtypeStruct(s, dt) for s, dt in outs])(*ins)


def _rms(x):
    return x * lax.rsqrt(jnp.mean(x * x, axis=-1, keepdims=True) + EPS)


def _norm_mod_fn(h, g, sc, sh):
    return (_rms(h) * g) * (1.0 + sc) + sh


def _rms_gain_fn(x, g):
    return _rms(x) * g


def _gate_add_fn(h, y, gt):
    return h + gt * y


def _relu2_fn(x):
    return jnp.square(jnp.maximum(x, 0.0))


def _s5_act_fn(ypre, u, dskip):
    return jax.nn.gelu(ypre + dskip * u)


def _s5_glu_fn(z, zz, b):
    return z * jax.nn.sigmoid(zz + b)


def _ret_gate_fn(o, g):
    return _rms(o) * (g * jax.nn.sigmoid(g))


def _final_fn(h, g, tgt):
    err = _rms(h) * g - tgt
    return 0.5 * jnp.sum(jnp.mean(err * err, axis=-1, keepdims=True), axis=0, keepdims=True)


def _vjp_block(fn, n_args):
    def bwd(*vals):
        _, pull = jax.vjp(fn, *vals[:n_args])
        return pull(vals[n_args])
    return bwd


def _rope_tables(t):
    d = RET_QK
    inv = ROPE_BASE ** (-jnp.arange(0, d, 2, dtype=F32) / d)
    ang = jnp.arange(t, dtype=F32)[:, None] * inv[None, :]
    cos, sin = jnp.cos(ang), jnp.sin(ang)
    cos2, sin2 = jnp.concatenate([cos, cos], -1), jnp.concatenate([-sin, sin], -1)
    ret = (jnp.tile(cos2, (1, 8)), jnp.tile(sin2, (1, 8)))
    one, zero = jnp.ones((t, MLA_NOPE), F32), jnp.zeros((t, MLA_NOPE), F32)
    mla_c = jnp.concatenate([jnp.tile(jnp.concatenate([one, cos2], -1), (1, MLA_HEADS)), cos2, one[:, :d]], -1)
    mla_s = jnp.concatenate([jnp.tile(jnp.concatenate([zero, sin2], -1), (1, MLA_HEADS)), sin2, zero[:, :d]], -1)
    return ret, (mla_c, mla_s)


def _rope_fn(x, c, s, sign):
    w = x.shape[1]
    lane = lax.broadcasted_iota(jnp.int32, x.shape, 1)
    swapped = jnp.where((lane & 63) < 32, pltpu.roll(x, w - 32, 1), pltpu.roll(x, 32, 1))
    return x * c + swapped * (sign * s)


def _rope(x, tables, name, inverse=False, out_dtype=F32):
    c, s = tables
    fn = functools.partial(_rope_fn, sign=-1.0 if inverse else 1.0)
    return _ew(fn, [(x, True), (c, True), (s, True)], [(x.shape[1], out_dtype, "tile")], ROWS, name)[0]


SCAN_ROWS, SCAN_LANES = 256, 512


def _cmul(ar, ai, br, bi):
    return ar * br - ai * bi, ar * bi + ai * br


def _scan_chunk(xr, xi, ar, ai, reverse):
    rows = xr.shape[0]
    row = lax.broadcasted_iota(jnp.int32, xr.shape, 0)
    pr, pi = ar, ai
    k = 1
    while k < rows:
        if reverse:
            sr, si = pltpu.roll(xr, rows - k, 0), pltpu.roll(xi, rows - k, 0)
            keep = row < rows - k
        else:
            sr, si = pltpu.roll(xr, k, 0), pltpu.roll(xi, k, 0)
            keep = row >= k
        tr, ti = _cmul(pr, pi, sr, si)
        xr = xr + jnp.where(keep, tr, 0.0)
        xi = xi + jnp.where(keep, ti, 0.0)
        pr, pi = _cmul(pr, pi, pr, pi)
        k *= 2
    return xr, xi


def _s5_scan_fwd(bu, a_r, a_i, name):
    t = bu.shape[0]
    rows = min(SCAN_ROWS, t)
    nl = S5_WIDTH // SCAN_LANES

    def body(br_ref, bi_ref, ar_ref, ai_ref, or_ref, oi_ref, cr_ref, ci_ref):
        i = pl.program_id(1)

        @pl.when(i == 0)
        def _():
            cr_ref[...] = jnp.zeros_like(cr_ref)
            ci_ref[...] = jnp.zeros_like(ci_ref)

        ar, ai = ar_ref[...], ai_ref[...]
        xr, xi = br_ref[...], bi_ref[...]
        inr, ini = _cmul(ar, ai, cr_ref[...], ci_ref[...])
        first = lax.broadcasted_iota(jnp.int32, xr.shape, 0) == 0
        xr = xr + jnp.where(first, inr, 0.0)
        xi = xi + jnp.where(first, ini, 0.0)
        xr, xi = _scan_chunk(xr, xi, ar, ai, reverse=False)
        or_ref[...] = xr
        oi_ref[...] = xi
        cr_ref[...] = xr[rows - 1:rows, :]
        ci_ref[...] = xi[rows - 1:rows, :]

    blk = lambda off: pl.BlockSpec((rows, SCAN_LANES), lambda j, i: (i, j + off))
    par = pl.BlockSpec((1, SCAN_LANES), lambda j, i: (0, j))
    st_r, st_i = pl.pallas_call(
        body, name=name, grid=(nl, t // rows), in_specs=[blk(0), blk(nl), par, par], out_specs=[blk(0), blk(0)],
        out_shape=[jax.ShapeDtypeStruct((t, S5_WIDTH), F32)] * 2,
        scratch_shapes=[pltpu.VMEM((1, SCAN_LANES), F32)] * 2, compiler_params=_params(("parallel", "arbitrary")),
    )(bu, bu, a_r, a_i)
    return st_r, st_i


def _s5_scan_bwd(dst, st_r, st_i, a_r, a_i, name):
    t = dst.shape[0]
    rows = min(SCAN_ROWS, t)
    nl = S5_WIDTH // SCAN_LANES
    nc = t // rows

    def body(dr_ref, di_ref, xr_ref, xi_ref, ar_ref, ai_ref, gr_ref, gi_ref, dar_ref, dai_ref, cr_ref, ci_ref):
        i = pl.program_id(1)

        @pl.when(i == 0)
        def _():
            cr_ref[...] = jnp.zeros_like(cr_ref)
            ci_ref[...] = jnp.zeros_like(ci_ref)
            dar_ref[...] = jnp.zeros_like(dar_ref)
            dai_ref[...] = jnp.zeros_like(dai_ref)

        ar, ai = ar_ref[...], ai_ref[...]
        cr, ci = cr_ref[...], ci_ref[...]
        gr, gi = dr_ref[...], di_ref[...]
        row = lax.broadcasted_iota(jnp.int32, gr.shape, 0)
        last = row == rows - 1
        inr, ini = _cmul(ar, -ai, cr, ci)
        gr = gr + jnp.where(last, inr, 0.0)
        gi = gi + jnp.where(last, ini, 0.0)
        gr, gi = _scan_chunk(gr, gi, ar, -ai, reverse=True)
        gr_ref[...] = gr
        gi_ref[...] = gi
        nr = jnp.where(last, cr, pltpu.roll(gr, rows - 1, 0))
        ni = jnp.where(last, ci, pltpu.roll(gi, rows - 1, 0))
        xr, xi = xr_ref[...], xi_ref[...]
        dar_ref[...] += jnp.sum(nr * xr + ni * xi, axis=0, keepdims=True)
        dai_ref[...] += jnp.sum(ni * xr - nr * xi, axis=0, keepdims=True)
        cr_ref[...] = gr[0:1, :]
        ci_ref[...] = gi[0:1, :]

    blk = lambda off: pl.BlockSpec((rows, SCAN_LANES), lambda j, i: (nc - 1 - i, j + off))
    par = pl.BlockSpec((1, SCAN_LANES), lambda j, i: (0, j))
    return pl.pallas_call(
        body, name=name, grid=(nl, nc), in_specs=[blk(0), blk(nl), blk(0), blk(0), par, par],
        out_specs=[blk(0), blk(0), par, par],
        out_shape=[jax.ShapeDtypeStruct((t, S5_WIDTH), F32)] * 2 + [jax.ShapeDtypeStruct((1, S5_WIDTH), F32)] * 2,
        scratch_shapes=[pltpu.VMEM((1, SCAN_LANES), F32)] * 2, compiler_params=_params(("parallel", "arbitrary")),
    )(dst, dst, st_r, st_i, a_r, a_i)


def _s5_prep(lam_re, lam_im, log_dt, b_re, b_im, c_re, c_im, d_skip):
    dt = jnp.exp(log_dt)[:, None]
    mag = jnp.exp(lam_re * dt)
    ar, ai = mag * jnp.cos(lam_im * dt), mag * jnp.sin(lam_im * dt)
    den = lam_re * lam_re + lam_im * lam_im
    cr = ((ar - 1.0) * lam_re + ai * lam_im) / den
    ci = (ai * lam_re - (ar - 1.0) * lam_im) / den
    bbar_r = cr[..., None] * b_re - ci[..., None] * b_im
    bbar_i = cr[..., None] * b_im + ci[..., None] * b_re
    eye = jnp.eye(S5_GROUPS, dtype=F32)

    def bdiag(m):
        g, a, b = m.shape
        return (eye[:, None, :, None] * m[:, :, None, :]).reshape(g * a, g * b)

    b_cat = jnp.concatenate([bdiag(bbar_r.transpose(0, 2, 1)), bdiag(bbar_i.transpose(0, 2, 1))], axis=1)
    c_cat = jnp.concatenate([bdiag(c_re.transpose(0, 2, 1)), -bdiag(c_im.transpose(0, 2, 1))], axis=0)
    return b_cat, c_cat, d_skip.reshape(1, GROUP_WIDTH), ar.reshape(1, S5_WIDTH), ai.reshape(1, S5_WIDTH)


def _ret_consts(lgam):
    c = RET_CHUNK
    r = lax.broadcasted_iota(jnp.int32, (c, c), 0)
    m = lax.broadcasted_iota(jnp.int32, (c, c), 1)
    rel = (r - m).astype(F32)
    decay = jnp.where(rel >= 0, jnp.exp(lgam * jnp.maximum(rel, 0.0)), 0.0)
    idx = lax.broadcasted_iota(jnp.int32, (c, 1), 0).astype(F32)
    zeta = jnp.exp(lgam * (c - 1.0 - idx))
    xi = jnp.exp(lgam * (idx + 1.0))
    return decay, zeta, xi, jnp.exp(lgam * c)


def _ret_specs(t):
    qk = lambda off: pl.BlockSpec((1, t, RET_QK), lambda h: (h + off, 0, 0))
    col = lambda off: pl.BlockSpec((t, RET_V), lambda h: (0, h + off))
    return qk, col


def _ret_fwd(qk, p_ret, lgam, name):
    t = qk.shape[1]
    nck = t // RET_CHUNK
    qk_spec, col = _ret_specs(t)

    def body(lg_ref, q_ref, k_ref, v_ref, g_ref, o_ref, y_ref):
        decay, zeta, xi, gam = _ret_consts(lg_ref[pl.program_id(0)])

        def step(n, state):
            sl = pl.ds(pl.multiple_of(n * RET_CHUNK, RET_CHUNK), RET_CHUNK)
            q, k, v = q_ref[0, sl, :], k_ref[0, sl, :] * (RET_QK ** -0.5), v_ref[sl, :]
            s = _dot(q, k, "nt") * decay
            o = _dot(s, v) + _dot(q, state) * xi
            o_ref[sl, :] = o
            y_ref[sl, :] = _ret_gate_fn(o, g_ref[sl, :]).astype(y_ref.dtype)
            return gam * state + _dot(k, zeta * v, "tn")

        lax.fori_loop(0, nck, step, jnp.zeros((RET_QK, RET_V), F32))

    return pl.pallas_call(
        body, name=name, grid=(RET_HEADS,), in_specs=[_SMEM, qk_spec(0), qk_spec(RET_HEADS), col(4), col(8)],
        out_specs=[col(0), col(0)],
        out_shape=[jax.ShapeDtypeStruct((t, GROUP_WIDTH), F32), jax.ShapeDtypeStruct((t, GROUP_WIDTH), BF16)],
        compiler_params=_params(("parallel",)),
    )(lgam, qk, qk, p_ret, p_ret)


def _ret_bwd(qk, p_ret, o_all, dy, lgam, name):
    t = qk.shape[1]
    nck = t // RET_CHUNK
    qk_spec, col = _ret_specs(t)
    gate_bwd = _vjp_block(_ret_gate_fn, 2)

    def body(lg_ref, q_ref, k_ref, v_ref, g_ref, o_ref, dy_ref, dq_ref, dk_ref, dv_ref, dg_ref, st_ref):
        decay, zeta, xi, gam = _ret_consts(lg_ref[pl.program_id(0)])
        scale = RET_QK ** -0.5

        def fstep(n, state):
            sl = pl.ds(pl.multiple_of(n * RET_CHUNK, RET_CHUNK), RET_CHUNK)
            st_ref[n] = state
            return gam * state + _dot(k_ref[0, sl, :] * scale, zeta * v_ref[sl, :], "tn")

        lax.fori_loop(0, nck, fstep, jnp.zeros((RET_QK, RET_V), F32))

        def bstep(r, grad_state):
            n = nck - 1 - r
            sl = pl.ds(pl.multiple_of(n * RET_CHUNK, RET_CHUNK), RET_CHUNK)
            q, k, v = q_ref[0, sl, :], k_ref[0, sl, :] * scale, v_ref[sl, :]
            d_o, dg = gate_bwd(o_ref[sl, :], g_ref[sl, :], dy_ref[sl, :])
            dg_ref[sl, :] = dg
            s = _dot(q, k, "nt") * decay
            ds = _dot(d_o, v, "nt") * decay
            xdo = xi * d_o
            dq_ref[0, sl, :] = _dot(ds, k) + _dot(xdo, st_ref[n], "nt")
            dk_ref[0, sl, :] = (_dot(ds, q, "tn") + _dot(zeta * v, grad_state, "nt")) * scale
            dv_ref[sl, :] = _dot(s, d_o, "tn") + zeta * _dot(k, grad_state)
            return gam * grad_state + _dot(q, xdo, "tn")

        lax.fori_loop(0, nck, bstep, jnp.zeros((RET_QK, RET_V), F32))

    hd = pl.BlockSpec((1, t, RET_QK), lambda h: (h, 0, 0))
    return pl.pallas_call(
        body, name=name, grid=(RET_HEADS,),
        in_specs=[_SMEM, qk_spec(0), qk_spec(RET_HEADS), col(4), col(8), col(0), col(0)],
        out_specs=[hd, hd, col(0), col(0)],
        out_shape=[jax.ShapeDtypeStruct((RET_HEADS, t, RET_QK), F32)] * 2 + [jax.ShapeDtypeStruct((t, GROUP_WIDTH), F32)] * 2,
        scratch_shapes=[pltpu.VMEM((nck, RET_QK, RET_V), F32)], compiler_params=_params(("parallel",)),
    )(lgam, qk, qk, p_ret, p_ret, o_all, dy)


SWA_GROUP = SWA_HEADS // SWA_KV_HEADS
SWA_SCALE = SWA_HD ** -0.5


def _swa_mask(n):
    rows = SWA_GROUP * WINDOW
    r = lax.broadcasted_iota(jnp.int32, (rows, 2 * WINDOW), 0) & (WINDOW - 1)
    j = lax.broadcasted_iota(jnp.int32, (rows, 2 * WINDOW), 1)
    dist = r + WINDOW - j
    return (dist >= 0) & (dist < WINDOW) & (n * WINDOW + j - WINDOW >= 0)


def _swa_sink_rows(sink_ref, kv):
    row = lax.broadcasted_iota(jnp.int32, (SWA_GROUP * WINDOW, 1), 0)
    sink = jnp.zeros((SWA_GROUP * WINDOW, 1), F32)
    for g in range(SWA_GROUP):
        sink = jnp.where(row >= g * WINDOW, sink_ref[kv * SWA_GROUP + g], sink)
    return sink


def _swa_pad_keys(n, k_ref, v_ref, kp_ref, vp_ref):
    @pl.when(n == 0)
    def _():
        zero = jnp.zeros((WINDOW, SWA_HD), F32)
        kp_ref[0:WINDOW, :] = zero
        vp_ref[0:WINDOW, :] = zero
        kp_ref[WINDOW:, :] = k_ref[0]
        vp_ref[WINDOW:, :] = v_ref[0]


def _swa_specs(t):
    blk = lambda w: pl.BlockSpec((SWA_GROUP, WINDOW, w), lambda kv, n: (kv, n, 0))
    kspec = lambda off: pl.BlockSpec((1, t, SWA_HD), lambda kv, n: (SWA_HEADS + off + kv, 0, 0))
    return blk, kspec


def _swa_fwd(qkv, sinks, name):
    t = qkv.shape[1]
    rows = SWA_GROUP * WINDOW
    blk, kspec = _swa_specs(t)

    def body(sink_ref, q_ref, k_ref, v_ref, o_ref, lse_ref, kp_ref, vp_ref):
        kv, n = pl.program_id(0), pl.program_id(1)
        _swa_pad_keys(n, k_ref, v_ref, kp_ref, vp_ref)
        win = pl.ds(pl.multiple_of(n * WINDOW, WINDOW), 2 * WINDOW)
        sink = _swa_sink_rows(sink_ref, kv)
        s = _dot(q_ref[...].reshape(rows, SWA_HD), kp_ref[win, :], "nt") * SWA_SCALE
        s = jnp.where(_swa_mask(n), s, NEG)
        m = jnp.maximum(jnp.max(s, axis=-1, keepdims=True), sink)
        p = jnp.exp(s - m)
        den = jnp.sum(p, axis=-1, keepdims=True) + jnp.exp(sink - m)
        o_ref[...] = _dot(p / den, vp_ref[win, :]).reshape(SWA_GROUP, WINDOW, SWA_HD)
        lse_ref[...] = (m + jnp.log(den)).reshape(SWA_GROUP, WINDOW, 1)

    return pl.pallas_call(
        body, name=name, grid=(SWA_KV_HEADS, t // WINDOW),
        in_specs=[_SMEM, blk(SWA_HD), kspec(0), kspec(SWA_KV_HEADS)], out_specs=[blk(SWA_HD), blk(1)],
        out_shape=[jax.ShapeDtypeStruct((SWA_HEADS, t, SWA_HD), F32), jax.ShapeDtypeStruct((SWA_HEADS, t, 1), F32)],
        scratch_shapes=[pltpu.VMEM((t + WINDOW, SWA_HD), F32)] * 2, compiler_params=_params(("parallel", "arbitrary")),
    )(sinks, qkv, qkv, qkv)


def _swa_bwd(qkv, o, lse, d_o, sinks, name):
    t = qkv.shape[1]
    nb = t // WINDOW
    rows = SWA_GROUP * WINDOW
    blk, kspec = _swa_specs(t)

    def body(sink_ref, q_ref, k_ref, v_ref, o_ref, lse_ref, do_ref, dq_ref, dk_ref, dv_ref, dsink_ref,
             kp_ref, vp_ref, dkp_ref, dvp_ref):
        kv, n = pl.program_id(0), pl.program_id(1)
        _swa_pad_keys(n, k_ref, v_ref, kp_ref, vp_ref)

        @pl.when(n == 0)
        def _():
            dkp_ref[...] = jnp.zeros_like(dkp_ref)
            dvp_ref[...] = jnp.zeros_like(dvp_ref)
            dsink_ref[...] = jnp.zeros_like(dsink_ref)

        win = pl.ds(pl.multiple_of(n * WINDOW, WINDOW), 2 * WINDOW)
        sink = _swa_sink_rows(sink_ref, kv)
        q, dout = q_ref[...].reshape(rows, SWA_HD), do_ref[...].reshape(rows, SWA_HD)
        lse_n = lse_ref[...].reshape(rows, 1)
        s = _dot(q, kp_ref[win, :], "nt") * SWA_SCALE
        s = jnp.where(_swa_mask(n), s, NEG)
        p = jnp.exp(s - lse_n)
        delta = jnp.sum(dout * o_ref[...].reshape(rows, SWA_HD), axis=-1, keepdims=True)
        ds = p * (_dot(dout, vp_ref[win, :], "nt") - delta)
        dq_ref[...] = (_dot(ds, kp_ref[win, :]) * SWA_SCALE).reshape(SWA_GROUP, WINDOW, SWA_HD)
        dkp_ref[win, :] += _dot(ds, q, "tn") * SWA_SCALE
        dvp_ref[win, :] += _dot(p, dout, "tn")
        term = jnp.exp(sink - lse_n) * delta
        head = lax.broadcasted_iota(jnp.int32, (SWA_GROUP, 128), 0)
        acc = jnp.zeros((SWA_GROUP, 128), F32)
        for g in range(SWA_GROUP):
            acc = jnp.where(head == g, jnp.sum(term[g * WINDOW:(g + 1) * WINDOW], axis=0, keepdims=True), acc)
        dsink_ref[0] -= acc

        @pl.when(n == nb - 1)
        def _():
            dk_ref[0] = dkp_ref[WINDOW:, :]
            dv_ref[0] = dvp_ref[WINDOW:, :]

    kout = pl.BlockSpec((1, t, SWA_HD), lambda kv, n: (kv, 0, 0))
    dq, dk, dv, dsink = pl.pallas_call(
        body, name=name, grid=(SWA_KV_HEADS, nb),
        in_specs=[_SMEM, blk(SWA_HD), kspec(0), kspec(SWA_KV_HEADS), blk(SWA_HD), blk(1), blk(SWA_HD)],
        out_specs=[blk(SWA_HD), kout, kout, pl.BlockSpec((1, SWA_GROUP, 128), lambda kv, n: (kv, 0, 0))],
        out_shape=[jax.ShapeDtypeStruct((SWA_HEADS, t, SWA_HD), F32), jax.ShapeDtypeStruct((SWA_KV_HEADS, t, SWA_HD), F32),
                   jax.ShapeDtypeStruct((SWA_KV_HEADS, t, SWA_HD), F32), jax.ShapeDtypeStruct((SWA_KV_HEADS, SWA_GROUP, 128), F32)],
        scratch_shapes=[pltpu.VMEM((t + WINDOW, SWA_HD), F32)] * 4, compiler_params=_params(("parallel", "arbitrary")),
    )(sinks, qkv, qkv, qkv, o, lse, d_o)
    return jnp.concatenate([dq, dk, dv], axis=0), dsink[:, :, 0].reshape(SWA_HEADS)


MLA_SCALE = (MLA_NOPE + MLA_ROPE) ** -0.5
MLA_TILE = 512
MLA_KEY_TILE = 512
MLA_BWD_TILE = 512


def _mla_diag(s):
    r = lax.broadcasted_iota(jnp.int32, s.shape, 0)
    c = lax.broadcasted_iota(jnp.int32, s.shape, 1)
    return jnp.where(c <= r, s, NEG)


def _mla_specs(t, tile):
    whole = lambda w, off: pl.BlockSpec((t, w), lambda h, i: (0, 2 * h + off))
    head = lambda w: pl.BlockSpec((1, t, w), lambda h, i: (h, 0, 0))
    key_rope = pl.BlockSpec((1, t, MLA_ROPE), lambda h, i: (MLA_HEADS, 0, 0))
    tile_of = lambda w: pl.BlockSpec((1, tile, w), lambda h, i: (h, i, 0))
    return whole, head, key_rope, tile_of


def _mla_attend(qn, rot, kv, name):
    t = qn.shape[1]
    tile = min(MLA_TILE, t)
    ktile = min(MLA_KEY_TILE, t)
    ratio = ktile // tile
    whole, head, key_rope, tile_of = _mla_specs(t, tile)

    def body(qn_ref, qr_ref, kn_ref, kr_ref, v_ref, o_ref, lse_ref, m_ref, l_ref, acc_ref):
        i = pl.program_id(1)
        qn_b, qr_b = qn_ref[0], qr_ref[0]

        def rows(j):
            return pl.ds(pl.multiple_of(j * ktile, ktile), ktile)

        def scores(j):
            return (_dot(qn_b, kn_ref[rows(j), :], "nt") + _dot(qr_b, kr_ref[0, rows(j), :], "nt")) * MLA_SCALE

        def causal(s, j):
            qpos = i * tile + lax.broadcasted_iota(jnp.int32, s.shape, 0)
            kpos = j * ktile + lax.broadcasted_iota(jnp.int32, s.shape, 1)
            return jnp.where(kpos <= qpos, s, NEG)

        def update(s, j):
            m_old = m_ref[...]
            m_new = jnp.maximum(m_old, jnp.max(s, axis=-1, keepdims=True))
            alpha = jnp.exp(m_old - m_new)
            p = jnp.exp(s - m_new)
            l_ref[...] = alpha * l_ref[...] + jnp.sum(p, axis=-1, keepdims=True)
            acc_ref[...] = alpha * acc_ref[...] + _dot(p, v_ref[rows(j), :])
            m_ref[...] = m_new

        m_ref[...] = jnp.full_like(m_ref, NEG)
        l_ref[...] = jnp.zeros_like(l_ref)
        acc_ref[...] = jnp.zeros_like(acc_ref)

        def step(j, s_cur):
            s_next = scores(j + 1)
            update(s_cur, j)
            return s_next

        last = i // ratio
        s_last = lax.fori_loop(0, last, step, scores(0))
        update(causal(s_last, last), last)
        o_ref[...] = acc_ref[...] / l_ref[...]
        lse_ref[0] = m_ref[...] + jnp.log(l_ref[...])

    return pl.pallas_call(
        body, name=name, grid=(MLA_HEADS, t // tile),
        in_specs=[tile_of(MLA_NOPE), tile_of(MLA_ROPE), whole(MLA_NOPE, 0), key_rope, whole(MLA_V, 1)],
        out_specs=[pl.BlockSpec((tile, MLA_V), lambda h, i: (i, h)), tile_of(1)],
        out_shape=[jax.ShapeDtypeStruct((t, GROUP_WIDTH), F32), jax.ShapeDtypeStruct((MLA_HEADS, t, 1), F32)],
        scratch_shapes=[pltpu.VMEM((tile, 1), F32), pltpu.VMEM((tile, 1), F32), pltpu.VMEM((tile, MLA_V), F32)],
        compiler_params=_params(("parallel", "parallel")),
    )(qn, rot, kv, rot, kv)


def _mla_delta(o, d_o, name):
    t = o.shape[0]
    tt = min(ROWS, t)

    def body(o_ref, do_ref, d_ref):
        d_ref[0] = jnp.sum(o_ref[...] * do_ref[...], axis=-1, keepdims=True)

    blk = pl.BlockSpec((tt, MLA_V), lambda h, i: (i, h))
    return pl.pallas_call(
        body, name=name, grid=(MLA_HEADS, t // tt), in_specs=[blk, blk],
        out_specs=pl.BlockSpec((1, tt, 1), lambda h, i: (h, i, 0)), out_shape=jax.ShapeDtypeStruct((MLA_HEADS, t, 1), F32),
        compiler_params=_params(("parallel", "parallel")),
    )(o, d_o)


def _mla_attend_bwd(qn, rot, kv, lse, delta, d_o, name):
    t = qn.shape[1]
    tile = min(MLA_BWD_TILE, t)
    nt = t // tile
    whole, head, key_rope, tile_of = _mla_specs(t, tile)

    def body(qn_ref, qr_ref, kn_ref, kr_ref, v_ref, lse_ref, dl_ref, do_ref, dqn_ref, dqr_ref, dkn_ref, dv_ref, dkr_ref):
        j = pl.program_id(1)

        @pl.when(j == 0)
        def _():
            dqn_ref[...] = jnp.zeros_like(dqn_ref)
            dqr_ref[...] = jnp.zeros_like(dqr_ref)

        dkn_ref[...] = jnp.zeros_like(dkn_ref)
        dv_ref[...] = jnp.zeros_like(dv_ref)
        dkr_ref[...] = jnp.zeros_like(dkr_ref)
        kn_b, kr_b, v_b = kn_ref[...], kr_ref[0], v_ref[...]

        def block(i, diagonal):
            sl = pl.ds(pl.multiple_of(i * tile, tile), tile)
            qn_b, qr_b, dout = qn_ref[0, sl, :], qr_ref[0, sl, :], do_ref[sl, :]
            s = (_dot(qn_b, kn_b, "nt") + _dot(qr_b, kr_b, "nt")) * MLA_SCALE
            if diagonal:
                s = _mla_diag(s)
            p = jnp.exp(s - lse_ref[0, sl, :])
            ds = p * (_dot(dout, v_b, "nt") - dl_ref[0, sl, :]) * MLA_SCALE
            dv_ref[...] += _dot(p, dout, "tn")
            dkn_ref[...] += _dot(ds, qn_b, "tn")
            dkr_ref[0] += _dot(ds, qr_b, "tn")
            dqn_ref[0, sl, :] += _dot(ds, kn_b)
            dqr_ref[0, sl, :] += _dot(ds, kr_b)

        block(j, True)

        def step(i, carry):
            block(i, False)
            return carry

        lax.fori_loop(j + 1, nt, step, 0)

    key_tile = lambda w, off: pl.BlockSpec((tile, w), lambda h, j: (j, 2 * h + off))
    out_tile = pl.BlockSpec((tile, MLA_V), lambda h, j: (j, h))
    return pl.pallas_call(
        body, name=name, grid=(MLA_HEADS, nt),
        in_specs=[head(MLA_NOPE), head(MLA_ROPE), key_tile(MLA_NOPE, 0), pl.BlockSpec((1, tile, MLA_ROPE), lambda h, j: (MLA_HEADS, j, 0)),
                  key_tile(MLA_V, 1), head(1), head(1), pl.BlockSpec((t, MLA_V), lambda h, j: (0, h))],
        out_specs=[head(MLA_NOPE), head(MLA_ROPE), out_tile, out_tile, tile_of(MLA_ROPE)],
        out_shape=[jax.ShapeDtypeStruct((MLA_HEADS, t, MLA_NOPE), F32), jax.ShapeDtypeStruct((MLA_HEADS, t, MLA_ROPE), F32),
                   jax.ShapeDtypeStruct((t, MLA_HEADS * MLA_NOPE), F32), jax.ShapeDtypeStruct((t, MLA_HEADS * MLA_V), F32),
                   jax.ShapeDtypeStruct((MLA_HEADS, t, MLA_ROPE), F32)],
        compiler_params=_params(("parallel", "arbitrary")),
    )(qn, rot, kv, rot, kv, lse, delta, d_o)


def _place():
    return lax.axis_index("x"), lax.axis_index("y"), lax.axis_index("c")


def _all_gather(arrs, name):
    n = len(arrs)

    def body(*refs):
        x_refs, o_refs = refs[:n], refs[n:2 * n]
        send_sems, recv_sems, local_sems = refs[2 * n:]
        x, y, c = _place()
        me, sibling = (x, y, c), (x, y, 1 - c)
        chips = [(1 - x, y), (x, 1 - y), (1 - x, 1 - y)]

        def slot(a, p):
            return o_refs[a].at[4 * p[0] + 2 * p[1] + p[2]]

        def copy(a, k, block, to, src=None):
            return pltpu.make_async_remote_copy(
                src_ref=slot(a, block) if src is None else src, dst_ref=slot(a, block),
                send_sem=send_sems.at[a, k], recv_sem=recv_sems.at[a, k], device_id=to, device_id_type=MESH_ID)

        mine = [pltpu.make_async_copy(x_refs[a], slot(a, me), local_sems.at[a]) for a in range(n)]
        for cp in mine:
            cp.start()
        first = []
        for a in range(n):
            first.append(copy(a, 0, me, sibling, src=x_refs[a]))
            first += [copy(a, 1 + j, me, (*chip, c), src=x_refs[a]) for j, chip in enumerate(chips)]
        for cp in first:
            cp.start()
        passed = []
        for j, chip in enumerate(chips):
            for a in range(n):
                copy(a, 1 + j, (*chip, c), me).wait_recv()
                cp = copy(a, 4 + j, (*chip, c), sibling)
                cp.start()
                passed.append(cp)
        for a in range(n):
            copy(a, 0, sibling, me).wait_recv()
            for j, chip in enumerate(chips):
                copy(a, 4 + j, (*chip, 1 - c), me).wait_recv()
        for cp in first + passed:
            cp.wait_send()
        for cp in mine:
            cp.wait()

    return pl.pallas_call(
        body, name=name, in_specs=[_ANY] * n, out_specs=[_ANY] * n,
        out_shape=[jax.ShapeDtypeStruct((N_DEV,) + a.shape, a.dtype) for a in arrs],
        scratch_shapes=[pltpu.SemaphoreType.DMA((n, 7)), pltpu.SemaphoreType.DMA((n, 7)), pltpu.SemaphoreType.DMA((n,))],
    )(*arrs)


def _scatter_core(grads, name):
    n = len(grads)

    def body(*refs):
        g_refs, got_refs = refs[:n], refs[n:2 * n]
        send_sems, recv_sems = refs[2 * n:]
        x, y, c = _place()
        sends = [pltpu.make_async_remote_copy(
            src_ref=g_refs[a].at[2 * q + 1 - c], dst_ref=got_refs[a].at[q], send_sem=send_sems.at[a, q],
            recv_sem=recv_sems.at[a, q], device_id=(x, y, 1 - c), device_id_type=MESH_ID) for a in range(n) for q in range(4)]
        for cp in sends:
            cp.start()
        for cp in sends:
            cp.wait()

    return pl.pallas_call(
        body, name=name, in_specs=[_ANY] * n, out_specs=[_ANY] * n,
        out_shape=[jax.ShapeDtypeStruct((4,) + g.shape[1:], g.dtype) for g in grads],
        scratch_shapes=[pltpu.SemaphoreType.DMA((n, 4)), pltpu.SemaphoreType.DMA((n, 4))],
    )(*grads)


def _scatter_chips(parts, name):
    n = len(parts)

    def body(*refs):
        p_refs, o_refs = refs[:n], refs[n:2 * n]
        send_sems, recv_sems = refs[2 * n:]
        x, y, c = _place()
        chips = [(1 - x, y), (x, 1 - y), (1 - x, 1 - y)]
        sends = [pltpu.make_async_remote_copy(
            src_ref=p_refs[a].at[2 * px + py], dst_ref=o_refs[a].at[j], send_sem=send_sems.at[a, j],
            recv_sem=recv_sems.at[a, j], device_id=(px, py, c), device_id_type=MESH_ID)
            for a in range(n) for j, (px, py) in enumerate(chips)]
        for cp in sends:
            cp.start()
        for cp in sends:
            cp.wait()

    return pl.pallas_call(
        body, name=name, in_specs=[_ANY] * n, out_specs=[_ANY] * n,
        out_shape=[jax.ShapeDtypeStruct((3,) + p.shape[1:], p.dtype) for p in parts],
        scratch_shapes=[pltpu.SemaphoreType.DMA((n, 3)), pltpu.SemaphoreType.DMA((n, 3))],
    )(*parts)


ALL_PEERS = [(k >> 2 & 1, k >> 1 & 1, k & 1) for k in range(1, N_DEV)]
CHIP_PEERS = [(1, 0, 0), (0, 1, 0), (1, 1, 0)]
_HBM = pl.BlockSpec(memory_space=pltpu.HBM)
_SEM = pl.BlockSpec(memory_space=pltpu.SEMAPHORE)
_EFFECT = pltpu.SideEffectType.DATAFLOW_SIDE_EFFECTING


def _push_copies(src_refs, land_refs, send_sems, recv_sems, peers, src_of, slot_of):
    place = _place()
    flip = lambda v, f: 1 - v if f else v
    return [pltpu.make_async_remote_copy(
        src_ref=src_of(src_refs[a], k), dst_ref=land_refs[a].at[slot_of(k)], send_sem=send_sems[a], recv_sem=recv_sems[a],
        device_id=tuple(flip(v, f) for v, f in zip(place, peer)), device_id_type=MESH_ID)
        for a in range(len(src_refs)) for k, peer in enumerate(peers)]


def _push_start(srcs, land_shapes, peers, src_of, slot_of, name):
    n = len(srcs)

    def body(*refs):
        src_refs, land_refs = refs[:n], refs[n:2 * n]
        send_sems, recv_sems = refs[2 * n:3 * n], refs[3 * n:4 * n]
        token = refs[-1]
        for cp in _push_copies(src_refs, land_refs, send_sems, recv_sems, peers, src_of, slot_of):
            cp.start()
        token[...] = jnp.zeros_like(token)

    sems = [pltpu.SemaphoreType.DMA(())] * (2 * n)
    lands = [pltpu.with_memory_space_constraint(lax.empty(s.shape, s.dtype), pltpu.HBM) for s in land_shapes]
    res = pl.pallas_call(
        body, name=name, in_specs=[_HBM] * (2 * n), out_specs=[_SEM] * (2 * n) + [_HBM] * (2 * n) + [pl.BlockSpec(memory_space=pltpu.VMEM)],
        out_shape=sems + [pltpu.HBM(s.shape, s.dtype) for s in srcs] + [pltpu.HBM(s.shape, s.dtype) for s in land_shapes]
        + [jax.ShapeDtypeStruct((8, 128), F32)],
        input_output_aliases={i: 2 * n + i for i in range(2 * n)},
        compiler_params=pltpu.CompilerParams(has_side_effects=_EFFECT),
    )(*[pltpu.with_memory_space_constraint(s, pltpu.HBM) for s in srcs], *lands)
    return list(res[:n]), list(res[n:2 * n]), list(res[2 * n:3 * n]), list(res[3 * n:4 * n]), res[-1]


def _push_wait(send_sems, recv_sems, srcs, lands, after, peers, src_of, slot_of, name):
    n = len(srcs)

    def body(*refs):
        src_refs, land_refs = refs[:n], refs[n:2 * n]
        s_sems, r_sems = refs[2 * n:3 * n], refs[3 * n:4 * n]
        copies = _push_copies(src_refs, land_refs, s_sems, r_sems, peers, src_of, slot_of)
        for cp in copies:
            cp.wait_send()
        for cp in copies:
            cp.wait_recv()

    res = pl.pallas_call(
        body, name=name, in_specs=[_HBM] * (2 * n) + [_SEM] * (2 * n) + [_ANY], out_specs=[_HBM] * (2 * n),
        out_shape=[pltpu.HBM(s.shape, s.dtype) for s in srcs] + [pltpu.HBM(s.shape, s.dtype) for s in lands],
        input_output_aliases={i: i for i in range(2 * n)},
        compiler_params=pltpu.CompilerParams(has_side_effects=_EFFECT),
    )(*srcs, *lands, *send_sems, *recv_sems, after)
    return list(res[:n]), list(res[n:])


def _pick_sum(picked, rest, index, pick_of, out_dtype, name):
    nq, r, cdim = rest.shape
    one = nq == 3
    tr = _row_tile(r, 512, 16)
    tc = 512 if cdim % 512 == 0 else cdim
    grid = (1 if one else nq, r // tr, cdim // tc)

    def body(i_ref, p_ref, r_ref, o_ref):
        acc = p_ref[0].astype(F32)
        if one:
            for j in range(3):
                acc = acc + r_ref[j].astype(F32)
            o_ref[...] = acc.astype(out_dtype)
        else:
            o_ref[0] = (acc + r_ref[0].astype(F32)).astype(out_dtype)

    spec = pltpu.PrefetchScalarGridSpec(
        num_scalar_prefetch=1, grid=grid,
        in_specs=[pl.BlockSpec((1, tr, tc), lambda q, i, j, i_ref: (pick_of(q, i_ref[0]), i, j)),
                  pl.BlockSpec((3, tr, tc), lambda q, i, j, i_ref: (0, i, j)) if one else pl.BlockSpec((1, tr, tc), lambda q, i, j, i_ref: (q, i, j))],
        out_specs=pl.BlockSpec((tr, tc), lambda q, i, j, i_ref: (i, j)) if one else pl.BlockSpec((1, tr, tc), lambda q, i, j, i_ref: (q, i, j)))
    return pl.pallas_call(
        body, name=name, grid_spec=spec,
        out_shape=jax.ShapeDtypeStruct((r, cdim) if one else (nq, r, cdim), out_dtype),
        compiler_params=_params(("parallel", "parallel", "parallel")),
    )(index.astype(jnp.int32).reshape(1), picked, rest)


def _adamw_fn(w, g, m, v):
    m = ADAM_B1 * m + (1.0 - ADAM_B1) * g
    v = ADAM_B2 * v + (1.0 - ADAM_B2) * jnp.square(g)
    m_hat = m / (1.0 - ADAM_B1 ** ADAM_STEP)
    v_hat = v / (1.0 - ADAM_B2 ** ADAM_STEP)
    delta = -ADAM_LR * (m_hat / (jnp.sqrt(v_hat) + ADAM_EPS) + ADAM_WD * w)
    return delta, m, v


def _as2d(a):
    return a.reshape(-1, a.shape[-1])


def _adamw_shard(w, g, m, v, name):
    shape = w.shape
    ins = [_as2d(a) for a in (w, g, m, v)]
    cols = ins[0].shape[1]
    outs = _ew(_adamw_fn, [(a, True) for a in ins], [(cols, F32, "tile")] * 3, 256, name)
    return [o.reshape(shape) for o in outs]


def _adamw_small(ws, gs, ms, vs, name):
    shapes = [w.shape for w in ws]
    flat = lambda a: a.reshape(-1, 128) if a.size % 128 == 0 else a.reshape(1, -1)
    ins = [flat(a) for grp in zip(ws, gs, ms, vs) for a in grp]
    k = len(ws)

    def fn(*vals):
        out = []
        for i in range(k):
            out += list(_adamw_fn(*vals[4 * i:4 * i + 4]))
        return out

    outs = _whole(fn, ins, [(ins[4 * (i // 3)].shape, F32) for i in range(3 * k)], name)
    deltas = [outs[3 * i].reshape(shapes[i]) for i in range(k)]
    new_m = [outs[3 * i + 1].reshape(shapes[i]) for i in range(k)]
    new_v = [outs[3 * i + 2].reshape(shapes[i]) for i in range(k)]
    return deltas, new_m, new_v


def _sum8(stacked, name):
    def fn(a):
        s = a[0:1]
        for i in range(1, N_DEV):
            s = s + a[i:i + 1]
        return s
    w = stacked.shape[1]
    tw = 8192
    if w % tw:
        return _whole(fn, [stacked], [((1, w), F32)], name)[0]

    def body(a_ref, o_ref):
        o_ref[...] = fn(a_ref[...])

    return pl.pallas_call(body, name=name, grid=(w // tw,), in_specs=[pl.BlockSpec((N_DEV, tw), lambda i: (0, i))],
                          out_specs=pl.BlockSpec((1, tw), lambda i: (0, i)), out_shape=jax.ShapeDtypeStruct((1, w), F32))(stacked)


ROWS = 512


def _split_heads(p, nh):
    t = p.shape[0]
    return p.reshape(t, nh, p.shape[1] // nh).transpose(1, 0, 2)


def _merge_heads(p):
    nh, t, d = p.shape
    return p.transpose(1, 0, 2).reshape(t, nh * d)


def _layer_fwd(h, mod, w, small, rope, l):
    sh1, sc1, gt1, sh2, sc2, gt2 = mod
    rope_ret, rope_mla = rope
    t = h.shape[0]
    nm = lambda s: f"l{l}_{s}"
    a1 = _ew(_norm_mod_fn, [(h, True), (small["norm1_g"], False), (sc1, False), (sh1, False)], [(D_MODEL, BF16, "tile")], ROWS, nm("norm1"))[0]
    p_s5 = _mm(a1, w["w_in_t"], "nt", 512, 512, 2048, name=nm("proj_s5"), n=512)
    p_ret = _mm(a1, w["w_in_t"], "nt", 512, 512, 2048, name=nm("proj_ret"), b_off=1, n=1536)
    p_swa = _mm(a1, w["w_in_t"], "nt", 512, 256, 2048, name=nm("proj_swa"), b_off=8, n=768)
    p_mla = _mm(a1, w["w_in_t"][2816:], "nt", 512, 576, 2048, name=nm("proj_mla"))
    b_cat, c_cat, dskip, a_r, a_i = small["s5"]
    bu = _mm(p_s5, b_cat, "nn", 512, 1024, 512, name=nm("s5_bu"))
    st_r, st_i = _s5_scan_fwd(bu, a_r, a_i, nm("s5_scan"))
    ypre = _mm(st_r, c_cat[:S5_WIDTH], "nn", 512, 512, 1024, name=nm("s5_y"), pair=(st_i, c_cat[S5_WIDTH:]))
    z = _ew(_s5_act_fn, [(ypre, True), (p_s5, True), (dskip, False)], [(GROUP_WIDTH, F32, "tile")], ROWS, nm("s5_act"))[0]
    zz = _mm(z, w["glu_w"], "nn", 512, 512, 512, name=nm("s5_zz"))
    y_s5 = _ew(_s5_glu_fn, [(z, True), (zz, True), (small["s5_glu_b"], False)], [(GROUP_WIDTH, BF16, "tile")], ROWS, nm("s5_glu"))[0]
    qk_ret = _split_heads(_rope(p_ret[:, :2 * RET_HEADS * RET_QK], rope_ret, nm("ret_rope")), 2 * RET_HEADS)
    o_ret, y_ret = _ret_fwd(qk_ret, p_ret, small["ret_lgam"], nm("ret"))
    qkv_swa = _split_heads(p_swa, 12)
    o_swa, lse_swa = _swa_fwd(qkv_swa, small["swa_sinks"], nm("swa"))
    y_swa = _merge_heads(o_swa).astype(BF16)
    cq, ckv, kr = p_mla[:, :MLA_Q_RANK], p_mla[:, MLA_Q_RANK:MLA_Q_RANK + MLA_KV_RANK], p_mla[:, MLA_Q_RANK + MLA_KV_RANK:]
    cqn = _ew(_rms_gain_fn, [(cq, True), (small["mla_q_norm"], False)], [(MLA_Q_RANK, BF16, "tile")], ROWS, nm("mla_qnorm"))[0]
    ckvn = _ew(_rms_gain_fn, [(ckv, True), (small["mla_kv_norm"], False)], [(MLA_KV_RANK, BF16, "tile")], ROWS, nm("mla_kvnorm"))[0]
    q_full = _mm(cqn, w["w_uq_t"], "nt", 512, 768, 384, name=nm("mla_q"))
    kv_full = _mm(ckvn, w["w_ukv_t"], "nt", 512, 1024, 128, BF16, name=nm("mla_kv"))
    nq = q_full.shape[1]
    roped = _rope(jnp.concatenate([q_full, kr, jnp.zeros_like(kr)], axis=1), rope_mla, nm("mla_rope"), out_dtype=BF16)
    q4 = roped[:, :nq].reshape(t, MLA_HEADS, MLA_NOPE + MLA_ROPE)
    qn = q4[:, :, :MLA_NOPE].transpose(1, 0, 2)
    rot = jnp.concatenate([q4[:, :, MLA_NOPE:].transpose(1, 0, 2), roped[None, :, nq:nq + MLA_ROPE]], axis=0)
    o_mla, lse_mla = _mla_attend(qn, rot, kv_full, nm("mla"))
    cat = jnp.concatenate([y_s5, y_ret, y_swa, o_mla.astype(BF16)], axis=1)
    mixed = _mm(cat, w["w_out"], "nn", 512, 1024, 2048, name=nm("out_proj"))
    h1 = _ew(_gate_add_fn, [(h, True), (mixed, True), (gt1, False)], [(D_MODEL, F32, "tile")], ROWS, nm("res1"))[0]
    a2 = _ew(_norm_mod_fn, [(h1, True), (small["norm2_g"], False), (sc2, False), (sh2, False)], [(D_MODEL, BF16, "tile")], ROWS, nm("norm2"))[0]
    hid, act = _mm(a2, w["w1_t"], "nt", 1024, 1024, 2048, name=nm("mlp1"), epi=lambda acc: (acc, _relu2_fn(acc)), epi_outs=[F32, BF16])
    mo = _mm(act, w["w2"], "nn", 1024, 1024, 2048, name=nm("mlp2"))
    h2 = _ew(_gate_add_fn, [(h1, True), (mo, True), (gt2, False)], [(D_MODEL, F32, "tile")], ROWS, nm("res2"))[0]
    saved = dict(h=h, a1=a1, p_s5=p_s5, p_ret=p_ret, st_r=st_r, st_i=st_i, ypre=ypre, z=z, zz=zz, qk_ret=qk_ret, o_ret=o_ret,
                 qkv_swa=qkv_swa, o_swa=o_swa, lse_swa=lse_swa, cq=cq, ckv=ckv, cqn=cqn, ckvn=ckvn, qn=qn, rot=rot,
                 kv_full=kv_full, o_mla=o_mla, lse_mla=lse_mla, cat=cat, mixed=mixed, h1=h1, a2=a2, hid=hid, act=act, mo=mo)
    return h2, saved


def _layer_bwd(dh2, mod, w, small, rope, s, l, after_mlp=None):
    sh1, sc1, gt1, sh2, sc2, gt2 = mod
    rope_ret, rope_mla = rope
    t = dh2.shape[0]
    nm = lambda n: f"l{l}_{n}_bwd"
    gb, gs = {}, {}
    row = (D_MODEL, F32, "acc")
    dmo, dgt2 = _ew(lambda d, y, gt: (d * gt, jnp.sum(d * y, axis=0, keepdims=True)),
                    [(dh2, True), (s["mo"], True), (gt2, False)], [(D_MODEL, BF16, "tile"), row], ROWS, nm("res2"))
    dhid = _mm(dmo, w["w2"], "nt", 1024, 1024, 2048, name=nm("mlp2_x"), epi=lambda acc, x: (acc * 2.0 * jnp.maximum(x, 0.0),),
               epi_ins=[s["hid"]], epi_outs=[BF16])[0]
    gb["w2"] = _mm(s["act"], dmo, "tn", 1024, 1024, 1024, BF16, name=nm("mlp2_w"))
    da2 = _mm(dhid, w["w1_t"], "nn", 1024, 1024, 2048, name=nm("mlp1_x"))
    gb["w1_t"] = _mm(dhid, s["a2"], "tn", 1024, 1024, 1024, BF16, name=nm("mlp1_w"))
    if after_mlp is not None:
        gt1 = gt1 + after_mlp(gb)

    def norm_bwd(hh, g, sc, sh, da, dres):
        dh_, dg, dsc, dsh = _vjp_block(_norm_mod_fn, 4)(hh, g, sc, sh, da)
        return dh_ + dres, dg, dsc, dsh

    dh1, gs["norm2_g"], dsc2, dsh2 = _ew(norm_bwd, [(s["h1"], True), (small["norm2_g"], False), (sc2, False), (sh2, False), (da2, True), (dh2, True)],
                                         [(D_MODEL, F32, "tile"), row, row, row], ROWS, nm("norm2"))
    dmixed, dgt1 = _ew(lambda d, y, gt: (d * gt, jnp.sum(d * y, axis=0, keepdims=True)),
                       [(dh1, True), (s["mixed"], True), (gt1, False)], [(D_MODEL, BF16, "tile"), row], ROWS, nm("res1"))
    dcat = _mm(dmixed, w["w_out"], "nt", 512, 1024, 2048, name=nm("out_proj_x"))
    gb["w_out"] = _mm(s["cat"], dmixed, "tn", 1024, 1024, 1024, BF16, name=nm("out_proj_w"))
    dy_s5, dy_ret, dy_swa, dy_mla = (dcat[:, i * GROUP_WIDTH:(i + 1) * GROUP_WIDTH] for i in range(4))
    b_cat, c_cat, dskip, a_r, a_i = small["s5"]
    gw = (GROUP_WIDTH, F32, "tile")
    gacc = (GROUP_WIDTH, F32, "acc")
    dz_a, dzz, gs["s5_glu_b"] = _ew(_vjp_block(_s5_glu_fn, 3), [(s["z"], True), (s["zz"], True), (small["s5_glu_b"], False), (dy_s5, True)],
                                    [gw, gw, gacc], ROWS, nm("s5_glu"))
    dz_b = _mm(dzz, w["glu_w"], "nt", 512, 512, 512, name=nm("s5_zz_x"))
    gb["glu_w"] = _mm(s["z"], dzz, "tn", 512, 512, 1024, BF16, name=nm("s5_zz_w"))

    def act_bwd(ypre, u, dsk, dza, dzb):
        return _vjp_block(_s5_act_fn, 3)(ypre, u, dsk, dza + dzb)

    dypre, du_a, g_dskip = _ew(act_bwd, [(s["ypre"], True), (s["p_s5"], True), (dskip, False), (dz_a, True), (dz_b, True)],
                               [gw, gw, gacc], ROWS, nm("s5_act"))
    dst = _mm(dypre, c_cat, "nt", 512, 1024, 512, name=nm("s5_y_x"))
    g_ccat = jnp.concatenate([_mm(s["st_r"], dypre, "tn", 1024, 512, 1024, name=nm("s5_y_w_re")),
                              _mm(s["st_i"], dypre, "tn", 1024, 512, 1024, name=nm("s5_y_w_im"))], axis=0)
    dbu_r, dbu_i, g_ar, g_ai = _s5_scan_bwd(dst, s["st_r"], s["st_i"], a_r, a_i, nm("s5_scan"))
    du_b = _mm(dbu_r, b_cat[:, :S5_WIDTH], "nt", 512, 512, 1024, name=nm("s5_bu_x"), pair=(dbu_i, b_cat[:, S5_WIDTH:]))
    g_bcat = jnp.concatenate([_mm(s["p_s5"], dbu_r, "tn", 512, 1024, 1024, name=nm("s5_bu_w_re")),
                              _mm(s["p_s5"], dbu_i, "tn", 512, 1024, 1024, name=nm("s5_bu_w_im"))], axis=1)
    gs["s5"] = (g_bcat, g_ccat, g_dskip, g_ar, g_ai)
    dqk_rot, dk_rot, dv_ret, dg_ret = _ret_bwd(s["qk_ret"], s["p_ret"], s["o_ret"], dy_ret, small["ret_lgam"], nm("ret"))
    dqk = _rope(_merge_heads(jnp.concatenate([dqk_rot, dk_rot], axis=0)), rope_ret, nm("ret_rope"), inverse=True)
    dqkv_swa, gs["swa_sinks"] = _swa_bwd(s["qkv_swa"], s["o_swa"], s["lse_swa"], _split_heads(dy_swa, SWA_HEADS), small["swa_sinks"], nm("swa"))
    delta = _mla_delta(s["o_mla"], dy_mla, nm("mla_delta"))
    dqn, dqr, dkn, dv_mla, dkr_heads = _mla_attend_bwd(s["qn"], s["rot"], s["kv_full"], s["lse_mla"], delta, dy_mla, nm("mla_att"))
    dkv_full = jnp.stack([dkn.reshape(t, MLA_HEADS, MLA_NOPE), dv_mla.reshape(t, MLA_HEADS, MLA_V)], axis=2).reshape(t, 2 * MLA_HEADS * MLA_NOPE)
    dkr_rot = _ew(lambda a, b, c, d: a + b + c + d, [(dkr_heads[i], True) for i in range(MLA_HEADS)], [(MLA_ROPE, F32, "tile")], ROWS, nm("mla_dkr"))[0]
    nq = MLA_HEADS * (MLA_NOPE + MLA_ROPE)
    dq_rot = jnp.concatenate([dqn.transpose(1, 0, 2), dqr.transpose(1, 0, 2)], axis=2).reshape(t, nq)
    droped = _rope(jnp.concatenate([dq_rot, dkr_rot, jnp.zeros_like(dkr_rot)], axis=1), rope_mla, nm("mla_rope"), inverse=True)
    dq_full, dkr = droped[:, :nq], droped[:, nq:nq + MLA_ROPE]
    dcqn = _mm(dq_full, w["w_uq_t"], "nn", 512, 384, 768, name=nm("mla_q_x"))
    gb["w_uq_t"] = _mm(dq_full, s["cqn"], "tn", 768, 384, 1024, BF16, name=nm("mla_q_w"))
    dckvn = _mm(dkv_full, w["w_ukv_t"], "nn", 512, 128, 1024, name=nm("mla_kv_x"))
    gb["w_ukv_t"] = _mm(dkv_full, s["ckvn"], "tn", 1024, 128, 1024, BF16, name=nm("mla_kv_w"))
    dcq, gs["mla_q_norm"] = _ew(_vjp_block(_rms_gain_fn, 2), [(s["cq"], True), (small["mla_q_norm"], False), (dcqn, True)],
                                [(MLA_Q_RANK, F32, "tile"), (MLA_Q_RANK, F32, "acc")], ROWS, nm("mla_qnorm"))
    dckv, gs["mla_kv_norm"] = _ew(_vjp_block(_rms_gain_fn, 2), [(s["ckv"], True), (small["mla_kv_norm"], False), (dckvn, True)],
                                  [(MLA_KV_RANK, F32, "tile"), (MLA_KV_RANK, F32, "acc")], ROWS, nm("mla_kvnorm"))
    du = _ew(lambda a, b: a + b, [(du_a, True), (du_b, True)], [(GROUP_WIDTH, BF16, "tile")], ROWS, nm("s5_du"))[0]
    bf = lambda a: a.astype(BF16)
    dproj = jnp.concatenate([du, bf(dqk), bf(dv_ret), bf(dg_ret), bf(_merge_heads(dqkv_swa)), bf(dcq), bf(dckv), bf(dkr)], axis=1)
    da1 = _mm(dproj, w["w_in_t"], "nn", 512, 1024, N_IN, name=nm("proj_x"))
    gb["w_in_t"] = _mm(dproj, s["a1"], "tn", N_IN, 512, 512, BF16, name=nm("proj_w"))
    dh, gs["norm1_g"], dsc1, dsh1 = _ew(norm_bwd, [(s["h"], True), (small["norm1_g"], False), (sc1, False), (sh1, False), (da1, True), (dh1, True)],
                                        [(D_MODEL, F32, "tile"), row, row, row], ROWS, nm("norm1"))
    dmod = jnp.concatenate([dsh1, dsc1, dgt1, dsh2, dsc2, dgt2], axis=1)
    return dh, gb, gs, dmod


BIG = ("w_in_t", "w1_t", "w_uq_t", "w_ukv_t", "w_out", "w2", "glu_w")
MLP_BIG = ("w1_t", "w2")
S5_NAMES = ("s5_lambda_re", "s5_lambda_im", "s5_log_dt", "s5_b_re", "s5_b_im", "s5_c_re", "s5_c_im", "s5_d")


def kernel(x, c, norm1_g, norm2_g, ada_w, ada_b, w_in, s5_lambda_re, s5_lambda_im, s5_log_dt, s5_b_re, s5_b_im, s5_c_re, s5_c_im, s5_d, s5_glu_w, s5_glu_b, swa_sinks, mla_q_norm, mla_kv_norm, mla_w_uq, mla_w_ukv, w_out, mlp_w1, mlp_w2, final_norm_g, loss_target, m_norm1_g, m_norm2_g, m_ada_w, m_ada_b, m_w_in, m_s5_lambda_re, m_s5_lambda_im, m_s5_log_dt, m_s5_b_re, m_s5_b_im, m_s5_c_re, m_s5_c_im, m_s5_d, m_s5_glu_w, m_s5_glu_b, m_swa_sinks, m_mla_q_norm, m_mla_kv_norm, m_mla_w_uq, m_mla_w_ukv, m_w_out, m_mlp_w1, m_mlp_w2, m_final_norm_g, v_norm1_g, v_norm2_g, v_ada_w, v_ada_b, v_w_in, v_s5_lambda_re, v_s5_lambda_im, v_s5_log_dt, v_s5_b_re, v_s5_b_im, v_s5_c_re, v_s5_c_im, v_s5_d, v_s5_glu_w, v_s5_glu_b, v_swa_sinks, v_mla_q_norm, v_mla_kv_norm, v_mla_w_uq, v_mla_w_ukv, v_w_out, v_mlp_w1, v_mlp_w2, v_final_norm_g):
    names = ["norm1_g", "norm2_g", "ada_w", "ada_b", "w_in", "s5_lambda_re", "s5_lambda_im", "s5_log_dt", "s5_b_re", "s5_b_im",
             "s5_c_re", "s5_c_im", "s5_d", "s5_glu_w", "s5_glu_b", "swa_sinks", "mla_q_norm", "mla_kv_norm", "mla_w_uq",
             "mla_w_ukv", "w_out", "mlp_w1", "mlp_w2", "final_norm_g"]
    env = locals()
    wts = {n: env[n] for n in names}
    mom = {n: env["m_" + n] for n in names}
    var = {n: env["v_" + n] for n in names}
    t = x.shape[1]
    me = 4 * lax.axis_index("x") + 2 * lax.axis_index("y") + lax.axis_index("c")
    rope = _rope_tables(t)
    ret_lgam = jnp.log1p(-(2.0 ** (-5.0 - jnp.arange(RET_HEADS, dtype=F32))))

    tr = lambda a: a.transpose(0, 2, 1)
    shard = {"w_in_t": tr(w_in), "w1_t": tr(mlp_w1), "w_uq_t": tr(mla_w_uq), "w_ukv_t": tr(mla_w_ukv),
             "w_out": w_out, "w2": mlp_w2, "glu_w": s5_glu_w}
    to_send = [[shard[k][l].astype(BF16) for k in BIG] for l in range(DEPTH)]
    gathered = _all_gather(to_send[0] + [c], "gather_weights_l0")
    c_all = gathered[-1].reshape(N_DEV, D_MODEL)
    as_rows = lambda arrs: {k: a.reshape(-1, shard[k].shape[2]) for k, a in zip(BIG, arrs)}
    big = [as_rows(gathered[:len(BIG)]), None]
    own_slot = lambda k: 4 * lax.axis_index("x") + 2 * lax.axis_index("y") + lax.axis_index("c")
    gather1 = _push_start(to_send[1], [jax.ShapeDtypeStruct((N_DEV,) + a.shape, a.dtype) for a in to_send[1]], ALL_PEERS,
                          lambda ref, k: ref, own_slot, "gather_weights_l1_start")

    c_act = _whole(lambda v: v * jax.nn.sigmoid(v), [c_all], [((N_DEV, D_MODEL), F32)], "cond_silu")[0]
    c_pad = jnp.concatenate([c_act, jnp.zeros((128 - N_DEV, D_MODEL), F32)], axis=0)
    cols = ada_w.shape[2]
    mod_part = [_mm(c_pad, ada_w[l], "nn", 128, cols, 512, name=f"l{l}_mod")[:N_DEV] for l in range(DEPTH)]
    mod_all = _all_gather([jnp.stack(mod_part)], "gather_mod")[0]
    mod_rows = lax.dynamic_index_in_dim(mod_all, me, axis=2, keepdims=False)
    mods = []
    for l in range(DEPTH):
        row = mod_rows[:, l].reshape(1, 6 * D_MODEL) + ada_b[l][None]
        if l == 0:
            row = row + gather1[4][0, 0]
        mods.append([row[:, i * D_MODEL:(i + 1) * D_MODEL] for i in range(6)])

    smalls, s5_pulls = [], []
    for l in range(DEPTH):
        s5_ops, pull = jax.vjp(_s5_prep, *[wts[n][l] for n in S5_NAMES])
        s5_pulls.append(pull)
        smalls.append(dict(norm1_g=norm1_g[l][None], norm2_g=norm2_g[l][None], s5=s5_ops, s5_glu_b=s5_glu_b[l][None],
                           swa_sinks=swa_sinks[l], mla_q_norm=mla_q_norm[l][None], mla_kv_norm=mla_kv_norm[l][None], ret_lgam=ret_lgam))
    h = x[0]
    saved = []
    for l in range(DEPTH):
        if l == 1:
            sent, landed = _push_wait(gather1[0], gather1[1], gather1[2], gather1[3], h, ALL_PEERS, lambda ref, k: ref, own_slot,
                                      "gather_weights_l1_wait")
            big[1] = as_rows([lax.dynamic_update_index_in_dim(full, own, me, 0) for full, own in zip(landed, sent)])
        h, s = _layer_fwd(h, mods[l], big[l], smalls[l], rope, l)
        saved.append(s)

    fg = final_norm_g[None]
    tgt = loss_target[0]
    loss_local = _ew(_final_fn, [(h, True), (fg, False), (tgt, True)], [(1, F32, "acc")], ROWS, "loss")[0]

    def final_bwd(hh, g, tg):
        dh_, dg, _ = _vjp_block(_final_fn, 3)(hh, g, tg, jnp.ones((1, 1), F32))
        return dh_, dg

    dh, g_final = _ew(final_bwd, [(h, True), (fg, False), (tgt, True)], [(D_MODEL, F32, "tile"), (D_MODEL, F32, "acc")], ROWS, "loss_bwd")
    loss = lax.psum(loss_local[0, 0], ("x", "y", "c"))

    core, chip = lax.axis_index("c"), 2 * lax.axis_index("x") + lax.axis_index("y")

    def core_stage(g_layer, keys, tag):
        g_list = [g_layer[k].reshape(N_DEV, -1, g_layer[k].shape[1]) for k in keys]
        got = _scatter_core(g_list, f"scatter_core_{tag}")
        return [_pick_sum(g, o, core, lambda q, c_: 2 * q + c_, BF16, f"{tag}_core_sum_{k}") for k, g, o in zip(keys, g_list, got)]

    def their_block(ref, k):
        x_, y_ = lax.axis_index("x"), lax.axis_index("y")
        dx, dy, _ = CHIP_PEERS[k]
        return ref.at[2 * (1 - x_ if dx else x_) + (1 - y_ if dy else y_)]

    def chips_start(halves, tag):
        return _push_start(halves, [jax.ShapeDtypeStruct((3,) + a.shape[1:], a.dtype) for a in halves], CHIP_PEERS,
                           their_block, lambda k: k, f"scatter_chips_{tag}_start")

    def chips_wait(started, after, tag):
        return _push_wait(started[0], started[1], started[2], started[3], after, CHIP_PEERS, their_block, lambda k: k, f"scatter_chips_{tag}_wait")

    g_small, dmods = [None] * DEPTH, [None] * DEPTH
    dh, g_big1, g_small[1], dmods[1] = _layer_bwd(dh, mods[1], big[1], smalls[1], rope, saved[1], 1)
    chips1 = chips_start(core_stage(g_big1, BIG, "l1"), "l1")
    mods0 = [m + chips1[4][0, 0] for m in mods[0]]
    early = {}

    def after_mlp(gb):
        early["mlp"] = chips_start(core_stage(gb, MLP_BIG, "l0_mlp"), "l0_mlp")
        return early["mlp"][4][0, 0]

    dh, g_big0, g_small[0], dmods[0] = _layer_bwd(dh, mods0, big[0], smalls[0], rope, saved[0], 0, after_mlp=after_mlp)
    grad_x = dh[None]
    rest = [k for k in BIG if k not in MLP_BIG]
    halves_rest = core_stage(g_big0, rest, "l0_rest")
    halves1, landed1 = chips_wait(chips1, dh, "l1")
    halves_mlp, landed_mlp = chips_wait(early["mlp"], dh, "l0_mlp")
    landed_rest = _scatter_chips(halves_rest, "scatter_chips_l0_rest")
    terms = {(1, k): pair for k, pair in zip(BIG, zip(halves1, landed1))}
    terms.update({(0, k): pair for k, pair in zip(MLP_BIG, zip(halves_mlp, landed_mlp))})
    terms.update({(0, k): pair for k, pair in zip(rest, zip(halves_rest, landed_rest))})
    g_shard = [_pick_sum(*terms[l, k], chip, lambda q, m_: m_, F32, f"l{l}_chip_sum_{k}") for l in range(DEPTH) for k in BIG]

    small_parts = []
    for l in range(DEPTH):
        gs = g_small[l]
        s5g = s5_pulls[l](gs["s5"])
        small_parts += [gs["norm1_g"], gs["norm2_g"], *s5g, gs["s5_glu_b"], gs["swa_sinks"], gs["mla_q_norm"], gs["mla_kv_norm"]]
    small_parts += [g_final, *dmods]
    sizes = [int(np.prod(p.shape)) for p in small_parts]
    flat = jnp.concatenate([p.reshape(1, -1) for p in small_parts], axis=1)
    pad = (-flat.shape[1]) % 8192
    flat = jnp.pad(flat, ((0, 0), (0, pad)))
    flat_all = _all_gather([flat], "gather_small_grads")[0].reshape(N_DEV, -1)
    summed = _sum8(flat_all, "sum_small_grads")
    pieces, off = [], 0
    for sz in sizes:
        pieces.append(summed[0, off:off + sz])
        off += sz
    small_names = ["norm1_g", "norm2_g", *S5_NAMES, "s5_glu_b", "swa_sinks", "mla_q_norm", "mla_kv_norm"]
    per_layer = len(small_names)
    grads = {}
    for i, n in enumerate(small_names):
        grads[n] = jnp.stack([pieces[l * per_layer + i].reshape(wts[n].shape[1:]) for l in range(DEPTH)])
    grads["final_norm_g"] = pieces[DEPTH * per_layer]
    grads["ada_b"] = jnp.stack([pieces[DEPTH * per_layer + 1 + l] for l in range(DEPTH)])

    mod_off = sum(sizes[:DEPTH * per_layer + 1])
    dmod_all = flat_all[:, mod_off:mod_off + DEPTH * 6 * D_MODEL].reshape(N_DEV, DEPTH, N_DEV, cols)
    dmod_mine = lax.dynamic_index_in_dim(dmod_all, me, axis=2, keepdims=False).transpose(1, 0, 2)
    dmod_pad = jnp.concatenate([dmod_mine, jnp.zeros((DEPTH, 128 - N_DEV, cols), F32)], axis=1)
    grads["ada_w"] = jnp.stack([_mm(c_pad, dmod_pad[l], "tn", 512, cols, 128, name=f"l{l}_ada_w_grad") for l in range(DEPTH)])

    out_g, out_d, out_m, out_v = dict(grads), {}, {}, {}
    orig = {"w_in_t": "w_in", "w1_t": "mlp_w1", "w_uq_t": "mla_w_uq", "w_ukv_t": "mla_w_ukv", "w_out": "w_out", "w2": "mlp_w2", "glu_w": "s5_glu_w"}
    for i, k in enumerate(BIG):
        out_g[orig[k]] = jnp.stack([g_shard[l * len(BIG) + i] for l in range(DEPTH)])
        if k.endswith("_t"):
            out_g[orig[k]] = tr(out_g[orig[k]])
    for n in [*orig.values(), "ada_w"]:
        out_d[n], out_m[n], out_v[n] = _adamw_shard(wts[n], out_g[n], mom[n], var[n], f"adamw_{n}")
    small_all = small_names + ["ada_b", "final_norm_g"]
    ds, ms, vs = _adamw_small([wts[n] for n in small_all], [grads[n] for n in small_all], [mom[n] for n in small_all],
                              [var[n] for n in small_all], "adamw_small")
    for n, d, m_, v_ in zip(small_all, ds, ms, vs):
        out_d[n], out_m[n], out_v[n] = d, m_, v_
    return (loss, grad_x, *[out_g[n] for n in names], *[out_d[n] for n in names], *[out_m[n] for n in names], *[out_v[n] for n in names])
```

```python
import functools
import math

import numpy as np
import jax
import jax.numpy as jnp
from jax import lax
from jax.experimental import pallas as pl
from jax.experimental.pallas import tpu as pltpu

F32 = jnp.float32
BF16 = jnp.bfloat16
_MXU_DTYPE = jnp.bfloat16

N_DEV = 8
D_MODEL = 2048
DEPTH = 2
GROUP_WIDTH = 512
D_FF = 8192
S5_CH, S5_GROUPS, S5_STATE = 16, 32, 64
S5_WIDTH = S5_GROUPS * S5_STATE
RET_HEADS, RET_QK, RET_V, RET_CHUNK = 4, 64, 128, 128
SWA_HD, SWA_HEADS, SWA_KV_HEADS, WINDOW = 64, 8, 2, 128
MLA_HEADS, MLA_Q_RANK, MLA_KV_RANK, MLA_NOPE, MLA_ROPE, MLA_V = 4, 384, 128, 128, 64, 128
ROPE_BASE = 10000.0
EPS = 1e-6
NEG = -1e30
N_IN = 3392
ADAM_LR, ADAM_B1, ADAM_B2, ADAM_EPS, ADAM_WD, ADAM_STEP = 0.001, 0.9, 0.999, 1e-08, 0.01, 10

VMEM_LIMIT_BYTES = 52 * 1024 * 1024
MESH_ID = pl.DeviceIdType.MESH
_ANY = pl.BlockSpec(memory_space=pl.ANY)
_SMEM = pl.BlockSpec(memory_space=pltpu.SMEM)


def _params(sem):
    return pltpu.CompilerParams(dimension_semantics=sem, vmem_limit_bytes=VMEM_LIMIT_BYTES)


_DIMS = {"nn": (((1,), (0,)), ((), ())), "nt": (((1,), (1,)), ((), ())), "tn": (((0,), (0,)), ((), ()))}


def _dot(a, b, mode="nn"):
    return lax.dot_general(a.astype(_MXU_DTYPE), b.astype(_MXU_DTYPE), _DIMS[mode], preferred_element_type=F32)


def _mm(a, b, mode, tm, tn, tk, out_dtype=F32, name="mm", b_off=0, n=None, pair=None, epi=None, epi_ins=(), epi_outs=None):
    if mode == "tn":
        kdim, m = a.shape
    else:
        m, kdim = a.shape
    if n is None:
        n = b.shape[0] if mode == "nt" else b.shape[1]
    tm, tn, tk = min(tm, m), min(tn, n), min(tk, kdim)
    assert m % tm == 0 and n % tn == 0 and kdim % tk == 0, (name, a.shape, b.shape, tm, tn, tk)
    nk = kdim // tk
    a_spec = pl.BlockSpec((tk, tm), lambda i, j, k: (k, i)) if mode == "tn" else pl.BlockSpec((tm, tk), lambda i, j, k: (i, k))
    if mode == "nt":
        b_spec = pl.BlockSpec((tn, tk), lambda i, j, k: (j + b_off, k))
    else:
        b_spec = pl.BlockSpec((tk, tn), lambda i, j, k: (k, j + b_off))
    o_spec = pl.BlockSpec((tm, tn), lambda i, j, k: (i, j))
    n_mm = 2 if pair is None else 4
    out_dtypes = [out_dtype] if epi is None else list(epi_outs)

    def body(*refs):
        ins, extra = refs[:n_mm], refs[n_mm:n_mm + len(epi_ins)]
        outs = refs[n_mm + len(epi_ins):n_mm + len(epi_ins) + len(out_dtypes)]
        part = _dot(ins[0][...], ins[1][...], mode)
        if pair is not None:
            part = part + _dot(ins[2][...], ins[3][...], mode)

        def finish(acc):
            vals = (acc,) if epi is None else epi(acc, *[r[...] for r in extra])
            for o_ref, v, dt in zip(outs, vals, out_dtypes):
                o_ref[...] = v.astype(dt)

        if nk == 1:
            finish(part)
        else:
            acc_ref = refs[-1]
            k = pl.program_id(2)

            @pl.when(k == 0)
            def _():
                acc_ref[...] = part

            @pl.when(k > 0)
            def _():
                acc_ref[...] += part

            @pl.when(k == nk - 1)
            def _():
                finish(acc_ref[...])

    operands = [a, b] + ([] if pair is None else list(pair)) + list(epi_ins)
    res = pl.pallas_call(
        body, name=name, grid=(m // tm, n // tn, nk),
        in_specs=[a_spec, b_spec] * (n_mm // 2) + [o_spec] * len(epi_ins),
        out_specs=[o_spec] * len(out_dtypes), out_shape=[jax.ShapeDtypeStruct((m, n), dt) for dt in out_dtypes],
        scratch_shapes=[] if nk == 1 else [pltpu.VMEM((tm, tn), F32)],
        compiler_params=_params(("parallel", "parallel", "arbitrary")),
    )(*operands)
    return res[0] if epi is None else res


SUBLANES = 8


def _row_tile(rows, target, mult=SUBLANES):
    best = None
    for cand in range(mult, min(rows, target) + 1, mult):
        if rows % cand == 0:
            best = cand
    return best or rows


def _ew(fn, ins, outs, tt, name):
    t = [a.shape[0] for a, tiled in ins if tiled][0]
    tt = _row_tile(t, tt)
    n_in = len(ins)
    in_specs = [pl.BlockSpec((tt, a.shape[1]), lambda i: (i, 0)) if tiled else pl.BlockSpec(a.shape, lambda i: (0, 0))
                for a, tiled in ins]
    out_specs, out_shapes = [], []
    for w, dt, kind in outs:
        if kind == "tile":
            out_specs.append(pl.BlockSpec((tt, w), lambda i: (i, 0)))
            out_shapes.append(jax.ShapeDtypeStruct((t, w), dt))
        else:
            out_specs.append(pl.BlockSpec((1, w), lambda i: (0, 0)))
            out_shapes.append(jax.ShapeDtypeStruct((1, w), F32))
    has_acc = any(kind == "acc" for _, _, kind in outs)

    def body(*refs):
        vals = fn(*[r[...] for r in refs[:n_in]])
        if not isinstance(vals, (tuple, list)):
            vals = (vals,)
        i = pl.program_id(0)
        for o_ref, v, (w, dt, kind) in zip(refs[n_in:], vals, outs):
            if kind == "tile":
                o_ref[...] = v.astype(dt)
            else:
                @pl.when(i == 0)
                def _(o_ref=o_ref, v=v):
                    o_ref[...] = v.astype(F32)

                @pl.when(i > 0)
                def _(o_ref=o_ref, v=v):
                    o_ref[...] += v.astype(F32)

    res = pl.pallas_call(
        body, name=name, grid=(t // tt,), in_specs=in_specs, out_specs=out_specs, out_shape=out_shapes,
        compiler_params=_params(("arbitrary" if has_acc else "parallel",)),
    )(*[a for a, _ in ins])
    return res


def _whole(fn, ins, outs, name):
    def body(*refs):
        vals = fn(*[r[...] for r in refs[:len(ins)]])
        if not isinstance(vals, (tuple, list)):
            vals = (vals,)
        for o_ref, v in zip(refs[len(ins):], vals):
            o_ref[...] = v.astype(o_ref.dtype)

    return pl.pallas_call(body, name=name, out_shape=[jax.ShapeDtypeStruct(s, dt) for s, dt in outs])(*ins)


def _rms(x):
    return x * lax.rsqrt(jnp.mean(x * x, axis=-1, keepdims=True) + EPS)


def _norm_mod_fn(h, g, sc, sh):
    return (_rms(h) * g) * (1.0 + sc) + sh


def _rms_gain_fn(x, g):
    return _rms(x) * g


def _gate_add_fn(h, y, gt):
    return h + gt * y


def _relu2_fn(x):
    return jnp.square(jnp.maximum(x, 0.0))


def _s5_act_fn(ypre, u, dskip):
    return jax.nn.gelu(ypre + dskip * u)


def _s5_glu_fn(z, zz, b):
    return z * jax.nn.sigmoid(zz + b)


def _ret_gate_fn(o, g):
    return _rms(o) * (g * jax.nn.sigmoid(g))


def _final_fn(h, g, tgt):
    err = _rms(h) * g - tgt
    return 0.5 * jnp.sum(jnp.mean(err * err, axis=-1, keepdims=True), axis=0, keepdims=True)


def _vjp_block(fn, n_args):
    def bwd(*vals):
        _, pull = jax.vjp(fn, *vals[:n_args])
        return pull(vals[n_args])
    return bwd


def _rope_tables(t):
    d = RET_QK
    inv = ROPE_BASE ** (-jnp.arange(0, d, 2, dtype=F32) / d)
    ang = jnp.arange(t, dtype=F32)[:, None] * inv[None, :]
    cos, sin = jnp.cos(ang), jnp.sin(ang)
    cos2, sin2 = jnp.concatenate([cos, cos], -1), jnp.concatenate([-sin, sin], -1)
    ret = (jnp.tile(cos2, (1, 8)), jnp.tile(sin2, (1, 8)))
    one, zero = jnp.ones((t, MLA_NOPE), F32), jnp.zeros((t, MLA_NOPE), F32)
    mla_c = jnp.concatenate([jnp.tile(jnp.concatenate([one, cos2], -1), (1, MLA_HEADS)), cos2, one[:, :d]], -1)
    mla_s = jnp.concatenate([jnp.tile(jnp.concatenate([zero, sin2], -1), (1, MLA_HEADS)), sin2, zero[:, :d]], -1)
    return ret, (mla_c, mla_s)


def _rope_fn(x, c, s, sign):
    w = x.shape[1]
    lane = lax.broadcasted_iota(jnp.int32, x.shape, 1)
    swapped = jnp.where((lane & 63) < 32, pltpu.roll(x, w - 32, 1), pltpu.roll(x, 32, 1))
    return x * c + swapped * (sign * s)


def _rope(x, tables, name, inverse=False, out_dtype=F32):
    c, s = tables
    fn = functools.partial(_rope_fn, sign=-1.0 if inverse else 1.0)
    return _ew(fn, [(x, True), (c, True), (s, True)], [(x.shape[1], out_dtype, "tile")], ROWS, name)[0]


SCAN_ROWS, SCAN_LANES = 256, 512


def _cmul(ar, ai, br, bi):
    return ar * br - ai * bi, ar * bi + ai * br


def _scan_chunk(xr, xi, ar, ai, reverse):
    rows = xr.shape[0]
    row = lax.broadcasted_iota(jnp.int32, xr.shape, 0)
    pr, pi = ar, ai
    k = 1
    while k < rows:
        if reverse:
            sr, si = pltpu.roll(xr, rows - k, 0), pltpu.roll(xi, rows - k, 0)
            keep = row < rows - k
        else:
            sr, si = pltpu.roll(xr, k, 0), pltpu.roll(xi, k, 0)
            keep = row >= k
        tr, ti = _cmul(pr, pi, sr, si)
        xr = xr + jnp.where(keep, tr, 0.0)
        xi = xi + jnp.where(keep, ti, 0.0)
        pr, pi = _cmul(pr, pi, pr, pi)
        k *= 2
    return xr, xi


def _s5_scan_fwd(bu, a_r, a_i, name):
    t = bu.shape[0]
    rows = min(SCAN_ROWS, t)
    nl = S5_WIDTH // SCAN_LANES

    def body(br_ref, bi_ref, ar_ref, ai_ref, or_ref, oi_ref, cr_ref, ci_ref):
        i = pl.program_id(1)

        @pl.when(i == 0)
        def _():
            cr_ref[...] = jnp.zeros_like(cr_ref)
            ci_ref[...] = jnp.zeros_like(ci_ref)

        ar, ai = ar_ref[...], ai_ref[...]
        xr, xi = br_ref[...], bi_ref[...]
        inr, ini = _cmul(ar, ai, cr_ref[...], ci_ref[...])
        first = lax.broadcasted_iota(jnp.int32, xr.shape, 0) == 0
        xr = xr + jnp.where(first, inr, 0.0)
        xi = xi + jnp.where(first, ini, 0.0)
        xr, xi = _scan_chunk(xr, xi, ar, ai, reverse=False)
        or_ref[...] = xr
        oi_ref[...] = xi
        cr_ref[...] = xr[rows - 1:rows, :]
        ci_ref[...] = xi[rows - 1:rows, :]

    blk = lambda off: pl.BlockSpec((rows, SCAN_LANES), lambda j, i: (i, j + off))
    par = pl.BlockSpec((1, SCAN_LANES), lambda j, i: (0, j))
    st_r, st_i = pl.pallas_call(
        body, name=name, grid=(nl, t // rows), in_specs=[blk(0), blk(nl), par, par], out_specs=[blk(0), blk(0)],
        out_shape=[jax.ShapeDtypeStruct((t, S5_WIDTH), F32)] * 2,
        scratch_shapes=[pltpu.VMEM((1, SCAN_LANES), F32)] * 2, compiler_params=_params(("parallel", "arbitrary")),
    )(bu, bu, a_r, a_i)
    return st_r, st_i


def _s5_scan_bwd(dst, st_r, st_i, a_r, a_i, name):
    t = dst.shape[0]
    rows = min(SCAN_ROWS, t)
    nl = S5_WIDTH // SCAN_LANES
    nc = t // rows

    def body(dr_ref, di_ref, xr_ref, xi_ref, ar_ref, ai_ref, gr_ref, gi_ref, dar_ref, dai_ref, cr_ref, ci_ref):
        i = pl.program_id(1)

        @pl.when(i == 0)
        def _():
            cr_ref[...] = jnp.zeros_like(cr_ref)
            ci_ref[...] = jnp.zeros_like(ci_ref)
            dar_ref[...] = jnp.zeros_like(dar_ref)
            dai_ref[...] = jnp.zeros_like(dai_ref)

        ar, ai = ar_ref[...], ai_ref[...]
        cr, ci = cr_ref[...], ci_ref[...]
        gr, gi = dr_ref[...], di_ref[...]
        row = lax.broadcasted_iota(jnp.int32, gr.shape, 0)
        last = row == rows - 1
        inr, ini = _cmul(ar, -ai, cr, ci)
        gr = gr + jnp.where(last, inr, 0.0)
        gi = gi + jnp.where(last, ini, 0.0)
        gr, gi = _scan_chunk(gr, gi, ar, -ai, reverse=True)
        gr_ref[...] = gr
        gi_ref[...] = gi
        nr = jnp.where(last, cr, pltpu.roll(gr, rows - 1, 0))
        ni = jnp.where(last, ci, pltpu.roll(gi, rows - 1, 0))
        xr, xi = xr_ref[...], xi_ref[...]
        dar_ref[...] += jnp.sum(nr * xr + ni * xi, axis=0, keepdims=True)
        dai_ref[...] += jnp.sum(ni * xr - nr * xi, axis=0, keepdims=True)
        cr_ref[...] = gr[0:1, :]
        ci_ref[...] = gi[0:1, :]

    blk = lambda off: pl.BlockSpec((rows, SCAN_LANES), lambda j, i: (nc - 1 - i, j + off))
    par = pl.BlockSpec((1, SCAN_LANES), lambda j, i: (0, j))
    return pl.pallas_call(
        body, name=name, grid=(nl, nc), in_specs=[blk(0), blk(nl), blk(0), blk(0), par, par],
        out_specs=[blk(0), blk(0), par, par],
        out_shape=[jax.ShapeDtypeStruct((t, S5_WIDTH), F32)] * 2 + [jax.ShapeDtypeStruct((1, S5_WIDTH), F32)] * 2,
        scratch_shapes=[pltpu.VMEM((1, SCAN_LANES), F32)] * 2, compiler_params=_params(("parallel", "arbitrary")),
    )(dst, dst, st_r, st_i, a_r, a_i)


def _s5_prep(lam_re, lam_im, log_dt, b_re, b_im, c_re, c_im, d_skip):
    dt = jnp.exp(log_dt)[:, None]
    mag = jnp.exp(lam_re * dt)
    ar, ai = mag * jnp.cos(lam_im * dt), mag * jnp.sin(lam_im * dt)
    den = lam_re * lam_re + lam_im * lam_im
    cr = ((ar - 1.0) * lam_re + ai * lam_im) / den
    ci = (ai * lam_re - (ar - 1.0) * lam_im) / den
    bbar_r = cr[..., None] * b_re - ci[..., None] * b_im
    bbar_i = cr[..., None] * b_im + ci[..., None] * b_re
    eye = jnp.eye(S5_GROUPS, dtype=F32)

    def bdiag(m):
        g, a, b = m.shape
        return (eye[:, None, :, None] * m[:, :, None, :]).reshape(g * a, g * b)

    b_cat = jnp.concatenate([bdiag(bbar_r.transpose(0, 2, 1)), bdiag(bbar_i.transpose(0, 2, 1))], axis=1)
    c_cat = jnp.concatenate([bdiag(c_re.transpose(0, 2, 1)), -bdiag(c_im.transpose(0, 2, 1))], axis=0)
    return b_cat, c_cat, d_skip.reshape(1, GROUP_WIDTH), ar.reshape(1, S5_WIDTH), ai.reshape(1, S5_WIDTH)


def _ret_consts(lgam):
    c = RET_CHUNK
    r = lax.broadcasted_iota(jnp.int32, (c, c), 0)
    m = lax.broadcasted_iota(jnp.int32, (c, c), 1)
    rel = (r - m).astype(F32)
    decay = jnp.where(rel >= 0, jnp.exp(lgam * jnp.maximum(rel, 0.0)), 0.0)
    idx = lax.broadcasted_iota(jnp.int32, (c, 1), 0).astype(F32)
    zeta = jnp.exp(lgam * (c - 1.0 - idx))
    xi = jnp.exp(lgam * (idx + 1.0))
    return decay, zeta, xi, jnp.exp(lgam * c)


def _ret_specs(t):
    qk = lambda off: pl.BlockSpec((1, t, RET_QK), lambda h: (h + off, 0, 0))
    col = lambda off: pl.BlockSpec((t, RET_V), lambda h: (0, h + off))
    return qk, col


def _ret_fwd(qk, p_ret, lgam, name):
    t = qk.shape[1]
    nck = t // RET_CHUNK
    qk_spec, col = _ret_specs(t)

    def body(lg_ref, q_ref, k_ref, v_ref, g_ref, o_ref, y_ref):
        decay, zeta, xi, gam = _ret_consts(lg_ref[pl.program_id(0)])

        def step(n, state):
            sl = pl.ds(pl.multiple_of(n * RET_CHUNK, RET_CHUNK), RET_CHUNK)
            q, k, v = q_ref[0, sl, :], k_ref[0, sl, :] * (RET_QK ** -0.5), v_ref[sl, :]
            s = _dot(q, k, "nt") * decay
            o = _dot(s, v) + _dot(q, state) * xi
            o_ref[sl, :] = o
            y_ref[sl, :] = _ret_gate_fn(o, g_ref[sl, :]).astype(y_ref.dtype)
            return gam * state + _dot(k, zeta * v, "tn")

        lax.fori_loop(0, nck, step, jnp.zeros((RET_QK, RET_V), F32))

    return pl.pallas_call(
        body, name=name, grid=(RET_HEADS,), in_specs=[_SMEM, qk_spec(0), qk_spec(RET_HEADS), col(4), col(8)],
        out_specs=[col(0), col(0)],
        out_shape=[jax.ShapeDtypeStruct((t, GROUP_WIDTH), F32), jax.ShapeDtypeStruct((t, GROUP_WIDTH), BF16)],
        compiler_params=_params(("parallel",)),
    )(lgam, qk, qk, p_ret, p_ret)


def _ret_bwd(qk, p_ret, o_all, dy, lgam, name):
    t = qk.shape[1]
    nck = t // RET_CHUNK
    qk_spec, col = _ret_specs(t)
    gate_bwd = _vjp_block(_ret_gate_fn, 2)

    def body(lg_ref, q_ref, k_ref, v_ref, g_ref, o_ref, dy_ref, dq_ref, dk_ref, dv_ref, dg_ref, st_ref):
        decay, zeta, xi, gam = _ret_consts(lg_ref[pl.program_id(0)])
        scale = RET_QK ** -0.5

        def fstep(n, state):
            sl = pl.ds(pl.multiple_of(n * RET_CHUNK, RET_CHUNK), RET_CHUNK)
            st_ref[n] = state
            return gam * state + _dot(k_ref[0, sl, :] * scale, zeta * v_ref[sl, :], "tn")

        lax.fori_loop(0, nck, fstep, jnp.zeros((RET_QK, RET_V), F32))

        def bstep(r, grad_state):
            n = nck - 1 - r
            sl = pl.ds(pl.multiple_of(n * RET_CHUNK, RET_CHUNK), RET_CHUNK)
            q, k, v = q_ref[0, sl, :], k_ref[0, sl, :] * scale, v_ref[sl, :]
            d_o, dg = gate_bwd(o_ref[sl, :], g_ref[sl, :], dy_ref[sl, :])
            dg_ref[sl, :] = dg
            s = _dot(q, k, "nt") * decay
            ds = _dot(d_o, v, "nt") * decay
            xdo = xi * d_o
            dq_ref[0, sl, :] = _dot(ds, k) + _dot(xdo, st_ref[n], "nt")
            dk_ref[0, sl, :] = (_dot(ds, q, "tn") + _dot(zeta * v, grad_state, "nt")) * scale
            dv_ref[sl, :] = _dot(s, d_o, "tn") + zeta * _dot(k, grad_state)
            return gam * grad_state + _dot(q, xdo, "tn")

        lax.fori_loop(0, nck, bstep, jnp.zeros((RET_QK, RET_V), F32))

    hd = pl.BlockSpec((1, t, RET_QK), lambda h: (h, 0, 0))
    return pl.pallas_call(
        body, name=name, grid=(RET_HEADS,),
        in_specs=[_SMEM, qk_spec(0), qk_spec(RET_HEADS), col(4), col(8), col(0), col(0)],
        out_specs=[hd, hd, col(0), col(0)],
        out_shape=[jax.ShapeDtypeStruct((RET_HEADS, t, RET_QK), F32)] * 2 + [jax.ShapeDtypeStruct((t, GROUP_WIDTH), F32)] * 2,
        scratch_shapes=[pltpu.VMEM((nck, RET_QK, RET_V), F32)], compiler_params=_params(("parallel",)),
    )(lgam, qk, qk, p_ret, p_ret, o_all, dy)


SWA_GROUP = SWA_HEADS // SWA_KV_HEADS
SWA_SCALE = SWA_HD ** -0.5


def _swa_mask(n):
    rows = SWA_GROUP * WINDOW
    r = lax.broadcasted_iota(jnp.int32, (rows, 2 * WINDOW), 0) & (WINDOW - 1)
    j = lax.broadcasted_iota(jnp.int32, (rows, 2 * WINDOW), 1)
    dist = r + WINDOW - j
    return (dist >= 0) & (dist < WINDOW) & (n * WINDOW + j - WINDOW >= 0)


def _swa_sink_rows(sink_ref, kv):
    row = lax.broadcasted_iota(jnp.int32, (SWA_GROUP * WINDOW, 1), 0)
    sink = jnp.zeros((SWA_GROUP * WINDOW, 1), F32)
    for g in range(SWA_GROUP):
        sink = jnp.where(row >= g * WINDOW, sink_ref[kv * SWA_GROUP + g], sink)
    return sink


def _swa_pad_keys(n, k_ref, v_ref, kp_ref, vp_ref):
    @pl.when(n == 0)
    def _():
        zero = jnp.zeros((WINDOW, SWA_HD), F32)
        kp_ref[0:WINDOW, :] = zero
        vp_ref[0:WINDOW, :] = zero
        kp_ref[WINDOW:, :] = k_ref[0]
        vp_ref[WINDOW:, :] = v_ref[0]


def _swa_specs(t):
    blk = lambda w: pl.BlockSpec((SWA_GROUP, WINDOW, w), lambda kv, n: (kv, n, 0))
    kspec = lambda off: pl.BlockSpec((1, t, SWA_HD), lambda kv, n: (SWA_HEADS + off + kv, 0, 0))
    return blk, kspec


def _swa_fwd(qkv, sinks, name):
    t = qkv.shape[1]
    rows = SWA_GROUP * WINDOW
    blk, kspec = _swa_specs(t)

    def body(sink_ref, q_ref, k_ref, v_ref, o_ref, lse_ref, kp_ref, vp_ref):
        kv, n = pl.program_id(0), pl.program_id(1)
        _swa_pad_keys(n, k_ref, v_ref, kp_ref, vp_ref)
        win = pl.ds(pl.multiple_of(n * WINDOW, WINDOW), 2 * WINDOW)
        sink = _swa_sink_rows(sink_ref, kv)
        s = _dot(q_ref[...].reshape(rows, SWA_HD), kp_ref[win, :], "nt") * SWA_SCALE
        s = jnp.where(_swa_mask(n), s, NEG)
        m = jnp.maximum(jnp.max(s, axis=-1, keepdims=True), sink)
        p = jnp.exp(s - m)
        den = jnp.sum(p, axis=-1, keepdims=True) + jnp.exp(sink - m)
        o_ref[...] = _dot(p / den, vp_ref[win, :]).reshape(SWA_GROUP, WINDOW, SWA_HD)
        lse_ref[...] = (m + jnp.log(den)).reshape(SWA_GROUP, WINDOW, 1)

    return pl.pallas_call(
        body, name=name, grid=(SWA_KV_HEADS, t // WINDOW),
        in_specs=[_SMEM, blk(SWA_HD), kspec(0), kspec(SWA_KV_HEADS)], out_specs=[blk(SWA_HD), blk(1)],
        out_shape=[jax.ShapeDtypeStruct((SWA_HEADS, t, SWA_HD), F32), jax.ShapeDtypeStruct((SWA_HEADS, t, 1), F32)],
        scratch_shapes=[pltpu.VMEM((t + WINDOW, SWA_HD), F32)] * 2, compiler_params=_params(("parallel", "arbitrary")),
    )(sinks, qkv, qkv, qkv)


def _swa_bwd(qkv, o, lse, d_o, sinks, name):
    t = qkv.shape[1]
    nb = t // WINDOW
    rows = SWA_GROUP * WINDOW
    blk, kspec = _swa_specs(t)

    def body(sink_ref, q_ref, k_ref, v_ref, o_ref, lse_ref, do_ref, dq_ref, dk_ref, dv_ref, dsink_ref,
             kp_ref, vp_ref, dkp_ref, dvp_ref):
        kv, n = pl.program_id(0), pl.program_id(1)
        _swa_pad_keys(n, k_ref, v_ref, kp_ref, vp_ref)

        @pl.when(n == 0)
        def _():
            dkp_ref[...] = jnp.zeros_like(dkp_ref)
            dvp_ref[...] = jnp.zeros_like(dvp_ref)
            dsink_ref[...] = jnp.zeros_like(dsink_ref)

        win = pl.ds(pl.multiple_of(n * WINDOW, WINDOW), 2 * WINDOW)
        sink = _swa_sink_rows(sink_ref, kv)
        q, dout = q_ref[...].reshape(rows, SWA_HD), do_ref[...].reshape(rows, SWA_HD)
        lse_n = lse_ref[...].reshape(rows, 1)
        s = _dot(q, kp_ref[win, :], "nt") * SWA_SCALE
        s = jnp.where(_swa_mask(n), s, NEG)
        p = jnp.exp(s - lse_n)
        delta = jnp.sum(dout * o_ref[...].reshape(rows, SWA_HD), axis=-1, keepdims=True)
        ds = p * (_dot(dout, vp_ref[win, :], "nt") - delta)
        dq_ref[...] = (_dot(ds, kp_ref[win, :]) * SWA_SCALE).reshape(SWA_GROUP, WINDOW, SWA_HD)
        dkp_ref[win, :] += _dot(ds, q, "tn") * SWA_SCALE
        dvp_ref[win, :] += _dot(p, dout, "tn")
        term = jnp.exp(sink - lse_n) * delta
        head = lax.broadcasted_iota(jnp.int32, (SWA_GROUP, 128), 0)
        acc = jnp.zeros((SWA_GROUP, 128), F32)
        for g in range(SWA_GROUP):
            acc = jnp.where(head == g, jnp.sum(term[g * WINDOW:(g + 1) * WINDOW], axis=0, keepdims=True), acc)
        dsink_ref[0] -= acc

        @pl.when(n == nb - 1)
        def _():
            dk_ref[0] = dkp_ref[WINDOW:, :]
            dv_ref[0] = dvp_ref[WINDOW:, :]

    kout = pl.BlockSpec((1, t, SWA_HD), lambda kv, n: (kv, 0, 0))
    dq, dk, dv, dsink = pl.pallas_call(
        body, name=name, grid=(SWA_KV_HEADS, nb),
        in_specs=[_SMEM, blk(SWA_HD), kspec(0), kspec(SWA_KV_HEADS), blk(SWA_HD), blk(1), blk(SWA_HD)],
        out_specs=[blk(SWA_HD), kout, kout, pl.BlockSpec((1, SWA_GROUP, 128), lambda kv, n: (kv, 0, 0))],
        out_shape=[jax.ShapeDtypeStruct((SWA_HEADS, t, SWA_HD), F32), jax.ShapeDtypeStruct((SWA_KV_HEADS, t, SWA_HD), F32),
                   jax.ShapeDtypeStruct((SWA_KV_HEADS, t, SWA_HD), F32), jax.ShapeDtypeStruct((SWA_KV_HEADS, SWA_GROUP, 128), F32)],
        scratch_shapes=[pltpu.VMEM((t + WINDOW, SWA_HD), F32)] * 4, compiler_params=_params(("parallel", "arbitrary")),
    )(sinks, qkv, qkv, qkv, o, lse, d_o)
    return jnp.concatenate([dq, dk, dv], axis=0), dsink[:, :, 0].reshape(SWA_HEADS)


MLA_SCALE = (MLA_NOPE + MLA_ROPE) ** -0.5
MLA_TILE = 512
MLA_KEY_TILE = 512
MLA_BWD_TILE = 512


def _mla_diag(s):
    r = lax.broadcasted_iota(jnp.int32, s.shape, 0)
    c = lax.broadcasted_iota(jnp.int32, s.shape, 1)
    return jnp.where(c <= r, s, NEG)


def _mla_specs(t, tile):
    whole = lambda w, off: pl.BlockSpec((t, w), lambda h, i: (0, 2 * h + off))
    head = lambda w: pl.BlockSpec((1, t, w), lambda h, i: (h, 0, 0))
    key_rope = pl.BlockSpec((1, t, MLA_ROPE), lambda h, i: (MLA_HEADS, 0, 0))
    tile_of = lambda w: pl.BlockSpec((1, tile, w), lambda h, i: (h, i, 0))
    return whole, head, key_rope, tile_of


def _mla_attend(qn, rot, kv, name):
    t = qn.shape[1]
    tile = min(MLA_TILE, t)
    ktile = min(MLA_KEY_TILE, t)
    ratio = ktile // tile
    whole, head, key_rope, tile_of = _mla_specs(t, tile)

    def body(qn_ref, qr_ref, kn_ref, kr_ref, v_ref, o_ref, lse_ref, m_ref, l_ref, acc_ref):
        i = pl.program_id(1)
        qn_b, qr_b = qn_ref[0], qr_ref[0]

        def rows(j):
            return pl.ds(pl.multiple_of(j * ktile, ktile), ktile)

        def scores(j):
            return (_dot(qn_b, kn_ref[rows(j), :], "nt") + _dot(qr_b, kr_ref[0, rows(j), :], "nt")) * MLA_SCALE

        def causal(s, j):
            qpos = i * tile + lax.broadcasted_iota(jnp.int32, s.shape, 0)
            kpos = j * ktile + lax.broadcasted_iota(jnp.int32, s.shape, 1)
            return jnp.where(kpos <= qpos, s, NEG)

        def update(s, j):
            m_old = m_ref[...]
            m_new = jnp.maximum(m_old, jnp.max(s, axis=-1, keepdims=True))
            alpha = jnp.exp(m_old - m_new)
            p = jnp.exp(s - m_new)
            l_ref[...] = alpha * l_ref[...] + jnp.sum(p, axis=-1, keepdims=True)
            acc_ref[...] = alpha * acc_ref[...] + _dot(p, v_ref[rows(j), :])
            m_ref[...] = m_new

        m_ref[...] = jnp.full_like(m_ref, NEG)
        l_ref[...] = jnp.zeros_like(l_ref)
        acc_ref[...] = jnp.zeros_like(acc_ref)

        def step(j, s_cur):
            s_next = scores(j + 1)
            update(s_cur, j)
            return s_next

        last = i // ratio
        s_last = lax.fori_loop(0, last, step, scores(0))
        update(causal(s_last, last), last)
        o_ref[...] = acc_ref[...] / l_ref[...]
        lse_ref[0] = m_ref[...] + jnp.log(l_ref[...])

    return pl.pallas_call(
        body, name=name, grid=(MLA_HEADS, t // tile),
        in_specs=[tile_of(MLA_NOPE), tile_of(MLA_ROPE), whole(MLA_NOPE, 0), key_rope, whole(MLA_V, 1)],
        out_specs=[pl.BlockSpec((tile, MLA_V), lambda h, i: (i, h)), tile_of(1)],
        out_shape=[jax.ShapeDtypeStruct((t, GROUP_WIDTH), F32), jax.ShapeDtypeStruct((MLA_HEADS, t, 1), F32)],
        scratch_shapes=[pltpu.VMEM((tile, 1), F32), pltpu.VMEM((tile, 1), F32), pltpu.VMEM((tile, MLA_V), F32)],
        compiler_params=_params(("parallel", "parallel")),
    )(qn, rot, kv, rot, kv)


def _mla_delta(o, d_o, name):
    t = o.shape[0]
    tt = min(ROWS, t)

    def body(o_ref, do_ref, d_ref):
        d_ref[0] = jnp.sum(o_ref[...] * do_ref[...], axis=-1, keepdims=True)

    blk = pl.BlockSpec((tt, MLA_V), lambda h, i: (i, h))
    return pl.pallas_call(
        body, name=name, grid=(MLA_HEADS, t // tt), in_specs=[blk, blk],
        out_specs=pl.BlockSpec((1, tt, 1), lambda h, i: (h, i, 0)), out_shape=jax.ShapeDtypeStruct((MLA_HEADS, t, 1), F32),
        compiler_params=_params(("parallel", "parallel")),
    )(o, d_o)


def _mla_attend_bwd(qn, rot, kv, lse, delta, d_o, name):
    t = qn.shape[1]
    tile = min(MLA_BWD_TILE, t)
    nt = t // tile
    whole, head, key_rope, tile_of = _mla_specs(t, tile)

    def body(qn_ref, qr_ref, kn_ref, kr_ref, v_ref, lse_ref, dl_ref, do_ref, dqn_ref, dqr_ref, dkn_ref, dv_ref, dkr_ref):
        j = pl.program_id(1)

        @pl.when(j == 0)
        def _():
            dqn_ref[...] = jnp.zeros_like(dqn_ref)
            dqr_ref[...] = jnp.zeros_like(dqr_ref)

        dkn_ref[...] = jnp.zeros_like(dkn_ref)
        dv_ref[...] = jnp.zeros_like(dv_ref)
        dkr_ref[...] = jnp.zeros_like(dkr_ref)
        kn_b, kr_b, v_b = kn_ref[...], kr_ref[0], v_ref[...]

        def block(i, diagonal):
            sl = pl.ds(pl.multiple_of(i * tile, tile), tile)
            qn_b, qr_b, dout = qn_ref[0, sl, :], qr_ref[0, sl, :], do_ref[sl, :]
            s = (_dot(qn_b, kn_b, "nt") + _dot(qr_b, kr_b, "nt")) * MLA_SCALE
            if diagonal:
                s = _mla_diag(s)
            p = jnp.exp(s - lse_ref[0, sl, :])
            ds = p * (_dot(dout, v_b, "nt") - dl_ref[0, sl, :]) * MLA_SCALE
            dv_ref[...] += _dot(p, dout, "tn")
            dkn_ref[...] += _dot(ds, qn_b, "tn")
            dkr_ref[0] += _dot(ds, qr_b, "tn")
            dqn_ref[0, sl, :] += _dot(ds, kn_b)
            dqr_ref[0, sl, :] += _dot(ds, kr_b)

        block(j, True)

        def step(i, carry):
            block(i, False)
            return carry

        lax.fori_loop(j + 1, nt, step, 0)

    key_tile = lambda w, off: pl.BlockSpec((tile, w), lambda h, j: (j, 2 * h + off))
    out_tile = pl.BlockSpec((tile, MLA_V), lambda h, j: (j, h))
    return pl.pallas_call(
        body, name=name, grid=(MLA_HEADS, nt),
        in_specs=[head(MLA_NOPE), head(MLA_ROPE), key_tile(MLA_NOPE, 0), pl.BlockSpec((1, tile, MLA_ROPE), lambda h, j: (MLA_HEADS, j, 0)),
                  key_tile(MLA_V, 1), head(1), head(1), pl.BlockSpec((t, MLA_V), lambda h, j: (0, h))],
        out_specs=[head(MLA_NOPE), head(MLA_ROPE), out_tile, out_tile, tile_of(MLA_ROPE)],
        out_shape=[jax.ShapeDtypeStruct((MLA_HEADS, t, MLA_NOPE), F32), jax.ShapeDtypeStruct((MLA_HEADS, t, MLA_ROPE), F32),
                   jax.ShapeDtypeStruct((t, MLA_HEADS * MLA_NOPE), F32), jax.ShapeDtypeStruct((t, MLA_HEADS * MLA_V), F32),
                   jax.ShapeDtypeStruct((MLA_HEADS, t, MLA_ROPE), F32)],
        compiler_params=_params(("parallel", "arbitrary")),
    )(qn, rot, kv, rot, kv, lse, delta, d_o)


def _place():
    return lax.axis_index("x"), lax.axis_index("y"), lax.axis_index("c")


def _all_gather(arrs, name):
    n = len(arrs)

    def body(*refs):
        x_refs, o_refs = refs[:n], refs[n:2 * n]
        send_sems, recv_sems, local_sems = refs[2 * n:]
        x, y, c = _place()
        me, sibling = (x, y, c), (x, y, 1 - c)
        chips = [(1 - x, y), (x, 1 - y), (1 - x, 1 - y)]

        def slot(a, p):
            return o_refs[a].at[4 * p[0] + 2 * p[1] + p[2]]

        def copy(a, k, block, to, src=None):
            return pltpu.make_async_remote_copy(
                src_ref=slot(a, block) if src is None else src, dst_ref=slot(a, block),
                send_sem=send_sems.at[a, k], recv_sem=recv_sems.at[a, k], device_id=to, device_id_type=MESH_ID)

        mine = [pltpu.make_async_copy(x_refs[a], slot(a, me), local_sems.at[a]) for a in range(n)]
        for cp in mine:
            cp.start()
        first = []
        for a in range(n):
            first.append(copy(a, 0, me, sibling, src=x_refs[a]))
            first += [copy(a, 1 + j, me, (*chip, c), src=x_refs[a]) for j, chip in enumerate(chips)]
        for cp in first:
            cp.start()
        passed = []
        for j, chip in enumerate(chips):
            for a in range(n):
                copy(a, 1 + j, (*chip, c), me).wait_recv()
                cp = copy(a, 4 + j, (*chip, c), sibling)
                cp.start()
                passed.append(cp)
        for a in range(n):
            copy(a, 0, sibling, me).wait_recv()
            for j, chip in enumerate(chips):
                copy(a, 4 + j, (*chip, 1 - c), me).wait_recv()
        for cp in first + passed:
            cp.wait_send()
        for cp in mine:
            cp.wait()

    return pl.pallas_call(
        body, name=name, in_specs=[_ANY] * n, out_specs=[_ANY] * n,
        out_shape=[jax.ShapeDtypeStruct((N_DEV,) + a.shape, a.dtype) for a in arrs],
        scratch_shapes=[pltpu.SemaphoreType.DMA((n, 7)), pltpu.SemaphoreType.DMA((n, 7)), pltpu.SemaphoreType.DMA((n,))],
    )(*arrs)


def _pass_to_sibling(arrs, name):
    n = len(arrs)

    def body(*refs):
        a_refs, o_refs = refs[:n], refs[n:2 * n]
        send_sems, recv_sems = refs[2 * n:]
        x, y, c = _place()
        chips = [(1 - x, y), (x, 1 - y), (1 - x, 1 - y)]
        slot = lambda px, py, pc: 4 * px + 2 * py + pc
        sends, recvs = [], []
        for a in range(n):
            for j, (px, py) in enumerate(chips):
                sends.append(pltpu.make_async_remote_copy(
                    src_ref=a_refs[a].at[slot(px, py, c)], dst_ref=o_refs[a].at[slot(px, py, c)], send_sem=send_sems.at[a, j],
                    recv_sem=recv_sems.at[a, j], device_id=(x, y, 1 - c), device_id_type=MESH_ID))
                recvs.append(pltpu.make_async_remote_copy(
                    src_ref=a_refs[a].at[slot(px, py, c)], dst_ref=o_refs[a].at[slot(px, py, 1 - c)], send_sem=send_sems.at[a, j],
                    recv_sem=recv_sems.at[a, j], device_id=(x, y, 1 - c), device_id_type=MESH_ID))
        for cp in sends:
            cp.start()
        for cp in sends:
            cp.wait_send()
        for cp in recvs:
            cp.wait_recv()

    return pl.pallas_call(
        body, name=name, in_specs=[_ANY] * n, out_specs=[_ANY] * n,
        out_shape=[jax.ShapeDtypeStruct(a.shape, a.dtype) for a in arrs], input_output_aliases={i: i for i in range(n)},
        scratch_shapes=[pltpu.SemaphoreType.DMA((n, 3)), pltpu.SemaphoreType.DMA((n, 3))],
    )(*arrs)


def _scatter_core(grads, name):
    n = len(grads)

    def body(*refs):
        g_refs, got_refs = refs[:n], refs[n:2 * n]
        send_sems, recv_sems = refs[2 * n:]
        x, y, c = _place()
        sends = [pltpu.make_async_remote_copy(
            src_ref=g_refs[a].at[2 * q + 1 - c], dst_ref=got_refs[a].at[q], send_sem=send_sems.at[a, q],
            recv_sem=recv_sems.at[a, q], device_id=(x, y, 1 - c), device_id_type=MESH_ID) for a in range(n) for q in range(4)]
        for cp in sends:
            cp.start()
        for cp in sends:
            cp.wait()

    return pl.pallas_call(
        body, name=name, in_specs=[_ANY] * n, out_specs=[_ANY] * n,
        out_shape=[jax.ShapeDtypeStruct((4,) + g.shape[1:], g.dtype) for g in grads],
        scratch_shapes=[pltpu.SemaphoreType.DMA((n, 4)), pltpu.SemaphoreType.DMA((n, 4))],
    )(*grads)


def _scatter_chips(parts, name):
    n = len(parts)

    def body(*refs):
        p_refs, o_refs = refs[:n], refs[n:2 * n]
        send_sems, recv_sems = refs[2 * n:]
        x, y, c = _place()
        chips = [(1 - x, y), (x, 1 - y), (1 - x, 1 - y)]
        sends = [pltpu.make_async_remote_copy(
            src_ref=p_refs[a].at[2 * px + py], dst_ref=o_refs[a].at[j], send_sem=send_sems.at[a, j],
            recv_sem=recv_sems.at[a, j], device_id=(px, py, c), device_id_type=MESH_ID)
            for a in range(n) for j, (px, py) in enumerate(chips)]
        for cp in sends:
            cp.start()
        for cp in sends:
            cp.wait()

    return pl.pallas_call(
        body, name=name, in_specs=[_ANY] * n, out_specs=[_ANY] * n,
        out_shape=[jax.ShapeDtypeStruct((3,) + p.shape[1:], p.dtype) for p in parts],
        scratch_shapes=[pltpu.SemaphoreType.DMA((n, 3)), pltpu.SemaphoreType.DMA((n, 3))],
    )(*parts)


GATHER_PEERS = [(0, 0, 1), (1, 0, 0), (0, 1, 0), (1, 1, 0)]
CHIP_PEERS = [(1, 0, 0), (0, 1, 0), (1, 1, 0)]
_HBM = pl.BlockSpec(memory_space=pltpu.HBM)
_SEM = pl.BlockSpec(memory_space=pltpu.SEMAPHORE)
_EFFECT = pltpu.SideEffectType.DATAFLOW_SIDE_EFFECTING


def _push_copies(src_refs, land_refs, send_sems, recv_sems, peers, src_of, slot_of):
    place = _place()
    flip = lambda v, f: 1 - v if f else v
    return [pltpu.make_async_remote_copy(
        src_ref=src_of(src_refs[a], k), dst_ref=land_refs[a].at[slot_of(k)], send_sem=send_sems[a], recv_sem=recv_sems[a],
        device_id=tuple(flip(v, f) for v, f in zip(place, peer)), device_id_type=MESH_ID)
        for a in range(len(src_refs)) for k, peer in enumerate(peers)]


def _push_start(srcs, land_shapes, peers, src_of, slot_of, name):
    n = len(srcs)

    def body(*refs):
        src_refs, land_refs = refs[:n], refs[n:2 * n]
        send_sems, recv_sems = refs[2 * n:3 * n], refs[3 * n:4 * n]
        token = refs[-1]
        for cp in _push_copies(src_refs, land_refs, send_sems, recv_sems, peers, src_of, slot_of):
            cp.start()
        token[...] = jnp.zeros_like(token)

    sems = [pltpu.SemaphoreType.DMA(())] * (2 * n)
    lands = [pltpu.with_memory_space_constraint(lax.empty(s.shape, s.dtype), pltpu.HBM) for s in land_shapes]
    res = pl.pallas_call(
        body, name=name, in_specs=[_HBM] * (2 * n), out_specs=[_SEM] * (2 * n) + [_HBM] * (2 * n) + [pl.BlockSpec(memory_space=pltpu.VMEM)],
        out_shape=sems + [pltpu.HBM(s.shape, s.dtype) for s in srcs] + [pltpu.HBM(s.shape, s.dtype) for s in land_shapes]
        + [jax.ShapeDtypeStruct((8, 128), F32)],
        input_output_aliases={i: 2 * n + i for i in range(2 * n)},
        compiler_params=pltpu.CompilerParams(has_side_effects=_EFFECT),
    )(*[pltpu.with_memory_space_constraint(s, pltpu.HBM) for s in srcs], *lands)
    return list(res[:n]), list(res[n:2 * n]), list(res[2 * n:3 * n]), list(res[3 * n:4 * n]), res[-1]


def _push_wait(send_sems, recv_sems, srcs, lands, after, peers, src_of, slot_of, name):
    n = len(srcs)

    def body(*refs):
        src_refs, land_refs = refs[:n], refs[n:2 * n]
        s_sems, r_sems = refs[2 * n:3 * n], refs[3 * n:4 * n]
        copies = _push_copies(src_refs, land_refs, s_sems, r_sems, peers, src_of, slot_of)
        for cp in copies:
            cp.wait_send()
        for cp in copies:
            cp.wait_recv()

    res = pl.pallas_call(
        body, name=name, in_specs=[_HBM] * (2 * n) + [_SEM] * (2 * n) + [_ANY], out_specs=[_HBM] * (2 * n),
        out_shape=[pltpu.HBM(s.shape, s.dtype) for s in srcs] + [pltpu.HBM(s.shape, s.dtype) for s in lands],
        input_output_aliases={i: i for i in range(2 * n)},
        compiler_params=pltpu.CompilerParams(has_side_effects=_EFFECT),
    )(*srcs, *lands, *send_sems, *recv_sems, after)
    return list(res[:n]), list(res[n:])


def _pick_sum(picked, rest, index, pick_of, out_dtype, name):
    nq, r, cdim = rest.shape
    one = nq == 3
    tr = _row_tile(r, 512, 16)
    tc = 512 if cdim % 512 == 0 else cdim
    grid = (1 if one else nq, r // tr, cdim // tc)

    def body(i_ref, p_ref, r_ref, o_ref):
        acc = p_ref[0].astype(F32)
        if one:
            for j in range(3):
                acc = acc + r_ref[j].astype(F32)
            o_ref[...] = acc.astype(out_dtype)
        else:
            o_ref[0] = (acc + r_ref[0].astype(F32)).astype(out_dtype)

    spec = pltpu.PrefetchScalarGridSpec(
        num_scalar_prefetch=1, grid=grid,
        in_specs=[pl.BlockSpec((1, tr, tc), lambda q, i, j, i_ref: (pick_of(q, i_ref[0]), i, j)),
                  pl.BlockSpec((3, tr, tc), lambda q, i, j, i_ref: (0, i, j)) if one else pl.BlockSpec((1, tr, tc), lambda q, i, j, i_ref: (q, i, j))],
        out_specs=pl.BlockSpec((tr, tc), lambda q, i, j, i_ref: (i, j)) if one else pl.BlockSpec((1, tr, tc), lambda q, i, j, i_ref: (q, i, j)))
    return pl.pallas_call(
        body, name=name, grid_spec=spec,
        out_shape=jax.ShapeDtypeStruct((r, cdim) if one else (nq, r, cdim), out_dtype),
        compiler_params=_params(("parallel", "parallel", "parallel")),
    )(index.astype(jnp.int32).reshape(1), picked, rest)


def _adamw_fn(w, g, m, v):
    m = ADAM_B1 * m + (1.0 - ADAM_B1) * g
    v = ADAM_B2 * v + (1.0 - ADAM_B2) * jnp.square(g)
    m_hat = m / (1.0 - ADAM_B1 ** ADAM_STEP)
    v_hat = v / (1.0 - ADAM_B2 ** ADAM_STEP)
    delta = -ADAM_LR * (m_hat / (jnp.sqrt(v_hat) + ADAM_EPS) + ADAM_WD * w)
    return delta, m, v


def _as2d(a):
    return a.reshape(-1, a.shape[-1])


def _adamw_shard(w, g, m, v, name):
    shape = w.shape
    ins = [_as2d(a) for a in (w, g, m, v)]
    cols = ins[0].shape[1]
    outs = _ew(_adamw_fn, [(a, True) for a in ins], [(cols, F32, "tile")] * 3, 256, name)
    return [o.reshape(shape) for o in outs]


def _adamw_small(ws, gs, ms, vs, name):
    shapes = [w.shape for w in ws]
    flat = lambda a: a.reshape(-1, 128) if a.size % 128 == 0 else a.reshape(1, -1)
    ins = [flat(a) for grp in zip(ws, gs, ms, vs) for a in grp]
    k = len(ws)

    def fn(*vals):
        out = []
        for i in range(k):
            out += list(_adamw_fn(*vals[4 * i:4 * i + 4]))
        return out

    outs = _whole(fn, ins, [(ins[4 * (i // 3)].shape, F32) for i in range(3 * k)], name)
    deltas = [outs[3 * i].reshape(shapes[i]) for i in range(k)]
    new_m = [outs[3 * i + 1].reshape(shapes[i]) for i in range(k)]
    new_v = [outs[3 * i + 2].reshape(shapes[i]) for i in range(k)]
    return deltas, new_m, new_v


def _sum8(stacked, name):
    def fn(a):
        s = a[0:1]
        for i in range(1, N_DEV):
            s = s + a[i:i + 1]
        return s
    w = stacked.shape[1]
    tw = 8192
    if w % tw:
        return _whole(fn, [stacked], [((1, w), F32)], name)[0]

    def body(a_ref, o_ref):
        o_ref[...] = fn(a_ref[...])

    return pl.pallas_call(body, name=name, grid=(w // tw,), in_specs=[pl.BlockSpec((N_DEV, tw), lambda i: (0, i))],
                          out_specs=pl.BlockSpec((1, tw), lambda i: (0, i)), out_shape=jax.ShapeDtypeStruct((1, w), F32))(stacked)


ROWS = 512


def _split_heads(p, nh):
    t = p.shape[0]
    return p.reshape(t, nh, p.shape[1] // nh).transpose(1, 0, 2)


def _merge_heads(p):
    nh, t, d = p.shape
    return p.transpose(1, 0, 2).reshape(t, nh * d)


def _layer_fwd(h, mod, w, small, rope, l, late=None):
    sh1, sc1, gt1, sh2, sc2, gt2 = mod
    rope_ret, rope_mla = rope
    t = h.shape[0]
    nm = lambda s: f"l{l}_{s}"
    a1 = _ew(_norm_mod_fn, [(h, True), (small["norm1_g"], False), (sc1, False), (sh1, False)], [(D_MODEL, BF16, "tile")], ROWS, nm("norm1"))[0]
    p_s5 = _mm(a1, w["w_in_t"], "nt", 512, 512, 2048, name=nm("proj_s5"), n=512)
    p_ret = _mm(a1, w["w_in_t"], "nt", 512, 512, 2048, name=nm("proj_ret"), b_off=1, n=1536)
    p_swa = _mm(a1, w["w_in_t"], "nt", 512, 256, 2048, name=nm("proj_swa"), b_off=8, n=768)
    p_mla = _mm(a1, w["w_in_t"][2816:], "nt", 512, 576, 2048, name=nm("proj_mla"))
    b_cat, c_cat, dskip, a_r, a_i = small["s5"]
    bu = _mm(p_s5, b_cat, "nn", 512, 1024, 512, name=nm("s5_bu"))
    st_r, st_i = _s5_scan_fwd(bu, a_r, a_i, nm("s5_scan"))
    ypre = _mm(st_r, c_cat[:S5_WIDTH], "nn", 512, 512, 2048, name=nm("s5_y"), pair=(st_i, c_cat[S5_WIDTH:]))
    z = _ew(_s5_act_fn, [(ypre, True), (p_s5, True), (dskip, False)], [(GROUP_WIDTH, F32, "tile")], ROWS, nm("s5_act"))[0]
    zz = _mm(z, w["glu_w"], "nn", 512, 512, 512, name=nm("s5_zz"))
    y_s5 = _ew(_s5_glu_fn, [(z, True), (zz, True), (small["s5_glu_b"], False)], [(GROUP_WIDTH, BF16, "tile")], ROWS, nm("s5_glu"))[0]
    qk_ret = _split_heads(_rope(p_ret[:, :2 * RET_HEADS * RET_QK], rope_ret, nm("ret_rope")), 2 * RET_HEADS)
    o_ret, y_ret = _ret_fwd(qk_ret, p_ret, small["ret_lgam"], nm("ret"))
    qkv_swa = _split_heads(p_swa, 12)
    o_swa, lse_swa = _swa_fwd(qkv_swa, small["swa_sinks"], nm("swa"))
    y_swa = _merge_heads(o_swa).astype(BF16)
    cq, ckv, kr = p_mla[:, :MLA_Q_RANK], p_mla[:, MLA_Q_RANK:MLA_Q_RANK + MLA_KV_RANK], p_mla[:, MLA_Q_RANK + MLA_KV_RANK:]
    cqn = _ew(_rms_gain_fn, [(cq, True), (small["mla_q_norm"], False)], [(MLA_Q_RANK, BF16, "tile")], ROWS, nm("mla_qnorm"))[0]
    ckvn = _ew(_rms_gain_fn, [(ckv, True), (small["mla_kv_norm"], False)], [(MLA_KV_RANK, BF16, "tile")], ROWS, nm("mla_kvnorm"))[0]
    q_full = _mm(cqn, w["w_uq_t"], "nt", 512, 768, 384, name=nm("mla_q"))
    kv_full = _mm(ckvn, w["w_ukv_t"], "nt", 512, 1024, 128, BF16, name=nm("mla_kv"))
    nq = q_full.shape[1]
    roped = _rope(jnp.concatenate([q_full, kr, jnp.zeros_like(kr)], axis=1), rope_mla, nm("mla_rope"), out_dtype=BF16)
    q4 = roped[:, :nq].reshape(t, MLA_HEADS, MLA_NOPE + MLA_ROPE)
    qn = q4[:, :, :MLA_NOPE].transpose(1, 0, 2)
    rot = jnp.concatenate([q4[:, :, MLA_NOPE:].transpose(1, 0, 2), roped[None, :, nq:nq + MLA_ROPE]], axis=0)
    o_mla, lse_mla = _mla_attend(qn, rot, kv_full, nm("mla"))
    cat = jnp.concatenate([y_s5, y_ret, y_swa, o_mla.astype(BF16)], axis=1)
    if late is not None:
        w = {**w, **late(lse_mla)}
    mixed = _mm(cat, w["w_out"], "nn", 512, 1024, 2048, name=nm("out_proj"))
    h1 = _ew(_gate_add_fn, [(h, True), (mixed, True), (gt1, False)], [(D_MODEL, F32, "tile")], ROWS, nm("res1"))[0]
    a2 = _ew(_norm_mod_fn, [(h1, True), (small["norm2_g"], False), (sc2, False), (sh2, False)], [(D_MODEL, BF16, "tile")], ROWS, nm("norm2"))[0]
    hid, act = _mm(a2, w["w1_t"], "nt", 1024, 1024, 2048, name=nm("mlp1"), epi=lambda acc: (acc, _relu2_fn(acc)), epi_outs=[F32, BF16])
    mo = _mm(act, w["w2"], "nn", 1024, 512, 4096, name=nm("mlp2"))
    h2 = _ew(_gate_add_fn, [(h1, True), (mo, True), (gt2, False)], [(D_MODEL, F32, "tile")], ROWS, nm("res2"))[0]
    saved = dict(w=w, h=h, a1=a1, p_s5=p_s5, p_ret=p_ret, st_r=st_r, st_i=st_i, ypre=ypre, z=z, zz=zz, qk_ret=qk_ret, o_ret=o_ret,
                 qkv_swa=qkv_swa, o_swa=o_swa, lse_swa=lse_swa, cq=cq, ckv=ckv, cqn=cqn, ckvn=ckvn, qn=qn, rot=rot,
                 kv_full=kv_full, o_mla=o_mla, lse_mla=lse_mla, cat=cat, mixed=mixed, h1=h1, a2=a2, hid=hid, act=act, mo=mo)
    return h2, saved


def _layer_bwd(dh2, mod, w, small, rope, s, l, after_mlp=None):
    sh1, sc1, gt1, sh2, sc2, gt2 = mod
    rope_ret, rope_mla = rope
    t = dh2.shape[0]
    nm = lambda n: f"l{l}_{n}_bwd"
    gb, gs = {}, {}
    row = (D_MODEL, F32, "acc")
    dmo, dgt2 = _ew(lambda d, y, gt: (d * gt, jnp.sum(d * y, axis=0, keepdims=True)),
                    [(dh2, True), (s["mo"], True), (gt2, False)], [(D_MODEL, BF16, "tile"), row], ROWS, nm("res2"))
    dhid = _mm(dmo, w["w2"], "nt", 1024, 1024, 2048, name=nm("mlp2_x"), epi=lambda acc, x: (acc * 2.0 * jnp.maximum(x, 0.0),),
               epi_ins=[s["hid"]], epi_outs=[BF16])[0]
    gb["w2"] = _mm(s["act"], dmo, "tn", 1024, 1024, 4096, BF16, name=nm("mlp2_w"))
    da2 = _mm(dhid, w["w1_t"], "nn", 1024, 512, 4096, name=nm("mlp1_x"))
    gb["w1_t"] = _mm(dhid, s["a2"], "tn", 1024, 1024, 4096, BF16, name=nm("mlp1_w"))
    if after_mlp is not None:
        gt1 = gt1 + after_mlp(gb)

    def norm_bwd(hh, g, sc, sh, da, dres):
        dh_, dg, dsc, dsh = _vjp_block(_norm_mod_fn, 4)(hh, g, sc, sh, da)
        return dh_ + dres, dg, dsc, dsh

    dh1, gs["norm2_g"], dsc2, dsh2 = _ew(norm_bwd, [(s["h1"], True), (small["norm2_g"], False), (sc2, False), (sh2, False), (da2, True), (dh2, True)],
                                         [(D_MODEL, F32, "tile"), row, row, row], ROWS, nm("norm2"))
    dmixed, dgt1 = _ew(lambda d, y, gt: (d * gt, jnp.sum(d * y, axis=0, keepdims=True)),
                       [(dh1, True), (s["mixed"], True), (gt1, False)], [(D_MODEL, BF16, "tile"), row], ROWS, nm("res1"))
    dcat = _mm(dmixed, w["w_out"], "nt", 512, 1024, 2048, name=nm("out_proj_x"))
    gb["w_out"] = _mm(s["cat"], dmixed, "tn", 1024, 1024, 4096, BF16, name=nm("out_proj_w"))
    dy_s5, dy_ret, dy_swa, dy_mla = (dcat[:, i * GROUP_WIDTH:(i + 1) * GROUP_WIDTH] for i in range(4))
    b_cat, c_cat, dskip, a_r, a_i = small["s5"]
    gw = (GROUP_WIDTH, F32, "tile")
    gacc = (GROUP_WIDTH, F32, "acc")
    dz_a, dzz, gs["s5_glu_b"] = _ew(_vjp_block(_s5_glu_fn, 3), [(s["z"], True), (s["zz"], True), (small["s5_glu_b"], False), (dy_s5, True)],
                                    [gw, gw, gacc], ROWS, nm("s5_glu"))
    dz_b = _mm(dzz, w["glu_w"], "nt", 512, 512, 512, name=nm("s5_zz_x"))
    gb["glu_w"] = _mm(s["z"], dzz, "tn", 512, 512, 1024, BF16, name=nm("s5_zz_w"))

    def act_bwd(ypre, u, dsk, dza, dzb):
        return _vjp_block(_s5_act_fn, 3)(ypre, u, dsk, dza + dzb)

    dypre, du_a, g_dskip = _ew(act_bwd, [(s["ypre"], True), (s["p_s5"], True), (dskip, False), (dz_a, True), (dz_b, True)],
                               [gw, gw, gacc], ROWS, nm("s5_act"))
    dst = _mm(dypre, c_cat, "nt", 512, 1024, 512, name=nm("s5_y_x"))
    g_ccat = jnp.concatenate([_mm(s["st_r"], dypre, "tn", 1024, 512, 2048, name=nm("s5_y_w_re")),
                              _mm(s["st_i"], dypre, "tn", 1024, 512, 2048, name=nm("s5_y_w_im"))], axis=0)
    dbu_r, dbu_i, g_ar, g_ai = _s5_scan_bwd(dst, s["st_r"], s["st_i"], a_r, a_i, nm("s5_scan"))
    du_b = _mm(dbu_r, b_cat[:, :S5_WIDTH], "nt", 512, 512, 2048, name=nm("s5_bu_x"), pair=(dbu_i, b_cat[:, S5_WIDTH:]))
    g_bcat = jnp.concatenate([_mm(s["p_s5"], dbu_r, "tn", 512, 1024, 2048, name=nm("s5_bu_w_re")),
                              _mm(s["p_s5"], dbu_i, "tn", 512, 1024, 2048, name=nm("s5_bu_w_im"))], axis=1)
    gs["s5"] = (g_bcat, g_ccat, g_dskip, g_ar, g_ai)
    dqk_rot, dk_rot, dv_ret, dg_ret = _ret_bwd(s["qk_ret"], s["p_ret"], s["o_ret"], dy_ret, small["ret_lgam"], nm("ret"))
    dqk = _rope(_merge_heads(jnp.concatenate([dqk_rot, dk_rot], axis=0)), rope_ret, nm("ret_rope"), inverse=True)
    dqkv_swa, gs["swa_sinks"] = _swa_bwd(s["qkv_swa"], s["o_swa"], s["lse_swa"], _split_heads(dy_swa, SWA_HEADS), small["swa_sinks"], nm("swa"))
    delta = _mla_delta(s["o_mla"], dy_mla, nm("mla_delta"))
    dqn, dqr, dkn, dv_mla, dkr_heads = _mla_attend_bwd(s["qn"], s["rot"], s["kv_full"], s["lse_mla"], delta, dy_mla, nm("mla_att"))
    dkv_full = jnp.stack([dkn.reshape(t, MLA_HEADS, MLA_NOPE), dv_mla.reshape(t, MLA_HEADS, MLA_V)], axis=2).reshape(t, 2 * MLA_HEADS * MLA_NOPE)
    dkr_rot = _ew(lambda a, b, c, d: a + b + c + d, [(dkr_heads[i], True) for i in range(MLA_HEADS)], [(MLA_ROPE, F32, "tile")], ROWS, nm("mla_dkr"))[0]
    nq = MLA_HEADS * (MLA_NOPE + MLA_ROPE)
    dq_rot = jnp.concatenate([dqn.transpose(1, 0, 2), dqr.transpose(1, 0, 2)], axis=2).reshape(t, nq)
    droped = _rope(jnp.concatenate([dq_rot, dkr_rot, jnp.zeros_like(dkr_rot)], axis=1), rope_mla, nm("mla_rope"), inverse=True)
    dq_full, dkr = droped[:, :nq], droped[:, nq:nq + MLA_ROPE]
    dcqn = _mm(dq_full, w["w_uq_t"], "nn", 512, 384, 768, name=nm("mla_q_x"))
    gb["w_uq_t"] = _mm(dq_full, s["cqn"], "tn", 768, 384, 1024, BF16, name=nm("mla_q_w"))
    dckvn = _mm(dkv_full, w["w_ukv_t"], "nn", 512, 128, 1024, name=nm("mla_kv_x"))
    gb["w_ukv_t"] = _mm(dkv_full, s["ckvn"], "tn", 1024, 128, 1024, BF16, name=nm("mla_kv_w"))
    dcq, gs["mla_q_norm"] = _ew(_vjp_block(_rms_gain_fn, 2), [(s["cq"], True), (small["mla_q_norm"], False), (dcqn, True)],
                                [(MLA_Q_RANK, F32, "tile"), (MLA_Q_RANK, F32, "acc")], ROWS, nm("mla_qnorm"))
    dckv, gs["mla_kv_norm"] = _ew(_vjp_block(_rms_gain_fn, 2), [(s["ckv"], True), (small["mla_kv_norm"], False), (dckvn, True)],
                                  [(MLA_KV_RANK, F32, "tile"), (MLA_KV_RANK, F32, "acc")], ROWS, nm("mla_kvnorm"))
    du = _ew(lambda a, b: a + b, [(du_a, True), (du_b, True)], [(GROUP_WIDTH, BF16, "tile")], ROWS, nm("s5_du"))[0]
    bf = lambda a: a.astype(BF16)
    dproj = jnp.concatenate([du, bf(dqk), bf(dv_ret), bf(dg_ret), bf(_merge_heads(dqkv_swa)), bf(dcq), bf(dckv), bf(dkr)], axis=1)
    da1 = _mm(dproj, w["w_in_t"], "nn", 512, 1024, N_IN, name=nm("proj_x"))
    gb["w_in_t"] = _mm(dproj, s["a1"], "tn", N_IN, 512, 2048, BF16, name=nm("proj_w"))
    dh, gs["norm1_g"], dsc1, dsh1 = _ew(norm_bwd, [(s["h"], True), (small["norm1_g"], False), (sc1, False), (sh1, False), (da1, True), (dh1, True)],
                                        [(D_MODEL, F32, "tile"), row, row, row], ROWS, nm("norm1"))
    dmod = jnp.concatenate([dsh1, dsc1, dgt1, dsh2, dsc2, dgt2], axis=1)
    return dh, gb, gs, dmod


BIG = ("w_in_t", "w1_t", "w_uq_t", "w_ukv_t", "w_out", "w2", "glu_w")
MLP_BIG = ("w1_t", "w2")
LATE_BIG = ("w_out", "w1_t", "w2")
S5_NAMES = ("s5_lambda_re", "s5_lambda_im", "s5_log_dt", "s5_b_re", "s5_b_im", "s5_c_re", "s5_c_im", "s5_d")


def kernel(x, c, norm1_g, norm2_g, ada_w, ada_b, w_in, s5_lambda_re, s5_lambda_im, s5_log_dt, s5_b_re, s5_b_im, s5_c_re, s5_c_im, s5_d, s5_glu_w, s5_glu_b, swa_sinks, mla_q_norm, mla_kv_norm, mla_w_uq, mla_w_ukv, w_out, mlp_w1, mlp_w2, final_norm_g, loss_target, m_norm1_g, m_norm2_g, m_ada_w, m_ada_b, m_w_in, m_s5_lambda_re, m_s5_lambda_im, m_s5_log_dt, m_s5_b_re, m_s5_b_im, m_s5_c_re, m_s5_c_im, m_s5_d, m_s5_glu_w, m_s5_glu_b, m_swa_sinks, m_mla_q_norm, m_mla_kv_norm, m_mla_w_uq, m_mla_w_ukv, m_w_out, m_mlp_w1, m_mlp_w2, m_final_norm_g, v_norm1_g, v_norm2_g, v_ada_w, v_ada_b, v_w_in, v_s5_lambda_re, v_s5_lambda_im, v_s5_log_dt, v_s5_b_re, v_s5_b_im, v_s5_c_re, v_s5_c_im, v_s5_d, v_s5_glu_w, v_s5_glu_b, v_swa_sinks, v_mla_q_norm, v_mla_kv_norm, v_mla_w_uq, v_mla_w_ukv, v_w_out, v_mlp_w1, v_mlp_w2, v_final_norm_g):
    names = ["norm1_g", "norm2_g", "ada_w", "ada_b", "w_in", "s5_lambda_re", "s5_lambda_im", "s5_log_dt", "s5_b_re", "s5_b_im",
             "s5_c_re", "s5_c_im", "s5_d", "s5_glu_w", "s5_glu_b", "swa_sinks", "mla_q_norm", "mla_kv_norm", "mla_w_uq",
             "mla_w_ukv", "w_out", "mlp_w1", "mlp_w2", "final_norm_g"]
    env = locals()
    wts = {n: env[n] for n in names}
    mom = {n: env["m_" + n] for n in names}
    var = {n: env["v_" + n] for n in names}
    t = x.shape[1]
    me = 4 * lax.axis_index("x") + 2 * lax.axis_index("y") + lax.axis_index("c")
    rope = _rope_tables(t)
    ret_lgam = jnp.log1p(-(2.0 ** (-5.0 - jnp.arange(RET_HEADS, dtype=F32))))

    tr = lambda a: a.transpose(0, 2, 1)
    shard = {"w_in_t": tr(w_in), "w1_t": tr(mlp_w1), "w_uq_t": tr(mla_w_uq), "w_ukv_t": tr(mla_w_ukv),
             "w_out": w_out, "w2": mlp_w2, "glu_w": s5_glu_w}
    to_send = [{k: shard[k][l].astype(BF16) for k in BIG} for l in range(DEPTH)]
    as_rows = lambda keys, arrs: {k: a.reshape(-1, shard[k].shape[2]) for k, a in zip(keys, arrs)}
    first = [k for k in BIG if k not in LATE_BIG]
    gathered = _all_gather([to_send[0][k] for k in first] + [c], "gather_weights_first")
    c_all = gathered[-1].reshape(N_DEV, D_MODEL)
    big = [as_rows(first, gathered[:len(first)]), None]
    own_slot = lambda k: 4 * lax.axis_index("x") + 2 * lax.axis_index("y") + lax.axis_index("c")

    def gather_start(arrs, tag):
        return _push_start(arrs, [jax.ShapeDtypeStruct((N_DEV,) + a.shape, a.dtype) for a in arrs], GATHER_PEERS,
                           lambda ref, k: ref, own_slot, f"gather_weights_{tag}_start")

    def gather_finish(started, after, tag):
        sent, landed = _push_wait(started[0], started[1], started[2], started[3], after, GATHER_PEERS, lambda ref, k: ref, own_slot,
                                  f"gather_weights_{tag}_wait")
        with_own = [lax.dynamic_update_index_in_dim(full, own, me, 0) for full, own in zip(landed, sent)]
        return _pass_to_sibling(with_own, f"gather_weights_{tag}_pass")

    gather0 = gather_start([to_send[0][k] for k in LATE_BIG], "l0")
    gather1 = gather_start([to_send[1][k] for k in BIG], "l1")

    c_act = _whole(lambda v: v * jax.nn.sigmoid(v), [c_all], [((N_DEV, D_MODEL), F32)], "cond_silu")[0]
    c_pad = jnp.concatenate([c_act, jnp.zeros((128 - N_DEV, D_MODEL), F32)], axis=0)
    cols = ada_w.shape[2]
    mod_part = [_mm(c_pad, ada_w[l], "nn", 128, cols, 512, name=f"l{l}_mod")[:N_DEV] for l in range(DEPTH)]
    mod_all = _all_gather([jnp.stack(mod_part)], "gather_mod")[0]
    mod_rows = lax.dynamic_index_in_dim(mod_all, me, axis=2, keepdims=False)
    mods = []
    for l in range(DEPTH):
        row = mod_rows[:, l].reshape(1, 6 * D_MODEL) + ada_b[l][None]
        if l == 0:
            row = row + (gather0[4][0, 0] + gather1[4][0, 0])
        mods.append([row[:, i * D_MODEL:(i + 1) * D_MODEL] for i in range(6)])

    smalls, s5_pulls = [], []
    for l in range(DEPTH):
        s5_ops, pull = jax.vjp(_s5_prep, *[wts[n][l] for n in S5_NAMES])
        s5_pulls.append(pull)
        smalls.append(dict(norm1_g=norm1_g[l][None], norm2_g=norm2_g[l][None], s5=s5_ops, s5_glu_b=s5_glu_b[l][None],
                           swa_sinks=swa_sinks[l], mla_q_norm=mla_q_norm[l][None], mla_kv_norm=mla_kv_norm[l][None], ret_lgam=ret_lgam))
    h = x[0]
    saved = []
    for l in range(DEPTH):
        if l == 0:
            late = lambda after: as_rows(LATE_BIG, gather_finish(gather0, after, "l0"))
        else:
            big[1] = as_rows(BIG, gather_finish(gather1, h, "l1"))
            late = None
        h, s = _layer_fwd(h, mods[l], big[l], smalls[l], rope, l, late)
        big[l] = s.pop("w")
        saved.append(s)

    fg = final_norm_g[None]
    tgt = loss_target[0]
    loss_local = _ew(_final_fn, [(h, True), (fg, False), (tgt, True)], [(1, F32, "acc")], ROWS, "loss")[0]

    def final_bwd(hh, g, tg):
        dh_, dg, _ = _vjp_block(_final_fn, 3)(hh, g, tg, jnp.ones((1, 1), F32))
        return dh_, dg

    dh, g_final = _ew(final_bwd, [(h, True), (fg, False), (tgt, True)], [(D_MODEL, F32, "tile"), (D_MODEL, F32, "acc")], ROWS, "loss_bwd")
    loss = lax.psum(loss_local[0, 0], ("x", "y", "c"))

    core, chip = lax.axis_index("c"), 2 * lax.axis_index("x") + lax.axis_index("y")

    def core_stage(g_layer, keys, tag):
        g_list = [g_layer[k].reshape(N_DEV, -1, g_layer[k].shape[1]) for k in keys]
        got = _scatter_core(g_list, f"scatter_core_{tag}")
        return [_pick_sum(g, o, core, lambda q, c_: 2 * q + c_, BF16, f"{tag}_core_sum_{k}") for k, g, o in zip(keys, g_list, got)]

    def their_block(ref, k):
        x_, y_ = lax.axis_index("x"), lax.axis_index("y")
        dx, dy, _ = CHIP_PEERS[k]
        return ref.at[2 * (1 - x_ if dx else x_) + (1 - y_ if dy else y_)]

    def chips_start(halves, tag):
        return _push_start(halves, [jax.ShapeDtypeStruct((3,) + a.shape[1:], a.dtype) for a in halves], CHIP_PEERS,
                           their_block, lambda k: k, f"scatter_chips_{tag}_start")

    def chips_wait(started, after, tag):
        return _push_wait(started[0], started[1], started[2], started[3], after, CHIP_PEERS, their_block, lambda k: k, f"scatter_chips_{tag}_wait")

    g_small, dmods = [None] * DEPTH, [None] * DEPTH
    dh, g_big1, g_small[1], dmods[1] = _layer_bwd(dh, mods[1], big[1], smalls[1], rope, saved[1], 1)
    chips1 = chips_start(core_stage(g_big1, BIG, "l1"), "l1")
    mods0 = [m + chips1[4][0, 0] for m in mods[0]]
    early = {}

    def after_mlp(gb):
        early["mlp"] = chips_start(core_stage(gb, MLP_BIG, "l0_mlp"), "l0_mlp")
        return early["mlp"][4][0, 0]

    dh, g_big0, g_small[0], dmods[0] = _layer_bwd(dh, mods0, big[0], smalls[0], rope, saved[0], 0, after_mlp=after_mlp)
    grad_x = dh[None]
    rest = [k for k in BIG if k not in MLP_BIG]
    halves_rest = core_stage(g_big0, rest, "l0_rest")
    halves1, landed1 = chips_wait(chips1, dh, "l1")
    halves_mlp, landed_mlp = chips_wait(early["mlp"], dh, "l0_mlp")
    landed_rest = _scatter_chips(halves_rest, "scatter_chips_l0_rest")
    terms = {(1, k): pair for k, pair in zip(BIG, zip(halves1, landed1))}
    terms.update({(0, k): pair for k, pair in zip(MLP_BIG, zip(halves_mlp, landed_mlp))})
    terms.update({(0, k): pair for k, pair in zip(rest, zip(halves_rest, landed_rest))})
    g_shard = [_pick_sum(*terms[l, k], chip, lambda q, m_: m_, F32, f"l{l}_chip_sum_{k}") for l in range(DEPTH) for k in BIG]

    small_parts = []
    for l in range(DEPTH):
        gs = g_small[l]
        s5g = s5_pulls[l](gs["s5"])
        small_parts += [gs["norm1_g"], gs["norm2_g"], *s5g, gs["s5_glu_b"], gs["swa_sinks"], gs["mla_q_norm"], gs["mla_kv_norm"]]
    small_parts += [g_final, *dmods]
    sizes = [int(np.prod(p.shape)) for p in small_parts]
    flat = jnp.concatenate([p.reshape(1, -1) for p in small_parts], axis=1)
    pad = (-flat.shape[1]) % 8192
    flat = jnp.pad(flat, ((0, 0), (0, pad)))
    flat_all = _all_gather([flat], "gather_small_grads")[0].reshape(N_DEV, -1)
    summed = _sum8(flat_all, "sum_small_grads")
    pieces, off = [], 0
    for sz in sizes:
        pieces.append(summed[0, off:off + sz])
        off += sz
    small_names = ["norm1_g", "norm2_g", *S5_NAMES, "s5_glu_b", "swa_sinks", "mla_q_norm", "mla_kv_norm"]
    per_layer = len(small_names)
    grads = {}
    for i, n in enumerate(small_names):
        grads[n] = jnp.stack([pieces[l * per_layer + i].reshape(wts[n].shape[1:]) for l in range(DEPTH)])
    grads["final_norm_g"] = pieces[DEPTH * per_layer]
    grads["ada_b"] = jnp.stack([pieces[DEPTH * per_layer + 1 + l] for l in range(DEPTH)])

    mod_off = sum(sizes[:DEPTH * per_layer + 1])
    dmod_all = flat_all[:, mod_off:mod_off + DEPTH * 6 * D_MODEL].reshape(N_DEV, DEPTH, N_DEV, cols)
    dmod_mine = lax.dynamic_index_in_dim(dmod_all, me, axis=2, keepdims=False).transpose(1, 0, 2)
    dmod_pad = jnp.concatenate([dmod_mine, jnp.zeros((DEPTH, 128 - N_DEV, cols), F32)], axis=1)
    grads["ada_w"] = jnp.stack([_mm(c_pad, dmod_pad[l], "tn", 512, cols, 128, name=f"l{l}_ada_w_grad") for l in range(DEPTH)])

    out_g, out_d, out_m, out_v = dict(grads), {}, {}, {}
    orig = {"w_in_t": "w_in", "w1_t": "mlp_w1", "w_uq_t": "mla_w_uq", "w_ukv_t": "mla_w_ukv", "w_out": "w_out", "w2": "mlp_w2", "glu_w": "s5_glu_w"}
    for i, k in enumerate(BIG):
        out_g[orig[k]] = jnp.stack([g_shard[l * len(BIG) + i] for l in range(DEPTH)])
        if k.endswith("_t"):
            out_g[orig[k]] = tr(out_g[orig[k]])
    for n in [*orig.values(), "ada_w"]:
        out_d[n], out_m[n], out_v[n] = _adamw_shard(wts[n], out_g[n], mom[n], var[n], f"adamw_{n}")
    small_all = small_names + ["ada_b", "final_norm_g"]
    ds, ms, vs = _adamw_small([wts[n] for n in small_all], [grads[n] for n in small_all], [mom[n] for n in small_all],
                              [var[n] for n in small_all], "adamw_small")
    for n, d, m_, v_ in zip(small_all, ds, ms, vs):
        out_d[n], out_m[n], out_v[n] = d, m_, v_
    return (loss, grad_x, *[out_g[n] for n in names], *[out_d[n] for n in names], *[out_m[n] for n in names], *[out_v[n] for n in names])
```

```python
import functools
import math

import numpy as np
import jax
import jax.numpy as jnp
from jax import lax
from jax.experimental import pallas as pl
from jax.experimental.pallas import tpu as pltpu

F32 = jnp.float32
BF16 = jnp.bfloat16
_MXU_DTYPE = jnp.bfloat16

N_DEV = 8
D_MODEL = 2048
DEPTH = 2
GROUP_WIDTH = 512
D_FF = 8192
S5_CH, S5_GROUPS, S5_STATE = 16, 32, 64
S5_WIDTH = S5_GROUPS * S5_STATE
S5_PACK = 8
S5_BLOCKS = S5_GROUPS // S5_PACK
RET_HEADS, RET_QK, RET_V, RET_CHUNK = 4, 64, 128, 128
SWA_HD, SWA_HEADS, SWA_KV_HEADS, WINDOW = 64, 8, 2, 128
MLA_HEADS, MLA_Q_RANK, MLA_KV_RANK, MLA_NOPE, MLA_ROPE, MLA_V = 4, 384, 128, 128, 64, 128
ROPE_BASE = 10000.0
EPS = 1e-6
NEG = -1e30
N_IN = 3392
ADAM_LR, ADAM_B1, ADAM_B2, ADAM_EPS, ADAM_WD, ADAM_STEP = 0.001, 0.9, 0.999, 1e-08, 0.01, 10

VMEM_LIMIT_BYTES = 52 * 1024 * 1024
MESH_ID = pl.DeviceIdType.MESH
_ANY = pl.BlockSpec(memory_space=pl.ANY)
_SMEM = pl.BlockSpec(memory_space=pltpu.SMEM)


def _params(sem):
    return pltpu.CompilerParams(dimension_semantics=sem, vmem_limit_bytes=VMEM_LIMIT_BYTES)


_DIMS = {"nn": (((1,), (0,)), ((), ())), "nt": (((1,), (1,)), ((), ())), "tn": (((0,), (0,)), ((), ()))}


def _dot(a, b, mode="nn"):
    return lax.dot_general(a.astype(_MXU_DTYPE), b.astype(_MXU_DTYPE), _DIMS[mode], preferred_element_type=F32)


def _mm(a, b, mode, tm, tn, tk, out_dtype=F32, name="mm", b_off=0, n=None, pair=None, epi=None, epi_ins=(), epi_outs=None):
    if mode == "tn":
        kdim, m = a.shape
    else:
        m, kdim = a.shape
    if n is None:
        n = b.shape[0] if mode == "nt" else b.shape[1]
    tm, tn, tk = min(tm, m), min(tn, n), min(tk, kdim)
    assert m % tm == 0 and n % tn == 0 and kdim % tk == 0, (name, a.shape, b.shape, tm, tn, tk)
    nk = kdim // tk
    a_spec = pl.BlockSpec((tk, tm), lambda i, j, k: (k, i)) if mode == "tn" else pl.BlockSpec((tm, tk), lambda i, j, k: (i, k))
    if mode == "nt":
        b_spec = pl.BlockSpec((tn, tk), lambda i, j, k: (j + b_off, k))
    else:
        b_spec = pl.BlockSpec((tk, tn), lambda i, j, k: (k, j + b_off))
    o_spec = pl.BlockSpec((tm, tn), lambda i, j, k: (i, j))
    n_mm = 2 if pair is None else 4
    out_dtypes = [out_dtype] if epi is None else list(epi_outs)

    def body(*refs):
        ins, extra = refs[:n_mm], refs[n_mm:n_mm + len(epi_ins)]
        outs = refs[n_mm + len(epi_ins):n_mm + len(epi_ins) + len(out_dtypes)]
        part = _dot(ins[0][...], ins[1][...], mode)
        if pair is not None:
            part = part + _dot(ins[2][...], ins[3][...], mode)

        def finish(acc):
            vals = (acc,) if epi is None else epi(acc, *[r[...] for r in extra])
            for o_ref, v, dt in zip(outs, vals, out_dtypes):
                o_ref[...] = v.astype(dt)

        if nk == 1:
            finish(part)
        else:
            acc_ref = refs[-1]
            k = pl.program_id(2)

            @pl.when(k == 0)
            def _():
                acc_ref[...] = part

            @pl.when(k > 0)
            def _():
                acc_ref[...] += part

            @pl.when(k == nk - 1)
            def _():
                finish(acc_ref[...])

    operands = [a, b] + ([] if pair is None else list(pair)) + list(epi_ins)
    res = pl.pallas_call(
        body, name=name, grid=(m // tm, n // tn, nk),
        in_specs=[a_spec, b_spec] * (n_mm // 2) + [o_spec] * len(epi_ins),
        out_specs=[o_spec] * len(out_dtypes), out_shape=[jax.ShapeDtypeStruct((m, n), dt) for dt in out_dtypes],
        scratch_shapes=[] if nk == 1 else [pltpu.VMEM((tm, tn), F32)],
        compiler_params=_params(("parallel", "parallel", "arbitrary")),
    )(*operands)
    return res[0] if epi is None else res


def _mm_blocks(a, b, mode, tm, a_of=None, pair=None, name="mm_blocks"):
    a_of = a_of or (lambda j: j)
    m = a.shape[0]
    nj, kb, nb = b.shape
    a_w, o_w = (kb, nb) if mode == "nn" else (nb, kb)
    tm = min(tm, m)
    a_spec = pl.BlockSpec((tm, a_w), lambda i, j: (i, a_of(j)))
    b_spec = pl.BlockSpec((1, kb, nb), lambda i, j: (j, 0, 0))
    n_in = 2 if pair is None else 4

    def body(*refs):
        acc = _dot(refs[0][...], refs[1][0], mode)
        if pair is not None:
            acc = acc + _dot(refs[2][...], refs[3][0], mode)
        refs[n_in][...] = acc

    operands = [a, b] + ([] if pair is None else list(pair))
    return pl.pallas_call(
        body, name=name, grid=(m // tm, nj), in_specs=[a_spec, b_spec] * (n_in // 2),
        out_specs=pl.BlockSpec((tm, o_w), lambda i, j: (i, j)), out_shape=jax.ShapeDtypeStruct((m, nj * o_w), F32),
        compiler_params=_params(("parallel", "parallel")),
    )(*operands)


def _mm_blocks_tn(a, b, x, y, b_of, name):
    kdim = a.shape[0]
    nj = a.shape[1] // x

    def body(a_ref, b_ref, o_ref):
        o_ref[0] = _dot(a_ref[...], b_ref[...], "tn")

    return pl.pallas_call(
        body, name=name, grid=(nj,), in_specs=[pl.BlockSpec((kdim, x), lambda j: (0, j)), pl.BlockSpec((kdim, y), lambda j: (0, b_of(j)))],
        out_specs=pl.BlockSpec((1, x, y), lambda j: (j, 0, 0)), out_shape=jax.ShapeDtypeStruct((nj, x, y), F32),
        compiler_params=_params(("parallel",)),
    )(a, b)


SUBLANES = 8


def _row_tile(rows, target, mult=SUBLANES):
    best = None
    for cand in range(mult, min(rows, target) + 1, mult):
        if rows % cand == 0:
            best = cand
    return best or rows


def _ew(fn, ins, outs, tt, name):
    t = [a.shape[0] for a, tiled in ins if tiled][0]
    tt = _row_tile(t, tt)
    n_in = len(ins)
    in_specs = [pl.BlockSpec((tt, a.shape[1]), lambda i: (i, 0)) if tiled else pl.BlockSpec(a.shape, lambda i: (0, 0))
                for a, tiled in ins]
    out_specs, out_shapes = [], []
    for w, dt, kind in outs:
        if kind == "tile":
            out_specs.append(pl.BlockSpec((tt, w), lambda i: (i, 0)))
            out_shapes.append(jax.ShapeDtypeStruct((t, w), dt))
        else:
            out_specs.append(pl.BlockSpec((1, w), lambda i: (0, 0)))
            out_shapes.append(jax.ShapeDtypeStruct((1, w), F32))
    has_acc = any(kind == "acc" for _, _, kind in outs)

    def body(*refs):
        vals = fn(*[r[...] for r in refs[:n_in]])
        if not isinstance(vals, (tuple, list)):
            vals = (vals,)
        i = pl.program_id(0)
        for o_ref, v, (w, dt, kind) in zip(refs[n_in:], vals, outs):
            if kind == "tile":
                o_ref[...] = v.astype(dt)
            else:
                @pl.when(i == 0)
                def _(o_ref=o_ref, v=v):
                    o_ref[...] = v.astype(F32)

                @pl.when(i > 0)
                def _(o_ref=o_ref, v=v):
                    o_ref[...] += v.astype(F32)

    res = pl.pallas_call(
        body, name=name, grid=(t // tt,), in_specs=in_specs, out_specs=out_specs, out_shape=out_shapes,
        compiler_params=_params(("arbitrary" if has_acc else "parallel",)),
    )(*[a for a, _ in ins])
    return res


def _whole(fn, ins, outs, name):
    def body(*refs):
        vals = fn(*[r[...] for r in refs[:len(ins)]])
        if not isinstance(vals, (tuple, list)):
            vals = (vals,)
        for o_ref, v in zip(refs[len(ins):], vals):
            o_ref[...] = v.astype(o_ref.dtype)

    return pl.pallas_call(body, name=name, out_shape=[jax.ShapeDtypeStruct(s, dt) for s, dt in outs])(*ins)


def _rms(x):
    return x * lax.rsqrt(jnp.mean(x * x, axis=-1, keepdims=True) + EPS)


def _norm_mod_fn(h, g, sc, sh):
    return (_rms(h) * g) * (1.0 + sc) + sh


def _rms_gain_fn(x, g):
    return _rms(x) * g


def _gate_add_fn(h, y, gt):
    return h + gt * y


def _relu2_fn(x):
    return jnp.square(jnp.maximum(x, 0.0))


def _s5_act_fn(ypre, u, dskip):
    return jax.nn.gelu(ypre + dskip * u)


def _s5_glu_fn(z, zz, b):
    return z * jax.nn.sigmoid(zz + b)


def _ret_gate_fn(o, g):
    return _rms(o) * (g * jax.nn.sigmoid(g))


def _final_fn(h, g, tgt):
    err = _rms(h) * g - tgt
    return 0.5 * jnp.sum(jnp.mean(err * err, axis=-1, keepdims=True), axis=0, keepdims=True)


def _vjp_block(fn, n_args):
    def bwd(*vals):
        _, pull = jax.vjp(fn, *vals[:n_args])
        return pull(vals[n_args])
    return bwd


def _rope_tables(t):
    d = RET_QK
    inv = ROPE_BASE ** (-jnp.arange(0, d, 2, dtype=F32) / d)
    ang = jnp.arange(t, dtype=F32)[:, None] * inv[None, :]
    cos, sin = jnp.cos(ang), jnp.sin(ang)
    cos2, sin2 = jnp.concatenate([cos, cos], -1), jnp.concatenate([-sin, sin], -1)
    ret = (jnp.tile(cos2, (1, 8)), jnp.tile(sin2, (1, 8)))
    one, zero = jnp.ones((t, MLA_NOPE), F32), jnp.zeros((t, MLA_NOPE), F32)
    mla_c = jnp.concatenate([jnp.tile(jnp.concatenate([one, cos2], -1), (1, MLA_HEADS)), cos2, one[:, :d]], -1)
    mla_s = jnp.concatenate([jnp.tile(jnp.concatenate([zero, sin2], -1), (1, MLA_HEADS)), sin2, zero[:, :d]], -1)
    return ret, (mla_c, mla_s)


def _rope_fn(x, c, s, sign):
    w = x.shape[1]
    lane = lax.broadcasted_iota(jnp.int32, x.shape, 1)
    swapped = jnp.where((lane & 63) < 32, pltpu.roll(x, w - 32, 1), pltpu.roll(x, 32, 1))
    return x * c + swapped * (sign * s)


def _rope(x, tables, name, inverse=False, out_dtype=F32):
    c, s = tables
    fn = functools.partial(_rope_fn, sign=-1.0 if inverse else 1.0)
    return _ew(fn, [(x, True), (c, True), (s, True)], [(x.shape[1], out_dtype, "tile")], ROWS, name)[0]


SCAN_ROWS, SCAN_LANES = 256, 512


def _cmul(ar, ai, br, bi):
    return ar * br - ai * bi, ar * bi + ai * br


def _group_powers(ar, ai, reverse):
    shape = (SUBLANES, ar.shape[1])
    row = lax.broadcasted_iota(jnp.int32, shape, 0)
    pr, pi = ar, ai
    out_r, out_i = jnp.zeros(shape, F32), jnp.zeros(shape, F32)
    for e in range(1, SUBLANES + 1):
        hit = row == (SUBLANES - e if reverse else e - 1)
        out_r, out_i = jnp.where(hit, pr, out_r), jnp.where(hit, pi, out_i)
        if e < SUBLANES:
            pr, pi = _cmul(pr, pi, ar, ai)
    return out_r, out_i


def _scan_chunk(in_r_ref, in_i_ref, out_r_ref, out_i_ref, ar, ai, cr, ci, reverse, visit=None):
    rows, lanes = in_r_ref.shape
    sub = lax.broadcasted_iota(jnp.int32, (SUBLANES, lanes), 0)
    edge_r, edge_i = _group_powers(ar, ai, reverse)
    steps, pr, pi, k = [], ar, ai, 1
    while k < SUBLANES:
        steps.append((k, pr, pi))
        pr, pi = _cmul(pr, pi, pr, pi)
        k *= 2
    groups = range(rows // SUBLANES)
    for g in (reversed(groups) if reverse else groups):
        sl = slice(g * SUBLANES, (g + 1) * SUBLANES)
        xr, xi = in_r_ref[sl, :], in_i_ref[sl, :]
        for k, pr, pi in steps:
            shift = SUBLANES - k if reverse else k
            keep = sub < SUBLANES - k if reverse else sub >= k
            tr, ti = _cmul(pr, pi, pltpu.roll(xr, shift, 0), pltpu.roll(xi, shift, 0))
            xr, xi = xr + jnp.where(keep, tr, 0.0), xi + jnp.where(keep, ti, 0.0)
        tr, ti = _cmul(edge_r, edge_i, cr, ci)
        xr, xi = xr + tr, xi + ti
        out_r_ref[sl, :] = xr
        out_i_ref[sl, :] = xi
        if visit is not None:
            visit(sl, xr, xi, cr, ci)
        edge = slice(0, 1) if reverse else slice(SUBLANES - 1, SUBLANES)
        cr, ci = xr[edge, :], xi[edge, :]
    return cr, ci


def _s5_scan_fwd(bu, a_r, a_i, name):
    t = bu.shape[0]
    rows = min(SCAN_ROWS, t)
    nl = S5_WIDTH // SCAN_LANES

    def body(br_ref, bi_ref, ar_ref, ai_ref, or_ref, oi_ref, cr_ref, ci_ref):
        i = pl.program_id(1)

        @pl.when(i == 0)
        def _():
            cr_ref[...] = jnp.zeros_like(cr_ref)
            ci_ref[...] = jnp.zeros_like(ci_ref)

        ar, ai = ar_ref[...], ai_ref[...]
        cr, ci = _scan_chunk(br_ref, bi_ref, or_ref, oi_ref, ar, ai, cr_ref[...], ci_ref[...], reverse=False)
        cr_ref[...] = cr
        ci_ref[...] = ci

    blk = lambda off: pl.BlockSpec((rows, SCAN_LANES), lambda j, i: (i, j + off))
    par = pl.BlockSpec((1, SCAN_LANES), lambda j, i: (0, j))
    st_r, st_i = pl.pallas_call(
        body, name=name, grid=(nl, t // rows), in_specs=[blk(0), blk(nl), par, par], out_specs=[blk(0), blk(0)],
        out_shape=[jax.ShapeDtypeStruct((t, S5_WIDTH), F32)] * 2,
        scratch_shapes=[pltpu.VMEM((1, SCAN_LANES), F32)] * 2, compiler_params=_params(("parallel", "arbitrary")),
    )(bu, bu, a_r, a_i)
    return st_r, st_i


def _s5_scan_bwd(dst, st_r, st_i, a_r, a_i, name):
    t = dst.shape[0]
    rows = min(SCAN_ROWS, t)
    nl = S5_WIDTH // SCAN_LANES
    nc = t // rows

    def body(dr_ref, di_ref, xr_ref, xi_ref, ar_ref, ai_ref, gr_ref, gi_ref, dar_ref, dai_ref, cr_ref, ci_ref):
        i = pl.program_id(1)

        @pl.when(i == 0)
        def _():
            cr_ref[...] = jnp.zeros_like(cr_ref)
            ci_ref[...] = jnp.zeros_like(ci_ref)
            dar_ref[...] = jnp.zeros_like(dar_ref)
            dai_ref[...] = jnp.zeros_like(dai_ref)

        ar, ai = ar_ref[...], ai_ref[...]
        cr, ci = cr_ref[...], ci_ref[...]
        last = lax.broadcasted_iota(jnp.int32, (SUBLANES, SCAN_LANES), 0) == SUBLANES - 1
        sums = [jnp.zeros((SUBLANES, SCAN_LANES), F32), jnp.zeros((SUBLANES, SCAN_LANES), F32)]

        def visit(sl, gr, gi, next_r, next_i):
            nr = jnp.where(last, next_r, pltpu.roll(gr, SUBLANES - 1, 0))
            ni = jnp.where(last, next_i, pltpu.roll(gi, SUBLANES - 1, 0))
            xr, xi = xr_ref[sl, :], xi_ref[sl, :]
            sums[0] = sums[0] + (nr * xr + ni * xi)
            sums[1] = sums[1] + (ni * xr - nr * xi)

        first_r, first_i = _scan_chunk(dr_ref, di_ref, gr_ref, gi_ref, ar, -ai, cr, ci, reverse=True, visit=visit)
        dar_ref[...] += jnp.sum(sums[0], axis=0, keepdims=True)
        dai_ref[...] += jnp.sum(sums[1], axis=0, keepdims=True)
        cr_ref[...] = first_r
        ci_ref[...] = first_i

    blk = lambda off: pl.BlockSpec((rows, SCAN_LANES), lambda j, i: (nc - 1 - i, j + off))
    par = pl.BlockSpec((1, SCAN_LANES), lambda j, i: (0, j))
    return pl.pallas_call(
        body, name=name, grid=(nl, nc), in_specs=[blk(0), blk(nl), blk(0), blk(0), par, par],
        out_specs=[blk(0), blk(0), par, par],
        out_shape=[jax.ShapeDtypeStruct((t, S5_WIDTH), F32)] * 2 + [jax.ShapeDtypeStruct((1, S5_WIDTH), F32)] * 2,
        scratch_shapes=[pltpu.VMEM((1, SCAN_LANES), F32)] * 2, compiler_params=_params(("parallel", "arbitrary")),
    )(dst, dst, st_r, st_i, a_r, a_i)


def _s5_prep(lam_re, lam_im, log_dt, b_re, b_im, c_re, c_im, d_skip):
    dt = jnp.exp(log_dt)[:, None]
    mag = jnp.exp(lam_re * dt)
    ar, ai = mag * jnp.cos(lam_im * dt), mag * jnp.sin(lam_im * dt)
    den = lam_re * lam_re + lam_im * lam_im
    cr = ((ar - 1.0) * lam_re + ai * lam_im) / den
    ci = (ai * lam_re - (ar - 1.0) * lam_im) / den
    bbar_r = cr[..., None] * b_re - ci[..., None] * b_im
    bbar_i = cr[..., None] * b_im + ci[..., None] * b_re
    eye = jnp.eye(S5_PACK, dtype=F32)

    def bdiag(m):
        g, a, b = m.shape
        m4 = m.reshape(g // S5_PACK, S5_PACK, a, b)
        return (eye[None, :, None, :, None] * m4[:, :, :, None, :]).reshape(g // S5_PACK, S5_PACK * a, S5_PACK * b)

    b3 = jnp.concatenate([bdiag(bbar_r.transpose(0, 2, 1)), bdiag(bbar_i.transpose(0, 2, 1))], axis=0)
    c3 = jnp.concatenate([bdiag(c_re.transpose(0, 2, 1)), -bdiag(c_im.transpose(0, 2, 1))], axis=0)
    return b3, c3, d_skip.reshape(1, GROUP_WIDTH), ar.reshape(1, S5_WIDTH), ai.reshape(1, S5_WIDTH)


def _ret_consts(lgam):
    c = RET_CHUNK
    r = lax.broadcasted_iota(jnp.int32, (c, c), 0)
    m = lax.broadcasted_iota(jnp.int32, (c, c), 1)
    rel = (r - m).astype(F32)
    decay = jnp.where(rel >= 0, jnp.exp(lgam * jnp.maximum(rel, 0.0)), 0.0)
    idx = lax.broadcasted_iota(jnp.int32, (c, 1), 0).astype(F32)
    zeta = jnp.exp(lgam * (c - 1.0 - idx))
    xi = jnp.exp(lgam * (idx + 1.0))
    return decay, zeta, xi, jnp.exp(lgam * c)


def _ret_specs(t):
    qk = lambda off: pl.BlockSpec((1, t, RET_QK), lambda h: (h + off, 0, 0))
    col = lambda off: pl.BlockSpec((t, RET_V), lambda h: (0, h + off))
    return qk, col


def _ret_fwd(qk, p_ret, lgam, name):
    t = qk.shape[1]
    nck = t // RET_CHUNK
    qk_spec, col = _ret_specs(t)

    def body(lg_ref, q_ref, k_ref, v_ref, g_ref, o_ref, y_ref):
        decay, zeta, xi, gam = _ret_consts(lg_ref[pl.program_id(0)])

        def step(n, state):
            sl = pl.ds(pl.multiple_of(n * RET_CHUNK, RET_CHUNK), RET_CHUNK)
            q, k, v = q_ref[0, sl, :], k_ref[0, sl, :] * (RET_QK ** -0.5), v_ref[sl, :]
            s = _dot(q, k, "nt") * decay
            o = _dot(s, v) + _dot(q, state) * xi
            o_ref[sl, :] = o
            y_ref[sl, :] = _ret_gate_fn(o, g_ref[sl, :]).astype(y_ref.dtype)
            return gam * state + _dot(k, zeta * v, "tn")

        lax.fori_loop(0, nck, step, jnp.zeros((RET_QK, RET_V), F32))

    return pl.pallas_call(
        body, name=name, grid=(RET_HEADS,), in_specs=[_SMEM, qk_spec(0), qk_spec(RET_HEADS), col(4), col(8)],
        out_specs=[col(0), col(0)],
        out_shape=[jax.ShapeDtypeStruct((t, GROUP_WIDTH), F32), jax.ShapeDtypeStruct((t, GROUP_WIDTH), BF16)],
        compiler_params=_params(("parallel",)),
    )(lgam, qk, qk, p_ret, p_ret)


def _ret_bwd(qk, p_ret, o_all, dy, lgam, name):
    t = qk.shape[1]
    nck = t // RET_CHUNK
    qk_spec, col = _ret_specs(t)
    gate_bwd = _vjp_block(_ret_gate_fn, 2)

    def body(lg_ref, q_ref, k_ref, v_ref, g_ref, o_ref, dy_ref, dq_ref, dk_ref, dv_ref, dg_ref, st_ref):
        decay, zeta, xi, gam = _ret_consts(lg_ref[pl.program_id(0)])
        scale = RET_QK ** -0.5

        def fstep(n, state):
            sl = pl.ds(pl.multiple_of(n * RET_CHUNK, RET_CHUNK), RET_CHUNK)
            st_ref[n] = state
            return gam * state + _dot(k_ref[0, sl, :] * scale, zeta * v_ref[sl, :], "tn")

        lax.fori_loop(0, nck, fstep, jnp.zeros((RET_QK, RET_V), F32))

        def bstep(r, grad_state):
            n = nck - 1 - r
            sl = pl.ds(pl.multiple_of(n * RET_CHUNK, RET_CHUNK), RET_CHUNK)
            q, k, v = q_ref[0, sl, :], k_ref[0, sl, :] * scale, v_ref[sl, :]
            d_o, dg = gate_bwd(o_ref[sl, :], g_ref[sl, :], dy_ref[sl, :])
            dg_ref[sl, :] = dg
            s = _dot(q, k, "nt") * decay
            ds = _dot(d_o, v, "nt") * decay
            xdo = xi * d_o
            dq_ref[0, sl, :] = _dot(ds, k) + _dot(xdo, st_ref[n], "nt")
            dk_ref[0, sl, :] = (_dot(ds, q, "tn") + _dot(zeta * v, grad_state, "nt")) * scale
            dv_ref[sl, :] = _dot(s, d_o, "tn") + zeta * _dot(k, grad_state)
            return gam * grad_state + _dot(q, xdo, "tn")

        lax.fori_loop(0, nck, bstep, jnp.zeros((RET_QK, RET_V), F32))

    hd = pl.BlockSpec((1, t, RET_QK), lambda h: (h, 0, 0))
    return pl.pallas_call(
        body, name=name, grid=(RET_HEADS,),
        in_specs=[_SMEM, qk_spec(0), qk_spec(RET_HEADS), col(4), col(8), col(0), col(0)],
        out_specs=[hd, hd, col(0), col(0)],
        out_shape=[jax.ShapeDtypeStruct((RET_HEADS, t, RET_QK), F32)] * 2 + [jax.ShapeDtypeStruct((t, GROUP_WIDTH), F32)] * 2,
        scratch_shapes=[pltpu.VMEM((nck, RET_QK, RET_V), F32)], compiler_params=_params(("parallel",)),
    )(lgam, qk, qk, p_ret, p_ret, o_all, dy)


SWA_GROUP = SWA_HEADS // SWA_KV_HEADS
SWA_SCALE = SWA_HD ** -0.5


def _swa_mask(n):
    rows = SWA_GROUP * WINDOW
    r = lax.broadcasted_iota(jnp.int32, (rows, 2 * WINDOW), 0) & (WINDOW - 1)
    j = lax.broadcasted_iota(jnp.int32, (rows, 2 * WINDOW), 1)
    dist = r + WINDOW - j
    return (dist >= 0) & (dist < WINDOW) & (n * WINDOW + j - WINDOW >= 0)


def _swa_sink_rows(sink_ref, kv):
    row = lax.broadcasted_iota(jnp.int32, (SWA_GROUP * WINDOW, 1), 0)
    sink = jnp.zeros((SWA_GROUP * WINDOW, 1), F32)
    for g in range(SWA_GROUP):
        sink = jnp.where(row >= g * WINDOW, sink_ref[kv * SWA_GROUP + g], sink)
    return sink


def _swa_pad_keys(n, k_ref, v_ref, kp_ref, vp_ref):
    @pl.when(n == 0)
    def _():
        zero = jnp.zeros((WINDOW, SWA_HD), F32)
        kp_ref[0:WINDOW, :] = zero
        vp_ref[0:WINDOW, :] = zero
        kp_ref[WINDOW:, :] = k_ref[0]
        vp_ref[WINDOW:, :] = v_ref[0]


def _swa_specs(t):
    blk = lambda w: pl.BlockSpec((SWA_GROUP, WINDOW, w), lambda kv, n: (kv, n, 0))
    kspec = lambda off: pl.BlockSpec((1, t, SWA_HD), lambda kv, n: (SWA_HEADS + off + kv, 0, 0))
    return blk, kspec


def _swa_fwd(qkv, sinks, name):
    t = qkv.shape[1]
    rows = SWA_GROUP * WINDOW
    blk, kspec = _swa_specs(t)

    def body(sink_ref, q_ref, k_ref, v_ref, o_ref, lse_ref, kp_ref, vp_ref):
        kv, n = pl.program_id(0), pl.program_id(1)
        _swa_pad_keys(n, k_ref, v_ref, kp_ref, vp_ref)
        win = pl.ds(pl.multiple_of(n * WINDOW, WINDOW), 2 * WINDOW)
        sink = _swa_sink_rows(sink_ref, kv)
        s = _dot(q_ref[...].reshape(rows, SWA_HD), kp_ref[win, :], "nt") * SWA_SCALE
        s = jnp.where(_swa_mask(n), s, NEG)
        m = jnp.maximum(jnp.max(s, axis=-1, keepdims=True), sink)
        p = jnp.exp(s - m)
        den = jnp.sum(p, axis=-1, keepdims=True) + jnp.exp(sink - m)
        o_ref[...] = _dot(p / den, vp_ref[win, :]).reshape(SWA_GROUP, WINDOW, SWA_HD)
        lse_ref[...] = (m + jnp.log(den)).reshape(SWA_GROUP, WINDOW, 1)

    return pl.pallas_call(
        body, name=name, grid=(SWA_KV_HEADS, t // WINDOW),
        in_specs=[_SMEM, blk(SWA_HD), kspec(0), kspec(SWA_KV_HEADS)], out_specs=[blk(SWA_HD), blk(1)],
        out_shape=[jax.ShapeDtypeStruct((SWA_HEADS, t, SWA_HD), F32), jax.ShapeDtypeStruct((SWA_HEADS, t, 1), F32)],
        scratch_shapes=[pltpu.VMEM((t + WINDOW, SWA_HD), F32)] * 2, compiler_params=_params(("parallel", "arbitrary")),
    )(sinks, qkv, qkv, qkv)


def _swa_bwd(qkv, o, lse, d_o, sinks, name):
    t = qkv.shape[1]
    nb = t // WINDOW
    rows = SWA_GROUP * WINDOW
    blk, kspec = _swa_specs(t)

    def body(sink_ref, q_ref, k_ref, v_ref, o_ref, lse_ref, do_ref, dq_ref, dk_ref, dv_ref, dsink_ref,
             kp_ref, vp_ref, dkp_ref, dvp_ref):
        kv, n = pl.program_id(0), pl.program_id(1)
        _swa_pad_keys(n, k_ref, v_ref, kp_ref, vp_ref)

        @pl.when(n == 0)
        def _():
            dkp_ref[...] = jnp.zeros_like(dkp_ref)
            dvp_ref[...] = jnp.zeros_like(dvp_ref)
            dsink_ref[...] = jnp.zeros_like(dsink_ref)

        win = pl.ds(pl.multiple_of(n * WINDOW, WINDOW), 2 * WINDOW)
        sink = _swa_sink_rows(sink_ref, kv)
        q, dout = q_ref[...].reshape(rows, SWA_HD), do_ref[...].reshape(rows, SWA_HD)
        lse_n = lse_ref[...].reshape(rows, 1)
        s = _dot(q, kp_ref[win, :], "nt") * SWA_SCALE
        s = jnp.where(_swa_mask(n), s, NEG)
        p = jnp.exp(s - lse_n)
        delta = jnp.sum(dout * o_ref[...].reshape(rows, SWA_HD), axis=-1, keepdims=True)
        ds = p * (_dot(dout, vp_ref[win, :], "nt") - delta)
        dq_ref[...] = (_dot(ds, kp_ref[win, :]) * SWA_SCALE).reshape(SWA_GROUP, WINDOW, SWA_HD)
        dkp_ref[win, :] += _dot(ds, q, "tn") * SWA_SCALE
        dvp_ref[win, :] += _dot(p, dout, "tn")
        term = jnp.exp(sink - lse_n) * delta
        head = lax.broadcasted_iota(jnp.int32, (SWA_GROUP, 128), 0)
        acc = jnp.zeros((SWA_GROUP, 128), F32)
        for g in range(SWA_GROUP):
            acc = jnp.where(head == g, jnp.sum(term[g * WINDOW:(g + 1) * WINDOW], axis=0, keepdims=True), acc)
        dsink_ref[0] -= acc

        @pl.when(n == nb - 1)
        def _():
            dk_ref[0] = dkp_ref[WINDOW:, :]
            dv_ref[0] = dvp_ref[WINDOW:, :]

    kout = pl.BlockSpec((1, t, SWA_HD), lambda kv, n: (kv, 0, 0))
    dq, dk, dv, dsink = pl.pallas_call(
        body, name=name, grid=(SWA_KV_HEADS, nb),
        in_specs=[_SMEM, blk(SWA_HD), kspec(0), kspec(SWA_KV_HEADS), blk(SWA_HD), blk(1), blk(SWA_HD)],
        out_specs=[blk(SWA_HD), kout, kout, pl.BlockSpec((1, SWA_GROUP, 128), lambda kv, n: (kv, 0, 0))],
        out_shape=[jax.ShapeDtypeStruct((SWA_HEADS, t, SWA_HD), F32), jax.ShapeDtypeStruct((SWA_KV_HEADS, t, SWA_HD), F32),
                   jax.ShapeDtypeStruct((SWA_KV_HEADS, t, SWA_HD), F32), jax.ShapeDtypeStruct((SWA_KV_HEADS, SWA_GROUP, 128), F32)],
        scratch_shapes=[pltpu.VMEM((t + WINDOW, SWA_HD), F32)] * 4, compiler_params=_params(("parallel", "arbitrary")),
    )(sinks, qkv, qkv, qkv, o, lse, d_o)
    return jnp.concatenate([dq, dk, dv], axis=0), dsink[:, :, 0].reshape(SWA_HEADS)


MLA_SCALE = (MLA_NOPE + MLA_ROPE) ** -0.5
MLA_TILE = 512
MLA_KEY_TILE = 512
MLA_BWD_TILE = 512


def _mla_diag(s):
    r = lax.broadcasted_iota(jnp.int32, s.shape, 0)
    c = lax.broadcasted_iota(jnp.int32, s.shape, 1)
    return jnp.where(c <= r, s, NEG)


def _mla_specs(t, tile):
    whole = lambda w, off: pl.BlockSpec((t, w), lambda h, i: (0, 2 * h + off))
    head = lambda w: pl.BlockSpec((1, t, w), lambda h, i: (h, 0, 0))
    key_rope = pl.BlockSpec((1, t, MLA_ROPE), lambda h, i: (MLA_HEADS, 0, 0))
    tile_of = lambda w: pl.BlockSpec((1, tile, w), lambda h, i: (h, i, 0))
    return whole, head, key_rope, tile_of


def _mla_attend(qn, rot, kv, name):
    t = qn.shape[1]
    tile = min(MLA_TILE, t)
    ktile = min(MLA_KEY_TILE, t)
    ratio = ktile // tile
    whole, head, key_rope, tile_of = _mla_specs(t, tile)

    def body(qn_ref, qr_ref, kn_ref, kr_ref, v_ref, o_ref, lse_ref, m_ref, l_ref, acc_ref):
        i = pl.program_id(1)
        qn_b, qr_b = qn_ref[0], qr_ref[0]

        def rows(j):
            return pl.ds(pl.multiple_of(j * ktile, ktile), ktile)

        def scores(j):
            return (_dot(qn_b, kn_ref[rows(j), :], "nt") + _dot(qr_b, kr_ref[0, rows(j), :], "nt")) * MLA_SCALE

        def causal(s, j):
            qpos = i * tile + lax.broadcasted_iota(jnp.int32, s.shape, 0)
            kpos = j * ktile + lax.broadcasted_iota(jnp.int32, s.shape, 1)
            return jnp.where(kpos <= qpos, s, NEG)

        def update(s, j):
            m_old = m_ref[...]
            m_new = jnp.maximum(m_old, jnp.max(s, axis=-1, keepdims=True))
            alpha = jnp.exp(m_old - m_new)
            p = jnp.exp(s - m_new)
            l_ref[...] = alpha * l_ref[...] + jnp.sum(p, axis=-1, keepdims=True)
            acc_ref[...] = alpha * acc_ref[...] + _dot(p, v_ref[rows(j), :])
            m_ref[...] = m_new

        m_ref[...] = jnp.full_like(m_ref, NEG)
        l_ref[...] = jnp.zeros_like(l_ref)
        acc_ref[...] = jnp.zeros_like(acc_ref)

        def step(j, s_cur):
            s_next = scores(j + 1)
            update(s_cur, j)
            return s_next

        last = i // ratio
        s_last = lax.fori_loop(0, last, step, scores(0))
        update(causal(s_last, last), last)
        o_ref[...] = acc_ref[...] / l_ref[...]
        lse_ref[0] = m_ref[...] + jnp.log(l_ref[...])

    return pl.pallas_call(
        body, name=name, grid=(MLA_HEADS, t // tile),
        in_specs=[tile_of(MLA_NOPE), tile_of(MLA_ROPE), whole(MLA_NOPE, 0), key_rope, whole(MLA_V, 1)],
        out_specs=[pl.BlockSpec((tile, MLA_V), lambda h, i: (i, h)), tile_of(1)],
        out_shape=[jax.ShapeDtypeStruct((t, GROUP_WIDTH), F32), jax.ShapeDtypeStruct((MLA_HEADS, t, 1), F32)],
        scratch_shapes=[pltpu.VMEM((tile, 1), F32), pltpu.VMEM((tile, 1), F32), pltpu.VMEM((tile, MLA_V), F32)],
        compiler_params=_params(("parallel", "parallel")),
    )(qn, rot, kv, rot, kv)


def _mla_delta(o, d_o, name):
    t = o.shape[0]
    tt = min(ROWS, t)

    def body(o_ref, do_ref, d_ref):
        d_ref[0] = jnp.sum(o_ref[...] * do_ref[...], axis=-1, keepdims=True)

    blk = pl.BlockSpec((tt, MLA_V), lambda h, i: (i, h))
    return pl.pallas_call(
        body, name=name, grid=(MLA_HEADS, t // tt), in_specs=[blk, blk],
        out_specs=pl.BlockSpec((1, tt, 1), lambda h, i: (h, i, 0)), out_shape=jax.ShapeDtypeStruct((MLA_HEADS, t, 1), F32),
        compiler_params=_params(("parallel", "parallel")),
    )(o, d_o)


def _mla_attend_bwd(qn, rot, kv, lse, delta, d_o, name):
    t = qn.shape[1]
    tile = min(MLA_BWD_TILE, t)
    nt = t // tile
    whole, head, key_rope, tile_of = _mla_specs(t, tile)

    def body(qn_ref, qr_ref, kn_ref, kr_ref, v_ref, lse_ref, dl_ref, do_ref, dqn_ref, dqr_ref, dkn_ref, dv_ref, dkr_ref):
        j = pl.program_id(1)

        @pl.when(j == 0)
        def _():
            dqn_ref[...] = jnp.zeros_like(dqn_ref)
            dqr_ref[...] = jnp.zeros_like(dqr_ref)

        dkn_ref[...] = jnp.zeros_like(dkn_ref)
        dv_ref[...] = jnp.zeros_like(dv_ref)
        dkr_ref[...] = jnp.zeros_like(dkr_ref)
        kn_b, kr_b, v_b = kn_ref[...], kr_ref[0], v_ref[...]

        def block(i, diagonal):
            sl = pl.ds(pl.multiple_of(i * tile, tile), tile)
            qn_b, qr_b, dout = qn_ref[0, sl, :], qr_ref[0, sl, :], do_ref[sl, :]
            s = (_dot(qn_b, kn_b, "nt") + _dot(qr_b, kr_b, "nt")) * MLA_SCALE
            if diagonal:
                s = _mla_diag(s)
            p = jnp.exp(s - lse_ref[0, sl, :])
            ds = p * (_dot(dout, v_b, "nt") - dl_ref[0, sl, :]) * MLA_SCALE
            dv_ref[...] += _dot(p, dout, "tn")
            dkn_ref[...] += _dot(ds, qn_b, "tn")
            dkr_ref[0] += _dot(ds, qr_b, "tn")
            dqn_ref[0, sl, :] += _dot(ds, kn_b)
            dqr_ref[0, sl, :] += _dot(ds, kr_b)

        block(j, True)

        def step(i, carry):
            block(i, False)
            return carry

        lax.fori_loop(j + 1, nt, step, 0)

    key_tile = lambda w, off: pl.BlockSpec((tile, w), lambda h, j: (j, 2 * h + off))
    out_tile = pl.BlockSpec((tile, MLA_V), lambda h, j: (j, h))
    return pl.pallas_call(
        body, name=name, grid=(MLA_HEADS, nt),
        in_specs=[head(MLA_NOPE), head(MLA_ROPE), key_tile(MLA_NOPE, 0), pl.BlockSpec((1, tile, MLA_ROPE), lambda h, j: (MLA_HEADS, j, 0)),
                  key_tile(MLA_V, 1), head(1), head(1), pl.BlockSpec((t, MLA_V), lambda h, j: (0, h))],
        out_specs=[head(MLA_NOPE), head(MLA_ROPE), out_tile, out_tile, tile_of(MLA_ROPE)],
        out_shape=[jax.ShapeDtypeStruct((MLA_HEADS, t, MLA_NOPE), F32), jax.ShapeDtypeStruct((MLA_HEADS, t, MLA_ROPE), F32),
                   jax.ShapeDtypeStruct((t, MLA_HEADS * MLA_NOPE), F32), jax.ShapeDtypeStruct((t, MLA_HEADS * MLA_V), F32),
                   jax.ShapeDtypeStruct((MLA_HEADS, t, MLA_ROPE), F32)],
        compiler_params=_params(("parallel", "arbitrary")),
    )(qn, rot, kv, rot, kv, lse, delta, d_o)


def _place():
    return lax.axis_index("x"), lax.axis_index("y"), lax.axis_index("c")


def _all_gather(arrs, name):
    n = len(arrs)

    def body(*refs):
        x_refs, o_refs = refs[:n], refs[n:2 * n]
        send_sems, recv_sems, local_sems = refs[2 * n:]
        x, y, c = _place()
        me, sibling = (x, y, c), (x, y, 1 - c)
        chips = [(1 - x, y), (x, 1 - y), (1 - x, 1 - y)]

        def slot(a, p):
            return o_refs[a].at[4 * p[0] + 2 * p[1] + p[2]]

        def copy(a, k, block, to, src=None):
            return pltpu.make_async_remote_copy(
                src_ref=slot(a, block) if src is None else src, dst_ref=slot(a, block),
                send_sem=send_sems.at[a, k], recv_sem=recv_sems.at[a, k], device_id=to, device_id_type=MESH_ID)

        mine = [pltpu.make_async_copy(x_refs[a], slot(a, me), local_sems.at[a]) for a in range(n)]
        for cp in mine:
            cp.start()
        first = []
        for a in range(n):
            first.append(copy(a, 0, me, sibling, src=x_refs[a]))
            first += [copy(a, 1 + j, me, (*chip, c), src=x_refs[a]) for j, chip in enumerate(chips)]
        for cp in first:
            cp.start()
        passed = []
        for j, chip in enumerate(chips):
            for a in range(n):
                copy(a, 1 + j, (*chip, c), me).wait_recv()
                cp = copy(a, 4 + j, (*chip, c), sibling)
                cp.start()
                passed.append(cp)
        for a in range(n):
            copy(a, 0, sibling, me).wait_recv()
            for j, chip in enumerate(chips):
                copy(a, 4 + j, (*chip, 1 - c), me).wait_recv()
        for cp in first + passed:
            cp.wait_send()
        for cp in mine:
            cp.wait()

    return pl.pallas_call(
        body, name=name, in_specs=[_ANY] * n, out_specs=[_ANY] * n,
        out_shape=[jax.ShapeDtypeStruct((N_DEV,) + a.shape, a.dtype) for a in arrs],
        scratch_shapes=[pltpu.SemaphoreType.DMA((n, 7)), pltpu.SemaphoreType.DMA((n, 7)), pltpu.SemaphoreType.DMA((n,))],
    )(*arrs)


def _pass_to_sibling(arrs, name):
    n = len(arrs)

    def body(*refs):
        a_refs, o_refs = refs[:n], refs[n:2 * n]
        send_sems, recv_sems = refs[2 * n:]
        x, y, c = _place()
        chips = [(1 - x, y), (x, 1 - y), (1 - x, 1 - y)]
        slot = lambda px, py, pc: 4 * px + 2 * py + pc
        sends, recvs = [], []
        for a in range(n):
            for j, (px, py) in enumerate(chips):
                sends.append(pltpu.make_async_remote_copy(
                    src_ref=a_refs[a].at[slot(px, py, c)], dst_ref=o_refs[a].at[slot(px, py, c)], send_sem=send_sems.at[a, j],
                    recv_sem=recv_sems.at[a, j], device_id=(x, y, 1 - c), device_id_type=MESH_ID))
                recvs.append(pltpu.make_async_remote_copy(
                    src_ref=a_refs[a].at[slot(px, py, c)], dst_ref=o_refs[a].at[slot(px, py, 1 - c)], send_sem=send_sems.at[a, j],
                    recv_sem=recv_sems.at[a, j], device_id=(x, y, 1 - c), device_id_type=MESH_ID))
        for cp in sends:
            cp.start()
        for cp in sends:
            cp.wait_send()
        for cp in recvs:
            cp.wait_recv()

    return pl.pallas_call(
        body, name=name, in_specs=[_ANY] * n, out_specs=[_ANY] * n,
        out_shape=[jax.ShapeDtypeStruct(a.shape, a.dtype) for a in arrs], input_output_aliases={i: i for i in range(n)},
        scratch_shapes=[pltpu.SemaphoreType.DMA((n, 3)), pltpu.SemaphoreType.DMA((n, 3))],
    )(*arrs)


def _scatter_core(grads, name):
    n = len(grads)

    def body(*refs):
        g_refs, got_refs = refs[:n], refs[n:2 * n]
        send_sems, recv_sems = refs[2 * n:]
        x, y, c = _place()
        sends = [pltpu.make_async_remote_copy(
            src_ref=g_refs[a].at[2 * q + 1 - c], dst_ref=got_refs[a].at[q], send_sem=send_sems.at[a, q],
            recv_sem=recv_sems.at[a, q], device_id=(x, y, 1 - c), device_id_type=MESH_ID) for a in range(n) for q in range(4)]
        for cp in sends:
            cp.start()
        for cp in sends:
            cp.wait()

    return pl.pallas_call(
        body, name=name, in_specs=[_ANY] * n, out_specs=[_ANY] * n,
        out_shape=[jax.ShapeDtypeStruct((4,) + g.shape[1:], g.dtype) for g in grads],
        scratch_shapes=[pltpu.SemaphoreType.DMA((n, 4)), pltpu.SemaphoreType.DMA((n, 4))],
    )(*grads)


def _scatter_chips(parts, name):
    n = len(parts)

    def body(*refs):
        p_refs, o_refs = refs[:n], refs[n:2 * n]
        send_sems, recv_sems = refs[2 * n:]
        x, y, c = _place()
        chips = [(1 - x, y), (x, 1 - y), (1 - x, 1 - y)]
        sends = [pltpu.make_async_remote_copy(
            src_ref=p_refs[a].at[2 * px + py], dst_ref=o_refs[a].at[j], send_sem=send_sems.at[a, j],
            recv_sem=recv_sems.at[a, j], device_id=(px, py, c), device_id_type=MESH_ID)
            for a in range(n) for j, (px, py) in enumerate(chips)]
        for cp in sends:
            cp.start()
        for cp in sends:
            cp.wait()

    return pl.pallas_call(
        body, name=name, in_specs=[_ANY] * n, out_specs=[_ANY] * n,
        out_shape=[jax.ShapeDtypeStruct((3,) + p.shape[1:], p.dtype) for p in parts],
        scratch_shapes=[pltpu.SemaphoreType.DMA((n, 3)), pltpu.SemaphoreType.DMA((n, 3))],
    )(*parts)


GATHER_PEERS = [(0, 0, 1), (1, 0, 0), (0, 1, 0), (1, 1, 0)]
CHIP_PEERS = [(1, 0, 0), (0, 1, 0), (1, 1, 0)]
_HBM = pl.BlockSpec(memory_space=pltpu.HBM)
_SEM = pl.BlockSpec(memory_space=pltpu.SEMAPHORE)
_EFFECT = pltpu.SideEffectType.DATAFLOW_SIDE_EFFECTING


def _push_copies(src_refs, land_refs, send_sems, recv_sems, peers, src_of, slot_of):
    place = _place()
    flip = lambda v, f: 1 - v if f else v
    return [pltpu.make_async_remote_copy(
        src_ref=src_of(src_refs[a], k), dst_ref=land_refs[a].at[slot_of(k)], send_sem=send_sems[a], recv_sem=recv_sems[a],
        device_id=tuple(flip(v, f) for v, f in zip(place, peer)), device_id_type=MESH_ID)
        for a in range(len(src_refs)) for k, peer in enumerate(peers)]


def _push_start(srcs, land_shapes, peers, src_of, slot_of, name):
    n = len(srcs)

    def body(*refs):
        src_refs, land_refs = refs[:n], refs[n:2 * n]
        send_sems, recv_sems = refs[2 * n:3 * n], refs[3 * n:4 * n]
        token = refs[-1]
        for cp in _push_copies(src_refs, land_refs, send_sems, recv_sems, peers, src_of, slot_of):
            cp.start()
        token[...] = jnp.zeros_like(token)

    sems = [pltpu.SemaphoreType.DMA(())] * (2 * n)
    lands = [pltpu.with_memory_space_constraint(lax.empty(s.shape, s.dtype), pltpu.HBM) for s in land_shapes]
    res = pl.pallas_call(
        body, name=name, in_specs=[_HBM] * (2 * n), out_specs=[_SEM] * (2 * n) + [_HBM] * (2 * n) + [pl.BlockSpec(memory_space=pltpu.VMEM)],
        out_shape=sems + [pltpu.HBM(s.shape, s.dtype) for s in srcs] + [pltpu.HBM(s.shape, s.dtype) for s in land_shapes]
        + [jax.ShapeDtypeStruct((8, 128), F32)],
        input_output_aliases={i: 2 * n + i for i in range(2 * n)},
        compiler_params=pltpu.CompilerParams(has_side_effects=_EFFECT),
    )(*[pltpu.with_memory_space_constraint(s, pltpu.HBM) for s in srcs], *lands)
    return list(res[:n]), list(res[n:2 * n]), list(res[2 * n:3 * n]), list(res[3 * n:4 * n]), res[-1]


def _push_wait(send_sems, recv_sems, srcs, lands, after, peers, src_of, slot_of, name):
    n = len(srcs)

    def body(*refs):
        src_refs, land_refs = refs[:n], refs[n:2 * n]
        s_sems, r_sems = refs[2 * n:3 * n], refs[3 * n:4 * n]
        copies = _push_copies(src_refs, land_refs, s_sems, r_sems, peers, src_of, slot_of)
        for cp in copies:
            cp.wait_send()
        for cp in copies:
            cp.wait_recv()

    res = pl.pallas_call(
        body, name=name, in_specs=[_HBM] * (2 * n) + [_SEM] * (2 * n) + [_ANY], out_specs=[_HBM] * (2 * n),
        out_shape=[pltpu.HBM(s.shape, s.dtype) for s in srcs] + [pltpu.HBM(s.shape, s.dtype) for s in lands],
        input_output_aliases={i: i for i in range(2 * n)},
        compiler_params=pltpu.CompilerParams(has_side_effects=_EFFECT),
    )(*srcs, *lands, *send_sems, *recv_sems, after)
    return list(res[:n]), list(res[n:])


def _pick_sum(picked, rest, index, pick_of, out_dtype, name):
    nq, r, cdim = rest.shape
    one = nq == 3
    tr = _row_tile(r, 512, 16)
    tc = 512 if cdim % 512 == 0 else cdim
    grid = (1 if one else nq, r // tr, cdim // tc)

    def body(i_ref, p_ref, r_ref, o_ref):
        acc = p_ref[0].astype(F32)
        if one:
            for j in range(3):
                acc = acc + r_ref[j].astype(F32)
            o_ref[...] = acc.astype(out_dtype)
        else:
            o_ref[0] = (acc + r_ref[0].astype(F32)).astype(out_dtype)

    spec = pltpu.PrefetchScalarGridSpec(
        num_scalar_prefetch=1, grid=grid,
        in_specs=[pl.BlockSpec((1, tr, tc), lambda q, i, j, i_ref: (pick_of(q, i_ref[0]), i, j)),
                  pl.BlockSpec((3, tr, tc), lambda q, i, j, i_ref: (0, i, j)) if one else pl.BlockSpec((1, tr, tc), lambda q, i, j, i_ref: (q, i, j))],
        out_specs=pl.BlockSpec((tr, tc), lambda q, i, j, i_ref: (i, j)) if one else pl.BlockSpec((1, tr, tc), lambda q, i, j, i_ref: (q, i, j)))
    return pl.pallas_call(
        body, name=name, grid_spec=spec,
        out_shape=jax.ShapeDtypeStruct((r, cdim) if one else (nq, r, cdim), out_dtype),
        compiler_params=_params(("parallel", "parallel", "parallel")),
    )(index.astype(jnp.int32).reshape(1), picked, rest)


def _adamw_fn(w, g, m, v):
    m = ADAM_B1 * m + (1.0 - ADAM_B1) * g
    v = ADAM_B2 * v + (1.0 - ADAM_B2) * jnp.square(g)
    m_hat = m / (1.0 - ADAM_B1 ** ADAM_STEP)
    v_hat = v / (1.0 - ADAM_B2 ** ADAM_STEP)
    delta = -ADAM_LR * (m_hat / (jnp.sqrt(v_hat) + ADAM_EPS) + ADAM_WD * w)
    return delta, m, v


def _as2d(a):
    return a.reshape(-1, a.shape[-1])


def _adamw_shard(w, g, m, v, name):
    shape = w.shape
    ins = [_as2d(a) for a in (w, g, m, v)]
    cols = ins[0].shape[1]
    outs = _ew(_adamw_fn, [(a, True) for a in ins], [(cols, F32, "tile")] * 3, 256, name)
    return [o.reshape(shape) for o in outs]


def _adamw_small(ws, gs, ms, vs, name):
    shapes = [w.shape for w in ws]
    flat = lambda a: a.reshape(-1, 128) if a.size % 128 == 0 else a.reshape(1, -1)
    ins = [flat(a) for grp in zip(ws, gs, ms, vs) for a in grp]
    k = len(ws)

    def fn(*vals):
        out = []
        for i in range(k):
            out += list(_adamw_fn(*vals[4 * i:4 * i + 4]))
        return out

    outs = _whole(fn, ins, [(ins[4 * (i // 3)].shape, F32) for i in range(3 * k)], name)
    deltas = [outs[3 * i].reshape(shapes[i]) for i in range(k)]
    new_m = [outs[3 * i + 1].reshape(shapes[i]) for i in range(k)]
    new_v = [outs[3 * i + 2].reshape(shapes[i]) for i in range(k)]
    return deltas, new_m, new_v


def _sum8(stacked, name):
    def fn(a):
        s = a[0:1]
        for i in range(1, N_DEV):
            s = s + a[i:i + 1]
        return s
    w = stacked.shape[1]
    tw = 8192
    if w % tw:
        return _whole(fn, [stacked], [((1, w), F32)], name)[0]

    def body(a_ref, o_ref):
        o_ref[...] = fn(a_ref[...])

    return pl.pallas_call(body, name=name, grid=(w // tw,), in_specs=[pl.BlockSpec((N_DEV, tw), lambda i: (0, i))],
                          out_specs=pl.BlockSpec((1, tw), lambda i: (0, i)), out_shape=jax.ShapeDtypeStruct((1, w), F32))(stacked)


ROWS = 512


def _split_heads(p, nh):
    t = p.shape[0]
    return p.reshape(t, nh, p.shape[1] // nh).transpose(1, 0, 2)


def _merge_heads(p):
    nh, t, d = p.shape
    return p.transpose(1, 0, 2).reshape(t, nh * d)


def _layer_fwd(h, mod, w, small, rope, l, late=None):
    sh1, sc1, gt1, sh2, sc2, gt2 = mod
    rope_ret, rope_mla = rope
    t = h.shape[0]
    nm = lambda s: f"l{l}_{s}"
    a1 = _ew(_norm_mod_fn, [(h, True), (small["norm1_g"], False), (sc1, False), (sh1, False)], [(D_MODEL, BF16, "tile")], ROWS, nm("norm1"))[0]
    p_s5 = _mm(a1, w["w_in_t"], "nt", 512, 512, 2048, name=nm("proj_s5"), n=512)
    p_ret = _mm(a1, w["w_in_t"], "nt", 512, 512, 2048, name=nm("proj_ret"), b_off=1, n=1536)
    p_swa = _mm(a1, w["w_in_t"], "nt", 512, 256, 2048, name=nm("proj_swa"), b_off=8, n=768)
    p_mla = _mm(a1, w["w_in_t"][2816:], "nt", 512, 576, 2048, name=nm("proj_mla"))
    b3, c3, dskip, a_r, a_i = small["s5"]
    bu = _mm_blocks(p_s5, b3, "nn", 512, a_of=lambda j: j % S5_BLOCKS, name=nm("s5_bu"))
    st_r, st_i = _s5_scan_fwd(bu, a_r, a_i, nm("s5_scan"))
    ypre = _mm_blocks(st_r, c3[:S5_BLOCKS], "nn", 512, pair=(st_i, c3[S5_BLOCKS:]), name=nm("s5_y"))
    z = _ew(_s5_act_fn, [(ypre, True), (p_s5, True), (dskip, False)], [(GROUP_WIDTH, F32, "tile")], ROWS, nm("s5_act"))[0]
    zz = _mm(z, w["glu_w"], "nn", 512, 512, 512, name=nm("s5_zz"))
    y_s5 = _ew(_s5_glu_fn, [(z, True), (zz, True), (small["s5_glu_b"], False)], [(GROUP_WIDTH, BF16, "tile")], ROWS, nm("s5_glu"))[0]
    qk_ret = _split_heads(_rope(p_ret[:, :2 * RET_HEADS * RET_QK], rope_ret, nm("ret_rope")), 2 * RET_HEADS)
    o_ret, y_ret = _ret_fwd(qk_ret, p_ret, small["ret_lgam"], nm("ret"))
    qkv_swa = _split_heads(p_swa, 12)
    o_swa, lse_swa = _swa_fwd(qkv_swa, small["swa_sinks"], nm("swa"))
    y_swa = _merge_heads(o_swa).astype(BF16)
    cq, ckv, kr = p_mla[:, :MLA_Q_RANK], p_mla[:, MLA_Q_RANK:MLA_Q_RANK + MLA_KV_RANK], p_mla[:, MLA_Q_RANK + MLA_KV_RANK:]
    cqn = _ew(_rms_gain_fn, [(cq, True), (small["mla_q_norm"], False)], [(MLA_Q_RANK, BF16, "tile")], ROWS, nm("mla_qnorm"))[0]
    ckvn = _ew(_rms_gain_fn, [(ckv, True), (small["mla_kv_norm"], False)], [(MLA_KV_RANK, BF16, "tile")], ROWS, nm("mla_kvnorm"))[0]
    q_full = _mm(cqn, w["w_uq_t"], "nt", 512, 768, 384, name=nm("mla_q"))
    kv_full = _mm(ckvn, w["w_ukv_t"], "nt", 512, 1024, 128, BF16, name=nm("mla_kv"))
    nq = q_full.shape[1]
    roped = _rope(jnp.concatenate([q_full, kr, jnp.zeros_like(kr)], axis=1), rope_mla, nm("mla_rope"), out_dtype=BF16)
    q4 = roped[:, :nq].reshape(t, MLA_HEADS, MLA_NOPE + MLA_ROPE)
    qn = q4[:, :, :MLA_NOPE].transpose(1, 0, 2)
    rot = jnp.concatenate([q4[:, :, MLA_NOPE:].transpose(1, 0, 2), roped[None, :, nq:nq + MLA_ROPE]], axis=0)
    o_mla, lse_mla = _mla_attend(qn, rot, kv_full, nm("mla"))
    cat = jnp.concatenate([y_s5, y_ret, y_swa, o_mla.astype(BF16)], axis=1)
    if late is not None:
        w = {**w, **late(lse_mla)}
    mixed = _mm(cat, w["w_out"], "nn", 512, 1024, 2048, name=nm("out_proj"))
    h1 = _ew(_gate_add_fn, [(h, True), (mixed, True), (gt1, False)], [(D_MODEL, F32, "tile")], ROWS, nm("res1"))[0]
    a2 = _ew(_norm_mod_fn, [(h1, True), (small["norm2_g"], False), (sc2, False), (sh2, False)], [(D_MODEL, BF16, "tile")], ROWS, nm("norm2"))[0]
    hid, act = _mm(a2, w["w1_t"], "nt", 1024, 1024, 2048, name=nm("mlp1"), epi=lambda acc: (acc, _relu2_fn(acc)), epi_outs=[F32, BF16])
    mo = _mm(act, w["w2"], "nn", 1024, 1024, 2048, name=nm("mlp2"))
    h2 = _ew(_gate_add_fn, [(h1, True), (mo, True), (gt2, False)], [(D_MODEL, F32, "tile")], ROWS, nm("res2"))[0]
    saved = dict(w=w, h=h, a1=a1, p_s5=p_s5, p_ret=p_ret, st_r=st_r, st_i=st_i, ypre=ypre, z=z, zz=zz, qk_ret=qk_ret, o_ret=o_ret,
                 qkv_swa=qkv_swa, o_swa=o_swa, lse_swa=lse_swa, cq=cq, ckv=ckv, cqn=cqn, ckvn=ckvn, qn=qn, rot=rot,
                 kv_full=kv_full, o_mla=o_mla, lse_mla=lse_mla, cat=cat, mixed=mixed, h1=h1, a2=a2, hid=hid, act=act, mo=mo)
    return h2, saved


def _layer_bwd(dh2, mod, w, small, rope, s, l, after_mlp=None):
    sh1, sc1, gt1, sh2, sc2, gt2 = mod
    rope_ret, rope_mla = rope
    t = dh2.shape[0]
    nm = lambda n: f"l{l}_{n}_bwd"
    gb, gs = {}, {}
    row = (D_MODEL, F32, "acc")
    dmo, dgt2 = _ew(lambda d, y, gt: (d * gt, jnp.sum(d * y, axis=0, keepdims=True)),
                    [(dh2, True), (s["mo"], True), (gt2, False)], [(D_MODEL, BF16, "tile"), row], ROWS, nm("res2"))
    dhid = _mm(dmo, w["w2"], "nt", 1024, 1024, 2048, name=nm("mlp2_x"), epi=lambda acc, x: (acc * 2.0 * jnp.maximum(x, 0.0),),
               epi_ins=[s["hid"]], epi_outs=[BF16])[0]
    gb["w2"] = _mm(s["act"], dmo, "tn", 1024, 1024, 4096, BF16, name=nm("mlp2_w"))
    da2 = _mm(dhid, w["w1_t"], "nn", 1024, 1024, 2048, name=nm("mlp1_x"))
    gb["w1_t"] = _mm(dhid, s["a2"], "tn", 1024, 1024, 4096, BF16, name=nm("mlp1_w"))
    if after_mlp is not None:
        gt1 = gt1 + after_mlp(gb)

    def norm_bwd(hh, g, sc, sh, da, dres):
        dh_, dg, dsc, dsh = _vjp_block(_norm_mod_fn, 4)(hh, g, sc, sh, da)
        return dh_ + dres, dg, dsc, dsh

    dh1, gs["norm2_g"], dsc2, dsh2 = _ew(norm_bwd, [(s["h1"], True), (small["norm2_g"], False), (sc2, False), (sh2, False), (da2, True), (dh2, True)],
                                         [(D_MODEL, F32, "tile"), row, row, row], ROWS, nm("norm2"))
    dmixed, dgt1 = _ew(lambda d, y, gt: (d * gt, jnp.sum(d * y, axis=0, keepdims=True)),
                       [(dh1, True), (s["mixed"], True), (gt1, False)], [(D_MODEL, BF16, "tile"), row], ROWS, nm("res1"))
    dcat = _mm(dmixed, w["w_out"], "nt", 512, 1024, 2048, name=nm("out_proj_x"))
    gb["w_out"] = _mm(s["cat"], dmixed, "tn", 1024, 1024, 4096, BF16, name=nm("out_proj_w"))
    dy_s5, dy_ret, dy_swa, dy_mla = (dcat[:, i * GROUP_WIDTH:(i + 1) * GROUP_WIDTH] for i in range(4))
    b3, c3, dskip, a_r, a_i = small["s5"]
    gw = (GROUP_WIDTH, F32, "tile")
    gacc = (GROUP_WIDTH, F32, "acc")
    dz_a, dzz, gs["s5_glu_b"] = _ew(_vjp_block(_s5_glu_fn, 3), [(s["z"], True), (s["zz"], True), (small["s5_glu_b"], False), (dy_s5, True)],
                                    [gw, gw, gacc], ROWS, nm("s5_glu"))
    dz_b = _mm(dzz, w["glu_w"], "nt", 512, 512, 512, name=nm("s5_zz_x"))
    gb["glu_w"] = _mm(s["z"], dzz, "tn", 512, 512, 1024, BF16, name=nm("s5_zz_w"))

    def act_bwd(ypre, u, dsk, dza, dzb):
        return _vjp_block(_s5_act_fn, 3)(ypre, u, dsk, dza + dzb)

    dypre, du_a, g_dskip = _ew(act_bwd, [(s["ypre"], True), (s["p_s5"], True), (dskip, False), (dz_a, True), (dz_b, True)],
                               [gw, gw, gacc], ROWS, nm("s5_act"))
    ch, st = S5_PACK * S5_CH, S5_PACK * S5_STATE
    dst = _mm_blocks(dypre, c3, "nt", 512, a_of=lambda j: j % S5_BLOCKS, name=nm("s5_y_x"))
    g_c3 = jnp.concatenate([_mm_blocks_tn(s["st_r"], dypre, st, ch, lambda j: j, nm("s5_y_w_re")),
                            _mm_blocks_tn(s["st_i"], dypre, st, ch, lambda j: j, nm("s5_y_w_im"))], axis=0)
    dbu_r, dbu_i, g_ar, g_ai = _s5_scan_bwd(dst, s["st_r"], s["st_i"], a_r, a_i, nm("s5_scan"))
    du_b = _mm_blocks(dbu_r, b3[:S5_BLOCKS], "nt", 512, pair=(dbu_i, b3[S5_BLOCKS:]), name=nm("s5_bu_x"))
    g_b3 = jnp.concatenate([_mm_blocks_tn(s["p_s5"], dbu_r, ch, st, lambda j: j, nm("s5_bu_w_re")),
                            _mm_blocks_tn(s["p_s5"], dbu_i, ch, st, lambda j: j, nm("s5_bu_w_im"))], axis=0)
    gs["s5"] = (g_b3, g_c3, g_dskip, g_ar, g_ai)
    dqk_rot, dk_rot, dv_ret, dg_ret = _ret_bwd(s["qk_ret"], s["p_ret"], s["o_ret"], dy_ret, small["ret_lgam"], nm("ret"))
    dqk = _rope(_merge_heads(jnp.concatenate([dqk_rot, dk_rot], axis=0)), rope_ret, nm("ret_rope"), inverse=True)
    dqkv_swa, gs["swa_sinks"] = _swa_bwd(s["qkv_swa"], s["o_swa"], s["lse_swa"], _split_heads(dy_swa, SWA_HEADS), small["swa_sinks"], nm("swa"))
    delta = _mla_delta(s["o_mla"], dy_mla, nm("mla_delta"))
    dqn, dqr, dkn, dv_mla, dkr_heads = _mla_attend_bwd(s["qn"], s["rot"], s["kv_full"], s["lse_mla"], delta, dy_mla, nm("mla_att"))
    dkv_full = jnp.stack([dkn.reshape(t, MLA_HEADS, MLA_NOPE), dv_mla.reshape(t, MLA_HEADS, MLA_V)], axis=2).reshape(t, 2 * MLA_HEADS * MLA_NOPE)
    dkr_rot = _ew(lambda a, b, c, d: a + b + c + d, [(dkr_heads[i], True) for i in range(MLA_HEADS)], [(MLA_ROPE, F32, "tile")], ROWS, nm("mla_dkr"))[0]
    nq = MLA_HEADS * (MLA_NOPE + MLA_ROPE)
    dq_rot = jnp.concatenate([dqn.transpose(1, 0, 2), dqr.transpose(1, 0, 2)], axis=2).reshape(t, nq)
    droped = _rope(jnp.concatenate([dq_rot, dkr_rot, jnp.zeros_like(dkr_rot)], axis=1), rope_mla, nm("mla_rope"), inverse=True)
    dq_full, dkr = droped[:, :nq], droped[:, nq:nq + MLA_ROPE]
    dcqn = _mm(dq_full, w["w_uq_t"], "nn", 512, 384, 768, name=nm("mla_q_x"))
    gb["w_uq_t"] = _mm(dq_full, s["cqn"], "tn", 768, 384, 1024, BF16, name=nm("mla_q_w"))
    dckvn = _mm(dkv_full, w["w_ukv_t"], "nn", 512, 128, 1024, name=nm("mla_kv_x"))
    gb["w_ukv_t"] = _mm(dkv_full, s["ckvn"], "tn", 1024, 128, 1024, BF16, name=nm("mla_kv_w"))
    dcq, gs["mla_q_norm"] = _ew(_vjp_block(_rms_gain_fn, 2), [(s["cq"], True), (small["mla_q_norm"], False), (dcqn, True)],
                                [(MLA_Q_RANK, F32, "tile"), (MLA_Q_RANK, F32, "acc")], ROWS, nm("mla_qnorm"))
    dckv, gs["mla_kv_norm"] = _ew(_vjp_block(_rms_gain_fn, 2), [(s["ckv"], True), (small["mla_kv_norm"], False), (dckvn, True)],
                                  [(MLA_KV_RANK, F32, "tile"), (MLA_KV_RANK, F32, "acc")], ROWS, nm("mla_kvnorm"))
    du = _ew(lambda a, b: a + b, [(du_a, True), (du_b, True)], [(GROUP_WIDTH, BF16, "tile")], ROWS, nm("s5_du"))[0]
    bf = lambda a: a.astype(BF16)
    dproj = jnp.concatenate([du, bf(dqk), bf(dv_ret), bf(dg_ret), bf(_merge_heads(dqkv_swa)), bf(dcq), bf(dckv), bf(dkr)], axis=1)
    da1 = _mm(dproj, w["w_in_t"], "nn", 512, 1024, N_IN, name=nm("proj_x"))
    gb["w_in_t"] = _mm(dproj, s["a1"], "tn", N_IN, 512, 2048, BF16, name=nm("proj_w"))
    dh, gs["norm1_g"], dsc1, dsh1 = _ew(norm_bwd, [(s["h"], True), (small["norm1_g"], False), (sc1, False), (sh1, False), (da1, True), (dh1, True)],
                                        [(D_MODEL, F32, "tile"), row, row, row], ROWS, nm("norm1"))
    dmod = jnp.concatenate([dsh1, dsc1, dgt1, dsh2, dsc2, dgt2], axis=1)
    return dh, gb, gs, dmod


BIG = ("w_in_t", "w1_t", "w_uq_t", "w_ukv_t", "w_out", "w2", "glu_w")
MLP_BIG = ("w1_t", "w2")
LATE_BIG = ("w_out", "w1_t", "w2")
S5_NAMES = ("s5_lambda_re", "s5_lambda_im", "s5_log_dt", "s5_b_re", "s5_b_im", "s5_c_re", "s5_c_im", "s5_d")


def kernel(x, c, norm1_g, norm2_g, ada_w, ada_b, w_in, s5_lambda_re, s5_lambda_im, s5_log_dt, s5_b_re, s5_b_im, s5_c_re, s5_c_im, s5_d, s5_glu_w, s5_glu_b, swa_sinks, mla_q_norm, mla_kv_norm, mla_w_uq, mla_w_ukv, w_out, mlp_w1, mlp_w2, final_norm_g, loss_target, m_norm1_g, m_norm2_g, m_ada_w, m_ada_b, m_w_in, m_s5_lambda_re, m_s5_lambda_im, m_s5_log_dt, m_s5_b_re, m_s5_b_im, m_s5_c_re, m_s5_c_im, m_s5_d, m_s5_glu_w, m_s5_glu_b, m_swa_sinks, m_mla_q_norm, m_mla_kv_norm, m_mla_w_uq, m_mla_w_ukv, m_w_out, m_mlp_w1, m_mlp_w2, m_final_norm_g, v_norm1_g, v_norm2_g, v_ada_w, v_ada_b, v_w_in, v_s5_lambda_re, v_s5_lambda_im, v_s5_log_dt, v_s5_b_re, v_s5_b_im, v_s5_c_re, v_s5_c_im, v_s5_d, v_s5_glu_w, v_s5_glu_b, v_swa_sinks, v_mla_q_norm, v_mla_kv_norm, v_mla_w_uq, v_mla_w_ukv, v_w_out, v_mlp_w1, v_mlp_w2, v_final_norm_g):
    names = ["norm1_g", "norm2_g", "ada_w", "ada_b", "w_in", "s5_lambda_re", "s5_lambda_im", "s5_log_dt", "s5_b_re", "s5_b_im",
             "s5_c_re", "s5_c_im", "s5_d", "s5_glu_w", "s5_glu_b", "swa_sinks", "mla_q_norm", "mla_kv_norm", "mla_w_uq",
             "mla_w_ukv", "w_out", "mlp_w1", "mlp_w2", "final_norm_g"]
    env = locals()
    wts = {n: env[n] for n in names}
    mom = {n: env["m_" + n] for n in names}
    var = {n: env["v_" + n] for n in names}
    t = x.shape[1]
    me = 4 * lax.axis_index("x") + 2 * lax.axis_index("y") + lax.axis_index("c")
    rope = _rope_tables(t)
    ret_lgam = jnp.log1p(-(2.0 ** (-5.0 - jnp.arange(RET_HEADS, dtype=F32))))

    tr = lambda a: a.transpose(0, 2, 1)
    shard = {"w_in_t": tr(w_in), "w1_t": tr(mlp_w1), "w_uq_t": tr(mla_w_uq), "w_ukv_t": tr(mla_w_ukv),
             "w_out": w_out, "w2": mlp_w2, "glu_w": s5_glu_w}
    to_send = [{k: shard[k][l].astype(BF16) for k in BIG} for l in range(DEPTH)]
    as_rows = lambda keys, arrs: {k: a.reshape(-1, shard[k].shape[2]) for k, a in zip(keys, arrs)}
    first = [k for k in BIG if k not in LATE_BIG]
    gathered = _all_gather([to_send[0][k] for k in first] + [c], "gather_weights_first")
    c_all = gathered[-1].reshape(N_DEV, D_MODEL)
    big = [as_rows(first, gathered[:len(first)]), None]
    own_slot = lambda k: 4 * lax.axis_index("x") + 2 * lax.axis_index("y") + lax.axis_index("c")

    def gather_start(arrs, tag):
        return _push_start(arrs, [jax.ShapeDtypeStruct((N_DEV,) + a.shape, a.dtype) for a in arrs], GATHER_PEERS,
                           lambda ref, k: ref, own_slot, f"gather_weights_{tag}_start")

    def gather_finish(started, after, tag):
        sent, landed = _push_wait(started[0], started[1], started[2], started[3], after, GATHER_PEERS, lambda ref, k: ref, own_slot,
                                  f"gather_weights_{tag}_wait")
        with_own = [lax.dynamic_update_index_in_dim(full, own, me, 0) for full, own in zip(landed, sent)]
        return _pass_to_sibling(with_own, f"gather_weights_{tag}_pass")

    gather0 = gather_start([to_send[0][k] for k in LATE_BIG], "l0")
    gather1 = gather_start([to_send[1][k] for k in BIG], "l1")

    c_act = _whole(lambda v: v * jax.nn.sigmoid(v), [c_all], [((N_DEV, D_MODEL), F32)], "cond_silu")[0]
    c_pad = jnp.concatenate([c_act, jnp.zeros((128 - N_DEV, D_MODEL), F32)], axis=0)
    cols = ada_w.shape[2]
    mod_part = [_mm(c_pad, ada_w[l], "nn", 128, cols, 512, name=f"l{l}_mod")[:N_DEV] for l in range(DEPTH)]
    mod_all = _all_gather([jnp.stack(mod_part)], "gather_mod")[0]
    mod_rows = lax.dynamic_index_in_dim(mod_all, me, axis=2, keepdims=False)
    mods = []
    for l in range(DEPTH):
        row = mod_rows[:, l].reshape(1, 6 * D_MODEL) + ada_b[l][None]
        if l == 0:
            row = row + (gather0[4][0, 0] + gather1[4][0, 0])
        mods.append([row[:, i * D_MODEL:(i + 1) * D_MODEL] for i in range(6)])

    smalls, s5_pulls = [], []
    for l in range(DEPTH):
        s5_ops, pull = jax.vjp(_s5_prep, *[wts[n][l] for n in S5_NAMES])
        s5_pulls.append(pull)
        smalls.append(dict(norm1_g=norm1_g[l][None], norm2_g=norm2_g[l][None], s5=s5_ops, s5_glu_b=s5_glu_b[l][None],
                           swa_sinks=swa_sinks[l], mla_q_norm=mla_q_norm[l][None], mla_kv_norm=mla_kv_norm[l][None], ret_lgam=ret_lgam))
    h = x[0]
    saved = []
    for l in range(DEPTH):
        if l == 0:
            late = lambda after: as_rows(LATE_BIG, gather_finish(gather0, after, "l0"))
        else:
            big[1] = as_rows(BIG, gather_finish(gather1, h, "l1"))
            late = None
        h, s = _layer_fwd(h, mods[l], big[l], smalls[l], rope, l, late)
        big[l] = s.pop("w")
        saved.append(s)

    fg = final_norm_g[None]
    tgt = loss_target[0]
    loss_local = _ew(_final_fn, [(h, True), (fg, False), (tgt, True)], [(1, F32, "acc")], ROWS, "loss")[0]

    def final_bwd(hh, g, tg):
        dh_, dg, _ = _vjp_block(_final_fn, 3)(hh, g, tg, jnp.ones((1, 1), F32))
        return dh_, dg

    dh, g_final = _ew(final_bwd, [(h, True), (fg, False), (tgt, True)], [(D_MODEL, F32, "tile"), (D_MODEL, F32, "acc")], ROWS, "loss_bwd")
    loss = lax.psum(loss_local[0, 0], ("x", "y", "c"))

    core, chip = lax.axis_index("c"), 2 * lax.axis_index("x") + lax.axis_index("y")

    def core_stage(g_layer, keys, tag):
        g_list = [g_layer[k].reshape(N_DEV, -1, g_layer[k].shape[1]) for k in keys]
        got = _scatter_core(g_list, f"scatter_core_{tag}")
        return [_pick_sum(g, o, core, lambda q, c_: 2 * q + c_, BF16, f"{tag}_core_sum_{k}") for k, g, o in zip(keys, g_list, got)]

    def their_block(ref, k):
        x_, y_ = lax.axis_index("x"), lax.axis_index("y")
        dx, dy, _ = CHIP_PEERS[k]
        return ref.at[2 * (1 - x_ if dx else x_) + (1 - y_ if dy else y_)]

    def chips_start(halves, tag):
        return _push_start(halves, [jax.ShapeDtypeStruct((3,) + a.shape[1:], a.dtype) for a in halves], CHIP_PEERS,
                           their_block, lambda k: k, f"scatter_chips_{tag}_start")

    def chips_wait(started, after, tag):
        return _push_wait(started[0], started[1], started[2], started[3], after, CHIP_PEERS, their_block, lambda k: k, f"scatter_chips_{tag}_wait")

    g_small, dmods = [None] * DEPTH, [None] * DEPTH
    dh, g_big1, g_small[1], dmods[1] = _layer_bwd(dh, mods[1], big[1], smalls[1], rope, saved[1], 1)
    chips1 = chips_start(core_stage(g_big1, BIG, "l1"), "l1")
    mods0 = [m + chips1[4][0, 0] for m in mods[0]]
    early = {}

    def after_mlp(gb):
        early["mlp"] = chips_start(core_stage(gb, MLP_BIG, "l0_mlp"), "l0_mlp")
        return early["mlp"][4][0, 0]

    dh, g_big0, g_small[0], dmods[0] = _layer_bwd(dh, mods0, big[0], smalls[0], rope, saved[0], 0, after_mlp=after_mlp)
    grad_x = dh[None]
    rest = [k for k in BIG if k not in MLP_BIG]
    halves_rest = core_stage(g_big0, rest, "l0_rest")
    halves1, landed1 = chips_wait(chips1, dh, "l1")
    halves_mlp, landed_mlp = chips_wait(early["mlp"], dh, "l0_mlp")
    landed_rest = _scatter_chips(halves_rest, "scatter_chips_l0_rest")
    terms = {(1, k): pair for k, pair in zip(BIG, zip(halves1, landed1))}
    terms.update({(0, k): pair for k, pair in zip(MLP_BIG, zip(halves_mlp, landed_mlp))})
    terms.update({(0, k): pair for k, pair in zip(rest, zip(halves_rest, landed_rest))})
    g_shard = [_pick_sum(*terms[l, k], chip, lambda q, m_: m_, F32, f"l{l}_chip_sum_{k}") for l in range(DEPTH) for k in BIG]

    small_parts = []
    for l in range(DEPTH):
        gs = g_small[l]
        s5g = s5_pulls[l](gs["s5"])
        small_parts += [gs["norm1_g"], gs["norm2_g"], *s5g, gs["s5_glu_b"], gs["swa_sinks"], gs["mla_q_norm"], gs["mla_kv_norm"]]
    small_parts += [g_final, *dmods]
    sizes = [int(np.prod(p.shape)) for p in small_parts]
    flat = jnp.concatenate([p.reshape(1, -1) for p in small_parts], axis=1)
    pad = (-flat.shape[1]) % 8192
    flat = jnp.pad(flat, ((0, 0), (0, pad)))
    flat_all = _all_gather([flat], "gather_small_grads")[0].reshape(N_DEV, -1)
    summed = _sum8(flat_all, "sum_small_grads")
    pieces, off = [], 0
    for sz in sizes:
        pieces.append(summed[0, off:off + sz])
        off += sz
    small_names = ["norm1_g", "norm2_g", *S5_NAMES, "s5_glu_b", "swa_sinks", "mla_q_norm", "mla_kv_norm"]
    per_layer = len(small_names)
    grads = {}
    for i, n in enumerate(small_names):
        grads[n] = jnp.stack([pieces[l * per_layer + i].reshape(wts[n].shape[1:]) for l in range(DEPTH)])
    grads["final_norm_g"] = pieces[DEPTH * per_layer]
    grads["ada_b"] = jnp.stack([pieces[DEPTH * per_layer + 1 + l] for l in range(DEPTH)])

    mod_off = sum(sizes[:DEPTH * per_layer + 1])
    dmod_all = flat_all[:, mod_off:mod_off + DEPTH * 6 * D_MODEL].reshape(N_DEV, DEPTH, N_DEV, cols)
    dmod_mine = lax.dynamic_index_in_dim(dmod_all, me, axis=2, keepdims=False).transpose(1, 0, 2)
    dmod_pad = jnp.concatenate([dmod_mine, jnp.zeros((DEPTH, 128 - N_DEV, cols), F32)], axis=1)
    grads["ada_w"] = jnp.stack([_mm(c_pad, dmod_pad[l], "tn", 512, cols, 128, name=f"l{l}_ada_w_grad") for l in range(DEPTH)])

    out_g, out_d, out_m, out_v = dict(grads), {}, {}, {}
    orig = {"w_in_t": "w_in", "w1_t": "mlp_w1", "w_uq_t": "mla_w_uq", "w_ukv_t": "mla_w_ukv", "w_out": "w_out", "w2": "mlp_w2", "glu_w": "s5_glu_w"}
    for i, k in enumerate(BIG):
        out_g[orig[k]] = jnp.stack([g_shard[l * len(BIG) + i] for l in range(DEPTH)])
        if k.endswith("_t"):
            out_g[orig[k]] = tr(out_g[orig[k]])
    for n in [*orig.values(), "ada_w"]:
        out_d[n], out_m[n], out_v[n] = _adamw_shard(wts[n], out_g[n], mom[n], var[n], f"adamw_{n}")
    small_all = small_names + ["ada_b", "final_norm_g"]
    ds, ms, vs = _adamw_small([wts[n] for n in small_all], [grads[n] for n in small_all], [mom[n] for n in small_all],
                              [var[n] for n in small_all], "adamw_small")
    for n, d, m_, v_ in zip(small_all, ds, ms, vs):
        out_d[n], out_m[n], out_v[n] = d, m_, v_
    return (loss, grad_x, *[out_g[n] for n in names], *[out_d[n] for n in names], *[out_m[n] for n in names], *[out_v[n] for n in names])
```

```python
import functools
import math

import numpy as np
import jax
import jax.numpy as jnp
from jax import lax
from jax.experimental import pallas as pl
from jax.experimental.pallas import tpu as pltpu

F32 = jnp.float32
BF16 = jnp.bfloat16
_MXU_DTYPE = jnp.bfloat16

N_DEV = 8
D_MODEL = 2048
DEPTH = 2
GROUP_WIDTH = 512
D_FF = 8192
S5_CH, S5_GROUPS, S5_STATE = 16, 32, 64
S5_WIDTH = S5_GROUPS * S5_STATE
S5_PACK = 8
S5_BLOCKS = S5_GROUPS // S5_PACK
RET_HEADS, RET_QK, RET_V, RET_CHUNK = 4, 64, 128, 128
SWA_HD, SWA_HEADS, SWA_KV_HEADS, WINDOW = 64, 8, 2, 128
MLA_HEADS, MLA_Q_RANK, MLA_KV_RANK, MLA_NOPE, MLA_ROPE, MLA_V = 4, 384, 128, 128, 64, 128
ROPE_BASE = 10000.0
EPS = 1e-6
NEG = -1e30
N_IN = 3392
ADAM_LR, ADAM_B1, ADAM_B2, ADAM_EPS, ADAM_WD, ADAM_STEP = 0.001, 0.9, 0.999, 1e-08, 0.01, 10

VMEM_LIMIT_BYTES = 52 * 1024 * 1024
MESH_ID = pl.DeviceIdType.MESH
_ANY = pl.BlockSpec(memory_space=pl.ANY)
_SMEM = pl.BlockSpec(memory_space=pltpu.SMEM)


def _params(sem):
    return pltpu.CompilerParams(dimension_semantics=sem, vmem_limit_bytes=VMEM_LIMIT_BYTES)


_DIMS = {"nn": (((1,), (0,)), ((), ())), "nt": (((1,), (1,)), ((), ())), "tn": (((0,), (0,)), ((), ()))}


def _dot(a, b, mode="nn"):
    return lax.dot_general(a.astype(_MXU_DTYPE), b.astype(_MXU_DTYPE), _DIMS[mode], preferred_element_type=F32)


def _mm(a, b, mode, tm, tn, tk, out_dtype=F32, name="mm", b_off=0, n=None, pair=None, epi=None, epi_ins=(), epi_outs=None):
    if mode == "tn":
        kdim, m = a.shape
    else:
        m, kdim = a.shape
    if n is None:
        n = b.shape[0] if mode == "nt" else b.shape[1]
    tm, tn, tk = min(tm, m), min(tn, n), min(tk, kdim)
    assert m % tm == 0 and n % tn == 0 and kdim % tk == 0, (name, a.shape, b.shape, tm, tn, tk)
    nk = kdim // tk
    a_spec = pl.BlockSpec((tk, tm), lambda i, j, k: (k, i)) if mode == "tn" else pl.BlockSpec((tm, tk), lambda i, j, k: (i, k))
    if mode == "nt":
        b_spec = pl.BlockSpec((tn, tk), lambda i, j, k: (j + b_off, k))
    else:
        b_spec = pl.BlockSpec((tk, tn), lambda i, j, k: (k, j + b_off))
    o_spec = pl.BlockSpec((tm, tn), lambda i, j, k: (i, j))
    n_mm = 2 if pair is None else 4
    out_dtypes = [out_dtype] if epi is None else list(epi_outs)

    def body(*refs):
        ins, extra = refs[:n_mm], refs[n_mm:n_mm + len(epi_ins)]
        outs = refs[n_mm + len(epi_ins):n_mm + len(epi_ins) + len(out_dtypes)]
        part = _dot(ins[0][...], ins[1][...], mode)
        if pair is not None:
            part = part + _dot(ins[2][...], ins[3][...], mode)

        def finish(acc):
            vals = (acc,) if epi is None else epi(acc, *[r[...] for r in extra])
            for o_ref, v, dt in zip(outs, vals, out_dtypes):
                o_ref[...] = v.astype(dt)

        if nk == 1:
            finish(part)
        else:
            acc_ref = refs[-1]
            k = pl.program_id(2)

            @pl.when(k == 0)
            def _():
                acc_ref[...] = part

            @pl.when(k > 0)
            def _():
                acc_ref[...] += part

            @pl.when(k == nk - 1)
            def _():
                finish(acc_ref[...])

    operands = [a, b] + ([] if pair is None else list(pair)) + list(epi_ins)
    res = pl.pallas_call(
        body, name=name, grid=(m // tm, n // tn, nk),
        in_specs=[a_spec, b_spec] * (n_mm // 2) + [o_spec] * len(epi_ins),
        out_specs=[o_spec] * len(out_dtypes), out_shape=[jax.ShapeDtypeStruct((m, n), dt) for dt in out_dtypes],
        scratch_shapes=[] if nk == 1 else [pltpu.VMEM((tm, tn), F32)],
        compiler_params=_params(("parallel", "parallel", "arbitrary")),
    )(*operands)
    return res[0] if epi is None else res


def _mm_blocks(a, b, mode, tm, a_of=None, pair=None, name="mm_blocks"):
    a_of = a_of or (lambda j: j)
    m = a.shape[0]
    nj, kb, nb = b.shape
    a_w, o_w = (kb, nb) if mode == "nn" else (nb, kb)
    tm = min(tm, m)
    a_spec = pl.BlockSpec((tm, a_w), lambda i, j: (i, a_of(j)))
    b_spec = pl.BlockSpec((1, kb, nb), lambda i, j: (j, 0, 0))
    n_in = 2 if pair is None else 4

    def body(*refs):
        acc = _dot(refs[0][...], refs[1][0], mode)
        if pair is not None:
            acc = acc + _dot(refs[2][...], refs[3][0], mode)
        refs[n_in][...] = acc

    operands = [a, b] + ([] if pair is None else list(pair))
    return pl.pallas_call(
        body, name=name, grid=(m // tm, nj), in_specs=[a_spec, b_spec] * (n_in // 2),
        out_specs=pl.BlockSpec((tm, o_w), lambda i, j: (i, j)), out_shape=jax.ShapeDtypeStruct((m, nj * o_w), F32),
        compiler_params=_params(("parallel", "parallel")),
    )(*operands)


def _mm_blocks_tn(a, b, x, y, b_of, name):
    kdim = a.shape[0]
    nj = a.shape[1] // x

    def body(a_ref, b_ref, o_ref):
        o_ref[0] = _dot(a_ref[...], b_ref[...], "tn")

    return pl.pallas_call(
        body, name=name, grid=(nj,), in_specs=[pl.BlockSpec((kdim, x), lambda j: (0, j)), pl.BlockSpec((kdim, y), lambda j: (0, b_of(j)))],
        out_specs=pl.BlockSpec((1, x, y), lambda j: (j, 0, 0)), out_shape=jax.ShapeDtypeStruct((nj, x, y), F32),
        compiler_params=_params(("parallel",)),
    )(a, b)


SUBLANES = 8


def _row_tile(rows, target, mult=SUBLANES):
    best = None
    for cand in range(mult, min(rows, target) + 1, mult):
        if rows % cand == 0:
            best = cand
    return best or rows


def _ew(fn, ins, outs, tt, name):
    t = [a.shape[0] for a, tiled in ins if tiled][0]
    tt = _row_tile(t, tt)
    n_in = len(ins)
    in_specs = [pl.BlockSpec((tt, a.shape[1]), lambda i: (i, 0)) if tiled else pl.BlockSpec(a.shape, lambda i: (0, 0))
                for a, tiled in ins]
    out_specs, out_shapes = [], []
    for w, dt, kind in outs:
        if kind == "tile":
            out_specs.append(pl.BlockSpec((tt, w), lambda i: (i, 0)))
            out_shapes.append(jax.ShapeDtypeStruct((t, w), dt))
        else:
            out_specs.append(pl.BlockSpec((1, w), lambda i: (0, 0)))
            out_shapes.append(jax.ShapeDtypeStruct((1, w), F32))
    has_acc = any(kind == "acc" for _, _, kind in outs)

    def body(*refs):
        vals = fn(*[r[...] for r in refs[:n_in]])
        if not isinstance(vals, (tuple, list)):
            vals = (vals,)
        i = pl.program_id(0)
        for o_ref, v, (w, dt, kind) in zip(refs[n_in:], vals, outs):
            if kind == "tile":
                o_ref[...] = v.astype(dt)
            else:
                @pl.when(i == 0)
                def _(o_ref=o_ref, v=v):
                    o_ref[...] = v.astype(F32)

                @pl.when(i > 0)
                def _(o_ref=o_ref, v=v):
                    o_ref[...] += v.astype(F32)

    res = pl.pallas_call(
        body, name=name, grid=(t // tt,), in_specs=in_specs, out_specs=out_specs, out_shape=out_shapes,
        compiler_params=_params(("arbitrary" if has_acc else "parallel",)),
    )(*[a for a, _ in ins])
    return res


def _whole(fn, ins, outs, name):
    def body(*refs):
        vals = fn(*[r[...] for r in refs[:len(ins)]])
        if not isinstance(vals, (tuple, list)):
            vals = (vals,)
        for o_ref, v in zip(refs[len(ins):], vals):
            o_ref[...] = v.astype(o_ref.dtype)

    return pl.pallas_call(body, name=name, out_shape=[jax.ShapeDtypeStruct(s, dt) for s, dt in outs])(*ins)


def _rms(x):
    return x * lax.rsqrt(jnp.mean(x * x, axis=-1, keepdims=True) + EPS)


def _norm_mod_fn(h, g, sc, sh):
    return (_rms(h) * g) * (1.0 + sc) + sh


def _rms_gain_fn(x, g):
    return _rms(x) * g


def _gate_add_fn(h, y, gt):
    return h + gt * y


def _relu2_fn(x):
    return jnp.square(jnp.maximum(x, 0.0))


def _s5_act_fn(ypre, u, dskip):
    return jax.nn.gelu(ypre + dskip * u)


def _s5_glu_fn(z, zz, b):
    return z * jax.nn.sigmoid(zz + b)


def _ret_gate_fn(o, g):
    return _rms(o) * (g * jax.nn.sigmoid(g))


def _final_fn(h, g, tgt):
    err = _rms(h) * g - tgt
    return 0.5 * jnp.sum(jnp.mean(err * err, axis=-1, keepdims=True), axis=0, keepdims=True)


def _vjp_block(fn, n_args):
    def bwd(*vals):
        _, pull = jax.vjp(fn, *vals[:n_args])
        return pull(vals[n_args])
    return bwd


def _rope_tables(t):
    d = RET_QK
    inv = ROPE_BASE ** (-jnp.arange(0, d, 2, dtype=F32) / d)
    ang = jnp.arange(t, dtype=F32)[:, None] * inv[None, :]
    cos, sin = jnp.cos(ang), jnp.sin(ang)
    cos2, sin2 = jnp.concatenate([cos, cos], -1), jnp.concatenate([-sin, sin], -1)
    ret = (jnp.tile(cos2, (1, 8)), jnp.tile(sin2, (1, 8)))
    one, zero = jnp.ones((t, MLA_NOPE), F32), jnp.zeros((t, MLA_NOPE), F32)
    mla_c = jnp.concatenate([jnp.tile(jnp.concatenate([one, cos2], -1), (1, MLA_HEADS)), cos2, one[:, :d]], -1)
    mla_s = jnp.concatenate([jnp.tile(jnp.concatenate([zero, sin2], -1), (1, MLA_HEADS)), sin2, zero[:, :d]], -1)
    return ret, (mla_c, mla_s)


def _rope_fn(x, c, s, sign):
    w = x.shape[1]
    lane = lax.broadcasted_iota(jnp.int32, x.shape, 1)
    swapped = jnp.where((lane & 63) < 32, pltpu.roll(x, w - 32, 1), pltpu.roll(x, 32, 1))
    return x * c + swapped * (sign * s)


def _rope(x, tables, name, inverse=False, out_dtype=F32):
    c, s = tables
    fn = functools.partial(_rope_fn, sign=-1.0 if inverse else 1.0)
    return _ew(fn, [(x, True), (c, True), (s, True)], [(x.shape[1], out_dtype, "tile")], ROWS, name)[0]


SCAN_ROWS, SCAN_LANES = 256, 512


def _cmul(ar, ai, br, bi):
    return ar * br - ai * bi, ar * bi + ai * br


def _group_powers(ar, ai, reverse):
    shape = (SUBLANES, ar.shape[1])
    row = lax.broadcasted_iota(jnp.int32, shape, 0)
    pr, pi = ar, ai
    out_r, out_i = jnp.zeros(shape, F32), jnp.zeros(shape, F32)
    for e in range(1, SUBLANES + 1):
        hit = row == (SUBLANES - e if reverse else e - 1)
        out_r, out_i = jnp.where(hit, pr, out_r), jnp.where(hit, pi, out_i)
        if e < SUBLANES:
            pr, pi = _cmul(pr, pi, ar, ai)
    return out_r, out_i


def _scan_chunk(in_r_ref, in_i_ref, out_r_ref, out_i_ref, ar, ai, cr, ci, reverse, visit=None):
    rows, lanes = in_r_ref.shape
    sub = lax.broadcasted_iota(jnp.int32, (SUBLANES, lanes), 0)
    edge_r, edge_i = _group_powers(ar, ai, reverse)
    steps, pr, pi, k = [], ar, ai, 1
    while k < SUBLANES:
        steps.append((k, pr, pi))
        pr, pi = _cmul(pr, pi, pr, pi)
        k *= 2
    groups = range(rows // SUBLANES)
    for g in (reversed(groups) if reverse else groups):
        sl = slice(g * SUBLANES, (g + 1) * SUBLANES)
        xr, xi = in_r_ref[sl, :], in_i_ref[sl, :]
        for k, pr, pi in steps:
            shift = SUBLANES - k if reverse else k
            keep = sub < SUBLANES - k if reverse else sub >= k
            tr, ti = _cmul(pr, pi, pltpu.roll(xr, shift, 0), pltpu.roll(xi, shift, 0))
            xr, xi = xr + jnp.where(keep, tr, 0.0), xi + jnp.where(keep, ti, 0.0)
        tr, ti = _cmul(edge_r, edge_i, cr, ci)
        xr, xi = xr + tr, xi + ti
        out_r_ref[sl, :] = xr
        out_i_ref[sl, :] = xi
        if visit is not None:
            visit(sl, xr, xi, cr, ci)
        edge = slice(0, 1) if reverse else slice(SUBLANES - 1, SUBLANES)
        cr, ci = xr[edge, :], xi[edge, :]
    return cr, ci


def _s5_scan_specs(rows, row_block):
    assert SCAN_LANES == S5_PACK * S5_STATE
    chan = pl.BlockSpec((rows, S5_PACK * S5_CH), lambda j, i: (row_block(i), j))
    op = lambda off, shape: pl.BlockSpec((1,) + shape, lambda j, i: (j + off, 0, 0))
    blk = pl.BlockSpec((rows, SCAN_LANES), lambda j, i: (row_block(i), j))
    par = pl.BlockSpec((1, SCAN_LANES), lambda j, i: (0, j))
    return chan, op, blk, par


def _s5_scan_fwd(u, b3, a_r, a_i, name):
    t = u.shape[0]
    rows = min(SCAN_ROWS, t)
    chan, op, blk, par = _s5_scan_specs(rows, lambda i: i)

    def body(u_ref, b_re_ref, b_im_ref, ar_ref, ai_ref, or_ref, oi_ref, cr_ref, ci_ref, sr_ref, si_ref):
        i = pl.program_id(1)

        @pl.when(i == 0)
        def _():
            cr_ref[...] = jnp.zeros_like(cr_ref)
            ci_ref[...] = jnp.zeros_like(ci_ref)

        sr_ref[...] = _dot(u_ref[...], b_re_ref[0])
        si_ref[...] = _dot(u_ref[...], b_im_ref[0])
        ar, ai = ar_ref[...], ai_ref[...]
        cr, ci = _scan_chunk(sr_ref, si_ref, or_ref, oi_ref, ar, ai, cr_ref[...], ci_ref[...], reverse=False)
        cr_ref[...] = cr
        ci_ref[...] = ci

    st_r, st_i = pl.pallas_call(
        body, name=name, grid=(S5_BLOCKS, t // rows),
        in_specs=[chan, op(0, b3.shape[1:]), op(S5_BLOCKS, b3.shape[1:]), par, par], out_specs=[blk, blk],
        out_shape=[jax.ShapeDtypeStruct((t, S5_WIDTH), F32)] * 2,
        scratch_shapes=[pltpu.VMEM((1, SCAN_LANES), F32)] * 2 + [pltpu.VMEM((rows, SCAN_LANES), F32)] * 2,
        compiler_params=_params(("parallel", "arbitrary")),
    )(u, b3, b3, a_r, a_i)
    return st_r, st_i


def _s5_scan_bwd(dy, c3, st_r, st_i, a_r, a_i, name):
    t = dy.shape[0]
    rows = min(SCAN_ROWS, t)
    nc = t // rows
    chan, op, blk, par = _s5_scan_specs(rows, lambda i: nc - 1 - i)

    def body(dy_ref, c_re_ref, c_im_ref, xr_ref, xi_ref, ar_ref, ai_ref, gr_ref, gi_ref, dar_ref, dai_ref, cr_ref, ci_ref, dr_ref, di_ref):
        i = pl.program_id(1)
        dr_ref[...] = _dot(dy_ref[...], c_re_ref[0], "nt")
        di_ref[...] = _dot(dy_ref[...], c_im_ref[0], "nt")

        @pl.when(i == 0)
        def _():
            cr_ref[...] = jnp.zeros_like(cr_ref)
            ci_ref[...] = jnp.zeros_like(ci_ref)
            dar_ref[...] = jnp.zeros_like(dar_ref)
            dai_ref[...] = jnp.zeros_like(dai_ref)

        ar, ai = ar_ref[...], ai_ref[...]
        cr, ci = cr_ref[...], ci_ref[...]
        last = lax.broadcasted_iota(jnp.int32, (SUBLANES, SCAN_LANES), 0) == SUBLANES - 1
        sums = [jnp.zeros((SUBLANES, SCAN_LANES), F32), jnp.zeros((SUBLANES, SCAN_LANES), F32)]

        def visit(sl, gr, gi, next_r, next_i):
            nr = jnp.where(last, next_r, pltpu.roll(gr, SUBLANES - 1, 0))
            ni = jnp.where(last, next_i, pltpu.roll(gi, SUBLANES - 1, 0))
            xr, xi = xr_ref[sl, :], xi_ref[sl, :]
            sums[0] = sums[0] + (nr * xr + ni * xi)
            sums[1] = sums[1] + (ni * xr - nr * xi)

        first_r, first_i = _scan_chunk(dr_ref, di_ref, gr_ref, gi_ref, ar, -ai, cr, ci, reverse=True, visit=visit)
        dar_ref[...] += jnp.sum(sums[0], axis=0, keepdims=True)
        dai_ref[...] += jnp.sum(sums[1], axis=0, keepdims=True)
        cr_ref[...] = first_r
        ci_ref[...] = first_i

    return pl.pallas_call(
        body, name=name, grid=(S5_BLOCKS, nc),
        in_specs=[chan, op(0, c3.shape[1:]), op(S5_BLOCKS, c3.shape[1:]), blk, blk, par, par], out_specs=[blk, blk, par, par],
        out_shape=[jax.ShapeDtypeStruct((t, S5_WIDTH), F32)] * 2 + [jax.ShapeDtypeStruct((1, S5_WIDTH), F32)] * 2,
        scratch_shapes=[pltpu.VMEM((1, SCAN_LANES), F32)] * 2 + [pltpu.VMEM((rows, SCAN_LANES), F32)] * 2,
        compiler_params=_params(("parallel", "arbitrary")),
    )(dy, c3, c3, st_r, st_i, a_r, a_i)


def _s5_prep(lam_re, lam_im, log_dt, b_re, b_im, c_re, c_im, d_skip):
    dt = jnp.exp(log_dt)[:, None]
    mag = jnp.exp(lam_re * dt)
    ar, ai = mag * jnp.cos(lam_im * dt), mag * jnp.sin(lam_im * dt)
    den = lam_re * lam_re + lam_im * lam_im
    cr = ((ar - 1.0) * lam_re + ai * lam_im) / den
    ci = (ai * lam_re - (ar - 1.0) * lam_im) / den
    bbar_r = cr[..., None] * b_re - ci[..., None] * b_im
    bbar_i = cr[..., None] * b_im + ci[..., None] * b_re
    eye = jnp.eye(S5_PACK, dtype=F32)

    def bdiag(m):
        g, a, b = m.shape
        m4 = m.reshape(g // S5_PACK, S5_PACK, a, b)
        return (eye[None, :, None, :, None] * m4[:, :, :, None, :]).reshape(g // S5_PACK, S5_PACK * a, S5_PACK * b)

    b3 = jnp.concatenate([bdiag(bbar_r.transpose(0, 2, 1)), bdiag(bbar_i.transpose(0, 2, 1))], axis=0)
    c3 = jnp.concatenate([bdiag(c_re.transpose(0, 2, 1)), -bdiag(c_im.transpose(0, 2, 1))], axis=0)
    return b3, c3, d_skip.reshape(1, GROUP_WIDTH), ar.reshape(1, S5_WIDTH), ai.reshape(1, S5_WIDTH)


def _ret_consts(lgam):
    c = RET_CHUNK
    r = lax.broadcasted_iota(jnp.int32, (c, c), 0)
    m = lax.broadcasted_iota(jnp.int32, (c, c), 1)
    rel = (r - m).astype(F32)
    decay = jnp.where(rel >= 0, jnp.exp(lgam * jnp.maximum(rel, 0.0)), 0.0)
    idx = lax.broadcasted_iota(jnp.int32, (c, 1), 0).astype(F32)
    zeta = jnp.exp(lgam * (c - 1.0 - idx))
    xi = jnp.exp(lgam * (idx + 1.0))
    return decay, zeta, xi, jnp.exp(lgam * c)


def _ret_specs(t):
    qk = lambda off: pl.BlockSpec((1, t, RET_QK), lambda h: (h + off, 0, 0))
    col = lambda off: pl.BlockSpec((t, RET_V), lambda h: (0, h + off))
    return qk, col


def _ret_fwd(qk, p_ret, lgam, name):
    t = qk.shape[1]
    nck = t // RET_CHUNK
    qk_spec, col = _ret_specs(t)

    def body(lg_ref, q_ref, k_ref, v_ref, g_ref, o_ref, y_ref):
        decay, zeta, xi, gam = _ret_consts(lg_ref[pl.program_id(0)])

        def step(n, state):
            sl = pl.ds(pl.multiple_of(n * RET_CHUNK, RET_CHUNK), RET_CHUNK)
            q, k, v = q_ref[0, sl, :], k_ref[0, sl, :] * (RET_QK ** -0.5), v_ref[sl, :]
            s = _dot(q, k, "nt") * decay
            o = _dot(s, v) + _dot(q, state) * xi
            o_ref[sl, :] = o
            y_ref[sl, :] = _ret_gate_fn(o, g_ref[sl, :]).astype(y_ref.dtype)
            return gam * state + _dot(k, zeta * v, "tn")

        lax.fori_loop(0, nck, step, jnp.zeros((RET_QK, RET_V), F32))

    return pl.pallas_call(
        body, name=name, grid=(RET_HEADS,), in_specs=[_SMEM, qk_spec(0), qk_spec(RET_HEADS), col(4), col(8)],
        out_specs=[col(0), col(0)],
        out_shape=[jax.ShapeDtypeStruct((t, GROUP_WIDTH), F32), jax.ShapeDtypeStruct((t, GROUP_WIDTH), BF16)],
        compiler_params=_params(("parallel",)),
    )(lgam, qk, qk, p_ret, p_ret)


def _ret_bwd(qk, p_ret, o_all, dy, lgam, name):
    t = qk.shape[1]
    nck = t // RET_CHUNK
    qk_spec, col = _ret_specs(t)
    gate_bwd = _vjp_block(_ret_gate_fn, 2)

    def body(lg_ref, q_ref, k_ref, v_ref, g_ref, o_ref, dy_ref, dq_ref, dk_ref, dv_ref, dg_ref, st_ref):
        decay, zeta, xi, gam = _ret_consts(lg_ref[pl.program_id(0)])
        scale = RET_QK ** -0.5

        def fstep(n, state):
            sl = pl.ds(pl.multiple_of(n * RET_CHUNK, RET_CHUNK), RET_CHUNK)
            st_ref[n] = state
            return gam * state + _dot(k_ref[0, sl, :] * scale, zeta * v_ref[sl, :], "tn")

        lax.fori_loop(0, nck, fstep, jnp.zeros((RET_QK, RET_V), F32))

        def bstep(r, grad_state):
            n = nck - 1 - r
            sl = pl.ds(pl.multiple_of(n * RET_CHUNK, RET_CHUNK), RET_CHUNK)
            q, k, v = q_ref[0, sl, :], k_ref[0, sl, :] * scale, v_ref[sl, :]
            d_o, dg = gate_bwd(o_ref[sl, :], g_ref[sl, :], dy_ref[sl, :])
            dg_ref[sl, :] = dg.astype(dg_ref.dtype)
            s = _dot(q, k, "nt") * decay
            ds = _dot(d_o, v, "nt") * decay
            xdo = xi * d_o
            dq_ref[0, sl, :] = _dot(ds, k) + _dot(xdo, st_ref[n], "nt")
            dk_ref[0, sl, :] = (_dot(ds, q, "tn") + _dot(zeta * v, grad_state, "nt")) * scale
            dv_ref[sl, :] = (_dot(s, d_o, "tn") + zeta * _dot(k, grad_state)).astype(dv_ref.dtype)
            return gam * grad_state + _dot(q, xdo, "tn")

        lax.fori_loop(0, nck, bstep, jnp.zeros((RET_QK, RET_V), F32))

    hd = pl.BlockSpec((1, t, RET_QK), lambda h: (h, 0, 0))
    return pl.pallas_call(
        body, name=name, grid=(RET_HEADS,),
        in_specs=[_SMEM, qk_spec(0), qk_spec(RET_HEADS), col(4), col(8), col(0), col(0)],
        out_specs=[hd, hd, col(0), col(0)],
        out_shape=[jax.ShapeDtypeStruct((RET_HEADS, t, RET_QK), F32)] * 2 + [jax.ShapeDtypeStruct((t, GROUP_WIDTH), BF16)] * 2,
        scratch_shapes=[pltpu.VMEM((nck, RET_QK, RET_V), F32)], compiler_params=_params(("parallel",)),
    )(lgam, qk, qk, p_ret, p_ret, o_all, dy)


SWA_GROUP = SWA_HEADS // SWA_KV_HEADS
SWA_SCALE = SWA_HD ** -0.5


def _swa_mask(n):
    rows = SWA_GROUP * WINDOW
    r = lax.broadcasted_iota(jnp.int32, (rows, 2 * WINDOW), 0) & (WINDOW - 1)
    j = lax.broadcasted_iota(jnp.int32, (rows, 2 * WINDOW), 1)
    dist = r + WINDOW - j
    return (dist >= 0) & (dist < WINDOW) & (n * WINDOW + j - WINDOW >= 0)


def _swa_sink_rows(sink_ref, kv):
    row = lax.broadcasted_iota(jnp.int32, (SWA_GROUP * WINDOW, 1), 0)
    sink = jnp.zeros((SWA_GROUP * WINDOW, 1), F32)
    for g in range(SWA_GROUP):
        sink = jnp.where(row >= g * WINDOW, sink_ref[kv * SWA_GROUP + g], sink)
    return sink


def _swa_pad_keys(n, k_ref, v_ref, kp_ref, vp_ref):
    @pl.when(n == 0)
    def _():
        zero = jnp.zeros((WINDOW, SWA_HD), F32)
        kp_ref[0:WINDOW, :] = zero
        vp_ref[0:WINDOW, :] = zero
        kp_ref[WINDOW:, :] = k_ref[0]
        vp_ref[WINDOW:, :] = v_ref[0]


def _swa_specs(t):
    blk = lambda w: pl.BlockSpec((SWA_GROUP, WINDOW, w), lambda kv, n: (kv, n, 0))
    kspec = lambda off: pl.BlockSpec((1, t, SWA_HD), lambda kv, n: (SWA_HEADS + off + kv, 0, 0))
    return blk, kspec


def _swa_fwd(qkv, sinks, name):
    t = qkv.shape[1]
    rows = SWA_GROUP * WINDOW
    blk, kspec = _swa_specs(t)

    def body(sink_ref, q_ref, k_ref, v_ref, o_ref, lse_ref, kp_ref, vp_ref):
        kv, n = pl.program_id(0), pl.program_id(1)
        _swa_pad_keys(n, k_ref, v_ref, kp_ref, vp_ref)
        win = pl.ds(pl.multiple_of(n * WINDOW, WINDOW), 2 * WINDOW)
        sink = _swa_sink_rows(sink_ref, kv)
        s = _dot(q_ref[...].reshape(rows, SWA_HD), kp_ref[win, :], "nt") * SWA_SCALE
        s = jnp.where(_swa_mask(n), s, NEG)
        m = jnp.maximum(jnp.max(s, axis=-1, keepdims=True), sink)
        p = jnp.exp(s - m)
        den = jnp.sum(p, axis=-1, keepdims=True) + jnp.exp(sink - m)
        o_ref[...] = _dot(p / den, vp_ref[win, :]).reshape(SWA_GROUP, WINDOW, SWA_HD)
        lse_ref[...] = (m + jnp.log(den)).reshape(SWA_GROUP, WINDOW, 1)

    return pl.pallas_call(
        body, name=name, grid=(SWA_KV_HEADS, t // WINDOW),
        in_specs=[_SMEM, blk(SWA_HD), kspec(0), kspec(SWA_KV_HEADS)], out_specs=[blk(SWA_HD), blk(1)],
        out_shape=[jax.ShapeDtypeStruct((SWA_HEADS, t, SWA_HD), F32), jax.ShapeDtypeStruct((SWA_HEADS, t, 1), F32)],
        scratch_shapes=[pltpu.VMEM((t + WINDOW, SWA_HD), F32)] * 2, compiler_params=_params(("parallel", "arbitrary")),
    )(sinks, qkv, qkv, qkv)


def _swa_bwd(qkv, o, lse, d_o, sinks, name):
    t = qkv.shape[1]
    nb = t // WINDOW
    rows = SWA_GROUP * WINDOW
    blk, kspec = _swa_specs(t)

    def body(sink_ref, q_ref, k_ref, v_ref, o_ref, lse_ref, do_ref, dq_ref, dk_ref, dv_ref, dsink_ref,
             kp_ref, vp_ref, dkp_ref, dvp_ref):
        kv, n = pl.program_id(0), pl.program_id(1)
        _swa_pad_keys(n, k_ref, v_ref, kp_ref, vp_ref)

        @pl.when(n == 0)
        def _():
            dkp_ref[...] = jnp.zeros_like(dkp_ref)
            dvp_ref[...] = jnp.zeros_like(dvp_ref)
            dsink_ref[...] = jnp.zeros_like(dsink_ref)

        win = pl.ds(pl.multiple_of(n * WINDOW, WINDOW), 2 * WINDOW)
        sink = _swa_sink_rows(sink_ref, kv)
        q, dout = q_ref[...].reshape(rows, SWA_HD), do_ref[...].reshape(rows, SWA_HD)
        lse_n = lse_ref[...].reshape(rows, 1)
        s = _dot(q, kp_ref[win, :], "nt") * SWA_SCALE
        s = jnp.where(_swa_mask(n), s, NEG)
        p = jnp.exp(s - lse_n)
        delta = jnp.sum(dout * o_ref[...].reshape(rows, SWA_HD), axis=-1, keepdims=True)
        ds = p * (_dot(dout, vp_ref[win, :], "nt") - delta)
        dq_ref[...] = (_dot(ds, kp_ref[win, :]) * SWA_SCALE).reshape(SWA_GROUP, WINDOW, SWA_HD).astype(dq_ref.dtype)
        dkp_ref[win, :] += _dot(ds, q, "tn") * SWA_SCALE
        dvp_ref[win, :] += _dot(p, dout, "tn")
        term = jnp.exp(sink - lse_n) * delta
        head = lax.broadcasted_iota(jnp.int32, (SWA_GROUP, 128), 0)
        acc = jnp.zeros((SWA_GROUP, 128), F32)
        for g in range(SWA_GROUP):
            acc = jnp.where(head == g, jnp.sum(term[g * WINDOW:(g + 1) * WINDOW], axis=0, keepdims=True), acc)
        dsink_ref[0] -= acc

        @pl.when(n == nb - 1)
        def _():
            dk_ref[0] = dkp_ref[WINDOW:, :].astype(dk_ref.dtype)
            dv_ref[0] = dvp_ref[WINDOW:, :].astype(dv_ref.dtype)

    kout = pl.BlockSpec((1, t, SWA_HD), lambda kv, n: (kv, 0, 0))
    dq, dk, dv, dsink = pl.pallas_call(
        body, name=name, grid=(SWA_KV_HEADS, nb),
        in_specs=[_SMEM, blk(SWA_HD), kspec(0), kspec(SWA_KV_HEADS), blk(SWA_HD), blk(1), blk(SWA_HD)],
        out_specs=[blk(SWA_HD), kout, kout, pl.BlockSpec((1, SWA_GROUP, 128), lambda kv, n: (kv, 0, 0))],
        out_shape=[jax.ShapeDtypeStruct((SWA_HEADS, t, SWA_HD), BF16), jax.ShapeDtypeStruct((SWA_KV_HEADS, t, SWA_HD), BF16),
                   jax.ShapeDtypeStruct((SWA_KV_HEADS, t, SWA_HD), BF16), jax.ShapeDtypeStruct((SWA_KV_HEADS, SWA_GROUP, 128), F32)],
        scratch_shapes=[pltpu.VMEM((t + WINDOW, SWA_HD), F32)] * 4, compiler_params=_params(("parallel", "arbitrary")),
    )(sinks, qkv, qkv, qkv, o, lse, d_o)
    return jnp.concatenate([dq, dk, dv], axis=0), dsink[:, :, 0].reshape(SWA_HEADS)


MLA_SCALE = (MLA_NOPE + MLA_ROPE) ** -0.5
MLA_TILE = 512
MLA_KEY_TILE = 512
MLA_BWD_TILE = 512


def _mla_diag(s):
    r = lax.broadcasted_iota(jnp.int32, s.shape, 0)
    c = lax.broadcasted_iota(jnp.int32, s.shape, 1)
    return jnp.where(c <= r, s, NEG)


def _mla_specs(t, tile):
    whole = lambda w, off: pl.BlockSpec((t, w), lambda h, i: (0, 2 * h + off))
    head = lambda w: pl.BlockSpec((1, t, w), lambda h, i: (h, 0, 0))
    key_rope = pl.BlockSpec((1, t, MLA_ROPE), lambda h, i: (MLA_HEADS, 0, 0))
    tile_of = lambda w: pl.BlockSpec((1, tile, w), lambda h, i: (h, i, 0))
    return whole, head, key_rope, tile_of


def _mla_attend(qn, rot, kv, name):
    t = qn.shape[1]
    tile = min(MLA_TILE, t)
    ktile = min(MLA_KEY_TILE, t)
    ratio = ktile // tile
    whole, head, key_rope, tile_of = _mla_specs(t, tile)

    def body(qn_ref, qr_ref, kn_ref, kr_ref, v_ref, o_ref, lse_ref, m_ref, l_ref, acc_ref):
        i = pl.program_id(1)
        qn_b, qr_b = qn_ref[0], qr_ref[0]

        def rows(j):
            return pl.ds(pl.multiple_of(j * ktile, ktile), ktile)

        def scores(j):
            return (_dot(qn_b, kn_ref[rows(j), :], "nt") + _dot(qr_b, kr_ref[0, rows(j), :], "nt")) * MLA_SCALE

        def causal(s, j):
            qpos = i * tile + lax.broadcasted_iota(jnp.int32, s.shape, 0)
            kpos = j * ktile + lax.broadcasted_iota(jnp.int32, s.shape, 1)
            return jnp.where(kpos <= qpos, s, NEG)

        def update(s, j):
            m_old = m_ref[...]
            m_new = jnp.maximum(m_old, jnp.max(s, axis=-1, keepdims=True))
            alpha = jnp.exp(m_old - m_new)
            p = jnp.exp(s - m_new)
            l_ref[...] = alpha * l_ref[...] + jnp.sum(p, axis=-1, keepdims=True)
            acc_ref[...] = alpha * acc_ref[...] + _dot(p, v_ref[rows(j), :])
            m_ref[...] = m_new

        m_ref[...] = jnp.full_like(m_ref, NEG)
        l_ref[...] = jnp.zeros_like(l_ref)
        acc_ref[...] = jnp.zeros_like(acc_ref)

        def step(j, s_cur):
            s_next = scores(j + 1)
            update(s_cur, j)
            return s_next

        last = i // ratio
        s_last = lax.fori_loop(0, last, step, scores(0))
        update(causal(s_last, last), last)
        o_ref[...] = acc_ref[...] / l_ref[...]
        lse_ref[0] = m_ref[...] + jnp.log(l_ref[...])

    return pl.pallas_call(
        body, name=name, grid=(MLA_HEADS, t // tile),
        in_specs=[tile_of(MLA_NOPE), tile_of(MLA_ROPE), whole(MLA_NOPE, 0), key_rope, whole(MLA_V, 1)],
        out_specs=[pl.BlockSpec((tile, MLA_V), lambda h, i: (i, h)), tile_of(1)],
        out_shape=[jax.ShapeDtypeStruct((t, GROUP_WIDTH), F32), jax.ShapeDtypeStruct((MLA_HEADS, t, 1), F32)],
        scratch_shapes=[pltpu.VMEM((tile, 1), F32), pltpu.VMEM((tile, 1), F32), pltpu.VMEM((tile, MLA_V), F32)],
        compiler_params=_params(("parallel", "parallel")),
    )(qn, rot, kv, rot, kv)


def _mla_delta(o, d_o, name):
    t = o.shape[0]
    tt = min(ROWS, t)

    def body(o_ref, do_ref, d_ref):
        d_ref[0] = jnp.sum(o_ref[...] * do_ref[...], axis=-1, keepdims=True)

    blk = pl.BlockSpec((tt, MLA_V), lambda h, i: (i, h))
    return pl.pallas_call(
        body, name=name, grid=(MLA_HEADS, t // tt), in_specs=[blk, blk],
        out_specs=pl.BlockSpec((1, tt, 1), lambda h, i: (h, i, 0)), out_shape=jax.ShapeDtypeStruct((MLA_HEADS, t, 1), F32),
        compiler_params=_params(("parallel", "parallel")),
    )(o, d_o)


def _mla_attend_bwd(qn, rot, kv, lse, delta, d_o, name):
    t = qn.shape[1]
    tile = min(MLA_BWD_TILE, t)
    nt = t // tile
    whole, head, key_rope, tile_of = _mla_specs(t, tile)

    def body(qn_ref, qr_ref, kn_ref, kr_ref, v_ref, lse_ref, dl_ref, do_ref, dqn_ref, dqr_ref, dkn_ref, dv_ref, dkr_ref):
        j = pl.program_id(1)

        @pl.when(j == 0)
        def _():
            dqn_ref[...] = jnp.zeros_like(dqn_ref)
            dqr_ref[...] = jnp.zeros_like(dqr_ref)

        dkn_ref[...] = jnp.zeros_like(dkn_ref)
        dv_ref[...] = jnp.zeros_like(dv_ref)
        dkr_ref[...] = jnp.zeros_like(dkr_ref)
        kn_b, kr_b, v_b = kn_ref[...], kr_ref[0], v_ref[...]

        def block(i, diagonal):
            sl = pl.ds(pl.multiple_of(i * tile, tile), tile)
            qn_b, qr_b, dout = qn_ref[0, sl, :], qr_ref[0, sl, :], do_ref[sl, :]
            s = (_dot(qn_b, kn_b, "nt") + _dot(qr_b, kr_b, "nt")) * MLA_SCALE
            if diagonal:
                s = _mla_diag(s)
            p = jnp.exp(s - lse_ref[0, sl, :])
            ds = p * (_dot(dout, v_b, "nt") - dl_ref[0, sl, :]) * MLA_SCALE
            dv_ref[...] += _dot(p, dout, "tn")
            dkn_ref[...] += _dot(ds, qn_b, "tn")
            dkr_ref[0] += _dot(ds, qr_b, "tn")
            dqn_ref[0, sl, :] += _dot(ds, kn_b)
            dqr_ref[0, sl, :] += _dot(ds, kr_b)

        block(j, True)

        def step(i, carry):
            block(i, False)
            return carry

        lax.fori_loop(j + 1, nt, step, 0)

    key_tile = lambda w, off: pl.BlockSpec((tile, w), lambda h, j: (j, 2 * h + off))
    out_tile = pl.BlockSpec((tile, MLA_V), lambda h, j: (j, h))
    return pl.pallas_call(
        body, name=name, grid=(MLA_HEADS, nt),
        in_specs=[head(MLA_NOPE), head(MLA_ROPE), key_tile(MLA_NOPE, 0), pl.BlockSpec((1, tile, MLA_ROPE), lambda h, j: (MLA_HEADS, j, 0)),
                  key_tile(MLA_V, 1), head(1), head(1), pl.BlockSpec((t, MLA_V), lambda h, j: (0, h))],
        out_specs=[head(MLA_NOPE), head(MLA_ROPE), out_tile, out_tile, tile_of(MLA_ROPE)],
        out_shape=[jax.ShapeDtypeStruct((MLA_HEADS, t, MLA_NOPE), F32), jax.ShapeDtypeStruct((MLA_HEADS, t, MLA_ROPE), F32),
                   jax.ShapeDtypeStruct((t, MLA_HEADS * MLA_NOPE), F32), jax.ShapeDtypeStruct((t, MLA_HEADS * MLA_V), F32),
                   jax.ShapeDtypeStruct((MLA_HEADS, t, MLA_ROPE), F32)],
        compiler_params=_params(("parallel", "arbitrary")),
    )(qn, rot, kv, rot, kv, lse, delta, d_o)


def _place():
    return lax.axis_index("x"), lax.axis_index("y"), lax.axis_index("c")


def _all_gather(arrs, name):
    n = len(arrs)

    def body(*refs):
        x_refs, o_refs = refs[:n], refs[n:2 * n]
        send_sems, recv_sems, local_sems = refs[2 * n:]
        x, y, c = _place()
        me, sibling = (x, y, c), (x, y, 1 - c)
        chips = [(1 - x, y), (x, 1 - y), (1 - x, 1 - y)]

        def slot(a, p):
            return o_refs[a].at[4 * p[0] + 2 * p[1] + p[2]]

        def copy(a, k, block, to, src=None):
            return pltpu.make_async_remote_copy(
                src_ref=slot(a, block) if src is None else src, dst_ref=slot(a, block),
                send_sem=send_sems.at[a, k], recv_sem=recv_sems.at[a, k], device_id=to, device_id_type=MESH_ID)

        mine = [pltpu.make_async_copy(x_refs[a], slot(a, me), local_sems.at[a]) for a in range(n)]
        for cp in mine:
            cp.start()
        first = []
        for a in range(n):
            first.append(copy(a, 0, me, sibling, src=x_refs[a]))
            first += [copy(a, 1 + j, me, (*chip, c), src=x_refs[a]) for j, chip in enumerate(chips)]
        for cp in first:
            cp.start()
        passed = []
        for j, chip in enumerate(chips):
            for a in range(n):
                copy(a, 1 + j, (*chip, c), me).wait_recv()
                cp = copy(a, 4 + j, (*chip, c), sibling)
                cp.start()
                passed.append(cp)
        for a in range(n):
            copy(a, 0, sibling, me).wait_recv()
            for j, chip in enumerate(chips):
                copy(a, 4 + j, (*chip, 1 - c), me).wait_recv()
        for cp in first + passed:
            cp.wait_send()
        for cp in mine:
            cp.wait()

    return pl.pallas_call(
        body, name=name, in_specs=[_ANY] * n, out_specs=[_ANY] * n,
        out_shape=[jax.ShapeDtypeStruct((N_DEV,) + a.shape, a.dtype) for a in arrs],
        scratch_shapes=[pltpu.SemaphoreType.DMA((n, 7)), pltpu.SemaphoreType.DMA((n, 7)), pltpu.SemaphoreType.DMA((n,))],
    )(*arrs)


def _pass_to_sibling(arrs, name):
    n = len(arrs)

    def body(*refs):
        a_refs, o_refs = refs[:n], refs[n:2 * n]
        send_sems, recv_sems = refs[2 * n:]
        x, y, c = _place()
        chips = [(1 - x, y), (x, 1 - y), (1 - x, 1 - y)]
        slot = lambda px, py, pc: 4 * px + 2 * py + pc
        sends, recvs = [], []
        for a in range(n):
            for j, (px, py) in enumerate(chips):
                sends.append(pltpu.make_async_remote_copy(
                    src_ref=a_refs[a].at[slot(px, py, c)], dst_ref=o_refs[a].at[slot(px, py, c)], send_sem=send_sems.at[a, j],
                    recv_sem=recv_sems.at[a, j], device_id=(x, y, 1 - c), device_id_type=MESH_ID))
                recvs.append(pltpu.make_async_remote_copy(
                    src_ref=a_refs[a].at[slot(px, py, c)], dst_ref=o_refs[a].at[slot(px, py, 1 - c)], send_sem=send_sems.at[a, j],
                    recv_sem=recv_sems.at[a, j], device_id=(x, y, 1 - c), device_id_type=MESH_ID))
        for cp in sends:
            cp.start()
        for cp in sends:
            cp.wait_send()
        for cp in recvs:
            cp.wait_recv()

    return pl.pallas_call(
        body, name=name, in_specs=[_ANY] * n, out_specs=[_ANY] * n,
        out_shape=[jax.ShapeDtypeStruct(a.shape, a.dtype) for a in arrs], input_output_aliases={i: i for i in range(n)},
        scratch_shapes=[pltpu.SemaphoreType.DMA((n, 3)), pltpu.SemaphoreType.DMA((n, 3))],
    )(*arrs)


def _scatter_core(grads, name):
    n = len(grads)

    def body(*refs):
        g_refs, got_refs = refs[:n], refs[n:2 * n]
        send_sems, recv_sems = refs[2 * n:]
        x, y, c = _place()
        sends = [pltpu.make_async_remote_copy(
            src_ref=g_refs[a].at[2 * q + 1 - c], dst_ref=got_refs[a].at[q], send_sem=send_sems.at[a, q],
            recv_sem=recv_sems.at[a, q], device_id=(x, y, 1 - c), device_id_type=MESH_ID) for a in range(n) for q in range(4)]
        for cp in sends:
            cp.start()
        for cp in sends:
            cp.wait()

    return pl.pallas_call(
        body, name=name, in_specs=[_ANY] * n, out_specs=[_ANY] * n,
        out_shape=[jax.ShapeDtypeStruct((4,) + g.shape[1:], g.dtype) for g in grads],
        scratch_shapes=[pltpu.SemaphoreType.DMA((n, 4)), pltpu.SemaphoreType.DMA((n, 4))],
    )(*grads)


def _scatter_chips(parts, name):
    n = len(parts)

    def body(*refs):
        p_refs, o_refs = refs[:n], refs[n:2 * n]
        send_sems, recv_sems = refs[2 * n:]
        x, y, c = _place()
        chips = [(1 - x, y), (x, 1 - y), (1 - x, 1 - y)]
        sends = [pltpu.make_async_remote_copy(
            src_ref=p_refs[a].at[2 * px + py], dst_ref=o_refs[a].at[j], send_sem=send_sems.at[a, j],
            recv_sem=recv_sems.at[a, j], device_id=(px, py, c), device_id_type=MESH_ID)
            for a in range(n) for j, (px, py) in enumerate(chips)]
        for cp in sends:
            cp.start()
        for cp in sends:
            cp.wait()

    return pl.pallas_call(
        body, name=name, in_specs=[_ANY] * n, out_specs=[_ANY] * n,
        out_shape=[jax.ShapeDtypeStruct((3,) + p.shape[1:], p.dtype) for p in parts],
        scratch_shapes=[pltpu.SemaphoreType.DMA((n, 3)), pltpu.SemaphoreType.DMA((n, 3))],
    )(*parts)


GATHER_PEERS = [(0, 0, 1), (1, 0, 0), (0, 1, 0), (1, 1, 0)]
CHIP_PEERS = [(1, 0, 0), (0, 1, 0), (1, 1, 0)]
_HBM = pl.BlockSpec(memory_space=pltpu.HBM)
_SEM = pl.BlockSpec(memory_space=pltpu.SEMAPHORE)
_EFFECT = pltpu.SideEffectType.DATAFLOW_SIDE_EFFECTING


def _push_copies(src_refs, land_refs, send_sems, recv_sems, peers, src_of, slot_of):
    place = _place()
    flip = lambda v, f: 1 - v if f else v
    return [pltpu.make_async_remote_copy(
        src_ref=src_of(src_refs[a], k), dst_ref=land_refs[a].at[slot_of(k)], send_sem=send_sems[a], recv_sem=recv_sems[a],
        device_id=tuple(flip(v, f) for v, f in zip(place, peer)), device_id_type=MESH_ID)
        for a in range(len(src_refs)) for k, peer in enumerate(peers)]


def _push_start(srcs, land_shapes, peers, src_of, slot_of, name):
    n = len(srcs)

    def body(*refs):
        src_refs, land_refs = refs[:n], refs[n:2 * n]
        send_sems, recv_sems = refs[2 * n:3 * n], refs[3 * n:4 * n]
        token = refs[-1]
        for cp in _push_copies(src_refs, land_refs, send_sems, recv_sems, peers, src_of, slot_of):
            cp.start()
        token[...] = jnp.zeros_like(token)

    sems = [pltpu.SemaphoreType.DMA(())] * (2 * n)
    lands = [pltpu.with_memory_space_constraint(lax.empty(s.shape, s.dtype), pltpu.HBM) for s in land_shapes]
    res = pl.pallas_call(
        body, name=name, in_specs=[_HBM] * (2 * n), out_specs=[_SEM] * (2 * n) + [_HBM] * (2 * n) + [pl.BlockSpec(memory_space=pltpu.VMEM)],
        out_shape=sems + [pltpu.HBM(s.shape, s.dtype) for s in srcs] + [pltpu.HBM(s.shape, s.dtype) for s in land_shapes]
        + [jax.ShapeDtypeStruct((8, 128), F32)],
        input_output_aliases={i: 2 * n + i for i in range(2 * n)},
        compiler_params=pltpu.CompilerParams(has_side_effects=_EFFECT),
    )(*[pltpu.with_memory_space_constraint(s, pltpu.HBM) for s in srcs], *lands)
    return list(res[:n]), list(res[n:2 * n]), list(res[2 * n:3 * n]), list(res[3 * n:4 * n]), res[-1]


def _push_wait(send_sems, recv_sems, srcs, lands, after, peers, src_of, slot_of, name):
    n = len(srcs)

    def body(*refs):
        src_refs, land_refs = refs[:n], refs[n:2 * n]
        s_sems, r_sems = refs[2 * n:3 * n], refs[3 * n:4 * n]
        copies = _push_copies(src_refs, land_refs, s_sems, r_sems, peers, src_of, slot_of)
        for cp in copies:
            cp.wait_send()
        for cp in copies:
            cp.wait_recv()

    res = pl.pallas_call(
        body, name=name, in_specs=[_HBM] * (2 * n) + [_SEM] * (2 * n) + [_ANY], out_specs=[_HBM] * (2 * n),
        out_shape=[pltpu.HBM(s.shape, s.dtype) for s in srcs] + [pltpu.HBM(s.shape, s.dtype) for s in lands],
        input_output_aliases={i: i for i in range(2 * n)},
        compiler_params=pltpu.CompilerParams(has_side_effects=_EFFECT),
    )(*srcs, *lands, *send_sems, *recv_sems, after)
    return list(res[:n]), list(res[n:])


def _pick_sum(picked, rest, index, pick_of, out_dtype, name):
    nq, r, cdim = rest.shape
    one = nq == 3
    tr = _row_tile(r, 512, 16)
    tc = 512 if cdim % 512 == 0 else cdim
    grid = (1 if one else nq, r // tr, cdim // tc)

    def body(i_ref, p_ref, r_ref, o_ref):
        acc = p_ref[0].astype(F32)
        if one:
            for j in range(3):
                acc = acc + r_ref[j].astype(F32)
            o_ref[...] = acc.astype(out_dtype)
        else:
            o_ref[0] = (acc + r_ref[0].astype(F32)).astype(out_dtype)

    spec = pltpu.PrefetchScalarGridSpec(
        num_scalar_prefetch=1, grid=grid,
        in_specs=[pl.BlockSpec((1, tr, tc), lambda q, i, j, i_ref: (pick_of(q, i_ref[0]), i, j)),
                  pl.BlockSpec((3, tr, tc), lambda q, i, j, i_ref: (0, i, j)) if one else pl.BlockSpec((1, tr, tc), lambda q, i, j, i_ref: (q, i, j))],
        out_specs=pl.BlockSpec((tr, tc), lambda q, i, j, i_ref: (i, j)) if one else pl.BlockSpec((1, tr, tc), lambda q, i, j, i_ref: (q, i, j)))
    return pl.pallas_call(
        body, name=name, grid_spec=spec,
        out_shape=jax.ShapeDtypeStruct((r, cdim) if one else (nq, r, cdim), out_dtype),
        compiler_params=_params(("parallel", "parallel", "parallel")),
    )(index.astype(jnp.int32).reshape(1), picked, rest)


def _adamw_fn(w, g, m, v):
    m = ADAM_B1 * m + (1.0 - ADAM_B1) * g
    v = ADAM_B2 * v + (1.0 - ADAM_B2) * jnp.square(g)
    m_hat = m / (1.0 - ADAM_B1 ** ADAM_STEP)
    v_hat = v / (1.0 - ADAM_B2 ** ADAM_STEP)
    delta = -ADAM_LR * (m_hat / (jnp.sqrt(v_hat) + ADAM_EPS) + ADAM_WD * w)
    return delta, m, v


def _as2d(a):
    return a.reshape(-1, a.shape[-1])


def _adamw_shard(w, g, m, v, name):
    shape = w.shape
    ins = [_as2d(a) for a in (w, g, m, v)]
    cols = ins[0].shape[1]
    outs = _ew(_adamw_fn, [(a, True) for a in ins], [(cols, F32, "tile")] * 3, 256, name)
    return [o.reshape(shape) for o in outs]


def _adamw_small(ws, gs, ms, vs, name):
    shapes = [w.shape for w in ws]
    flat = lambda a: a.reshape(-1, 128) if a.size % 128 == 0 else a.reshape(1, -1)
    ins = [flat(a) for grp in zip(ws, gs, ms, vs) for a in grp]
    k = len(ws)

    def fn(*vals):
        out = []
        for i in range(k):
            out += list(_adamw_fn(*vals[4 * i:4 * i + 4]))
        return out

    outs = _whole(fn, ins, [(ins[4 * (i // 3)].shape, F32) for i in range(3 * k)], name)
    deltas = [outs[3 * i].reshape(shapes[i]) for i in range(k)]
    new_m = [outs[3 * i + 1].reshape(shapes[i]) for i in range(k)]
    new_v = [outs[3 * i + 2].reshape(shapes[i]) for i in range(k)]
    return deltas, new_m, new_v


def _sum8(stacked, name):
    def fn(a):
        s = a[0:1]
        for i in range(1, N_DEV):
            s = s + a[i:i + 1]
        return s
    w = stacked.shape[1]
    tw = 8192
    if w % tw:
        return _whole(fn, [stacked], [((1, w), F32)], name)[0]

    def body(a_ref, o_ref):
        o_ref[...] = fn(a_ref[...])

    return pl.pallas_call(body, name=name, grid=(w // tw,), in_specs=[pl.BlockSpec((N_DEV, tw), lambda i: (0, i))],
                          out_specs=pl.BlockSpec((1, tw), lambda i: (0, i)), out_shape=jax.ShapeDtypeStruct((1, w), F32))(stacked)


ROWS = 512


def _split_heads(p, nh):
    t = p.shape[0]
    return p.reshape(t, nh, p.shape[1] // nh).transpose(1, 0, 2)


def _merge_heads(p):
    nh, t, d = p.shape
    return p.transpose(1, 0, 2).reshape(t, nh * d)


def _layer_fwd(h, mod, w, small, rope, l, late=None):
    sh1, sc1, gt1, sh2, sc2, gt2 = mod
    rope_ret, rope_mla = rope
    t = h.shape[0]
    nm = lambda s: f"l{l}_{s}"
    a1 = _ew(_norm_mod_fn, [(h, True), (small["norm1_g"], False), (sc1, False), (sh1, False)], [(D_MODEL, BF16, "tile")], ROWS, nm("norm1"))[0]
    p_s5 = _mm(a1, w["w_in_t"], "nt", 512, 512, 2048, name=nm("proj_s5"), n=512)
    p_ret = _mm(a1, w["w_in_t"], "nt", 512, 512, 2048, name=nm("proj_ret"), b_off=1, n=1536)
    p_swa = _mm(a1, w["w_in_t"], "nt", 512, 256, 2048, name=nm("proj_swa"), b_off=8, n=768)
    p_mla = _mm(a1, w["w_in_t"][2816:], "nt", 512, 576, 2048, name=nm("proj_mla"))
    b3, c3, dskip, a_r, a_i = small["s5"]
    st_r, st_i = _s5_scan_fwd(p_s5, b3, a_r, a_i, nm("s5_scan"))
    ypre = _mm_blocks(st_r, c3[:S5_BLOCKS], "nn", 512, pair=(st_i, c3[S5_BLOCKS:]), name=nm("s5_y"))
    z = _ew(_s5_act_fn, [(ypre, True), (p_s5, True), (dskip, False)], [(GROUP_WIDTH, F32, "tile")], ROWS, nm("s5_act"))[0]
    zz = _mm(z, w["glu_w"], "nn", 512, 512, 512, name=nm("s5_zz"))
    y_s5 = _ew(_s5_glu_fn, [(z, True), (zz, True), (small["s5_glu_b"], False)], [(GROUP_WIDTH, BF16, "tile")], ROWS, nm("s5_glu"))[0]
    qk_ret = _split_heads(_rope(p_ret[:, :2 * RET_HEADS * RET_QK], rope_ret, nm("ret_rope")), 2 * RET_HEADS)
    o_ret, y_ret = _ret_fwd(qk_ret, p_ret, small["ret_lgam"], nm("ret"))
    qkv_swa = _split_heads(p_swa, 12)
    o_swa, lse_swa = _swa_fwd(qkv_swa, small["swa_sinks"], nm("swa"))
    y_swa = _merge_heads(o_swa).astype(BF16)
    cq, ckv, kr = p_mla[:, :MLA_Q_RANK], p_mla[:, MLA_Q_RANK:MLA_Q_RANK + MLA_KV_RANK], p_mla[:, MLA_Q_RANK + MLA_KV_RANK:]
    cqn = _ew(_rms_gain_fn, [(cq, True), (small["mla_q_norm"], False)], [(MLA_Q_RANK, BF16, "tile")], ROWS, nm("mla_qnorm"))[0]
    ckvn = _ew(_rms_gain_fn, [(ckv, True), (small["mla_kv_norm"], False)], [(MLA_KV_RANK, BF16, "tile")], ROWS, nm("mla_kvnorm"))[0]
    q_full = _mm(cqn, w["w_uq_t"], "nt", 512, 768, 384, name=nm("mla_q"))
    kv_full = _mm(ckvn, w["w_ukv_t"], "nt", 512, 1024, 128, BF16, name=nm("mla_kv"))
    nq = q_full.shape[1]
    roped = _rope(jnp.concatenate([q_full, kr, jnp.zeros_like(kr)], axis=1), rope_mla, nm("mla_rope"), out_dtype=BF16)
    q4 = roped[:, :nq].reshape(t, MLA_HEADS, MLA_NOPE + MLA_ROPE)
    qn = q4[:, :, :MLA_NOPE].transpose(1, 0, 2)
    rot = jnp.concatenate([q4[:, :, MLA_NOPE:].transpose(1, 0, 2), roped[None, :, nq:nq + MLA_ROPE]], axis=0)
    o_mla, lse_mla = _mla_attend(qn, rot, kv_full, nm("mla"))
    cat = jnp.concatenate([y_s5, y_ret, y_swa, o_mla.astype(BF16)], axis=1)
    if late is not None:
        w = {**w, **late(lse_mla)}
    mixed = _mm(cat, w["w_out"], "nn", 512, 1024, 2048, name=nm("out_proj"))
    h1 = _ew(_gate_add_fn, [(h, True), (mixed, True), (gt1, False)], [(D_MODEL, F32, "tile")], ROWS, nm("res1"))[0]
    a2 = _ew(_norm_mod_fn, [(h1, True), (small["norm2_g"], False), (sc2, False), (sh2, False)], [(D_MODEL, BF16, "tile")], ROWS, nm("norm2"))[0]
    hid, act = _mm(a2, w["w1_t"], "nt", 1024, 1024, 2048, name=nm("mlp1"), epi=lambda acc: (acc, _relu2_fn(acc)), epi_outs=[F32, BF16])
    mo = _mm(act, w["w2"], "nn", 1024, 1024, 2048, name=nm("mlp2"))
    h2 = _ew(_gate_add_fn, [(h1, True), (mo, True), (gt2, False)], [(D_MODEL, F32, "tile")], ROWS, nm("res2"))[0]
    saved = dict(w=w, h=h, a1=a1, p_s5=p_s5, p_ret=p_ret, st_r=st_r, st_i=st_i, ypre=ypre, z=z, zz=zz, qk_ret=qk_ret, o_ret=o_ret,
                 qkv_swa=qkv_swa, o_swa=o_swa, lse_swa=lse_swa, cq=cq, ckv=ckv, cqn=cqn, ckvn=ckvn, qn=qn, rot=rot,
                 kv_full=kv_full, o_mla=o_mla, lse_mla=lse_mla, cat=cat, mixed=mixed, h1=h1, a2=a2, hid=hid, act=act, mo=mo)
    return h2, saved


def _layer_bwd(dh2, mod, w, small, rope, s, l, after_mlp=None):
    sh1, sc1, gt1, sh2, sc2, gt2 = mod
    rope_ret, rope_mla = rope
    t = dh2.shape[0]
    nm = lambda n: f"l{l}_{n}_bwd"
    gb, gs = {}, {}
    row = (D_MODEL, F32, "acc")
    dmo, dgt2 = _ew(lambda d, y, gt: (d * gt, jnp.sum(d * y, axis=0, keepdims=True)),
                    [(dh2, True), (s["mo"], True), (gt2, False)], [(D_MODEL, BF16, "tile"), row], ROWS, nm("res2"))
    dhid = _mm(dmo, w["w2"], "nt", 1024, 1024, 2048, name=nm("mlp2_x"), epi=lambda acc, x: (acc * 2.0 * jnp.maximum(x, 0.0),),
               epi_ins=[s["hid"]], epi_outs=[BF16])[0]
    gb["w2"] = _mm(s["act"], dmo, "tn", 1024, 1024, 4096, BF16, name=nm("mlp2_w"))
    da2 = _mm(dhid, w["w1_t"], "nn", 1024, 1024, 2048, name=nm("mlp1_x"))
    gb["w1_t"] = _mm(dhid, s["a2"], "tn", 1024, 1024, 4096, BF16, name=nm("mlp1_w"))
    if after_mlp is not None:
        gt1 = gt1 + after_mlp(gb)

    def norm_bwd(hh, g, sc, sh, da, dres):
        dh_, dg, dsc, dsh = _vjp_block(_norm_mod_fn, 4)(hh, g, sc, sh, da)
        return dh_ + dres, dg, dsc, dsh

    dh1, gs["norm2_g"], dsc2, dsh2 = _ew(norm_bwd, [(s["h1"], True), (small["norm2_g"], False), (sc2, False), (sh2, False), (da2, True), (dh2, True)],
                                         [(D_MODEL, F32, "tile"), row, row, row], ROWS, nm("norm2"))
    dmixed, dgt1 = _ew(lambda d, y, gt: (d * gt, jnp.sum(d * y, axis=0, keepdims=True)),
                       [(dh1, True), (s["mixed"], True), (gt1, False)], [(D_MODEL, BF16, "tile"), row], ROWS, nm("res1"))
    dcat = _mm(dmixed, w["w_out"], "nt", 512, 1024, 2048, name=nm("out_proj_x"))
    gb["w_out"] = _mm(s["cat"], dmixed, "tn", 1024, 1024, 4096, BF16, name=nm("out_proj_w"))
    dy_s5, dy_ret, dy_swa, dy_mla = (dcat[:, i * GROUP_WIDTH:(i + 1) * GROUP_WIDTH] for i in range(4))
    b3, c3, dskip, a_r, a_i = small["s5"]
    gw = (GROUP_WIDTH, F32, "tile")
    gacc = (GROUP_WIDTH, F32, "acc")
    dz_a, dzz, gs["s5_glu_b"] = _ew(_vjp_block(_s5_glu_fn, 3), [(s["z"], True), (s["zz"], True), (small["s5_glu_b"], False), (dy_s5, True)],
                                    [gw, gw, gacc], ROWS, nm("s5_glu"))
    dz_b = _mm(dzz, w["glu_w"], "nt", 512, 512, 512, name=nm("s5_zz_x"))
    gb["glu_w"] = _mm(s["z"], dzz, "tn", 512, 512, 1024, BF16, name=nm("s5_zz_w"))

    def act_bwd(ypre, u, dsk, dza, dzb):
        return _vjp_block(_s5_act_fn, 3)(ypre, u, dsk, dza + dzb)

    dypre, du_a, g_dskip = _ew(act_bwd, [(s["ypre"], True), (s["p_s5"], True), (dskip, False), (dz_a, True), (dz_b, True)],
                               [gw, gw, gacc], ROWS, nm("s5_act"))
    ch, st = S5_PACK * S5_CH, S5_PACK * S5_STATE
    g_c3 =jnp.concatenate([_mm_blocks_tn(s["st_r"], dypre, st, ch, lambda j: j, nm("s5_y_w_re")),
                            _mm_blocks_tn(s["st_i"], dypre, st, ch, lambda j: j, nm("s5_y_w_im"))], axis=0)
    dbu_r, dbu_i, g_ar, g_ai = _s5_scan_bwd(dypre, c3, s["st_r"], s["st_i"], a_r, a_i, nm("s5_scan"))
    du_b = _mm_blocks(dbu_r, b3[:S5_BLOCKS], "nt", 512, pair=(dbu_i, b3[S5_BLOCKS:]), name=nm("s5_bu_x"))
    g_b3 = jnp.concatenate([_mm_blocks_tn(s["p_s5"], dbu_r, ch, st, lambda j: j, nm("s5_bu_w_re")),
                            _mm_blocks_tn(s["p_s5"], dbu_i, ch, st, lambda j: j, nm("s5_bu_w_im"))], axis=0)
    gs["s5"] = (g_b3, g_c3, g_dskip, g_ar, g_ai)
    dqk_rot, dk_rot, dv_ret, dg_ret = _ret_bwd(s["qk_ret"], s["p_ret"], s["o_ret"], dy_ret, small["ret_lgam"], nm("ret"))
    dqk = _rope(_merge_heads(jnp.concatenate([dqk_rot, dk_rot], axis=0)), rope_ret, nm("ret_rope"), inverse=True, out_dtype=BF16)
    dqkv_swa, gs["swa_sinks"] = _swa_bwd(s["qkv_swa"], s["o_swa"], s["lse_swa"], _split_heads(dy_swa, SWA_HEADS), small["swa_sinks"], nm("swa"))
    delta = _mla_delta(s["o_mla"], dy_mla, nm("mla_delta"))
    dqn, dqr, dkn, dv_mla, dkr_heads = _mla_attend_bwd(s["qn"], s["rot"], s["kv_full"], s["lse_mla"], delta, dy_mla, nm("mla_att"))
    dkv_full = jnp.stack([dkn.reshape(t, MLA_HEADS, MLA_NOPE), dv_mla.reshape(t, MLA_HEADS, MLA_V)], axis=2).reshape(t, 2 * MLA_HEADS * MLA_NOPE)
    dkr_rot = _ew(lambda a, b, c, d: a + b + c + d, [(dkr_heads[i], True) for i in range(MLA_HEADS)], [(MLA_ROPE, F32, "tile")], ROWS, nm("mla_dkr"))[0]
    nq = MLA_HEADS * (MLA_NOPE + MLA_ROPE)
    dq_rot = jnp.concatenate([dqn.transpose(1, 0, 2), dqr.transpose(1, 0, 2)], axis=2).reshape(t, nq)
    droped = _rope(jnp.concatenate([dq_rot, dkr_rot, jnp.zeros_like(dkr_rot)], axis=1), rope_mla, nm("mla_rope"), inverse=True, out_dtype=BF16)
    dq_full, dkr = droped[:, :nq], droped[:, nq:nq + MLA_ROPE]
    dcqn = _mm(dq_full, w["w_uq_t"], "nn", 512, 384, 768, name=nm("mla_q_x"))
    gb["w_uq_t"] = _mm(dq_full, s["cqn"], "tn", 768, 384, 1024, BF16, name=nm("mla_q_w"))
    dckvn = _mm(dkv_full, w["w_ukv_t"], "nn", 512, 128, 1024, name=nm("mla_kv_x"))
    gb["w_ukv_t"] = _mm(dkv_full, s["ckvn"], "tn", 1024, 128, 1024, BF16, name=nm("mla_kv_w"))
    dcq, gs["mla_q_norm"] = _ew(_vjp_block(_rms_gain_fn, 2), [(s["cq"], True), (small["mla_q_norm"], False), (dcqn, True)],
                                [(MLA_Q_RANK, BF16, "tile"), (MLA_Q_RANK, F32, "acc")], ROWS, nm("mla_qnorm"))
    dckv, gs["mla_kv_norm"] = _ew(_vjp_block(_rms_gain_fn, 2), [(s["ckv"], True), (small["mla_kv_norm"], False), (dckvn, True)],
                                  [(MLA_KV_RANK, BF16, "tile"), (MLA_KV_RANK, F32, "acc")], ROWS, nm("mla_kvnorm"))
    du = _ew(lambda a, b: a + b, [(du_a, True), (du_b, True)], [(GROUP_WIDTH, BF16, "tile")], ROWS, nm("s5_du"))[0]
    bf = lambda a: a.astype(BF16)
    dproj = jnp.concatenate([du, bf(dqk), bf(dv_ret), bf(dg_ret), bf(_merge_heads(dqkv_swa)), bf(dcq), bf(dckv), bf(dkr)], axis=1)
    da1 = _mm(dproj, w["w_in_t"], "nn", 512, 1024, N_IN, name=nm("proj_x"))
    gb["w_in_t"] = _mm(dproj, s["a1"], "tn", N_IN, 512, 2048, BF16, name=nm("proj_w"))
    dh, gs["norm1_g"], dsc1, dsh1 = _ew(norm_bwd, [(s["h"], True), (small["norm1_g"], False), (sc1, False), (sh1, False), (da1, True), (dh1, True)],
                                        [(D_MODEL, F32, "tile"), row, row, row], ROWS, nm("norm1"))
    dmod = jnp.concatenate([dsh1, dsc1, dgt1, dsh2, dsc2, dgt2], axis=1)
    return dh, gb, gs, dmod


BIG = ("w_in_t", "w1_t", "w_uq_t", "w_ukv_t", "w_out", "w2", "glu_w")
MLP_BIG = ("w1_t", "w2")
LATE_BIG = ("w_out", "w1_t", "w2")
S5_NAMES = ("s5_lambda_re", "s5_lambda_im", "s5_log_dt", "s5_b_re", "s5_b_im", "s5_c_re", "s5_c_im", "s5_d")


def kernel(x, c, norm1_g, norm2_g, ada_w, ada_b, w_in, s5_lambda_re, s5_lambda_im, s5_log_dt, s5_b_re, s5_b_im, s5_c_re, s5_c_im, s5_d, s5_glu_w, s5_glu_b, swa_sinks, mla_q_norm, mla_kv_norm, mla_w_uq, mla_w_ukv, w_out, mlp_w1, mlp_w2, final_norm_g, loss_target, m_norm1_g, m_norm2_g, m_ada_w, m_ada_b, m_w_in, m_s5_lambda_re, m_s5_lambda_im, m_s5_log_dt, m_s5_b_re, m_s5_b_im, m_s5_c_re, m_s5_c_im, m_s5_d, m_s5_glu_w, m_s5_glu_b, m_swa_sinks, m_mla_q_norm, m_mla_kv_norm, m_mla_w_uq, m_mla_w_ukv, m_w_out, m_mlp_w1, m_mlp_w2, m_final_norm_g, v_norm1_g, v_norm2_g, v_ada_w, v_ada_b, v_w_in, v_s5_lambda_re, v_s5_lambda_im, v_s5_log_dt, v_s5_b_re, v_s5_b_im, v_s5_c_re, v_s5_c_im, v_s5_d, v_s5_glu_w, v_s5_glu_b, v_swa_sinks, v_mla_q_norm, v_mla_kv_norm, v_mla_w_uq, v_mla_w_ukv, v_w_out, v_mlp_w1, v_mlp_w2, v_final_norm_g):
    names = ["norm1_g", "norm2_g", "ada_w", "ada_b", "w_in", "s5_lambda_re", "s5_lambda_im", "s5_log_dt", "s5_b_re", "s5_b_im",
             "s5_c_re", "s5_c_im", "s5_d", "s5_glu_w", "s5_glu_b", "swa_sinks", "mla_q_norm", "mla_kv_norm", "mla_w_uq",
             "mla_w_ukv", "w_out", "mlp_w1", "mlp_w2", "final_norm_g"]
    env = locals()
    wts = {n: env[n] for n in names}
    mom = {n: env["m_" + n] for n in names}
    var = {n: env["v_" + n] for n in names}
    t = x.shape[1]
    me = 4 * lax.axis_index("x") + 2 * lax.axis_index("y") + lax.axis_index("c")
    rope = _rope_tables(t)
    ret_lgam = jnp.log1p(-(2.0 ** (-5.0 - jnp.arange(RET_HEADS, dtype=F32))))

    tr = lambda a: a.transpose(0, 2, 1)
    shard = {"w_in_t": tr(w_in), "w1_t": tr(mlp_w1), "w_uq_t": tr(mla_w_uq), "w_ukv_t": tr(mla_w_ukv),
             "w_out": w_out, "w2": mlp_w2, "glu_w": s5_glu_w}
    to_send = [{k: shard[k][l].astype(BF16) for k in BIG} for l in range(DEPTH)]
    as_rows = lambda keys, arrs: {k: a.reshape(-1, shard[k].shape[2]) for k, a in zip(keys, arrs)}
    first = [k for k in BIG if k not in LATE_BIG]
    gathered = _all_gather([to_send[0][k] for k in first] + [c], "gather_weights_first")
    c_all = gathered[-1].reshape(N_DEV, D_MODEL)
    big = [as_rows(first, gathered[:len(first)]), None]
    own_slot = lambda k: 4 * lax.axis_index("x") + 2 * lax.axis_index("y") + lax.axis_index("c")

    def gather_start(arrs, tag):
        return _push_start(arrs, [jax.ShapeDtypeStruct((N_DEV,) + a.shape, a.dtype) for a in arrs], GATHER_PEERS,
                           lambda ref, k: ref, own_slot, f"gather_weights_{tag}_start")

    def gather_finish(started, after, tag):
        sent, landed = _push_wait(started[0], started[1], started[2], started[3], after, GATHER_PEERS, lambda ref, k: ref, own_slot,
                                  f"gather_weights_{tag}_wait")
        with_own = [lax.dynamic_update_index_in_dim(full, own, me, 0) for full, own in zip(landed, sent)]
        return _pass_to_sibling(with_own, f"gather_weights_{tag}_pass")

    gather0 = gather_start([to_send[0][k] for k in LATE_BIG], "l0")
    gather1 = gather_start([to_send[1][k] for k in BIG], "l1")

    c_act = _whole(lambda v: v * jax.nn.sigmoid(v), [c_all], [((N_DEV, D_MODEL), F32)], "cond_silu")[0]
    c_pad = jnp.concatenate([c_act, jnp.zeros((128 - N_DEV, D_MODEL), F32)], axis=0)
    cols = ada_w.shape[2]
    mod_part = [_mm(c_pad, ada_w[l], "nn", 128, cols, 512, name=f"l{l}_mod")[:N_DEV] for l in range(DEPTH)]
    mod_all = _all_gather([jnp.stack(mod_part)], "gather_mod")[0]
    mod_rows = lax.dynamic_index_in_dim(mod_all, me, axis=2, keepdims=False)
    mods = []
    for l in range(DEPTH):
        row = mod_rows[:, l].reshape(1, 6 * D_MODEL) + ada_b[l][None]
        if l == 0:
            row = row + (gather0[4][0, 0] + gather1[4][0, 0])
        mods.append([row[:, i * D_MODEL:(i + 1) * D_MODEL] for i in range(6)])

    smalls, s5_pulls = [], []
    for l in range(DEPTH):
        s5_ops, pull = jax.vjp(_s5_prep, *[wts[n][l] for n in S5_NAMES])
        s5_pulls.append(pull)
        smalls.append(dict(norm1_g=norm1_g[l][None], norm2_g=norm2_g[l][None], s5=s5_ops, s5_glu_b=s5_glu_b[l][None],
                           swa_sinks=swa_sinks[l], mla_q_norm=mla_q_norm[l][None], mla_kv_norm=mla_kv_norm[l][None], ret_lgam=ret_lgam))
    h = x[0]
    saved = []
    for l in range(DEPTH):
        if l == 0:
            late = lambda after: as_rows(LATE_BIG, gather_finish(gather0, after, "l0"))
        else:
            big[1] = as_rows(BIG, gather_finish(gather1, h, "l1"))
            late = None
        h, s = _layer_fwd(h, mods[l], big[l], smalls[l], rope, l, late)
        big[l] = s.pop("w")
        saved.append(s)

    fg = final_norm_g[None]
    tgt = loss_target[0]
    loss_local = _ew(_final_fn, [(h, True), (fg, False), (tgt, True)], [(1, F32, "acc")], ROWS, "loss")[0]

    def final_bwd(hh, g, tg):
        dh_, dg, _ = _vjp_block(_final_fn, 3)(hh, g, tg, jnp.ones((1, 1), F32))
        return dh_, dg

    dh, g_final = _ew(final_bwd, [(h, True), (fg, False), (tgt, True)], [(D_MODEL, F32, "tile"), (D_MODEL, F32, "acc")], ROWS, "loss_bwd")
    loss = lax.psum(loss_local[0, 0], ("x", "y", "c"))

    core, chip = lax.axis_index("c"), 2 * lax.axis_index("x") + lax.axis_index("y")

    def core_stage(g_layer, keys, tag):
        g_list = [g_layer[k].reshape(N_DEV, -1, g_layer[k].shape[1]) for k in keys]
        got = _scatter_core(g_list, f"scatter_core_{tag}")
        return [_pick_sum(g, o, core, lambda q, c_: 2 * q + c_, BF16, f"{tag}_core_sum_{k}") for k, g, o in zip(keys, g_list, got)]

    def their_block(ref, k):
        x_, y_ = lax.axis_index("x"), lax.axis_index("y")
        dx, dy, _ = CHIP_PEERS[k]
        return ref.at[2 * (1 - x_ if dx else x_) + (1 - y_ if dy else y_)]

    def chips_start(halves, tag):
        return _push_start(halves, [jax.ShapeDtypeStruct((3,) + a.shape[1:], a.dtype) for a in halves], CHIP_PEERS,
                           their_block, lambda k: k, f"scatter_chips_{tag}_start")

    def chips_wait(started, after, tag):
        return _push_wait(started[0], started[1], started[2], started[3], after, CHIP_PEERS, their_block, lambda k: k, f"scatter_chips_{tag}_wait")

    g_small, dmods = [None] * DEPTH, [None] * DEPTH
    dh, g_big1, g_small[1], dmods[1] = _layer_bwd(dh, mods[1], big[1], smalls[1], rope, saved[1], 1)
    chips1 = chips_start(core_stage(g_big1, BIG, "l1"), "l1")
    mods0 = [m + chips1[4][0, 0] for m in mods[0]]
    early = {}

    def after_mlp(gb):
        early["mlp"] = chips_start(core_stage(gb, MLP_BIG, "l0_mlp"), "l0_mlp")
        return early["mlp"][4][0, 0]

    dh, g_big0, g_small[0], dmods[0] = _layer_bwd(dh, mods0, big[0], smalls[0], rope, saved[0], 0, after_mlp=after_mlp)
    grad_x = dh[None]
    rest = [k for k in BIG if k not in MLP_BIG]
    chips_rest = chips_start(core_stage(g_big0, rest, "l0_rest"), "l0_rest")
    rest_token = chips_rest[4]
    halves1, landed1 = chips_wait(chips1, rest_token, "l1")
    halves_mlp, landed_mlp = chips_wait(early["mlp"], rest_token, "l0_mlp")
    terms = {(1, k): pair for k, pair in zip(BIG, zip(halves1, landed1))}
    terms.update({(0, k): pair for k, pair in zip(MLP_BIG, zip(halves_mlp, landed_mlp))})
    chip_sum = lambda l, k: _pick_sum(*terms[l, k], chip, lambda q, m_: m_, F32, f"l{l}_chip_sum_{k}")

    small_parts = []
    for l in range(DEPTH):
        gs = g_small[l]
        s5g = s5_pulls[l](gs["s5"])
        small_parts += [gs["norm1_g"], gs["norm2_g"], *s5g, gs["s5_glu_b"], gs["swa_sinks"], gs["mla_q_norm"], gs["mla_kv_norm"]]
    small_parts += [g_final, *dmods]
    sizes = [int(np.prod(p.shape)) for p in small_parts]
    flat = jnp.concatenate([p.reshape(1, -1) for p in small_parts], axis=1) + rest_token[0, 0]
    pad = (-flat.shape[1]) % 8192
    flat = jnp.pad(flat, ((0, 0), (0, pad)))
    flat_all = _all_gather([flat], "gather_small_grads")[0].reshape(N_DEV, -1)
    summed = _sum8(flat_all, "sum_small_grads")
    pieces, off = [], 0
    for sz in sizes:
        pieces.append(summed[0, off:off + sz])
        off += sz
    small_names = ["norm1_g", "norm2_g", *S5_NAMES, "s5_glu_b", "swa_sinks", "mla_q_norm", "mla_kv_norm"]
    per_layer = len(small_names)
    grads = {}
    for i, n in enumerate(small_names):
        grads[n] = jnp.stack([pieces[l * per_layer + i].reshape(wts[n].shape[1:]) for l in range(DEPTH)])
    grads["final_norm_g"] = pieces[DEPTH * per_layer]
    grads["ada_b"] = jnp.stack([pieces[DEPTH * per_layer + 1 + l] for l in range(DEPTH)])

    mod_off = sum(sizes[:DEPTH * per_layer + 1])
    dmod_all = flat_all[:, mod_off:mod_off + DEPTH * 6 * D_MODEL].reshape(N_DEV, DEPTH, N_DEV, cols)
    dmod_mine = lax.dynamic_index_in_dim(dmod_all, me, axis=2, keepdims=False).transpose(1, 0, 2)
    dmod_pad = jnp.concatenate([dmod_mine, jnp.zeros((DEPTH, 128 - N_DEV, cols), F32)], axis=1)
    grads["ada_w"] = jnp.stack([_mm(c_pad, dmod_pad[l], "tn", 512, cols, 128, name=f"l{l}_ada_w_grad") for l in range(DEPTH)])

    out_g, out_d, out_m, out_v = dict(grads), {}, {}, {}
    orig = {"w_in_t": "w_in", "w1_t": "mlp_w1", "w_uq_t": "mla_w_uq", "w_ukv_t": "mla_w_ukv", "w_out": "w_out", "w2": "mlp_w2", "glu_w": "s5_glu_w"}

    def update_big(keys):
        for k in keys:
            n = orig[k]
            out_g[n] = jnp.stack([chip_sum(l, k) for l in range(DEPTH)])
            if k.endswith("_t"):
                out_g[n] = tr(out_g[n])
            out_d[n], out_m[n], out_v[n] = _adamw_shard(wts[n], out_g[n], mom[n], var[n], f"adamw_{n}")

    update_big(MLP_BIG)
    out_d["ada_w"], out_m["ada_w"], out_v["ada_w"] = _adamw_shard(wts["ada_w"], out_g["ada_w"], mom["ada_w"], var["ada_w"], "adamw_ada_w")
    small_all = small_names + ["ada_b", "final_norm_g"]
    ds, ms, vs = _adamw_small([wts[n] for n in small_all], [grads[n] for n in small_all], [mom[n] for n in small_all],
                              [var[n] for n in small_all], "adamw_small")
    for n, d, m_, v_ in zip(small_all, ds, ms, vs):
        out_d[n], out_m[n], out_v[n] = d, m_, v_
    halves_rest, landed_rest = chips_wait(chips_rest, out_d["ada_w"], "l0_rest")
    terms.update({(0, k): pair for k, pair in zip(rest, zip(halves_rest, landed_rest))})
    update_big(rest)
    return (loss, grad_x, *[out_g[n] for n in names], *[out_d[n] for n in names], *[out_m[n] for n in names], *[out_v[n] for n in names])
```

```python
import functools
import math

import numpy as np
import jax
import jax.numpy as jnp
from jax import lax
from jax.experimental import pallas as pl
from jax.experimental.pallas import tpu as pltpu

F32 = jnp.float32
BF16 = jnp.bfloat16
_MXU_DTYPE = jnp.bfloat16

N_DEV = 8
D_MODEL = 2048
DEPTH = 2
GROUP_WIDTH = 512
D_FF = 8192
S5_CH, S5_GROUPS, S5_STATE = 16, 32, 64
S5_WIDTH = S5_GROUPS * S5_STATE
S5_PACK = 8
S5_BLOCKS = S5_GROUPS // S5_PACK
RET_HEADS, RET_QK, RET_V, RET_CHUNK = 4, 64, 128, 128
SWA_HD, SWA_HEADS, SWA_KV_HEADS, WINDOW = 64, 8, 2, 128
MLA_HEADS, MLA_Q_RANK, MLA_KV_RANK, MLA_NOPE, MLA_ROPE, MLA_V = 4, 384, 128, 128, 64, 128
ROPE_BASE = 10000.0
EPS = 1e-6
NEG = -1e30
N_IN = 3392
ADAM_LR, ADAM_B1, ADAM_B2, ADAM_EPS, ADAM_WD, ADAM_STEP = 0.001, 0.9, 0.999, 1e-08, 0.01, 10

VMEM_LIMIT_BYTES = 52 * 1024 * 1024
MESH_ID = pl.DeviceIdType.MESH
_ANY = pl.BlockSpec(memory_space=pl.ANY)
_SMEM = pl.BlockSpec(memory_space=pltpu.SMEM)


def _params(sem):
    return pltpu.CompilerParams(dimension_semantics=sem, vmem_limit_bytes=VMEM_LIMIT_BYTES)


_DIMS = {"nn": (((1,), (0,)), ((), ())), "nt": (((1,), (1,)), ((), ())), "tn": (((0,), (0,)), ((), ()))}


def _dot(a, b, mode="nn"):
    return lax.dot_general(a.astype(_MXU_DTYPE), b.astype(_MXU_DTYPE), _DIMS[mode], preferred_element_type=F32)


def _mm(a, b, mode, tm, tn, tk, out_dtype=F32, name="mm", b_off=0, n=None, pair=None, epi=None, epi_ins=(), epi_outs=None):
    if mode == "tn":
        kdim, m = a.shape
    else:
        m, kdim = a.shape
    if n is None:
        n = b.shape[0] if mode == "nt" else b.shape[1]
    tm, tn, tk = min(tm, m), min(tn, n), min(tk, kdim)
    assert m % tm == 0 and n % tn == 0 and kdim % tk == 0, (name, a.shape, b.shape, tm, tn, tk)
    nk = kdim // tk
    a_spec = pl.BlockSpec((tk, tm), lambda i, j, k: (k, i)) if mode == "tn" else pl.BlockSpec((tm, tk), lambda i, j, k: (i, k))
    if mode == "nt":
        b_spec = pl.BlockSpec((tn, tk), lambda i, j, k: (j + b_off, k))
    else:
        b_spec = pl.BlockSpec((tk, tn), lambda i, j, k: (k, j + b_off))
    o_spec = pl.BlockSpec((tm, tn), lambda i, j, k: (i, j))
    n_mm = 2 if pair is None else 4
    out_dtypes = [out_dtype] if epi is None else list(epi_outs)

    def body(*refs):
        ins, extra = refs[:n_mm], refs[n_mm:n_mm + len(epi_ins)]
        outs = refs[n_mm + len(epi_ins):n_mm + len(epi_ins) + len(out_dtypes)]
        part = _dot(ins[0][...], ins[1][...], mode)
        if pair is not None:
            part = part + _dot(ins[2][...], ins[3][...], mode)

        def finish(acc):
            vals = (acc,) if epi is None else epi(acc, *[r[...] for r in extra])
            for o_ref, v, dt in zip(outs, vals, out_dtypes):
                o_ref[...] = v.astype(dt)

        if nk == 1:
            finish(part)
        else:
            acc_ref = refs[-1]
            k = pl.program_id(2)

            @pl.when(k == 0)
            def _():
                acc_ref[...] = part

            @pl.when(k > 0)
            def _():
                acc_ref[...] += part

            @pl.when(k == nk - 1)
            def _():
                finish(acc_ref[...])

    operands = [a, b] + ([] if pair is None else list(pair)) + list(epi_ins)
    res = pl.pallas_call(
        body, name=name, grid=(m // tm, n // tn, nk),
        in_specs=[a_spec, b_spec] * (n_mm // 2) + [o_spec] * len(epi_ins),
        out_specs=[o_spec] * len(out_dtypes), out_shape=[jax.ShapeDtypeStruct((m, n), dt) for dt in out_dtypes],
        scratch_shapes=[] if nk == 1 else [pltpu.VMEM((tm, tn), F32)],
        compiler_params=_params(("parallel", "parallel", "arbitrary")),
    )(*operands)
    return res[0] if epi is None else res


def _mm_blocks(a, b, mode, tm, a_of=None, pair=None, name="mm_blocks"):
    a_of = a_of or (lambda j: j)
    m = a.shape[0]
    nj, kb, nb = b.shape
    a_w, o_w = (kb, nb) if mode == "nn" else (nb, kb)
    tm = min(tm, m)
    a_spec = pl.BlockSpec((tm, a_w), lambda i, j: (i, a_of(j)))
    b_spec = pl.BlockSpec((1, kb, nb), lambda i, j: (j, 0, 0))
    n_in = 2 if pair is None else 4

    def body(*refs):
        acc = _dot(refs[0][...], refs[1][0], mode)
        if pair is not None:
            acc = acc + _dot(refs[2][...], refs[3][0], mode)
        refs[n_in][...] = acc

    operands = [a, b] + ([] if pair is None else list(pair))
    return pl.pallas_call(
        body, name=name, grid=(m // tm, nj), in_specs=[a_spec, b_spec] * (n_in // 2),
        out_specs=pl.BlockSpec((tm, o_w), lambda i, j: (i, j)), out_shape=jax.ShapeDtypeStruct((m, nj * o_w), F32),
        compiler_params=_params(("parallel", "parallel")),
    )(*operands)


def _mm_blocks_tn(a, b, x, y, b_of, name):
    kdim = a.shape[0]
    nj = a.shape[1] // x

    def body(a_ref, b_ref, o_ref):
        o_ref[0] = _dot(a_ref[...], b_ref[...], "tn")

    return pl.pallas_call(
        body, name=name, grid=(nj,), in_specs=[pl.BlockSpec((kdim, x), lambda j: (0, j)), pl.BlockSpec((kdim, y), lambda j: (0, b_of(j)))],
        out_specs=pl.BlockSpec((1, x, y), lambda j: (j, 0, 0)), out_shape=jax.ShapeDtypeStruct((nj, x, y), F32),
        compiler_params=_params(("parallel",)),
    )(a, b)


SUBLANES = 8


def _row_tile(rows, target, mult=SUBLANES):
    best = None
    for cand in range(mult, min(rows, target) + 1, mult):
        if rows % cand == 0:
            best = cand
    return best or rows


def _ew(fn, ins, outs, tt, name):
    t = [a.shape[0] for a, tiled in ins if tiled][0]
    tt = _row_tile(t, tt)
    n_in = len(ins)
    in_specs = [pl.BlockSpec((tt, a.shape[1]), lambda i: (i, 0)) if tiled else pl.BlockSpec(a.shape, lambda i: (0, 0))
                for a, tiled in ins]
    out_specs, out_shapes = [], []
    for w, dt, kind in outs:
        if kind == "tile":
            out_specs.append(pl.BlockSpec((tt, w), lambda i: (i, 0)))
            out_shapes.append(jax.ShapeDtypeStruct((t, w), dt))
        else:
            out_specs.append(pl.BlockSpec((1, w), lambda i: (0, 0)))
            out_shapes.append(jax.ShapeDtypeStruct((1, w), F32))
    has_acc = any(kind == "acc" for _, _, kind in outs)

    def body(*refs):
        vals = fn(*[r[...] for r in refs[:n_in]])
        if not isinstance(vals, (tuple, list)):
            vals = (vals,)
        i = pl.program_id(0)
        for o_ref, v, (w, dt, kind) in zip(refs[n_in:], vals, outs):
            if kind == "tile":
                o_ref[...] = v.astype(dt)
            else:
                @pl.when(i == 0)
                def _(o_ref=o_ref, v=v):
                    o_ref[...] = v.astype(F32)

                @pl.when(i > 0)
                def _(o_ref=o_ref, v=v):
                    o_ref[...] += v.astype(F32)

    res = pl.pallas_call(
        body, name=name, grid=(t // tt,), in_specs=in_specs, out_specs=out_specs, out_shape=out_shapes,
        compiler_params=_params(("arbitrary" if has_acc else "parallel",)),
    )(*[a for a, _ in ins])
    return res


def _whole(fn, ins, outs, name):
    def body(*refs):
        vals = fn(*[r[...] for r in refs[:len(ins)]])
        if not isinstance(vals, (tuple, list)):
            vals = (vals,)
        for o_ref, v in zip(refs[len(ins):], vals):
            o_ref[...] = v.astype(o_ref.dtype)

    return pl.pallas_call(body, name=name, out_shape=[jax.ShapeDtypeStruct(s, dt) for s, dt in outs])(*ins)


def _rms(x):
    return x * lax.rsqrt(jnp.mean(x * x, axis=-1, keepdims=True) + EPS)


def _norm_mod_fn(h, g, sc, sh):
    return (_rms(h) * g) * (1.0 + sc) + sh


def _rms_gain_fn(x, g):
    return _rms(x) * g


def _gate_add_fn(h, y, gt):
    return h + gt * y


def _relu2_fn(x):
    return jnp.square(jnp.maximum(x, 0.0))


def _s5_act_fn(ypre, u, dskip):
    return jax.nn.gelu(ypre + dskip * u)


def _s5_glu_fn(z, zz, b):
    return z * jax.nn.sigmoid(zz + b)


def _ret_gate_fn(o, g):
    return _rms(o) * (g * jax.nn.sigmoid(g))


def _final_fn(h, g, tgt):
    err = _rms(h) * g - tgt
    return 0.5 * jnp.sum(jnp.mean(err * err, axis=-1, keepdims=True), axis=0, keepdims=True)


def _vjp_block(fn, n_args):
    def bwd(*vals):
        _, pull = jax.vjp(fn, *vals[:n_args])
        return pull(vals[n_args])
    return bwd


def _rope_tables(t):
    d = RET_QK
    inv = ROPE_BASE ** (-jnp.arange(0, d, 2, dtype=F32) / d)
    ang = jnp.arange(t, dtype=F32)[:, None] * inv[None, :]
    cos, sin = jnp.cos(ang), jnp.sin(ang)
    cos2, sin2 = jnp.concatenate([cos, cos], -1), jnp.concatenate([-sin, sin], -1)
    ret = (jnp.tile(cos2, (1, 8)), jnp.tile(sin2, (1, 8)))
    one, zero = jnp.ones((t, MLA_NOPE), F32), jnp.zeros((t, MLA_NOPE), F32)
    mla_c = jnp.concatenate([jnp.tile(jnp.concatenate([one, cos2], -1), (1, MLA_HEADS)), cos2, one[:, :d]], -1)
    mla_s = jnp.concatenate([jnp.tile(jnp.concatenate([zero, sin2], -1), (1, MLA_HEADS)), sin2, zero[:, :d]], -1)
    return ret, (mla_c, mla_s)


def _rope_fn(x, c, s, sign):
    w = x.shape[1]
    lane = lax.broadcasted_iota(jnp.int32, x.shape, 1)
    swapped = jnp.where((lane & 63) < 32, pltpu.roll(x, w - 32, 1), pltpu.roll(x, 32, 1))
    return x * c + swapped * (sign * s)


def _rope(x, tables, name, inverse=False, out_dtype=F32):
    c, s = tables
    fn = functools.partial(_rope_fn, sign=-1.0 if inverse else 1.0)
    return _ew(fn, [(x, True), (c, True), (s, True)], [(x.shape[1], out_dtype, "tile")], ROWS, name)[0]


SCAN_ROWS, SCAN_LANES = 256, 512


def _cmul(ar, ai, br, bi):
    return ar * br - ai * bi, ar * bi + ai * br


def _group_powers(ar, ai, reverse):
    shape = (SUBLANES, ar.shape[1])
    row = lax.broadcasted_iota(jnp.int32, shape, 0)
    pr, pi = ar, ai
    out_r, out_i = jnp.zeros(shape, F32), jnp.zeros(shape, F32)
    for e in range(1, SUBLANES + 1):
        hit = row == (SUBLANES - e if reverse else e - 1)
        out_r, out_i = jnp.where(hit, pr, out_r), jnp.where(hit, pi, out_i)
        if e < SUBLANES:
            pr, pi = _cmul(pr, pi, ar, ai)
    return out_r, out_i


def _scan_chunk(in_r_ref, in_i_ref, out_r_ref, out_i_ref, ar, ai, cr, ci, reverse, visit=None):
    rows, lanes = in_r_ref.shape
    sub = lax.broadcasted_iota(jnp.int32, (SUBLANES, lanes), 0)
    edge_r, edge_i = _group_powers(ar, ai, reverse)
    steps, pr, pi, k = [], ar, ai, 1
    while k < SUBLANES:
        steps.append((k, pr, pi))
        pr, pi = _cmul(pr, pi, pr, pi)
        k *= 2
    groups = range(rows // SUBLANES)
    for g in (reversed(groups) if reverse else groups):
        sl = slice(g * SUBLANES, (g + 1) * SUBLANES)
        xr, xi = in_r_ref[sl, :], in_i_ref[sl, :]
        for k, pr, pi in steps:
            shift = SUBLANES - k if reverse else k
            keep = sub < SUBLANES - k if reverse else sub >= k
            tr, ti = _cmul(pr, pi, pltpu.roll(xr, shift, 0), pltpu.roll(xi, shift, 0))
            xr, xi = xr + jnp.where(keep, tr, 0.0), xi + jnp.where(keep, ti, 0.0)
        tr, ti = _cmul(edge_r, edge_i, cr, ci)
        xr, xi = xr + tr, xi + ti
        out_r_ref[sl, :] = xr
        out_i_ref[sl, :] = xi
        if visit is not None:
            visit(sl, xr, xi, cr, ci)
        edge = slice(0, 1) if reverse else slice(SUBLANES - 1, SUBLANES)
        cr, ci = xr[edge, :], xi[edge, :]
    return cr, ci


def _s5_scan_specs(rows, row_block):
    assert SCAN_LANES == S5_PACK * S5_STATE
    chan = pl.BlockSpec((rows, S5_PACK * S5_CH), lambda j, i: (row_block(i), j))
    op = lambda off, shape: pl.BlockSpec((1,) + shape, lambda j, i: (j + off, 0, 0))
    blk = pl.BlockSpec((rows, SCAN_LANES), lambda j, i: (row_block(i), j))
    par = pl.BlockSpec((1, SCAN_LANES), lambda j, i: (0, j))
    return chan, op, blk, par


def _s5_scan_fwd(u, b3, a_r, a_i, name):
    t = u.shape[0]
    rows = min(SCAN_ROWS, t)
    chan, op, blk, par = _s5_scan_specs(rows, lambda i: i)

    def body(u_ref, b_re_ref, b_im_ref, ar_ref, ai_ref, or_ref, oi_ref, cr_ref, ci_ref, sr_ref, si_ref):
        i = pl.program_id(1)

        @pl.when(i == 0)
        def _():
            cr_ref[...] = jnp.zeros_like(cr_ref)
            ci_ref[...] = jnp.zeros_like(ci_ref)

        sr_ref[...] = _dot(u_ref[...], b_re_ref[0])
        si_ref[...] = _dot(u_ref[...], b_im_ref[0])
        ar, ai = ar_ref[...], ai_ref[...]
        cr, ci = _scan_chunk(sr_ref, si_ref, or_ref, oi_ref, ar, ai, cr_ref[...], ci_ref[...], reverse=False)
        cr_ref[...] = cr
        ci_ref[...] = ci

    st_r, st_i = pl.pallas_call(
        body, name=name, grid=(S5_BLOCKS, t // rows),
        in_specs=[chan, op(0, b3.shape[1:]), op(S5_BLOCKS, b3.shape[1:]), par, par], out_specs=[blk, blk],
        out_shape=[jax.ShapeDtypeStruct((t, S5_WIDTH), F32)] * 2,
        scratch_shapes=[pltpu.VMEM((1, SCAN_LANES), F32)] * 2 + [pltpu.VMEM((rows, SCAN_LANES), F32)] * 2,
        compiler_params=_params(("parallel", "arbitrary")),
    )(u, b3, b3, a_r, a_i)
    return st_r, st_i


def _s5_scan_bwd(dy, c3, st_r, st_i, a_r, a_i, name):
    t = dy.shape[0]
    rows = min(SCAN_ROWS, t)
    nc = t // rows
    chan, op, blk, par = _s5_scan_specs(rows, lambda i: nc - 1 - i)

    def body(dy_ref, c_re_ref, c_im_ref, xr_ref, xi_ref, ar_ref, ai_ref, gr_ref, gi_ref, dar_ref, dai_ref, cr_ref, ci_ref, dr_ref, di_ref):
        i = pl.program_id(1)
        dr_ref[...] = _dot(dy_ref[...], c_re_ref[0], "nt")
        di_ref[...] = _dot(dy_ref[...], c_im_ref[0], "nt")

        @pl.when(i == 0)
        def _():
            cr_ref[...] = jnp.zeros_like(cr_ref)
            ci_ref[...] = jnp.zeros_like(ci_ref)
            dar_ref[...] = jnp.zeros_like(dar_ref)
            dai_ref[...] = jnp.zeros_like(dai_ref)

        ar, ai = ar_ref[...], ai_ref[...]
        cr, ci = cr_ref[...], ci_ref[...]
        last = lax.broadcasted_iota(jnp.int32, (SUBLANES, SCAN_LANES), 0) == SUBLANES - 1
        sums = [jnp.zeros((SUBLANES, SCAN_LANES), F32), jnp.zeros((SUBLANES, SCAN_LANES), F32)]

        def visit(sl, gr, gi, next_r, next_i):
            nr = jnp.where(last, next_r, pltpu.roll(gr, SUBLANES - 1, 0))
            ni = jnp.where(last, next_i, pltpu.roll(gi, SUBLANES - 1, 0))
            xr, xi = xr_ref[sl, :], xi_ref[sl, :]
            sums[0] = sums[0] + (nr * xr + ni * xi)
            sums[1] = sums[1] + (ni * xr - nr * xi)

        first_r, first_i = _scan_chunk(dr_ref, di_ref, gr_ref, gi_ref, ar, -ai, cr, ci, reverse=True, visit=visit)
        dar_ref[...] += jnp.sum(sums[0], axis=0, keepdims=True)
        dai_ref[...] += jnp.sum(sums[1], axis=0, keepdims=True)
        cr_ref[...] = first_r
        ci_ref[...] = first_i

    return pl.pallas_call(
        body, name=name, grid=(S5_BLOCKS, nc),
        in_specs=[chan, op(0, c3.shape[1:]), op(S5_BLOCKS, c3.shape[1:]), blk, blk, par, par], out_specs=[blk, blk, par, par],
        out_shape=[jax.ShapeDtypeStruct((t, S5_WIDTH), F32)] * 2 + [jax.ShapeDtypeStruct((1, S5_WIDTH), F32)] * 2,
        scratch_shapes=[pltpu.VMEM((1, SCAN_LANES), F32)] * 2 + [pltpu.VMEM((rows, SCAN_LANES), F32)] * 2,
        compiler_params=_params(("parallel", "arbitrary")),
    )(dy, c3, c3, st_r, st_i, a_r, a_i)


def _s5_prep(lam_re, lam_im, log_dt, b_re, b_im, c_re, c_im, d_skip):
    dt = jnp.exp(log_dt)[:, None]
    mag = jnp.exp(lam_re * dt)
    ar, ai = mag * jnp.cos(lam_im * dt), mag * jnp.sin(lam_im * dt)
    den = lam_re * lam_re + lam_im * lam_im
    cr = ((ar - 1.0) * lam_re + ai * lam_im) / den
    ci = (ai * lam_re - (ar - 1.0) * lam_im) / den
    bbar_r = cr[..., None] * b_re - ci[..., None] * b_im
    bbar_i = cr[..., None] * b_im + ci[..., None] * b_re
    eye = jnp.eye(S5_PACK, dtype=F32)

    def bdiag(m):
        g, a, b = m.shape
        m4 = m.reshape(g // S5_PACK, S5_PACK, a, b)
        return (eye[None, :, None, :, None] * m4[:, :, :, None, :]).reshape(g // S5_PACK, S5_PACK * a, S5_PACK * b)

    b3 = jnp.concatenate([bdiag(bbar_r.transpose(0, 2, 1)), bdiag(bbar_i.transpose(0, 2, 1))], axis=0)
    c3 = jnp.concatenate([bdiag(c_re.transpose(0, 2, 1)), -bdiag(c_im.transpose(0, 2, 1))], axis=0)
    return b3, c3, d_skip.reshape(1, GROUP_WIDTH), ar.reshape(1, S5_WIDTH), ai.reshape(1, S5_WIDTH)


RET_UNROLL = 4


def _loop_unrolled(trips, body, init):
    factor = math.gcd(trips, RET_UNROLL)

    def several(i, carry):
        for u in range(factor):
            carry = body(i * factor + u, carry)
        return carry

    return lax.fori_loop(0, trips // factor, several, init)


def _ret_consts(lgam):
    c = RET_CHUNK
    r = lax.broadcasted_iota(jnp.int32, (c, c), 0)
    m = lax.broadcasted_iota(jnp.int32, (c, c), 1)
    rel = (r - m).astype(F32)
    decay = jnp.where(rel >= 0, jnp.exp(lgam * jnp.maximum(rel, 0.0)), 0.0)
    idx = lax.broadcasted_iota(jnp.int32, (c, 1), 0).astype(F32)
    zeta = jnp.exp(lgam * (c - 1.0 - idx))
    xi = jnp.exp(lgam * (idx + 1.0))
    return decay, zeta, xi, jnp.exp(lgam * c)


def _ret_specs(t):
    qk = lambda off: pl.BlockSpec((1, t, RET_QK), lambda h: (h + off, 0, 0))
    col = lambda off: pl.BlockSpec((t, RET_V), lambda h: (0, h + off))
    return qk, col


def _ret_fwd(qk, p_ret, lgam, name):
    t = qk.shape[1]
    nck = t // RET_CHUNK
    qk_spec, col = _ret_specs(t)

    def body(lg_ref, q_ref, k_ref, v_ref, g_ref, o_ref, y_ref):
        decay, zeta, xi, gam = _ret_consts(lg_ref[pl.program_id(0)])

        def step(n, state):
            sl = pl.ds(pl.multiple_of(n * RET_CHUNK, RET_CHUNK), RET_CHUNK)
            q, k, v = q_ref[0, sl, :], k_ref[0, sl, :] * (RET_QK ** -0.5), v_ref[sl, :]
            s = _dot(q, k, "nt") * decay
            o = _dot(s, v) + _dot(q, state) * xi
            o_ref[sl, :] = o
            y_ref[sl, :] = _ret_gate_fn(o, g_ref[sl, :]).astype(y_ref.dtype)
            return gam * state + _dot(k, zeta * v, "tn")

        _loop_unrolled(nck, step, jnp.zeros((RET_QK, RET_V), F32))

    return pl.pallas_call(
        body, name=name, grid=(RET_HEADS,), in_specs=[_SMEM, qk_spec(0), qk_spec(RET_HEADS), col(4), col(8)],
        out_specs=[col(0), col(0)],
        out_shape=[jax.ShapeDtypeStruct((t, GROUP_WIDTH), F32), jax.ShapeDtypeStruct((t, GROUP_WIDTH), BF16)],
        compiler_params=_params(("parallel",)),
    )(lgam, qk, qk, p_ret, p_ret)


def _ret_bwd(qk, p_ret, o_all, dy, lgam, name):
    t = qk.shape[1]
    nck = t // RET_CHUNK
    qk_spec, col = _ret_specs(t)
    gate_bwd = _vjp_block(_ret_gate_fn, 2)

    def body(lg_ref, q_ref, k_ref, v_ref, g_ref, o_ref, dy_ref, dq_ref, dk_ref, dv_ref, dg_ref, st_ref):
        decay, zeta, xi, gam = _ret_consts(lg_ref[pl.program_id(0)])
        scale = RET_QK ** -0.5

        def fstep(n, state):
            sl = pl.ds(pl.multiple_of(n * RET_CHUNK, RET_CHUNK), RET_CHUNK)
            st_ref[n] = state
            return gam * state + _dot(k_ref[0, sl, :] * scale, zeta * v_ref[sl, :], "tn")

        _loop_unrolled(nck, fstep, jnp.zeros((RET_QK, RET_V), F32))

        def bstep(r, grad_state):
            n = nck - 1 - r
            sl = pl.ds(pl.multiple_of(n * RET_CHUNK, RET_CHUNK), RET_CHUNK)
            q, k, v = q_ref[0, sl, :], k_ref[0, sl, :] * scale, v_ref[sl, :]
            d_o, dg = gate_bwd(o_ref[sl, :], g_ref[sl, :], dy_ref[sl, :])
            dg_ref[sl, :] = dg.astype(dg_ref.dtype)
            s = _dot(q, k, "nt") * decay
            ds = _dot(d_o, v, "nt") * decay
            xdo = xi * d_o
            dq_ref[0, sl, :] = _dot(ds, k) + _dot(xdo, st_ref[n], "nt")
            dk_ref[0, sl, :] = (_dot(ds, q, "tn") + _dot(zeta * v, grad_state, "nt")) * scale
            dv_ref[sl, :] = (_dot(s, d_o, "tn") + zeta * _dot(k, grad_state)).astype(dv_ref.dtype)
            return gam * grad_state + _dot(q, xdo, "tn")

        _loop_unrolled(nck, bstep, jnp.zeros((RET_QK, RET_V), F32))

    hd = pl.BlockSpec((1, t, RET_QK), lambda h: (h, 0, 0))
    return pl.pallas_call(
        body, name=name, grid=(RET_HEADS,),
        in_specs=[_SMEM, qk_spec(0), qk_spec(RET_HEADS), col(4), col(8), col(0), col(0)],
        out_specs=[hd, hd, col(0), col(0)],
        out_shape=[jax.ShapeDtypeStruct((RET_HEADS, t, RET_QK), F32)] * 2 + [jax.ShapeDtypeStruct((t, GROUP_WIDTH), BF16)] * 2,
        scratch_shapes=[pltpu.VMEM((nck, RET_QK, RET_V), F32)], compiler_params=_params(("parallel",)),
    )(lgam, qk, qk, p_ret, p_ret, o_all, dy)


SWA_GROUP = SWA_HEADS // SWA_KV_HEADS
SWA_SCALE = SWA_HD ** -0.5


def _swa_mask(n):
    rows = SWA_GROUP * WINDOW
    r = lax.broadcasted_iota(jnp.int32, (rows, 2 * WINDOW), 0) & (WINDOW - 1)
    j = lax.broadcasted_iota(jnp.int32, (rows, 2 * WINDOW), 1)
    dist = r + WINDOW - j
    return (dist >= 0) & (dist < WINDOW) & (n * WINDOW + j - WINDOW >= 0)


def _swa_sink_rows(sink_ref, kv):
    row = lax.broadcasted_iota(jnp.int32, (SWA_GROUP * WINDOW, 1), 0)
    sink = jnp.zeros((SWA_GROUP * WINDOW, 1), F32)
    for g in range(SWA_GROUP):
        sink = jnp.where(row >= g * WINDOW, sink_ref[kv * SWA_GROUP + g], sink)
    return sink


def _swa_pad_keys(n, k_ref, v_ref, kp_ref, vp_ref):
    @pl.when(n == 0)
    def _():
        zero = jnp.zeros((WINDOW, SWA_HD), F32)
        kp_ref[0:WINDOW, :] = zero
        vp_ref[0:WINDOW, :] = zero
        kp_ref[WINDOW:, :] = k_ref[0]
        vp_ref[WINDOW:, :] = v_ref[0]


SWA_STEP_BLOCKS = 2


def _swa_specs(t):
    per_step = math.gcd(t // WINDOW, SWA_STEP_BLOCKS)
    blk = lambda w: pl.BlockSpec((SWA_GROUP, per_step * WINDOW, w), lambda kv, n: (kv, n, 0))
    kspec = lambda off: pl.BlockSpec((1, t, SWA_HD), lambda kv, n: (SWA_HEADS + off + kv, 0, 0))
    return blk, kspec, per_step


def _swa_fwd(qkv, sinks, name):
    t = qkv.shape[1]
    rows = SWA_GROUP * WINDOW
    blk, kspec, per_step = _swa_specs(t)

    def body(sink_ref, q_ref, k_ref, v_ref, o_ref, lse_ref, kp_ref, vp_ref):
        kv, step = pl.program_id(0), pl.program_id(1)
        _swa_pad_keys(step, k_ref, v_ref, kp_ref, vp_ref)
        sink = _swa_sink_rows(sink_ref, kv)
        for u in range(per_step):
            n = step * per_step + u
            here = slice(u * WINDOW, (u + 1) * WINDOW)
            win = pl.ds(pl.multiple_of(n * WINDOW, WINDOW), 2 * WINDOW)
            s = _dot(q_ref[:, here, :].reshape(rows, SWA_HD), kp_ref[win, :], "nt") * SWA_SCALE
            s = jnp.where(_swa_mask(n), s, NEG)
            m = jnp.maximum(jnp.max(s, axis=-1, keepdims=True), sink)
            p = jnp.exp(s - m)
            den = jnp.sum(p, axis=-1, keepdims=True) + jnp.exp(sink - m)
            o_ref[:, here, :] = _dot(p / den, vp_ref[win, :]).reshape(SWA_GROUP, WINDOW, SWA_HD)
            lse_ref[:, here, :] = (m + jnp.log(den)).reshape(SWA_GROUP, WINDOW, 1)

    return pl.pallas_call(
        body, name=name, grid=(SWA_KV_HEADS, t // WINDOW // per_step),
        in_specs=[_SMEM, blk(SWA_HD), kspec(0), kspec(SWA_KV_HEADS)], out_specs=[blk(SWA_HD), blk(1)],
        out_shape=[jax.ShapeDtypeStruct((SWA_HEADS, t, SWA_HD), F32), jax.ShapeDtypeStruct((SWA_HEADS, t, 1), F32)],
        scratch_shapes=[pltpu.VMEM((t + WINDOW, SWA_HD), F32)] * 2, compiler_params=_params(("parallel", "arbitrary")),
    )(sinks, qkv, qkv, qkv)


def _swa_bwd(qkv, o, lse, d_o, sinks, name):
    t = qkv.shape[1]
    nb = t // WINDOW
    rows = SWA_GROUP * WINDOW
    blk, kspec, per_step = _swa_specs(t)

    def body(sink_ref, q_ref, k_ref, v_ref, o_ref, lse_ref, do_ref, dq_ref, dk_ref, dv_ref, dsink_ref,
             kp_ref, vp_ref, dkp_ref, dvp_ref):
        kv, step = pl.program_id(0), pl.program_id(1)
        _swa_pad_keys(step, k_ref, v_ref, kp_ref, vp_ref)

        @pl.when(step == 0)
        def _():
            dkp_ref[...] = jnp.zeros_like(dkp_ref)
            dvp_ref[...] = jnp.zeros_like(dvp_ref)
            dsink_ref[...] = jnp.zeros_like(dsink_ref)

        sink = _swa_sink_rows(sink_ref, kv)
        head = lax.broadcasted_iota(jnp.int32, (SWA_GROUP, 128), 0)
        acc = jnp.zeros((SWA_GROUP, 128), F32)
        for u in range(per_step):
            n = step * per_step + u
            here = slice(u * WINDOW, (u + 1) * WINDOW)
            win = pl.ds(pl.multiple_of(n * WINDOW, WINDOW), 2 * WINDOW)
            q, dout = q_ref[:, here, :].reshape(rows, SWA_HD), do_ref[:, here, :].reshape(rows, SWA_HD)
            lse_n = lse_ref[:, here, :].reshape(rows, 1)
            s = _dot(q, kp_ref[win, :], "nt") * SWA_SCALE
            s = jnp.where(_swa_mask(n), s, NEG)
            p = jnp.exp(s - lse_n)
            delta = jnp.sum(dout * o_ref[:, here, :].reshape(rows, SWA_HD), axis=-1, keepdims=True)
            ds = p * (_dot(dout, vp_ref[win, :], "nt") - delta)
            dq_ref[:, here, :] = (_dot(ds, kp_ref[win, :]) * SWA_SCALE).reshape(SWA_GROUP, WINDOW, SWA_HD).astype(dq_ref.dtype)
            dkp_ref[win, :] += _dot(ds, q, "tn") * SWA_SCALE
            dvp_ref[win, :] += _dot(p, dout, "tn")
            term = jnp.exp(sink - lse_n) * delta
            for g in range(SWA_GROUP):
                acc = jnp.where(head == g, acc + jnp.sum(term[g * WINDOW:(g + 1) * WINDOW], axis=0, keepdims=True), acc)
        dsink_ref[0] -= acc

        @pl.when(step == nb // per_step - 1)
        def _():
            dk_ref[0] = dkp_ref[WINDOW:, :].astype(dk_ref.dtype)
            dv_ref[0] = dvp_ref[WINDOW:, :].astype(dv_ref.dtype)

    kout = pl.BlockSpec((1, t, SWA_HD), lambda kv, n: (kv, 0, 0))
    dq, dk, dv, dsink = pl.pallas_call(
        body, name=name, grid=(SWA_KV_HEADS, nb // per_step),
        in_specs=[_SMEM, blk(SWA_HD), kspec(0), kspec(SWA_KV_HEADS), blk(SWA_HD), blk(1), blk(SWA_HD)],
        out_specs=[blk(SWA_HD), kout, kout, pl.BlockSpec((1, SWA_GROUP, 128), lambda kv, n: (kv, 0, 0))],
        out_shape=[jax.ShapeDtypeStruct((SWA_HEADS, t, SWA_HD), BF16), jax.ShapeDtypeStruct((SWA_KV_HEADS, t, SWA_HD), BF16),
                   jax.ShapeDtypeStruct((SWA_KV_HEADS, t, SWA_HD), BF16), jax.ShapeDtypeStruct((SWA_KV_HEADS, SWA_GROUP, 128), F32)],
        scratch_shapes=[pltpu.VMEM((t + WINDOW, SWA_HD), F32)] * 4, compiler_params=_params(("parallel", "arbitrary")),
    )(sinks, qkv, qkv, qkv, o, lse, d_o)
    return jnp.concatenate([dq, dk, dv], axis=0), dsink[:, :, 0].reshape(SWA_HEADS)


MLA_SCALE = (MLA_NOPE + MLA_ROPE) ** -0.5
MLA_TILE = 512
MLA_KEY_TILE = 512
MLA_BWD_TILE = 512


def _mla_diag(s):
    r = lax.broadcasted_iota(jnp.int32, s.shape, 0)
    c = lax.broadcasted_iota(jnp.int32, s.shape, 1)
    return jnp.where(c <= r, s, NEG)


def _mla_specs(t, tile):
    whole = lambda w, off: pl.BlockSpec((t, w), lambda h, i: (0, 2 * h + off))
    head = lambda w: pl.BlockSpec((1, t, w), lambda h, i: (h, 0, 0))
    key_rope = pl.BlockSpec((1, t, MLA_ROPE), lambda h, i: (MLA_HEADS, 0, 0))
    tile_of = lambda w: pl.BlockSpec((1, tile, w), lambda h, i: (h, i, 0))
    return whole, head, key_rope, tile_of


def _mla_attend(qn, rot, kv, name):
    t = qn.shape[1]
    tile = min(MLA_TILE, t)
    ktile = min(MLA_KEY_TILE, t)
    ratio = ktile // tile
    whole, head, key_rope, tile_of = _mla_specs(t, tile)

    def body(qn_ref, qr_ref, kn_ref, kr_ref, v_ref, o_ref, lse_ref, m_ref, l_ref, acc_ref):
        i = pl.program_id(1)
        qn_b, qr_b = qn_ref[0], qr_ref[0]

        def rows(j):
            return pl.ds(pl.multiple_of(j * ktile, ktile), ktile)

        def scores(j):
            return (_dot(qn_b, kn_ref[rows(j), :], "nt") + _dot(qr_b, kr_ref[0, rows(j), :], "nt")) * MLA_SCALE

        def causal(s, j):
            qpos = i * tile + lax.broadcasted_iota(jnp.int32, s.shape, 0)
            kpos = j * ktile + lax.broadcasted_iota(jnp.int32, s.shape, 1)
            return jnp.where(kpos <= qpos, s, NEG)

        def update(s, j):
            m_old = m_ref[...]
            m_new = jnp.maximum(m_old, jnp.max(s, axis=-1, keepdims=True))
            alpha = jnp.exp(m_old - m_new)
            p = jnp.exp(s - m_new)
            l_ref[...] = alpha * l_ref[...] + jnp.sum(p, axis=-1, keepdims=True)
            acc_ref[...] = alpha * acc_ref[...] + _dot(p, v_ref[rows(j), :])
            m_ref[...] = m_new

        m_ref[...] = jnp.full_like(m_ref, NEG)
        l_ref[...] = jnp.zeros_like(l_ref)
        acc_ref[...] = jnp.zeros_like(acc_ref)

        def step(j, s_cur):
            s_next = scores(j + 1)
            update(s_cur, j)
            return s_next

        last = i // ratio
        s_last = lax.fori_loop(0, last, step, scores(0))
        update(causal(s_last, last), last)
        o_ref[...] = acc_ref[...] / l_ref[...]
        lse_ref[0] = m_ref[...] + jnp.log(l_ref[...])

    return pl.pallas_call(
        body, name=name, grid=(MLA_HEADS, t // tile),
        in_specs=[tile_of(MLA_NOPE), tile_of(MLA_ROPE), whole(MLA_NOPE, 0), key_rope, whole(MLA_V, 1)],
        out_specs=[pl.BlockSpec((tile, MLA_V), lambda h, i: (i, h)), tile_of(1)],
        out_shape=[jax.ShapeDtypeStruct((t, GROUP_WIDTH), F32), jax.ShapeDtypeStruct((MLA_HEADS, t, 1), F32)],
        scratch_shapes=[pltpu.VMEM((tile, 1), F32), pltpu.VMEM((tile, 1), F32), pltpu.VMEM((tile, MLA_V), F32)],
        compiler_params=_params(("parallel", "parallel")),
    )(qn, rot, kv, rot, kv)


def _mla_delta(o, d_o, name):
    t = o.shape[0]
    tt = min(ROWS, t)

    def body(o_ref, do_ref, d_ref):
        d_ref[0] = jnp.sum(o_ref[...] * do_ref[...], axis=-1, keepdims=True)

    blk = pl.BlockSpec((tt, MLA_V), lambda h, i: (i, h))
    return pl.pallas_call(
        body, name=name, grid=(MLA_HEADS, t // tt), in_specs=[blk, blk],
        out_specs=pl.BlockSpec((1, tt, 1), lambda h, i: (h, i, 0)), out_shape=jax.ShapeDtypeStruct((MLA_HEADS, t, 1), F32),
        compiler_params=_params(("parallel", "parallel")),
    )(o, d_o)


def _mla_attend_bwd(qn, rot, kv, lse, delta, d_o, name):
    t = qn.shape[1]
    tile = min(MLA_BWD_TILE, t)
    nt = t // tile
    whole, head, key_rope, tile_of = _mla_specs(t, tile)

    def body(qn_ref, qr_ref, kn_ref, kr_ref, v_ref, lse_ref, dl_ref, do_ref, dqn_ref, dqr_ref, dkn_ref, dv_ref, dkr_ref):
        j = pl.program_id(1)

        @pl.when(j == 0)
        def _():
            dqn_ref[...] = jnp.zeros_like(dqn_ref)
            dqr_ref[...] = jnp.zeros_like(dqr_ref)

        dkn_ref[...] = jnp.zeros_like(dkn_ref)
        dv_ref[...] = jnp.zeros_like(dv_ref)
        dkr_ref[...] = jnp.zeros_like(dkr_ref)
        kn_b, kr_b, v_b = kn_ref[...], kr_ref[0], v_ref[...]

        def block(i, diagonal):
            sl = pl.ds(pl.multiple_of(i * tile, tile), tile)
            qn_b, qr_b, dout = qn_ref[0, sl, :], qr_ref[0, sl, :], do_ref[sl, :]
            s = (_dot(qn_b, kn_b, "nt") + _dot(qr_b, kr_b, "nt")) * MLA_SCALE
            if diagonal:
                s = _mla_diag(s)
            p = jnp.exp(s - lse_ref[0, sl, :])
            ds = p * (_dot(dout, v_b, "nt") - dl_ref[0, sl, :]) * MLA_SCALE
            dv_ref[...] += _dot(p, dout, "tn")
            dkn_ref[...] += _dot(ds, qn_b, "tn")
            dkr_ref[0] += _dot(ds, qr_b, "tn")
            dqn_ref[0, sl, :] += _dot(ds, kn_b)
            dqr_ref[0, sl, :] += _dot(ds, kr_b)

        block(j, True)

        def step(i, carry):
            block(i, False)
            return carry

        lax.fori_loop(j + 1, nt, step, 0)

    key_tile = lambda w, off: pl.BlockSpec((tile, w), lambda h, j: (j, 2 * h + off))
    out_tile = pl.BlockSpec((tile, MLA_V), lambda h, j: (j, h))
    return pl.pallas_call(
        body, name=name, grid=(MLA_HEADS, nt),
        in_specs=[head(MLA_NOPE), head(MLA_ROPE), key_tile(MLA_NOPE, 0), pl.BlockSpec((1, tile, MLA_ROPE), lambda h, j: (MLA_HEADS, j, 0)),
                  key_tile(MLA_V, 1), head(1), head(1), pl.BlockSpec((t, MLA_V), lambda h, j: (0, h))],
        out_specs=[head(MLA_NOPE), head(MLA_ROPE), out_tile, out_tile, tile_of(MLA_ROPE)],
        out_shape=[jax.ShapeDtypeStruct((MLA_HEADS, t, MLA_NOPE), F32), jax.ShapeDtypeStruct((MLA_HEADS, t, MLA_ROPE), F32),
                   jax.ShapeDtypeStruct((t, MLA_HEADS * MLA_NOPE), F32), jax.ShapeDtypeStruct((t, MLA_HEADS * MLA_V), F32),
                   jax.ShapeDtypeStruct((MLA_HEADS, t, MLA_ROPE), F32)],
        compiler_params=_params(("parallel", "arbitrary")),
    )(qn, rot, kv, rot, kv, lse, delta, d_o)


def _place():
    return lax.axis_index("x"), lax.axis_index("y"), lax.axis_index("c")


def _all_gather(arrs, name):
    n = len(arrs)

    def body(*refs):
        x_refs, o_refs = refs[:n], refs[n:2 * n]
        send_sems, recv_sems, local_sems = refs[2 * n:]
        x, y, c = _place()
        me, sibling = (x, y, c), (x, y, 1 - c)
        chips = [(1 - x, y), (x, 1 - y), (1 - x, 1 - y)]

        def slot(a, p):
            return o_refs[a].at[4 * p[0] + 2 * p[1] + p[2]]

        def copy(a, k, block, to, src=None):
            return pltpu.make_async_remote_copy(
                src_ref=slot(a, block) if src is None else src, dst_ref=slot(a, block),
                send_sem=send_sems.at[a, k], recv_sem=recv_sems.at[a, k], device_id=to, device_id_type=MESH_ID)

        mine = [pltpu.make_async_copy(x_refs[a], slot(a, me), local_sems.at[a]) for a in range(n)]
        for cp in mine:
            cp.start()
        first = []
        for a in range(n):
            first.append(copy(a, 0, me, sibling, src=x_refs[a]))
            first += [copy(a, 1 + j, me, (*chip, c), src=x_refs[a]) for j, chip in enumerate(chips)]
        for cp in first:
            cp.start()
        passed = []
        for j, chip in enumerate(chips):
            for a in range(n):
                copy(a, 1 + j, (*chip, c), me).wait_recv()
                cp = copy(a, 4 + j, (*chip, c), sibling)
                cp.start()
                passed.append(cp)
        for a in range(n):
            copy(a, 0, sibling, me).wait_recv()
            for j, chip in enumerate(chips):
                copy(a, 4 + j, (*chip, 1 - c), me).wait_recv()
        for cp in first + passed:
            cp.wait_send()
        for cp in mine:
            cp.wait()

    return pl.pallas_call(
        body, name=name, in_specs=[_ANY] * n, out_specs=[_ANY] * n,
        out_shape=[jax.ShapeDtypeStruct((N_DEV,) + a.shape, a.dtype) for a in arrs],
        scratch_shapes=[pltpu.SemaphoreType.DMA((n, 7)), pltpu.SemaphoreType.DMA((n, 7)), pltpu.SemaphoreType.DMA((n,))],
    )(*arrs)


def _pass_to_sibling(arrs, name):
    n = len(arrs)

    def body(*refs):
        a_refs, o_refs = refs[:n], refs[n:2 * n]
        send_sems, recv_sems = refs[2 * n:]
        x, y, c = _place()
        chips = [(1 - x, y), (x, 1 - y), (1 - x, 1 - y)]
        slot = lambda px, py, pc: 4 * px + 2 * py + pc
        sends, recvs = [], []
        for a in range(n):
            for j, (px, py) in enumerate(chips):
                sends.append(pltpu.make_async_remote_copy(
                    src_ref=a_refs[a].at[slot(px, py, c)], dst_ref=o_refs[a].at[slot(px, py, c)], send_sem=send_sems.at[a, j],
                    recv_sem=recv_sems.at[a, j], device_id=(x, y, 1 - c), device_id_type=MESH_ID))
                recvs.append(pltpu.make_async_remote_copy(
                    src_ref=a_refs[a].at[slot(px, py, c)], dst_ref=o_refs[a].at[slot(px, py, 1 - c)], send_sem=send_sems.at[a, j],
                    recv_sem=recv_sems.at[a, j], device_id=(x, y, 1 - c), device_id_type=MESH_ID))
        for cp in sends:
            cp.start()
        for cp in sends:
            cp.wait_send()
        for cp in recvs:
            cp.wait_recv()

    return pl.pallas_call(
        body, name=name, in_specs=[_ANY] * n, out_specs=[_ANY] * n,
        out_shape=[jax.ShapeDtypeStruct(a.shape, a.dtype) for a in arrs], input_output_aliases={i: i for i in range(n)},
        scratch_shapes=[pltpu.SemaphoreType.DMA((n, 3)), pltpu.SemaphoreType.DMA((n, 3))],
    )(*arrs)


def _scatter_core(grads, name):
    n = len(grads)

    def body(*refs):
        g_refs, got_refs = refs[:n], refs[n:2 * n]
        send_sems, recv_sems = refs[2 * n:]
        x, y, c = _place()
        sends = [pltpu.make_async_remote_copy(
            src_ref=g_refs[a].at[2 * q + 1 - c], dst_ref=got_refs[a].at[q], send_sem=send_sems.at[a, q],
            recv_sem=recv_sems.at[a, q], device_id=(x, y, 1 - c), device_id_type=MESH_ID) for a in range(n) for q in range(4)]
        for cp in sends:
            cp.start()
        for cp in sends:
            cp.wait()

    return pl.pallas_call(
        body, name=name, in_specs=[_ANY] * n, out_specs=[_ANY] * n,
        out_shape=[jax.ShapeDtypeStruct((4,) + g.shape[1:], g.dtype) for g in grads],
        scratch_shapes=[pltpu.SemaphoreType.DMA((n, 4)), pltpu.SemaphoreType.DMA((n, 4))],
    )(*grads)


def _scatter_chips(parts, name):
    n = len(parts)

    def body(*refs):
        p_refs, o_refs = refs[:n], refs[n:2 * n]
        send_sems, recv_sems = refs[2 * n:]
        x, y, c = _place()
        chips = [(1 - x, y), (x, 1 - y), (1 - x, 1 - y)]
        sends = [pltpu.make_async_remote_copy(
            src_ref=p_refs[a].at[2 * px + py], dst_ref=o_refs[a].at[j], send_sem=send_sems.at[a, j],
            recv_sem=recv_sems.at[a, j], device_id=(px, py, c), device_id_type=MESH_ID)
            for a in range(n) for j, (px, py) in enumerate(chips)]
        for cp in sends:
            cp.start()
        for cp in sends:
            cp.wait()

    return pl.pallas_call(
        body, name=name, in_specs=[_ANY] * n, out_specs=[_ANY] * n,
        out_shape=[jax.ShapeDtypeStruct((3,) + p.shape[1:], p.dtype) for p in parts],
        scratch_shapes=[pltpu.SemaphoreType.DMA((n, 3)), pltpu.SemaphoreType.DMA((n, 3))],
    )(*parts)


GATHER_PEERS = [(0, 0, 1), (1, 0, 0), (0, 1, 0), (1, 1, 0)]
CHIP_PEERS = [(1, 0, 0), (0, 1, 0), (1, 1, 0)]
_HBM = pl.BlockSpec(memory_space=pltpu.HBM)
_SEM = pl.BlockSpec(memory_space=pltpu.SEMAPHORE)
_EFFECT = pltpu.SideEffectType.DATAFLOW_SIDE_EFFECTING


def _push_copies(src_refs, land_refs, send_sems, recv_sems, peers, src_of, slot_of):
    place = _place()
    flip = lambda v, f: 1 - v if f else v
    return [pltpu.make_async_remote_copy(
        src_ref=src_of(src_refs[a], k), dst_ref=land_refs[a].at[slot_of(k)], send_sem=send_sems[a], recv_sem=recv_sems[a],
        device_id=tuple(flip(v, f) for v, f in zip(place, peer)), device_id_type=MESH_ID)
        for a in range(len(src_refs)) for k, peer in enumerate(peers)]


def _push_start(srcs, land_shapes, peers, src_of, slot_of, name):
    n = len(srcs)

    def body(*refs):
        src_refs, land_refs = refs[:n], refs[n:2 * n]
        send_sems, recv_sems = refs[2 * n:3 * n], refs[3 * n:4 * n]
        token = refs[-1]
        for cp in _push_copies(src_refs, land_refs, send_sems, recv_sems, peers, src_of, slot_of):
            cp.start()
        token[...] = jnp.zeros_like(token)

    sems = [pltpu.SemaphoreType.DMA(())] * (2 * n)
    lands = [pltpu.with_memory_space_constraint(lax.empty(s.shape, s.dtype), pltpu.HBM) for s in land_shapes]
    res = pl.pallas_call(
        body, name=name, in_specs=[_HBM] * (2 * n), out_specs=[_SEM] * (2 * n) + [_HBM] * (2 * n) + [pl.BlockSpec(memory_space=pltpu.VMEM)],
        out_shape=sems + [pltpu.HBM(s.shape, s.dtype) for s in srcs] + [pltpu.HBM(s.shape, s.dtype) for s in land_shapes]
        + [jax.ShapeDtypeStruct((8, 128), F32)],
        input_output_aliases={i: 2 * n + i for i in range(2 * n)},
        compiler_params=pltpu.CompilerParams(has_side_effects=_EFFECT),
    )(*[pltpu.with_memory_space_constraint(s, pltpu.HBM) for s in srcs], *lands)
    return list(res[:n]), list(res[n:2 * n]), list(res[2 * n:3 * n]), list(res[3 * n:4 * n]), res[-1]


def _push_wait(send_sems, recv_sems, srcs, lands, after, peers, src_of, slot_of, name):
    n = len(srcs)

    def body(*refs):
        src_refs, land_refs = refs[:n], refs[n:2 * n]
        s_sems, r_sems = refs[2 * n:3 * n], refs[3 * n:4 * n]
        copies = _push_copies(src_refs, land_refs, s_sems, r_sems, peers, src_of, slot_of)
        for cp in copies:
            cp.wait_send()
        for cp in copies:
            cp.wait_recv()

    res = pl.pallas_call(
        body, name=name, in_specs=[_HBM] * (2 * n) + [_SEM] * (2 * n) + [_ANY], out_specs=[_HBM] * (2 * n),
        out_shape=[pltpu.HBM(s.shape, s.dtype) for s in srcs] + [pltpu.HBM(s.shape, s.dtype) for s in lands],
        input_output_aliases={i: i for i in range(2 * n)},
        compiler_params=pltpu.CompilerParams(has_side_effects=_EFFECT),
    )(*srcs, *lands, *send_sems, *recv_sems, after)
    return list(res[:n]), list(res[n:])


def _pick_sum(picked, rest, index, pick_of, out_dtype, name):
    nq, r, cdim = rest.shape
    one = nq == 3
    tr = _row_tile(r, 512, 16)
    tc = 512 if cdim % 512 == 0 else cdim
    grid = (1 if one else nq, r // tr, cdim // tc)

    def body(i_ref, p_ref, r_ref, o_ref):
        acc = p_ref[0].astype(F32)
        if one:
            for j in range(3):
                acc = acc + r_ref[j].astype(F32)
            o_ref[...] = acc.astype(out_dtype)
        else:
            o_ref[0] = (acc + r_ref[0].astype(F32)).astype(out_dtype)

    spec = pltpu.PrefetchScalarGridSpec(
        num_scalar_prefetch=1, grid=grid,
        in_specs=[pl.BlockSpec((1, tr, tc), lambda q, i, j, i_ref: (pick_of(q, i_ref[0]), i, j)),
                  pl.BlockSpec((3, tr, tc), lambda q, i, j, i_ref: (0, i, j)) if one else pl.BlockSpec((1, tr, tc), lambda q, i, j, i_ref: (q, i, j))],
        out_specs=pl.BlockSpec((tr, tc), lambda q, i, j, i_ref: (i, j)) if one else pl.BlockSpec((1, tr, tc), lambda q, i, j, i_ref: (q, i, j)))
    return pl.pallas_call(
        body, name=name, grid_spec=spec,
        out_shape=jax.ShapeDtypeStruct((r, cdim) if one else (nq, r, cdim), out_dtype),
        compiler_params=_params(("parallel", "parallel", "parallel")),
    )(index.astype(jnp.int32).reshape(1), picked, rest)


def _adamw_fn(w, g, m, v):
    m = ADAM_B1 * m + (1.0 - ADAM_B1) * g
    v = ADAM_B2 * v + (1.0 - ADAM_B2) * jnp.square(g)
    m_hat = m / (1.0 - ADAM_B1 ** ADAM_STEP)
    v_hat = v / (1.0 - ADAM_B2 ** ADAM_STEP)
    delta = -ADAM_LR * (m_hat / (jnp.sqrt(v_hat) + ADAM_EPS) + ADAM_WD * w)
    return delta, m, v


def _as2d(a):
    return a.reshape(-1, a.shape[-1])


def _adamw_shard(w, g, m, v, name):
    shape = w.shape
    ins = [_as2d(a) for a in (w, g, m, v)]
    cols = ins[0].shape[1]
    outs = _ew(_adamw_fn, [(a, True) for a in ins], [(cols, F32, "tile")] * 3, 256, name)
    return [o.reshape(shape) for o in outs]


def _adamw_small(ws, gs, ms, vs, name):
    shapes = [w.shape for w in ws]
    flat = lambda a: a.reshape(-1, 128) if a.size % 128 == 0 else a.reshape(1, -1)
    ins = [flat(a) for grp in zip(ws, gs, ms, vs) for a in grp]
    k = len(ws)

    def fn(*vals):
        out = []
        for i in range(k):
            out += list(_adamw_fn(*vals[4 * i:4 * i + 4]))
        return out

    outs = _whole(fn, ins, [(ins[4 * (i // 3)].shape, F32) for i in range(3 * k)], name)
    deltas = [outs[3 * i].reshape(shapes[i]) for i in range(k)]
    new_m = [outs[3 * i + 1].reshape(shapes[i]) for i in range(k)]
    new_v = [outs[3 * i + 2].reshape(shapes[i]) for i in range(k)]
    return deltas, new_m, new_v


def _sum8(stacked, name):
    def fn(a):
        s = a[0:1]
        for i in range(1, N_DEV):
            s = s + a[i:i + 1]
        return s
    w = stacked.shape[1]
    tw = 8192
    if w % tw:
        return _whole(fn, [stacked], [((1, w), F32)], name)[0]

    def body(a_ref, o_ref):
        o_ref[...] = fn(a_ref[...])

    return pl.pallas_call(body, name=name, grid=(w // tw,), in_specs=[pl.BlockSpec((N_DEV, tw), lambda i: (0, i))],
                          out_specs=pl.BlockSpec((1, tw), lambda i: (0, i)), out_shape=jax.ShapeDtypeStruct((1, w), F32))(stacked)


ROWS = 512


def _split_heads(p, nh):
    t = p.shape[0]
    return p.reshape(t, nh, p.shape[1] // nh).transpose(1, 0, 2)


def _merge_heads(p):
    nh, t, d = p.shape
    return p.transpose(1, 0, 2).reshape(t, nh * d)


def _layer_fwd(h, mod, w, small, rope, l, late=None):
    sh1, sc1, gt1, sh2, sc2, gt2 = mod
    rope_ret, rope_mla = rope
    t = h.shape[0]
    nm = lambda s: f"l{l}_{s}"
    a1 = _ew(_norm_mod_fn, [(h, True), (small["norm1_g"], False), (sc1, False), (sh1, False)], [(D_MODEL, BF16, "tile")], ROWS, nm("norm1"))[0]
    p_s5 = _mm(a1, w["w_in_t"], "nt", 512, 512, 2048, name=nm("proj_s5"), n=512)
    p_ret = _mm(a1, w["w_in_t"], "nt", 512, 512, 2048, name=nm("proj_ret"), b_off=1, n=1536)
    p_swa = _mm(a1, w["w_in_t"], "nt", 512, 256, 2048, name=nm("proj_swa"), b_off=8, n=768)
    p_mla = _mm(a1, w["w_in_t"][2816:], "nt", 512, 576, 2048, name=nm("proj_mla"))
    b3, c3, dskip, a_r, a_i = small["s5"]
    st_r, st_i = _s5_scan_fwd(p_s5, b3, a_r, a_i, nm("s5_scan"))
    ypre = _mm_blocks(st_r, c3[:S5_BLOCKS], "nn", 512, pair=(st_i, c3[S5_BLOCKS:]), name=nm("s5_y"))
    z = _ew(_s5_act_fn, [(ypre, True), (p_s5, True), (dskip, False)], [(GROUP_WIDTH, F32, "tile")], ROWS, nm("s5_act"))[0]
    zz = _mm(z, w["glu_w"], "nn", 512, 512, 512, name=nm("s5_zz"))
    y_s5 = _ew(_s5_glu_fn, [(z, True), (zz, True), (small["s5_glu_b"], False)], [(GROUP_WIDTH, BF16, "tile")], ROWS, nm("s5_glu"))[0]
    qk_ret = _split_heads(_rope(p_ret[:, :2 * RET_HEADS * RET_QK], rope_ret, nm("ret_rope")), 2 * RET_HEADS)
    o_ret, y_ret = _ret_fwd(qk_ret, p_ret, small["ret_lgam"], nm("ret"))
    qkv_swa = _split_heads(p_swa, 12)
    o_swa, lse_swa = _swa_fwd(qkv_swa, small["swa_sinks"], nm("swa"))
    y_swa = _merge_heads(o_swa).astype(BF16)
    cq, ckv, kr = p_mla[:, :MLA_Q_RANK], p_mla[:, MLA_Q_RANK:MLA_Q_RANK + MLA_KV_RANK], p_mla[:, MLA_Q_RANK + MLA_KV_RANK:]
    cqn = _ew(_rms_gain_fn, [(cq, True), (small["mla_q_norm"], False)], [(MLA_Q_RANK, BF16, "tile")], ROWS, nm("mla_qnorm"))[0]
    ckvn = _ew(_rms_gain_fn, [(ckv, True), (small["mla_kv_norm"], False)], [(MLA_KV_RANK, BF16, "tile")], ROWS, nm("mla_kvnorm"))[0]
    q_full = _mm(cqn, w["w_uq_t"], "nt", 512, 768, 384, name=nm("mla_q"))
    kv_full = _mm(ckvn, w["w_ukv_t"], "nt", 512, 1024, 128, BF16, name=nm("mla_kv"))
    nq = q_full.shape[1]
    roped = _rope(jnp.concatenate([q_full, kr, jnp.zeros_like(kr)], axis=1), rope_mla, nm("mla_rope"), out_dtype=BF16)
    q4 = roped[:, :nq].reshape(t, MLA_HEADS, MLA_NOPE + MLA_ROPE)
    qn = q4[:, :, :MLA_NOPE].transpose(1, 0, 2)
    rot = jnp.concatenate([q4[:, :, MLA_NOPE:].transpose(1, 0, 2), roped[None, :, nq:nq + MLA_ROPE]], axis=0)
    o_mla, lse_mla = _mla_attend(qn, rot, kv_full, nm("mla"))
    cat = jnp.concatenate([y_s5, y_ret, y_swa, o_mla.astype(BF16)], axis=1)
    if late is not None:
        w = {**w, **late(lse_mla)}
    mixed = _mm(cat, w["w_out"], "nn", 512, 1024, 2048, name=nm("out_proj"))
    h1 = _ew(_gate_add_fn, [(h, True), (mixed, True), (gt1, False)], [(D_MODEL, F32, "tile")], ROWS, nm("res1"))[0]
    a2 = _ew(_norm_mod_fn, [(h1, True), (small["norm2_g"], False), (sc2, False), (sh2, False)], [(D_MODEL, BF16, "tile")], ROWS, nm("norm2"))[0]
    hid, act = _mm(a2, w["w1_t"], "nt", 1024, 1024, 2048, name=nm("mlp1"), epi=lambda acc: (acc, _relu2_fn(acc)), epi_outs=[F32, BF16])
    mo = _mm(act, w["w2"], "nn", 1024, 1024, 2048, name=nm("mlp2"))
    h2 = _ew(_gate_add_fn, [(h1, True), (mo, True), (gt2, False)], [(D_MODEL, F32, "tile")], ROWS, nm("res2"))[0]
    saved = dict(w=w, h=h, a1=a1, p_s5=p_s5, p_ret=p_ret, st_r=st_r, st_i=st_i, ypre=ypre, z=z, zz=zz, qk_ret=qk_ret, o_ret=o_ret,
                 qkv_swa=qkv_swa, o_swa=o_swa, lse_swa=lse_swa, cq=cq, ckv=ckv, cqn=cqn, ckvn=ckvn, qn=qn, rot=rot,
                 kv_full=kv_full, o_mla=o_mla, lse_mla=lse_mla, cat=cat, mixed=mixed, h1=h1, a2=a2, hid=hid, act=act, mo=mo)
    return h2, saved


def _layer_bwd(dh2, mod, w, small, rope, s, l, after_mlp=None):
    sh1, sc1, gt1, sh2, sc2, gt2 = mod
    rope_ret, rope_mla = rope
    t = dh2.shape[0]
    nm = lambda n: f"l{l}_{n}_bwd"
    gb, gs = {}, {}
    row = (D_MODEL, F32, "acc")
    dmo, dgt2 = _ew(lambda d, y, gt: (d * gt, jnp.sum(d * y, axis=0, keepdims=True)),
                    [(dh2, True), (s["mo"], True), (gt2, False)], [(D_MODEL, BF16, "tile"), row], ROWS, nm("res2"))
    dhid = _mm(dmo, w["w2"], "nt", 1024, 1024, 2048, name=nm("mlp2_x"), epi=lambda acc, x: (acc * 2.0 * jnp.maximum(x, 0.0),),
               epi_ins=[s["hid"]], epi_outs=[BF16])[0]
    gb["w2"] = _mm(s["act"], dmo, "tn", 1024, 1024, 4096, BF16, name=nm("mlp2_w"))
    da2 = _mm(dhid, w["w1_t"], "nn", 1024, 1024, 2048, name=nm("mlp1_x"))
    gb["w1_t"] = _mm(dhid, s["a2"], "tn", 1024, 1024, 4096, BF16, name=nm("mlp1_w"))
    if after_mlp is not None:
        gt1 = gt1 + after_mlp(gb)

    def norm_bwd(hh, g, sc, sh, da, dres):
        dh_, dg, dsc, dsh = _vjp_block(_norm_mod_fn, 4)(hh, g, sc, sh, da)
        return dh_ + dres, dg, dsc, dsh

    dh1, gs["norm2_g"], dsc2, dsh2 = _ew(norm_bwd, [(s["h1"], True), (small["norm2_g"], False), (sc2, False), (sh2, False), (da2, True), (dh2, True)],
                                         [(D_MODEL, F32, "tile"), row, row, row], ROWS, nm("norm2"))
    dmixed, dgt1 = _ew(lambda d, y, gt: (d * gt, jnp.sum(d * y, axis=0, keepdims=True)),
                       [(dh1, True), (s["mixed"], True), (gt1, False)], [(D_MODEL, BF16, "tile"), row], ROWS, nm("res1"))
    dcat = _mm(dmixed, w["w_out"], "nt", 512, 1024, 2048, name=nm("out_proj_x"))
    gb["w_out"] = _mm(s["cat"], dmixed, "tn", 1024, 1024, 4096, BF16, name=nm("out_proj_w"))
    dy_s5, dy_ret, dy_swa, dy_mla = (dcat[:, i * GROUP_WIDTH:(i + 1) * GROUP_WIDTH] for i in range(4))
    b3, c3, dskip, a_r, a_i = small["s5"]
    gw = (GROUP_WIDTH, F32, "tile")
    gacc = (GROUP_WIDTH, F32, "acc")
    dz_a, dzz, gs["s5_glu_b"] = _ew(_vjp_block(_s5_glu_fn, 3), [(s["z"], True), (s["zz"], True), (small["s5_glu_b"], False), (dy_s5, True)],
                                    [gw, gw, gacc], ROWS, nm("s5_glu"))
    dz_b = _mm(dzz, w["glu_w"], "nt", 512, 512, 512, name=nm("s5_zz_x"))
    gb["glu_w"] = _mm(s["z"], dzz, "tn", 512, 512, 1024, BF16, name=nm("s5_zz_w"))

    def act_bwd(ypre, u, dsk, dza, dzb):
        return _vjp_block(_s5_act_fn, 3)(ypre, u, dsk, dza + dzb)

    dypre, du_a, g_dskip = _ew(act_bwd, [(s["ypre"], True), (s["p_s5"], True), (dskip, False), (dz_a, True), (dz_b, True)],
                               [gw, gw, gacc], ROWS, nm("s5_act"))
    ch, st = S5_PACK * S5_CH, S5_PACK * S5_STATE
    g_c3 =jnp.concatenate([_mm_blocks_tn(s["st_r"], dypre, st, ch, lambda j: j, nm("s5_y_w_re")),
                            _mm_blocks_tn(s["st_i"], dypre, st, ch, lambda j: j, nm("s5_y_w_im"))], axis=0)
    dbu_r, dbu_i, g_ar, g_ai = _s5_scan_bwd(dypre, c3, s["st_r"], s["st_i"], a_r, a_i, nm("s5_scan"))
    du_b = _mm_blocks(dbu_r, b3[:S5_BLOCKS], "nt", 512, pair=(dbu_i, b3[S5_BLOCKS:]), name=nm("s5_bu_x"))
    g_b3 = jnp.concatenate([_mm_blocks_tn(s["p_s5"], dbu_r, ch, st, lambda j: j, nm("s5_bu_w_re")),
                            _mm_blocks_tn(s["p_s5"], dbu_i, ch, st, lambda j: j, nm("s5_bu_w_im"))], axis=0)
    gs["s5"] = (g_b3, g_c3, g_dskip, g_ar, g_ai)
    dqk_rot, dk_rot, dv_ret, dg_ret = _ret_bwd(s["qk_ret"], s["p_ret"], s["o_ret"], dy_ret, small["ret_lgam"], nm("ret"))
    dqk = _rope(_merge_heads(jnp.concatenate([dqk_rot, dk_rot], axis=0)), rope_ret, nm("ret_rope"), inverse=True, out_dtype=BF16)
    dqkv_swa, gs["swa_sinks"] = _swa_bwd(s["qkv_swa"], s["o_swa"], s["lse_swa"], _split_heads(dy_swa, SWA_HEADS), small["swa_sinks"], nm("swa"))
    delta = _mla_delta(s["o_mla"], dy_mla, nm("mla_delta"))
    dqn, dqr, dkn, dv_mla, dkr_heads = _mla_attend_bwd(s["qn"], s["rot"], s["kv_full"], s["lse_mla"], delta, dy_mla, nm("mla_att"))
    dkv_full = jnp.stack([dkn.reshape(t, MLA_HEADS, MLA_NOPE), dv_mla.reshape(t, MLA_HEADS, MLA_V)], axis=2).reshape(t, 2 * MLA_HEADS * MLA_NOPE)
    dkr_rot = _ew(lambda a, b, c, d: a + b + c + d, [(dkr_heads[i], True) for i in range(MLA_HEADS)], [(MLA_ROPE, F32, "tile")], ROWS, nm("mla_dkr"))[0]
    nq = MLA_HEADS * (MLA_NOPE + MLA_ROPE)
    dq_rot = jnp.concatenate([dqn.transpose(1, 0, 2), dqr.transpose(1, 0, 2)], axis=2).reshape(t, nq)
    droped = _rope(jnp.concatenate([dq_rot, dkr_rot, jnp.zeros_like(dkr_rot)], axis=1), rope_mla, nm("mla_rope"), inverse=True, out_dtype=BF16)
    dq_full, dkr = droped[:, :nq], droped[:, nq:nq + MLA_ROPE]
    dcqn = _mm(dq_full, w["w_uq_t"], "nn", 512, 384, 768, name=nm("mla_q_x"))
    gb["w_uq_t"] = _mm(dq_full, s["cqn"], "tn", 768, 384, 1024, BF16, name=nm("mla_q_w"))
    dckvn = _mm(dkv_full, w["w_ukv_t"], "nn", 512, 128, 1024, name=nm("mla_kv_x"))
    gb["w_ukv_t"] = _mm(dkv_full, s["ckvn"], "tn", 1024, 128, 1024, BF16, name=nm("mla_kv_w"))
    dcq, gs["mla_q_norm"] = _ew(_vjp_block(_rms_gain_fn, 2), [(s["cq"], True), (small["mla_q_norm"], False), (dcqn, True)],
                                [(MLA_Q_RANK, BF16, "tile"), (MLA_Q_RANK, F32, "acc")], ROWS, nm("mla_qnorm"))
    dckv, gs["mla_kv_norm"] = _ew(_vjp_block(_rms_gain_fn, 2), [(s["ckv"], True), (small["mla_kv_norm"], False), (dckvn, True)],
                                  [(MLA_KV_RANK, BF16, "tile"), (MLA_KV_RANK, F32, "acc")], ROWS, nm("mla_kvnorm"))
    du = _ew(lambda a, b: a + b, [(du_a, True), (du_b, True)], [(GROUP_WIDTH, BF16, "tile")], ROWS, nm("s5_du"))[0]
    bf = lambda a: a.astype(BF16)
    dproj = jnp.concatenate([du, bf(dqk), bf(dv_ret), bf(dg_ret), bf(_merge_heads(dqkv_swa)), bf(dcq), bf(dckv), bf(dkr)], axis=1)
    da1 = _mm(dproj, w["w_in_t"], "nn", 512, 1024, N_IN, name=nm("proj_x"))
    gb["w_in_t"] = _mm(dproj, s["a1"], "tn", N_IN, 512, 2048, BF16, name=nm("proj_w"))
    dh, gs["norm1_g"], dsc1, dsh1 = _ew(norm_bwd, [(s["h"], True), (small["norm1_g"], False), (sc1, False), (sh1, False), (da1, True), (dh1, True)],
                                        [(D_MODEL, F32, "tile"), row, row, row], ROWS, nm("norm1"))
    dmod = jnp.concatenate([dsh1, dsc1, dgt1, dsh2, dsc2, dgt2], axis=1)
    return dh, gb, gs, dmod


BIG = ("w_in_t", "w1_t", "w_uq_t", "w_ukv_t", "w_out", "w2", "glu_w")
MLP_BIG = ("w1_t", "w2")
LATE_BIG = ("w_out", "w1_t", "w2")
S5_NAMES = ("s5_lambda_re", "s5_lambda_im", "s5_log_dt", "s5_b_re", "s5_b_im", "s5_c_re", "s5_c_im", "s5_d")


def kernel(x, c, norm1_g, norm2_g, ada_w, ada_b, w_in, s5_lambda_re, s5_lambda_im, s5_log_dt, s5_b_re, s5_b_im, s5_c_re, s5_c_im, s5_d, s5_glu_w, s5_glu_b, swa_sinks, mla_q_norm, mla_kv_norm, mla_w_uq, mla_w_ukv, w_out, mlp_w1, mlp_w2, final_norm_g, loss_target, m_norm1_g, m_norm2_g, m_ada_w, m_ada_b, m_w_in, m_s5_lambda_re, m_s5_lambda_im, m_s5_log_dt, m_s5_b_re, m_s5_b_im, m_s5_c_re, m_s5_c_im, m_s5_d, m_s5_glu_w, m_s5_glu_b, m_swa_sinks, m_mla_q_norm, m_mla_kv_norm, m_mla_w_uq, m_mla_w_ukv, m_w_out, m_mlp_w1, m_mlp_w2, m_final_norm_g, v_norm1_g, v_norm2_g, v_ada_w, v_ada_b, v_w_in, v_s5_lambda_re, v_s5_lambda_im, v_s5_log_dt, v_s5_b_re, v_s5_b_im, v_s5_c_re, v_s5_c_im, v_s5_d, v_s5_glu_w, v_s5_glu_b, v_swa_sinks, v_mla_q_norm, v_mla_kv_norm, v_mla_w_uq, v_mla_w_ukv, v_w_out, v_mlp_w1, v_mlp_w2, v_final_norm_g):
    names = ["norm1_g", "norm2_g", "ada_w", "ada_b", "w_in", "s5_lambda_re", "s5_lambda_im", "s5_log_dt", "s5_b_re", "s5_b_im",
             "s5_c_re", "s5_c_im", "s5_d", "s5_glu_w", "s5_glu_b", "swa_sinks", "mla_q_norm", "mla_kv_norm", "mla_w_uq",
             "mla_w_ukv", "w_out", "mlp_w1", "mlp_w2", "final_norm_g"]
    env = locals()
    wts = {n: env[n] for n in names}
    mom = {n: env["m_" + n] for n in names}
    var = {n: env["v_" + n] for n in names}
    t = x.shape[1]
    me = 4 * lax.axis_index("x") + 2 * lax.axis_index("y") + lax.axis_index("c")
    rope = _rope_tables(t)
    ret_lgam = jnp.log1p(-(2.0 ** (-5.0 - jnp.arange(RET_HEADS, dtype=F32))))

    tr = lambda a: a.transpose(0, 2, 1)
    shard = {"w_in_t": tr(w_in), "w1_t": tr(mlp_w1), "w_uq_t": tr(mla_w_uq), "w_ukv_t": tr(mla_w_ukv),
             "w_out": w_out, "w2": mlp_w2, "glu_w": s5_glu_w}
    to_send = [{k: shard[k][l].astype(BF16) for k in BIG} for l in range(DEPTH)]
    as_rows = lambda keys, arrs: {k: a.reshape(-1, shard[k].shape[2]) for k, a in zip(keys, arrs)}
    first = [k for k in BIG if k not in LATE_BIG]
    gathered = _all_gather([to_send[0][k] for k in first] + [c], "gather_weights_first")
    c_all = gathered[-1].reshape(N_DEV, D_MODEL)
    big = [as_rows(first, gathered[:len(first)]), None]
    own_slot = lambda k: 4 * lax.axis_index("x") + 2 * lax.axis_index("y") + lax.axis_index("c")

    def gather_start(arrs, tag):
        return _push_start(arrs, [jax.ShapeDtypeStruct((N_DEV,) + a.shape, a.dtype) for a in arrs], GATHER_PEERS,
                           lambda ref, k: ref, own_slot, f"gather_weights_{tag}_start")

    def gather_finish(started, after, tag):
        sent, landed = _push_wait(started[0], started[1], started[2], started[3], after, GATHER_PEERS, lambda ref, k: ref, own_slot,
                                  f"gather_weights_{tag}_wait")
        with_own = [lax.dynamic_update_index_in_dim(full, own, me, 0) for full, own in zip(landed, sent)]
        return _pass_to_sibling(with_own, f"gather_weights_{tag}_pass")

    gather0 = gather_start([to_send[0][k] for k in LATE_BIG], "l0")
    gather1 = gather_start([to_send[1][k] for k in BIG], "l1")

    c_act = _whole(lambda v: v * jax.nn.sigmoid(v), [c_all], [((N_DEV, D_MODEL), F32)], "cond_silu")[0]
    c_pad = jnp.concatenate([c_act, jnp.zeros((128 - N_DEV, D_MODEL), F32)], axis=0)
    cols = ada_w.shape[2]
    mod_part = [_mm(c_pad, ada_w[l], "nn", 128, cols, 512, name=f"l{l}_mod")[:N_DEV] for l in range(DEPTH)]
    mod_all = _all_gather([jnp.stack(mod_part)], "gather_mod")[0]
    mod_rows = lax.dynamic_index_in_dim(mod_all, me, axis=2, keepdims=False)
    mods = []
    for l in range(DEPTH):
        row = mod_rows[:, l].reshape(1, 6 * D_MODEL) + ada_b[l][None]
        if l == 0:
            row = row + (gather0[4][0, 0] + gather1[4][0, 0])
        mods.append([row[:, i * D_MODEL:(i + 1) * D_MODEL] for i in range(6)])

    smalls, s5_pulls = [], []
    for l in range(DEPTH):
        s5_ops, pull = jax.vjp(_s5_prep, *[wts[n][l] for n in S5_NAMES])
        s5_pulls.append(pull)
        smalls.append(dict(norm1_g=norm1_g[l][None], norm2_g=norm2_g[l][None], s5=s5_ops, s5_glu_b=s5_glu_b[l][None],
                           swa_sinks=swa_sinks[l], mla_q_norm=mla_q_norm[l][None], mla_kv_norm=mla_kv_norm[l][None], ret_lgam=ret_lgam))
    h = x[0]
    saved = []
    for l in range(DEPTH):
        if l == 0:
            late = lambda after: as_rows(LATE_BIG, gather_finish(gather0, after, "l0"))
        else:
            big[1] = as_rows(BIG, gather_finish(gather1, h, "l1"))
            late = None
        h, s = _layer_fwd(h, mods[l], big[l], smalls[l], rope, l, late)
        big[l] = s.pop("w")
        saved.append(s)

    fg = final_norm_g[None]
    tgt = loss_target[0]
    loss_local = _ew(_final_fn, [(h, True), (fg, False), (tgt, True)], [(1, F32, "acc")], ROWS, "loss")[0]

    def final_bwd(hh, g, tg):
        dh_, dg, _ = _vjp_block(_final_fn, 3)(hh, g, tg, jnp.ones((1, 1), F32))
        return dh_, dg

    dh, g_final = _ew(final_bwd, [(h, True), (fg, False), (tgt, True)], [(D_MODEL, F32, "tile"), (D_MODEL, F32, "acc")], ROWS, "loss_bwd")
    loss = lax.psum(loss_local[0, 0], ("x", "y", "c"))

    core, chip = lax.axis_index("c"), 2 * lax.axis_index("x") + lax.axis_index("y")

    def core_stage(g_layer, keys, tag):
        g_list = [g_layer[k].reshape(N_DEV, -1, g_layer[k].shape[1]) for k in keys]
        got = _scatter_core(g_list, f"scatter_core_{tag}")
        return [_pick_sum(g, o, core, lambda q, c_: 2 * q + c_, BF16, f"{tag}_core_sum_{k}") for k, g, o in zip(keys, g_list, got)]

    def their_block(ref, k):
        x_, y_ = lax.axis_index("x"), lax.axis_index("y")
        dx, dy, _ = CHIP_PEERS[k]
        return ref.at[2 * (1 - x_ if dx else x_) + (1 - y_ if dy else y_)]

    def chips_start(halves, tag):
        return _push_start(halves, [jax.ShapeDtypeStruct((3,) + a.shape[1:], a.dtype) for a in halves], CHIP_PEERS,
                           their_block, lambda k: k, f"scatter_chips_{tag}_start")

    def chips_wait(started, after, tag):
        return _push_wait(started[0], started[1], started[2], started[3], after, CHIP_PEERS, their_block, lambda k: k, f"scatter_chips_{tag}_wait")

    g_small, dmods = [None] * DEPTH, [None] * DEPTH
    dh, g_big1, g_small[1], dmods[1] = _layer_bwd(dh, mods[1], big[1], smalls[1], rope, saved[1], 1)
    chips1 = chips_start(core_stage(g_big1, BIG, "l1"), "l1")
    mods0 = [m + chips1[4][0, 0] for m in mods[0]]
    early = {}

    def after_mlp(gb):
        early["mlp"] = chips_start(core_stage(gb, MLP_BIG, "l0_mlp"), "l0_mlp")
        return early["mlp"][4][0, 0]

    dh, g_big0, g_small[0], dmods[0] = _layer_bwd(dh, mods0, big[0], smalls[0], rope, saved[0], 0, after_mlp=after_mlp)
    grad_x = dh[None]
    rest = [k for k in BIG if k not in MLP_BIG]
    chips_rest = chips_start(core_stage(g_big0, rest, "l0_rest"), "l0_rest")
    rest_token = chips_rest[4]
    halves1, landed1 = chips_wait(chips1, rest_token, "l1")
    halves_mlp, landed_mlp = chips_wait(early["mlp"], rest_token, "l0_mlp")
    terms = {(1, k): pair for k, pair in zip(BIG, zip(halves1, landed1))}
    terms.update({(0, k): pair for k, pair in zip(MLP_BIG, zip(halves_mlp, landed_mlp))})
    chip_sum = lambda l, k: _pick_sum(*terms[l, k], chip, lambda q, m_: m_, F32, f"l{l}_chip_sum_{k}")

    small_parts = []
    for l in range(DEPTH):
        gs = g_small[l]
        s5g = s5_pulls[l](gs["s5"])
        small_parts += [gs["norm1_g"], gs["norm2_g"], *s5g, gs["s5_glu_b"], gs["swa_sinks"], gs["mla_q_norm"], gs["mla_kv_norm"]]
    small_parts += [g_final, *dmods]
    sizes = [int(np.prod(p.shape)) for p in small_parts]
    flat = jnp.concatenate([p.reshape(1, -1) for p in small_parts], axis=1) + rest_token[0, 0]
    pad = (-flat.shape[1]) % 8192
    flat = jnp.pad(flat, ((0, 0), (0, pad)))
    flat_all = _all_gather([flat], "gather_small_grads")[0].reshape(N_DEV, -1)
    summed = _sum8(flat_all, "sum_small_grads")
    pieces, off = [], 0
    for sz in sizes:
        pieces.append(summed[0, off:off + sz])
        off += sz
    small_names = ["norm1_g", "norm2_g", *S5_NAMES, "s5_glu_b", "swa_sinks", "mla_q_norm", "mla_kv_norm"]
    per_layer = len(small_names)
    grads = {}
    for i, n in enumerate(small_names):
        grads[n] = jnp.stack([pieces[l * per_layer + i].reshape(wts[n].shape[1:]) for l in range(DEPTH)])
    grads["final_norm_g"] = pieces[DEPTH * per_layer]
    grads["ada_b"] = jnp.stack([pieces[DEPTH * per_layer + 1 + l] for l in range(DEPTH)])

    mod_off = sum(sizes[:DEPTH * per_layer + 1])
    dmod_all = flat_all[:, mod_off:mod_off + DEPTH * 6 * D_MODEL].reshape(N_DEV, DEPTH, N_DEV, cols)
    dmod_mine = lax.dynamic_index_in_dim(dmod_all, me, axis=2, keepdims=False).transpose(1, 0, 2)
    dmod_pad = jnp.concatenate([dmod_mine, jnp.zeros((DEPTH, 128 - N_DEV, cols), F32)], axis=1)
    grads["ada_w"] = jnp.stack([_mm(c_pad, dmod_pad[l], "tn", 512, cols, 128, name=f"l{l}_ada_w_grad") for l in range(DEPTH)])

    out_g, out_d, out_m, out_v = dict(grads), {}, {}, {}
    orig = {"w_in_t": "w_in", "w1_t": "mlp_w1", "w_uq_t": "mla_w_uq", "w_ukv_t": "mla_w_ukv", "w_out": "w_out", "w2": "mlp_w2", "glu_w": "s5_glu_w"}

    def update_big(keys):
        for k in keys:
            n = orig[k]
            out_g[n] = jnp.stack([chip_sum(l, k) for l in range(DEPTH)])
            if k.endswith("_t"):
                out_g[n] = tr(out_g[n])
            out_d[n], out_m[n], out_v[n] = _adamw_shard(wts[n], out_g[n], mom[n], var[n], f"adamw_{n}")

    update_big(MLP_BIG)
    out_d["ada_w"], out_m["ada_w"], out_v["ada_w"] = _adamw_shard(wts["ada_w"], out_g["ada_w"], mom["ada_w"], var["ada_w"], "adamw_ada_w")
    small_all = small_names + ["ada_b", "final_norm_g"]
    ds, ms, vs = _adamw_small([wts[n] for n in small_all], [grads[n] for n in small_all], [mom[n] for n in small_all],
                              [var[n] for n in small_all], "adamw_small")
    for n, d, m_, v_ in zip(small_all, ds, ms, vs):
        out_d[n], out_m[n], out_v[n] = d, m_, v_
    halves_rest, landed_rest = chips_wait(chips_rest, out_d["ada_w"], "l0_rest")
    terms.update({(0, k): pair for k, pair in zip(rest, zip(halves_rest, landed_rest))})
    update_big(rest)
    return (loss, grad_x, *[out_g[n] for n in names], *[out_d[n] for n in names], *[out_m[n] for n in names], *[out_v[n] for n in names])
```

```python
import functools
import math

import numpy as np
import jax
import jax.numpy as jnp
from jax import lax
from jax.experimental import pallas as pl
from jax.experimental.pallas import tpu as pltpu

F32 = jnp.float32
BF16 = jnp.bfloat16
_MXU_DTYPE = jnp.bfloat16

N_DEV = 8
D_MODEL = 2048
DEPTH = 2
GROUP_WIDTH = 512
D_FF = 8192
S5_CH, S5_GROUPS, S5_STATE = 16, 32, 64
S5_WIDTH = S5_GROUPS * S5_STATE
S5_PACK = 8
S5_BLOCKS = S5_GROUPS // S5_PACK
RET_HEADS, RET_QK, RET_V, RET_CHUNK = 4, 64, 128, 128
SWA_HD, SWA_HEADS, SWA_KV_HEADS, WINDOW = 64, 8, 2, 128
MLA_HEADS, MLA_Q_RANK, MLA_KV_RANK, MLA_NOPE, MLA_ROPE, MLA_V = 4, 384, 128, 128, 64, 128
ROPE_BASE = 10000.0
EPS = 1e-6
NEG = -1e30
N_IN = 3392
ADAM_LR, ADAM_B1, ADAM_B2, ADAM_EPS, ADAM_WD, ADAM_STEP = 0.001, 0.9, 0.999, 1e-08, 0.01, 10

VMEM_LIMIT_BYTES = 52 * 1024 * 1024
MESH_ID = pl.DeviceIdType.MESH
_ANY = pl.BlockSpec(memory_space=pl.ANY)
_SMEM = pl.BlockSpec(memory_space=pltpu.SMEM)


def _params(sem):
    return pltpu.CompilerParams(dimension_semantics=sem, vmem_limit_bytes=VMEM_LIMIT_BYTES)


_DIMS = {"nn": (((1,), (0,)), ((), ())), "nt": (((1,), (1,)), ((), ())), "tn": (((0,), (0,)), ((), ()))}


def _dot(a, b, mode="nn"):
    return lax.dot_general(a.astype(_MXU_DTYPE), b.astype(_MXU_DTYPE), _DIMS[mode], preferred_element_type=F32)


def _mm(a, b, mode, tm, tn, tk, out_dtype=F32, name="mm", b_off=0, n=None, pair=None, epi=None, epi_ins=(), epi_outs=None):
    if mode == "tn":
        kdim, m = a.shape
    else:
        m, kdim = a.shape
    if n is None:
        n = b.shape[0] if mode == "nt" else b.shape[1]
    tm, tn, tk = min(tm, m), min(tn, n), min(tk, kdim)
    assert m % tm == 0 and n % tn == 0 and kdim % tk == 0, (name, a.shape, b.shape, tm, tn, tk)
    nk = kdim // tk
    a_spec = pl.BlockSpec((tk, tm), lambda i, j, k: (k, i)) if mode == "tn" else pl.BlockSpec((tm, tk), lambda i, j, k: (i, k))
    if mode == "nt":
        b_spec = pl.BlockSpec((tn, tk), lambda i, j, k: (j + b_off, k))
    else:
        b_spec = pl.BlockSpec((tk, tn), lambda i, j, k: (k, j + b_off))
    o_spec = pl.BlockSpec((tm, tn), lambda i, j, k: (i, j))
    n_mm = 2 if pair is None else 4
    out_dtypes = [out_dtype] if epi is None else list(epi_outs)

    def body(*refs):
        ins, extra = refs[:n_mm], refs[n_mm:n_mm + len(epi_ins)]
        outs = refs[n_mm + len(epi_ins):n_mm + len(epi_ins) + len(out_dtypes)]
        part = _dot(ins[0][...], ins[1][...], mode)
        if pair is not None:
            part = part + _dot(ins[2][...], ins[3][...], mode)

        def finish(acc):
            vals = (acc,) if epi is None else epi(acc, *[r[...] for r in extra])
            for o_ref, v, dt in zip(outs, vals, out_dtypes):
                o_ref[...] = v.astype(dt)

        if nk == 1:
            finish(part)
        else:
            acc_ref = refs[-1]
            k = pl.program_id(2)

            @pl.when(k == 0)
            def _():
                acc_ref[...] = part

            @pl.when(k > 0)
            def _():
                acc_ref[...] += part

            @pl.when(k == nk - 1)
            def _():
                finish(acc_ref[...])

    operands = [a, b] + ([] if pair is None else list(pair)) + list(epi_ins)
    res = pl.pallas_call(
        body, name=name, grid=(m // tm, n // tn, nk),
        in_specs=[a_spec, b_spec] * (n_mm // 2) + [o_spec] * len(epi_ins),
        out_specs=[o_spec] * len(out_dtypes), out_shape=[jax.ShapeDtypeStruct((m, n), dt) for dt in out_dtypes],
        scratch_shapes=[] if nk == 1 else [pltpu.VMEM((tm, tn), F32)],
        compiler_params=_params(("parallel", "parallel", "arbitrary")),
    )(*operands)
    return res[0] if epi is None else res


def _mm_blocks(a, b, mode, tm, a_of=None, pair=None, name="mm_blocks"):
    a_of = a_of or (lambda j: j)
    m = a.shape[0]
    nj, kb, nb = b.shape
    a_w, o_w = (kb, nb) if mode == "nn" else (nb, kb)
    tm = min(tm, m)
    a_spec = pl.BlockSpec((tm, a_w), lambda i, j: (i, a_of(j)))
    b_spec = pl.BlockSpec((1, kb, nb), lambda i, j: (j, 0, 0))
    n_in = 2 if pair is None else 4

    def body(*refs):
        acc = _dot(refs[0][...], refs[1][0], mode)
        if pair is not None:
            acc = acc + _dot(refs[2][...], refs[3][0], mode)
        refs[n_in][...] = acc

    operands = [a, b] + ([] if pair is None else list(pair))
    return pl.pallas_call(
        body, name=name, grid=(m // tm, nj), in_specs=[a_spec, b_spec] * (n_in // 2),
        out_specs=pl.BlockSpec((tm, o_w), lambda i, j: (i, j)), out_shape=jax.ShapeDtypeStruct((m, nj * o_w), F32),
        compiler_params=_params(("parallel", "parallel")),
    )(*operands)


def _mm_blocks_tn(a, b, x, y, b_of, name):
    kdim = a.shape[0]
    nj = a.shape[1] // x

    def body(a_ref, b_ref, o_ref):
        o_ref[0] = _dot(a_ref[...], b_ref[...], "tn")

    return pl.pallas_call(
        body, name=name, grid=(nj,), in_specs=[pl.BlockSpec((kdim, x), lambda j: (0, j)), pl.BlockSpec((kdim, y), lambda j: (0, b_of(j)))],
        out_specs=pl.BlockSpec((1, x, y), lambda j: (j, 0, 0)), out_shape=jax.ShapeDtypeStruct((nj, x, y), F32),
        compiler_params=_params(("parallel",)),
    )(a, b)


SUBLANES = 8


def _row_tile(rows, target, mult=SUBLANES):
    best = None
    for cand in range(mult, min(rows, target) + 1, mult):
        if rows % cand == 0:
            best = cand
    return best or rows


def _ew(fn, ins, outs, tt, name):
    t = [a.shape[0] for a, tiled in ins if tiled][0]
    tt = _row_tile(t, tt)
    n_in = len(ins)
    in_specs = [pl.BlockSpec((tt, a.shape[1]), lambda i: (i, 0)) if tiled else pl.BlockSpec(a.shape, lambda i: (0, 0))
                for a, tiled in ins]
    out_specs, out_shapes = [], []
    for w, dt, kind in outs:
        if kind == "tile":
            out_specs.append(pl.BlockSpec((tt, w), lambda i: (i, 0)))
            out_shapes.append(jax.ShapeDtypeStruct((t, w), dt))
        else:
            out_specs.append(pl.BlockSpec((1, w), lambda i: (0, 0)))
            out_shapes.append(jax.ShapeDtypeStruct((1, w), F32))
    has_acc = any(kind == "acc" for _, _, kind in outs)

    def body(*refs):
        vals = fn(*[r[...] for r in refs[:n_in]])
        if not isinstance(vals, (tuple, list)):
            vals = (vals,)
        i = pl.program_id(0)
        for o_ref, v, (w, dt, kind) in zip(refs[n_in:], vals, outs):
            if kind == "tile":
                o_ref[...] = v.astype(dt)
            else:
                @pl.when(i == 0)
                def _(o_ref=o_ref, v=v):
                    o_ref[...] = v.astype(F32)

                @pl.when(i > 0)
                def _(o_ref=o_ref, v=v):
                    o_ref[...] += v.astype(F32)

    res = pl.pallas_call(
        body, name=name, grid=(t // tt,), in_specs=in_specs, out_specs=out_specs, out_shape=out_shapes,
        compiler_params=_params(("arbitrary" if has_acc else "parallel",)),
    )(*[a for a, _ in ins])
    return res


def _whole(fn, ins, outs, name):
    def body(*refs):
        vals = fn(*[r[...] for r in refs[:len(ins)]])
        if not isinstance(vals, (tuple, list)):
            vals = (vals,)
        for o_ref, v in zip(refs[len(ins):], vals):
            o_ref[...] = v.astype(o_ref.dtype)

    return pl.pallas_call(body, name=name, out_shape=[jax.ShapeDtypeStruct(s, dt) for s, dt in outs])(*ins)


def _rms(x):
    return x * lax.rsqrt(jnp.mean(x * x, axis=-1, keepdims=True) + EPS)


def _norm_mod_fn(h, g, sc, sh):
    return (_rms(h) * g) * (1.0 + sc) + sh


def _rms_gain_fn(x, g):
    return _rms(x) * g


def _gate_add_fn(h, y, gt):
    return h + gt * y


def _relu2_fn(x):
    return jnp.square(jnp.maximum(x, 0.0))


def _s5_act_fn(ypre, u, dskip):
    return jax.nn.gelu(ypre + dskip * u)


def _s5_glu_fn(z, zz, b):
    return z * jax.nn.sigmoid(zz + b)


def _ret_gate_fn(o, g):
    return _rms(o) * (g * jax.nn.sigmoid(g))


def _final_fn(h, g, tgt):
    err = _rms(h) * g - tgt
    return 0.5 * jnp.sum(jnp.mean(err * err, axis=-1, keepdims=True), axis=0, keepdims=True)


def _vjp_block(fn, n_args):
    def bwd(*vals):
        _, pull = jax.vjp(fn, *vals[:n_args])
        return pull(vals[n_args])
    return bwd


def _rope_tables(t):
    d = RET_QK
    inv = ROPE_BASE ** (-jnp.arange(0, d, 2, dtype=F32) / d)
    ang = jnp.arange(t, dtype=F32)[:, None] * inv[None, :]
    cos, sin = jnp.cos(ang), jnp.sin(ang)
    cos2, sin2 = jnp.concatenate([cos, cos], -1), jnp.concatenate([-sin, sin], -1)
    ret = (jnp.tile(cos2, (1, 8)), jnp.tile(sin2, (1, 8)))
    one, zero = jnp.ones((t, MLA_NOPE), F32), jnp.zeros((t, MLA_NOPE), F32)
    mla_c = jnp.concatenate([jnp.tile(jnp.concatenate([one, cos2], -1), (1, MLA_HEADS)), cos2, one[:, :d]], -1)
    mla_s = jnp.concatenate([jnp.tile(jnp.concatenate([zero, sin2], -1), (1, MLA_HEADS)), sin2, zero[:, :d]], -1)
    return ret, (mla_c, mla_s)


def _rope_fn(x, c, s, sign):
    w = x.shape[1]
    lane = lax.broadcasted_iota(jnp.int32, x.shape, 1)
    swapped = jnp.where((lane & 63) < 32, pltpu.roll(x, w - 32, 1), pltpu.roll(x, 32, 1))
    return x * c + swapped * (sign * s)


def _rope(x, tables, name, inverse=False, out_dtype=F32):
    c, s = tables
    fn = functools.partial(_rope_fn, sign=-1.0 if inverse else 1.0)
    return _ew(fn, [(x, True), (c, True), (s, True)], [(x.shape[1], out_dtype, "tile")], ROWS, name)[0]


SCAN_ROWS, SCAN_LANES = 256, 512


def _cmul(ar, ai, br, bi):
    return ar * br - ai * bi, ar * bi + ai * br


def _group_powers(ar, ai, reverse):
    shape = (SUBLANES, ar.shape[1])
    row = lax.broadcasted_iota(jnp.int32, shape, 0)
    pr, pi = ar, ai
    out_r, out_i = jnp.zeros(shape, F32), jnp.zeros(shape, F32)
    for e in range(1, SUBLANES + 1):
        hit = row == (SUBLANES - e if reverse else e - 1)
        out_r, out_i = jnp.where(hit, pr, out_r), jnp.where(hit, pi, out_i)
        if e < SUBLANES:
            pr, pi = _cmul(pr, pi, ar, ai)
    return out_r, out_i


def _scan_chunk(in_r_ref, in_i_ref, out_r_ref, out_i_ref, ar, ai, cr, ci, reverse, visit=None):
    rows, lanes = in_r_ref.shape
    sub = lax.broadcasted_iota(jnp.int32, (SUBLANES, lanes), 0)
    edge_r, edge_i = _group_powers(ar, ai, reverse)
    steps, pr, pi, k = [], ar, ai, 1
    while k < SUBLANES:
        steps.append((k, pr, pi))
        pr, pi = _cmul(pr, pi, pr, pi)
        k *= 2
    groups = range(rows // SUBLANES)
    for g in (reversed(groups) if reverse else groups):
        sl = slice(g * SUBLANES, (g + 1) * SUBLANES)
        xr, xi = in_r_ref[sl, :], in_i_ref[sl, :]
        for k, pr, pi in steps:
            shift = SUBLANES - k if reverse else k
            keep = sub < SUBLANES - k if reverse else sub >= k
            tr, ti = _cmul(pr, pi, pltpu.roll(xr, shift, 0), pltpu.roll(xi, shift, 0))
            xr, xi = xr + jnp.where(keep, tr, 0.0), xi + jnp.where(keep, ti, 0.0)
        tr, ti = _cmul(edge_r, edge_i, cr, ci)
        xr, xi = xr + tr, xi + ti
        out_r_ref[sl, :] = xr
        out_i_ref[sl, :] = xi
        if visit is not None:
            visit(sl, xr, xi, cr, ci)
        edge = slice(0, 1) if reverse else slice(SUBLANES - 1, SUBLANES)
        cr, ci = xr[edge, :], xi[edge, :]
    return cr, ci


def _s5_scan_specs(rows, row_block):
    assert SCAN_LANES == S5_PACK * S5_STATE
    chan = pl.BlockSpec((rows, S5_PACK * S5_CH), lambda j, i: (row_block(i), j))
    op = lambda off, shape: pl.BlockSpec((1,) + shape, lambda j, i: (j + off, 0, 0))
    blk = pl.BlockSpec((rows, SCAN_LANES), lambda j, i: (row_block(i), j))
    par = pl.BlockSpec((1, SCAN_LANES), lambda j, i: (0, j))
    return chan, op, blk, par


def _s5_scan_fwd(u, b3, a_r, a_i, name):
    t = u.shape[0]
    rows = min(SCAN_ROWS, t)
    chan, op, blk, par = _s5_scan_specs(rows, lambda i: i)

    def body(u_ref, b_re_ref, b_im_ref, ar_ref, ai_ref, or_ref, oi_ref, cr_ref, ci_ref, sr_ref, si_ref):
        i = pl.program_id(1)

        @pl.when(i == 0)
        def _():
            cr_ref[...] = jnp.zeros_like(cr_ref)
            ci_ref[...] = jnp.zeros_like(ci_ref)

        sr_ref[...] = _dot(u_ref[...], b_re_ref[0])
        si_ref[...] = _dot(u_ref[...], b_im_ref[0])
        ar, ai = ar_ref[...], ai_ref[...]
        cr, ci = _scan_chunk(sr_ref, si_ref, or_ref, oi_ref, ar, ai, cr_ref[...], ci_ref[...], reverse=False)
        cr_ref[...] = cr
        ci_ref[...] = ci

    st_r, st_i = pl.pallas_call(
        body, name=name, grid=(S5_BLOCKS, t // rows),
        in_specs=[chan, op(0, b3.shape[1:]), op(S5_BLOCKS, b3.shape[1:]), par, par], out_specs=[blk, blk],
        out_shape=[jax.ShapeDtypeStruct((t, S5_WIDTH), F32)] * 2,
        scratch_shapes=[pltpu.VMEM((1, SCAN_LANES), F32)] * 2 + [pltpu.VMEM((rows, SCAN_LANES), F32)] * 2,
        compiler_params=_params(("parallel", "arbitrary")),
    )(u, b3, b3, a_r, a_i)
    return st_r, st_i


def _s5_scan_bwd(dy, c3, st_r, st_i, a_r, a_i, name):
    t = dy.shape[0]
    rows = min(SCAN_ROWS, t)
    nc = t // rows
    chan, op, blk, par = _s5_scan_specs(rows, lambda i: nc - 1 - i)

    def body(dy_ref, c_re_ref, c_im_ref, xr_ref, xi_ref, ar_ref, ai_ref, gr_ref, gi_ref, dar_ref, dai_ref, cr_ref, ci_ref, dr_ref, di_ref):
        i = pl.program_id(1)
        dr_ref[...] = _dot(dy_ref[...], c_re_ref[0], "nt")
        di_ref[...] = _dot(dy_ref[...], c_im_ref[0], "nt")

        @pl.when(i == 0)
        def _():
            cr_ref[...] = jnp.zeros_like(cr_ref)
            ci_ref[...] = jnp.zeros_like(ci_ref)
            dar_ref[...] = jnp.zeros_like(dar_ref)
            dai_ref[...] = jnp.zeros_like(dai_ref)

        ar, ai = ar_ref[...], ai_ref[...]
        cr, ci = cr_ref[...], ci_ref[...]
        last = lax.broadcasted_iota(jnp.int32, (SUBLANES, SCAN_LANES), 0) == SUBLANES - 1
        sums = [jnp.zeros((SUBLANES, SCAN_LANES), F32), jnp.zeros((SUBLANES, SCAN_LANES), F32)]

        def visit(sl, gr, gi, next_r, next_i):
            nr = jnp.where(last, next_r, pltpu.roll(gr, SUBLANES - 1, 0))
            ni = jnp.where(last, next_i, pltpu.roll(gi, SUBLANES - 1, 0))
            xr, xi = xr_ref[sl, :], xi_ref[sl, :]
            sums[0] = sums[0] + (nr * xr + ni * xi)
            sums[1] = sums[1] + (ni * xr - nr * xi)

        first_r, first_i = _scan_chunk(dr_ref, di_ref, gr_ref, gi_ref, ar, -ai, cr, ci, reverse=True, visit=visit)
        dar_ref[...] += jnp.sum(sums[0], axis=0, keepdims=True)
        dai_ref[...] += jnp.sum(sums[1], axis=0, keepdims=True)
        cr_ref[...] = first_r
        ci_ref[...] = first_i

    return pl.pallas_call(
        body, name=name, grid=(S5_BLOCKS, nc),
        in_specs=[chan, op(0, c3.shape[1:]), op(S5_BLOCKS, c3.shape[1:]), blk, blk, par, par], out_specs=[blk, blk, par, par],
        out_shape=[jax.ShapeDtypeStruct((t, S5_WIDTH), F32)] * 2 + [jax.ShapeDtypeStruct((1, S5_WIDTH), F32)] * 2,
        scratch_shapes=[pltpu.VMEM((1, SCAN_LANES), F32)] * 2 + [pltpu.VMEM((rows, SCAN_LANES), F32)] * 2,
        compiler_params=_params(("parallel", "arbitrary")),
    )(dy, c3, c3, st_r, st_i, a_r, a_i)


def _s5_prep(lam_re, lam_im, log_dt, b_re, b_im, c_re, c_im, d_skip):
    dt = jnp.exp(log_dt)[:, None]
    mag = jnp.exp(lam_re * dt)
    ar, ai = mag * jnp.cos(lam_im * dt), mag * jnp.sin(lam_im * dt)
    den = lam_re * lam_re + lam_im * lam_im
    cr = ((ar - 1.0) * lam_re + ai * lam_im) / den
    ci = (ai * lam_re - (ar - 1.0) * lam_im) / den
    bbar_r = cr[..., None] * b_re - ci[..., None] * b_im
    bbar_i = cr[..., None] * b_im + ci[..., None] * b_re
    eye = jnp.eye(S5_PACK, dtype=F32)

    def bdiag(m):
        g, a, b = m.shape
        m4 = m.reshape(g // S5_PACK, S5_PACK, a, b)
        return (eye[None, :, None, :, None] * m4[:, :, :, None, :]).reshape(g // S5_PACK, S5_PACK * a, S5_PACK * b)

    b3 = jnp.concatenate([bdiag(bbar_r.transpose(0, 2, 1)), bdiag(bbar_i.transpose(0, 2, 1))], axis=0)
    c3 = jnp.concatenate([bdiag(c_re.transpose(0, 2, 1)), -bdiag(c_im.transpose(0, 2, 1))], axis=0)
    return b3, c3, d_skip.reshape(1, GROUP_WIDTH), ar.reshape(1, S5_WIDTH), ai.reshape(1, S5_WIDTH)


RET_UNROLL = 8


def _loop_unrolled(trips, body, init):
    factor = math.gcd(trips, RET_UNROLL)

    def several(i, carry):
        for u in range(factor):
            carry = body(i * factor + u, carry)
        return carry

    return lax.fori_loop(0, trips // factor, several, init)


def _ret_consts(lgam):
    c = RET_CHUNK
    r = lax.broadcasted_iota(jnp.int32, (c, c), 0)
    m = lax.broadcasted_iota(jnp.int32, (c, c), 1)
    rel = (r - m).astype(F32)
    decay = jnp.where(rel >= 0, jnp.exp(lgam * jnp.maximum(rel, 0.0)), 0.0)
    idx = lax.broadcasted_iota(jnp.int32, (c, 1), 0).astype(F32)
    zeta = jnp.exp(lgam * (c - 1.0 - idx))
    xi = jnp.exp(lgam * (idx + 1.0))
    return decay, zeta, xi, jnp.exp(lgam * c)


def _ret_specs(t):
    qk = lambda off: pl.BlockSpec((1, t, RET_QK), lambda h: (h + off, 0, 0))
    col = lambda off: pl.BlockSpec((t, RET_V), lambda h: (0, h + off))
    return qk, col


def _ret_fwd(qk, p_ret, lgam, name):
    t = qk.shape[1]
    nck = t // RET_CHUNK
    qk_spec, col = _ret_specs(t)

    def body(lg_ref, q_ref, k_ref, v_ref, g_ref, o_ref, y_ref):
        decay, zeta, xi, gam = _ret_consts(lg_ref[pl.program_id(0)])

        def step(n, state):
            sl = pl.ds(pl.multiple_of(n * RET_CHUNK, RET_CHUNK), RET_CHUNK)
            q, k, v = q_ref[0, sl, :], k_ref[0, sl, :] * (RET_QK ** -0.5), v_ref[sl, :]
            s = _dot(q, k, "nt") * decay
            o = _dot(s, v) + _dot(q, state) * xi
            o_ref[sl, :] = o
            y_ref[sl, :] = _ret_gate_fn(o, g_ref[sl, :]).astype(y_ref.dtype)
            return gam * state + _dot(k, zeta * v, "tn")

        _loop_unrolled(nck, step, jnp.zeros((RET_QK, RET_V), F32))

    return pl.pallas_call(
        body, name=name, grid=(RET_HEADS,), in_specs=[_SMEM, qk_spec(0), qk_spec(RET_HEADS), col(4), col(8)],
        out_specs=[col(0), col(0)],
        out_shape=[jax.ShapeDtypeStruct((t, GROUP_WIDTH), F32), jax.ShapeDtypeStruct((t, GROUP_WIDTH), BF16)],
        compiler_params=_params(("parallel",)),
    )(lgam, qk, qk, p_ret, p_ret)


def _ret_bwd(qk, p_ret, o_all, dy, lgam, name):
    t = qk.shape[1]
    nck = t // RET_CHUNK
    qk_spec, col = _ret_specs(t)
    gate_bwd = _vjp_block(_ret_gate_fn, 2)

    def body(lg_ref, q_ref, k_ref, v_ref, g_ref, o_ref, dy_ref, dq_ref, dk_ref, dv_ref, dg_ref, st_ref):
        decay, zeta, xi, gam = _ret_consts(lg_ref[pl.program_id(0)])
        scale = RET_QK ** -0.5

        def fstep(n, state):
            sl = pl.ds(pl.multiple_of(n * RET_CHUNK, RET_CHUNK), RET_CHUNK)
            st_ref[n] = state
            return gam * state + _dot(k_ref[0, sl, :] * scale, zeta * v_ref[sl, :], "tn")

        _loop_unrolled(nck, fstep, jnp.zeros((RET_QK, RET_V), F32))

        def bstep(r, grad_state):
            n = nck - 1 - r
            sl = pl.ds(pl.multiple_of(n * RET_CHUNK, RET_CHUNK), RET_CHUNK)
            q, k, v = q_ref[0, sl, :], k_ref[0, sl, :] * scale, v_ref[sl, :]
            d_o, dg = gate_bwd(o_ref[sl, :], g_ref[sl, :], dy_ref[sl, :])
            dg_ref[sl, :] = dg.astype(dg_ref.dtype)
            s = _dot(q, k, "nt") * decay
            ds = _dot(d_o, v, "nt") * decay
            xdo = xi * d_o
            dq_ref[0, sl, :] = _dot(ds, k) + _dot(xdo, st_ref[n], "nt")
            dk_ref[0, sl, :] = (_dot(ds, q, "tn") + _dot(zeta * v, grad_state, "nt")) * scale
            dv_ref[sl, :] = (_dot(s, d_o, "tn") + zeta * _dot(k, grad_state)).astype(dv_ref.dtype)
            return gam * grad_state + _dot(q, xdo, "tn")

        _loop_unrolled(nck, bstep, jnp.zeros((RET_QK, RET_V), F32))

    hd = pl.BlockSpec((1, t, RET_QK), lambda h: (h, 0, 0))
    return pl.pallas_call(
        body, name=name, grid=(RET_HEADS,),
        in_specs=[_SMEM, qk_spec(0), qk_spec(RET_HEADS), col(4), col(8), col(0), col(0)],
        out_specs=[hd, hd, col(0), col(0)],
        out_shape=[jax.ShapeDtypeStruct((RET_HEADS, t, RET_QK), F32)] * 2 + [jax.ShapeDtypeStruct((t, GROUP_WIDTH), BF16)] * 2,
        scratch_shapes=[pltpu.VMEM((nck, RET_QK, RET_V), F32)], compiler_params=_params(("parallel",)),
    )(lgam, qk, qk, p_ret, p_ret, o_all, dy)


SWA_GROUP = SWA_HEADS // SWA_KV_HEADS
SWA_SCALE = SWA_HD ** -0.5


def _swa_mask(n):
    rows = SWA_GROUP * WINDOW
    r = lax.broadcasted_iota(jnp.int32, (rows, 2 * WINDOW), 0) & (WINDOW - 1)
    j = lax.broadcasted_iota(jnp.int32, (rows, 2 * WINDOW), 1)
    dist = r + WINDOW - j
    return (dist >= 0) & (dist < WINDOW) & (n * WINDOW + j - WINDOW >= 0)


def _swa_sink_rows(sink_ref, kv):
    row = lax.broadcasted_iota(jnp.int32, (SWA_GROUP * WINDOW, 1), 0)
    sink = jnp.zeros((SWA_GROUP * WINDOW, 1), F32)
    for g in range(SWA_GROUP):
        sink = jnp.where(row >= g * WINDOW, sink_ref[kv * SWA_GROUP + g], sink)
    return sink


def _swa_pad_keys(n, k_ref, v_ref, kp_ref, vp_ref):
    @pl.when(n == 0)
    def _():
        zero = jnp.zeros((WINDOW, SWA_HD), F32)
        kp_ref[0:WINDOW, :] = zero
        vp_ref[0:WINDOW, :] = zero
        kp_ref[WINDOW:, :] = k_ref[0]
        vp_ref[WINDOW:, :] = v_ref[0]


SWA_STEP_BLOCKS = 4


def _swa_specs(t):
    per_step = math.gcd(t // WINDOW, SWA_STEP_BLOCKS)
    blk = lambda w: pl.BlockSpec((SWA_GROUP, per_step * WINDOW, w), lambda kv, n: (kv, n, 0))
    kspec = lambda off: pl.BlockSpec((1, t, SWA_HD), lambda kv, n: (SWA_HEADS + off + kv, 0, 0))
    return blk, kspec, per_step


def _swa_fwd(qkv, sinks, name):
    t = qkv.shape[1]
    rows = SWA_GROUP * WINDOW
    blk, kspec, per_step = _swa_specs(t)

    def body(sink_ref, q_ref, k_ref, v_ref, o_ref, lse_ref, kp_ref, vp_ref):
        kv, step = pl.program_id(0), pl.program_id(1)
        _swa_pad_keys(step, k_ref, v_ref, kp_ref, vp_ref)
        sink = _swa_sink_rows(sink_ref, kv)
        for u in range(per_step):
            n = step * per_step + u
            here = slice(u * WINDOW, (u + 1) * WINDOW)
            win = pl.ds(pl.multiple_of(n * WINDOW, WINDOW), 2 * WINDOW)
            s = _dot(q_ref[:, here, :].reshape(rows, SWA_HD), kp_ref[win, :], "nt") * SWA_SCALE
            s = jnp.where(_swa_mask(n), s, NEG)
            m = jnp.maximum(jnp.max(s, axis=-1, keepdims=True), sink)
            p = jnp.exp(s - m)
            den = jnp.sum(p, axis=-1, keepdims=True) + jnp.exp(sink - m)
            o_ref[:, here, :] = _dot(p / den, vp_ref[win, :]).reshape(SWA_GROUP, WINDOW, SWA_HD)
            lse_ref[:, here, :] = (m + jnp.log(den)).reshape(SWA_GROUP, WINDOW, 1)

    return pl.pallas_call(
        body, name=name, grid=(SWA_KV_HEADS, t // WINDOW // per_step),
        in_specs=[_SMEM, blk(SWA_HD), kspec(0), kspec(SWA_KV_HEADS)], out_specs=[blk(SWA_HD), blk(1)],
        out_shape=[jax.ShapeDtypeStruct((SWA_HEADS, t, SWA_HD), F32), jax.ShapeDtypeStruct((SWA_HEADS, t, 1), F32)],
        scratch_shapes=[pltpu.VMEM((t + WINDOW, SWA_HD), F32)] * 2, compiler_params=_params(("parallel", "arbitrary")),
    )(sinks, qkv, qkv, qkv)


def _swa_bwd(qkv, o, lse, d_o, sinks, name):
    t = qkv.shape[1]
    nb = t // WINDOW
    rows = SWA_GROUP * WINDOW
    blk, kspec, per_step = _swa_specs(t)

    def body(sink_ref, q_ref, k_ref, v_ref, o_ref, lse_ref, do_ref, dq_ref, dk_ref, dv_ref, dsink_ref,
             kp_ref, vp_ref, dkp_ref, dvp_ref):
        kv, step = pl.program_id(0), pl.program_id(1)
        _swa_pad_keys(step, k_ref, v_ref, kp_ref, vp_ref)

        @pl.when(step == 0)
        def _():
            dkp_ref[...] = jnp.zeros_like(dkp_ref)
            dvp_ref[...] = jnp.zeros_like(dvp_ref)
            dsink_ref[...] = jnp.zeros_like(dsink_ref)

        sink = _swa_sink_rows(sink_ref, kv)
        head = lax.broadcasted_iota(jnp.int32, (SWA_GROUP, 128), 0)
        acc = jnp.zeros((SWA_GROUP, 128), F32)
        for u in range(per_step):
            n = step * per_step + u
            here = slice(u * WINDOW, (u + 1) * WINDOW)
            win = pl.ds(pl.multiple_of(n * WINDOW, WINDOW), 2 * WINDOW)
            q, dout = q_ref[:, here, :].reshape(rows, SWA_HD), do_ref[:, here, :].reshape(rows, SWA_HD)
            lse_n = lse_ref[:, here, :].reshape(rows, 1)
            s = _dot(q, kp_ref[win, :], "nt") * SWA_SCALE
            s = jnp.where(_swa_mask(n), s, NEG)
            p = jnp.exp(s - lse_n)
            delta = jnp.sum(dout * o_ref[:, here, :].reshape(rows, SWA_HD), axis=-1, keepdims=True)
            ds = p * (_dot(dout, vp_ref[win, :], "nt") - delta)
            dq_ref[:, here, :] = (_dot(ds, kp_ref[win, :]) * SWA_SCALE).reshape(SWA_GROUP, WINDOW, SWA_HD).astype(dq_ref.dtype)
            dkp_ref[win, :] += _dot(ds, q, "tn") * SWA_SCALE
            dvp_ref[win, :] += _dot(p, dout, "tn")
            term = jnp.exp(sink - lse_n) * delta
            for g in range(SWA_GROUP):
                acc = jnp.where(head == g, acc + jnp.sum(term[g * WINDOW:(g + 1) * WINDOW], axis=0, keepdims=True), acc)
        dsink_ref[0] -= acc

        @pl.when(step == nb // per_step - 1)
        def _():
            dk_ref[0] = dkp_ref[WINDOW:, :].astype(dk_ref.dtype)
            dv_ref[0] = dvp_ref[WINDOW:, :].astype(dv_ref.dtype)

    kout = pl.BlockSpec((1, t, SWA_HD), lambda kv, n: (kv, 0, 0))
    dq, dk, dv, dsink = pl.pallas_call(
        body, name=name, grid=(SWA_KV_HEADS, nb // per_step),
        in_specs=[_SMEM, blk(SWA_HD), kspec(0), kspec(SWA_KV_HEADS), blk(SWA_HD), blk(1), blk(SWA_HD)],
        out_specs=[blk(SWA_HD), kout, kout, pl.BlockSpec((1, SWA_GROUP, 128), lambda kv, n: (kv, 0, 0))],
        out_shape=[jax.ShapeDtypeStruct((SWA_HEADS, t, SWA_HD), BF16), jax.ShapeDtypeStruct((SWA_KV_HEADS, t, SWA_HD), BF16),
                   jax.ShapeDtypeStruct((SWA_KV_HEADS, t, SWA_HD), BF16), jax.ShapeDtypeStruct((SWA_KV_HEADS, SWA_GROUP, 128), F32)],
        scratch_shapes=[pltpu.VMEM((t + WINDOW, SWA_HD), F32)] * 4, compiler_params=_params(("parallel", "arbitrary")),
    )(sinks, qkv, qkv, qkv, o, lse, d_o)
    return jnp.concatenate([dq, dk, dv], axis=0), dsink[:, :, 0].reshape(SWA_HEADS)


MLA_SCALE = (MLA_NOPE + MLA_ROPE) ** -0.5
MLA_TILE = 512
MLA_KEY_TILE = 512
MLA_BWD_TILE = 512


def _mla_diag(s):
    r = lax.broadcasted_iota(jnp.int32, s.shape, 0)
    c = lax.broadcasted_iota(jnp.int32, s.shape, 1)
    return jnp.where(c <= r, s, NEG)


def _mla_specs(t, tile):
    whole = lambda w, off: pl.BlockSpec((t, w), lambda h, i: (0, 2 * h + off))
    head = lambda w: pl.BlockSpec((1, t, w), lambda h, i: (h, 0, 0))
    key_rope = pl.BlockSpec((1, t, MLA_ROPE), lambda h, i: (MLA_HEADS, 0, 0))
    tile_of = lambda w: pl.BlockSpec((1, tile, w), lambda h, i: (h, i, 0))
    return whole, head, key_rope, tile_of


def _mla_attend(qn, rot, kv, name):
    t = qn.shape[1]
    tile = min(MLA_TILE, t)
    ktile = min(MLA_KEY_TILE, t)
    ratio = ktile // tile
    whole, head, key_rope, tile_of = _mla_specs(t, tile)

    def body(qn_ref, qr_ref, kn_ref, kr_ref, v_ref, o_ref, lse_ref, m_ref, l_ref, acc_ref):
        i = pl.program_id(1)
        qn_b, qr_b = qn_ref[0], qr_ref[0]

        def rows(j):
            return pl.ds(pl.multiple_of(j * ktile, ktile), ktile)

        def scores(j):
            return (_dot(qn_b, kn_ref[rows(j), :], "nt") + _dot(qr_b, kr_ref[0, rows(j), :], "nt")) * MLA_SCALE

        def causal(s, j):
            qpos = i * tile + lax.broadcasted_iota(jnp.int32, s.shape, 0)
            kpos = j * ktile + lax.broadcasted_iota(jnp.int32, s.shape, 1)
            return jnp.where(kpos <= qpos, s, NEG)

        def update(s, j):
            m_old = m_ref[...]
            m_new = jnp.maximum(m_old, jnp.max(s, axis=-1, keepdims=True))
            alpha = jnp.exp(m_old - m_new)
            p = jnp.exp(s - m_new)
            l_ref[...] = alpha * l_ref[...] + jnp.sum(p, axis=-1, keepdims=True)
            acc_ref[...] = alpha * acc_ref[...] + _dot(p, v_ref[rows(j), :])
            m_ref[...] = m_new

        m_ref[...] = jnp.full_like(m_ref, NEG)
        l_ref[...] = jnp.zeros_like(l_ref)
        acc_ref[...] = jnp.zeros_like(acc_ref)

        def step(j, s_cur):
            s_next = scores(j + 1)
            update(s_cur, j)
            return s_next

        last = i // ratio
        s_last = lax.fori_loop(0, last, step, scores(0))
        update(causal(s_last, last), last)
        o_ref[...] = acc_ref[...] / l_ref[...]
        lse_ref[0] = m_ref[...] + jnp.log(l_ref[...])

    return pl.pallas_call(
        body, name=name, grid=(MLA_HEADS, t // tile),
        in_specs=[tile_of(MLA_NOPE), tile_of(MLA_ROPE), whole(MLA_NOPE, 0), key_rope, whole(MLA_V, 1)],
        out_specs=[pl.BlockSpec((tile, MLA_V), lambda h, i: (i, h)), tile_of(1)],
        out_shape=[jax.ShapeDtypeStruct((t, GROUP_WIDTH), F32), jax.ShapeDtypeStruct((MLA_HEADS, t, 1), F32)],
        scratch_shapes=[pltpu.VMEM((tile, 1), F32), pltpu.VMEM((tile, 1), F32), pltpu.VMEM((tile, MLA_V), F32)],
        compiler_params=_params(("parallel", "parallel")),
    )(qn, rot, kv, rot, kv)


def _mla_delta(o, d_o, name):
    t = o.shape[0]
    tt = min(ROWS, t)

    def body(o_ref, do_ref, d_ref):
        d_ref[0] = jnp.sum(o_ref[...] * do_ref[...], axis=-1, keepdims=True)

    blk = pl.BlockSpec((tt, MLA_V), lambda h, i: (i, h))
    return pl.pallas_call(
        body, name=name, grid=(MLA_HEADS, t // tt), in_specs=[blk, blk],
        out_specs=pl.BlockSpec((1, tt, 1), lambda h, i: (h, i, 0)), out_shape=jax.ShapeDtypeStruct((MLA_HEADS, t, 1), F32),
        compiler_params=_params(("parallel", "parallel")),
    )(o, d_o)


def _mla_attend_bwd(qn, rot, kv, lse, delta, d_o, name):
    t = qn.shape[1]
    tile = min(MLA_BWD_TILE, t)
    nt = t // tile
    whole, head, key_rope, tile_of = _mla_specs(t, tile)

    def body(qn_ref, qr_ref, kn_ref, kr_ref, v_ref, lse_ref, dl_ref, do_ref, dqn_ref, dqr_ref, dkn_ref, dv_ref, dkr_ref):
        j = pl.program_id(1)

        @pl.when(j == 0)
        def _():
            dqn_ref[...] = jnp.zeros_like(dqn_ref)
            dqr_ref[...] = jnp.zeros_like(dqr_ref)

        dkn_ref[...] = jnp.zeros_like(dkn_ref)
        dv_ref[...] = jnp.zeros_like(dv_ref)
        dkr_ref[...] = jnp.zeros_like(dkr_ref)
        kn_b, kr_b, v_b = kn_ref[...], kr_ref[0], v_ref[...]

        def block(i, diagonal):
            sl = pl.ds(pl.multiple_of(i * tile, tile), tile)
            qn_b, qr_b, dout = qn_ref[0, sl, :], qr_ref[0, sl, :], do_ref[sl, :]
            s = (_dot(qn_b, kn_b, "nt") + _dot(qr_b, kr_b, "nt")) * MLA_SCALE
            if diagonal:
                s = _mla_diag(s)
            p = jnp.exp(s - lse_ref[0, sl, :])
            ds = p * (_dot(dout, v_b, "nt") - dl_ref[0, sl, :]) * MLA_SCALE
            dv_ref[...] += _dot(p, dout, "tn")
            dkn_ref[...] += _dot(ds, qn_b, "tn")
            dkr_ref[0] += _dot(ds, qr_b, "tn")
            dqn_ref[0, sl, :] += _dot(ds, kn_b)
            dqr_ref[0, sl, :] += _dot(ds, kr_b)

        block(j, True)

        def step(i, carry):
            block(i, False)
            return carry

        lax.fori_loop(j + 1, nt, step, 0)

    key_tile = lambda w, off: pl.BlockSpec((tile, w), lambda h, j: (j, 2 * h + off))
    out_tile = pl.BlockSpec((tile, MLA_V), lambda h, j: (j, h))
    return pl.pallas_call(
        body, name=name, grid=(MLA_HEADS, nt),
        in_specs=[head(MLA_NOPE), head(MLA_ROPE), key_tile(MLA_NOPE, 0), pl.BlockSpec((1, tile, MLA_ROPE), lambda h, j: (MLA_HEADS, j, 0)),
                  key_tile(MLA_V, 1), head(1), head(1), pl.BlockSpec((t, MLA_V), lambda h, j: (0, h))],
        out_specs=[head(MLA_NOPE), head(MLA_ROPE), out_tile, out_tile, tile_of(MLA_ROPE)],
        out_shape=[jax.ShapeDtypeStruct((MLA_HEADS, t, MLA_NOPE), F32), jax.ShapeDtypeStruct((MLA_HEADS, t, MLA_ROPE), F32),
                   jax.ShapeDtypeStruct((t, MLA_HEADS * MLA_NOPE), F32), jax.ShapeDtypeStruct((t, MLA_HEADS * MLA_V), F32),
                   jax.ShapeDtypeStruct((MLA_HEADS, t, MLA_ROPE), F32)],
        compiler_params=_params(("parallel", "arbitrary")),
    )(qn, rot, kv, rot, kv, lse, delta, d_o)


def _place():
    return lax.axis_index("x"), lax.axis_index("y"), lax.axis_index("c")


def _all_gather(arrs, name):
    n = len(arrs)

    def body(*refs):
        x_refs, o_refs = refs[:n], refs[n:2 * n]
        send_sems, recv_sems, local_sems = refs[2 * n:]
        x, y, c = _place()
        me, sibling = (x, y, c), (x, y, 1 - c)
        chips = [(1 - x, y), (x, 1 - y), (1 - x, 1 - y)]

        def slot(a, p):
            return o_refs[a].at[4 * p[0] + 2 * p[1] + p[2]]

        def copy(a, k, block, to, src=None):
            return pltpu.make_async_remote_copy(
                src_ref=slot(a, block) if src is None else src, dst_ref=slot(a, block),
                send_sem=send_sems.at[a, k], recv_sem=recv_sems.at[a, k], device_id=to, device_id_type=MESH_ID)

        mine = [pltpu.make_async_copy(x_refs[a], slot(a, me), local_sems.at[a]) for a in range(n)]
        for cp in mine:
            cp.start()
        first = []
        for a in range(n):
            first.append(copy(a, 0, me, sibling, src=x_refs[a]))
            first += [copy(a, 1 + j, me, (*chip, c), src=x_refs[a]) for j, chip in enumerate(chips)]
        for cp in first:
            cp.start()
        passed = []
        for j, chip in enumerate(chips):
            for a in range(n):
                copy(a, 1 + j, (*chip, c), me).wait_recv()
                cp = copy(a, 4 + j, (*chip, c), sibling)
                cp.start()
                passed.append(cp)
        for a in range(n):
            copy(a, 0, sibling, me).wait_recv()
            for j, chip in enumerate(chips):
                copy(a, 4 + j, (*chip, 1 - c), me).wait_recv()
        for cp in first + passed:
            cp.wait_send()
        for cp in mine:
            cp.wait()

    return pl.pallas_call(
        body, name=name, in_specs=[_ANY] * n, out_specs=[_ANY] * n,
        out_shape=[jax.ShapeDtypeStruct((N_DEV,) + a.shape, a.dtype) for a in arrs],
        scratch_shapes=[pltpu.SemaphoreType.DMA((n, 7)), pltpu.SemaphoreType.DMA((n, 7)), pltpu.SemaphoreType.DMA((n,))],
    )(*arrs)


def _pass_to_sibling(arrs, name):
    n = len(arrs)

    def body(*refs):
        a_refs, o_refs = refs[:n], refs[n:2 * n]
        send_sems, recv_sems = refs[2 * n:]
        x, y, c = _place()
        chips = [(1 - x, y), (x, 1 - y), (1 - x, 1 - y)]
        slot = lambda px, py, pc: 4 * px + 2 * py + pc
        sends, recvs = [], []
        for a in range(n):
            for j, (px, py) in enumerate(chips):
                sends.append(pltpu.make_async_remote_copy(
                    src_ref=a_refs[a].at[slot(px, py, c)], dst_ref=o_refs[a].at[slot(px, py, c)], send_sem=send_sems.at[a, j],
                    recv_sem=recv_sems.at[a, j], device_id=(x, y, 1 - c), device_id_type=MESH_ID))
                recvs.append(pltpu.make_async_remote_copy(
                    src_ref=a_refs[a].at[slot(px, py, c)], dst_ref=o_refs[a].at[slot(px, py, 1 - c)], send_sem=send_sems.at[a, j],
                    recv_sem=recv_sems.at[a, j], device_id=(x, y, 1 - c), device_id_type=MESH_ID))
        for cp in sends:
            cp.start()
        for cp in sends:
            cp.wait_send()
        for cp in recvs:
            cp.wait_recv()

    return pl.pallas_call(
        body, name=name, in_specs=[_ANY] * n, out_specs=[_ANY] * n,
        out_shape=[jax.ShapeDtypeStruct(a.shape, a.dtype) for a in arrs], input_output_aliases={i: i for i in range(n)},
        scratch_shapes=[pltpu.SemaphoreType.DMA((n, 3)), pltpu.SemaphoreType.DMA((n, 3))],
    )(*arrs)


def _scatter_core(grads, name):
    n = len(grads)

    def body(*refs):
        g_refs, got_refs = refs[:n], refs[n:2 * n]
        send_sems, recv_sems = refs[2 * n:]
        x, y, c = _place()
        sends = [pltpu.make_async_remote_copy(
            src_ref=g_refs[a].at[2 * q + 1 - c], dst_ref=got_refs[a].at[q], send_sem=send_sems.at[a, q],
            recv_sem=recv_sems.at[a, q], device_id=(x, y, 1 - c), device_id_type=MESH_ID) for a in range(n) for q in range(4)]
        for cp in sends:
            cp.start()
        for cp in sends:
            cp.wait()

    return pl.pallas_call(
        body, name=name, in_specs=[_ANY] * n, out_specs=[_ANY] * n,
        out_shape=[jax.ShapeDtypeStruct((4,) + g.shape[1:], g.dtype) for g in grads],
        scratch_shapes=[pltpu.SemaphoreType.DMA((n, 4)), pltpu.SemaphoreType.DMA((n, 4))],
    )(*grads)


def _scatter_chips(parts, name):
    n = len(parts)

    def body(*refs):
        p_refs, o_refs = refs[:n], refs[n:2 * n]
        send_sems, recv_sems = refs[2 * n:]
        x, y, c = _place()
        chips = [(1 - x, y), (x, 1 - y), (1 - x, 1 - y)]
        sends = [pltpu.make_async_remote_copy(
            src_ref=p_refs[a].at[2 * px + py], dst_ref=o_refs[a].at[j], send_sem=send_sems.at[a, j],
            recv_sem=recv_sems.at[a, j], device_id=(px, py, c), device_id_type=MESH_ID)
            for a in range(n) for j, (px, py) in enumerate(chips)]
        for cp in sends:
            cp.start()
        for cp in sends:
            cp.wait()

    return pl.pallas_call(
        body, name=name, in_specs=[_ANY] * n, out_specs=[_ANY] * n,
        out_shape=[jax.ShapeDtypeStruct((3,) + p.shape[1:], p.dtype) for p in parts],
        scratch_shapes=[pltpu.SemaphoreType.DMA((n, 3)), pltpu.SemaphoreType.DMA((n, 3))],
    )(*parts)


GATHER_PEERS = [(0, 0, 1), (1, 0, 0), (0, 1, 0), (1, 1, 0)]
CHIP_PEERS = [(1, 0, 0), (0, 1, 0), (1, 1, 0)]
CORE_PEERS = [(0, 0, 1)] * 4
_HBM = pl.BlockSpec(memory_space=pltpu.HBM)
_SEM = pl.BlockSpec(memory_space=pltpu.SEMAPHORE)
_EFFECT = pltpu.SideEffectType.DATAFLOW_SIDE_EFFECTING


def _push_copies(src_refs, land_refs, send_sems, recv_sems, peers, src_of, slot_of):
    place = _place()
    flip = lambda v, f: 1 - v if f else v
    return [pltpu.make_async_remote_copy(
        src_ref=src_of(src_refs[a], k), dst_ref=land_refs[a].at[slot_of(k)], send_sem=send_sems[a], recv_sem=recv_sems[a],
        device_id=tuple(flip(v, f) for v, f in zip(place, peer)), device_id_type=MESH_ID)
        for a in range(len(src_refs)) for k, peer in enumerate(peers)]


def _push_start(srcs, land_shapes, peers, src_of, slot_of, name, after=None):
    n = len(srcs)
    extra = [] if after is None else [after]

    def body(*refs):
        src_refs, land_refs = refs[:n], refs[n:2 * n]
        refs = refs[2 * n + len(extra):]
        send_sems, recv_sems = refs[:n], refs[n:2 * n]
        token = refs[-1]
        for cp in _push_copies(src_refs, land_refs, send_sems, recv_sems, peers, src_of, slot_of):
            cp.start()
        token[...] = jnp.zeros_like(token)

    sems = [pltpu.SemaphoreType.DMA(())] * (2 * n)
    lands = [pltpu.with_memory_space_constraint(lax.empty(s.shape, s.dtype), pltpu.HBM) for s in land_shapes]
    res = pl.pallas_call(
        body, name=name, in_specs=[_HBM] * (2 * n) + [_ANY] * len(extra),
        out_specs=[_SEM] * (2 * n) + [_HBM] * (2 * n) + [pl.BlockSpec(memory_space=pltpu.VMEM)],
        out_shape=sems + [pltpu.HBM(s.shape, s.dtype) for s in srcs] + [pltpu.HBM(s.shape, s.dtype) for s in land_shapes]
        + [jax.ShapeDtypeStruct((8, 128), F32)],
        input_output_aliases={i: 2 * n + i for i in range(2 * n)},
        compiler_params=pltpu.CompilerParams(has_side_effects=_EFFECT),
    )(*[pltpu.with_memory_space_constraint(s, pltpu.HBM) for s in srcs], *lands, *extra)
    return list(res[:n]), list(res[n:2 * n]), list(res[2 * n:3 * n]), list(res[3 * n:4 * n]), res[-1]


def _push_wait(send_sems, recv_sems, srcs, lands, after, peers, src_of, slot_of, name):
    n = len(srcs)

    def body(*refs):
        src_refs, land_refs = refs[:n], refs[n:2 * n]
        s_sems, r_sems = refs[2 * n:3 * n], refs[3 * n:4 * n]
        copies = _push_copies(src_refs, land_refs, s_sems, r_sems, peers, src_of, slot_of)
        for cp in copies:
            cp.wait_send()
        for cp in copies:
            cp.wait_recv()

    res = pl.pallas_call(
        body, name=name, in_specs=[_HBM] * (2 * n) + [_SEM] * (2 * n) + [_ANY], out_specs=[_HBM] * (2 * n),
        out_shape=[pltpu.HBM(s.shape, s.dtype) for s in srcs] + [pltpu.HBM(s.shape, s.dtype) for s in lands],
        input_output_aliases={i: i for i in range(2 * n)},
        compiler_params=pltpu.CompilerParams(has_side_effects=_EFFECT),
    )(*srcs, *lands, *send_sems, *recv_sems, after)
    return list(res[:n]), list(res[n:])


def _pick_sum(picked, rest, index, pick_of, out_dtype, name):
    nq, r, cdim = rest.shape
    one = nq == 3
    tr = _row_tile(r, 512, 16)
    tc = 512 if cdim % 512 == 0 else cdim
    grid = (1 if one else nq, r // tr, cdim // tc)

    def body(i_ref, p_ref, r_ref, o_ref):
        acc = p_ref[0].astype(F32)
        if one:
            for j in range(3):
                acc = acc + r_ref[j].astype(F32)
            o_ref[...] = acc.astype(out_dtype)
        else:
            o_ref[0] = (acc + r_ref[0].astype(F32)).astype(out_dtype)

    spec = pltpu.PrefetchScalarGridSpec(
        num_scalar_prefetch=1, grid=grid,
        in_specs=[pl.BlockSpec((1, tr, tc), lambda q, i, j, i_ref: (pick_of(q, i_ref[0]), i, j)),
                  pl.BlockSpec((3, tr, tc), lambda q, i, j, i_ref: (0, i, j)) if one else pl.BlockSpec((1, tr, tc), lambda q, i, j, i_ref: (q, i, j))],
        out_specs=pl.BlockSpec((tr, tc), lambda q, i, j, i_ref: (i, j)) if one else pl.BlockSpec((1, tr, tc), lambda q, i, j, i_ref: (q, i, j)))
    return pl.pallas_call(
        body, name=name, grid_spec=spec,
        out_shape=jax.ShapeDtypeStruct((r, cdim) if one else (nq, r, cdim), out_dtype),
        compiler_params=_params(("parallel", "parallel", "parallel")),
    )(index.astype(jnp.int32).reshape(1), picked, rest)


def _adamw_fn(w, g, m, v):
    m = ADAM_B1 * m + (1.0 - ADAM_B1) * g
    v = ADAM_B2 * v + (1.0 - ADAM_B2) * jnp.square(g)
    m_hat = m / (1.0 - ADAM_B1 ** ADAM_STEP)
    v_hat = v / (1.0 - ADAM_B2 ** ADAM_STEP)
    delta = -ADAM_LR * (m_hat / (jnp.sqrt(v_hat) + ADAM_EPS) + ADAM_WD * w)
    return delta, m, v


def _as2d(a):
    return a.reshape(-1, a.shape[-1])


def _adamw_shard(w, g, m, v, name):
    shape = w.shape
    ins = [_as2d(a) for a in (w, g, m, v)]
    cols = ins[0].shape[1]
    outs = _ew(_adamw_fn, [(a, True) for a in ins], [(cols, F32, "tile")] * 3, 256, name)
    return [o.reshape(shape) for o in outs]


def _adamw_small(ws, gs, ms, vs, name):
    shapes = [w.shape for w in ws]
    flat = lambda a: a.reshape(-1, 128) if a.size % 128 == 0 else a.reshape(1, -1)
    ins = [flat(a) for grp in zip(ws, gs, ms, vs) for a in grp]
    k = len(ws)

    def fn(*vals):
        out = []
        for i in range(k):
            out += list(_adamw_fn(*vals[4 * i:4 * i + 4]))
        return out

    outs = _whole(fn, ins, [(ins[4 * (i // 3)].shape, F32) for i in range(3 * k)], name)
    deltas = [outs[3 * i].reshape(shapes[i]) for i in range(k)]
    new_m = [outs[3 * i + 1].reshape(shapes[i]) for i in range(k)]
    new_v = [outs[3 * i + 2].reshape(shapes[i]) for i in range(k)]
    return deltas, new_m, new_v


def _sum8(stacked, name):
    def fn(a):
        s = a[0:1]
        for i in range(1, N_DEV):
            s = s + a[i:i + 1]
        return s
    w = stacked.shape[1]
    tw = 8192
    if w % tw:
        return _whole(fn, [stacked], [((1, w), F32)], name)[0]

    def body(a_ref, o_ref):
        o_ref[...] = fn(a_ref[...])

    return pl.pallas_call(body, name=name, grid=(w // tw,), in_specs=[pl.BlockSpec((N_DEV, tw), lambda i: (0, i))],
                          out_specs=pl.BlockSpec((1, tw), lambda i: (0, i)), out_shape=jax.ShapeDtypeStruct((1, w), F32))(stacked)


ROWS = 512


def _split_heads(p, nh):
    t = p.shape[0]
    return p.reshape(t, nh, p.shape[1] // nh).transpose(1, 0, 2)


def _merge_heads(p):
    nh, t, d = p.shape
    return p.transpose(1, 0, 2).reshape(t, nh * d)


def _layer_fwd(h, mod, w, small, rope, l, late=None):
    sh1, sc1, gt1, sh2, sc2, gt2 = mod
    rope_ret, rope_mla = rope
    t = h.shape[0]
    nm = lambda s: f"l{l}_{s}"
    a1 = _ew(_norm_mod_fn, [(h, True), (small["norm1_g"], False), (sc1, False), (sh1, False)], [(D_MODEL, BF16, "tile")], ROWS, nm("norm1"))[0]
    p_s5 = _mm(a1, w["w_in_t"], "nt", 512, 512, 2048, name=nm("proj_s5"), n=512)
    p_ret = _mm(a1, w["w_in_t"], "nt", 512, 512, 2048, name=nm("proj_ret"), b_off=1, n=1536)
    p_swa = _mm(a1, w["w_in_t"], "nt", 512, 256, 2048, name=nm("proj_swa"), b_off=8, n=768)
    p_mla = _mm(a1, w["w_in_t"][2816:], "nt", 512, 576, 2048, name=nm("proj_mla"))
    b3, c3, dskip, a_r, a_i = small["s5"]
    st_r, st_i = _s5_scan_fwd(p_s5, b3, a_r, a_i, nm("s5_scan"))
    ypre = _mm_blocks(st_r, c3[:S5_BLOCKS], "nn", 512, pair=(st_i, c3[S5_BLOCKS:]), name=nm("s5_y"))
    z = _ew(_s5_act_fn, [(ypre, True), (p_s5, True), (dskip, False)], [(GROUP_WIDTH, F32, "tile")], ROWS, nm("s5_act"))[0]
    zz = _mm(z, w["glu_w"], "nn", 512, 512, 512, name=nm("s5_zz"))
    y_s5 = _ew(_s5_glu_fn, [(z, True), (zz, True), (small["s5_glu_b"], False)], [(GROUP_WIDTH, BF16, "tile")], ROWS, nm("s5_glu"))[0]
    qk_ret = _split_heads(_rope(p_ret[:, :2 * RET_HEADS * RET_QK], rope_ret, nm("ret_rope")), 2 * RET_HEADS)
    o_ret, y_ret = _ret_fwd(qk_ret, p_ret, small["ret_lgam"], nm("ret"))
    qkv_swa = _split_heads(p_swa, 12)
    o_swa, lse_swa = _swa_fwd(qkv_swa, small["swa_sinks"], nm("swa"))
    y_swa = _merge_heads(o_swa).astype(BF16)
    cq, ckv, kr = p_mla[:, :MLA_Q_RANK], p_mla[:, MLA_Q_RANK:MLA_Q_RANK + MLA_KV_RANK], p_mla[:, MLA_Q_RANK + MLA_KV_RANK:]
    cqn = _ew(_rms_gain_fn, [(cq, True), (small["mla_q_norm"], False)], [(MLA_Q_RANK, BF16, "tile")], ROWS, nm("mla_qnorm"))[0]
    ckvn = _ew(_rms_gain_fn, [(ckv, True), (small["mla_kv_norm"], False)], [(MLA_KV_RANK, BF16, "tile")], ROWS, nm("mla_kvnorm"))[0]
    q_full = _mm(cqn, w["w_uq_t"], "nt", 512, 768, 384, name=nm("mla_q"))
    kv_full = _mm(ckvn, w["w_ukv_t"], "nt", 512, 1024, 128, BF16, name=nm("mla_kv"))
    nq = q_full.shape[1]
    roped = _rope(jnp.concatenate([q_full, kr, jnp.zeros_like(kr)], axis=1), rope_mla, nm("mla_rope"), out_dtype=BF16)
    q4 = roped[:, :nq].reshape(t, MLA_HEADS, MLA_NOPE + MLA_ROPE)
    qn = q4[:, :, :MLA_NOPE].transpose(1, 0, 2)
    rot = jnp.concatenate([q4[:, :, MLA_NOPE:].transpose(1, 0, 2), roped[None, :, nq:nq + MLA_ROPE]], axis=0)
    o_mla, lse_mla = _mla_attend(qn, rot, kv_full, nm("mla"))
    cat = jnp.concatenate([y_s5, y_ret, y_swa, o_mla.astype(BF16)], axis=1)
    if late is not None:
        w = {**w, **late(lse_mla)}
    mixed = _mm(cat, w["w_out"], "nn", 512, 1024, 2048, name=nm("out_proj"))
    h1 = _ew(_gate_add_fn, [(h, True), (mixed, True), (gt1, False)], [(D_MODEL, F32, "tile")], ROWS, nm("res1"))[0]
    a2 = _ew(_norm_mod_fn, [(h1, True), (small["norm2_g"], False), (sc2, False), (sh2, False)], [(D_MODEL, BF16, "tile")], ROWS, nm("norm2"))[0]
    hid, act = _mm(a2, w["w1_t"], "nt", 1024, 1024, 2048, name=nm("mlp1"), epi=lambda acc: (acc, _relu2_fn(acc)), epi_outs=[F32, BF16])
    mo = _mm(act, w["w2"], "nn", 1024, 1024, 2048, name=nm("mlp2"))
    h2 = _ew(_gate_add_fn, [(h1, True), (mo, True), (gt2, False)], [(D_MODEL, F32, "tile")], ROWS, nm("res2"))[0]
    saved = dict(w=w, h=h, a1=a1, p_s5=p_s5, p_ret=p_ret, st_r=st_r, st_i=st_i, ypre=ypre, z=z, zz=zz, qk_ret=qk_ret, o_ret=o_ret,
                 qkv_swa=qkv_swa, o_swa=o_swa, lse_swa=lse_swa, cq=cq, ckv=ckv, cqn=cqn, ckvn=ckvn, qn=qn, rot=rot,
                 kv_full=kv_full, o_mla=o_mla, lse_mla=lse_mla, cat=cat, mixed=mixed, h1=h1, a2=a2, hid=hid, act=act, mo=mo)
    return h2, saved


def _layer_bwd(dh2, mod, w, small, rope, s, l, after_mlp=None):
    sh1, sc1, gt1, sh2, sc2, gt2 = mod
    rope_ret, rope_mla = rope
    t = dh2.shape[0]
    nm = lambda n: f"l{l}_{n}_bwd"
    gb, gs = {}, {}
    row = (D_MODEL, F32, "acc")
    dmo, dgt2 = _ew(lambda d, y, gt: (d * gt, jnp.sum(d * y, axis=0, keepdims=True)),
                    [(dh2, True), (s["mo"], True), (gt2, False)], [(D_MODEL, BF16, "tile"), row], ROWS, nm("res2"))
    dhid = _mm(dmo, w["w2"], "nt", 1024, 1024, 2048, name=nm("mlp2_x"), epi=lambda acc, x: (acc * 2.0 * jnp.maximum(x, 0.0),),
               epi_ins=[s["hid"]], epi_outs=[BF16])[0]
    gb["w2"] = _mm(s["act"], dmo, "tn", 1024, 1024, 4096, BF16, name=nm("mlp2_w"))
    da2 = _mm(dhid, w["w1_t"], "nn", 1024, 1024, 2048, name=nm("mlp1_x"))
    gb["w1_t"] = _mm(dhid, s["a2"], "tn", 1024, 1024, 4096, BF16, name=nm("mlp1_w"))
    if after_mlp is not None:
        gt1 = gt1 + after_mlp(gb)

    def norm_bwd(hh, g, sc, sh, da, dres):
        dh_, dg, dsc, dsh = _vjp_block(_norm_mod_fn, 4)(hh, g, sc, sh, da)
        return dh_ + dres, dg, dsc, dsh

    dh1, gs["norm2_g"], dsc2, dsh2 = _ew(norm_bwd, [(s["h1"], True), (small["norm2_g"], False), (sc2, False), (sh2, False), (da2, True), (dh2, True)],
                                         [(D_MODEL, F32, "tile"), row, row, row], ROWS, nm("norm2"))
    dmixed, dgt1 = _ew(lambda d, y, gt: (d * gt, jnp.sum(d * y, axis=0, keepdims=True)),
                       [(dh1, True), (s["mixed"], True), (gt1, False)], [(D_MODEL, BF16, "tile"), row], ROWS, nm("res1"))
    dcat = _mm(dmixed, w["w_out"], "nt", 512, 1024, 2048, name=nm("out_proj_x"))
    gb["w_out"] = _mm(s["cat"], dmixed, "tn", 1024, 1024, 4096, BF16, name=nm("out_proj_w"))
    dy_s5, dy_ret, dy_swa, dy_mla = (dcat[:, i * GROUP_WIDTH:(i + 1) * GROUP_WIDTH] for i in range(4))
    b3, c3, dskip, a_r, a_i = small["s5"]
    gw = (GROUP_WIDTH, F32, "tile")
    gacc = (GROUP_WIDTH, F32, "acc")
    dz_a, dzz, gs["s5_glu_b"] = _ew(_vjp_block(_s5_glu_fn, 3), [(s["z"], True), (s["zz"], True), (small["s5_glu_b"], False), (dy_s5, True)],
                                    [gw, gw, gacc], ROWS, nm("s5_glu"))
    dz_b = _mm(dzz, w["glu_w"], "nt", 512, 512, 512, name=nm("s5_zz_x"))
    gb["glu_w"] = _mm(s["z"], dzz, "tn", 512, 512, 1024, BF16, name=nm("s5_zz_w"))

    def act_bwd(ypre, u, dsk, dza, dzb):
        return _vjp_block(_s5_act_fn, 3)(ypre, u, dsk, dza + dzb)

    dypre, du_a, g_dskip = _ew(act_bwd, [(s["ypre"], True), (s["p_s5"], True), (dskip, False), (dz_a, True), (dz_b, True)],
                               [gw, gw, gacc], ROWS, nm("s5_act"))
    ch, st = S5_PACK * S5_CH, S5_PACK * S5_STATE
    g_c3 =jnp.concatenate([_mm_blocks_tn(s["st_r"], dypre, st, ch, lambda j: j, nm("s5_y_w_re")),
                            _mm_blocks_tn(s["st_i"], dypre, st, ch, lambda j: j, nm("s5_y_w_im"))], axis=0)
    dbu_r, dbu_i, g_ar, g_ai = _s5_scan_bwd(dypre, c3, s["st_r"], s["st_i"], a_r, a_i, nm("s5_scan"))
    du_b = _mm_blocks(dbu_r, b3[:S5_BLOCKS], "nt", 512, pair=(dbu_i, b3[S5_BLOCKS:]), name=nm("s5_bu_x"))
    g_b3 = jnp.concatenate([_mm_blocks_tn(s["p_s5"], dbu_r, ch, st, lambda j: j, nm("s5_bu_w_re")),
                            _mm_blocks_tn(s["p_s5"], dbu_i, ch, st, lambda j: j, nm("s5_bu_w_im"))], axis=0)
    gs["s5"] = (g_b3, g_c3, g_dskip, g_ar, g_ai)
    dqk_rot, dk_rot, dv_ret, dg_ret = _ret_bwd(s["qk_ret"], s["p_ret"], s["o_ret"], dy_ret, small["ret_lgam"], nm("ret"))
    dqk = _rope(_merge_heads(jnp.concatenate([dqk_rot, dk_rot], axis=0)), rope_ret, nm("ret_rope"), inverse=True, out_dtype=BF16)
    dqkv_swa, gs["swa_sinks"] = _swa_bwd(s["qkv_swa"], s["o_swa"], s["lse_swa"], _split_heads(dy_swa, SWA_HEADS), small["swa_sinks"], nm("swa"))
    delta = _mla_delta(s["o_mla"], dy_mla, nm("mla_delta"))
    dqn, dqr, dkn, dv_mla, dkr_heads = _mla_attend_bwd(s["qn"], s["rot"], s["kv_full"], s["lse_mla"], delta, dy_mla, nm("mla_att"))
    dkv_full = jnp.stack([dkn.reshape(t, MLA_HEADS, MLA_NOPE), dv_mla.reshape(t, MLA_HEADS, MLA_V)], axis=2).reshape(t, 2 * MLA_HEADS * MLA_NOPE)
    dkr_rot = _ew(lambda a, b, c, d: a + b + c + d, [(dkr_heads[i], True) for i in range(MLA_HEADS)], [(MLA_ROPE, F32, "tile")], ROWS, nm("mla_dkr"))[0]
    nq = MLA_HEADS * (MLA_NOPE + MLA_ROPE)
    dq_rot = jnp.concatenate([dqn.transpose(1, 0, 2), dqr.transpose(1, 0, 2)], axis=2).reshape(t, nq)
    droped = _rope(jnp.concatenate([dq_rot, dkr_rot, jnp.zeros_like(dkr_rot)], axis=1), rope_mla, nm("mla_rope"), inverse=True, out_dtype=BF16)
    dq_full, dkr = droped[:, :nq], droped[:, nq:nq + MLA_ROPE]
    dcqn = _mm(dq_full, w["w_uq_t"], "nn", 512, 384, 768, name=nm("mla_q_x"))
    gb["w_uq_t"] = _mm(dq_full, s["cqn"], "tn", 768, 384, 1024, BF16, name=nm("mla_q_w"))
    dckvn = _mm(dkv_full, w["w_ukv_t"], "nn", 512, 128, 1024, name=nm("mla_kv_x"))
    gb["w_ukv_t"] = _mm(dkv_full, s["ckvn"], "tn", 1024, 128, 1024, BF16, name=nm("mla_kv_w"))
    dcq, gs["mla_q_norm"] = _ew(_vjp_block(_rms_gain_fn, 2), [(s["cq"], True), (small["mla_q_norm"], False), (dcqn, True)],
                                [(MLA_Q_RANK, BF16, "tile"), (MLA_Q_RANK, F32, "acc")], ROWS, nm("mla_qnorm"))
    dckv, gs["mla_kv_norm"] = _ew(_vjp_block(_rms_gain_fn, 2), [(s["ckv"], True), (small["mla_kv_norm"], False), (dckvn, True)],
                                  [(MLA_KV_RANK, BF16, "tile"), (MLA_KV_RANK, F32, "acc")], ROWS, nm("mla_kvnorm"))
    du = _ew(lambda a, b: a + b, [(du_a, True), (du_b, True)], [(GROUP_WIDTH, BF16, "tile")], ROWS, nm("s5_du"))[0]
    bf = lambda a: a.astype(BF16)
    dproj = jnp.concatenate([du, bf(dqk), bf(dv_ret), bf(dg_ret), bf(_merge_heads(dqkv_swa)), bf(dcq), bf(dckv), bf(dkr)], axis=1)
    da1 = _mm(dproj, w["w_in_t"], "nn", 512, 1024, N_IN, name=nm("proj_x"))
    gb["w_in_t"] = _mm(dproj, s["a1"], "tn", N_IN, 512, 2048, BF16, name=nm("proj_w"))
    dh, gs["norm1_g"], dsc1, dsh1 = _ew(norm_bwd, [(s["h"], True), (small["norm1_g"], False), (sc1, False), (sh1, False), (da1, True), (dh1, True)],
                                        [(D_MODEL, F32, "tile"), row, row, row], ROWS, nm("norm1"))
    dmod = jnp.concatenate([dsh1, dsc1, dgt1, dsh2, dsc2, dgt2], axis=1)
    return dh, gb, gs, dmod


BIG = ("w_in_t", "w1_t", "w_uq_t", "w_ukv_t", "w_out", "w2", "glu_w")
MLP_BIG = ("w1_t", "w2")
LATE_BIG = ("w_out", "w1_t", "w2")
S5_NAMES = ("s5_lambda_re", "s5_lambda_im", "s5_log_dt", "s5_b_re", "s5_b_im", "s5_c_re", "s5_c_im", "s5_d")


def kernel(x, c, norm1_g, norm2_g, ada_w, ada_b, w_in, s5_lambda_re, s5_lambda_im, s5_log_dt, s5_b_re, s5_b_im, s5_c_re, s5_c_im, s5_d, s5_glu_w, s5_glu_b, swa_sinks, mla_q_norm, mla_kv_norm, mla_w_uq, mla_w_ukv, w_out, mlp_w1, mlp_w2, final_norm_g, loss_target, m_norm1_g, m_norm2_g, m_ada_w, m_ada_b, m_w_in, m_s5_lambda_re, m_s5_lambda_im, m_s5_log_dt, m_s5_b_re, m_s5_b_im, m_s5_c_re, m_s5_c_im, m_s5_d, m_s5_glu_w, m_s5_glu_b, m_swa_sinks, m_mla_q_norm, m_mla_kv_norm, m_mla_w_uq, m_mla_w_ukv, m_w_out, m_mlp_w1, m_mlp_w2, m_final_norm_g, v_norm1_g, v_norm2_g, v_ada_w, v_ada_b, v_w_in, v_s5_lambda_re, v_s5_lambda_im, v_s5_log_dt, v_s5_b_re, v_s5_b_im, v_s5_c_re, v_s5_c_im, v_s5_d, v_s5_glu_w, v_s5_glu_b, v_swa_sinks, v_mla_q_norm, v_mla_kv_norm, v_mla_w_uq, v_mla_w_ukv, v_w_out, v_mlp_w1, v_mlp_w2, v_final_norm_g):
    names = ["norm1_g", "norm2_g", "ada_w", "ada_b", "w_in", "s5_lambda_re", "s5_lambda_im", "s5_log_dt", "s5_b_re", "s5_b_im",
             "s5_c_re", "s5_c_im", "s5_d", "s5_glu_w", "s5_glu_b", "swa_sinks", "mla_q_norm", "mla_kv_norm", "mla_w_uq",
             "mla_w_ukv", "w_out", "mlp_w1", "mlp_w2", "final_norm_g"]
    env = locals()
    wts = {n: env[n] for n in names}
    mom = {n: env["m_" + n] for n in names}
    var = {n: env["v_" + n] for n in names}
    t = x.shape[1]
    me = 4 * lax.axis_index("x") + 2 * lax.axis_index("y") + lax.axis_index("c")
    rope = _rope_tables(t)
    ret_lgam = jnp.log1p(-(2.0 ** (-5.0 - jnp.arange(RET_HEADS, dtype=F32))))

    tr = lambda a: a.transpose(0, 2, 1)
    shard = {"w_in_t": tr(w_in), "w1_t": tr(mlp_w1), "w_uq_t": tr(mla_w_uq), "w_ukv_t": tr(mla_w_ukv),
             "w_out": w_out, "w2": mlp_w2, "glu_w": s5_glu_w}
    to_send = [{k: shard[k][l].astype(BF16) for k in BIG} for l in range(DEPTH)]
    as_rows = lambda keys, arrs: {k: a.reshape(-1, shard[k].shape[2]) for k, a in zip(keys, arrs)}
    first = [k for k in BIG if k not in LATE_BIG]
    gathered = _all_gather([to_send[0][k] for k in first] + [c], "gather_weights_first")
    c_all = gathered[-1].reshape(N_DEV, D_MODEL)
    big = [as_rows(first, gathered[:len(first)]), None]
    own_slot = lambda k: 4 * lax.axis_index("x") + 2 * lax.axis_index("y") + lax.axis_index("c")

    def gather_start(arrs, tag):
        return _push_start(arrs, [jax.ShapeDtypeStruct((N_DEV,) + a.shape, a.dtype) for a in arrs], GATHER_PEERS,
                           lambda ref, k: ref, own_slot, f"gather_weights_{tag}_start")

    def gather_finish(started, after, tag):
        sent, landed = _push_wait(started[0], started[1], started[2], started[3], after, GATHER_PEERS, lambda ref, k: ref, own_slot,
                                  f"gather_weights_{tag}_wait")
        with_own = [lax.dynamic_update_index_in_dim(full, own, me, 0) for full, own in zip(landed, sent)]
        return _pass_to_sibling(with_own, f"gather_weights_{tag}_pass")

    gather0 = gather_start([to_send[0][k] for k in LATE_BIG], "l0")
    gather1 = gather_start([to_send[1][k] for k in BIG], "l1")

    c_act = _whole(lambda v: v * jax.nn.sigmoid(v), [c_all], [((N_DEV, D_MODEL), F32)], "cond_silu")[0]
    c_pad = jnp.concatenate([c_act, jnp.zeros((128 - N_DEV, D_MODEL), F32)], axis=0)
    cols = ada_w.shape[2]
    mod_part = [_mm(c_pad, ada_w[l], "nn", 128, cols, 512, name=f"l{l}_mod")[:N_DEV] for l in range(DEPTH)]
    mod_all = _all_gather([jnp.stack(mod_part)], "gather_mod")[0]
    mod_rows = lax.dynamic_index_in_dim(mod_all, me, axis=2, keepdims=False)
    mods = []
    for l in range(DEPTH):
        row = mod_rows[:, l].reshape(1, 6 * D_MODEL) + ada_b[l][None]
        if l == 0:
            row = row + (gather0[4][0, 0] + gather1[4][0, 0])
        mods.append([row[:, i * D_MODEL:(i + 1) * D_MODEL] for i in range(6)])

    smalls, s5_pulls = [], []
    for l in range(DEPTH):
        s5_ops, pull = jax.vjp(_s5_prep, *[wts[n][l] for n in S5_NAMES])
        s5_pulls.append(pull)
        smalls.append(dict(norm1_g=norm1_g[l][None], norm2_g=norm2_g[l][None], s5=s5_ops, s5_glu_b=s5_glu_b[l][None],
                           swa_sinks=swa_sinks[l], mla_q_norm=mla_q_norm[l][None], mla_kv_norm=mla_kv_norm[l][None], ret_lgam=ret_lgam))
    h = x[0]
    saved = []
    for l in range(DEPTH):
        if l == 0:
            late = lambda after: as_rows(LATE_BIG, gather_finish(gather0, after, "l0"))
        else:
            big[1] = as_rows(BIG, gather_finish(gather1, h, "l1"))
            late = None
        h, s = _layer_fwd(h, mods[l], big[l], smalls[l], rope, l, late)
        big[l] = s.pop("w")
        saved.append(s)

    fg = final_norm_g[None]
    tgt = loss_target[0]
    loss_local = _ew(_final_fn, [(h, True), (fg, False), (tgt, True)], [(1, F32, "acc")], ROWS, "loss")[0]

    def final_bwd(hh, g, tg):
        dh_, dg, _ = _vjp_block(_final_fn, 3)(hh, g, tg, jnp.ones((1, 1), F32))
        return dh_, dg

    dh, g_final = _ew(final_bwd, [(h, True), (fg, False), (tgt, True)], [(D_MODEL, F32, "tile"), (D_MODEL, F32, "acc")], ROWS, "loss_bwd")
    loss = lax.psum(loss_local[0, 0], ("x", "y", "c"))

    core, chip = lax.axis_index("c"), 2 * lax.axis_index("x") + lax.axis_index("y")

    def core_stage(g_layer, keys, tag):
        g_list = [g_layer[k].reshape(N_DEV, -1, g_layer[k].shape[1]) for k in keys]
        got = _scatter_core(g_list, f"scatter_core_{tag}")
        return [_pick_sum(g, o, core, lambda q, c_: 2 * q + c_, BF16, f"{tag}_core_sum_{k}") for k, g, o in zip(keys, g_list, got)]

    def their_block(ref, k):
        x_, y_ = lax.axis_index("x"), lax.axis_index("y")
        dx, dy, _ = CHIP_PEERS[k]
        return ref.at[2 * (1 - x_ if dx else x_) + (1 - y_ if dy else y_)]

    def chips_start(halves, tag, after=None):
        return _push_start(halves, [jax.ShapeDtypeStruct((3,) + a.shape[1:], a.dtype) for a in halves], CHIP_PEERS,
                           their_block, lambda k: k, f"scatter_chips_{tag}_start", after)

    def chips_wait(started, after, tag):
        return _push_wait(started[0], started[1], started[2], started[3], after, CHIP_PEERS, their_block, lambda k: k, f"scatter_chips_{tag}_wait")

    def sibling_block(ref, q):
        return ref.at[2 * q + 1 - lax.axis_index("c")]

    g_small, dmods = [None] * DEPTH, [None] * DEPTH
    dh, g_big1, g_small[1], dmods[1] = _layer_bwd(dh, mods[1], big[1], smalls[1], rope, saved[1], 1)
    g_list1 = [g_big1[k].reshape(N_DEV, -1, g_big1[k].shape[1]) for k in BIG]
    core1 = _push_start(g_list1, [jax.ShapeDtypeStruct((4,) + g.shape[1:], g.dtype) for g in g_list1], CORE_PEERS,
                        sibling_block, lambda q: q, "scatter_core_l1_start")
    mods0 = [m + core1[4][0, 0] for m in mods[0]]
    early = {}

    def after_mlp(gb):
        g_mine, got = _push_wait(core1[0], core1[1], core1[2], core1[3], gb["w1_t"], CORE_PEERS, sibling_block, lambda q: q,
                                 "scatter_core_l1_wait")
        halves = [_pick_sum(g, o, core, lambda q, c_: 2 * q + c_, BF16, f"l1_core_sum_{k}") for k, g, o in zip(BIG, g_mine, got)]
        early["l1"] = chips_start(halves, "l1")
        early["mlp"] = chips_start(core_stage(gb, MLP_BIG, "l0_mlp"), "l0_mlp")
        return early["l1"][4][0, 0] + early["mlp"][4][0, 0]

    dh, g_big0, g_small[0], dmods[0] = _layer_bwd(dh, mods0, big[0], smalls[0], rope, saved[0], 0, after_mlp=after_mlp)
    grad_x = dh[None]
    rest = [k for k in BIG if k not in MLP_BIG]
    halves_rest = core_stage(g_big0, rest, "l0_rest")

    small_parts = []
    for l in range(DEPTH):
        gs = g_small[l]
        s5g = s5_pulls[l](gs["s5"])
        small_parts += [gs["norm1_g"], gs["norm2_g"], *s5g, gs["s5_glu_b"], gs["swa_sinks"], gs["mla_q_norm"], gs["mla_kv_norm"]]
    small_parts += [g_final, *dmods]
    sizes = [int(np.prod(p.shape)) for p in small_parts]
    flat = jnp.concatenate([p.reshape(1, -1) for p in small_parts], axis=1)
    pad = (-flat.shape[1]) % 8192
    flat = jnp.pad(flat, ((0, 0), (0, pad)))
    flat_all = _all_gather([flat], "gather_small_grads")[0].reshape(N_DEV, -1)
    summed = _sum8(flat_all, "sum_small_grads")
    chips_rest = chips_start(halves_rest, "l0_rest", after=summed)
    rest_token = chips_rest[4]
    halves1, landed1 = chips_wait(early["l1"], rest_token, "l1")
    halves_mlp, landed_mlp = chips_wait(early["mlp"], rest_token, "l0_mlp")
    terms = {(1, k): pair for k, pair in zip(BIG, zip(halves1, landed1))}
    terms.update({(0, k): pair for k, pair in zip(MLP_BIG, zip(halves_mlp, landed_mlp))})
    chip_sum = lambda l, k: _pick_sum(*terms[l, k], chip, lambda q, m_: m_, F32, f"l{l}_chip_sum_{k}")
    pieces, off = [], 0
    for sz in sizes:
        pieces.append(summed[0, off:off + sz])
        off += sz
    small_names = ["norm1_g", "norm2_g", *S5_NAMES, "s5_glu_b", "swa_sinks", "mla_q_norm", "mla_kv_norm"]
    per_layer = len(small_names)
    grads = {}
    for i, n in enumerate(small_names):
        grads[n] = jnp.stack([pieces[l * per_layer + i].reshape(wts[n].shape[1:]) for l in range(DEPTH)])
    grads["final_norm_g"] = pieces[DEPTH * per_layer]
    grads["ada_b"] = jnp.stack([pieces[DEPTH * per_layer + 1 + l] for l in range(DEPTH)])

    mod_off = sum(sizes[:DEPTH * per_layer + 1])
    dmod_all = flat_all[:, mod_off:mod_off + DEPTH * 6 * D_MODEL].reshape(N_DEV, DEPTH, N_DEV, cols)
    dmod_mine = lax.dynamic_index_in_dim(dmod_all, me, axis=2, keepdims=False).transpose(1, 0, 2)
    dmod_pad = jnp.concatenate([dmod_mine, jnp.zeros((DEPTH, 128 - N_DEV, cols), F32)], axis=1)
    grads["ada_w"] = jnp.stack([_mm(c_pad, dmod_pad[l], "tn", 512, cols, 128, name=f"l{l}_ada_w_grad") for l in range(DEPTH)])

    out_g, out_d, out_m, out_v = dict(grads), {}, {}, {}
    orig = {"w_in_t": "w_in", "w1_t": "mlp_w1", "w_uq_t": "mla_w_uq", "w_ukv_t": "mla_w_ukv", "w_out": "w_out", "w2": "mlp_w2", "glu_w": "s5_glu_w"}

    def update_big(keys):
        for k in keys:
            n = orig[k]
            out_g[n] = jnp.stack([chip_sum(l, k) for l in range(DEPTH)])
            if k.endswith("_t"):
                out_g[n] = tr(out_g[n])
            out_d[n], out_m[n], out_v[n] = _adamw_shard(wts[n], out_g[n], mom[n], var[n], f"adamw_{n}")

    update_big(MLP_BIG)
    out_d["ada_w"], out_m["ada_w"], out_v["ada_w"] = _adamw_shard(wts["ada_w"], out_g["ada_w"], mom["ada_w"], var["ada_w"], "adamw_ada_w")
    small_all = small_names + ["ada_b", "final_norm_g"]
    ds, ms, vs = _adamw_small([wts[n] for n in small_all], [grads[n] for n in small_all], [mom[n] for n in small_all],
                              [var[n] for n in small_all], "adamw_small")
    for n, d, m_, v_ in zip(small_all, ds, ms, vs):
        out_d[n], out_m[n], out_v[n] = d, m_, v_
    halves_rest, landed_rest = chips_wait(chips_rest, out_d["ada_w"], "l0_rest")
    terms.update({(0, k): pair for k, pair in zip(rest, zip(halves_rest, landed_rest))})
    update_big(rest)
    return (loss, grad_x, *[out_g[n] for n in names], *[out_d[n] for n in names], *[out_m[n] for n in names], *[out_v[n] for n in names])
```

```python
import functools
import math

import numpy as np
import jax
import jax.numpy as jnp
from jax import lax
from jax.experimental import pallas as pl
from jax.experimental.pallas import tpu as pltpu

F32 = jnp.float32
BF16 = jnp.bfloat16
_MXU_DTYPE = jnp.bfloat16

N_DEV = 8
D_MODEL = 2048
DEPTH = 2
GROUP_WIDTH = 512
D_FF = 8192
S5_CH, S5_GROUPS, S5_STATE = 16, 32, 64
S5_WIDTH = S5_GROUPS * S5_STATE
S5_PACK = 8
S5_BLOCKS = S5_GROUPS // S5_PACK
RET_HEADS, RET_QK, RET_V, RET_CHUNK = 4, 64, 128, 128
SWA_HD, SWA_HEADS, SWA_KV_HEADS, WINDOW = 64, 8, 2, 128
MLA_HEADS, MLA_Q_RANK, MLA_KV_RANK, MLA_NOPE, MLA_ROPE, MLA_V = 4, 384, 128, 128, 64, 128
ROPE_BASE = 10000.0
EPS = 1e-6
NEG = -1e30
N_IN = 3392
ADAM_LR, ADAM_B1, ADAM_B2, ADAM_EPS, ADAM_WD, ADAM_STEP = 0.001, 0.9, 0.999, 1e-08, 0.01, 10

VMEM_LIMIT_BYTES = 52 * 1024 * 1024
MESH_ID = pl.DeviceIdType.MESH
_ANY = pl.BlockSpec(memory_space=pl.ANY)
_SMEM = pl.BlockSpec(memory_space=pltpu.SMEM)


def _params(sem):
    return pltpu.CompilerParams(dimension_semantics=sem, vmem_limit_bytes=VMEM_LIMIT_BYTES)


_DIMS = {"nn": (((1,), (0,)), ((), ())), "nt": (((1,), (1,)), ((), ())), "tn": (((0,), (0,)), ((), ()))}


def _dot(a, b, mode="nn"):
    return lax.dot_general(a.astype(_MXU_DTYPE), b.astype(_MXU_DTYPE), _DIMS[mode], preferred_element_type=F32)


def _mm(a, b, mode, tm, tn, tk, out_dtype=F32, name="mm", b_off=0, n=None, pair=None, epi=None, epi_ins=(), epi_outs=None):
    if mode == "tn":
        kdim, m = a.shape
    else:
        m, kdim = a.shape
    if n is None:
        n = b.shape[0] if mode == "nt" else b.shape[1]
    tm, tn, tk = min(tm, m), min(tn, n), min(tk, kdim)
    assert m % tm == 0 and n % tn == 0 and kdim % tk == 0, (name, a.shape, b.shape, tm, tn, tk)
    nk = kdim // tk
    a_spec = pl.BlockSpec((tk, tm), lambda i, j, k: (k, i)) if mode == "tn" else pl.BlockSpec((tm, tk), lambda i, j, k: (i, k))
    if mode == "nt":
        b_spec = pl.BlockSpec((tn, tk), lambda i, j, k: (j + b_off, k))
    else:
        b_spec = pl.BlockSpec((tk, tn), lambda i, j, k: (k, j + b_off))
    o_spec = pl.BlockSpec((tm, tn), lambda i, j, k: (i, j))
    n_mm = 2 if pair is None else 4
    out_dtypes = [out_dtype] if epi is None else list(epi_outs)

    def body(*refs):
        ins, extra = refs[:n_mm], refs[n_mm:n_mm + len(epi_ins)]
        outs = refs[n_mm + len(epi_ins):n_mm + len(epi_ins) + len(out_dtypes)]
        part = _dot(ins[0][...], ins[1][...], mode)
        if pair is not None:
            part = part + _dot(ins[2][...], ins[3][...], mode)

        def finish(acc):
            vals = (acc,) if epi is None else epi(acc, *[r[...] for r in extra])
            for o_ref, v, dt in zip(outs, vals, out_dtypes):
                o_ref[...] = v.astype(dt)

        if nk == 1:
            finish(part)
        else:
            acc_ref = refs[-1]
            k = pl.program_id(2)

            @pl.when(k == 0)
            def _():
                acc_ref[...] = part

            @pl.when(k > 0)
            def _():
                acc_ref[...] += part

            @pl.when(k == nk - 1)
            def _():
                finish(acc_ref[...])

    operands = [a, b] + ([] if pair is None else list(pair)) + list(epi_ins)
    res = pl.pallas_call(
        body, name=name, grid=(m // tm, n // tn, nk),
        in_specs=[a_spec, b_spec] * (n_mm // 2) + [o_spec] * len(epi_ins),
        out_specs=[o_spec] * len(out_dtypes), out_shape=[jax.ShapeDtypeStruct((m, n), dt) for dt in out_dtypes],
        scratch_shapes=[] if nk == 1 else [pltpu.VMEM((tm, tn), F32)],
        compiler_params=_params(("parallel", "parallel", "arbitrary")),
    )(*operands)
    return res[0] if epi is None else res


def _mm_blocks(a, b, mode, tm, a_of=None, pair=None, name="mm_blocks"):
    a_of = a_of or (lambda j: j)
    m = a.shape[0]
    nj, kb, nb = b.shape
    a_w, o_w = (kb, nb) if mode == "nn" else (nb, kb)
    tm = min(tm, m)
    a_spec = pl.BlockSpec((tm, a_w), lambda i, j: (i, a_of(j)))
    b_spec = pl.BlockSpec((1, kb, nb), lambda i, j: (j, 0, 0))
    n_in = 2 if pair is None else 4

    def body(*refs):
        acc = _dot(refs[0][...], refs[1][0], mode)
        if pair is not None:
            acc = acc + _dot(refs[2][...], refs[3][0], mode)
        refs[n_in][...] = acc

    operands = [a, b] + ([] if pair is None else list(pair))
    return pl.pallas_call(
        body, name=name, grid=(m // tm, nj), in_specs=[a_spec, b_spec] * (n_in // 2),
        out_specs=pl.BlockSpec((tm, o_w), lambda i, j: (i, j)), out_shape=jax.ShapeDtypeStruct((m, nj * o_w), F32),
        compiler_params=_params(("parallel", "parallel")),
    )(*operands)


def _mm_blocks_tn(a, b, x, y, b_of, name):
    kdim = a.shape[0]
    nj = a.shape[1] // x

    def body(a_ref, b_ref, o_ref):
        o_ref[0] = _dot(a_ref[...], b_ref[...], "tn")

    return pl.pallas_call(
        body, name=name, grid=(nj,), in_specs=[pl.BlockSpec((kdim, x), lambda j: (0, j)), pl.BlockSpec((kdim, y), lambda j: (0, b_of(j)))],
        out_specs=pl.BlockSpec((1, x, y), lambda j: (j, 0, 0)), out_shape=jax.ShapeDtypeStruct((nj, x, y), F32),
        compiler_params=_params(("parallel",)),
    )(a, b)


SUBLANES = 8


def _row_tile(rows, target, mult=SUBLANES):
    best = None
    for cand in range(mult, min(rows, target) + 1, mult):
        if rows % cand == 0:
            best = cand
    return best or rows


def _ew(fn, ins, outs, tt, name):
    t = [a.shape[0] for a, tiled in ins if tiled][0]
    tt = _row_tile(t, tt)
    n_in = len(ins)
    in_specs = [pl.BlockSpec((tt, a.shape[1]), lambda i: (i, 0)) if tiled else pl.BlockSpec(a.shape, lambda i: (0, 0))
                for a, tiled in ins]
    out_specs, out_shapes = [], []
    for w, dt, kind in outs:
        if kind == "tile":
            out_specs.append(pl.BlockSpec((tt, w), lambda i: (i, 0)))
            out_shapes.append(jax.ShapeDtypeStruct((t, w), dt))
        else:
            out_specs.append(pl.BlockSpec((1, w), lambda i: (0, 0)))
            out_shapes.append(jax.ShapeDtypeStruct((1, w), F32))
    has_acc = any(kind == "acc" for _, _, kind in outs)

    def body(*refs):
        vals = fn(*[r[...] for r in refs[:n_in]])
        if not isinstance(vals, (tuple, list)):
            vals = (vals,)
        i = pl.program_id(0)
        for o_ref, v, (w, dt, kind) in zip(refs[n_in:], vals, outs):
            if kind == "tile":
                o_ref[...] = v.astype(dt)
            else:
                @pl.when(i == 0)
                def _(o_ref=o_ref, v=v):
                    o_ref[...] = v.astype(F32)

                @pl.when(i > 0)
                def _(o_ref=o_ref, v=v):
                    o_ref[...] += v.astype(F32)

    res = pl.pallas_call(
        body, name=name, grid=(t // tt,), in_specs=in_specs, out_specs=out_specs, out_shape=out_shapes,
        compiler_params=_params(("arbitrary" if has_acc else "parallel",)),
    )(*[a for a, _ in ins])
    return res


def _whole(fn, ins, outs, name):
    def body(*refs):
        vals = fn(*[r[...] for r in refs[:len(ins)]])
        if not isinstance(vals, (tuple, list)):
            vals = (vals,)
        for o_ref, v in zip(refs[len(ins):], vals):
            o_ref[...] = v.astype(o_ref.dtype)

    return pl.pallas_call(body, name=name, out_shape=[jax.ShapeDtypeStruct(s, dt) for s, dt in outs])(*ins)


def _rms(x):
    return x * lax.rsqrt(jnp.mean(x * x, axis=-1, keepdims=True) + EPS)


def _norm_mod_fn(h, g, sc, sh):
    return (_rms(h) * g) * (1.0 + sc) + sh


def _rms_gain_fn(x, g):
    return _rms(x) * g


def _gate_add_fn(h, y, gt):
    return h + gt * y


def _relu2_fn(x):
    return jnp.square(jnp.maximum(x, 0.0))


def _s5_act_fn(ypre, u, dskip):
    return jax.nn.gelu(ypre + dskip * u)


def _s5_glu_fn(z, zz, b):
    return z * jax.nn.sigmoid(zz + b)


def _ret_gate_fn(o, g):
    return _rms(o) * (g * jax.nn.sigmoid(g))


def _final_fn(h, g, tgt):
    err = _rms(h) * g - tgt
    return 0.5 * jnp.sum(jnp.mean(err * err, axis=-1, keepdims=True), axis=0, keepdims=True)


def _vjp_block(fn, n_args):
    def bwd(*vals):
        _, pull = jax.vjp(fn, *vals[:n_args])
        return pull(vals[n_args])
    return bwd


def _rope_tables(t):
    d = RET_QK
    inv = ROPE_BASE ** (-jnp.arange(0, d, 2, dtype=F32) / d)
    ang = jnp.arange(t, dtype=F32)[:, None] * inv[None, :]
    cos, sin = jnp.cos(ang), jnp.sin(ang)
    cos2, sin2 = jnp.concatenate([cos, cos], -1), jnp.concatenate([-sin, sin], -1)
    ret = (jnp.tile(cos2, (1, 8)), jnp.tile(sin2, (1, 8)))
    one, zero = jnp.ones((t, MLA_NOPE), F32), jnp.zeros((t, MLA_NOPE), F32)
    mla_c = jnp.concatenate([jnp.tile(jnp.concatenate([one, cos2], -1), (1, MLA_HEADS)), cos2, one[:, :d]], -1)
    mla_s = jnp.concatenate([jnp.tile(jnp.concatenate([zero, sin2], -1), (1, MLA_HEADS)), sin2, zero[:, :d]], -1)
    return ret, (mla_c, mla_s)


def _rope_fn(x, c, s, sign):
    w = x.shape[1]
    lane = lax.broadcasted_iota(jnp.int32, x.shape, 1)
    swapped = jnp.where((lane & 63) < 32, pltpu.roll(x, w - 32, 1), pltpu.roll(x, 32, 1))
    return x * c + swapped * (sign * s)


def _rope(x, tables, name, inverse=False, out_dtype=F32):
    c, s = tables
    fn = functools.partial(_rope_fn, sign=-1.0 if inverse else 1.0)
    return _ew(fn, [(x, True), (c, True), (s, True)], [(x.shape[1], out_dtype, "tile")], ROWS, name)[0]


SCAN_ROWS, SCAN_LANES = 256, 512


def _cmul(ar, ai, br, bi):
    return ar * br - ai * bi, ar * bi + ai * br


def _group_powers(ar, ai, reverse):
    shape = (SUBLANES, ar.shape[1])
    row = lax.broadcasted_iota(jnp.int32, shape, 0)
    pr, pi = ar, ai
    out_r, out_i = jnp.zeros(shape, F32), jnp.zeros(shape, F32)
    for e in range(1, SUBLANES + 1):
        hit = row == (SUBLANES - e if reverse else e - 1)
        out_r, out_i = jnp.where(hit, pr, out_r), jnp.where(hit, pi, out_i)
        if e < SUBLANES:
            pr, pi = _cmul(pr, pi, ar, ai)
    return out_r, out_i


def _scan_chunk(in_r_ref, in_i_ref, out_r_ref, out_i_ref, ar, ai, cr, ci, reverse, visit=None):
    rows, lanes = in_r_ref.shape
    sub = lax.broadcasted_iota(jnp.int32, (SUBLANES, lanes), 0)
    edge_r, edge_i = _group_powers(ar, ai, reverse)
    steps, pr, pi, k = [], ar, ai, 1
    while k < SUBLANES:
        steps.append((k, pr, pi))
        pr, pi = _cmul(pr, pi, pr, pi)
        k *= 2
    groups = range(rows // SUBLANES)
    for g in (reversed(groups) if reverse else groups):
        sl = slice(g * SUBLANES, (g + 1) * SUBLANES)
        xr, xi = in_r_ref[sl, :], in_i_ref[sl, :]
        for k, pr, pi in steps:
            shift = SUBLANES - k if reverse else k
            keep = sub < SUBLANES - k if reverse else sub >= k
            tr, ti = _cmul(pr, pi, pltpu.roll(xr, shift, 0), pltpu.roll(xi, shift, 0))
            xr, xi = xr + jnp.where(keep, tr, 0.0), xi + jnp.where(keep, ti, 0.0)
        tr, ti = _cmul(edge_r, edge_i, cr, ci)
        xr, xi = xr + tr, xi + ti
        out_r_ref[sl, :] = xr
        out_i_ref[sl, :] = xi
        if visit is not None:
            visit(sl, xr, xi, cr, ci)
        edge = slice(0, 1) if reverse else slice(SUBLANES - 1, SUBLANES)
        cr, ci = xr[edge, :], xi[edge, :]
    return cr, ci


def _s5_scan_specs(rows, row_block):
    assert SCAN_LANES == S5_PACK * S5_STATE
    chan = pl.BlockSpec((rows, S5_PACK * S5_CH), lambda j, i: (row_block(i), j))
    op = lambda off, shape: pl.BlockSpec((1,) + shape, lambda j, i: (j + off, 0, 0))
    blk = pl.BlockSpec((rows, SCAN_LANES), lambda j, i: (row_block(i), j))
    par = pl.BlockSpec((1, SCAN_LANES), lambda j, i: (0, j))
    return chan, op, blk, par


def _s5_scan_fwd(u, b3, a_r, a_i, name):
    t = u.shape[0]
    rows = min(SCAN_ROWS, t)
    chan, op, blk, par = _s5_scan_specs(rows, lambda i: i)

    def body(u_ref, b_re_ref, b_im_ref, ar_ref, ai_ref, or_ref, oi_ref, cr_ref, ci_ref, sr_ref, si_ref):
        i = pl.program_id(1)

        @pl.when(i == 0)
        def _():
            cr_ref[...] = jnp.zeros_like(cr_ref)
            ci_ref[...] = jnp.zeros_like(ci_ref)

        sr_ref[...] = _dot(u_ref[...], b_re_ref[0])
        si_ref[...] = _dot(u_ref[...], b_im_ref[0])
        ar, ai = ar_ref[...], ai_ref[...]
        cr, ci = _scan_chunk(sr_ref, si_ref, or_ref, oi_ref, ar, ai, cr_ref[...], ci_ref[...], reverse=False)
        cr_ref[...] = cr
        ci_ref[...] = ci

    st_r, st_i = pl.pallas_call(
        body, name=name, grid=(S5_BLOCKS, t // rows),
        in_specs=[chan, op(0, b3.shape[1:]), op(S5_BLOCKS, b3.shape[1:]), par, par], out_specs=[blk, blk],
        out_shape=[jax.ShapeDtypeStruct((t, S5_WIDTH), F32)] * 2,
        scratch_shapes=[pltpu.VMEM((1, SCAN_LANES), F32)] * 2 + [pltpu.VMEM((rows, SCAN_LANES), F32)] * 2,
        compiler_params=_params(("parallel", "arbitrary")),
    )(u, b3, b3, a_r, a_i)
    return st_r, st_i


def _s5_scan_bwd(dy, c3, st_r, st_i, a_r, a_i, name):
    t = dy.shape[0]
    rows = min(SCAN_ROWS, t)
    nc = t // rows
    chan, op, blk, par = _s5_scan_specs(rows, lambda i: nc - 1 - i)

    def body(dy_ref, c_re_ref, c_im_ref, xr_ref, xi_ref, ar_ref, ai_ref, gr_ref, gi_ref, dar_ref, dai_ref, cr_ref, ci_ref, dr_ref, di_ref):
        i = pl.program_id(1)
        dr_ref[...] = _dot(dy_ref[...], c_re_ref[0], "nt")
        di_ref[...] = _dot(dy_ref[...], c_im_ref[0], "nt")

        @pl.when(i == 0)
        def _():
            cr_ref[...] = jnp.zeros_like(cr_ref)
            ci_ref[...] = jnp.zeros_like(ci_ref)
            dar_ref[...] = jnp.zeros_like(dar_ref)
            dai_ref[...] = jnp.zeros_like(dai_ref)

        ar, ai = ar_ref[...], ai_ref[...]
        cr, ci = cr_ref[...], ci_ref[...]
        last = lax.broadcasted_iota(jnp.int32, (SUBLANES, SCAN_LANES), 0) == SUBLANES - 1
        sums = [jnp.zeros((SUBLANES, SCAN_LANES), F32), jnp.zeros((SUBLANES, SCAN_LANES), F32)]

        def visit(sl, gr, gi, next_r, next_i):
            nr = jnp.where(last, next_r, pltpu.roll(gr, SUBLANES - 1, 0))
            ni = jnp.where(last, next_i, pltpu.roll(gi, SUBLANES - 1, 0))
            xr, xi = xr_ref[sl, :], xi_ref[sl, :]
            sums[0] = sums[0] + (nr * xr + ni * xi)
            sums[1] = sums[1] + (ni * xr - nr * xi)

        first_r, first_i = _scan_chunk(dr_ref, di_ref, gr_ref, gi_ref, ar, -ai, cr, ci, reverse=True, visit=visit)
        dar_ref[...] += jnp.sum(sums[0], axis=0, keepdims=True)
        dai_ref[...] += jnp.sum(sums[1], axis=0, keepdims=True)
        cr_ref[...] = first_r
        ci_ref[...] = first_i

    return pl.pallas_call(
        body, name=name, grid=(S5_BLOCKS, nc),
        in_specs=[chan, op(0, c3.shape[1:]), op(S5_BLOCKS, c3.shape[1:]), blk, blk, par, par], out_specs=[blk, blk, par, par],
        out_shape=[jax.ShapeDtypeStruct((t, S5_WIDTH), F32)] * 2 + [jax.ShapeDtypeStruct((1, S5_WIDTH), F32)] * 2,
        scratch_shapes=[pltpu.VMEM((1, SCAN_LANES), F32)] * 2 + [pltpu.VMEM((rows, SCAN_LANES), F32)] * 2,
        compiler_params=_params(("parallel", "arbitrary")),
    )(dy, c3, c3, st_r, st_i, a_r, a_i)


def _s5_prep(lam_re, lam_im, log_dt, b_re, b_im, c_re, c_im, d_skip):
    dt = jnp.exp(log_dt)[:, None]
    mag = jnp.exp(lam_re * dt)
    ar, ai = mag * jnp.cos(lam_im * dt), mag * jnp.sin(lam_im * dt)
    den = lam_re * lam_re + lam_im * lam_im
    cr = ((ar - 1.0) * lam_re + ai * lam_im) / den
    ci = (ai * lam_re - (ar - 1.0) * lam_im) / den
    bbar_r = cr[..., None] * b_re - ci[..., None] * b_im
    bbar_i = cr[..., None] * b_im + ci[..., None] * b_re
    eye = jnp.eye(S5_PACK, dtype=F32)

    def bdiag(m):
        g, a, b = m.shape
        m4 = m.reshape(g // S5_PACK, S5_PACK, a, b)
        return (eye[None, :, None, :, None] * m4[:, :, :, None, :]).reshape(g // S5_PACK, S5_PACK * a, S5_PACK * b)

    b3 = jnp.concatenate([bdiag(bbar_r.transpose(0, 2, 1)), bdiag(bbar_i.transpose(0, 2, 1))], axis=0)
    c3 = jnp.concatenate([bdiag(c_re.transpose(0, 2, 1)), -bdiag(c_im.transpose(0, 2, 1))], axis=0)
    return b3, c3, d_skip.reshape(1, GROUP_WIDTH), ar.reshape(1, S5_WIDTH), ai.reshape(1, S5_WIDTH)


RET_UNROLL = 8


def _loop_unrolled(trips, body, init):
    factor = math.gcd(trips, RET_UNROLL)

    def several(i, carry):
        for u in range(factor):
            carry = body(i * factor + u, carry)
        return carry

    return lax.fori_loop(0, trips // factor, several, init)


def _ret_consts(lgam):
    c = RET_CHUNK
    r = lax.broadcasted_iota(jnp.int32, (c, c), 0)
    m = lax.broadcasted_iota(jnp.int32, (c, c), 1)
    rel = (r - m).astype(F32)
    decay = jnp.where(rel >= 0, jnp.exp(lgam * jnp.maximum(rel, 0.0)), 0.0)
    idx = lax.broadcasted_iota(jnp.int32, (c, 1), 0).astype(F32)
    zeta = jnp.exp(lgam * (c - 1.0 - idx))
    xi = jnp.exp(lgam * (idx + 1.0))
    return decay, zeta, xi, jnp.exp(lgam * c)


def _ret_specs(t):
    qk = lambda off: pl.BlockSpec((1, t, RET_QK), lambda h: (h + off, 0, 0))
    col = lambda off: pl.BlockSpec((t, RET_V), lambda h: (0, h + off))
    return qk, col


def _ret_fwd(qk, p_ret, lgam, name):
    t = qk.shape[1]
    nck = t // RET_CHUNK
    qk_spec, col = _ret_specs(t)

    def body(lg_ref, q_ref, k_ref, v_ref, g_ref, o_ref, y_ref):
        decay, zeta, xi, gam = _ret_consts(lg_ref[pl.program_id(0)])

        def step(n, state):
            sl = pl.ds(pl.multiple_of(n * RET_CHUNK, RET_CHUNK), RET_CHUNK)
            q, k, v = q_ref[0, sl, :], k_ref[0, sl, :] * (RET_QK ** -0.5), v_ref[sl, :]
            s = _dot(q, k, "nt") * decay
            o = _dot(s, v) + _dot(q, state) * xi
            o_ref[sl, :] = o
            y_ref[sl, :] = _ret_gate_fn(o, g_ref[sl, :]).astype(y_ref.dtype)
            return gam * state + _dot(k, zeta * v, "tn")

        _loop_unrolled(nck, step, jnp.zeros((RET_QK, RET_V), F32))

    return pl.pallas_call(
        body, name=name, grid=(RET_HEADS,), in_specs=[_SMEM, qk_spec(0), qk_spec(RET_HEADS), col(4), col(8)],
        out_specs=[col(0), col(0)],
        out_shape=[jax.ShapeDtypeStruct((t, GROUP_WIDTH), F32), jax.ShapeDtypeStruct((t, GROUP_WIDTH), BF16)],
        compiler_params=_params(("parallel",)),
    )(lgam, qk, qk, p_ret, p_ret)


def _ret_bwd(qk, p_ret, o_all, dy, lgam, name):
    t = qk.shape[1]
    nck = t // RET_CHUNK
    qk_spec, col = _ret_specs(t)
    gate_bwd = _vjp_block(_ret_gate_fn, 2)

    def body(lg_ref, q_ref, k_ref, v_ref, g_ref, o_ref, dy_ref, dq_ref, dk_ref, dv_ref, dg_ref, st_ref):
        decay, zeta, xi, gam = _ret_consts(lg_ref[pl.program_id(0)])
        scale = RET_QK ** -0.5

        def fstep(n, state):
            sl = pl.ds(pl.multiple_of(n * RET_CHUNK, RET_CHUNK), RET_CHUNK)
            st_ref[n] = state
            return gam * state + _dot(k_ref[0, sl, :] * scale, zeta * v_ref[sl, :], "tn")

        _loop_unrolled(nck, fstep, jnp.zeros((RET_QK, RET_V), F32))

        def bstep(r, grad_state):
            n = nck - 1 - r
            sl = pl.ds(pl.multiple_of(n * RET_CHUNK, RET_CHUNK), RET_CHUNK)
            q, k, v = q_ref[0, sl, :], k_ref[0, sl, :] * scale, v_ref[sl, :]
            d_o, dg = gate_bwd(o_ref[sl, :], g_ref[sl, :], dy_ref[sl, :])
            dg_ref[sl, :] = dg.astype(dg_ref.dtype)
            s = _dot(q, k, "nt") * decay
            ds = _dot(d_o, v, "nt") * decay
            xdo = xi * d_o
            dq_ref[0, sl, :] = _dot(ds, k) + _dot(xdo, st_ref[n], "nt")
            dk_ref[0, sl, :] = (_dot(ds, q, "tn") + _dot(zeta * v, grad_state, "nt")) * scale
            dv_ref[sl, :] = (_dot(s, d_o, "tn") + zeta * _dot(k, grad_state)).astype(dv_ref.dtype)
            return gam * grad_state + _dot(q, xdo, "tn")

        _loop_unrolled(nck, bstep, jnp.zeros((RET_QK, RET_V), F32))

    hd = pl.BlockSpec((1, t, RET_QK), lambda h: (h, 0, 0))
    return pl.pallas_call(
        body, name=name, grid=(RET_HEADS,),
        in_specs=[_SMEM, qk_spec(0), qk_spec(RET_HEADS), col(4), col(8), col(0), col(0)],
        out_specs=[hd, hd, col(0), col(0)],
        out_shape=[jax.ShapeDtypeStruct((RET_HEADS, t, RET_QK), F32)] * 2 + [jax.ShapeDtypeStruct((t, GROUP_WIDTH), BF16)] * 2,
        scratch_shapes=[pltpu.VMEM((nck, RET_QK, RET_V), F32)], compiler_params=_params(("parallel",)),
    )(lgam, qk, qk, p_ret, p_ret, o_all, dy)


SWA_GROUP = SWA_HEADS // SWA_KV_HEADS
SWA_SCALE = SWA_HD ** -0.5


def _swa_mask(n):
    rows = SWA_GROUP * WINDOW
    r = lax.broadcasted_iota(jnp.int32, (rows, 2 * WINDOW), 0) & (WINDOW - 1)
    j = lax.broadcasted_iota(jnp.int32, (rows, 2 * WINDOW), 1)
    dist = r + WINDOW - j
    return (dist >= 0) & (dist < WINDOW) & (n * WINDOW + j - WINDOW >= 0)


def _swa_sink_rows(sink_ref, kv):
    row = lax.broadcasted_iota(jnp.int32, (SWA_GROUP * WINDOW, 1), 0)
    sink = jnp.zeros((SWA_GROUP * WINDOW, 1), F32)
    for g in range(SWA_GROUP):
        sink = jnp.where(row >= g * WINDOW, sink_ref[kv * SWA_GROUP + g], sink)
    return sink


def _swa_pad_keys(n, k_ref, v_ref, kp_ref, vp_ref):
    @pl.when(n == 0)
    def _():
        zero = jnp.zeros((WINDOW, SWA_HD), F32)
        kp_ref[0:WINDOW, :] = zero
        vp_ref[0:WINDOW, :] = zero
        kp_ref[WINDOW:, :] = k_ref[0]
        vp_ref[WINDOW:, :] = v_ref[0]


SWA_STEP_BLOCKS = 4


def _swa_specs(t):
    per_step = math.gcd(t // WINDOW, SWA_STEP_BLOCKS)
    blk = lambda w: pl.BlockSpec((SWA_GROUP, per_step * WINDOW, w), lambda kv, n: (kv, n, 0))
    kspec = lambda off: pl.BlockSpec((1, t, SWA_HD), lambda kv, n: (SWA_HEADS + off + kv, 0, 0))
    return blk, kspec, per_step


def _swa_fwd(qkv, sinks, name):
    t = qkv.shape[1]
    rows = SWA_GROUP * WINDOW
    blk, kspec, per_step = _swa_specs(t)

    def body(sink_ref, q_ref, k_ref, v_ref, o_ref, lse_ref, kp_ref, vp_ref):
        kv, step = pl.program_id(0), pl.program_id(1)
        _swa_pad_keys(step, k_ref, v_ref, kp_ref, vp_ref)
        sink = _swa_sink_rows(sink_ref, kv)
        for u in range(per_step):
            n = step * per_step + u
            here = slice(u * WINDOW, (u + 1) * WINDOW)
            win = pl.ds(pl.multiple_of(n * WINDOW, WINDOW), 2 * WINDOW)
            s = _dot(q_ref[:, here, :].reshape(rows, SWA_HD), kp_ref[win, :], "nt") * SWA_SCALE
            s = jnp.where(_swa_mask(n), s, NEG)
            m = jnp.maximum(jnp.max(s, axis=-1, keepdims=True), sink)
            p = jnp.exp(s - m)
            den = jnp.sum(p, axis=-1, keepdims=True) + jnp.exp(sink - m)
            o_ref[:, here, :] = _dot(p / den, vp_ref[win, :]).reshape(SWA_GROUP, WINDOW, SWA_HD)
            lse_ref[:, here, :] = (m + jnp.log(den)).reshape(SWA_GROUP, WINDOW, 1)

    return pl.pallas_call(
        body, name=name, grid=(SWA_KV_HEADS, t // WINDOW // per_step),
        in_specs=[_SMEM, blk(SWA_HD), kspec(0), kspec(SWA_KV_HEADS)], out_specs=[blk(SWA_HD), blk(1)],
        out_shape=[jax.ShapeDtypeStruct((SWA_HEADS, t, SWA_HD), F32), jax.ShapeDtypeStruct((SWA_HEADS, t, 1), F32)],
        scratch_shapes=[pltpu.VMEM((t + WINDOW, SWA_HD), F32)] * 2, compiler_params=_params(("parallel", "arbitrary")),
    )(sinks, qkv, qkv, qkv)


def _swa_bwd(qkv, o, lse, d_o, sinks, name):
    t = qkv.shape[1]
    nb = t // WINDOW
    rows = SWA_GROUP * WINDOW
    blk, kspec, per_step = _swa_specs(t)

    def body(sink_ref, q_ref, k_ref, v_ref, o_ref, lse_ref, do_ref, dq_ref, dk_ref, dv_ref, dsink_ref,
             kp_ref, vp_ref, dkp_ref, dvp_ref):
        kv, step = pl.program_id(0), pl.program_id(1)
        _swa_pad_keys(step, k_ref, v_ref, kp_ref, vp_ref)

        @pl.when(step == 0)
        def _():
            dkp_ref[...] = jnp.zeros_like(dkp_ref)
            dvp_ref[...] = jnp.zeros_like(dvp_ref)
            dsink_ref[...] = jnp.zeros_like(dsink_ref)

        sink = _swa_sink_rows(sink_ref, kv)
        head = lax.broadcasted_iota(jnp.int32, (SWA_GROUP, 128), 0)
        acc = jnp.zeros((SWA_GROUP, 128), F32)
        for u in range(per_step):
            n = step * per_step + u
            here = slice(u * WINDOW, (u + 1) * WINDOW)
            win = pl.ds(pl.multiple_of(n * WINDOW, WINDOW), 2 * WINDOW)
            q, dout = q_ref[:, here, :].reshape(rows, SWA_HD), do_ref[:, here, :].reshape(rows, SWA_HD)
            lse_n = lse_ref[:, here, :].reshape(rows, 1)
            s = _dot(q, kp_ref[win, :], "nt") * SWA_SCALE
            s = jnp.where(_swa_mask(n), s, NEG)
            p = jnp.exp(s - lse_n)
            delta = jnp.sum(dout * o_ref[:, here, :].reshape(rows, SWA_HD), axis=-1, keepdims=True)
            ds = p * (_dot(dout, vp_ref[win, :], "nt") - delta)
            dq_ref[:, here, :] = (_dot(ds, kp_ref[win, :]) * SWA_SCALE).reshape(SWA_GROUP, WINDOW, SWA_HD).astype(dq_ref.dtype)
            dkp_ref[win, :] += _dot(ds, q, "tn") * SWA_SCALE
            dvp_ref[win, :] += _dot(p, dout, "tn")
            term = jnp.exp(sink - lse_n) * delta
            for g in range(SWA_GROUP):
                acc = jnp.where(head == g, acc + jnp.sum(term[g * WINDOW:(g + 1) * WINDOW], axis=0, keepdims=True), acc)
        dsink_ref[0] -= acc

        @pl.when(step == nb // per_step - 1)
        def _():
            dk_ref[0] = dkp_ref[WINDOW:, :].astype(dk_ref.dtype)
            dv_ref[0] = dvp_ref[WINDOW:, :].astype(dv_ref.dtype)

    kout = pl.BlockSpec((1, t, SWA_HD), lambda kv, n: (kv, 0, 0))
    dq, dk, dv, dsink = pl.pallas_call(
        body, name=name, grid=(SWA_KV_HEADS, nb // per_step),
        in_specs=[_SMEM, blk(SWA_HD), kspec(0), kspec(SWA_KV_HEADS), blk(SWA_HD), blk(1), blk(SWA_HD)],
        out_specs=[blk(SWA_HD), kout, kout, pl.BlockSpec((1, SWA_GROUP, 128), lambda kv, n: (kv, 0, 0))],
        out_shape=[jax.ShapeDtypeStruct((SWA_HEADS, t, SWA_HD), BF16), jax.ShapeDtypeStruct((SWA_KV_HEADS, t, SWA_HD), BF16),
                   jax.ShapeDtypeStruct((SWA_KV_HEADS, t, SWA_HD), BF16), jax.ShapeDtypeStruct((SWA_KV_HEADS, SWA_GROUP, 128), F32)],
        scratch_shapes=[pltpu.VMEM((t + WINDOW, SWA_HD), F32)] * 4, compiler_params=_params(("parallel", "arbitrary")),
    )(sinks, qkv, qkv, qkv, o, lse, d_o)
    return jnp.concatenate([dq, dk, dv], axis=0), dsink[:, :, 0].reshape(SWA_HEADS)


MLA_SCALE = (MLA_NOPE + MLA_ROPE) ** -0.5
MLA_TILE = 512
MLA_KEY_TILE = 512
MLA_BWD_TILE = 512


def _mla_diag(s):
    r = lax.broadcasted_iota(jnp.int32, s.shape, 0)
    c = lax.broadcasted_iota(jnp.int32, s.shape, 1)
    return jnp.where(c <= r, s, NEG)


def _mla_specs(t, tile):
    whole = lambda w, off: pl.BlockSpec((t, w), lambda h, i: (0, 2 * h + off))
    head = lambda w: pl.BlockSpec((1, t, w), lambda h, i: (h, 0, 0))
    key_rope = pl.BlockSpec((1, t, MLA_ROPE), lambda h, i: (MLA_HEADS, 0, 0))
    tile_of = lambda w: pl.BlockSpec((1, tile, w), lambda h, i: (h, i, 0))
    return whole, head, key_rope, tile_of


def _mla_attend(qn, rot, kv, name):
    t = qn.shape[1]
    tile = min(MLA_TILE, t)
    ktile = min(MLA_KEY_TILE, t)
    ratio = ktile // tile
    whole, head, key_rope, tile_of = _mla_specs(t, tile)

    def body(qn_ref, qr_ref, kn_ref, kr_ref, v_ref, o_ref, lse_ref, m_ref, l_ref, acc_ref):
        i = pl.program_id(1)
        qn_b, qr_b = qn_ref[0], qr_ref[0]

        def rows(j):
            return pl.ds(pl.multiple_of(j * ktile, ktile), ktile)

        def scores(j):
            return (_dot(qn_b, kn_ref[rows(j), :], "nt") + _dot(qr_b, kr_ref[0, rows(j), :], "nt")) * MLA_SCALE

        def causal(s, j):
            qpos = i * tile + lax.broadcasted_iota(jnp.int32, s.shape, 0)
            kpos = j * ktile + lax.broadcasted_iota(jnp.int32, s.shape, 1)
            return jnp.where(kpos <= qpos, s, NEG)

        def update(s, j):
            m_old = m_ref[...]
            m_new = jnp.maximum(m_old, jnp.max(s, axis=-1, keepdims=True))
            alpha = jnp.exp(m_old - m_new)
            p = jnp.exp(s - m_new)
            l_ref[...] = alpha * l_ref[...] + jnp.sum(p, axis=-1, keepdims=True)
            acc_ref[...] = alpha * acc_ref[...] + _dot(p, v_ref[rows(j), :])
            m_ref[...] = m_new

        m_ref[...] = jnp.full_like(m_ref, NEG)
        l_ref[...] = jnp.zeros_like(l_ref)
        acc_ref[...] = jnp.zeros_like(acc_ref)

        def step(j, s_cur):
            s_next = scores(j + 1)
            update(s_cur, j)
            return s_next

        last = i // ratio
        s_last = lax.fori_loop(0, last, step, scores(0))
        update(causal(s_last, last), last)
        o_ref[...] = acc_ref[...] / l_ref[...]
        lse_ref[0] = m_ref[...] + jnp.log(l_ref[...])

    return pl.pallas_call(
        body, name=name, grid=(MLA_HEADS, t // tile),
        in_specs=[tile_of(MLA_NOPE), tile_of(MLA_ROPE), whole(MLA_NOPE, 0), key_rope, whole(MLA_V, 1)],
        out_specs=[pl.BlockSpec((tile, MLA_V), lambda h, i: (i, h)), tile_of(1)],
        out_shape=[jax.ShapeDtypeStruct((t, GROUP_WIDTH), F32), jax.ShapeDtypeStruct((MLA_HEADS, t, 1), F32)],
        scratch_shapes=[pltpu.VMEM((tile, 1), F32), pltpu.VMEM((tile, 1), F32), pltpu.VMEM((tile, MLA_V), F32)],
        compiler_params=_params(("parallel", "parallel")),
    )(qn, rot, kv, rot, kv)


def _mla_attend_bwd(qn, rot, kv, lse, o, d_o, name):
    t = qn.shape[1]
    tile = min(MLA_BWD_TILE, t)
    nt = t // tile
    whole, head, key_rope, tile_of = _mla_specs(t, tile)

    def body(qn_ref, qr_ref, kn_ref, kr_ref, v_ref, lse_ref, o_ref, do_ref, dqn_ref, dqr_ref, dkn_ref, dv_ref, dkr_ref, dl_ref):
        j = pl.program_id(1)

        @pl.when(j == 0)
        def _():
            dqn_ref[...] = jnp.zeros_like(dqn_ref)
            dqr_ref[...] = jnp.zeros_like(dqr_ref)
            dl_ref[0] = jnp.sum(do_ref[...] * o_ref[...], axis=-1, keepdims=True)

        dkn_ref[...] = jnp.zeros_like(dkn_ref)
        dv_ref[...] = jnp.zeros_like(dv_ref)
        dkr_ref[...] = jnp.zeros_like(dkr_ref)
        kn_b, kr_b, v_b = kn_ref[...], kr_ref[0], v_ref[...]

        def block(i, diagonal):
            sl = pl.ds(pl.multiple_of(i * tile, tile), tile)
            qn_b, qr_b, dout = qn_ref[0, sl, :], qr_ref[0, sl, :], do_ref[sl, :]
            s = (_dot(qn_b, kn_b, "nt") + _dot(qr_b, kr_b, "nt")) * MLA_SCALE
            if diagonal:
                s = _mla_diag(s)
            p = jnp.exp(s - lse_ref[0, sl, :])
            ds = p * (_dot(dout, v_b, "nt") - dl_ref[0, sl, :]) * MLA_SCALE
            dv_ref[...] += _dot(p, dout, "tn")
            dkn_ref[...] += _dot(ds, qn_b, "tn")
            dkr_ref[0] += _dot(ds, qr_b, "tn")
            dqn_ref[0, sl, :] += _dot(ds, kn_b)
            dqr_ref[0, sl, :] += _dot(ds, kr_b)

        block(j, True)

        def step(i, carry):
            block(i, False)
            return carry

        lax.fori_loop(j + 1, nt, step, 0)

    key_tile = lambda w, off: pl.BlockSpec((tile, w), lambda h, j: (j, 2 * h + off))
    out_tile = pl.BlockSpec((tile, MLA_V), lambda h, j: (j, h))
    return pl.pallas_call(
        body, name=name, grid=(MLA_HEADS, nt),
        in_specs=[head(MLA_NOPE), head(MLA_ROPE), key_tile(MLA_NOPE, 0), pl.BlockSpec((1, tile, MLA_ROPE), lambda h, j: (MLA_HEADS, j, 0)),
                  key_tile(MLA_V, 1), head(1), pl.BlockSpec((t, MLA_V), lambda h, j: (0, h)), pl.BlockSpec((t, MLA_V), lambda h, j: (0, h))],
        out_specs=[head(MLA_NOPE), head(MLA_ROPE), out_tile, out_tile, tile_of(MLA_ROPE)],
        out_shape=[jax.ShapeDtypeStruct((MLA_HEADS, t, MLA_NOPE), F32), jax.ShapeDtypeStruct((MLA_HEADS, t, MLA_ROPE), F32),
                   jax.ShapeDtypeStruct((t, MLA_HEADS * MLA_NOPE), F32), jax.ShapeDtypeStruct((t, MLA_HEADS * MLA_V), F32),
                   jax.ShapeDtypeStruct((MLA_HEADS, t, MLA_ROPE), F32)],
        scratch_shapes=[pltpu.VMEM((1, t, 1), F32)], compiler_params=_params(("parallel", "arbitrary")),
    )(qn, rot, kv, rot, kv, lse, o, d_o)


def _place():
    return lax.axis_index("x"), lax.axis_index("y"), lax.axis_index("c")


def _all_gather(arrs, name):
    n = len(arrs)

    def body(*refs):
        x_refs, o_refs = refs[:n], refs[n:2 * n]
        send_sems, recv_sems, local_sems = refs[2 * n:]
        x, y, c = _place()
        me, sibling = (x, y, c), (x, y, 1 - c)
        chips = [(1 - x, y), (x, 1 - y), (1 - x, 1 - y)]

        def slot(a, p):
            return o_refs[a].at[4 * p[0] + 2 * p[1] + p[2]]

        def copy(a, k, block, to, src=None):
            return pltpu.make_async_remote_copy(
                src_ref=slot(a, block) if src is None else src, dst_ref=slot(a, block),
                send_sem=send_sems.at[a, k], recv_sem=recv_sems.at[a, k], device_id=to, device_id_type=MESH_ID)

        mine = [pltpu.make_async_copy(x_refs[a], slot(a, me), local_sems.at[a]) for a in range(n)]
        for cp in mine:
            cp.start()
        first = []
        for a in range(n):
            first.append(copy(a, 0, me, sibling, src=x_refs[a]))
            first += [copy(a, 1 + j, me, (*chip, c), src=x_refs[a]) for j, chip in enumerate(chips)]
        for cp in first:
            cp.start()
        passed = []
        for j, chip in enumerate(chips):
            for a in range(n):
                copy(a, 1 + j, (*chip, c), me).wait_recv()
                cp = copy(a, 4 + j, (*chip, c), sibling)
                cp.start()
                passed.append(cp)
        for a in range(n):
            copy(a, 0, sibling, me).wait_recv()
            for j, chip in enumerate(chips):
                copy(a, 4 + j, (*chip, 1 - c), me).wait_recv()
        for cp in first + passed:
            cp.wait_send()
        for cp in mine:
            cp.wait()

    return pl.pallas_call(
        body, name=name, in_specs=[_ANY] * n, out_specs=[_ANY] * n,
        out_shape=[jax.ShapeDtypeStruct((N_DEV,) + a.shape, a.dtype) for a in arrs],
        scratch_shapes=[pltpu.SemaphoreType.DMA((n, 7)), pltpu.SemaphoreType.DMA((n, 7)), pltpu.SemaphoreType.DMA((n,))],
    )(*arrs)


def _pass_to_sibling(arrs, name):
    n = len(arrs)

    def body(*refs):
        a_refs, o_refs = refs[:n], refs[n:2 * n]
        send_sems, recv_sems = refs[2 * n:]
        x, y, c = _place()
        chips = [(1 - x, y), (x, 1 - y), (1 - x, 1 - y)]
        slot = lambda px, py, pc: 4 * px + 2 * py + pc
        sends, recvs = [], []
        for a in range(n):
            for j, (px, py) in enumerate(chips):
                sends.append(pltpu.make_async_remote_copy(
                    src_ref=a_refs[a].at[slot(px, py, c)], dst_ref=o_refs[a].at[slot(px, py, c)], send_sem=send_sems.at[a, j],
                    recv_sem=recv_sems.at[a, j], device_id=(x, y, 1 - c), device_id_type=MESH_ID))
                recvs.append(pltpu.make_async_remote_copy(
                    src_ref=a_refs[a].at[slot(px, py, c)], dst_ref=o_refs[a].at[slot(px, py, 1 - c)], send_sem=send_sems.at[a, j],
                    recv_sem=recv_sems.at[a, j], device_id=(x, y, 1 - c), device_id_type=MESH_ID))
        for cp in sends:
            cp.start()
        for cp in sends:
            cp.wait_send()
        for cp in recvs:
            cp.wait_recv()

    return pl.pallas_call(
        body, name=name, in_specs=[_ANY] * n, out_specs=[_ANY] * n,
        out_shape=[jax.ShapeDtypeStruct(a.shape, a.dtype) for a in arrs], input_output_aliases={i: i for i in range(n)},
        scratch_shapes=[pltpu.SemaphoreType.DMA((n, 3)), pltpu.SemaphoreType.DMA((n, 3))],
    )(*arrs)


def _scatter_core(grads, name):
    n = len(grads)

    def body(*refs):
        g_refs, got_refs = refs[:n], refs[n:2 * n]
        send_sems, recv_sems = refs[2 * n:]
        x, y, c = _place()
        sends = [pltpu.make_async_remote_copy(
            src_ref=g_refs[a].at[2 * q + 1 - c], dst_ref=got_refs[a].at[q], send_sem=send_sems.at[a, q],
            recv_sem=recv_sems.at[a, q], device_id=(x, y, 1 - c), device_id_type=MESH_ID) for a in range(n) for q in range(4)]
        for cp in sends:
            cp.start()
        for cp in sends:
            cp.wait()

    return pl.pallas_call(
        body, name=name, in_specs=[_ANY] * n, out_specs=[_ANY] * n,
        out_shape=[jax.ShapeDtypeStruct((4,) + g.shape[1:], g.dtype) for g in grads],
        scratch_shapes=[pltpu.SemaphoreType.DMA((n, 4)), pltpu.SemaphoreType.DMA((n, 4))],
    )(*grads)


def _scatter_chips(parts, name):
    n = len(parts)

    def body(*refs):
        p_refs, o_refs = refs[:n], refs[n:2 * n]
        send_sems, recv_sems = refs[2 * n:]
        x, y, c = _place()
        chips = [(1 - x, y), (x, 1 - y), (1 - x, 1 - y)]
        sends = [pltpu.make_async_remote_copy(
            src_ref=p_refs[a].at[2 * px + py], dst_ref=o_refs[a].at[j], send_sem=send_sems.at[a, j],
            recv_sem=recv_sems.at[a, j], device_id=(px, py, c), device_id_type=MESH_ID)
            for a in range(n) for j, (px, py) in enumerate(chips)]
        for cp in sends:
            cp.start()
        for cp in sends:
            cp.wait()

    return pl.pallas_call(
        body, name=name, in_specs=[_ANY] * n, out_specs=[_ANY] * n,
        out_shape=[jax.ShapeDtypeStruct((3,) + p.shape[1:], p.dtype) for p in parts],
        scratch_shapes=[pltpu.SemaphoreType.DMA((n, 3)), pltpu.SemaphoreType.DMA((n, 3))],
    )(*parts)


GATHER_PEERS = [(0, 0, 1), (1, 0, 0), (0, 1, 0), (1, 1, 0)]
CHIP_PEERS = [(1, 0, 0), (0, 1, 0), (1, 1, 0)]
CORE_PEERS = [(0, 0, 1)] * 4
_HBM = pl.BlockSpec(memory_space=pltpu.HBM)
_SEM = pl.BlockSpec(memory_space=pltpu.SEMAPHORE)
_EFFECT = pltpu.SideEffectType.DATAFLOW_SIDE_EFFECTING


def _push_copies(src_refs, land_refs, send_sems, recv_sems, peers, src_of, slot_of):
    place = _place()
    flip = lambda v, f: 1 - v if f else v
    return [pltpu.make_async_remote_copy(
        src_ref=src_of(src_refs[a], k), dst_ref=land_refs[a].at[slot_of(k)], send_sem=send_sems[a], recv_sem=recv_sems[a],
        device_id=tuple(flip(v, f) for v, f in zip(place, peer)), device_id_type=MESH_ID)
        for a in range(len(src_refs)) for k, peer in enumerate(peers)]


def _push_start(srcs, land_shapes, peers, src_of, slot_of, name, after=None):
    n = len(srcs)
    extra = [] if after is None else [after]

    def body(*refs):
        src_refs, land_refs = refs[:n], refs[n:2 * n]
        refs = refs[2 * n + len(extra):]
        send_sems, recv_sems = refs[:n], refs[n:2 * n]
        token = refs[-1]
        for cp in _push_copies(src_refs, land_refs, send_sems, recv_sems, peers, src_of, slot_of):
            cp.start()
        token[...] = jnp.zeros_like(token)

    sems = [pltpu.SemaphoreType.DMA(())] * (2 * n)
    lands = [pltpu.with_memory_space_constraint(lax.empty(s.shape, s.dtype), pltpu.HBM) for s in land_shapes]
    res = pl.pallas_call(
        body, name=name, in_specs=[_HBM] * (2 * n) + [_ANY] * len(extra),
        out_specs=[_SEM] * (2 * n) + [_HBM] * (2 * n) + [pl.BlockSpec(memory_space=pltpu.VMEM)],
        out_shape=sems + [pltpu.HBM(s.shape, s.dtype) for s in srcs] + [pltpu.HBM(s.shape, s.dtype) for s in land_shapes]
        + [jax.ShapeDtypeStruct((8, 128), F32)],
        input_output_aliases={i: 2 * n + i for i in range(2 * n)},
        compiler_params=pltpu.CompilerParams(has_side_effects=_EFFECT),
    )(*[pltpu.with_memory_space_constraint(s, pltpu.HBM) for s in srcs], *lands, *extra)
    return list(res[:n]), list(res[n:2 * n]), list(res[2 * n:3 * n]), list(res[3 * n:4 * n]), res[-1]


def _push_wait(send_sems, recv_sems, srcs, lands, after, peers, src_of, slot_of, name):
    n = len(srcs)

    def body(*refs):
        src_refs, land_refs = refs[:n], refs[n:2 * n]
        s_sems, r_sems = refs[2 * n:3 * n], refs[3 * n:4 * n]
        copies = _push_copies(src_refs, land_refs, s_sems, r_sems, peers, src_of, slot_of)
        for cp in copies:
            cp.wait_send()
        for cp in copies:
            cp.wait_recv()

    res = pl.pallas_call(
        body, name=name, in_specs=[_HBM] * (2 * n) + [_SEM] * (2 * n) + [_ANY], out_specs=[_HBM] * (2 * n),
        out_shape=[pltpu.HBM(s.shape, s.dtype) for s in srcs] + [pltpu.HBM(s.shape, s.dtype) for s in lands],
        input_output_aliases={i: i for i in range(2 * n)},
        compiler_params=pltpu.CompilerParams(has_side_effects=_EFFECT),
    )(*srcs, *lands, *send_sems, *recv_sems, after)
    return list(res[:n]), list(res[n:])


def _pick_sum(picked, rest, index, pick_of, out_dtype, name):
    nq, r, cdim = rest.shape
    one = nq == 3
    tr = _row_tile(r, 512, 16)
    tc = 512 if cdim % 512 == 0 else cdim
    grid = (1 if one else nq, r // tr, cdim // tc)

    def body(i_ref, p_ref, r_ref, o_ref):
        acc = p_ref[0].astype(F32)
        if one:
            for j in range(3):
                acc = acc + r_ref[j].astype(F32)
            o_ref[...] = acc.astype(out_dtype)
        else:
            o_ref[0] = (acc + r_ref[0].astype(F32)).astype(out_dtype)

    spec = pltpu.PrefetchScalarGridSpec(
        num_scalar_prefetch=1, grid=grid,
        in_specs=[pl.BlockSpec((1, tr, tc), lambda q, i, j, i_ref: (pick_of(q, i_ref[0]), i, j)),
                  pl.BlockSpec((3, tr, tc), lambda q, i, j, i_ref: (0, i, j)) if one else pl.BlockSpec((1, tr, tc), lambda q, i, j, i_ref: (q, i, j))],
        out_specs=pl.BlockSpec((tr, tc), lambda q, i, j, i_ref: (i, j)) if one else pl.BlockSpec((1, tr, tc), lambda q, i, j, i_ref: (q, i, j)))
    return pl.pallas_call(
        body, name=name, grid_spec=spec,
        out_shape=jax.ShapeDtypeStruct((r, cdim) if one else (nq, r, cdim), out_dtype),
        compiler_params=_params(("parallel", "parallel", "parallel")),
    )(index.astype(jnp.int32).reshape(1), picked, rest)


def _adamw_fn(w, g, m, v):
    m = ADAM_B1 * m + (1.0 - ADAM_B1) * g
    v = ADAM_B2 * v + (1.0 - ADAM_B2) * jnp.square(g)
    m_hat = m / (1.0 - ADAM_B1 ** ADAM_STEP)
    v_hat = v / (1.0 - ADAM_B2 ** ADAM_STEP)
    delta = -ADAM_LR * (m_hat / (jnp.sqrt(v_hat) + ADAM_EPS) + ADAM_WD * w)
    return delta, m, v


def _as2d(a):
    return a.reshape(-1, a.shape[-1])


def _adamw_shard(w, g, m, v, name):
    shape = w.shape
    ins = [_as2d(a) for a in (w, g, m, v)]
    cols = ins[0].shape[1]
    outs = _ew(_adamw_fn, [(a, True) for a in ins], [(cols, F32, "tile")] * 3, 256, name)
    return [o.reshape(shape) for o in outs]


def _adamw_small(ws, gs, ms, vs, name):
    shapes = [w.shape for w in ws]
    flat = lambda a: a.reshape(-1, 128) if a.size % 128 == 0 else a.reshape(1, -1)
    ins = [flat(a) for grp in zip(ws, gs, ms, vs) for a in grp]
    k = len(ws)

    def fn(*vals):
        out = []
        for i in range(k):
            out += list(_adamw_fn(*vals[4 * i:4 * i + 4]))
        return out

    outs = _whole(fn, ins, [(ins[4 * (i // 3)].shape, F32) for i in range(3 * k)], name)
    deltas = [outs[3 * i].reshape(shapes[i]) for i in range(k)]
    new_m = [outs[3 * i + 1].reshape(shapes[i]) for i in range(k)]
    new_v = [outs[3 * i + 2].reshape(shapes[i]) for i in range(k)]
    return deltas, new_m, new_v


def _sum8(stacked, name):
    def fn(a):
        s = a[0:1]
        for i in range(1, N_DEV):
            s = s + a[i:i + 1]
        return s
    w = stacked.shape[1]
    tw = 8192
    if w % tw:
        return _whole(fn, [stacked], [((1, w), F32)], name)[0]

    def body(a_ref, o_ref):
        o_ref[...] = fn(a_ref[...])

    return pl.pallas_call(body, name=name, grid=(w // tw,), in_specs=[pl.BlockSpec((N_DEV, tw), lambda i: (0, i))],
                          out_specs=pl.BlockSpec((1, tw), lambda i: (0, i)), out_shape=jax.ShapeDtypeStruct((1, w), F32))(stacked)


ROWS = 512


def _split_heads(p, nh):
    t = p.shape[0]
    return p.reshape(t, nh, p.shape[1] // nh).transpose(1, 0, 2)


def _merge_heads(p):
    nh, t, d = p.shape
    return p.transpose(1, 0, 2).reshape(t, nh * d)


def _layer_fwd(h, mod, w, small, rope, l, late=None):
    sh1, sc1, gt1, sh2, sc2, gt2 = mod
    rope_ret, rope_mla = rope
    t = h.shape[0]
    nm = lambda s: f"l{l}_{s}"
    a1 = _ew(_norm_mod_fn, [(h, True), (small["norm1_g"], False), (sc1, False), (sh1, False)], [(D_MODEL, BF16, "tile")], ROWS, nm("norm1"))[0]
    p_s5 = _mm(a1, w["w_in_t"], "nt", 512, 512, 2048, name=nm("proj_s5"), n=512)
    p_ret = _mm(a1, w["w_in_t"], "nt", 512, 512, 2048, name=nm("proj_ret"), b_off=1, n=1536)
    p_swa = _mm(a1, w["w_in_t"], "nt", 512, 256, 2048, name=nm("proj_swa"), b_off=8, n=768)
    p_mla = _mm(a1, w["w_in_t"][2816:], "nt", 512, 576, 2048, name=nm("proj_mla"))
    b3, c3, dskip, a_r, a_i = small["s5"]
    st_r, st_i = _s5_scan_fwd(p_s5, b3, a_r, a_i, nm("s5_scan"))
    ypre = _mm_blocks(st_r, c3[:S5_BLOCKS], "nn", 512, pair=(st_i, c3[S5_BLOCKS:]), name=nm("s5_y"))
    z = _ew(_s5_act_fn, [(ypre, True), (p_s5, True), (dskip, False)], [(GROUP_WIDTH, F32, "tile")], ROWS, nm("s5_act"))[0]
    zz = _mm(z, w["glu_w"], "nn", 512, 512, 512, name=nm("s5_zz"))
    y_s5 = _ew(_s5_glu_fn, [(z, True), (zz, True), (small["s5_glu_b"], False)], [(GROUP_WIDTH, BF16, "tile")], ROWS, nm("s5_glu"))[0]
    qk_ret = _split_heads(_rope(p_ret[:, :2 * RET_HEADS * RET_QK], rope_ret, nm("ret_rope")), 2 * RET_HEADS)
    o_ret, y_ret = _ret_fwd(qk_ret, p_ret, small["ret_lgam"], nm("ret"))
    qkv_swa = _split_heads(p_swa, 12)
    o_swa, lse_swa = _swa_fwd(qkv_swa, small["swa_sinks"], nm("swa"))
    y_swa = _merge_heads(o_swa).astype(BF16)
    cq, ckv, kr = p_mla[:, :MLA_Q_RANK], p_mla[:, MLA_Q_RANK:MLA_Q_RANK + MLA_KV_RANK], p_mla[:, MLA_Q_RANK + MLA_KV_RANK:]
    cqn = _ew(_rms_gain_fn, [(cq, True), (small["mla_q_norm"], False)], [(MLA_Q_RANK, BF16, "tile")], ROWS, nm("mla_qnorm"))[0]
    ckvn = _ew(_rms_gain_fn, [(ckv, True), (small["mla_kv_norm"], False)], [(MLA_KV_RANK, BF16, "tile")], ROWS, nm("mla_kvnorm"))[0]
    q_full = _mm(cqn, w["w_uq_t"], "nt", 512, 768, 384, name=nm("mla_q"))
    kv_full = _mm(ckvn, w["w_ukv_t"], "nt", 512, 1024, 128, BF16, name=nm("mla_kv"))
    nq = q_full.shape[1]
    roped = _rope(jnp.concatenate([q_full, kr, jnp.zeros_like(kr)], axis=1), rope_mla, nm("mla_rope"), out_dtype=BF16)
    q4 = roped[:, :nq].reshape(t, MLA_HEADS, MLA_NOPE + MLA_ROPE)
    qn = q4[:, :, :MLA_NOPE].transpose(1, 0, 2)
    rot = jnp.concatenate([q4[:, :, MLA_NOPE:].transpose(1, 0, 2), roped[None, :, nq:nq + MLA_ROPE]], axis=0)
    o_mla, lse_mla = _mla_attend(qn, rot, kv_full, nm("mla"))
    cat = jnp.concatenate([y_s5, y_ret, y_swa, o_mla.astype(BF16)], axis=1)
    if late is not None:
        w = {**w, **late(lse_mla)}
    mixed = _mm(cat, w["w_out"], "nn", 512, 1024, 2048, name=nm("out_proj"))
    h1 = _ew(_gate_add_fn, [(h, True), (mixed, True), (gt1, False)], [(D_MODEL, F32, "tile")], ROWS, nm("res1"))[0]
    a2 = _ew(_norm_mod_fn, [(h1, True), (small["norm2_g"], False), (sc2, False), (sh2, False)], [(D_MODEL, BF16, "tile")], ROWS, nm("norm2"))[0]
    hid, act = _mm(a2, w["w1_t"], "nt", 1024, 1024, 2048, name=nm("mlp1"), epi=lambda acc: (acc, _relu2_fn(acc)), epi_outs=[F32, BF16])
    mo = _mm(act, w["w2"], "nn", 1024, 1024, 2048, name=nm("mlp2"))
    h2 = _ew(_gate_add_fn, [(h1, True), (mo, True), (gt2, False)], [(D_MODEL, F32, "tile")], ROWS, nm("res2"))[0]
    saved = dict(w=w, h=h, a1=a1, p_s5=p_s5, p_ret=p_ret, st_r=st_r, st_i=st_i, ypre=ypre, z=z, zz=zz, qk_ret=qk_ret, o_ret=o_ret,
                 qkv_swa=qkv_swa, o_swa=o_swa, lse_swa=lse_swa, cq=cq, ckv=ckv, cqn=cqn, ckvn=ckvn, qn=qn, rot=rot,
                 kv_full=kv_full, o_mla=o_mla, lse_mla=lse_mla, cat=cat, mixed=mixed, h1=h1, a2=a2, hid=hid, act=act, mo=mo)
    return h2, saved


def _layer_bwd(dh2, mod, w, small, rope, s, l, after_mlp=None, after_out=None):
    sh1, sc1, gt1, sh2, sc2, gt2 = mod
    rope_ret, rope_mla = rope
    t = dh2.shape[0]
    nm = lambda n: f"l{l}_{n}_bwd"
    gb, gs = {}, {}
    row = (D_MODEL, F32, "acc")
    dmo, dgt2 = _ew(lambda d, y, gt: (d * gt, jnp.sum(d * y, axis=0, keepdims=True)),
                    [(dh2, True), (s["mo"], True), (gt2, False)], [(D_MODEL, BF16, "tile"), row], ROWS, nm("res2"))
    dhid = _mm(dmo, w["w2"], "nt", 1024, 1024, 2048, name=nm("mlp2_x"), epi=lambda acc, x: (acc * 2.0 * jnp.maximum(x, 0.0),),
               epi_ins=[s["hid"]], epi_outs=[BF16])[0]
    gb["w2"] = _mm(s["act"], dmo, "tn", 1024, 1024, 4096, BF16, name=nm("mlp2_w"))
    da2 = _mm(dhid, w["w1_t"], "nn", 1024, 1024, 2048, name=nm("mlp1_x"))
    gb["w1_t"] = _mm(dhid, s["a2"], "tn", 1024, 1024, 4096, BF16, name=nm("mlp1_w"))
    if after_mlp is not None:
        gt1 = gt1 + after_mlp(gb)

    def norm_bwd(hh, g, sc, sh, da, dres):
        dh_, dg, dsc, dsh = _vjp_block(_norm_mod_fn, 4)(hh, g, sc, sh, da)
        return dh_ + dres, dg, dsc, dsh

    dh1, gs["norm2_g"], dsc2, dsh2 = _ew(norm_bwd, [(s["h1"], True), (small["norm2_g"], False), (sc2, False), (sh2, False), (da2, True), (dh2, True)],
                                         [(D_MODEL, F32, "tile"), row, row, row], ROWS, nm("norm2"))
    dmixed, dgt1 = _ew(lambda d, y, gt: (d * gt, jnp.sum(d * y, axis=0, keepdims=True)),
                       [(dh1, True), (s["mixed"], True), (gt1, False)], [(D_MODEL, BF16, "tile"), row], ROWS, nm("res1"))
    dcat = _mm(dmixed, w["w_out"], "nt", 512, 1024, 2048, name=nm("out_proj_x"))
    gb["w_out"] = _mm(s["cat"], dmixed, "tn", 1024, 1024, 4096, BF16, name=nm("out_proj_w"))
    if after_out is not None:
        small = {**small, "s5_glu_b": small["s5_glu_b"] + after_out(gb)}
    dy_s5, dy_ret, dy_swa, dy_mla = (dcat[:, i * GROUP_WIDTH:(i + 1) * GROUP_WIDTH] for i in range(4))
    b3, c3, dskip, a_r, a_i = small["s5"]
    gw = (GROUP_WIDTH, F32, "tile")
    gacc = (GROUP_WIDTH, F32, "acc")
    dz_a, dzz, gs["s5_glu_b"] = _ew(_vjp_block(_s5_glu_fn, 3), [(s["z"], True), (s["zz"], True), (small["s5_glu_b"], False), (dy_s5, True)],
                                    [gw, gw, gacc], ROWS, nm("s5_glu"))
    dz_b = _mm(dzz, w["glu_w"], "nt", 512, 512, 512, name=nm("s5_zz_x"))
    gb["glu_w"] = _mm(s["z"], dzz, "tn", 512, 512, 1024, BF16, name=nm("s5_zz_w"))

    def act_bwd(ypre, u, dsk, dza, dzb):
        return _vjp_block(_s5_act_fn, 3)(ypre, u, dsk, dza + dzb)

    dypre, du_a, g_dskip = _ew(act_bwd, [(s["ypre"], True), (s["p_s5"], True), (dskip, False), (dz_a, True), (dz_b, True)],
                               [gw, gw, gacc], ROWS, nm("s5_act"))
    ch, st = S5_PACK * S5_CH, S5_PACK * S5_STATE
    g_c3 =jnp.concatenate([_mm_blocks_tn(s["st_r"], dypre, st, ch, lambda j: j, nm("s5_y_w_re")),
                            _mm_blocks_tn(s["st_i"], dypre, st, ch, lambda j: j, nm("s5_y_w_im"))], axis=0)
    dbu_r, dbu_i, g_ar, g_ai = _s5_scan_bwd(dypre, c3, s["st_r"], s["st_i"], a_r, a_i, nm("s5_scan"))
    du_b = _mm_blocks(dbu_r, b3[:S5_BLOCKS], "nt", 512, pair=(dbu_i, b3[S5_BLOCKS:]), name=nm("s5_bu_x"))
    g_b3 = jnp.concatenate([_mm_blocks_tn(s["p_s5"], dbu_r, ch, st, lambda j: j, nm("s5_bu_w_re")),
                            _mm_blocks_tn(s["p_s5"], dbu_i, ch, st, lambda j: j, nm("s5_bu_w_im"))], axis=0)
    gs["s5"] = (g_b3, g_c3, g_dskip, g_ar, g_ai)
    dqk_rot, dk_rot, dv_ret, dg_ret = _ret_bwd(s["qk_ret"], s["p_ret"], s["o_ret"], dy_ret, small["ret_lgam"], nm("ret"))
    dqk = _rope(_merge_heads(jnp.concatenate([dqk_rot, dk_rot], axis=0)), rope_ret, nm("ret_rope"), inverse=True, out_dtype=BF16)
    dqkv_swa, gs["swa_sinks"] = _swa_bwd(s["qkv_swa"], s["o_swa"], s["lse_swa"], _split_heads(dy_swa, SWA_HEADS), small["swa_sinks"], nm("swa"))
    dqn, dqr, dkn, dv_mla, dkr_heads = _mla_attend_bwd(s["qn"], s["rot"], s["kv_full"], s["lse_mla"], s["o_mla"], dy_mla, nm("mla_att"))
    dkv_full = jnp.stack([dkn.reshape(t, MLA_HEADS, MLA_NOPE), dv_mla.reshape(t, MLA_HEADS, MLA_V)], axis=2).reshape(t, 2 * MLA_HEADS * MLA_NOPE)
    dkr_rot = _ew(lambda a, b, c, d: a + b + c + d, [(dkr_heads[i], True) for i in range(MLA_HEADS)], [(MLA_ROPE, F32, "tile")], ROWS, nm("mla_dkr"))[0]
    nq = MLA_HEADS * (MLA_NOPE + MLA_ROPE)
    dq_rot = jnp.concatenate([dqn.transpose(1, 0, 2), dqr.transpose(1, 0, 2)], axis=2).reshape(t, nq)
    droped = _rope(jnp.concatenate([dq_rot, dkr_rot, jnp.zeros_like(dkr_rot)], axis=1), rope_mla, nm("mla_rope"), inverse=True, out_dtype=BF16)
    dq_full, dkr = droped[:, :nq], droped[:, nq:nq + MLA_ROPE]
    dcqn = _mm(dq_full, w["w_uq_t"], "nn", 512, 384, 768, name=nm("mla_q_x"))
    gb["w_uq_t"] = _mm(dq_full, s["cqn"], "tn", 768, 384, 1024, BF16, name=nm("mla_q_w"))
    dckvn = _mm(dkv_full, w["w_ukv_t"], "nn", 512, 128, 1024, name=nm("mla_kv_x"))
    gb["w_ukv_t"] = _mm(dkv_full, s["ckvn"], "tn", 1024, 128, 1024, BF16, name=nm("mla_kv_w"))
    dcq, gs["mla_q_norm"] = _ew(_vjp_block(_rms_gain_fn, 2), [(s["cq"], True), (small["mla_q_norm"], False), (dcqn, True)],
                                [(MLA_Q_RANK, BF16, "tile"), (MLA_Q_RANK, F32, "acc")], ROWS, nm("mla_qnorm"))
    dckv, gs["mla_kv_norm"] = _ew(_vjp_block(_rms_gain_fn, 2), [(s["ckv"], True), (small["mla_kv_norm"], False), (dckvn, True)],
                                  [(MLA_KV_RANK, BF16, "tile"), (MLA_KV_RANK, F32, "acc")], ROWS, nm("mla_kvnorm"))
    du = _ew(lambda a, b: a + b, [(du_a, True), (du_b, True)], [(GROUP_WIDTH, BF16, "tile")], ROWS, nm("s5_du"))[0]
    bf = lambda a: a.astype(BF16)
    dproj = jnp.concatenate([du, bf(dqk), bf(dv_ret), bf(dg_ret), bf(_merge_heads(dqkv_swa)), bf(dcq), bf(dckv), bf(dkr)], axis=1)
    da1 = _mm(dproj, w["w_in_t"], "nn", 512, 1024, N_IN, name=nm("proj_x"))
    gb["w_in_t"] = _mm(dproj, s["a1"], "tn", N_IN, 512, 2048, BF16, name=nm("proj_w"))
    dh, gs["norm1_g"], dsc1, dsh1 = _ew(norm_bwd, [(s["h"], True), (small["norm1_g"], False), (sc1, False), (sh1, False), (da1, True), (dh1, True)],
                                        [(D_MODEL, F32, "tile"), row, row, row], ROWS, nm("norm1"))
    dmod = jnp.concatenate([dsh1, dsc1, dgt1, dsh2, dsc2, dgt2], axis=1)
    return dh, gb, gs, dmod


BIG = ("w_in_t", "w1_t", "w_uq_t", "w_ukv_t", "w_out", "w2", "glu_w")
MLP_BIG = ("w1_t", "w2")
OUT_BIG = ("w_out",)
LATE_BIG = ("w_out", "w1_t", "w2")
S5_NAMES = ("s5_lambda_re", "s5_lambda_im", "s5_log_dt", "s5_b_re", "s5_b_im", "s5_c_re", "s5_c_im", "s5_d")


def kernel(x, c, norm1_g, norm2_g, ada_w, ada_b, w_in, s5_lambda_re, s5_lambda_im, s5_log_dt, s5_b_re, s5_b_im, s5_c_re, s5_c_im, s5_d, s5_glu_w, s5_glu_b, swa_sinks, mla_q_norm, mla_kv_norm, mla_w_uq, mla_w_ukv, w_out, mlp_w1, mlp_w2, final_norm_g, loss_target, m_norm1_g, m_norm2_g, m_ada_w, m_ada_b, m_w_in, m_s5_lambda_re, m_s5_lambda_im, m_s5_log_dt, m_s5_b_re, m_s5_b_im, m_s5_c_re, m_s5_c_im, m_s5_d, m_s5_glu_w, m_s5_glu_b, m_swa_sinks, m_mla_q_norm, m_mla_kv_norm, m_mla_w_uq, m_mla_w_ukv, m_w_out, m_mlp_w1, m_mlp_w2, m_final_norm_g, v_norm1_g, v_norm2_g, v_ada_w, v_ada_b, v_w_in, v_s5_lambda_re, v_s5_lambda_im, v_s5_log_dt, v_s5_b_re, v_s5_b_im, v_s5_c_re, v_s5_c_im, v_s5_d, v_s5_glu_w, v_s5_glu_b, v_swa_sinks, v_mla_q_norm, v_mla_kv_norm, v_mla_w_uq, v_mla_w_ukv, v_w_out, v_mlp_w1, v_mlp_w2, v_final_norm_g):
    names = ["norm1_g", "norm2_g", "ada_w", "ada_b", "w_in", "s5_lambda_re", "s5_lambda_im", "s5_log_dt", "s5_b_re", "s5_b_im",
             "s5_c_re", "s5_c_im", "s5_d", "s5_glu_w", "s5_glu_b", "swa_sinks", "mla_q_norm", "mla_kv_norm", "mla_w_uq",
             "mla_w_ukv", "w_out", "mlp_w1", "mlp_w2", "final_norm_g"]
    env = locals()
    wts = {n: env[n] for n in names}
    mom = {n: env["m_" + n] for n in names}
    var = {n: env["v_" + n] for n in names}
    t = x.shape[1]
    me = 4 * lax.axis_index("x") + 2 * lax.axis_index("y") + lax.axis_index("c")
    rope = _rope_tables(t)
    ret_lgam = jnp.log1p(-(2.0 ** (-5.0 - jnp.arange(RET_HEADS, dtype=F32))))

    tr = lambda a: a.transpose(0, 2, 1)
    shard = {"w_in_t": tr(w_in), "w1_t": tr(mlp_w1), "w_uq_t": tr(mla_w_uq), "w_ukv_t": tr(mla_w_ukv),
             "w_out": w_out, "w2": mlp_w2, "glu_w": s5_glu_w}
    to_send = [{k: shard[k][l].astype(BF16) for k in BIG} for l in range(DEPTH)]
    as_rows = lambda keys, arrs: {k: a.reshape(-1, shard[k].shape[2]) for k, a in zip(keys, arrs)}
    first = [k for k in BIG if k not in LATE_BIG]
    gathered = _all_gather([to_send[0][k] for k in first] + [c], "gather_weights_first")
    c_all = gathered[-1].reshape(N_DEV, D_MODEL)
    big = [as_rows(first, gathered[:len(first)]), None]
    own_slot = lambda k: 4 * lax.axis_index("x") + 2 * lax.axis_index("y") + lax.axis_index("c")

    def gather_start(arrs, tag):
        return _push_start(arrs, [jax.ShapeDtypeStruct((N_DEV,) + a.shape, a.dtype) for a in arrs], GATHER_PEERS,
                           lambda ref, k: ref, own_slot, f"gather_weights_{tag}_start")

    def gather_finish(started, after, tag):
        sent, landed = _push_wait(started[0], started[1], started[2], started[3], after, GATHER_PEERS, lambda ref, k: ref, own_slot,
                                  f"gather_weights_{tag}_wait")
        with_own = [lax.dynamic_update_index_in_dim(full, own, me, 0) for full, own in zip(landed, sent)]
        return _pass_to_sibling(with_own, f"gather_weights_{tag}_pass")

    gather0 = gather_start([to_send[0][k] for k in LATE_BIG], "l0")
    gather1 = gather_start([to_send[1][k] for k in BIG], "l1")

    c_act = _whole(lambda v: v * jax.nn.sigmoid(v), [c_all], [((N_DEV, D_MODEL), F32)], "cond_silu")[0]
    c_pad = jnp.concatenate([c_act, jnp.zeros((128 - N_DEV, D_MODEL), F32)], axis=0)
    cols = ada_w.shape[2]
    mod_part = [_mm(c_pad, ada_w[l], "nn", 128, cols, 512, name=f"l{l}_mod")[:N_DEV] for l in range(DEPTH)]
    mod_all = _all_gather([jnp.stack(mod_part)], "gather_mod")[0]
    mod_rows = lax.dynamic_index_in_dim(mod_all, me, axis=2, keepdims=False)
    mods = []
    for l in range(DEPTH):
        row = mod_rows[:, l].reshape(1, 6 * D_MODEL) + ada_b[l][None]
        if l == 0:
            row = row + (gather0[4][0, 0] + gather1[4][0, 0])
        mods.append([row[:, i * D_MODEL:(i + 1) * D_MODEL] for i in range(6)])

    smalls, s5_pulls = [], []
    for l in range(DEPTH):
        s5_ops, pull = jax.vjp(_s5_prep, *[wts[n][l] for n in S5_NAMES])
        s5_pulls.append(pull)
        smalls.append(dict(norm1_g=norm1_g[l][None], norm2_g=norm2_g[l][None], s5=s5_ops, s5_glu_b=s5_glu_b[l][None],
                           swa_sinks=swa_sinks[l], mla_q_norm=mla_q_norm[l][None], mla_kv_norm=mla_kv_norm[l][None], ret_lgam=ret_lgam))
    h = x[0]
    saved = []
    for l in range(DEPTH):
        if l == 0:
            late = lambda after: as_rows(LATE_BIG, gather_finish(gather0, after, "l0"))
        else:
            big[1] = as_rows(BIG, gather_finish(gather1, h, "l1"))
            late = None
        h, s = _layer_fwd(h, mods[l], big[l], smalls[l], rope, l, late)
        big[l] = s.pop("w")
        saved.append(s)

    fg = final_norm_g[None]
    tgt = loss_target[0]
    loss_local = _ew(_final_fn, [(h, True), (fg, False), (tgt, True)], [(1, F32, "acc")], ROWS, "loss")[0]

    def final_bwd(hh, g, tg):
        dh_, dg, _ = _vjp_block(_final_fn, 3)(hh, g, tg, jnp.ones((1, 1), F32))
        return dh_, dg

    dh, g_final = _ew(final_bwd, [(h, True), (fg, False), (tgt, True)], [(D_MODEL, F32, "tile"), (D_MODEL, F32, "acc")], ROWS, "loss_bwd")
    loss = lax.psum(loss_local[0, 0], ("x", "y", "c"))

    core, chip = lax.axis_index("c"), 2 * lax.axis_index("x") + lax.axis_index("y")

    def core_stage(g_layer, keys, tag):
        g_list = [g_layer[k].reshape(N_DEV, -1, g_layer[k].shape[1]) for k in keys]
        got = _scatter_core(g_list, f"scatter_core_{tag}")
        return [_pick_sum(g, o, core, lambda q, c_: 2 * q + c_, BF16, f"{tag}_core_sum_{k}") for k, g, o in zip(keys, g_list, got)]

    def their_block(ref, k):
        x_, y_ = lax.axis_index("x"), lax.axis_index("y")
        dx, dy, _ = CHIP_PEERS[k]
        return ref.at[2 * (1 - x_ if dx else x_) + (1 - y_ if dy else y_)]

    def chips_start(halves, tag, after=None):
        return _push_start(halves, [jax.ShapeDtypeStruct((3,) + a.shape[1:], a.dtype) for a in halves], CHIP_PEERS,
                           their_block, lambda k: k, f"scatter_chips_{tag}_start", after)

    def chips_wait(started, after, tag):
        return _push_wait(started[0], started[1], started[2], started[3], after, CHIP_PEERS, their_block, lambda k: k, f"scatter_chips_{tag}_wait")

    def sibling_block(ref, q):
        return ref.at[2 * q + 1 - lax.axis_index("c")]

    g_small, dmods = [None] * DEPTH, [None] * DEPTH
    dh, g_big1, g_small[1], dmods[1] = _layer_bwd(dh, mods[1], big[1], smalls[1], rope, saved[1], 1)
    g_list1 = [g_big1[k].reshape(N_DEV, -1, g_big1[k].shape[1]) for k in BIG]
    core1 = _push_start(g_list1, [jax.ShapeDtypeStruct((4,) + g.shape[1:], g.dtype) for g in g_list1], CORE_PEERS,
                        sibling_block, lambda q: q, "scatter_core_l1_start")
    mods0 = [m + core1[4][0, 0] for m in mods[0]]
    early = {}

    def after_mlp(gb):
        g_mine, got = _push_wait(core1[0], core1[1], core1[2], core1[3], gb["w1_t"], CORE_PEERS, sibling_block, lambda q: q,
                                 "scatter_core_l1_wait")
        halves = [_pick_sum(g, o, core, lambda q, c_: 2 * q + c_, BF16, f"l1_core_sum_{k}") for k, g, o in zip(BIG, g_mine, got)]
        early["l1"] = chips_start(halves, "l1")
        g_list = [gb[k].reshape(N_DEV, -1, gb[k].shape[1]) for k in MLP_BIG]
        early["mlp_core"] = _push_start(g_list, [jax.ShapeDtypeStruct((4,) + g.shape[1:], g.dtype) for g in g_list], CORE_PEERS,
                                        sibling_block, lambda q: q, "scatter_core_l0_mlp_start")
        return early["l1"][4][0, 0] + early["mlp_core"][4][0, 0]

    def after_out(gb):
        started = early["mlp_core"]
        g_mine, got = _push_wait(started[0], started[1], started[2], started[3], gb["w_out"], CORE_PEERS, sibling_block, lambda q: q,
                                 "scatter_core_l0_mlp_wait")
        halves = [_pick_sum(g, o, core, lambda q, c_: 2 * q + c_, BF16, f"l0_mlp_core_sum_{k}") for k, g, o in zip(MLP_BIG, g_mine, got)]
        early["mlp"] = chips_start(halves, "l0_mlp")
        early["out"] = chips_start(core_stage(gb, OUT_BIG, "l0_out"), "l0_out")
        return early["mlp"][4][0, 0] + early["out"][4][0, 0]

    dh, g_big0, g_small[0], dmods[0] = _layer_bwd(dh, mods0, big[0], smalls[0], rope, saved[0], 0, after_mlp=after_mlp, after_out=after_out)
    grad_x = dh[None]
    rest = [k for k in BIG if k not in MLP_BIG + OUT_BIG]
    halves_rest = core_stage(g_big0, rest, "l0_rest")

    small_parts = []
    for l in range(DEPTH):
        gs = g_small[l]
        s5g = s5_pulls[l](gs["s5"])
        small_parts += [gs["norm1_g"], gs["norm2_g"], *s5g, gs["s5_glu_b"], gs["swa_sinks"], gs["mla_q_norm"], gs["mla_kv_norm"]]
    small_parts += [g_final, *dmods]
    sizes = [int(np.prod(p.shape)) for p in small_parts]
    flat = jnp.concatenate([p.reshape(1, -1) for p in small_parts], axis=1)
    pad = (-flat.shape[1]) % 8192
    flat = jnp.pad(flat, ((0, 0), (0, pad)))
    flat_all = _all_gather([flat], "gather_small_grads")[0].reshape(N_DEV, -1)
    summed = _sum8(flat_all, "sum_small_grads")
    chips_rest = chips_start(halves_rest, "l0_rest", after=summed)
    rest_token = chips_rest[4]
    halves1, landed1 = chips_wait(early["l1"], rest_token, "l1")
    halves_mlp, landed_mlp = chips_wait(early["mlp"], rest_token, "l0_mlp")
    halves_out, landed_out = chips_wait(early["out"], rest_token, "l0_out")
    terms = {(1, k): pair for k, pair in zip(BIG, zip(halves1, landed1))}
    terms.update({(0, k): pair for k, pair in zip(MLP_BIG, zip(halves_mlp, landed_mlp))})
    terms.update({(0, k): pair for k, pair in zip(OUT_BIG, zip(halves_out, landed_out))})
    chip_sum = lambda l, k: _pick_sum(*terms[l, k], chip, lambda q, m_: m_, F32, f"l{l}_chip_sum_{k}")
    pieces, off = [], 0
    for sz in sizes:
        pieces.append(summed[0, off:off + sz])
        off += sz
    small_names = ["norm1_g", "norm2_g", *S5_NAMES, "s5_glu_b", "swa_sinks", "mla_q_norm", "mla_kv_norm"]
    per_layer = len(small_names)
    grads = {}
    for i, n in enumerate(small_names):
        grads[n] = jnp.stack([pieces[l * per_layer + i].reshape(wts[n].shape[1:]) for l in range(DEPTH)])
    grads["final_norm_g"] = pieces[DEPTH * per_layer]
    grads["ada_b"] = jnp.stack([pieces[DEPTH * per_layer + 1 + l] for l in range(DEPTH)])

    mod_off = sum(sizes[:DEPTH * per_layer + 1])
    dmod_all = flat_all[:, mod_off:mod_off + DEPTH * 6 * D_MODEL].reshape(N_DEV, DEPTH, N_DEV, cols)
    dmod_mine = lax.dynamic_index_in_dim(dmod_all, me, axis=2, keepdims=False).transpose(1, 0, 2)
    dmod_pad = jnp.concatenate([dmod_mine, jnp.zeros((DEPTH, 128 - N_DEV, cols), F32)], axis=1)
    grads["ada_w"] = jnp.stack([_mm(c_pad, dmod_pad[l], "tn", 512, cols, 128, name=f"l{l}_ada_w_grad") for l in range(DEPTH)])

    out_g, out_d, out_m, out_v = dict(grads), {}, {}, {}
    orig = {"w_in_t": "w_in", "w1_t": "mlp_w1", "w_uq_t": "mla_w_uq", "w_ukv_t": "mla_w_ukv", "w_out": "w_out", "w2": "mlp_w2", "glu_w": "s5_glu_w"}

    def update_big(keys):
        for k in keys:
            n = orig[k]
            out_g[n] = jnp.stack([chip_sum(l, k) for l in range(DEPTH)])
            if k.endswith("_t"):
                out_g[n] = tr(out_g[n])
            out_d[n], out_m[n], out_v[n] = _adamw_shard(wts[n], out_g[n], mom[n], var[n], f"adamw_{n}")

    update_big(MLP_BIG + OUT_BIG)
    out_d["ada_w"], out_m["ada_w"], out_v["ada_w"] = _adamw_shard(wts["ada_w"], out_g["ada_w"], mom["ada_w"], var["ada_w"], "adamw_ada_w")
    small_all = small_names + ["ada_b", "final_norm_g"]
    ds, ms, vs = _adamw_small([wts[n] for n in small_all], [grads[n] for n in small_all], [mom[n] for n in small_all],
                              [var[n] for n in small_all], "adamw_small")
    for n, d, m_, v_ in zip(small_all, ds, ms, vs):
        out_d[n], out_m[n], out_v[n] = d, m_, v_
    halves_rest, landed_rest = chips_wait(chips_rest, out_d["ada_w"], "l0_rest")
    terms.update({(0, k): pair for k, pair in zip(rest, zip(halves_rest, landed_rest))})
    update_big(rest)
    return (loss, grad_x, *[out_g[n] for n in names], *[out_d[n] for n in names], *[out_m[n] for n in names], *[out_v[n] for n in names])
```

```python
import functools
import math

import numpy as np
import jax
import jax.numpy as jnp
from jax import lax
from jax.experimental import pallas as pl
from jax.experimental.pallas import tpu as pltpu

F32 = jnp.float32
BF16 = jnp.bfloat16
_MXU_DTYPE = jnp.bfloat16

N_DEV = 8
D_MODEL = 2048
DEPTH = 2
GROUP_WIDTH = 512
D_FF = 8192
S5_CH, S5_GROUPS, S5_STATE = 16, 32, 64
S5_WIDTH = S5_GROUPS * S5_STATE
S5_PACK = 8
S5_BLOCKS = S5_GROUPS // S5_PACK
RET_HEADS, RET_QK, RET_V, RET_CHUNK = 4, 64, 128, 128
SWA_HD, SWA_HEADS, SWA_KV_HEADS, WINDOW = 64, 8, 2, 128
MLA_HEADS, MLA_Q_RANK, MLA_KV_RANK, MLA_NOPE, MLA_ROPE, MLA_V = 4, 384, 128, 128, 64, 128
ROPE_BASE = 10000.0
EPS = 1e-6
NEG = -1e30
N_IN = 3392
ADAM_LR, ADAM_B1, ADAM_B2, ADAM_EPS, ADAM_WD, ADAM_STEP = 0.001, 0.9, 0.999, 1e-08, 0.01, 10

VMEM_LIMIT_BYTES = 52 * 1024 * 1024
MESH_ID = pl.DeviceIdType.MESH
_ANY = pl.BlockSpec(memory_space=pl.ANY)
_SMEM = pl.BlockSpec(memory_space=pltpu.SMEM)


def _params(sem):
    return pltpu.CompilerParams(dimension_semantics=sem, vmem_limit_bytes=VMEM_LIMIT_BYTES)


_DIMS = {"nn": (((1,), (0,)), ((), ())), "nt": (((1,), (1,)), ((), ())), "tn": (((0,), (0,)), ((), ()))}


def _dot(a, b, mode="nn"):
    return lax.dot_general(a.astype(_MXU_DTYPE), b.astype(_MXU_DTYPE), _DIMS[mode], preferred_element_type=F32)


def _mm(a, b, mode, tm, tn, tk, out_dtype=F32, name="mm", b_off=0, n=None, pair=None, epi=None, epi_ins=(), epi_outs=None):
    if mode == "tn":
        kdim, m = a.shape
    else:
        m, kdim = a.shape
    if n is None:
        n = b.shape[0] if mode == "nt" else b.shape[1]
    tm, tn, tk = min(tm, m), min(tn, n), min(tk, kdim)
    assert m % tm == 0 and n % tn == 0 and kdim % tk == 0, (name, a.shape, b.shape, tm, tn, tk)
    nk = kdim // tk
    a_spec = pl.BlockSpec((tk, tm), lambda i, j, k: (k, i)) if mode == "tn" else pl.BlockSpec((tm, tk), lambda i, j, k: (i, k))
    if mode == "nt":
        b_spec = pl.BlockSpec((tn, tk), lambda i, j, k: (j + b_off, k))
    else:
        b_spec = pl.BlockSpec((tk, tn), lambda i, j, k: (k, j + b_off))
    o_spec = pl.BlockSpec((tm, tn), lambda i, j, k: (i, j))
    n_mm = 2 if pair is None else 4
    out_dtypes = [out_dtype] if epi is None else list(epi_outs)

    def body(*refs):
        ins, extra = refs[:n_mm], refs[n_mm:n_mm + len(epi_ins)]
        outs = refs[n_mm + len(epi_ins):n_mm + len(epi_ins) + len(out_dtypes)]
        part = _dot(ins[0][...], ins[1][...], mode)
        if pair is not None:
            part = part + _dot(ins[2][...], ins[3][...], mode)

        def finish(acc):
            vals = (acc,) if epi is None else epi(acc, *[r[...] for r in extra])
            for o_ref, v, dt in zip(outs, vals, out_dtypes):
                o_ref[...] = v.astype(dt)

        if nk == 1:
            finish(part)
        else:
            acc_ref = refs[-1]
            k = pl.program_id(2)

            @pl.when(k == 0)
            def _():
                acc_ref[...] = part

            @pl.when(k > 0)
            def _():
                acc_ref[...] += part

            @pl.when(k == nk - 1)
            def _():
                finish(acc_ref[...])

    operands = [a, b] + ([] if pair is None else list(pair)) + list(epi_ins)
    res = pl.pallas_call(
        body, name=name, grid=(m // tm, n // tn, nk),
        in_specs=[a_spec, b_spec] * (n_mm // 2) + [o_spec] * len(epi_ins),
        out_specs=[o_spec] * len(out_dtypes), out_shape=[jax.ShapeDtypeStruct((m, n), dt) for dt in out_dtypes],
        scratch_shapes=[] if nk == 1 else [pltpu.VMEM((tm, tn), F32)],
        compiler_params=_params(("parallel", "parallel", "arbitrary")),
    )(*operands)
    return res[0] if epi is None else res


def _mm_blocks(a, b, mode, tm, a_of=None, pair=None, name="mm_blocks"):
    a_of = a_of or (lambda j: j)
    m = a.shape[0]
    nj, kb, nb = b.shape
    a_w, o_w = (kb, nb) if mode == "nn" else (nb, kb)
    tm = min(tm, m)
    a_spec = pl.BlockSpec((tm, a_w), lambda i, j: (i, a_of(j)))
    b_spec = pl.BlockSpec((1, kb, nb), lambda i, j: (j, 0, 0))
    n_in = 2 if pair is None else 4

    def body(*refs):
        acc = _dot(refs[0][...], refs[1][0], mode)
        if pair is not None:
            acc = acc + _dot(refs[2][...], refs[3][0], mode)
        refs[n_in][...] = acc

    operands = [a, b] + ([] if pair is None else list(pair))
    return pl.pallas_call(
        body, name=name, grid=(m // tm, nj), in_specs=[a_spec, b_spec] * (n_in // 2),
        out_specs=pl.BlockSpec((tm, o_w), lambda i, j: (i, j)), out_shape=jax.ShapeDtypeStruct((m, nj * o_w), F32),
        compiler_params=_params(("parallel", "parallel")),
    )(*operands)


def _mm_blocks_tn(a, b, x, y, b_of, name):
    kdim = a.shape[0]
    nj = a.shape[1] // x

    def body(a_ref, b_ref, o_ref):
        o_ref[0] = _dot(a_ref[...], b_ref[...], "tn")

    return pl.pallas_call(
        body, name=name, grid=(nj,), in_specs=[pl.BlockSpec((kdim, x), lambda j: (0, j)), pl.BlockSpec((kdim, y), lambda j: (0, b_of(j)))],
        out_specs=pl.BlockSpec((1, x, y), lambda j: (j, 0, 0)), out_shape=jax.ShapeDtypeStruct((nj, x, y), F32),
        compiler_params=_params(("parallel",)),
    )(a, b)


SUBLANES = 8


def _row_tile(rows, target, mult=SUBLANES):
    best = None
    for cand in range(mult, min(rows, target) + 1, mult):
        if rows % cand == 0:
            best = cand
    return best or rows


def _ew(fn, ins, outs, tt, name):
    t = [a.shape[0] for a, tiled in ins if tiled][0]
    tt = _row_tile(t, tt)
    n_in = len(ins)
    in_specs = [pl.BlockSpec((tt, a.shape[1]), lambda i: (i, 0)) if tiled else pl.BlockSpec(a.shape, lambda i: (0, 0))
                for a, tiled in ins]
    out_specs, out_shapes = [], []
    for w, dt, kind in outs:
        if kind == "tile":
            out_specs.append(pl.BlockSpec((tt, w), lambda i: (i, 0)))
            out_shapes.append(jax.ShapeDtypeStruct((t, w), dt))
        else:
            out_specs.append(pl.BlockSpec((1, w), lambda i: (0, 0)))
            out_shapes.append(jax.ShapeDtypeStruct((1, w), F32))
    has_acc = any(kind == "acc" for _, _, kind in outs)

    def body(*refs):
        vals = fn(*[r[...] for r in refs[:n_in]])
        if not isinstance(vals, (tuple, list)):
            vals = (vals,)
        i = pl.program_id(0)
        for o_ref, v, (w, dt, kind) in zip(refs[n_in:], vals, outs):
            if kind == "tile":
                o_ref[...] = v.astype(dt)
            else:
                @pl.when(i == 0)
                def _(o_ref=o_ref, v=v):
                    o_ref[...] = v.astype(F32)

                @pl.when(i > 0)
                def _(o_ref=o_ref, v=v):
                    o_ref[...] += v.astype(F32)

    res = pl.pallas_call(
        body, name=name, grid=(t // tt,), in_specs=in_specs, out_specs=out_specs, out_shape=out_shapes,
        compiler_params=_params(("arbitrary" if has_acc else "parallel",)),
    )(*[a for a, _ in ins])
    return res


def _whole(fn, ins, outs, name):
    def body(*refs):
        vals = fn(*[r[...] for r in refs[:len(ins)]])
        if not isinstance(vals, (tuple, list)):
            vals = (vals,)
        for o_ref, v in zip(refs[len(ins):], vals):
            o_ref[...] = v.astype(o_ref.dtype)

    return pl.pallas_call(body, name=name, out_shape=[jax.ShapeDtypeStruct(s, dt) for s, dt in outs])(*ins)


def _rms(x):
    return x * lax.rsqrt(jnp.mean(x * x, axis=-1, keepdims=True) + EPS)


def _norm_mod_fn(h, g, sc, sh):
    return (_rms(h) * g) * (1.0 + sc) + sh


def _rms_gain_fn(x, g):
    return _rms(x) * g


def _gate_add_fn(h, y, gt):
    return h + gt * y


def _relu2_fn(x):
    return jnp.square(jnp.maximum(x, 0.0))


def _s5_act_fn(ypre, u, dskip):
    return jax.nn.gelu(ypre + dskip * u)


def _s5_glu_fn(z, zz, b):
    return z * jax.nn.sigmoid(zz + b)


def _ret_gate_fn(o, g):
    return _rms(o) * (g * jax.nn.sigmoid(g))


def _final_fn(h, g, tgt):
    err = _rms(h) * g - tgt
    return 0.5 * jnp.sum(jnp.mean(err * err, axis=-1, keepdims=True), axis=0, keepdims=True)


def _vjp_block(fn, n_args):
    def bwd(*vals):
        _, pull = jax.vjp(fn, *vals[:n_args])
        return pull(vals[n_args])
    return bwd


def _rope_tables(t, zero=0.0):
    d = RET_QK
    inv = ROPE_BASE ** (-jnp.arange(0, d, 2, dtype=F32) / d)
    ang = (jnp.arange(t, dtype=F32) + zero)[:, None] * inv[None, :]
    cos, sin = jnp.cos(ang), jnp.sin(ang)
    cos2, sin2 = jnp.concatenate([cos, cos], -1), jnp.concatenate([-sin, sin], -1)
    ret = (jnp.tile(cos2, (1, 8)), jnp.tile(sin2, (1, 8)))
    one, zero = jnp.ones((t, MLA_NOPE), F32), jnp.zeros((t, MLA_NOPE), F32)
    mla_c = jnp.concatenate([jnp.tile(jnp.concatenate([one, cos2], -1), (1, MLA_HEADS)), cos2, one[:, :d]], -1)
    mla_s = jnp.concatenate([jnp.tile(jnp.concatenate([zero, sin2], -1), (1, MLA_HEADS)), sin2, zero[:, :d]], -1)
    return ret, (mla_c, mla_s)


def _rope_fn(x, c, s, sign):
    w = x.shape[1]
    lane = lax.broadcasted_iota(jnp.int32, x.shape, 1)
    swapped = jnp.where((lane & 63) < 32, pltpu.roll(x, w - 32, 1), pltpu.roll(x, 32, 1))
    return x * c + swapped * (sign * s)


def _rope(x, tables, name, inverse=False, out_dtype=F32):
    c, s = tables
    fn = functools.partial(_rope_fn, sign=-1.0 if inverse else 1.0)
    return _ew(fn, [(x, True), (c, True), (s, True)], [(x.shape[1], out_dtype, "tile")], ROWS, name)[0]


SCAN_ROWS, SCAN_LANES = 256, 512


def _cmul(ar, ai, br, bi):
    return ar * br - ai * bi, ar * bi + ai * br


def _group_powers(ar, ai, reverse):
    shape = (SUBLANES, ar.shape[1])
    row = lax.broadcasted_iota(jnp.int32, shape, 0)
    pr, pi = ar, ai
    out_r, out_i = jnp.zeros(shape, F32), jnp.zeros(shape, F32)
    for e in range(1, SUBLANES + 1):
        hit = row == (SUBLANES - e if reverse else e - 1)
        out_r, out_i = jnp.where(hit, pr, out_r), jnp.where(hit, pi, out_i)
        if e < SUBLANES:
            pr, pi = _cmul(pr, pi, ar, ai)
    return out_r, out_i


def _scan_chunk(in_r_ref, in_i_ref, out_r_ref, out_i_ref, ar, ai, cr, ci, reverse, visit=None):
    rows, lanes = in_r_ref.shape
    sub = lax.broadcasted_iota(jnp.int32, (SUBLANES, lanes), 0)
    edge_r, edge_i = _group_powers(ar, ai, reverse)
    steps, pr, pi, k = [], ar, ai, 1
    while k < SUBLANES:
        steps.append((k, pr, pi))
        pr, pi = _cmul(pr, pi, pr, pi)
        k *= 2
    groups = range(rows // SUBLANES)
    for g in (reversed(groups) if reverse else groups):
        sl = slice(g * SUBLANES, (g + 1) * SUBLANES)
        xr, xi = in_r_ref[sl, :], in_i_ref[sl, :]
        for k, pr, pi in steps:
            shift = SUBLANES - k if reverse else k
            keep = sub < SUBLANES - k if reverse else sub >= k
            tr, ti = _cmul(pr, pi, pltpu.roll(xr, shift, 0), pltpu.roll(xi, shift, 0))
            xr, xi = xr + jnp.where(keep, tr, 0.0), xi + jnp.where(keep, ti, 0.0)
        tr, ti = _cmul(edge_r, edge_i, cr, ci)
        xr, xi = xr + tr, xi + ti
        out_r_ref[sl, :] = xr
        out_i_ref[sl, :] = xi
        if visit is not None:
            visit(sl, xr, xi, cr, ci)
        edge = slice(0, 1) if reverse else slice(SUBLANES - 1, SUBLANES)
        cr, ci = xr[edge, :], xi[edge, :]
    return cr, ci


def _s5_scan_specs(rows, row_block):
    assert SCAN_LANES == S5_PACK * S5_STATE
    chan = pl.BlockSpec((rows, S5_PACK * S5_CH), lambda j, i: (row_block(i), j))
    op = lambda off, shape: pl.BlockSpec((1,) + shape, lambda j, i: (j + off, 0, 0))
    blk = pl.BlockSpec((rows, SCAN_LANES), lambda j, i: (row_block(i), j))
    par = pl.BlockSpec((1, SCAN_LANES), lambda j, i: (0, j))
    return chan, op, blk, par


def _s5_scan_fwd(u, b3, a_r, a_i, name):
    t = u.shape[0]
    rows = min(SCAN_ROWS, t)
    chan, op, blk, par = _s5_scan_specs(rows, lambda i: i)

    def body(u_ref, b_re_ref, b_im_ref, ar_ref, ai_ref, or_ref, oi_ref, cr_ref, ci_ref, sr_ref, si_ref):
        i = pl.program_id(1)

        @pl.when(i == 0)
        def _():
            cr_ref[...] = jnp.zeros_like(cr_ref)
            ci_ref[...] = jnp.zeros_like(ci_ref)

        sr_ref[...] = _dot(u_ref[...], b_re_ref[0])
        si_ref[...] = _dot(u_ref[...], b_im_ref[0])
        ar, ai = ar_ref[...], ai_ref[...]
        cr, ci = _scan_chunk(sr_ref, si_ref, or_ref, oi_ref, ar, ai, cr_ref[...], ci_ref[...], reverse=False)
        cr_ref[...] = cr
        ci_ref[...] = ci

    st_r, st_i = pl.pallas_call(
        body, name=name, grid=(S5_BLOCKS, t // rows),
        in_specs=[chan, op(0, b3.shape[1:]), op(S5_BLOCKS, b3.shape[1:]), par, par], out_specs=[blk, blk],
        out_shape=[jax.ShapeDtypeStruct((t, S5_WIDTH), F32)] * 2,
        scratch_shapes=[pltpu.VMEM((1, SCAN_LANES), F32)] * 2 + [pltpu.VMEM((rows, SCAN_LANES), F32)] * 2,
        compiler_params=_params(("parallel", "arbitrary")),
    )(u, b3, b3, a_r, a_i)
    return st_r, st_i


def _s5_scan_bwd(dy, c3, st_r, st_i, a_r, a_i, name):
    t = dy.shape[0]
    rows = min(SCAN_ROWS, t)
    nc = t // rows
    chan, op, blk, par = _s5_scan_specs(rows, lambda i: nc - 1 - i)

    def body(dy_ref, c_re_ref, c_im_ref, xr_ref, xi_ref, ar_ref, ai_ref, gr_ref, gi_ref, dar_ref, dai_ref, cr_ref, ci_ref, dr_ref, di_ref):
        i = pl.program_id(1)
        dr_ref[...] = _dot(dy_ref[...], c_re_ref[0], "nt")
        di_ref[...] = _dot(dy_ref[...], c_im_ref[0], "nt")

        @pl.when(i == 0)
        def _():
            cr_ref[...] = jnp.zeros_like(cr_ref)
            ci_ref[...] = jnp.zeros_like(ci_ref)
            dar_ref[...] = jnp.zeros_like(dar_ref)
            dai_ref[...] = jnp.zeros_like(dai_ref)

        ar, ai = ar_ref[...], ai_ref[...]
        cr, ci = cr_ref[...], ci_ref[...]
        last = lax.broadcasted_iota(jnp.int32, (SUBLANES, SCAN_LANES), 0) == SUBLANES - 1
        sums = [jnp.zeros((SUBLANES, SCAN_LANES), F32), jnp.zeros((SUBLANES, SCAN_LANES), F32)]

        def visit(sl, gr, gi, next_r, next_i):
            nr = jnp.where(last, next_r, pltpu.roll(gr, SUBLANES - 1, 0))
            ni = jnp.where(last, next_i, pltpu.roll(gi, SUBLANES - 1, 0))
            xr, xi = xr_ref[sl, :], xi_ref[sl, :]
            sums[0] = sums[0] + (nr * xr + ni * xi)
            sums[1] = sums[1] + (ni * xr - nr * xi)

        first_r, first_i = _scan_chunk(dr_ref, di_ref, gr_ref, gi_ref, ar, -ai, cr, ci, reverse=True, visit=visit)
        dar_ref[...] += jnp.sum(sums[0], axis=0, keepdims=True)
        dai_ref[...] += jnp.sum(sums[1], axis=0, keepdims=True)
        cr_ref[...] = first_r
        ci_ref[...] = first_i

    return pl.pallas_call(
        body, name=name, grid=(S5_BLOCKS, nc),
        in_specs=[chan, op(0, c3.shape[1:]), op(S5_BLOCKS, c3.shape[1:]), blk, blk, par, par], out_specs=[blk, blk, par, par],
        out_shape=[jax.ShapeDtypeStruct((t, S5_WIDTH), F32)] * 2 + [jax.ShapeDtypeStruct((1, S5_WIDTH), F32)] * 2,
        scratch_shapes=[pltpu.VMEM((1, SCAN_LANES), F32)] * 2 + [pltpu.VMEM((rows, SCAN_LANES), F32)] * 2,
        compiler_params=_params(("parallel", "arbitrary")),
    )(dy, c3, c3, st_r, st_i, a_r, a_i)


def _s5_prep(lam_re, lam_im, log_dt, b_re, b_im, c_re, c_im, d_skip):
    dt = jnp.exp(log_dt)[:, None]
    mag = jnp.exp(lam_re * dt)
    ar, ai = mag * jnp.cos(lam_im * dt), mag * jnp.sin(lam_im * dt)
    den = lam_re * lam_re + lam_im * lam_im
    cr = ((ar - 1.0) * lam_re + ai * lam_im) / den
    ci = (ai * lam_re - (ar - 1.0) * lam_im) / den
    bbar_r = cr[..., None] * b_re - ci[..., None] * b_im
    bbar_i = cr[..., None] * b_im + ci[..., None] * b_re
    eye = jnp.eye(S5_PACK, dtype=F32)

    def bdiag(m):
        g, a, b = m.shape
        m4 = m.reshape(g // S5_PACK, S5_PACK, a, b)
        return (eye[None, :, None, :, None] * m4[:, :, :, None, :]).reshape(g // S5_PACK, S5_PACK * a, S5_PACK * b)

    b3 = jnp.concatenate([bdiag(bbar_r.transpose(0, 2, 1)), bdiag(bbar_i.transpose(0, 2, 1))], axis=0)
    c3 = jnp.concatenate([bdiag(c_re.transpose(0, 2, 1)), -bdiag(c_im.transpose(0, 2, 1))], axis=0)
    return b3, c3, d_skip.reshape(1, GROUP_WIDTH), ar.reshape(1, S5_WIDTH), ai.reshape(1, S5_WIDTH)


RET_UNROLL = 8


def _loop_unrolled(trips, body, init):
    factor = math.gcd(trips, RET_UNROLL)

    def several(i, carry):
        for u in range(factor):
            carry = body(i * factor + u, carry)
        return carry

    return lax.fori_loop(0, trips // factor, several, init)


def _ret_consts(lgam):
    c = RET_CHUNK
    r = lax.broadcasted_iota(jnp.int32, (c, c), 0)
    m = lax.broadcasted_iota(jnp.int32, (c, c), 1)
    rel = (r - m).astype(F32)
    decay = jnp.where(rel >= 0, jnp.exp(lgam * jnp.maximum(rel, 0.0)), 0.0)
    idx = lax.broadcasted_iota(jnp.int32, (c, 1), 0).astype(F32)
    zeta = jnp.exp(lgam * (c - 1.0 - idx))
    xi = jnp.exp(lgam * (idx + 1.0))
    return decay, zeta, xi, jnp.exp(lgam * c)


def _ret_specs(t):
    qk = lambda off: pl.BlockSpec((1, t, RET_QK), lambda h: (h + off, 0, 0))
    col = lambda off: pl.BlockSpec((t, RET_V), lambda h: (0, h + off))
    return qk, col


def _ret_fwd(qk, p_ret, lgam, name):
    t = qk.shape[1]
    nck = t // RET_CHUNK
    qk_spec, col = _ret_specs(t)

    def body(lg_ref, q_ref, k_ref, v_ref, g_ref, o_ref, y_ref):
        decay, zeta, xi, gam = _ret_consts(lg_ref[pl.program_id(0)])

        def step(n, state):
            sl = pl.ds(pl.multiple_of(n * RET_CHUNK, RET_CHUNK), RET_CHUNK)
            q, k, v = q_ref[0, sl, :], k_ref[0, sl, :] * (RET_QK ** -0.5), v_ref[sl, :]
            s = _dot(q, k, "nt") * decay
            o = _dot(s, v) + _dot(q, state) * xi
            o_ref[sl, :] = o
            y_ref[sl, :] = _ret_gate_fn(o, g_ref[sl, :]).astype(y_ref.dtype)
            return gam * state + _dot(k, zeta * v, "tn")

        _loop_unrolled(nck, step, jnp.zeros((RET_QK, RET_V), F32))

    return pl.pallas_call(
        body, name=name, grid=(RET_HEADS,), in_specs=[_SMEM, qk_spec(0), qk_spec(RET_HEADS), col(4), col(8)],
        out_specs=[col(0), col(0)],
        out_shape=[jax.ShapeDtypeStruct((t, GROUP_WIDTH), F32), jax.ShapeDtypeStruct((t, GROUP_WIDTH), BF16)],
        compiler_params=_params(("parallel",)),
    )(lgam, qk, qk, p_ret, p_ret)


def _ret_bwd(qk, p_ret, o_all, dy, lgam, name):
    t = qk.shape[1]
    nck = t // RET_CHUNK
    qk_spec, col = _ret_specs(t)
    gate_bwd = _vjp_block(_ret_gate_fn, 2)

    def body(lg_ref, q_ref, k_ref, v_ref, g_ref, o_ref, dy_ref, dq_ref, dk_ref, dv_ref, dg_ref, st_ref):
        decay, zeta, xi, gam = _ret_consts(lg_ref[pl.program_id(0)])
        scale = RET_QK ** -0.5

        def fstep(n, state):
            sl = pl.ds(pl.multiple_of(n * RET_CHUNK, RET_CHUNK), RET_CHUNK)
            st_ref[n] = state
            return gam * state + _dot(k_ref[0, sl, :] * scale, zeta * v_ref[sl, :], "tn")

        _loop_unrolled(nck, fstep, jnp.zeros((RET_QK, RET_V), F32))

        def bstep(r, grad_state):
            n = nck - 1 - r
            sl = pl.ds(pl.multiple_of(n * RET_CHUNK, RET_CHUNK), RET_CHUNK)
            q, k, v = q_ref[0, sl, :], k_ref[0, sl, :] * scale, v_ref[sl, :]
            d_o, dg = gate_bwd(o_ref[sl, :], g_ref[sl, :], dy_ref[sl, :])
            dg_ref[sl, :] = dg.astype(dg_ref.dtype)
            s = _dot(q, k, "nt") * decay
            ds = _dot(d_o, v, "nt") * decay
            xdo = xi * d_o
            dq_ref[0, sl, :] = _dot(ds, k) + _dot(xdo, st_ref[n], "nt")
            dk_ref[0, sl, :] = (_dot(ds, q, "tn") + _dot(zeta * v, grad_state, "nt")) * scale
            dv_ref[sl, :] = (_dot(s, d_o, "tn") + zeta * _dot(k, grad_state)).astype(dv_ref.dtype)
            return gam * grad_state + _dot(q, xdo, "tn")

        _loop_unrolled(nck, bstep, jnp.zeros((RET_QK, RET_V), F32))

    hd = pl.BlockSpec((1, t, RET_QK), lambda h: (h, 0, 0))
    return pl.pallas_call(
        body, name=name, grid=(RET_HEADS,),
        in_specs=[_SMEM, qk_spec(0), qk_spec(RET_HEADS), col(4), col(8), col(0), col(0)],
        out_specs=[hd, hd, col(0), col(0)],
        out_shape=[jax.ShapeDtypeStruct((RET_HEADS, t, RET_QK), F32)] * 2 + [jax.ShapeDtypeStruct((t, GROUP_WIDTH), BF16)] * 2,
        scratch_shapes=[pltpu.VMEM((nck, RET_QK, RET_V), F32)], compiler_params=_params(("parallel",)),
    )(lgam, qk, qk, p_ret, p_ret, o_all, dy)


SWA_GROUP = SWA_HEADS // SWA_KV_HEADS
SWA_SCALE = SWA_HD ** -0.5


def _swa_mask(n):
    rows = SWA_GROUP * WINDOW
    r = lax.broadcasted_iota(jnp.int32, (rows, 2 * WINDOW), 0) & (WINDOW - 1)
    j = lax.broadcasted_iota(jnp.int32, (rows, 2 * WINDOW), 1)
    dist = r + WINDOW - j
    return (dist >= 0) & (dist < WINDOW) & (n * WINDOW + j - WINDOW >= 0)


def _swa_sink_rows(sink_ref, kv):
    row = lax.broadcasted_iota(jnp.int32, (SWA_GROUP * WINDOW, 1), 0)
    sink = jnp.zeros((SWA_GROUP * WINDOW, 1), F32)
    for g in range(SWA_GROUP):
        sink = jnp.where(row >= g * WINDOW, sink_ref[kv * SWA_GROUP + g], sink)
    return sink


def _swa_pad_keys(n, k_ref, v_ref, kp_ref, vp_ref):
    @pl.when(n == 0)
    def _():
        zero = jnp.zeros((WINDOW, SWA_HD), F32)
        kp_ref[0:WINDOW, :] = zero
        vp_ref[0:WINDOW, :] = zero
        kp_ref[WINDOW:, :] = k_ref[0]
        vp_ref[WINDOW:, :] = v_ref[0]


SWA_STEP_BLOCKS = 4


def _swa_specs(t):
    per_step = math.gcd(t // WINDOW, SWA_STEP_BLOCKS)
    blk = lambda w: pl.BlockSpec((SWA_GROUP, per_step * WINDOW, w), lambda kv, n: (kv, n, 0))
    kspec = lambda off: pl.BlockSpec((1, t, SWA_HD), lambda kv, n: (SWA_HEADS + off + kv, 0, 0))
    return blk, kspec, per_step


def _swa_fwd(qkv, sinks, name):
    t = qkv.shape[1]
    rows = SWA_GROUP * WINDOW
    blk, kspec, per_step = _swa_specs(t)

    def body(sink_ref, q_ref, k_ref, v_ref, o_ref, lse_ref, kp_ref, vp_ref):
        kv, step = pl.program_id(0), pl.program_id(1)
        _swa_pad_keys(step, k_ref, v_ref, kp_ref, vp_ref)
        sink = _swa_sink_rows(sink_ref, kv)
        for u in range(per_step):
            n = step * per_step + u
            here = slice(u * WINDOW, (u + 1) * WINDOW)
            win = pl.ds(pl.multiple_of(n * WINDOW, WINDOW), 2 * WINDOW)
            s = _dot(q_ref[:, here, :].reshape(rows, SWA_HD), kp_ref[win, :], "nt") * SWA_SCALE
            s = jnp.where(_swa_mask(n), s, NEG)
            m = jnp.maximum(jnp.max(s, axis=-1, keepdims=True), sink)
            p = jnp.exp(s - m)
            den = jnp.sum(p, axis=-1, keepdims=True) + jnp.exp(sink - m)
            o_ref[:, here, :] = _dot(p / den, vp_ref[win, :]).reshape(SWA_GROUP, WINDOW, SWA_HD)
            lse_ref[:, here, :] = (m + jnp.log(den)).reshape(SWA_GROUP, WINDOW, 1)

    return pl.pallas_call(
        body, name=name, grid=(SWA_KV_HEADS, t // WINDOW // per_step),
        in_specs=[_SMEM, blk(SWA_HD), kspec(0), kspec(SWA_KV_HEADS)], out_specs=[blk(SWA_HD), blk(1)],
        out_shape=[jax.ShapeDtypeStruct((SWA_HEADS, t, SWA_HD), F32), jax.ShapeDtypeStruct((SWA_HEADS, t, 1), F32)],
        scratch_shapes=[pltpu.VMEM((t + WINDOW, SWA_HD), F32)] * 2, compiler_params=_params(("parallel", "arbitrary")),
    )(sinks, qkv, qkv, qkv)


def _swa_bwd(qkv, o, lse, d_o, sinks, name):
    t = qkv.shape[1]
    nb = t // WINDOW
    rows = SWA_GROUP * WINDOW
    blk, kspec, per_step = _swa_specs(t)

    def body(sink_ref, q_ref, k_ref, v_ref, o_ref, lse_ref, do_ref, dq_ref, dk_ref, dv_ref, dsink_ref,
             kp_ref, vp_ref, dkp_ref, dvp_ref):
        kv, step = pl.program_id(0), pl.program_id(1)
        _swa_pad_keys(step, k_ref, v_ref, kp_ref, vp_ref)

        @pl.when(step == 0)
        def _():
            dkp_ref[...] = jnp.zeros_like(dkp_ref)
            dvp_ref[...] = jnp.zeros_like(dvp_ref)
            dsink_ref[...] = jnp.zeros_like(dsink_ref)

        sink = _swa_sink_rows(sink_ref, kv)
        head = lax.broadcasted_iota(jnp.int32, (SWA_GROUP, 128), 0)
        acc = jnp.zeros((SWA_GROUP, 128), F32)
        for u in range(per_step):
            n = step * per_step + u
            here = slice(u * WINDOW, (u + 1) * WINDOW)
            win = pl.ds(pl.multiple_of(n * WINDOW, WINDOW), 2 * WINDOW)
            q, dout = q_ref[:, here, :].reshape(rows, SWA_HD), do_ref[:, here, :].reshape(rows, SWA_HD)
            lse_n = lse_ref[:, here, :].reshape(rows, 1)
            s = _dot(q, kp_ref[win, :], "nt") * SWA_SCALE
            s = jnp.where(_swa_mask(n), s, NEG)
            p = jnp.exp(s - lse_n)
            delta = jnp.sum(dout * o_ref[:, here, :].reshape(rows, SWA_HD), axis=-1, keepdims=True)
            ds = p * (_dot(dout, vp_ref[win, :], "nt") - delta)
            dq_ref[:, here, :] = (_dot(ds, kp_ref[win, :]) * SWA_SCALE).reshape(SWA_GROUP, WINDOW, SWA_HD).astype(dq_ref.dtype)
            dkp_ref[win, :] += _dot(ds, q, "tn") * SWA_SCALE
            dvp_ref[win, :] += _dot(p, dout, "tn")
            term = jnp.exp(sink - lse_n) * delta
            for g in range(SWA_GROUP):
                acc = jnp.where(head == g, acc + jnp.sum(term[g * WINDOW:(g + 1) * WINDOW], axis=0, keepdims=True), acc)
        dsink_ref[0] -= acc

        @pl.when(step == nb // per_step - 1)
        def _():
            dk_ref[0] = dkp_ref[WINDOW:, :].astype(dk_ref.dtype)
            dv_ref[0] = dvp_ref[WINDOW:, :].astype(dv_ref.dtype)

    kout = pl.BlockSpec((1, t, SWA_HD), lambda kv, n: (kv, 0, 0))
    dq, dk, dv, dsink = pl.pallas_call(
        body, name=name, grid=(SWA_KV_HEADS, nb // per_step),
        in_specs=[_SMEM, blk(SWA_HD), kspec(0), kspec(SWA_KV_HEADS), blk(SWA_HD), blk(1), blk(SWA_HD)],
        out_specs=[blk(SWA_HD), kout, kout, pl.BlockSpec((1, SWA_GROUP, 128), lambda kv, n: (kv, 0, 0))],
        out_shape=[jax.ShapeDtypeStruct((SWA_HEADS, t, SWA_HD), BF16), jax.ShapeDtypeStruct((SWA_KV_HEADS, t, SWA_HD), BF16),
                   jax.ShapeDtypeStruct((SWA_KV_HEADS, t, SWA_HD), BF16), jax.ShapeDtypeStruct((SWA_KV_HEADS, SWA_GROUP, 128), F32)],
        scratch_shapes=[pltpu.VMEM((t + WINDOW, SWA_HD), F32)] * 4, compiler_params=_params(("parallel", "arbitrary")),
    )(sinks, qkv, qkv, qkv, o, lse, d_o)
    return jnp.concatenate([dq, dk, dv], axis=0), dsink[:, :, 0].reshape(SWA_HEADS)


MLA_SCALE = (MLA_NOPE + MLA_ROPE) ** -0.5
MLA_TILE = 512
MLA_KEY_TILE = 512
MLA_BWD_TILE = 512


def _mla_diag(s):
    r = lax.broadcasted_iota(jnp.int32, s.shape, 0)
    c = lax.broadcasted_iota(jnp.int32, s.shape, 1)
    return jnp.where(c <= r, s, NEG)


def _mla_specs(t, tile):
    whole = lambda w, off: pl.BlockSpec((t, w), lambda h, i: (0, 2 * h + off))
    head = lambda w: pl.BlockSpec((1, t, w), lambda h, i: (h, 0, 0))
    key_rope = pl.BlockSpec((1, t, MLA_ROPE), lambda h, i: (MLA_HEADS, 0, 0))
    tile_of = lambda w: pl.BlockSpec((1, tile, w), lambda h, i: (h, i, 0))
    return whole, head, key_rope, tile_of


def _mla_attend(qn, rot, kv, name):
    t = qn.shape[1]
    tile = min(MLA_TILE, t)
    ktile = min(MLA_KEY_TILE, t)
    ratio = ktile // tile
    whole, head, key_rope, tile_of = _mla_specs(t, tile)

    def body(qn_ref, qr_ref, kn_ref, kr_ref, v_ref, o_ref, lse_ref, m_ref, l_ref, acc_ref):
        i = pl.program_id(1)
        qn_b, qr_b = qn_ref[0], qr_ref[0]

        def rows(j):
            return pl.ds(pl.multiple_of(j * ktile, ktile), ktile)

        def scores(j):
            return (_dot(qn_b, kn_ref[rows(j), :], "nt") + _dot(qr_b, kr_ref[0, rows(j), :], "nt")) * MLA_SCALE

        def causal(s, j):
            qpos = i * tile + lax.broadcasted_iota(jnp.int32, s.shape, 0)
            kpos = j * ktile + lax.broadcasted_iota(jnp.int32, s.shape, 1)
            return jnp.where(kpos <= qpos, s, NEG)

        def update(s, j):
            m_old = m_ref[...]
            m_new = jnp.maximum(m_old, jnp.max(s, axis=-1, keepdims=True))
            alpha = jnp.exp(m_old - m_new)
            p = jnp.exp(s - m_new)
            l_ref[...] = alpha * l_ref[...] + jnp.sum(p, axis=-1, keepdims=True)
            acc_ref[...] = alpha * acc_ref[...] + _dot(p, v_ref[rows(j), :])
            m_ref[...] = m_new

        m_ref[...] = jnp.full_like(m_ref, NEG)
        l_ref[...] = jnp.zeros_like(l_ref)
        acc_ref[...] = jnp.zeros_like(acc_ref)

        def step(j, s_cur):
            s_next = scores(j + 1)
            update(s_cur, j)
            return s_next

        last = i // ratio
        s_last = lax.fori_loop(0, last, step, scores(0))
        update(causal(s_last, last), last)
        o_ref[...] = acc_ref[...] / l_ref[...]
        lse_ref[0] = m_ref[...] + jnp.log(l_ref[...])

    return pl.pallas_call(
        body, name=name, grid=(MLA_HEADS, t // tile),
        in_specs=[tile_of(MLA_NOPE), tile_of(MLA_ROPE), whole(MLA_NOPE, 0), key_rope, whole(MLA_V, 1)],
        out_specs=[pl.BlockSpec((tile, MLA_V), lambda h, i: (i, h)), tile_of(1)],
        out_shape=[jax.ShapeDtypeStruct((t, GROUP_WIDTH), F32), jax.ShapeDtypeStruct((MLA_HEADS, t, 1), F32)],
        scratch_shapes=[pltpu.VMEM((tile, 1), F32), pltpu.VMEM((tile, 1), F32), pltpu.VMEM((tile, MLA_V), F32)],
        compiler_params=_params(("parallel", "parallel")),
    )(qn, rot, kv, rot, kv)


def _mla_attend_bwd(qn, rot, kv, lse, o, d_o, name):
    t = qn.shape[1]
    tile = min(MLA_BWD_TILE, t)
    nt = t // tile
    whole, head, key_rope, tile_of = _mla_specs(t, tile)

    def body(qn_ref, qr_ref, kn_ref, kr_ref, v_ref, lse_ref, o_ref, do_ref, dqn_ref, dqr_ref, dkn_ref, dv_ref, dkr_ref, dl_ref):
        j = pl.program_id(1)

        @pl.when(j == 0)
        def _():
            dqn_ref[...] = jnp.zeros_like(dqn_ref)
            dqr_ref[...] = jnp.zeros_like(dqr_ref)
            dl_ref[0] = jnp.sum(do_ref[...] * o_ref[...], axis=-1, keepdims=True)

        dkn_ref[...] = jnp.zeros_like(dkn_ref)
        dv_ref[...] = jnp.zeros_like(dv_ref)
        dkr_ref[...] = jnp.zeros_like(dkr_ref)
        kn_b, kr_b, v_b = kn_ref[...], kr_ref[0], v_ref[...]

        def block(i, diagonal):
            sl = pl.ds(pl.multiple_of(i * tile, tile), tile)
            qn_b, qr_b, dout = qn_ref[0, sl, :], qr_ref[0, sl, :], do_ref[sl, :]
            s = (_dot(qn_b, kn_b, "nt") + _dot(qr_b, kr_b, "nt")) * MLA_SCALE
            if diagonal:
                s = _mla_diag(s)
            p = jnp.exp(s - lse_ref[0, sl, :])
            ds = p * (_dot(dout, v_b, "nt") - dl_ref[0, sl, :]) * MLA_SCALE
            dv_ref[...] += _dot(p, dout, "tn")
            dkn_ref[...] += _dot(ds, qn_b, "tn")
            dkr_ref[0] += _dot(ds, qr_b, "tn")
            dqn_ref[0, sl, :] += _dot(ds, kn_b)
            dqr_ref[0, sl, :] += _dot(ds, kr_b)

        block(j, True)

        def step(i, carry):
            block(i, False)
            return carry

        lax.fori_loop(j + 1, nt, step, 0)

    key_tile = lambda w, off: pl.BlockSpec((tile, w), lambda h, j: (j, 2 * h + off))
    out_tile = pl.BlockSpec((tile, MLA_V), lambda h, j: (j, h))
    return pl.pallas_call(
        body, name=name, grid=(MLA_HEADS, nt),
        in_specs=[head(MLA_NOPE), head(MLA_ROPE), key_tile(MLA_NOPE, 0), pl.BlockSpec((1, tile, MLA_ROPE), lambda h, j: (MLA_HEADS, j, 0)),
                  key_tile(MLA_V, 1), head(1), pl.BlockSpec((t, MLA_V), lambda h, j: (0, h)), pl.BlockSpec((t, MLA_V), lambda h, j: (0, h))],
        out_specs=[head(MLA_NOPE), head(MLA_ROPE), out_tile, out_tile, tile_of(MLA_ROPE)],
        out_shape=[jax.ShapeDtypeStruct((MLA_HEADS, t, MLA_NOPE), F32), jax.ShapeDtypeStruct((MLA_HEADS, t, MLA_ROPE), F32),
                   jax.ShapeDtypeStruct((t, MLA_HEADS * MLA_NOPE), F32), jax.ShapeDtypeStruct((t, MLA_HEADS * MLA_V), F32),
                   jax.ShapeDtypeStruct((MLA_HEADS, t, MLA_ROPE), F32)],
        scratch_shapes=[pltpu.VMEM((1, t, 1), F32)], compiler_params=_params(("parallel", "arbitrary")),
    )(qn, rot, kv, rot, kv, lse, o, d_o)


def _place():
    return lax.axis_index("x"), lax.axis_index("y"), lax.axis_index("c")


def _all_gather(arrs, name):
    n = len(arrs)

    def body(*refs):
        x_refs, o_refs = refs[:n], refs[n:2 * n]
        send_sems, recv_sems, local_sems = refs[2 * n:]
        x, y, c = _place()
        me, sibling = (x, y, c), (x, y, 1 - c)
        chips = [(1 - x, y), (x, 1 - y), (1 - x, 1 - y)]

        def slot(a, p):
            return o_refs[a].at[4 * p[0] + 2 * p[1] + p[2]]

        def copy(a, k, block, to, src=None):
            return pltpu.make_async_remote_copy(
                src_ref=slot(a, block) if src is None else src, dst_ref=slot(a, block),
                send_sem=send_sems.at[a, k], recv_sem=recv_sems.at[a, k], device_id=to, device_id_type=MESH_ID)

        mine = [pltpu.make_async_copy(x_refs[a], slot(a, me), local_sems.at[a]) for a in range(n)]
        for cp in mine:
            cp.start()
        first = []
        for a in range(n):
            first.append(copy(a, 0, me, sibling, src=x_refs[a]))
            first += [copy(a, 1 + j, me, (*chip, c), src=x_refs[a]) for j, chip in enumerate(chips)]
        for cp in first:
            cp.start()
        passed = []
        for j, chip in enumerate(chips):
            for a in range(n):
                copy(a, 1 + j, (*chip, c), me).wait_recv()
                cp = copy(a, 4 + j, (*chip, c), sibling)
                cp.start()
                passed.append(cp)
        for a in range(n):
            copy(a, 0, sibling, me).wait_recv()
            for j, chip in enumerate(chips):
                copy(a, 4 + j, (*chip, 1 - c), me).wait_recv()
        for cp in first + passed:
            cp.wait_send()
        for cp in mine:
            cp.wait()

    return pl.pallas_call(
        body, name=name, in_specs=[_ANY] * n, out_specs=[_ANY] * n,
        out_shape=[jax.ShapeDtypeStruct((N_DEV,) + a.shape, a.dtype) for a in arrs],
        scratch_shapes=[pltpu.SemaphoreType.DMA((n, 7)), pltpu.SemaphoreType.DMA((n, 7)), pltpu.SemaphoreType.DMA((n,))],
    )(*arrs)


def _pass_to_sibling(arrs, name):
    n = len(arrs)

    def body(*refs):
        a_refs, o_refs = refs[:n], refs[n:2 * n]
        send_sems, recv_sems = refs[2 * n:]
        x, y, c = _place()
        chips = [(1 - x, y), (x, 1 - y), (1 - x, 1 - y)]
        slot = lambda px, py, pc: 4 * px + 2 * py + pc
        sends, recvs = [], []
        for a in range(n):
            for j, (px, py) in enumerate(chips):
                sends.append(pltpu.make_async_remote_copy(
                    src_ref=a_refs[a].at[slot(px, py, c)], dst_ref=o_refs[a].at[slot(px, py, c)], send_sem=send_sems.at[a, j],
                    recv_sem=recv_sems.at[a, j], device_id=(x, y, 1 - c), device_id_type=MESH_ID))
                recvs.append(pltpu.make_async_remote_copy(
                    src_ref=a_refs[a].at[slot(px, py, c)], dst_ref=o_refs[a].at[slot(px, py, 1 - c)], send_sem=send_sems.at[a, j],
                    recv_sem=recv_sems.at[a, j], device_id=(x, y, 1 - c), device_id_type=MESH_ID))
        for cp in sends:
            cp.start()
        for cp in sends:
            cp.wait_send()
        for cp in recvs:
            cp.wait_recv()

    return pl.pallas_call(
        body, name=name, in_specs=[_ANY] * n, out_specs=[_ANY] * n,
        out_shape=[jax.ShapeDtypeStruct(a.shape, a.dtype) for a in arrs], input_output_aliases={i: i for i in range(n)},
        scratch_shapes=[pltpu.SemaphoreType.DMA((n, 3)), pltpu.SemaphoreType.DMA((n, 3))],
    )(*arrs)


def _scatter_core(grads, name):
    n = len(grads)

    def body(*refs):
        g_refs, got_refs = refs[:n], refs[n:2 * n]
        send_sems, recv_sems = refs[2 * n:]
        x, y, c = _place()
        sends = [pltpu.make_async_remote_copy(
            src_ref=g_refs[a].at[2 * q + 1 - c], dst_ref=got_refs[a].at[q], send_sem=send_sems.at[a, q],
            recv_sem=recv_sems.at[a, q], device_id=(x, y, 1 - c), device_id_type=MESH_ID) for a in range(n) for q in range(4)]
        for cp in sends:
            cp.start()
        for cp in sends:
            cp.wait()

    return pl.pallas_call(
        body, name=name, in_specs=[_ANY] * n, out_specs=[_ANY] * n,
        out_shape=[jax.ShapeDtypeStruct((4,) + g.shape[1:], g.dtype) for g in grads],
        scratch_shapes=[pltpu.SemaphoreType.DMA((n, 4)), pltpu.SemaphoreType.DMA((n, 4))],
    )(*grads)


def _scatter_chips(parts, name):
    n = len(parts)

    def body(*refs):
        p_refs, o_refs = refs[:n], refs[n:2 * n]
        send_sems, recv_sems = refs[2 * n:]
        x, y, c = _place()
        chips = [(1 - x, y), (x, 1 - y), (1 - x, 1 - y)]
        sends = [pltpu.make_async_remote_copy(
            src_ref=p_refs[a].at[2 * px + py], dst_ref=o_refs[a].at[j], send_sem=send_sems.at[a, j],
            recv_sem=recv_sems.at[a, j], device_id=(px, py, c), device_id_type=MESH_ID)
            for a in range(n) for j, (px, py) in enumerate(chips)]
        for cp in sends:
            cp.start()
        for cp in sends:
            cp.wait()

    return pl.pallas_call(
        body, name=name, in_specs=[_ANY] * n, out_specs=[_ANY] * n,
        out_shape=[jax.ShapeDtypeStruct((3,) + p.shape[1:], p.dtype) for p in parts],
        scratch_shapes=[pltpu.SemaphoreType.DMA((n, 3)), pltpu.SemaphoreType.DMA((n, 3))],
    )(*parts)


GATHER_PEERS = [(0, 0, 1), (1, 0, 0), (0, 1, 0), (1, 1, 0)]
CHIP_PEERS = [(1, 0, 0), (0, 1, 0), (1, 1, 0)]
CORE_PEERS = [(0, 0, 1)] * 4
_HBM = pl.BlockSpec(memory_space=pltpu.HBM)
_SEM = pl.BlockSpec(memory_space=pltpu.SEMAPHORE)
_EFFECT = pltpu.SideEffectType.DATAFLOW_SIDE_EFFECTING


def _push_copies(src_refs, land_refs, send_sems, recv_sems, peers, src_of, slot_of):
    place = _place()
    flip = lambda v, f: 1 - v if f else v
    return [pltpu.make_async_remote_copy(
        src_ref=src_of(src_refs[a], k), dst_ref=land_refs[a].at[slot_of(k)], send_sem=send_sems[a], recv_sem=recv_sems[a],
        device_id=tuple(flip(v, f) for v, f in zip(place, peer)), device_id_type=MESH_ID)
        for a in range(len(src_refs)) for k, peer in enumerate(peers)]


def _push_start(srcs, land_shapes, peers, src_of, slot_of, name, after=None):
    n = len(srcs)
    extra = [] if after is None else [after]

    def body(*refs):
        src_refs, land_refs = refs[:n], refs[n:2 * n]
        refs = refs[2 * n + len(extra):]
        send_sems, recv_sems = refs[:n], refs[n:2 * n]
        token = refs[-1]
        for cp in _push_copies(src_refs, land_refs, send_sems, recv_sems, peers, src_of, slot_of):
            cp.start()
        token[...] = jnp.zeros_like(token)

    sems = [pltpu.SemaphoreType.DMA(())] * (2 * n)
    lands = [pltpu.with_memory_space_constraint(lax.empty(s.shape, s.dtype), pltpu.HBM) for s in land_shapes]
    res = pl.pallas_call(
        body, name=name, in_specs=[_HBM] * (2 * n) + [_ANY] * len(extra),
        out_specs=[_SEM] * (2 * n) + [_HBM] * (2 * n) + [pl.BlockSpec(memory_space=pltpu.VMEM)],
        out_shape=sems + [pltpu.HBM(s.shape, s.dtype) for s in srcs] + [pltpu.HBM(s.shape, s.dtype) for s in land_shapes]
        + [jax.ShapeDtypeStruct((8, 128), F32)],
        input_output_aliases={i: 2 * n + i for i in range(2 * n)},
        compiler_params=pltpu.CompilerParams(has_side_effects=_EFFECT),
    )(*[pltpu.with_memory_space_constraint(s, pltpu.HBM) for s in srcs], *lands, *extra)
    return list(res[:n]), list(res[n:2 * n]), list(res[2 * n:3 * n]), list(res[3 * n:4 * n]), res[-1]


def _push_wait(send_sems, recv_sems, srcs, lands, after, peers, src_of, slot_of, name):
    n = len(srcs)

    def body(*refs):
        src_refs, land_refs = refs[:n], refs[n:2 * n]
        s_sems, r_sems = refs[2 * n:3 * n], refs[3 * n:4 * n]
        copies = _push_copies(src_refs, land_refs, s_sems, r_sems, peers, src_of, slot_of)
        for cp in copies:
            cp.wait_send()
        for cp in copies:
            cp.wait_recv()

    res = pl.pallas_call(
        body, name=name, in_specs=[_HBM] * (2 * n) + [_SEM] * (2 * n) + [_ANY], out_specs=[_HBM] * (2 * n),
        out_shape=[pltpu.HBM(s.shape, s.dtype) for s in srcs] + [pltpu.HBM(s.shape, s.dtype) for s in lands],
        input_output_aliases={i: i for i in range(2 * n)},
        compiler_params=pltpu.CompilerParams(has_side_effects=_EFFECT),
    )(*srcs, *lands, *send_sems, *recv_sems, after)
    return list(res[:n]), list(res[n:])


def _pick_sum(picked, rest, index, pick_of, out_dtype, name):
    nq, r, cdim = rest.shape
    one = nq == 3
    tr = _row_tile(r, 512, 16)
    tc = 512 if cdim % 512 == 0 else cdim
    grid = (1 if one else nq, r // tr, cdim // tc)

    def body(i_ref, p_ref, r_ref, o_ref):
        acc = p_ref[0].astype(F32)
        if one:
            for j in range(3):
                acc = acc + r_ref[j].astype(F32)
            o_ref[...] = acc.astype(out_dtype)
        else:
            o_ref[0] = (acc + r_ref[0].astype(F32)).astype(out_dtype)

    spec = pltpu.PrefetchScalarGridSpec(
        num_scalar_prefetch=1, grid=grid,
        in_specs=[pl.BlockSpec((1, tr, tc), lambda q, i, j, i_ref: (pick_of(q, i_ref[0]), i, j)),
                  pl.BlockSpec((3, tr, tc), lambda q, i, j, i_ref: (0, i, j)) if one else pl.BlockSpec((1, tr, tc), lambda q, i, j, i_ref: (q, i, j))],
        out_specs=pl.BlockSpec((tr, tc), lambda q, i, j, i_ref: (i, j)) if one else pl.BlockSpec((1, tr, tc), lambda q, i, j, i_ref: (q, i, j)))
    return pl.pallas_call(
        body, name=name, grid_spec=spec,
        out_shape=jax.ShapeDtypeStruct((r, cdim) if one else (nq, r, cdim), out_dtype),
        compiler_params=_params(("parallel", "parallel", "parallel")),
    )(index.astype(jnp.int32).reshape(1), picked, rest)


def _adamw_fn(w, g, m, v):
    m = ADAM_B1 * m + (1.0 - ADAM_B1) * g
    v = ADAM_B2 * v + (1.0 - ADAM_B2) * jnp.square(g)
    m_hat = m / (1.0 - ADAM_B1 ** ADAM_STEP)
    v_hat = v / (1.0 - ADAM_B2 ** ADAM_STEP)
    delta = -ADAM_LR * (m_hat / (jnp.sqrt(v_hat) + ADAM_EPS) + ADAM_WD * w)
    return delta, m, v


def _as2d(a):
    return a.reshape(-1, a.shape[-1])


def _adamw_shard(w, g, m, v, name):
    shape = w.shape
    ins = [_as2d(a) for a in (w, g, m, v)]
    cols = ins[0].shape[1]
    outs = _ew(_adamw_fn, [(a, True) for a in ins], [(cols, F32, "tile")] * 3, 256, name)
    return [o.reshape(shape) for o in outs]


def _adamw_small(ws, gs, ms, vs, name):
    shapes = [w.shape for w in ws]
    flat = lambda a: a.reshape(-1, 128) if a.size % 128 == 0 else a.reshape(1, -1)
    ins = [flat(a) for grp in zip(ws, gs, ms, vs) for a in grp]
    k = len(ws)

    def fn(*vals):
        out = []
        for i in range(k):
            out += list(_adamw_fn(*vals[4 * i:4 * i + 4]))
        return out

    outs = _whole(fn, ins, [(ins[4 * (i // 3)].shape, F32) for i in range(3 * k)], name)
    deltas = [outs[3 * i].reshape(shapes[i]) for i in range(k)]
    new_m = [outs[3 * i + 1].reshape(shapes[i]) for i in range(k)]
    new_v = [outs[3 * i + 2].reshape(shapes[i]) for i in range(k)]
    return deltas, new_m, new_v


def _sum8(stacked, name):
    def fn(a):
        s = a[0:1]
        for i in range(1, N_DEV):
            s = s + a[i:i + 1]
        return s
    w = stacked.shape[1]
    tw = 8192
    if w % tw:
        return _whole(fn, [stacked], [((1, w), F32)], name)[0]

    def body(a_ref, o_ref):
        o_ref[...] = fn(a_ref[...])

    return pl.pallas_call(body, name=name, grid=(w // tw,), in_specs=[pl.BlockSpec((N_DEV, tw), lambda i: (0, i))],
                          out_specs=pl.BlockSpec((1, tw), lambda i: (0, i)), out_shape=jax.ShapeDtypeStruct((1, w), F32))(stacked)


ROWS = 512


def _split_heads(p, nh):
    t = p.shape[0]
    return p.reshape(t, nh, p.shape[1] // nh).transpose(1, 0, 2)


def _merge_heads(p):
    nh, t, d = p.shape
    return p.transpose(1, 0, 2).reshape(t, nh * d)


def _layer_fwd(h, mod, w, small, rope, l, late=None):
    sh1, sc1, gt1, sh2, sc2, gt2 = mod
    rope_ret, rope_mla = rope
    t = h.shape[0]
    nm = lambda s: f"l{l}_{s}"
    a1 = _ew(_norm_mod_fn, [(h, True), (small["norm1_g"], False), (sc1, False), (sh1, False)], [(D_MODEL, BF16, "tile")], ROWS, nm("norm1"))[0]
    p_s5 = _mm(a1, w["w_in_t"], "nt", 512, 512, 2048, name=nm("proj_s5"), n=512)
    p_ret = _mm(a1, w["w_in_t"], "nt", 512, 512, 2048, name=nm("proj_ret"), b_off=1, n=1536)
    p_swa = _mm(a1, w["w_in_t"], "nt", 512, 256, 2048, name=nm("proj_swa"), b_off=8, n=768)
    p_mla = _mm(a1, w["w_in_t"][2816:], "nt", 512, 576, 2048, name=nm("proj_mla"))
    b3, c3, dskip, a_r, a_i = small["s5"]
    st_r, st_i = _s5_scan_fwd(p_s5, b3, a_r, a_i, nm("s5_scan"))
    ypre = _mm_blocks(st_r, c3[:S5_BLOCKS], "nn", 512, pair=(st_i, c3[S5_BLOCKS:]), name=nm("s5_y"))
    z = _ew(_s5_act_fn, [(ypre, True), (p_s5, True), (dskip, False)], [(GROUP_WIDTH, F32, "tile")], ROWS, nm("s5_act"))[0]
    zz = _mm(z, w["glu_w"], "nn", 512, 512, 512, name=nm("s5_zz"))
    y_s5 = _ew(_s5_glu_fn, [(z, True), (zz, True), (small["s5_glu_b"], False)], [(GROUP_WIDTH, BF16, "tile")], ROWS, nm("s5_glu"))[0]
    qk_ret = _split_heads(_rope(p_ret[:, :2 * RET_HEADS * RET_QK], rope_ret, nm("ret_rope")), 2 * RET_HEADS)
    o_ret, y_ret = _ret_fwd(qk_ret, p_ret, small["ret_lgam"], nm("ret"))
    qkv_swa = _split_heads(p_swa, 12)
    o_swa, lse_swa = _swa_fwd(qkv_swa, small["swa_sinks"], nm("swa"))
    y_swa = _merge_heads(o_swa).astype(BF16)
    cq, ckv, kr = p_mla[:, :MLA_Q_RANK], p_mla[:, MLA_Q_RANK:MLA_Q_RANK + MLA_KV_RANK], p_mla[:, MLA_Q_RANK + MLA_KV_RANK:]
    cqn = _ew(_rms_gain_fn, [(cq, True), (small["mla_q_norm"], False)], [(MLA_Q_RANK, BF16, "tile")], ROWS, nm("mla_qnorm"))[0]
    ckvn = _ew(_rms_gain_fn, [(ckv, True), (small["mla_kv_norm"], False)], [(MLA_KV_RANK, BF16, "tile")], ROWS, nm("mla_kvnorm"))[0]
    q_full = _mm(cqn, w["w_uq_t"], "nt", 512, 768, 384, name=nm("mla_q"))
    kv_full = _mm(ckvn, w["w_ukv_t"], "nt", 512, 1024, 128, BF16, name=nm("mla_kv"))
    nq = q_full.shape[1]
    roped = _rope(jnp.concatenate([q_full, kr, jnp.zeros_like(kr)], axis=1), rope_mla, nm("mla_rope"), out_dtype=BF16)
    q4 = roped[:, :nq].reshape(t, MLA_HEADS, MLA_NOPE + MLA_ROPE)
    qn = q4[:, :, :MLA_NOPE].transpose(1, 0, 2)
    rot = jnp.concatenate([q4[:, :, MLA_NOPE:].transpose(1, 0, 2), roped[None, :, nq:nq + MLA_ROPE]], axis=0)
    o_mla, lse_mla = _mla_attend(qn, rot, kv_full, nm("mla"))
    cat = jnp.concatenate([y_s5, y_ret, y_swa, o_mla.astype(BF16)], axis=1)
    if late is not None:
        w = {**w, **late(lse_mla)}
    mixed = _mm(cat, w["w_out"], "nn", 512, 1024, 2048, name=nm("out_proj"))
    h1 = _ew(_gate_add_fn, [(h, True), (mixed, True), (gt1, False)], [(D_MODEL, F32, "tile")], ROWS, nm("res1"))[0]
    a2 = _ew(_norm_mod_fn, [(h1, True), (small["norm2_g"], False), (sc2, False), (sh2, False)], [(D_MODEL, BF16, "tile")], ROWS, nm("norm2"))[0]
    hid, act = _mm(a2, w["w1_t"], "nt", 1024, 1024, 2048, name=nm("mlp1"), epi=lambda acc: (acc, _relu2_fn(acc)), epi_outs=[F32, BF16])
    mo = _mm(act, w["w2"], "nn", 1024, 1024, 2048, name=nm("mlp2"))
    h2 = _ew(_gate_add_fn, [(h1, True), (mo, True), (gt2, False)], [(D_MODEL, F32, "tile")], ROWS, nm("res2"))[0]
    saved = dict(w=w, h=h, a1=a1, p_s5=p_s5, p_ret=p_ret, st_r=st_r, st_i=st_i, ypre=ypre, z=z, zz=zz, qk_ret=qk_ret, o_ret=o_ret,
                 qkv_swa=qkv_swa, o_swa=o_swa, lse_swa=lse_swa, cq=cq, ckv=ckv, cqn=cqn, ckvn=ckvn, qn=qn, rot=rot,
                 kv_full=kv_full, o_mla=o_mla, lse_mla=lse_mla, cat=cat, mixed=mixed, h1=h1, a2=a2, hid=hid, act=act, mo=mo)
    return h2, saved


def _layer_bwd(dh2, mod, w, small, rope, s, l, after_mlp=None, after_out=None):
    sh1, sc1, gt1, sh2, sc2, gt2 = mod
    rope_ret, rope_mla = rope
    t = dh2.shape[0]
    nm = lambda n: f"l{l}_{n}_bwd"
    gb, gs = {}, {}
    row = (D_MODEL, F32, "acc")
    dmo, dgt2 = _ew(lambda d, y, gt: (d * gt, jnp.sum(d * y, axis=0, keepdims=True)),
                    [(dh2, True), (s["mo"], True), (gt2, False)], [(D_MODEL, BF16, "tile"), row], ROWS, nm("res2"))
    dhid = _mm(dmo, w["w2"], "nt", 1024, 1024, 2048, name=nm("mlp2_x"), epi=lambda acc, x: (acc * 2.0 * jnp.maximum(x, 0.0),),
               epi_ins=[s["hid"]], epi_outs=[BF16])[0]
    gb["w2"] = _mm(s["act"], dmo, "tn", 1024, 1024, 4096, BF16, name=nm("mlp2_w"))
    da2 = _mm(dhid, w["w1_t"], "nn", 1024, 1024, 2048, name=nm("mlp1_x"))
    gb["w1_t"] = _mm(dhid, s["a2"], "tn", 1024, 1024, 4096, BF16, name=nm("mlp1_w"))
    if after_mlp is not None:
        gt1 = gt1 + after_mlp(gb)

    def norm_bwd(hh, g, sc, sh, da, dres):
        dh_, dg, dsc, dsh = _vjp_block(_norm_mod_fn, 4)(hh, g, sc, sh, da)
        return dh_ + dres, dg, dsc, dsh

    dh1, gs["norm2_g"], dsc2, dsh2 = _ew(norm_bwd, [(s["h1"], True), (small["norm2_g"], False), (sc2, False), (sh2, False), (da2, True), (dh2, True)],
                                         [(D_MODEL, F32, "tile"), row, row, row], ROWS, nm("norm2"))
    dmixed, dgt1 = _ew(lambda d, y, gt: (d * gt, jnp.sum(d * y, axis=0, keepdims=True)),
                       [(dh1, True), (s["mixed"], True), (gt1, False)], [(D_MODEL, BF16, "tile"), row], ROWS, nm("res1"))
    dcat = _mm(dmixed, w["w_out"], "nt", 512, 1024, 2048, name=nm("out_proj_x"))
    gb["w_out"] = _mm(s["cat"], dmixed, "tn", 1024, 1024, 4096, BF16, name=nm("out_proj_w"))
    if after_out is not None:
        small = {**small, "s5_glu_b": small["s5_glu_b"] + after_out(gb)}
    dy_s5, dy_ret, dy_swa, dy_mla = (dcat[:, i * GROUP_WIDTH:(i + 1) * GROUP_WIDTH] for i in range(4))
    b3, c3, dskip, a_r, a_i = small["s5"]
    gw = (GROUP_WIDTH, F32, "tile")
    gacc = (GROUP_WIDTH, F32, "acc")
    dz_a, dzz, gs["s5_glu_b"] = _ew(_vjp_block(_s5_glu_fn, 3), [(s["z"], True), (s["zz"], True), (small["s5_glu_b"], False), (dy_s5, True)],
                                    [gw, gw, gacc], ROWS, nm("s5_glu"))
    dz_b = _mm(dzz, w["glu_w"], "nt", 512, 512, 512, name=nm("s5_zz_x"))
    gb["glu_w"] = _mm(s["z"], dzz, "tn", 512, 512, 1024, BF16, name=nm("s5_zz_w"))

    def act_bwd(ypre, u, dsk, dza, dzb):
        return _vjp_block(_s5_act_fn, 3)(ypre, u, dsk, dza + dzb)

    dypre, du_a, g_dskip = _ew(act_bwd, [(s["ypre"], True), (s["p_s5"], True), (dskip, False), (dz_a, True), (dz_b, True)],
                               [gw, gw, gacc], ROWS, nm("s5_act"))
    ch, st = S5_PACK * S5_CH, S5_PACK * S5_STATE
    g_c3 =jnp.concatenate([_mm_blocks_tn(s["st_r"], dypre, st, ch, lambda j: j, nm("s5_y_w_re")),
                            _mm_blocks_tn(s["st_i"], dypre, st, ch, lambda j: j, nm("s5_y_w_im"))], axis=0)
    dbu_r, dbu_i, g_ar, g_ai = _s5_scan_bwd(dypre, c3, s["st_r"], s["st_i"], a_r, a_i, nm("s5_scan"))
    du_b = _mm_blocks(dbu_r, b3[:S5_BLOCKS], "nt", 512, pair=(dbu_i, b3[S5_BLOCKS:]), name=nm("s5_bu_x"))
    g_b3 = jnp.concatenate([_mm_blocks_tn(s["p_s5"], dbu_r, ch, st, lambda j: j, nm("s5_bu_w_re")),
                            _mm_blocks_tn(s["p_s5"], dbu_i, ch, st, lambda j: j, nm("s5_bu_w_im"))], axis=0)
    gs["s5"] = (g_b3, g_c3, g_dskip, g_ar, g_ai)
    dqk_rot, dk_rot, dv_ret, dg_ret = _ret_bwd(s["qk_ret"], s["p_ret"], s["o_ret"], dy_ret, small["ret_lgam"], nm("ret"))
    dqk = _rope(_merge_heads(jnp.concatenate([dqk_rot, dk_rot], axis=0)), rope_ret, nm("ret_rope"), inverse=True, out_dtype=BF16)
    dqkv_swa, gs["swa_sinks"] = _swa_bwd(s["qkv_swa"], s["o_swa"], s["lse_swa"], _split_heads(dy_swa, SWA_HEADS), small["swa_sinks"], nm("swa"))
    dqn, dqr, dkn, dv_mla, dkr_heads = _mla_attend_bwd(s["qn"], s["rot"], s["kv_full"], s["lse_mla"], s["o_mla"], dy_mla, nm("mla_att"))
    dkv_full = jnp.stack([dkn.reshape(t, MLA_HEADS, MLA_NOPE), dv_mla.reshape(t, MLA_HEADS, MLA_V)], axis=2).reshape(t, 2 * MLA_HEADS * MLA_NOPE)
    dkr_rot = _ew(lambda a, b, c, d: a + b + c + d, [(dkr_heads[i], True) for i in range(MLA_HEADS)], [(MLA_ROPE, F32, "tile")], ROWS, nm("mla_dkr"))[0]
    nq = MLA_HEADS * (MLA_NOPE + MLA_ROPE)
    dq_rot = jnp.concatenate([dqn.transpose(1, 0, 2), dqr.transpose(1, 0, 2)], axis=2).reshape(t, nq)
    droped = _rope(jnp.concatenate([dq_rot, dkr_rot, jnp.zeros_like(dkr_rot)], axis=1), rope_mla, nm("mla_rope"), inverse=True, out_dtype=BF16)
    dq_full, dkr = droped[:, :nq], droped[:, nq:nq + MLA_ROPE]
    dcqn = _mm(dq_full, w["w_uq_t"], "nn", 512, 384, 768, name=nm("mla_q_x"))
    gb["w_uq_t"] = _mm(dq_full, s["cqn"], "tn", 768, 384, 1024, BF16, name=nm("mla_q_w"))
    dckvn = _mm(dkv_full, w["w_ukv_t"], "nn", 512, 128, 1024, name=nm("mla_kv_x"))
    gb["w_ukv_t"] = _mm(dkv_full, s["ckvn"], "tn", 1024, 128, 1024, BF16, name=nm("mla_kv_w"))
    dcq, gs["mla_q_norm"] = _ew(_vjp_block(_rms_gain_fn, 2), [(s["cq"], True), (small["mla_q_norm"], False), (dcqn, True)],
                                [(MLA_Q_RANK, BF16, "tile"), (MLA_Q_RANK, F32, "acc")], ROWS, nm("mla_qnorm"))
    dckv, gs["mla_kv_norm"] = _ew(_vjp_block(_rms_gain_fn, 2), [(s["ckv"], True), (small["mla_kv_norm"], False), (dckvn, True)],
                                  [(MLA_KV_RANK, BF16, "tile"), (MLA_KV_RANK, F32, "acc")], ROWS, nm("mla_kvnorm"))
    du = _ew(lambda a, b: a + b, [(du_a, True), (du_b, True)], [(GROUP_WIDTH, BF16, "tile")], ROWS, nm("s5_du"))[0]
    bf = lambda a: a.astype(BF16)
    dproj = jnp.concatenate([du, bf(dqk), bf(dv_ret), bf(dg_ret), bf(_merge_heads(dqkv_swa)), bf(dcq), bf(dckv), bf(dkr)], axis=1)
    da1 = _mm(dproj, w["w_in_t"], "nn", 512, 1024, N_IN, name=nm("proj_x"))
    gb["w_in_t"] = _mm(dproj, s["a1"], "tn", N_IN, 512, 2048, BF16, name=nm("proj_w"))
    dh, gs["norm1_g"], dsc1, dsh1 = _ew(norm_bwd, [(s["h"], True), (small["norm1_g"], False), (sc1, False), (sh1, False), (da1, True), (dh1, True)],
                                        [(D_MODEL, F32, "tile"), row, row, row], ROWS, nm("norm1"))
    dmod = jnp.concatenate([dsh1, dsc1, dgt1, dsh2, dsc2, dgt2], axis=1)
    return dh, gb, gs, dmod


BIG = ("w_in_t", "w1_t", "w_uq_t", "w_ukv_t", "w_out", "w2", "glu_w")
MLP_BIG = ("w1_t", "w2")
OUT_BIG = ("w_out",)
LATE_BIG = ("w_out", "w1_t", "w2")
S5_NAMES = ("s5_lambda_re", "s5_lambda_im", "s5_log_dt", "s5_b_re", "s5_b_im", "s5_c_re", "s5_c_im", "s5_d")


def kernel(x, c, norm1_g, norm2_g, ada_w, ada_b, w_in, s5_lambda_re, s5_lambda_im, s5_log_dt, s5_b_re, s5_b_im, s5_c_re, s5_c_im, s5_d, s5_glu_w, s5_glu_b, swa_sinks, mla_q_norm, mla_kv_norm, mla_w_uq, mla_w_ukv, w_out, mlp_w1, mlp_w2, final_norm_g, loss_target, m_norm1_g, m_norm2_g, m_ada_w, m_ada_b, m_w_in, m_s5_lambda_re, m_s5_lambda_im, m_s5_log_dt, m_s5_b_re, m_s5_b_im, m_s5_c_re, m_s5_c_im, m_s5_d, m_s5_glu_w, m_s5_glu_b, m_swa_sinks, m_mla_q_norm, m_mla_kv_norm, m_mla_w_uq, m_mla_w_ukv, m_w_out, m_mlp_w1, m_mlp_w2, m_final_norm_g, v_norm1_g, v_norm2_g, v_ada_w, v_ada_b, v_w_in, v_s5_lambda_re, v_s5_lambda_im, v_s5_log_dt, v_s5_b_re, v_s5_b_im, v_s5_c_re, v_s5_c_im, v_s5_d, v_s5_glu_w, v_s5_glu_b, v_swa_sinks, v_mla_q_norm, v_mla_kv_norm, v_mla_w_uq, v_mla_w_ukv, v_w_out, v_mlp_w1, v_mlp_w2, v_final_norm_g):
    names = ["norm1_g", "norm2_g", "ada_w", "ada_b", "w_in", "s5_lambda_re", "s5_lambda_im", "s5_log_dt", "s5_b_re", "s5_b_im",
             "s5_c_re", "s5_c_im", "s5_d", "s5_glu_w", "s5_glu_b", "swa_sinks", "mla_q_norm", "mla_kv_norm", "mla_w_uq",
             "mla_w_ukv", "w_out", "mlp_w1", "mlp_w2", "final_norm_g"]
    env = locals()
    wts = {n: env[n] for n in names}
    mom = {n: env["m_" + n] for n in names}
    var = {n: env["v_" + n] for n in names}
    t = x.shape[1]
    me = 4 * lax.axis_index("x") + 2 * lax.axis_index("y") + lax.axis_index("c")
    ret_lgam = jnp.log1p(-(2.0 ** (-5.0 - jnp.arange(RET_HEADS, dtype=F32))))

    tr = lambda a: a.transpose(0, 2, 1)
    shard = {"w_in_t": tr(w_in), "w1_t": tr(mlp_w1), "w_uq_t": tr(mla_w_uq), "w_ukv_t": tr(mla_w_ukv),
             "w_out": w_out, "w2": mlp_w2, "glu_w": s5_glu_w}
    to_send = [{k: shard[k][l].astype(BF16) for k in BIG} for l in range(DEPTH)]
    as_rows = lambda keys, arrs: {k: a.reshape(-1, shard[k].shape[2]) for k, a in zip(keys, arrs)}
    first = [k for k in BIG if k not in LATE_BIG]
    own_slot = lambda k: 4 * lax.axis_index("x") + 2 * lax.axis_index("y") + lax.axis_index("c")

    def gather_start(arrs, tag):
        return _push_start(arrs, [jax.ShapeDtypeStruct((N_DEV,) + a.shape, a.dtype) for a in arrs], GATHER_PEERS,
                           lambda ref, k: ref, own_slot, f"gather_weights_{tag}_start")

    def gather_finish(started, after, tag):
        sent, landed = _push_wait(started[0], started[1], started[2], started[3], after, GATHER_PEERS, lambda ref, k: ref, own_slot,
                                  f"gather_weights_{tag}_wait")
        with_own = [lax.dynamic_update_index_in_dim(full, own, me, 0) for full, own in zip(landed, sent)]
        return _pass_to_sibling(with_own, f"gather_weights_{tag}_pass")

    gather_first = gather_start([to_send[0][k] for k in first] + [c], "first")
    zero = gather_first[4][0, 0]
    rope = _rope_tables(t, zero)
    smalls, s5_pulls = [], []
    for l in range(DEPTH):
        s5_ops, pull = jax.vjp(_s5_prep, *[wts[n][l] + zero if n == "s5_log_dt" else wts[n][l] for n in S5_NAMES])
        s5_pulls.append(pull)
        smalls.append(dict(norm1_g=norm1_g[l][None], norm2_g=norm2_g[l][None], s5=s5_ops, s5_glu_b=s5_glu_b[l][None],
                           swa_sinks=swa_sinks[l], mla_q_norm=mla_q_norm[l][None], mla_kv_norm=mla_kv_norm[l][None], ret_lgam=ret_lgam))
    setup_done = (smalls[0]["s5"][3][0, 0] + smalls[1]["s5"][3][0, 0] + rope[0][0][0, 0] + rope[1][0][0, 0]).reshape(1, 1)
    gathered = gather_finish(gather_first, setup_done, "first")
    c_all = gathered[-1].reshape(N_DEV, D_MODEL)
    big = [as_rows(first, gathered[:len(first)]), None]
    gather0 = gather_start([to_send[0][k] for k in LATE_BIG], "l0")
    gather1 = gather_start([to_send[1][k] for k in BIG], "l1")

    c_act = _whole(lambda v: v * jax.nn.sigmoid(v), [c_all], [((N_DEV, D_MODEL), F32)], "cond_silu")[0]
    c_pad = jnp.concatenate([c_act, jnp.zeros((128 - N_DEV, D_MODEL), F32)], axis=0)
    cols = ada_w.shape[2]
    mod_part = [_mm(c_pad, ada_w[l], "nn", 128, cols, 512, name=f"l{l}_mod")[:N_DEV] for l in range(DEPTH)]
    mod_all = _all_gather([jnp.stack(mod_part)], "gather_mod")[0]
    mod_rows = lax.dynamic_index_in_dim(mod_all, me, axis=2, keepdims=False)
    mods = []
    for l in range(DEPTH):
        row = mod_rows[:, l].reshape(1, 6 * D_MODEL) + ada_b[l][None]
        if l == 0:
            row = row + (gather0[4][0, 0] + gather1[4][0, 0])
        mods.append([row[:, i * D_MODEL:(i + 1) * D_MODEL] for i in range(6)])

    h = x[0]
    saved = []
    for l in range(DEPTH):
        if l == 0:
            late = lambda after: as_rows(LATE_BIG, gather_finish(gather0, after, "l0"))
        else:
            big[1] = as_rows(BIG, gather_finish(gather1, h, "l1"))
            late = None
        h, s = _layer_fwd(h, mods[l], big[l], smalls[l], rope, l, late)
        big[l] = s.pop("w")
        saved.append(s)

    fg = final_norm_g[None]
    tgt = loss_target[0]
    loss_local = _ew(_final_fn, [(h, True), (fg, False), (tgt, True)], [(1, F32, "acc")], ROWS, "loss")[0]

    def final_bwd(hh, g, tg):
        dh_, dg, _ = _vjp_block(_final_fn, 3)(hh, g, tg, jnp.ones((1, 1), F32))
        return dh_, dg

    dh, g_final = _ew(final_bwd, [(h, True), (fg, False), (tgt, True)], [(D_MODEL, F32, "tile"), (D_MODEL, F32, "acc")], ROWS, "loss_bwd")
    loss = lax.psum(loss_local[0, 0], ("x", "y", "c"))

    core, chip = lax.axis_index("c"), 2 * lax.axis_index("x") + lax.axis_index("y")

    def core_stage(g_layer, keys, tag):
        g_list = [g_layer[k].reshape(N_DEV, -1, g_layer[k].shape[1]) for k in keys]
        got = _scatter_core(g_list, f"scatter_core_{tag}")
        return [_pick_sum(g, o, core, lambda q, c_: 2 * q + c_, BF16, f"{tag}_core_sum_{k}") for k, g, o in zip(keys, g_list, got)]

    def their_block(ref, k):
        x_, y_ = lax.axis_index("x"), lax.axis_index("y")
        dx, dy, _ = CHIP_PEERS[k]
        return ref.at[2 * (1 - x_ if dx else x_) + (1 - y_ if dy else y_)]

    def chips_start(halves, tag, after=None):
        return _push_start(halves, [jax.ShapeDtypeStruct((3,) + a.shape[1:], a.dtype) for a in halves], CHIP_PEERS,
                           their_block, lambda k: k, f"scatter_chips_{tag}_start", after)

    def chips_wait(started, after, tag):
        return _push_wait(started[0], started[1], started[2], started[3], after, CHIP_PEERS, their_block, lambda k: k, f"scatter_chips_{tag}_wait")

    def sibling_block(ref, q):
        return ref.at[2 * q + 1 - lax.axis_index("c")]

    g_small, dmods = [None] * DEPTH, [None] * DEPTH
    dh, g_big1, g_small[1], dmods[1] = _layer_bwd(dh, mods[1], big[1], smalls[1], rope, saved[1], 1)
    g_list1 = [g_big1[k].reshape(N_DEV, -1, g_big1[k].shape[1]) for k in BIG]
    core1 = _push_start(g_list1, [jax.ShapeDtypeStruct((4,) + g.shape[1:], g.dtype) for g in g_list1], CORE_PEERS,
                        sibling_block, lambda q: q, "scatter_core_l1_start")
    mods0 = [m + core1[4][0, 0] for m in mods[0]]
    early = {}

    def after_mlp(gb):
        g_mine, got = _push_wait(core1[0], core1[1], core1[2], core1[3], gb["w1_t"], CORE_PEERS, sibling_block, lambda q: q,
                                 "scatter_core_l1_wait")
        halves = [_pick_sum(g, o, core, lambda q, c_: 2 * q + c_, BF16, f"l1_core_sum_{k}") for k, g, o in zip(BIG, g_mine, got)]
        early["l1"] = chips_start(halves, "l1")
        g_list = [gb[k].reshape(N_DEV, -1, gb[k].shape[1]) for k in MLP_BIG]
        early["mlp_core"] = _push_start(g_list, [jax.ShapeDtypeStruct((4,) + g.shape[1:], g.dtype) for g in g_list], CORE_PEERS,
                                        sibling_block, lambda q: q, "scatter_core_l0_mlp_start")
        return early["l1"][4][0, 0] + early["mlp_core"][4][0, 0]

    def after_out(gb):
        started = early["mlp_core"]
        g_mine, got = _push_wait(started[0], started[1], started[2], started[3], gb["w_out"], CORE_PEERS, sibling_block, lambda q: q,
                                 "scatter_core_l0_mlp_wait")
        halves = [_pick_sum(g, o, core, lambda q, c_: 2 * q + c_, BF16, f"l0_mlp_core_sum_{k}") for k, g, o in zip(MLP_BIG, g_mine, got)]
        early["mlp"] = chips_start(halves, "l0_mlp")
        early["out"] = chips_start(core_stage(gb, OUT_BIG, "l0_out"), "l0_out")
        return early["mlp"][4][0, 0] + early["out"][4][0, 0]

    dh, g_big0, g_small[0], dmods[0] = _layer_bwd(dh, mods0, big[0], smalls[0], rope, saved[0], 0, after_mlp=after_mlp, after_out=after_out)
    grad_x = dh[None]
    rest = [k for k in BIG if k not in MLP_BIG + OUT_BIG]
    halves_rest = core_stage(g_big0, rest, "l0_rest")

    small_parts = []
    for l in range(DEPTH):
        gs = g_small[l]
        s5g = s5_pulls[l](gs["s5"])
        small_parts += [gs["norm1_g"], gs["norm2_g"], *s5g, gs["s5_glu_b"], gs["swa_sinks"], gs["mla_q_norm"], gs["mla_kv_norm"]]
    small_parts += [g_final, *dmods]
    sizes = [int(np.prod(p.shape)) for p in small_parts]
    flat = jnp.concatenate([p.reshape(1, -1) for p in small_parts], axis=1)
    pad = (-flat.shape[1]) % 8192
    flat = jnp.pad(flat, ((0, 0), (0, pad)))
    flat_all = _all_gather([flat], "gather_small_grads")[0].reshape(N_DEV, -1)
    summed = _sum8(flat_all, "sum_small_grads")
    chips_rest = chips_start(halves_rest, "l0_rest", after=summed)
    rest_token = chips_rest[4]
    halves1, landed1 = chips_wait(early["l1"], rest_token, "l1")
    halves_mlp, landed_mlp = chips_wait(early["mlp"], rest_token, "l0_mlp")
    halves_out, landed_out = chips_wait(early["out"], rest_token, "l0_out")
    terms = {(1, k): pair for k, pair in zip(BIG, zip(halves1, landed1))}
    terms.update({(0, k): pair for k, pair in zip(MLP_BIG, zip(halves_mlp, landed_mlp))})
    terms.update({(0, k): pair for k, pair in zip(OUT_BIG, zip(halves_out, landed_out))})
    chip_sum = lambda l, k: _pick_sum(*terms[l, k], chip, lambda q, m_: m_, F32, f"l{l}_chip_sum_{k}")
    pieces, off = [], 0
    for sz in sizes:
        pieces.append(summed[0, off:off + sz])
        off += sz
    small_names = ["norm1_g", "norm2_g", *S5_NAMES, "s5_glu_b", "swa_sinks", "mla_q_norm", "mla_kv_norm"]
    per_layer = len(small_names)
    grads = {}
    for i, n in enumerate(small_names):
        grads[n] = jnp.stack([pieces[l * per_layer + i].reshape(wts[n].shape[1:]) for l in range(DEPTH)])
    grads["final_norm_g"] = pieces[DEPTH * per_layer]
    grads["ada_b"] = jnp.stack([pieces[DEPTH * per_layer + 1 + l] for l in range(DEPTH)])

    mod_off = sum(sizes[:DEPTH * per_layer + 1])
    dmod_all = flat_all[:, mod_off:mod_off + DEPTH * 6 * D_MODEL].reshape(N_DEV, DEPTH, N_DEV, cols)
    dmod_mine = lax.dynamic_index_in_dim(dmod_all, me, axis=2, keepdims=False).transpose(1, 0, 2)
    dmod_pad = jnp.concatenate([dmod_mine, jnp.zeros((DEPTH, 128 - N_DEV, cols), F32)], axis=1)
    grads["ada_w"] = jnp.stack([_mm(c_pad, dmod_pad[l], "tn", 512, cols, 128, name=f"l{l}_ada_w_grad") for l in range(DEPTH)])

    out_g, out_d, out_m, out_v = dict(grads), {}, {}, {}
    orig = {"w_in_t": "w_in", "w1_t": "mlp_w1", "w_uq_t": "mla_w_uq", "w_ukv_t": "mla_w_ukv", "w_out": "w_out", "w2": "mlp_w2", "glu_w": "s5_glu_w"}

    def update_big(keys):
        for k in keys:
            n = orig[k]
            out_g[n] = jnp.stack([chip_sum(l, k) for l in range(DEPTH)])
            if k.endswith("_t"):
                out_g[n] = tr(out_g[n])
            out_d[n], out_m[n], out_v[n] = _adamw_shard(wts[n], out_g[n], mom[n], var[n], f"adamw_{n}")

    update_big(MLP_BIG + OUT_BIG)
    out_d["ada_w"], out_m["ada_w"], out_v["ada_w"] = _adamw_shard(wts["ada_w"], out_g["ada_w"], mom["ada_w"], var["ada_w"], "adamw_ada_w")
    small_all = small_names + ["ada_b", "final_norm_g"]
    ds, ms, vs = _adamw_small([wts[n] for n in small_all], [grads[n] for n in small_all], [mom[n] for n in small_all],
                              [var[n] for n in small_all], "adamw_small")
    for n, d, m_, v_ in zip(small_all, ds, ms, vs):
        out_d[n], out_m[n], out_v[n] = d, m_, v_
    halves_rest, landed_rest = chips_wait(chips_rest, out_d["ada_w"], "l0_rest")
    terms.update({(0, k): pair for k, pair in zip(rest, zip(halves_rest, landed_rest))})
    update_big(rest)
    return (loss, grad_x, *[out_g[n] for n in names], *[out_d[n] for n in names], *[out_m[n] for n in names], *[out_v[n] for n in names])
```

```python
import functools
import math

import numpy as np
import jax
import jax.numpy as jnp
from jax import lax
from jax.experimental import pallas as pl
from jax.experimental.pallas import tpu as pltpu

F32 = jnp.float32
BF16 = jnp.bfloat16
_MXU_DTYPE = jnp.bfloat16

N_DEV = 8
D_MODEL = 2048
DEPTH = 2
GROUP_WIDTH = 512
D_FF = 8192
S5_CH, S5_GROUPS, S5_STATE = 16, 32, 64
S5_WIDTH = S5_GROUPS * S5_STATE
S5_PACK = 8
S5_BLOCKS = S5_GROUPS // S5_PACK
RET_HEADS, RET_QK, RET_V, RET_CHUNK = 4, 64, 128, 128
SWA_HD, SWA_HEADS, SWA_KV_HEADS, WINDOW = 64, 8, 2, 128
MLA_HEADS, MLA_Q_RANK, MLA_KV_RANK, MLA_NOPE, MLA_ROPE, MLA_V = 4, 384, 128, 128, 64, 128
ROPE_BASE = 10000.0
EPS = 1e-6
NEG = -1e30
N_IN = 3392
ADAM_LR, ADAM_B1, ADAM_B2, ADAM_EPS, ADAM_WD, ADAM_STEP = 0.001, 0.9, 0.999, 1e-08, 0.01, 10

VMEM_LIMIT_BYTES = 52 * 1024 * 1024
MESH_ID = pl.DeviceIdType.MESH
_ANY = pl.BlockSpec(memory_space=pl.ANY)
_SMEM = pl.BlockSpec(memory_space=pltpu.SMEM)


def _params(sem):
    return pltpu.CompilerParams(dimension_semantics=sem, vmem_limit_bytes=VMEM_LIMIT_BYTES)


_DIMS = {"nn": (((1,), (0,)), ((), ())), "nt": (((1,), (1,)), ((), ())), "tn": (((0,), (0,)), ((), ()))}


def _dot(a, b, mode="nn"):
    return lax.dot_general(a.astype(_MXU_DTYPE), b.astype(_MXU_DTYPE), _DIMS[mode], preferred_element_type=F32)


def _mm(a, b, mode, tm, tn, tk, out_dtype=F32, name="mm", b_off=0, n=None, pair=None, epi=None, epi_ins=(), epi_outs=None):
    if mode == "tn":
        kdim, m = a.shape
    else:
        m, kdim = a.shape
    if n is None:
        n = b.shape[0] if mode == "nt" else b.shape[1]
    tm, tn, tk = min(tm, m), min(tn, n), min(tk, kdim)
    assert m % tm == 0 and n % tn == 0 and kdim % tk == 0, (name, a.shape, b.shape, tm, tn, tk)
    nk = kdim // tk
    a_spec = pl.BlockSpec((tk, tm), lambda i, j, k: (k, i)) if mode == "tn" else pl.BlockSpec((tm, tk), lambda i, j, k: (i, k))
    if mode == "nt":
        b_spec = pl.BlockSpec((tn, tk), lambda i, j, k: (j + b_off, k))
    else:
        b_spec = pl.BlockSpec((tk, tn), lambda i, j, k: (k, j + b_off))
    o_spec = pl.BlockSpec((tm, tn), lambda i, j, k: (i, j))
    n_mm = 2 if pair is None else 4
    out_dtypes = [out_dtype] if epi is None else list(epi_outs)

    def body(*refs):
        ins, extra = refs[:n_mm], refs[n_mm:n_mm + len(epi_ins)]
        outs = refs[n_mm + len(epi_ins):n_mm + len(epi_ins) + len(out_dtypes)]
        part = _dot(ins[0][...], ins[1][...], mode)
        if pair is not None:
            part = part + _dot(ins[2][...], ins[3][...], mode)

        def finish(acc):
            vals = (acc,) if epi is None else epi(acc, *[r[...] for r in extra])
            for o_ref, v, dt in zip(outs, vals, out_dtypes):
                o_ref[...] = v.astype(dt)

        if nk == 1:
            finish(part)
        else:
            acc_ref = refs[-1]
            k = pl.program_id(2)

            @pl.when(k == 0)
            def _():
                acc_ref[...] = part

            @pl.when(k > 0)
            def _():
                acc_ref[...] += part

            @pl.when(k == nk - 1)
            def _():
                finish(acc_ref[...])

    operands = [a, b] + ([] if pair is None else list(pair)) + list(epi_ins)
    res = pl.pallas_call(
        body, name=name, grid=(m // tm, n // tn, nk),
        in_specs=[a_spec, b_spec] * (n_mm // 2) + [o_spec] * len(epi_ins),
        out_specs=[o_spec] * len(out_dtypes), out_shape=[jax.ShapeDtypeStruct((m, n), dt) for dt in out_dtypes],
        scratch_shapes=[] if nk == 1 else [pltpu.VMEM((tm, tn), F32)],
        compiler_params=_params(("parallel", "parallel", "arbitrary")),
    )(*operands)
    return res[0] if epi is None else res


def _mm_blocks(a, b, mode, tm, a_of=None, pair=None, name="mm_blocks"):
    a_of = a_of or (lambda j: j)
    m = a.shape[0]
    nj, kb, nb = b.shape
    a_w, o_w = (kb, nb) if mode == "nn" else (nb, kb)
    tm = min(tm, m)
    a_spec = pl.BlockSpec((tm, a_w), lambda i, j: (i, a_of(j)))
    b_spec = pl.BlockSpec((1, kb, nb), lambda i, j: (j, 0, 0))
    n_in = 2 if pair is None else 4

    def body(*refs):
        acc = _dot(refs[0][...], refs[1][0], mode)
        if pair is not None:
            acc = acc + _dot(refs[2][...], refs[3][0], mode)
        refs[n_in][...] = acc

    operands = [a, b] + ([] if pair is None else list(pair))
    return pl.pallas_call(
        body, name=name, grid=(m // tm, nj), in_specs=[a_spec, b_spec] * (n_in // 2),
        out_specs=pl.BlockSpec((tm, o_w), lambda i, j: (i, j)), out_shape=jax.ShapeDtypeStruct((m, nj * o_w), F32),
        compiler_params=_params(("parallel", "parallel")),
    )(*operands)


def _mm_blocks_tn(a, b, x, y, b_of, name):
    kdim = a.shape[0]
    nj = a.shape[1] // x

    def body(a_ref, b_ref, o_ref):
        o_ref[0] = _dot(a_ref[...], b_ref[...], "tn")

    return pl.pallas_call(
        body, name=name, grid=(nj,), in_specs=[pl.BlockSpec((kdim, x), lambda j: (0, j)), pl.BlockSpec((kdim, y), lambda j: (0, b_of(j)))],
        out_specs=pl.BlockSpec((1, x, y), lambda j: (j, 0, 0)), out_shape=jax.ShapeDtypeStruct((nj, x, y), F32),
        compiler_params=_params(("parallel",)),
    )(a, b)


SUBLANES = 8


def _row_tile(rows, target, mult=SUBLANES):
    best = None
    for cand in range(mult, min(rows, target) + 1, mult):
        if rows % cand == 0:
            best = cand
    return best or rows


def _ew(fn, ins, outs, tt, name):
    t = [a.shape[0] for a, tiled in ins if tiled][0]
    tt = _row_tile(t, tt)
    n_in = len(ins)
    in_specs = [pl.BlockSpec((tt, a.shape[1]), lambda i: (i, 0)) if tiled else pl.BlockSpec(a.shape, lambda i: (0, 0))
                for a, tiled in ins]
    out_specs, out_shapes = [], []
    for w, dt, kind in outs:
        if kind == "tile":
            out_specs.append(pl.BlockSpec((tt, w), lambda i: (i, 0)))
            out_shapes.append(jax.ShapeDtypeStruct((t, w), dt))
        else:
            out_specs.append(pl.BlockSpec((1, w), lambda i: (0, 0)))
            out_shapes.append(jax.ShapeDtypeStruct((1, w), F32))
    has_acc = any(kind == "acc" for _, _, kind in outs)

    def body(*refs):
        vals = fn(*[r[...] for r in refs[:n_in]])
        if not isinstance(vals, (tuple, list)):
            vals = (vals,)
        i = pl.program_id(0)
        for o_ref, v, (w, dt, kind) in zip(refs[n_in:], vals, outs):
            if kind == "tile":
                o_ref[...] = v.astype(dt)
            else:
                @pl.when(i == 0)
                def _(o_ref=o_ref, v=v):
                    o_ref[...] = v.astype(F32)

                @pl.when(i > 0)
                def _(o_ref=o_ref, v=v):
                    o_ref[...] += v.astype(F32)

    res = pl.pallas_call(
        body, name=name, grid=(t // tt,), in_specs=in_specs, out_specs=out_specs, out_shape=out_shapes,
        compiler_params=_params(("arbitrary" if has_acc else "parallel",)),
    )(*[a for a, _ in ins])
    return res


def _whole(fn, ins, outs, name):
    def body(*refs):
        vals = fn(*[r[...] for r in refs[:len(ins)]])
        if not isinstance(vals, (tuple, list)):
            vals = (vals,)
        for o_ref, v in zip(refs[len(ins):], vals):
            o_ref[...] = v.astype(o_ref.dtype)

    return pl.pallas_call(body, name=name, out_shape=[jax.ShapeDtypeStruct(s, dt) for s, dt in outs])(*ins)


def _rms(x):
    return x * lax.rsqrt(jnp.mean(x * x, axis=-1, keepdims=True) + EPS)


def _norm_mod_fn(h, g, sc, sh):
    return (_rms(h) * g) * (1.0 + sc) + sh


def _rms_gain_fn(x, g):
    return _rms(x) * g


def _gate_add_fn(h, y, gt):
    return h + gt * y


def _relu2_fn(x):
    return jnp.square(jnp.maximum(x, 0.0))


def _s5_act_fn(ypre, u, dskip):
    return jax.nn.gelu(ypre + dskip * u)


def _s5_glu_fn(z, zz, b):
    return z * jax.nn.sigmoid(zz + b)


def _ret_gate_fn(o, g):
    return _rms(o) * (g * jax.nn.sigmoid(g))


def _final_fn(h, g, tgt):
    err = _rms(h) * g - tgt
    return 0.5 * jnp.sum(jnp.mean(err * err, axis=-1, keepdims=True), axis=0, keepdims=True)


def _vjp_block(fn, n_args):
    def bwd(*vals):
        _, pull = jax.vjp(fn, *vals[:n_args])
        return pull(vals[n_args])
    return bwd


def _rope_tables(t, zero=0.0):
    d = RET_QK
    inv = ROPE_BASE ** (-jnp.arange(0, d, 2, dtype=F32) / d)
    ang = (jnp.arange(t, dtype=F32) + zero)[:, None] * inv[None, :]
    cos, sin = jnp.cos(ang), jnp.sin(ang)
    cos2, sin2 = jnp.concatenate([cos, cos], -1), jnp.concatenate([-sin, sin], -1)
    ret = (jnp.tile(cos2, (1, 8)), jnp.tile(sin2, (1, 8)))
    one, zero = jnp.ones((t, MLA_NOPE), F32), jnp.zeros((t, MLA_NOPE), F32)
    mla_c = jnp.concatenate([jnp.tile(jnp.concatenate([one, cos2], -1), (1, MLA_HEADS)), cos2, one[:, :d]], -1)
    mla_s = jnp.concatenate([jnp.tile(jnp.concatenate([zero, sin2], -1), (1, MLA_HEADS)), sin2, zero[:, :d]], -1)
    return ret, (mla_c, mla_s)


def _rope_fn(x, c, s, sign):
    w = x.shape[1]
    lane = lax.broadcasted_iota(jnp.int32, x.shape, 1)
    swapped = jnp.where((lane & 63) < 32, pltpu.roll(x, w - 32, 1), pltpu.roll(x, 32, 1))
    return x * c + swapped * (sign * s)


def _rope(x, tables, name, inverse=False, out_dtype=F32):
    c, s = tables
    fn = functools.partial(_rope_fn, sign=-1.0 if inverse else 1.0)
    return _ew(fn, [(x, True), (c, True), (s, True)], [(x.shape[1], out_dtype, "tile")], ROWS, name)[0]


SCAN_ROWS, SCAN_LANES = 256, 512


def _cmul(ar, ai, br, bi):
    return ar * br - ai * bi, ar * bi + ai * br


def _group_powers(ar, ai, reverse):
    shape = (SUBLANES, ar.shape[1])
    row = lax.broadcasted_iota(jnp.int32, shape, 0)
    pr, pi = ar, ai
    out_r, out_i = jnp.zeros(shape, F32), jnp.zeros(shape, F32)
    for e in range(1, SUBLANES + 1):
        hit = row == (SUBLANES - e if reverse else e - 1)
        out_r, out_i = jnp.where(hit, pr, out_r), jnp.where(hit, pi, out_i)
        if e < SUBLANES:
            pr, pi = _cmul(pr, pi, ar, ai)
    return out_r, out_i


def _scan_chunk(in_r_ref, in_i_ref, out_r_ref, out_i_ref, ar, ai, cr, ci, reverse, visit=None):
    rows, lanes = in_r_ref.shape
    sub = lax.broadcasted_iota(jnp.int32, (SUBLANES, lanes), 0)
    edge_r, edge_i = _group_powers(ar, ai, reverse)
    steps, pr, pi, k = [], ar, ai, 1
    while k < SUBLANES:
        steps.append((k, pr, pi))
        pr, pi = _cmul(pr, pi, pr, pi)
        k *= 2
    groups = range(rows // SUBLANES)
    for g in (reversed(groups) if reverse else groups):
        sl = slice(g * SUBLANES, (g + 1) * SUBLANES)
        xr, xi = in_r_ref[sl, :], in_i_ref[sl, :]
        for k, pr, pi in steps:
            shift = SUBLANES - k if reverse else k
            keep = sub < SUBLANES - k if reverse else sub >= k
            tr, ti = _cmul(pr, pi, pltpu.roll(xr, shift, 0), pltpu.roll(xi, shift, 0))
            xr, xi = xr + jnp.where(keep, tr, 0.0), xi + jnp.where(keep, ti, 0.0)
        tr, ti = _cmul(edge_r, edge_i, cr, ci)
        xr, xi = xr + tr, xi + ti
        out_r_ref[sl, :] = xr
        out_i_ref[sl, :] = xi
        if visit is not None:
            visit(sl, xr, xi, cr, ci)
        edge = slice(0, 1) if reverse else slice(SUBLANES - 1, SUBLANES)
        cr, ci = xr[edge, :], xi[edge, :]
    return cr, ci


def _s5_scan_specs(rows, row_block):
    assert SCAN_LANES == S5_PACK * S5_STATE
    chan = pl.BlockSpec((rows, S5_PACK * S5_CH), lambda j, i: (row_block(i), j))
    op = lambda off, shape: pl.BlockSpec((1,) + shape, lambda j, i: (j + off, 0, 0))
    blk = pl.BlockSpec((rows, SCAN_LANES), lambda j, i: (row_block(i), j))
    par = pl.BlockSpec((1, SCAN_LANES), lambda j, i: (0, j))
    return chan, op, blk, par


def _s5_scan_fwd(u, b3, a_r, a_i, name):
    t = u.shape[0]
    rows = min(SCAN_ROWS, t)
    chan, op, blk, par = _s5_scan_specs(rows, lambda i: i)

    def body(u_ref, b_re_ref, b_im_ref, ar_ref, ai_ref, or_ref, oi_ref, cr_ref, ci_ref, sr_ref, si_ref):
        i = pl.program_id(1)

        @pl.when(i == 0)
        def _():
            cr_ref[...] = jnp.zeros_like(cr_ref)
            ci_ref[...] = jnp.zeros_like(ci_ref)

        sr_ref[...] = _dot(u_ref[...], b_re_ref[0])
        si_ref[...] = _dot(u_ref[...], b_im_ref[0])
        ar, ai = ar_ref[...], ai_ref[...]
        cr, ci = _scan_chunk(sr_ref, si_ref, or_ref, oi_ref, ar, ai, cr_ref[...], ci_ref[...], reverse=False)
        cr_ref[...] = cr
        ci_ref[...] = ci

    st_r, st_i = pl.pallas_call(
        body, name=name, grid=(S5_BLOCKS, t // rows),
        in_specs=[chan, op(0, b3.shape[1:]), op(S5_BLOCKS, b3.shape[1:]), par, par], out_specs=[blk, blk],
        out_shape=[jax.ShapeDtypeStruct((t, S5_WIDTH), F32)] * 2,
        scratch_shapes=[pltpu.VMEM((1, SCAN_LANES), F32)] * 2 + [pltpu.VMEM((rows, SCAN_LANES), F32)] * 2,
        compiler_params=_params(("parallel", "arbitrary")),
    )(u, b3, b3, a_r, a_i)
    return st_r, st_i


def _s5_scan_bwd(dy, c3, st_r, st_i, a_r, a_i, name):
    t = dy.shape[0]
    rows = min(SCAN_ROWS, t)
    nc = t // rows
    chan, op, blk, par = _s5_scan_specs(rows, lambda i: nc - 1 - i)

    def body(dy_ref, c_re_ref, c_im_ref, xr_ref, xi_ref, ar_ref, ai_ref, gr_ref, gi_ref, dar_ref, dai_ref, cr_ref, ci_ref, dr_ref, di_ref):
        i = pl.program_id(1)
        dr_ref[...] = _dot(dy_ref[...], c_re_ref[0], "nt")
        di_ref[...] = _dot(dy_ref[...], c_im_ref[0], "nt")

        @pl.when(i == 0)
        def _():
            cr_ref[...] = jnp.zeros_like(cr_ref)
            ci_ref[...] = jnp.zeros_like(ci_ref)
            dar_ref[...] = jnp.zeros_like(dar_ref)
            dai_ref[...] = jnp.zeros_like(dai_ref)

        ar, ai = ar_ref[...], ai_ref[...]
        cr, ci = cr_ref[...], ci_ref[...]
        last = lax.broadcasted_iota(jnp.int32, (SUBLANES, SCAN_LANES), 0) == SUBLANES - 1
        sums = [jnp.zeros((SUBLANES, SCAN_LANES), F32), jnp.zeros((SUBLANES, SCAN_LANES), F32)]

        def visit(sl, gr, gi, next_r, next_i):
            nr = jnp.where(last, next_r, pltpu.roll(gr, SUBLANES - 1, 0))
            ni = jnp.where(last, next_i, pltpu.roll(gi, SUBLANES - 1, 0))
            xr, xi = xr_ref[sl, :], xi_ref[sl, :]
            sums[0] = sums[0] + (nr * xr + ni * xi)
            sums[1] = sums[1] + (ni * xr - nr * xi)

        first_r, first_i = _scan_chunk(dr_ref, di_ref, gr_ref, gi_ref, ar, -ai, cr, ci, reverse=True, visit=visit)
        dar_ref[...] += jnp.sum(sums[0], axis=0, keepdims=True)
        dai_ref[...] += jnp.sum(sums[1], axis=0, keepdims=True)
        cr_ref[...] = first_r
        ci_ref[...] = first_i

    return pl.pallas_call(
        body, name=name, grid=(S5_BLOCKS, nc),
        in_specs=[chan, op(0, c3.shape[1:]), op(S5_BLOCKS, c3.shape[1:]), blk, blk, par, par], out_specs=[blk, blk, par, par],
        out_shape=[jax.ShapeDtypeStruct((t, S5_WIDTH), F32)] * 2 + [jax.ShapeDtypeStruct((1, S5_WIDTH), F32)] * 2,
        scratch_shapes=[pltpu.VMEM((1, SCAN_LANES), F32)] * 2 + [pltpu.VMEM((rows, SCAN_LANES), F32)] * 2,
        compiler_params=_params(("parallel", "arbitrary")),
    )(dy, c3, c3, st_r, st_i, a_r, a_i)


def _s5_prep(lam_re, lam_im, log_dt, b_re, b_im, c_re, c_im, d_skip):
    dt = jnp.exp(log_dt)[:, None]
    mag = jnp.exp(lam_re * dt)
    ar, ai = mag * jnp.cos(lam_im * dt), mag * jnp.sin(lam_im * dt)
    den = lam_re * lam_re + lam_im * lam_im
    cr = ((ar - 1.0) * lam_re + ai * lam_im) / den
    ci = (ai * lam_re - (ar - 1.0) * lam_im) / den
    bbar_r = cr[..., None] * b_re - ci[..., None] * b_im
    bbar_i = cr[..., None] * b_im + ci[..., None] * b_re
    eye = jnp.eye(S5_PACK, dtype=F32)

    def bdiag(m):
        g, a, b = m.shape
        m4 = m.reshape(g // S5_PACK, S5_PACK, a, b)
        return (eye[None, :, None, :, None] * m4[:, :, :, None, :]).reshape(g // S5_PACK, S5_PACK * a, S5_PACK * b)

    b3 = jnp.concatenate([bdiag(bbar_r.transpose(0, 2, 1)), bdiag(bbar_i.transpose(0, 2, 1))], axis=0)
    c3 = jnp.concatenate([bdiag(c_re.transpose(0, 2, 1)), -bdiag(c_im.transpose(0, 2, 1))], axis=0)
    return b3, c3, d_skip.reshape(1, GROUP_WIDTH), ar.reshape(1, S5_WIDTH), ai.reshape(1, S5_WIDTH)


RET_UNROLL = 8


def _loop_unrolled(trips, body, init):
    factor = math.gcd(trips, RET_UNROLL)

    def several(i, carry):
        for u in range(factor):
            carry = body(i * factor + u, carry)
        return carry

    return lax.fori_loop(0, trips // factor, several, init)


def _ret_consts(lgam):
    c = RET_CHUNK
    r = lax.broadcasted_iota(jnp.int32, (c, c), 0)
    m = lax.broadcasted_iota(jnp.int32, (c, c), 1)
    rel = (r - m).astype(F32)
    decay = jnp.where(rel >= 0, jnp.exp(lgam * jnp.maximum(rel, 0.0)), 0.0)
    idx = lax.broadcasted_iota(jnp.int32, (c, 1), 0).astype(F32)
    zeta = jnp.exp(lgam * (c - 1.0 - idx))
    xi = jnp.exp(lgam * (idx + 1.0))
    return decay, zeta, xi, jnp.exp(lgam * c)


def _ret_specs(t):
    qk = lambda off: pl.BlockSpec((1, t, RET_QK), lambda h: (h + off, 0, 0))
    col = lambda off: pl.BlockSpec((t, RET_V), lambda h: (0, h + off))
    return qk, col


def _ret_fwd(qk, p_ret, lgam, name):
    t = qk.shape[1]
    nck = t // RET_CHUNK
    qk_spec, col = _ret_specs(t)

    def body(lg_ref, q_ref, k_ref, v_ref, g_ref, o_ref, y_ref):
        decay, zeta, xi, gam = _ret_consts(lg_ref[pl.program_id(0)])

        def step(n, state):
            sl = pl.ds(pl.multiple_of(n * RET_CHUNK, RET_CHUNK), RET_CHUNK)
            q, k, v = q_ref[0, sl, :], k_ref[0, sl, :] * (RET_QK ** -0.5), v_ref[sl, :]
            s = _dot(q, k, "nt") * decay
            o = _dot(s, v) + _dot(q, state) * xi
            o_ref[sl, :] = o
            y_ref[sl, :] = _ret_gate_fn(o, g_ref[sl, :]).astype(y_ref.dtype)
            return gam * state + _dot(k, zeta * v, "tn")

        _loop_unrolled(nck, step, jnp.zeros((RET_QK, RET_V), F32))

    return pl.pallas_call(
        body, name=name, grid=(RET_HEADS,), in_specs=[_SMEM, qk_spec(0), qk_spec(RET_HEADS), col(4), col(8)],
        out_specs=[col(0), col(0)],
        out_shape=[jax.ShapeDtypeStruct((t, GROUP_WIDTH), F32), jax.ShapeDtypeStruct((t, GROUP_WIDTH), BF16)],
        compiler_params=_params(("parallel",)),
    )(lgam, qk, qk, p_ret, p_ret)


def _ret_bwd(qk, p_ret, o_all, dy, lgam, name):
    t = qk.shape[1]
    nck = t // RET_CHUNK
    qk_spec, col = _ret_specs(t)
    gate_bwd = _vjp_block(_ret_gate_fn, 2)

    def body(lg_ref, q_ref, k_ref, v_ref, g_ref, o_ref, dy_ref, dq_ref, dk_ref, dv_ref, dg_ref, st_ref):
        decay, zeta, xi, gam = _ret_consts(lg_ref[pl.program_id(0)])
        scale = RET_QK ** -0.5

        def fstep(n, state):
            sl = pl.ds(pl.multiple_of(n * RET_CHUNK, RET_CHUNK), RET_CHUNK)
            st_ref[n] = state
            return gam * state + _dot(k_ref[0, sl, :] * scale, zeta * v_ref[sl, :], "tn")

        _loop_unrolled(nck, fstep, jnp.zeros((RET_QK, RET_V), F32))

        def bstep(r, grad_state):
            n = nck - 1 - r
            sl = pl.ds(pl.multiple_of(n * RET_CHUNK, RET_CHUNK), RET_CHUNK)
            q, k, v = q_ref[0, sl, :], k_ref[0, sl, :] * scale, v_ref[sl, :]
            d_o, dg = gate_bwd(o_ref[sl, :], g_ref[sl, :], dy_ref[sl, :])
            dg_ref[sl, :] = dg.astype(dg_ref.dtype)
            s = _dot(q, k, "nt") * decay
            ds = _dot(d_o, v, "nt") * decay
            xdo = xi * d_o
            dq_ref[0, sl, :] = _dot(ds, k) + _dot(xdo, st_ref[n], "nt")
            dk_ref[0, sl, :] = (_dot(ds, q, "tn") + _dot(zeta * v, grad_state, "nt")) * scale
            dv_ref[sl, :] = (_dot(s, d_o, "tn") + zeta * _dot(k, grad_state)).astype(dv_ref.dtype)
            return gam * grad_state + _dot(q, xdo, "tn")

        _loop_unrolled(nck, bstep, jnp.zeros((RET_QK, RET_V), F32))

    hd = pl.BlockSpec((1, t, RET_QK), lambda h: (h, 0, 0))
    return pl.pallas_call(
        body, name=name, grid=(RET_HEADS,),
        in_specs=[_SMEM, qk_spec(0), qk_spec(RET_HEADS), col(4), col(8), col(0), col(0)],
        out_specs=[hd, hd, col(0), col(0)],
        out_shape=[jax.ShapeDtypeStruct((RET_HEADS, t, RET_QK), F32)] * 2 + [jax.ShapeDtypeStruct((t, GROUP_WIDTH), BF16)] * 2,
        scratch_shapes=[pltpu.VMEM((nck, RET_QK, RET_V), F32)], compiler_params=_params(("parallel",)),
    )(lgam, qk, qk, p_ret, p_ret, o_all, dy)


SWA_GROUP = SWA_HEADS // SWA_KV_HEADS
SWA_SCALE = SWA_HD ** -0.5


def _swa_mask(n):
    rows = SWA_GROUP * WINDOW
    r = lax.broadcasted_iota(jnp.int32, (rows, 2 * WINDOW), 0) & (WINDOW - 1)
    j = lax.broadcasted_iota(jnp.int32, (rows, 2 * WINDOW), 1)
    dist = r + WINDOW - j
    return (dist >= 0) & (dist < WINDOW) & (n * WINDOW + j - WINDOW >= 0)


def _swa_sink_rows(sink_ref, kv):
    row = lax.broadcasted_iota(jnp.int32, (SWA_GROUP * WINDOW, 1), 0)
    sink = jnp.zeros((SWA_GROUP * WINDOW, 1), F32)
    for g in range(SWA_GROUP):
        sink = jnp.where(row >= g * WINDOW, sink_ref[kv * SWA_GROUP + g], sink)
    return sink


def _swa_pad_keys(n, k_ref, v_ref, kp_ref, vp_ref):
    @pl.when(n == 0)
    def _():
        zero = jnp.zeros((WINDOW, SWA_HD), F32)
        kp_ref[0:WINDOW, :] = zero
        vp_ref[0:WINDOW, :] = zero
        kp_ref[WINDOW:, :] = k_ref[0]
        vp_ref[WINDOW:, :] = v_ref[0]


SWA_STEP_BLOCKS = 4


def _swa_specs(t):
    per_step = math.gcd(t // WINDOW, SWA_STEP_BLOCKS)
    blk = lambda w: pl.BlockSpec((SWA_GROUP, per_step * WINDOW, w), lambda kv, n: (kv, n, 0))
    kspec = lambda off: pl.BlockSpec((1, t, SWA_HD), lambda kv, n: (SWA_HEADS + off + kv, 0, 0))
    return blk, kspec, per_step


def _swa_fwd(qkv, sinks, name):
    t = qkv.shape[1]
    rows = SWA_GROUP * WINDOW
    blk, kspec, per_step = _swa_specs(t)

    def body(sink_ref, q_ref, k_ref, v_ref, o_ref, lse_ref, kp_ref, vp_ref):
        kv, step = pl.program_id(0), pl.program_id(1)
        _swa_pad_keys(step, k_ref, v_ref, kp_ref, vp_ref)
        sink = _swa_sink_rows(sink_ref, kv)
        for u in range(per_step):
            n = step * per_step + u
            here = slice(u * WINDOW, (u + 1) * WINDOW)
            win = pl.ds(pl.multiple_of(n * WINDOW, WINDOW), 2 * WINDOW)
            s = _dot(q_ref[:, here, :].reshape(rows, SWA_HD), kp_ref[win, :], "nt") * SWA_SCALE
            s = jnp.where(_swa_mask(n), s, NEG)
            m = jnp.maximum(jnp.max(s, axis=-1, keepdims=True), sink)
            p = jnp.exp(s - m)
            den = jnp.sum(p, axis=-1, keepdims=True) + jnp.exp(sink - m)
            o_ref[:, here, :] = _dot(p / den, vp_ref[win, :]).reshape(SWA_GROUP, WINDOW, SWA_HD)
            lse_ref[:, here, :] = (m + jnp.log(den)).reshape(SWA_GROUP, WINDOW, 1)

    return pl.pallas_call(
        body, name=name, grid=(SWA_KV_HEADS, t // WINDOW // per_step),
        in_specs=[_SMEM, blk(SWA_HD), kspec(0), kspec(SWA_KV_HEADS)], out_specs=[blk(SWA_HD), blk(1)],
        out_shape=[jax.ShapeDtypeStruct((SWA_HEADS, t, SWA_HD), F32), jax.ShapeDtypeStruct((SWA_HEADS, t, 1), F32)],
        scratch_shapes=[pltpu.VMEM((t + WINDOW, SWA_HD), F32)] * 2, compiler_params=_params(("parallel", "arbitrary")),
    )(sinks, qkv, qkv, qkv)


def _swa_bwd(qkv, o, lse, d_o, sinks, name):
    t = qkv.shape[1]
    nb = t // WINDOW
    rows = SWA_GROUP * WINDOW
    blk, kspec, per_step = _swa_specs(t)

    def body(sink_ref, q_ref, k_ref, v_ref, o_ref, lse_ref, do_ref, dq_ref, dk_ref, dv_ref, dsink_ref,
             kp_ref, vp_ref, dkp_ref, dvp_ref):
        kv, step = pl.program_id(0), pl.program_id(1)
        _swa_pad_keys(step, k_ref, v_ref, kp_ref, vp_ref)

        @pl.when(step == 0)
        def _():
            dkp_ref[...] = jnp.zeros_like(dkp_ref)
            dvp_ref[...] = jnp.zeros_like(dvp_ref)
            dsink_ref[...] = jnp.zeros_like(dsink_ref)

        sink = _swa_sink_rows(sink_ref, kv)
        head = lax.broadcasted_iota(jnp.int32, (SWA_GROUP, 128), 0)
        acc = jnp.zeros((SWA_GROUP, 128), F32)
        for u in range(per_step):
            n = step * per_step + u
            here = slice(u * WINDOW, (u + 1) * WINDOW)
            win = pl.ds(pl.multiple_of(n * WINDOW, WINDOW), 2 * WINDOW)
            q, dout = q_ref[:, here, :].reshape(rows, SWA_HD), do_ref[:, here, :].reshape(rows, SWA_HD)
            lse_n = lse_ref[:, here, :].reshape(rows, 1)
            s = _dot(q, kp_ref[win, :], "nt") * SWA_SCALE
            s = jnp.where(_swa_mask(n), s, NEG)
            p = jnp.exp(s - lse_n)
            delta = jnp.sum(dout * o_ref[:, here, :].reshape(rows, SWA_HD), axis=-1, keepdims=True)
            ds = p * (_dot(dout, vp_ref[win, :], "nt") - delta)
            dq_ref[:, here, :] = (_dot(ds, kp_ref[win, :]) * SWA_SCALE).reshape(SWA_GROUP, WINDOW, SWA_HD).astype(dq_ref.dtype)
            dkp_ref[win, :] += _dot(ds, q, "tn") * SWA_SCALE
            dvp_ref[win, :] += _dot(p, dout, "tn")
            term = jnp.exp(sink - lse_n) * delta
            for g in range(SWA_GROUP):
                acc = jnp.where(head == g, acc + jnp.sum(term[g * WINDOW:(g + 1) * WINDOW], axis=0, keepdims=True), acc)
        dsink_ref[0] -= acc

        @pl.when(step == nb // per_step - 1)
        def _():
            dk_ref[0] = dkp_ref[WINDOW:, :].astype(dk_ref.dtype)
            dv_ref[0] = dvp_ref[WINDOW:, :].astype(dv_ref.dtype)

    kout = pl.BlockSpec((1, t, SWA_HD), lambda kv, n: (kv, 0, 0))
    dq, dk, dv, dsink = pl.pallas_call(
        body, name=name, grid=(SWA_KV_HEADS, nb // per_step),
        in_specs=[_SMEM, blk(SWA_HD), kspec(0), kspec(SWA_KV_HEADS), blk(SWA_HD), blk(1), blk(SWA_HD)],
        out_specs=[blk(SWA_HD), kout, kout, pl.BlockSpec((1, SWA_GROUP, 128), lambda kv, n: (kv, 0, 0))],
        out_shape=[jax.ShapeDtypeStruct((SWA_HEADS, t, SWA_HD), BF16), jax.ShapeDtypeStruct((SWA_KV_HEADS, t, SWA_HD), BF16),
                   jax.ShapeDtypeStruct((SWA_KV_HEADS, t, SWA_HD), BF16), jax.ShapeDtypeStruct((SWA_KV_HEADS, SWA_GROUP, 128), F32)],
        scratch_shapes=[pltpu.VMEM((t + WINDOW, SWA_HD), F32)] * 4, compiler_params=_params(("parallel", "arbitrary")),
    )(sinks, qkv, qkv, qkv, o, lse, d_o)
    return jnp.concatenate([dq, dk, dv], axis=0), dsink[:, :, 0].reshape(SWA_HEADS)


MLA_SCALE = (MLA_NOPE + MLA_ROPE) ** -0.5
MLA_TILE = 512
MLA_KEY_TILE = 512
MLA_BWD_TILE = 512


def _mla_diag(s):
    r = lax.broadcasted_iota(jnp.int32, s.shape, 0)
    c = lax.broadcasted_iota(jnp.int32, s.shape, 1)
    return jnp.where(c <= r, s, NEG)


def _mla_specs(t, tile):
    whole = lambda w, off: pl.BlockSpec((t, w), lambda h, i: (0, 2 * h + off))
    head = lambda w: pl.BlockSpec((1, t, w), lambda h, i: (h, 0, 0))
    key_rope = pl.BlockSpec((1, t, MLA_ROPE), lambda h, i: (MLA_HEADS, 0, 0))
    tile_of = lambda w: pl.BlockSpec((1, tile, w), lambda h, i: (h, i, 0))
    return whole, head, key_rope, tile_of


def _mla_attend(qn, rot, kv, name):
    t = qn.shape[1]
    tile = min(MLA_TILE, t)
    ktile = min(MLA_KEY_TILE, t)
    ratio = ktile // tile
    whole, head, key_rope, tile_of = _mla_specs(t, tile)

    def body(qn_ref, qr_ref, kn_ref, kr_ref, v_ref, o_ref, lse_ref, m_ref, l_ref, acc_ref):
        i = pl.program_id(1)
        qn_b, qr_b = qn_ref[0], qr_ref[0]

        def rows(j):
            return pl.ds(pl.multiple_of(j * ktile, ktile), ktile)

        def scores(j):
            return (_dot(qn_b, kn_ref[rows(j), :], "nt") + _dot(qr_b, kr_ref[0, rows(j), :], "nt")) * MLA_SCALE

        def causal(s, j):
            qpos = i * tile + lax.broadcasted_iota(jnp.int32, s.shape, 0)
            kpos = j * ktile + lax.broadcasted_iota(jnp.int32, s.shape, 1)
            return jnp.where(kpos <= qpos, s, NEG)

        def update(s, j):
            m_old = m_ref[...]
            m_new = jnp.maximum(m_old, jnp.max(s, axis=-1, keepdims=True))
            alpha = jnp.exp(m_old - m_new)
            p = jnp.exp(s - m_new)
            l_ref[...] = alpha * l_ref[...] + jnp.sum(p, axis=-1, keepdims=True)
            acc_ref[...] = alpha * acc_ref[...] + _dot(p, v_ref[rows(j), :])
            m_ref[...] = m_new

        m_ref[...] = jnp.full_like(m_ref, NEG)
        l_ref[...] = jnp.zeros_like(l_ref)
        acc_ref[...] = jnp.zeros_like(acc_ref)

        def step(j, s_cur):
            s_next = scores(j + 1)
            update(s_cur, j)
            return s_next

        last = i // ratio
        s_last = lax.fori_loop(0, last, step, scores(0))
        update(causal(s_last, last), last)
        o_ref[...] = acc_ref[...] / l_ref[...]
        lse_ref[0] = m_ref[...] + jnp.log(l_ref[...])

    return pl.pallas_call(
        body, name=name, grid=(MLA_HEADS, t // tile),
        in_specs=[tile_of(MLA_NOPE), tile_of(MLA_ROPE), whole(MLA_NOPE, 0), key_rope, whole(MLA_V, 1)],
        out_specs=[pl.BlockSpec((tile, MLA_V), lambda h, i: (i, h)), tile_of(1)],
        out_shape=[jax.ShapeDtypeStruct((t, GROUP_WIDTH), F32), jax.ShapeDtypeStruct((MLA_HEADS, t, 1), F32)],
        scratch_shapes=[pltpu.VMEM((tile, 1), F32), pltpu.VMEM((tile, 1), F32), pltpu.VMEM((tile, MLA_V), F32)],
        compiler_params=_params(("parallel", "parallel")),
    )(qn, rot, kv, rot, kv)


def _mla_attend_bwd(qn, rot, kv, lse, o, d_o, name):
    t = qn.shape[1]
    tile = min(MLA_BWD_TILE, t)
    nt = t // tile
    whole, head, key_rope, tile_of = _mla_specs(t, tile)

    def body(qn_ref, qr_ref, kn_ref, kr_ref, v_ref, lse_ref, o_ref, do_ref, dqn_ref, dqr_ref, dkn_ref, dv_ref, dkr_ref, dl_ref):
        j = pl.program_id(1)

        @pl.when(j == 0)
        def _():
            dqn_ref[...] = jnp.zeros_like(dqn_ref)
            dqr_ref[...] = jnp.zeros_like(dqr_ref)
            dl_ref[0] = jnp.sum(do_ref[...] * o_ref[...], axis=-1, keepdims=True)

        dkn_ref[...] = jnp.zeros_like(dkn_ref)
        dv_ref[...] = jnp.zeros_like(dv_ref)
        dkr_ref[...] = jnp.zeros_like(dkr_ref)
        kn_b, kr_b, v_b = kn_ref[...], kr_ref[0], v_ref[...]

        def block(i, diagonal):
            sl = pl.ds(pl.multiple_of(i * tile, tile), tile)
            qn_b, qr_b, dout = qn_ref[0, sl, :], qr_ref[0, sl, :], do_ref[sl, :]
            s = (_dot(qn_b, kn_b, "nt") + _dot(qr_b, kr_b, "nt")) * MLA_SCALE
            if diagonal:
                s = _mla_diag(s)
            p = jnp.exp(s - lse_ref[0, sl, :])
            ds = p * (_dot(dout, v_b, "nt") - dl_ref[0, sl, :]) * MLA_SCALE
            dv_ref[...] += _dot(p, dout, "tn")
            dkn_ref[...] += _dot(ds, qn_b, "tn")
            dkr_ref[0] += _dot(ds, qr_b, "tn")
            dqn_ref[0, sl, :] += _dot(ds, kn_b)
            dqr_ref[0, sl, :] += _dot(ds, kr_b)

        block(j, True)

        def step(i, carry):
            block(i, False)
            return carry

        lax.fori_loop(j + 1, nt, step, 0)

    key_tile = lambda w, off: pl.BlockSpec((tile, w), lambda h, j: (j, 2 * h + off))
    out_tile = pl.BlockSpec((tile, MLA_V), lambda h, j: (j, h))
    return pl.pallas_call(
        body, name=name, grid=(MLA_HEADS, nt),
        in_specs=[head(MLA_NOPE), head(MLA_ROPE), key_tile(MLA_NOPE, 0), pl.BlockSpec((1, tile, MLA_ROPE), lambda h, j: (MLA_HEADS, j, 0)),
                  key_tile(MLA_V, 1), head(1), pl.BlockSpec((t, MLA_V), lambda h, j: (0, h)), pl.BlockSpec((t, MLA_V), lambda h, j: (0, h))],
        out_specs=[head(MLA_NOPE), head(MLA_ROPE), out_tile, out_tile, tile_of(MLA_ROPE)],
        out_shape=[jax.ShapeDtypeStruct((MLA_HEADS, t, MLA_NOPE), F32), jax.ShapeDtypeStruct((MLA_HEADS, t, MLA_ROPE), F32),
                   jax.ShapeDtypeStruct((t, MLA_HEADS * MLA_NOPE), F32), jax.ShapeDtypeStruct((t, MLA_HEADS * MLA_V), F32),
                   jax.ShapeDtypeStruct((MLA_HEADS, t, MLA_ROPE), F32)],
        scratch_shapes=[pltpu.VMEM((1, t, 1), F32)], compiler_params=_params(("parallel", "arbitrary")),
    )(qn, rot, kv, rot, kv, lse, o, d_o)


def _place():
    return lax.axis_index("x"), lax.axis_index("y"), lax.axis_index("c")


def _all_gather(arrs, name):
    n = len(arrs)

    def body(*refs):
        x_refs, o_refs = refs[:n], refs[n:2 * n]
        send_sems, recv_sems, local_sems = refs[2 * n:]
        x, y, c = _place()
        me, sibling = (x, y, c), (x, y, 1 - c)
        chips = [(1 - x, y), (x, 1 - y), (1 - x, 1 - y)]

        def slot(a, p):
            return o_refs[a].at[4 * p[0] + 2 * p[1] + p[2]]

        def copy(a, k, block, to, src=None):
            return pltpu.make_async_remote_copy(
                src_ref=slot(a, block) if src is None else src, dst_ref=slot(a, block),
                send_sem=send_sems.at[a, k], recv_sem=recv_sems.at[a, k], device_id=to, device_id_type=MESH_ID)

        mine = [pltpu.make_async_copy(x_refs[a], slot(a, me), local_sems.at[a]) for a in range(n)]
        for cp in mine:
            cp.start()
        first = []
        for a in range(n):
            first.append(copy(a, 0, me, sibling, src=x_refs[a]))
            first += [copy(a, 1 + j, me, (*chip, c), src=x_refs[a]) for j, chip in enumerate(chips)]
        for cp in first:
            cp.start()
        passed = []
        for j, chip in enumerate(chips):
            for a in range(n):
                copy(a, 1 + j, (*chip, c), me).wait_recv()
                cp = copy(a, 4 + j, (*chip, c), sibling)
                cp.start()
                passed.append(cp)
        for a in range(n):
            copy(a, 0, sibling, me).wait_recv()
            for j, chip in enumerate(chips):
                copy(a, 4 + j, (*chip, 1 - c), me).wait_recv()
        for cp in first + passed:
            cp.wait_send()
        for cp in mine:
            cp.wait()

    return pl.pallas_call(
        body, name=name, in_specs=[_ANY] * n, out_specs=[_ANY] * n,
        out_shape=[jax.ShapeDtypeStruct((N_DEV,) + a.shape, a.dtype) for a in arrs],
        scratch_shapes=[pltpu.SemaphoreType.DMA((n, 7)), pltpu.SemaphoreType.DMA((n, 7)), pltpu.SemaphoreType.DMA((n,))],
    )(*arrs)


def _pass_to_sibling(arrs, name):
    n = len(arrs)

    def body(*refs):
        a_refs, o_refs = refs[:n], refs[n:2 * n]
        send_sems, recv_sems = refs[2 * n:]
        x, y, c = _place()
        chips = [(1 - x, y), (x, 1 - y), (1 - x, 1 - y)]
        slot = lambda px, py, pc: 4 * px + 2 * py + pc
        sends, recvs = [], []
        for a in range(n):
            for j, (px, py) in enumerate(chips):
                sends.append(pltpu.make_async_remote_copy(
                    src_ref=a_refs[a].at[slot(px, py, c)], dst_ref=o_refs[a].at[slot(px, py, c)], send_sem=send_sems.at[a, j],
                    recv_sem=recv_sems.at[a, j], device_id=(x, y, 1 - c), device_id_type=MESH_ID))
                recvs.append(pltpu.make_async_remote_copy(
                    src_ref=a_refs[a].at[slot(px, py, c)], dst_ref=o_refs[a].at[slot(px, py, 1 - c)], send_sem=send_sems.at[a, j],
                    recv_sem=recv_sems.at[a, j], device_id=(x, y, 1 - c), device_id_type=MESH_ID))
        for cp in sends:
            cp.start()
        for cp in sends:
            cp.wait_send()
        for cp in recvs:
            cp.wait_recv()

    return pl.pallas_call(
        body, name=name, in_specs=[_ANY] * n, out_specs=[_ANY] * n,
        out_shape=[jax.ShapeDtypeStruct(a.shape, a.dtype) for a in arrs], input_output_aliases={i: i for i in range(n)},
        scratch_shapes=[pltpu.SemaphoreType.DMA((n, 3)), pltpu.SemaphoreType.DMA((n, 3))],
    )(*arrs)


def _scatter_core(grads, name):
    n = len(grads)

    def body(*refs):
        g_refs, got_refs = refs[:n], refs[n:2 * n]
        send_sems, recv_sems = refs[2 * n:]
        x, y, c = _place()
        sends = [pltpu.make_async_remote_copy(
            src_ref=g_refs[a].at[2 * q + 1 - c], dst_ref=got_refs[a].at[q], send_sem=send_sems.at[a, q],
            recv_sem=recv_sems.at[a, q], device_id=(x, y, 1 - c), device_id_type=MESH_ID) for a in range(n) for q in range(4)]
        for cp in sends:
            cp.start()
        for cp in sends:
            cp.wait()

    return pl.pallas_call(
        body, name=name, in_specs=[_ANY] * n, out_specs=[_ANY] * n,
        out_shape=[jax.ShapeDtypeStruct((4,) + g.shape[1:], g.dtype) for g in grads],
        scratch_shapes=[pltpu.SemaphoreType.DMA((n, 4)), pltpu.SemaphoreType.DMA((n, 4))],
    )(*grads)


def _scatter_chips(parts, name):
    n = len(parts)

    def body(*refs):
        p_refs, o_refs = refs[:n], refs[n:2 * n]
        send_sems, recv_sems = refs[2 * n:]
        x, y, c = _place()
        chips = [(1 - x, y), (x, 1 - y), (1 - x, 1 - y)]
        sends = [pltpu.make_async_remote_copy(
            src_ref=p_refs[a].at[2 * px + py], dst_ref=o_refs[a].at[j], send_sem=send_sems.at[a, j],
            recv_sem=recv_sems.at[a, j], device_id=(px, py, c), device_id_type=MESH_ID)
            for a in range(n) for j, (px, py) in enumerate(chips)]
        for cp in sends:
            cp.start()
        for cp in sends:
            cp.wait()

    return pl.pallas_call(
        body, name=name, in_specs=[_ANY] * n, out_specs=[_ANY] * n,
        out_shape=[jax.ShapeDtypeStruct((3,) + p.shape[1:], p.dtype) for p in parts],
        scratch_shapes=[pltpu.SemaphoreType.DMA((n, 3)), pltpu.SemaphoreType.DMA((n, 3))],
    )(*parts)


GATHER_PEERS = [(0, 0, 1), (1, 0, 0), (0, 1, 0), (1, 1, 0)]
CHIP_PEERS = [(1, 0, 0), (0, 1, 0), (1, 1, 0)]
CORE_PEERS = [(0, 0, 1)] * 4
_HBM = pl.BlockSpec(memory_space=pltpu.HBM)
_SEM = pl.BlockSpec(memory_space=pltpu.SEMAPHORE)
_EFFECT = pltpu.SideEffectType.DATAFLOW_SIDE_EFFECTING


def _push_copies(src_refs, land_refs, send_sems, recv_sems, peers, src_of, slot_of):
    place = _place()
    flip = lambda v, f: 1 - v if f else v
    return [pltpu.make_async_remote_copy(
        src_ref=src_of(src_refs[a], k), dst_ref=land_refs[a].at[slot_of(k)], send_sem=send_sems[a], recv_sem=recv_sems[a],
        device_id=tuple(flip(v, f) for v, f in zip(place, peer)), device_id_type=MESH_ID)
        for a in range(len(src_refs)) for k, peer in enumerate(peers)]


def _push_start(srcs, land_shapes, peers, src_of, slot_of, name, after=None):
    n = len(srcs)
    extra = [] if after is None else [after]

    def body(*refs):
        src_refs, land_refs = refs[:n], refs[n:2 * n]
        refs = refs[2 * n + len(extra):]
        send_sems, recv_sems = refs[:n], refs[n:2 * n]
        token = refs[-1]
        for cp in _push_copies(src_refs, land_refs, send_sems, recv_sems, peers, src_of, slot_of):
            cp.start()
        token[...] = jnp.zeros_like(token)

    sems = [pltpu.SemaphoreType.DMA(())] * (2 * n)
    lands = [pltpu.with_memory_space_constraint(lax.empty(s.shape, s.dtype), pltpu.HBM) for s in land_shapes]
    res = pl.pallas_call(
        body, name=name, in_specs=[_HBM] * (2 * n) + [_ANY] * len(extra),
        out_specs=[_SEM] * (2 * n) + [_HBM] * (2 * n) + [pl.BlockSpec(memory_space=pltpu.VMEM)],
        out_shape=sems + [pltpu.HBM(s.shape, s.dtype) for s in srcs] + [pltpu.HBM(s.shape, s.dtype) for s in land_shapes]
        + [jax.ShapeDtypeStruct((8, 128), F32)],
        input_output_aliases={i: 2 * n + i for i in range(2 * n)},
        compiler_params=pltpu.CompilerParams(has_side_effects=_EFFECT),
    )(*[pltpu.with_memory_space_constraint(s, pltpu.HBM) for s in srcs], *lands, *extra)
    return list(res[:n]), list(res[n:2 * n]), list(res[2 * n:3 * n]), list(res[3 * n:4 * n]), res[-1]


def _push_wait(send_sems, recv_sems, srcs, lands, after, peers, src_of, slot_of, name):
    n = len(srcs)

    def body(*refs):
        src_refs, land_refs = refs[:n], refs[n:2 * n]
        s_sems, r_sems = refs[2 * n:3 * n], refs[3 * n:4 * n]
        copies = _push_copies(src_refs, land_refs, s_sems, r_sems, peers, src_of, slot_of)
        for cp in copies:
            cp.wait_send()
        for cp in copies:
            cp.wait_recv()

    res = pl.pallas_call(
        body, name=name, in_specs=[_HBM] * (2 * n) + [_SEM] * (2 * n) + [_ANY], out_specs=[_HBM] * (2 * n),
        out_shape=[pltpu.HBM(s.shape, s.dtype) for s in srcs] + [pltpu.HBM(s.shape, s.dtype) for s in lands],
        input_output_aliases={i: i for i in range(2 * n)},
        compiler_params=pltpu.CompilerParams(has_side_effects=_EFFECT),
    )(*srcs, *lands, *send_sems, *recv_sems, after)
    return list(res[:n]), list(res[n:])


def _pick_sum(picked, rest, index, pick_of, out_dtype, name):
    nq, r, cdim = rest.shape
    one = nq == 3
    tr = _row_tile(r, 512, 16)
    tc = 512 if cdim % 512 == 0 else cdim
    grid = (1 if one else nq, r // tr, cdim // tc)

    def body(i_ref, p_ref, r_ref, o_ref):
        acc = p_ref[0].astype(F32)
        if one:
            for j in range(3):
                acc = acc + r_ref[j].astype(F32)
            o_ref[...] = acc.astype(out_dtype)
        else:
            o_ref[0] = (acc + r_ref[0].astype(F32)).astype(out_dtype)

    spec = pltpu.PrefetchScalarGridSpec(
        num_scalar_prefetch=1, grid=grid,
        in_specs=[pl.BlockSpec((1, tr, tc), lambda q, i, j, i_ref: (pick_of(q, i_ref[0]), i, j)),
                  pl.BlockSpec((3, tr, tc), lambda q, i, j, i_ref: (0, i, j)) if one else pl.BlockSpec((1, tr, tc), lambda q, i, j, i_ref: (q, i, j))],
        out_specs=pl.BlockSpec((tr, tc), lambda q, i, j, i_ref: (i, j)) if one else pl.BlockSpec((1, tr, tc), lambda q, i, j, i_ref: (q, i, j)))
    return pl.pallas_call(
        body, name=name, grid_spec=spec,
        out_shape=jax.ShapeDtypeStruct((r, cdim) if one else (nq, r, cdim), out_dtype),
        compiler_params=_params(("parallel", "parallel", "parallel")),
    )(index.astype(jnp.int32).reshape(1), picked, rest)


def _adamw_fn(w, g, m, v):
    m = ADAM_B1 * m + (1.0 - ADAM_B1) * g
    v = ADAM_B2 * v + (1.0 - ADAM_B2) * jnp.square(g)
    m_hat = m / (1.0 - ADAM_B1 ** ADAM_STEP)
    v_hat = v / (1.0 - ADAM_B2 ** ADAM_STEP)
    delta = -ADAM_LR * (m_hat / (jnp.sqrt(v_hat) + ADAM_EPS) + ADAM_WD * w)
    return delta, m, v


def _as2d(a):
    return a.reshape(-1, a.shape[-1])


def _adamw_shard(w, g, m, v, name):
    shape = w.shape
    ins = [_as2d(a) for a in (w, g, m, v)]
    cols = ins[0].shape[1]
    outs = _ew(_adamw_fn, [(a, True) for a in ins], [(cols, F32, "tile")] * 3, 256, name)
    return [o.reshape(shape) for o in outs]


def _adamw_small(ws, gs, ms, vs, name):
    shapes = [w.shape for w in ws]
    flat = lambda a: a.reshape(-1, 128) if a.size % 128 == 0 else a.reshape(1, -1)
    ins = [flat(a) for grp in zip(ws, gs, ms, vs) for a in grp]
    k = len(ws)

    def fn(*vals):
        out = []
        for i in range(k):
            out += list(_adamw_fn(*vals[4 * i:4 * i + 4]))
        return out

    outs = _whole(fn, ins, [(ins[4 * (i // 3)].shape, F32) for i in range(3 * k)], name)
    deltas = [outs[3 * i].reshape(shapes[i]) for i in range(k)]
    new_m = [outs[3 * i + 1].reshape(shapes[i]) for i in range(k)]
    new_v = [outs[3 * i + 2].reshape(shapes[i]) for i in range(k)]
    return deltas, new_m, new_v


def _sum8(stacked, name):
    def fn(a):
        s = a[0:1]
        for i in range(1, N_DEV):
            s = s + a[i:i + 1]
        return s
    w = stacked.shape[1]
    tw = 8192
    if w % tw:
        return _whole(fn, [stacked], [((1, w), F32)], name)[0]

    def body(a_ref, o_ref):
        o_ref[...] = fn(a_ref[...])

    return pl.pallas_call(body, name=name, grid=(w // tw,), in_specs=[pl.BlockSpec((N_DEV, tw), lambda i: (0, i))],
                          out_specs=pl.BlockSpec((1, tw), lambda i: (0, i)), out_shape=jax.ShapeDtypeStruct((1, w), F32))(stacked)


ROWS = 512


def _split_heads(p, nh):
    t = p.shape[0]
    return p.reshape(t, nh, p.shape[1] // nh).transpose(1, 0, 2)


def _merge_heads(p):
    nh, t, d = p.shape
    return p.transpose(1, 0, 2).reshape(t, nh * d)


def _layer_fwd(h, mod, w, small, rope, l, late=None):
    sh1, sc1, gt1, sh2, sc2, gt2 = mod
    rope_ret, rope_mla = rope
    t = h.shape[0]
    nm = lambda s: f"l{l}_{s}"
    a1 = _ew(_norm_mod_fn, [(h, True), (small["norm1_g"], False), (sc1, False), (sh1, False)], [(D_MODEL, BF16, "tile")], ROWS, nm("norm1"))[0]
    p_s5 = _mm(a1, w["w_in_t"], "nt", 512, 512, 2048, name=nm("proj_s5"), n=512)
    p_ret = _mm(a1, w["w_in_t"], "nt", 512, 512, 2048, name=nm("proj_ret"), b_off=1, n=1536)
    p_swa = _mm(a1, w["w_in_t"], "nt", 512, 256, 2048, name=nm("proj_swa"), b_off=8, n=768)
    p_mla = _mm(a1, w["w_in_t"][2816:], "nt", 512, 576, 2048, name=nm("proj_mla"))
    b3, c3, dskip, a_r, a_i = small["s5"]
    st_r, st_i = _s5_scan_fwd(p_s5, b3, a_r, a_i, nm("s5_scan"))
    ypre = _mm_blocks(st_r, c3[:S5_BLOCKS], "nn", 512, pair=(st_i, c3[S5_BLOCKS:]), name=nm("s5_y"))
    z = _ew(_s5_act_fn, [(ypre, True), (p_s5, True), (dskip, False)], [(GROUP_WIDTH, F32, "tile")], ROWS, nm("s5_act"))[0]
    zz = _mm(z, w["glu_w"], "nn", 512, 512, 512, name=nm("s5_zz"))
    y_s5 = _ew(_s5_glu_fn, [(z, True), (zz, True), (small["s5_glu_b"], False)], [(GROUP_WIDTH, BF16, "tile")], ROWS, nm("s5_glu"))[0]
    qk_ret = _split_heads(_rope(p_ret[:, :2 * RET_HEADS * RET_QK], rope_ret, nm("ret_rope")), 2 * RET_HEADS)
    o_ret, y_ret = _ret_fwd(qk_ret, p_ret, small["ret_lgam"], nm("ret"))
    qkv_swa = _split_heads(p_swa, 12)
    o_swa, lse_swa = _swa_fwd(qkv_swa, small["swa_sinks"], nm("swa"))
    y_swa = _merge_heads(o_swa).astype(BF16)
    cq, ckv, kr = p_mla[:, :MLA_Q_RANK], p_mla[:, MLA_Q_RANK:MLA_Q_RANK + MLA_KV_RANK], p_mla[:, MLA_Q_RANK + MLA_KV_RANK:]
    cqn = _ew(_rms_gain_fn, [(cq, True), (small["mla_q_norm"], False)], [(MLA_Q_RANK, BF16, "tile")], ROWS, nm("mla_qnorm"))[0]
    ckvn = _ew(_rms_gain_fn, [(ckv, True), (small["mla_kv_norm"], False)], [(MLA_KV_RANK, BF16, "tile")], ROWS, nm("mla_kvnorm"))[0]
    q_full = _mm(cqn, w["w_uq_t"], "nt", 512, 768, 384, name=nm("mla_q"))
    kv_full = _mm(ckvn, w["w_ukv_t"], "nt", 512, 1024, 128, BF16, name=nm("mla_kv"))
    nq = q_full.shape[1]
    roped = _rope(jnp.concatenate([q_full, kr, jnp.zeros_like(kr)], axis=1), rope_mla, nm("mla_rope"), out_dtype=BF16)
    q4 = roped[:, :nq].reshape(t, MLA_HEADS, MLA_NOPE + MLA_ROPE)
    qn = q4[:, :, :MLA_NOPE].transpose(1, 0, 2)
    rot = jnp.concatenate([q4[:, :, MLA_NOPE:].transpose(1, 0, 2), roped[None, :, nq:nq + MLA_ROPE]], axis=0)
    o_mla, lse_mla = _mla_attend(qn, rot, kv_full, nm("mla"))
    cat = jnp.concatenate([y_s5, y_ret, y_swa, o_mla.astype(BF16)], axis=1)
    if late is not None:
        w = {**w, **late(lse_mla)}
    mixed = _mm(cat, w["w_out"], "nn", 512, 1024, 2048, name=nm("out_proj"))
    h1 = _ew(_gate_add_fn, [(h, True), (mixed, True), (gt1, False)], [(D_MODEL, F32, "tile")], ROWS, nm("res1"))[0]
    a2 = _ew(_norm_mod_fn, [(h1, True), (small["norm2_g"], False), (sc2, False), (sh2, False)], [(D_MODEL, BF16, "tile")], ROWS, nm("norm2"))[0]
    hid, act = _mm(a2, w["w1_t"], "nt", 1024, 1024, 2048, name=nm("mlp1"), epi=lambda acc: (acc, _relu2_fn(acc)), epi_outs=[F32, BF16])
    mo = _mm(act, w["w2"], "nn", 1024, 1024, 2048, name=nm("mlp2"))
    h2 = _ew(_gate_add_fn, [(h1, True), (mo, True), (gt2, False)], [(D_MODEL, F32, "tile")], ROWS, nm("res2"))[0]
    saved = dict(w=w, h=h, a1=a1, p_s5=p_s5, p_ret=p_ret, st_r=st_r, st_i=st_i, ypre=ypre, z=z, zz=zz, qk_ret=qk_ret, o_ret=o_ret,
                 qkv_swa=qkv_swa, o_swa=o_swa, lse_swa=lse_swa, cq=cq, ckv=ckv, cqn=cqn, ckvn=ckvn, qn=qn, rot=rot,
                 kv_full=kv_full, o_mla=o_mla, lse_mla=lse_mla, cat=cat, mixed=mixed, h1=h1, a2=a2, hid=hid, act=act, mo=mo)
    return h2, saved


def _layer_bwd(dh2, mod, w, small, rope, s, l, after_mlp=None, after_out=None):
    sh1, sc1, gt1, sh2, sc2, gt2 = mod
    rope_ret, rope_mla = rope
    t = dh2.shape[0]
    nm = lambda n: f"l{l}_{n}_bwd"
    gb, gs = {}, {}
    row = (D_MODEL, F32, "acc")
    dmo, dgt2 = _ew(lambda d, y, gt: (d * gt, jnp.sum(d * y, axis=0, keepdims=True)),
                    [(dh2, True), (s["mo"], True), (gt2, False)], [(D_MODEL, BF16, "tile"), row], ROWS, nm("res2"))
    dhid = _mm(dmo, w["w2"], "nt", 1024, 1024, 2048, name=nm("mlp2_x"), epi=lambda acc, x: (acc * 2.0 * jnp.maximum(x, 0.0),),
               epi_ins=[s["hid"]], epi_outs=[BF16])[0]
    gb["w2"] = _mm(s["act"], dmo, "tn", 1024, 1024, 4096, BF16, name=nm("mlp2_w"))
    da2 = _mm(dhid, w["w1_t"], "nn", 1024, 1024, 2048, name=nm("mlp1_x"))
    gb["w1_t"] = _mm(dhid, s["a2"], "tn", 1024, 1024, 4096, BF16, name=nm("mlp1_w"))
    if after_mlp is not None:
        gt1 = gt1 + after_mlp(gb)

    def norm_bwd(hh, g, sc, sh, da, dres):
        dh_, dg, dsc, dsh = _vjp_block(_norm_mod_fn, 4)(hh, g, sc, sh, da)
        return dh_ + dres, dg, dsc, dsh

    dh1, gs["norm2_g"], dsc2, dsh2 = _ew(norm_bwd, [(s["h1"], True), (small["norm2_g"], False), (sc2, False), (sh2, False), (da2, True), (dh2, True)],
                                         [(D_MODEL, F32, "tile"), row, row, row], ROWS, nm("norm2"))
    dmixed, dgt1 = _ew(lambda d, y, gt: (d * gt, jnp.sum(d * y, axis=0, keepdims=True)),
                       [(dh1, True), (s["mixed"], True), (gt1, False)], [(D_MODEL, BF16, "tile"), row], ROWS, nm("res1"))
    dcat = _mm(dmixed, w["w_out"], "nt", 512, 1024, 2048, name=nm("out_proj_x"))
    gb["w_out"] = _mm(s["cat"], dmixed, "tn", 1024, 1024, 4096, BF16, name=nm("out_proj_w"))
    if after_out is not None:
        small = {**small, "s5_glu_b": small["s5_glu_b"] + after_out(gb)}
    dy_s5, dy_ret, dy_swa, dy_mla = (dcat[:, i * GROUP_WIDTH:(i + 1) * GROUP_WIDTH] for i in range(4))
    b3, c3, dskip, a_r, a_i = small["s5"]
    gw = (GROUP_WIDTH, F32, "tile")
    gacc = (GROUP_WIDTH, F32, "acc")
    dz_a, dzz, gs["s5_glu_b"] = _ew(_vjp_block(_s5_glu_fn, 3), [(s["z"], True), (s["zz"], True), (small["s5_glu_b"], False), (dy_s5, True)],
                                    [gw, gw, gacc], ROWS, nm("s5_glu"))
    dz_b = _mm(dzz, w["glu_w"], "nt", 512, 512, 512, name=nm("s5_zz_x"))
    gb["glu_w"] = _mm(s["z"], dzz, "tn", 512, 512, 1024, BF16, name=nm("s5_zz_w"))

    def act_bwd(ypre, u, dsk, dza, dzb):
        return _vjp_block(_s5_act_fn, 3)(ypre, u, dsk, dza + dzb)

    dypre, du_a, g_dskip = _ew(act_bwd, [(s["ypre"], True), (s["p_s5"], True), (dskip, False), (dz_a, True), (dz_b, True)],
                               [gw, gw, gacc], ROWS, nm("s5_act"))
    ch, st = S5_PACK * S5_CH, S5_PACK * S5_STATE
    g_c3 =jnp.concatenate([_mm_blocks_tn(s["st_r"], dypre, st, ch, lambda j: j, nm("s5_y_w_re")),
                            _mm_blocks_tn(s["st_i"], dypre, st, ch, lambda j: j, nm("s5_y_w_im"))], axis=0)
    dbu_r, dbu_i, g_ar, g_ai = _s5_scan_bwd(dypre, c3, s["st_r"], s["st_i"], a_r, a_i, nm("s5_scan"))
    du_b = _mm_blocks(dbu_r, b3[:S5_BLOCKS], "nt", 512, pair=(dbu_i, b3[S5_BLOCKS:]), name=nm("s5_bu_x"))
    g_b3 = jnp.concatenate([_mm_blocks_tn(s["p_s5"], dbu_r, ch, st, lambda j: j, nm("s5_bu_w_re")),
                            _mm_blocks_tn(s["p_s5"], dbu_i, ch, st, lambda j: j, nm("s5_bu_w_im"))], axis=0)
    gs["s5"] = (g_b3, g_c3, g_dskip, g_ar, g_ai)
    dqk_rot, dk_rot, dv_ret, dg_ret = _ret_bwd(s["qk_ret"], s["p_ret"], s["o_ret"], dy_ret, small["ret_lgam"], nm("ret"))
    dqk = _rope(_merge_heads(jnp.concatenate([dqk_rot, dk_rot], axis=0)), rope_ret, nm("ret_rope"), inverse=True, out_dtype=BF16)
    dqkv_swa, gs["swa_sinks"] = _swa_bwd(s["qkv_swa"], s["o_swa"], s["lse_swa"], _split_heads(dy_swa, SWA_HEADS), small["swa_sinks"], nm("swa"))
    dqn, dqr, dkn, dv_mla, dkr_heads = _mla_attend_bwd(s["qn"], s["rot"], s["kv_full"], s["lse_mla"], s["o_mla"], dy_mla, nm("mla_att"))
    dkv_full = jnp.stack([dkn.reshape(t, MLA_HEADS, MLA_NOPE), dv_mla.reshape(t, MLA_HEADS, MLA_V)], axis=2).reshape(t, 2 * MLA_HEADS * MLA_NOPE)
    dkr_rot = _ew(lambda a, b, c, d: a + b + c + d, [(dkr_heads[i], True) for i in range(MLA_HEADS)], [(MLA_ROPE, F32, "tile")], ROWS, nm("mla_dkr"))[0]
    nq = MLA_HEADS * (MLA_NOPE + MLA_ROPE)
    dq_rot = jnp.concatenate([dqn.transpose(1, 0, 2), dqr.transpose(1, 0, 2)], axis=2).reshape(t, nq)
    droped = _rope(jnp.concatenate([dq_rot, dkr_rot, jnp.zeros_like(dkr_rot)], axis=1), rope_mla, nm("mla_rope"), inverse=True, out_dtype=BF16)
    dq_full, dkr = droped[:, :nq], droped[:, nq:nq + MLA_ROPE]
    dcqn = _mm(dq_full, w["w_uq_t"], "nn", 512, 384, 768, name=nm("mla_q_x"))
    gb["w_uq_t"] = _mm(dq_full, s["cqn"], "tn", 768, 384, 1024, BF16, name=nm("mla_q_w"))
    dckvn = _mm(dkv_full, w["w_ukv_t"], "nn", 512, 128, 1024, name=nm("mla_kv_x"))
    gb["w_ukv_t"] = _mm(dkv_full, s["ckvn"], "tn", 1024, 128, 1024, BF16, name=nm("mla_kv_w"))
    dcq, gs["mla_q_norm"] = _ew(_vjp_block(_rms_gain_fn, 2), [(s["cq"], True), (small["mla_q_norm"], False), (dcqn, True)],
                                [(MLA_Q_RANK, BF16, "tile"), (MLA_Q_RANK, F32, "acc")], ROWS, nm("mla_qnorm"))
    dckv, gs["mla_kv_norm"] = _ew(_vjp_block(_rms_gain_fn, 2), [(s["ckv"], True), (small["mla_kv_norm"], False), (dckvn, True)],
                                  [(MLA_KV_RANK, BF16, "tile"), (MLA_KV_RANK, F32, "acc")], ROWS, nm("mla_kvnorm"))
    du = _ew(lambda a, b: a + b, [(du_a, True), (du_b, True)], [(GROUP_WIDTH, BF16, "tile")], ROWS, nm("s5_du"))[0]
    bf = lambda a: a.astype(BF16)
    dproj = jnp.concatenate([du, bf(dqk), bf(dv_ret), bf(dg_ret), bf(_merge_heads(dqkv_swa)), bf(dcq), bf(dckv), bf(dkr)], axis=1)
    da1 = _mm(dproj, w["w_in_t"], "nn", 512, 1024, N_IN, name=nm("proj_x"))
    gb["w_in_t"] = _mm(dproj, s["a1"], "tn", N_IN, 512, 2048, BF16, name=nm("proj_w"))
    dh, gs["norm1_g"], dsc1, dsh1 = _ew(norm_bwd, [(s["h"], True), (small["norm1_g"], False), (sc1, False), (sh1, False), (da1, True), (dh1, True)],
                                        [(D_MODEL, F32, "tile"), row, row, row], ROWS, nm("norm1"))
    dmod = jnp.concatenate([dsh1, dsc1, dgt1, dsh2, dsc2, dgt2], axis=1)
    return dh, gb, gs, dmod


BIG = ("w_in_t", "w1_t", "w_uq_t", "w_ukv_t", "w_out", "w2", "glu_w")
MLP_BIG = ("w1_t", "w2")
OUT_BIG = ("w_out",)
LATE_BIG = ("w_out", "w1_t", "w2")
S5_NAMES = ("s5_lambda_re", "s5_lambda_im", "s5_log_dt", "s5_b_re", "s5_b_im", "s5_c_re", "s5_c_im", "s5_d")


def kernel(x, c, norm1_g, norm2_g, ada_w, ada_b, w_in, s5_lambda_re, s5_lambda_im, s5_log_dt, s5_b_re, s5_b_im, s5_c_re, s5_c_im, s5_d, s5_glu_w, s5_glu_b, swa_sinks, mla_q_norm, mla_kv_norm, mla_w_uq, mla_w_ukv, w_out, mlp_w1, mlp_w2, final_norm_g, loss_target, m_norm1_g, m_norm2_g, m_ada_w, m_ada_b, m_w_in, m_s5_lambda_re, m_s5_lambda_im, m_s5_log_dt, m_s5_b_re, m_s5_b_im, m_s5_c_re, m_s5_c_im, m_s5_d, m_s5_glu_w, m_s5_glu_b, m_swa_sinks, m_mla_q_norm, m_mla_kv_norm, m_mla_w_uq, m_mla_w_ukv, m_w_out, m_mlp_w1, m_mlp_w2, m_final_norm_g, v_norm1_g, v_norm2_g, v_ada_w, v_ada_b, v_w_in, v_s5_lambda_re, v_s5_lambda_im, v_s5_log_dt, v_s5_b_re, v_s5_b_im, v_s5_c_re, v_s5_c_im, v_s5_d, v_s5_glu_w, v_s5_glu_b, v_swa_sinks, v_mla_q_norm, v_mla_kv_norm, v_mla_w_uq, v_mla_w_ukv, v_w_out, v_mlp_w1, v_mlp_w2, v_final_norm_g):
    names = ["norm1_g", "norm2_g", "ada_w", "ada_b", "w_in", "s5_lambda_re", "s5_lambda_im", "s5_log_dt", "s5_b_re", "s5_b_im",
             "s5_c_re", "s5_c_im", "s5_d", "s5_glu_w", "s5_glu_b", "swa_sinks", "mla_q_norm", "mla_kv_norm", "mla_w_uq",
             "mla_w_ukv", "w_out", "mlp_w1", "mlp_w2", "final_norm_g"]
    env = locals()
    wts = {n: env[n] for n in names}
    mom = {n: env["m_" + n] for n in names}
    var = {n: env["v_" + n] for n in names}
    t = x.shape[1]
    me = 4 * lax.axis_index("x") + 2 * lax.axis_index("y") + lax.axis_index("c")
    ret_lgam = jnp.log1p(-(2.0 ** (-5.0 - jnp.arange(RET_HEADS, dtype=F32))))

    tr = lambda a: a.transpose(0, 2, 1)
    shard = {"w_in_t": tr(w_in), "w1_t": tr(mlp_w1), "w_uq_t": tr(mla_w_uq), "w_ukv_t": tr(mla_w_ukv),
             "w_out": w_out, "w2": mlp_w2, "glu_w": s5_glu_w}
    to_send = [{k: shard[k][l].astype(BF16) for k in BIG} for l in range(DEPTH)]
    as_rows = lambda keys, arrs: {k: a.reshape(-1, shard[k].shape[2]) for k, a in zip(keys, arrs)}
    first = [k for k in BIG if k not in LATE_BIG]
    own_slot = lambda k: 4 * lax.axis_index("x") + 2 * lax.axis_index("y") + lax.axis_index("c")

    def gather_start(arrs, tag, after=None):
        return _push_start(arrs, [jax.ShapeDtypeStruct((N_DEV,) + a.shape, a.dtype) for a in arrs], GATHER_PEERS,
                           lambda ref, k: ref, own_slot, f"gather_weights_{tag}_start", after)

    def gather_finish(started, after, tag):
        sent, landed = _push_wait(started[0], started[1], started[2], started[3], after, GATHER_PEERS, lambda ref, k: ref, own_slot,
                                  f"gather_weights_{tag}_wait")
        with_own = [lax.dynamic_update_index_in_dim(full, own, me, 0) for full, own in zip(landed, sent)]
        return _pass_to_sibling(with_own, f"gather_weights_{tag}_pass")

    gather_first = gather_start([to_send[0][k] for k in first] + [c], "first")
    zero = gather_first[4][0, 0]
    rope = _rope_tables(t, zero)
    smalls, s5_pulls = [], []
    for l in range(DEPTH):
        s5_ops, pull = jax.vjp(_s5_prep, *[wts[n][l] + zero if n == "s5_log_dt" else wts[n][l] for n in S5_NAMES])
        s5_pulls.append(pull)
        smalls.append(dict(norm1_g=norm1_g[l][None], norm2_g=norm2_g[l][None], s5=s5_ops, s5_glu_b=s5_glu_b[l][None],
                           swa_sinks=swa_sinks[l], mla_q_norm=mla_q_norm[l][None], mla_kv_norm=mla_kv_norm[l][None], ret_lgam=ret_lgam))
    setup_done = (smalls[0]["s5"][3][0, 0] + smalls[1]["s5"][3][0, 0] + rope[0][0][0, 0] + rope[1][0][0, 0]).reshape(1, 1)
    gathered = gather_finish(gather_first, setup_done, "first")
    c_all = gathered[-1].reshape(N_DEV, D_MODEL)
    big = [as_rows(first, gathered[:len(first)]), None]

    c_act = _whole(lambda v: v * jax.nn.sigmoid(v), [c_all], [((N_DEV, D_MODEL), F32)], "cond_silu")[0]
    c_pad = jnp.concatenate([c_act, jnp.zeros((128 - N_DEV, D_MODEL), F32)], axis=0)
    cols = ada_w.shape[2]
    mod_part = [_mm(c_pad, ada_w[l], "nn", 128, cols, 512, name=f"l{l}_mod")[:N_DEV] for l in range(DEPTH)]
    mod_all = _all_gather([jnp.stack(mod_part)], "gather_mod")[0]
    gather0 = gather_start([to_send[0][k] for k in LATE_BIG], "l0", after=mod_all)
    gather1 = gather_start([to_send[1][k] for k in BIG], "l1", after=mod_all)
    mod_rows = lax.dynamic_index_in_dim(mod_all, me, axis=2, keepdims=False)
    mods = []
    for l in range(DEPTH):
        row = mod_rows[:, l].reshape(1, 6 * D_MODEL) + ada_b[l][None]
        if l == 0:
            row = row + (gather0[4][0, 0] + gather1[4][0, 0])
        mods.append([row[:, i * D_MODEL:(i + 1) * D_MODEL] for i in range(6)])

    h = x[0]
    saved = []
    for l in range(DEPTH):
        if l == 0:
            late = lambda after: as_rows(LATE_BIG, gather_finish(gather0, after, "l0"))
        else:
            big[1] = as_rows(BIG, gather_finish(gather1, h, "l1"))
            late = None
        h, s = _layer_fwd(h, mods[l], big[l], smalls[l], rope, l, late)
        big[l] = s.pop("w")
        saved.append(s)

    fg = final_norm_g[None]
    tgt = loss_target[0]
    loss_local = _ew(_final_fn, [(h, True), (fg, False), (tgt, True)], [(1, F32, "acc")], ROWS, "loss")[0]

    def final_bwd(hh, g, tg):
        dh_, dg, _ = _vjp_block(_final_fn, 3)(hh, g, tg, jnp.ones((1, 1), F32))
        return dh_, dg

    dh, g_final = _ew(final_bwd, [(h, True), (fg, False), (tgt, True)], [(D_MODEL, F32, "tile"), (D_MODEL, F32, "acc")], ROWS, "loss_bwd")
    loss = lax.psum(loss_local[0, 0], ("x", "y", "c"))

    core, chip = lax.axis_index("c"), 2 * lax.axis_index("x") + lax.axis_index("y")

    def core_stage(g_layer, keys, tag):
        g_list = [g_layer[k].reshape(N_DEV, -1, g_layer[k].shape[1]) for k in keys]
        got = _scatter_core(g_list, f"scatter_core_{tag}")
        return [_pick_sum(g, o, core, lambda q, c_: 2 * q + c_, BF16, f"{tag}_core_sum_{k}") for k, g, o in zip(keys, g_list, got)]

    def their_block(ref, k):
        x_, y_ = lax.axis_index("x"), lax.axis_index("y")
        dx, dy, _ = CHIP_PEERS[k]
        return ref.at[2 * (1 - x_ if dx else x_) + (1 - y_ if dy else y_)]

    def chips_start(halves, tag, after=None):
        return _push_start(halves, [jax.ShapeDtypeStruct((3,) + a.shape[1:], a.dtype) for a in halves], CHIP_PEERS,
                           their_block, lambda k: k, f"scatter_chips_{tag}_start", after)

    def chips_wait(started, after, tag):
        return _push_wait(started[0], started[1], started[2], started[3], after, CHIP_PEERS, their_block, lambda k: k, f"scatter_chips_{tag}_wait")

    def sibling_block(ref, q):
        return ref.at[2 * q + 1 - lax.axis_index("c")]

    g_small, dmods = [None] * DEPTH, [None] * DEPTH
    dh, g_big1, g_small[1], dmods[1] = _layer_bwd(dh, mods[1], big[1], smalls[1], rope, saved[1], 1)
    g_list1 = [g_big1[k].reshape(N_DEV, -1, g_big1[k].shape[1]) for k in BIG]
    core1 = _push_start(g_list1, [jax.ShapeDtypeStruct((4,) + g.shape[1:], g.dtype) for g in g_list1], CORE_PEERS,
                        sibling_block, lambda q: q, "scatter_core_l1_start")
    mods0 = [m + core1[4][0, 0] for m in mods[0]]
    early = {}

    def after_mlp(gb):
        g_mine, got = _push_wait(core1[0], core1[1], core1[2], core1[3], gb["w1_t"], CORE_PEERS, sibling_block, lambda q: q,
                                 "scatter_core_l1_wait")
        halves = [_pick_sum(g, o, core, lambda q, c_: 2 * q + c_, BF16, f"l1_core_sum_{k}") for k, g, o in zip(BIG, g_mine, got)]
        early["l1"] = chips_start(halves, "l1")
        g_list = [gb[k].reshape(N_DEV, -1, gb[k].shape[1]) for k in MLP_BIG]
        early["mlp_core"] = _push_start(g_list, [jax.ShapeDtypeStruct((4,) + g.shape[1:], g.dtype) for g in g_list], CORE_PEERS,
                                        sibling_block, lambda q: q, "scatter_core_l0_mlp_start")
        return early["l1"][4][0, 0] + early["mlp_core"][4][0, 0]

    def after_out(gb):
        started = early["mlp_core"]
        g_mine, got = _push_wait(started[0], started[1], started[2], started[3], gb["w_out"], CORE_PEERS, sibling_block, lambda q: q,
                                 "scatter_core_l0_mlp_wait")
        halves = [_pick_sum(g, o, core, lambda q, c_: 2 * q + c_, BF16, f"l0_mlp_core_sum_{k}") for k, g, o in zip(MLP_BIG, g_mine, got)]
        early["mlp"] = chips_start(halves, "l0_mlp")
        early["out"] = chips_start(core_stage(gb, OUT_BIG, "l0_out"), "l0_out")
        return early["mlp"][4][0, 0] + early["out"][4][0, 0]

    dh, g_big0, g_small[0], dmods[0] = _layer_bwd(dh, mods0, big[0], smalls[0], rope, saved[0], 0, after_mlp=after_mlp, after_out=after_out)
    grad_x = dh[None]
    rest = [k for k in BIG if k not in MLP_BIG + OUT_BIG]
    halves_rest = core_stage(g_big0, rest, "l0_rest")

    small_parts = []
    for l in range(DEPTH):
        gs = g_small[l]
        s5g = s5_pulls[l](gs["s5"])
        small_parts += [gs["norm1_g"], gs["norm2_g"], *s5g, gs["s5_glu_b"], gs["swa_sinks"], gs["mla_q_norm"], gs["mla_kv_norm"]]
    small_parts += [g_final, *dmods]
    sizes = [int(np.prod(p.shape)) for p in small_parts]
    flat = jnp.concatenate([p.reshape(1, -1) for p in small_parts], axis=1)
    pad = (-flat.shape[1]) % 8192
    flat = jnp.pad(flat, ((0, 0), (0, pad)))
    flat_all = _all_gather([flat], "gather_small_grads")[0].reshape(N_DEV, -1)
    summed = _sum8(flat_all, "sum_small_grads")
    chips_rest = chips_start(halves_rest, "l0_rest", after=summed)
    rest_token = chips_rest[4]
    halves1, landed1 = chips_wait(early["l1"], rest_token, "l1")
    halves_mlp, landed_mlp = chips_wait(early["mlp"], rest_token, "l0_mlp")
    halves_out, landed_out = chips_wait(early["out"], rest_token, "l0_out")
    terms = {(1, k): pair for k, pair in zip(BIG, zip(halves1, landed1))}
    terms.update({(0, k): pair for k, pair in zip(MLP_BIG, zip(halves_mlp, landed_mlp))})
    terms.update({(0, k): pair for k, pair in zip(OUT_BIG, zip(halves_out, landed_out))})
    chip_sum = lambda l, k: _pick_sum(*terms[l, k], chip, lambda q, m_: m_, F32, f"l{l}_chip_sum_{k}")
    pieces, off = [], 0
    for sz in sizes:
        pieces.append(summed[0, off:off + sz])
        off += sz
    small_names = ["norm1_g", "norm2_g", *S5_NAMES, "s5_glu_b", "swa_sinks", "mla_q_norm", "mla_kv_norm"]
    per_layer = len(small_names)
    grads = {}
    for i, n in enumerate(small_names):
        grads[n] = jnp.stack([pieces[l * per_layer + i].reshape(wts[n].shape[1:]) for l in range(DEPTH)])
    grads["final_norm_g"] = pieces[DEPTH * per_layer]
    grads["ada_b"] = jnp.stack([pieces[DEPTH * per_layer + 1 + l] for l in range(DEPTH)])

    mod_off = sum(sizes[:DEPTH * per_layer + 1])
    dmod_all = flat_all[:, mod_off:mod_off + DEPTH * 6 * D_MODEL].reshape(N_DEV, DEPTH, N_DEV, cols)
    dmod_mine = lax.dynamic_index_in_dim(dmod_all, me, axis=2, keepdims=False).transpose(1, 0, 2)
    dmod_pad = jnp.concatenate([dmod_mine, jnp.zeros((DEPTH, 128 - N_DEV, cols), F32)], axis=1)
    grads["ada_w"] = jnp.stack([_mm(c_pad, dmod_pad[l], "tn", 512, cols, 128, name=f"l{l}_ada_w_grad") for l in range(DEPTH)])

    out_g, out_d, out_m, out_v = dict(grads), {}, {}, {}
    orig = {"w_in_t": "w_in", "w1_t": "mlp_w1", "w_uq_t": "mla_w_uq", "w_ukv_t": "mla_w_ukv", "w_out": "w_out", "w2": "mlp_w2", "glu_w": "s5_glu_w"}

    def update_big(keys):
        for k in keys:
            n = orig[k]
            out_g[n] = jnp.stack([chip_sum(l, k) for l in range(DEPTH)])
            if k.endswith("_t"):
                out_g[n] = tr(out_g[n])
            out_d[n], out_m[n], out_v[n] = _adamw_shard(wts[n], out_g[n], mom[n], var[n], f"adamw_{n}")

    update_big(MLP_BIG + OUT_BIG)
    out_d["ada_w"], out_m["ada_w"], out_v["ada_w"] = _adamw_shard(wts["ada_w"], out_g["ada_w"], mom["ada_w"], var["ada_w"], "adamw_ada_w")
    small_all = small_names + ["ada_b", "final_norm_g"]
    ds, ms, vs = _adamw_small([wts[n] for n in small_all], [grads[n] for n in small_all], [mom[n] for n in small_all],
                              [var[n] for n in small_all], "adamw_small")
    for n, d, m_, v_ in zip(small_all, ds, ms, vs):
        out_d[n], out_m[n], out_v[n] = d, m_, v_
    halves_rest, landed_rest = chips_wait(chips_rest, out_d["ada_w"], "l0_rest")
    terms.update({(0, k): pair for k, pair in zip(rest, zip(halves_rest, landed_rest))})
    update_big(rest)
    return (loss, grad_x, *[out_g[n] for n in names], *[out_d[n] for n in names], *[out_m[n] for n in names], *[out_v[n] for n in names])
```

```python
import functools
import math

import numpy as np
import jax
import jax.numpy as jnp
from jax import lax
from jax.experimental import pallas as pl
from jax.experimental.pallas import tpu as pltpu

F32 = jnp.float32
BF16 = jnp.bfloat16
_MXU_DTYPE = jnp.bfloat16

N_DEV = 8
D_MODEL = 2048
DEPTH = 2
GROUP_WIDTH = 512
D_FF = 8192
S5_CH, S5_GROUPS, S5_STATE = 16, 32, 64
S5_WIDTH = S5_GROUPS * S5_STATE
S5_PACK = 8
S5_BLOCKS = S5_GROUPS // S5_PACK
RET_HEADS, RET_QK, RET_V, RET_CHUNK = 4, 64, 128, 128
SWA_HD, SWA_HEADS, SWA_KV_HEADS, WINDOW = 64, 8, 2, 128
MLA_HEADS, MLA_Q_RANK, MLA_KV_RANK, MLA_NOPE, MLA_ROPE, MLA_V = 4, 384, 128, 128, 64, 128
ROPE_BASE = 10000.0
EPS = 1e-6
NEG = -1e30
N_IN = 3392
ADAM_LR, ADAM_B1, ADAM_B2, ADAM_EPS, ADAM_WD, ADAM_STEP = 0.001, 0.9, 0.999, 1e-08, 0.01, 10

VMEM_LIMIT_BYTES = 52 * 1024 * 1024
MESH_ID = pl.DeviceIdType.MESH
_ANY = pl.BlockSpec(memory_space=pl.ANY)
_SMEM = pl.BlockSpec(memory_space=pltpu.SMEM)


def _params(sem):
    return pltpu.CompilerParams(dimension_semantics=sem, vmem_limit_bytes=VMEM_LIMIT_BYTES)


_DIMS = {"nn": (((1,), (0,)), ((), ())), "nt": (((1,), (1,)), ((), ())), "tn": (((0,), (0,)), ((), ()))}


def _dot(a, b, mode="nn"):
    return lax.dot_general(a.astype(_MXU_DTYPE), b.astype(_MXU_DTYPE), _DIMS[mode], preferred_element_type=F32)


def _mm(a, b, mode, tm, tn, tk, out_dtype=F32, name="mm", b_off=0, n=None, pair=None, epi=None, epi_ins=(), epi_outs=None):
    if mode == "tn":
        kdim, m = a.shape
    else:
        m, kdim = a.shape
    if n is None:
        n = b.shape[0] if mode == "nt" else b.shape[1]
    tm, tn, tk = min(tm, m), min(tn, n), min(tk, kdim)
    assert m % tm == 0 and n % tn == 0 and kdim % tk == 0, (name, a.shape, b.shape, tm, tn, tk)
    nk = kdim // tk
    a_spec = pl.BlockSpec((tk, tm), lambda i, j, k: (k, i)) if mode == "tn" else pl.BlockSpec((tm, tk), lambda i, j, k: (i, k))
    if mode == "nt":
        b_spec = pl.BlockSpec((tn, tk), lambda i, j, k: (j + b_off, k))
    else:
        b_spec = pl.BlockSpec((tk, tn), lambda i, j, k: (k, j + b_off))
    o_spec = pl.BlockSpec((tm, tn), lambda i, j, k: (i, j))
    n_mm = 2 if pair is None else 4
    out_dtypes = [out_dtype] if epi is None else list(epi_outs)

    def body(*refs):
        ins, extra = refs[:n_mm], refs[n_mm:n_mm + len(epi_ins)]
        outs = refs[n_mm + len(epi_ins):n_mm + len(epi_ins) + len(out_dtypes)]
        part = _dot(ins[0][...], ins[1][...], mode)
        if pair is not None:
            part = part + _dot(ins[2][...], ins[3][...], mode)

        def finish(acc):
            vals = (acc,) if epi is None else epi(acc, *[r[...] for r in extra])
            for o_ref, v, dt in zip(outs, vals, out_dtypes):
                o_ref[...] = v.astype(dt)

        if nk == 1:
            finish(part)
        else:
            acc_ref = refs[-1]
            k = pl.program_id(2)

            @pl.when(k == 0)
            def _():
                acc_ref[...] = part

            @pl.when(k > 0)
            def _():
                acc_ref[...] += part

            @pl.when(k == nk - 1)
            def _():
                finish(acc_ref[...])

    operands = [a, b] + ([] if pair is None else list(pair)) + list(epi_ins)
    res = pl.pallas_call(
        body, name=name, grid=(m // tm, n // tn, nk),
        in_specs=[a_spec, b_spec] * (n_mm // 2)
        + [pl.BlockSpec((1, tn), lambda i, j, k: (0, j)) if e.shape[0] == 1 else o_spec for e in epi_ins],
        out_specs=[o_spec] * len(out_dtypes), out_shape=[jax.ShapeDtypeStruct((m, n), dt) for dt in out_dtypes],
        scratch_shapes=[] if nk == 1 else [pltpu.VMEM((tm, tn), F32)],
        compiler_params=_params(("parallel", "parallel", "arbitrary")),
    )(*operands)
    return res[0] if epi is None else res


def _mm_blocks(a, b, mode, tm, a_of=None, pair=None, name="mm_blocks"):
    a_of = a_of or (lambda j: j)
    m = a.shape[0]
    nj, kb, nb = b.shape
    a_w, o_w = (kb, nb) if mode == "nn" else (nb, kb)
    tm = min(tm, m)
    a_spec = pl.BlockSpec((tm, a_w), lambda i, j: (i, a_of(j)))
    b_spec = pl.BlockSpec((1, kb, nb), lambda i, j: (j, 0, 0))
    n_in = 2 if pair is None else 4

    def body(*refs):
        acc = _dot(refs[0][...], refs[1][0], mode)
        if pair is not None:
            acc = acc + _dot(refs[2][...], refs[3][0], mode)
        refs[n_in][...] = acc

    operands = [a, b] + ([] if pair is None else list(pair))
    return pl.pallas_call(
        body, name=name, grid=(m // tm, nj), in_specs=[a_spec, b_spec] * (n_in // 2),
        out_specs=pl.BlockSpec((tm, o_w), lambda i, j: (i, j)), out_shape=jax.ShapeDtypeStruct((m, nj * o_w), F32),
        compiler_params=_params(("parallel", "parallel")),
    )(*operands)


def _mm_blocks_tn(a, b, x, y, b_of, name):
    kdim = a.shape[0]
    nj = a.shape[1] // x

    def body(a_ref, b_ref, o_ref):
        o_ref[0] = _dot(a_ref[...], b_ref[...], "tn")

    return pl.pallas_call(
        body, name=name, grid=(nj,), in_specs=[pl.BlockSpec((kdim, x), lambda j: (0, j)), pl.BlockSpec((kdim, y), lambda j: (0, b_of(j)))],
        out_specs=pl.BlockSpec((1, x, y), lambda j: (j, 0, 0)), out_shape=jax.ShapeDtypeStruct((nj, x, y), F32),
        compiler_params=_params(("parallel",)),
    )(a, b)


SUBLANES = 8


def _row_tile(rows, target, mult=SUBLANES):
    best = None
    for cand in range(mult, min(rows, target) + 1, mult):
        if rows % cand == 0:
            best = cand
    return best or rows


def _ew(fn, ins, outs, tt, name):
    t = [a.shape[0] for a, tiled in ins if tiled][0]
    tt = _row_tile(t, tt)
    n_in = len(ins)
    in_specs = [pl.BlockSpec((tt, a.shape[1]), lambda i: (i, 0)) if tiled else pl.BlockSpec(a.shape, lambda i: (0, 0))
                for a, tiled in ins]
    out_specs, out_shapes = [], []
    for w, dt, kind in outs:
        if kind == "tile":
            out_specs.append(pl.BlockSpec((tt, w), lambda i: (i, 0)))
            out_shapes.append(jax.ShapeDtypeStruct((t, w), dt))
        else:
            out_specs.append(pl.BlockSpec((1, w), lambda i: (0, 0)))
            out_shapes.append(jax.ShapeDtypeStruct((1, w), F32))
    has_acc = any(kind == "acc" for _, _, kind in outs)

    def body(*refs):
        vals = fn(*[r[...] for r in refs[:n_in]])
        if not isinstance(vals, (tuple, list)):
            vals = (vals,)
        i = pl.program_id(0)
        for o_ref, v, (w, dt, kind) in zip(refs[n_in:], vals, outs):
            if kind == "tile":
                o_ref[...] = v.astype(dt)
            else:
                @pl.when(i == 0)
                def _(o_ref=o_ref, v=v):
                    o_ref[...] = v.astype(F32)

                @pl.when(i > 0)
                def _(o_ref=o_ref, v=v):
                    o_ref[...] += v.astype(F32)

    res = pl.pallas_call(
        body, name=name, grid=(t // tt,), in_specs=in_specs, out_specs=out_specs, out_shape=out_shapes,
        compiler_params=_params(("arbitrary" if has_acc else "parallel",)),
    )(*[a for a, _ in ins])
    return res


def _whole(fn, ins, outs, name):
    def body(*refs):
        vals = fn(*[r[...] for r in refs[:len(ins)]])
        if not isinstance(vals, (tuple, list)):
            vals = (vals,)
        for o_ref, v in zip(refs[len(ins):], vals):
            o_ref[...] = v.astype(o_ref.dtype)

    return pl.pallas_call(body, name=name, out_shape=[jax.ShapeDtypeStruct(s, dt) for s, dt in outs])(*ins)


def _rms(x):
    return x * lax.rsqrt(jnp.mean(x * x, axis=-1, keepdims=True) + EPS)


def _norm_mod_fn(h, g, sc, sh):
    return (_rms(h) * g) * (1.0 + sc) + sh


def _rms_gain_fn(x, g):
    return _rms(x) * g


def _gate_add_fn(h, y, gt):
    return h + gt * y


def _relu2_fn(x):
    return jnp.square(jnp.maximum(x, 0.0))


def _s5_act_fn(ypre, u, dskip):
    return jax.nn.gelu(ypre + dskip * u)


def _s5_glu_fn(z, zz, b):
    return z * jax.nn.sigmoid(zz + b)


def _ret_gate_fn(o, g):
    return _rms(o) * (g * jax.nn.sigmoid(g))


def _final_fn(h, g, tgt):
    err = _rms(h) * g - tgt
    return 0.5 * jnp.sum(jnp.mean(err * err, axis=-1, keepdims=True), axis=0, keepdims=True)


def _vjp_block(fn, n_args):
    def bwd(*vals):
        _, pull = jax.vjp(fn, *vals[:n_args])
        return pull(vals[n_args])
    return bwd


def _rope_tables(t, zero=0.0):
    d = RET_QK
    inv = ROPE_BASE ** (-jnp.arange(0, d, 2, dtype=F32) / d)
    ang = (jnp.arange(t, dtype=F32) + zero)[:, None] * inv[None, :]
    cos, sin = jnp.cos(ang), jnp.sin(ang)
    cos2, sin2 = jnp.concatenate([cos, cos], -1), jnp.concatenate([-sin, sin], -1)
    ret = (jnp.tile(cos2, (1, 8)), jnp.tile(sin2, (1, 8)))
    one, zero = jnp.ones((t, MLA_NOPE), F32), jnp.zeros((t, MLA_NOPE), F32)
    mla_c = jnp.concatenate([jnp.tile(jnp.concatenate([one, cos2], -1), (1, MLA_HEADS)), cos2, one[:, :d]], -1)
    mla_s = jnp.concatenate([jnp.tile(jnp.concatenate([zero, sin2], -1), (1, MLA_HEADS)), sin2, zero[:, :d]], -1)
    return ret, (mla_c, mla_s)


def _rope_fn(x, c, s, sign):
    w = x.shape[1]
    lane = lax.broadcasted_iota(jnp.int32, x.shape, 1)
    swapped = jnp.where((lane & 63) < 32, pltpu.roll(x, w - 32, 1), pltpu.roll(x, 32, 1))
    return x * c + swapped * (sign * s)


def _rope(x, tables, name, inverse=False, out_dtype=F32):
    c, s = tables
    fn = functools.partial(_rope_fn, sign=-1.0 if inverse else 1.0)
    return _ew(fn, [(x, True), (c, True), (s, True)], [(x.shape[1], out_dtype, "tile")], ROWS, name)[0]


SCAN_ROWS, SCAN_LANES = 256, 512


def _cmul(ar, ai, br, bi):
    return ar * br - ai * bi, ar * bi + ai * br


def _group_powers(ar, ai, reverse):
    shape = (SUBLANES, ar.shape[1])
    row = lax.broadcasted_iota(jnp.int32, shape, 0)
    pr, pi = ar, ai
    out_r, out_i = jnp.zeros(shape, F32), jnp.zeros(shape, F32)
    for e in range(1, SUBLANES + 1):
        hit = row == (SUBLANES - e if reverse else e - 1)
        out_r, out_i = jnp.where(hit, pr, out_r), jnp.where(hit, pi, out_i)
        if e < SUBLANES:
            pr, pi = _cmul(pr, pi, ar, ai)
    return out_r, out_i


def _scan_chunk(in_r_ref, in_i_ref, out_r_ref, out_i_ref, ar, ai, cr, ci, reverse, visit=None):
    rows, lanes = in_r_ref.shape
    sub = lax.broadcasted_iota(jnp.int32, (SUBLANES, lanes), 0)
    edge_r, edge_i = _group_powers(ar, ai, reverse)
    steps, pr, pi, k = [], ar, ai, 1
    while k < SUBLANES:
        steps.append((k, pr, pi))
        pr, pi = _cmul(pr, pi, pr, pi)
        k *= 2
    groups = range(rows // SUBLANES)
    for g in (reversed(groups) if reverse else groups):
        sl = slice(g * SUBLANES, (g + 1) * SUBLANES)
        xr, xi = in_r_ref[sl, :], in_i_ref[sl, :]
        for k, pr, pi in steps:
            shift = SUBLANES - k if reverse else k
            keep = sub < SUBLANES - k if reverse else sub >= k
            tr, ti = _cmul(pr, pi, pltpu.roll(xr, shift, 0), pltpu.roll(xi, shift, 0))
            xr, xi = xr + jnp.where(keep, tr, 0.0), xi + jnp.where(keep, ti, 0.0)
        tr, ti = _cmul(edge_r, edge_i, cr, ci)
        xr, xi = xr + tr, xi + ti
        out_r_ref[sl, :] = xr
        out_i_ref[sl, :] = xi
        if visit is not None:
            visit(sl, xr, xi, cr, ci)
        edge = slice(0, 1) if reverse else slice(SUBLANES - 1, SUBLANES)
        cr, ci = xr[edge, :], xi[edge, :]
    return cr, ci


def _s5_scan_specs(rows, row_block):
    assert SCAN_LANES == S5_PACK * S5_STATE
    chan = pl.BlockSpec((rows, S5_PACK * S5_CH), lambda j, i: (row_block(i), j))
    op = lambda off, shape: pl.BlockSpec((1,) + shape, lambda j, i: (j + off, 0, 0))
    blk = pl.BlockSpec((rows, SCAN_LANES), lambda j, i: (row_block(i), j))
    par = pl.BlockSpec((1, SCAN_LANES), lambda j, i: (0, j))
    return chan, op, blk, par


def _s5_scan_fwd(u, b3, a_r, a_i, name):
    t = u.shape[0]
    rows = min(SCAN_ROWS, t)
    chan, op, blk, par = _s5_scan_specs(rows, lambda i: i)

    def body(u_ref, b_re_ref, b_im_ref, ar_ref, ai_ref, or_ref, oi_ref, cr_ref, ci_ref, sr_ref, si_ref):
        i = pl.program_id(1)

        @pl.when(i == 0)
        def _():
            cr_ref[...] = jnp.zeros_like(cr_ref)
            ci_ref[...] = jnp.zeros_like(ci_ref)

        sr_ref[...] = _dot(u_ref[...], b_re_ref[0])
        si_ref[...] = _dot(u_ref[...], b_im_ref[0])
        ar, ai = ar_ref[...], ai_ref[...]
        cr, ci = _scan_chunk(sr_ref, si_ref, or_ref, oi_ref, ar, ai, cr_ref[...], ci_ref[...], reverse=False)
        cr_ref[...] = cr
        ci_ref[...] = ci

    st_r, st_i = pl.pallas_call(
        body, name=name, grid=(S5_BLOCKS, t // rows),
        in_specs=[chan, op(0, b3.shape[1:]), op(S5_BLOCKS, b3.shape[1:]), par, par], out_specs=[blk, blk],
        out_shape=[jax.ShapeDtypeStruct((t, S5_WIDTH), F32)] * 2,
        scratch_shapes=[pltpu.VMEM((1, SCAN_LANES), F32)] * 2 + [pltpu.VMEM((rows, SCAN_LANES), F32)] * 2,
        compiler_params=_params(("parallel", "arbitrary")),
    )(u, b3, b3, a_r, a_i)
    return st_r, st_i


def _s5_scan_bwd(dy, c3, st_r, st_i, a_r, a_i, name):
    t = dy.shape[0]
    rows = min(SCAN_ROWS, t)
    nc = t // rows
    chan, op, blk, par = _s5_scan_specs(rows, lambda i: nc - 1 - i)

    def body(dy_ref, c_re_ref, c_im_ref, xr_ref, xi_ref, ar_ref, ai_ref, gr_ref, gi_ref, dar_ref, dai_ref, cr_ref, ci_ref, dr_ref, di_ref):
        i = pl.program_id(1)
        dr_ref[...] = _dot(dy_ref[...], c_re_ref[0], "nt")
        di_ref[...] = _dot(dy_ref[...], c_im_ref[0], "nt")

        @pl.when(i == 0)
        def _():
            cr_ref[...] = jnp.zeros_like(cr_ref)
            ci_ref[...] = jnp.zeros_like(ci_ref)
            dar_ref[...] = jnp.zeros_like(dar_ref)
            dai_ref[...] = jnp.zeros_like(dai_ref)

        ar, ai = ar_ref[...], ai_ref[...]
        cr, ci = cr_ref[...], ci_ref[...]
        last = lax.broadcasted_iota(jnp.int32, (SUBLANES, SCAN_LANES), 0) == SUBLANES - 1
        sums = [jnp.zeros((SUBLANES, SCAN_LANES), F32), jnp.zeros((SUBLANES, SCAN_LANES), F32)]

        def visit(sl, gr, gi, next_r, next_i):
            nr = jnp.where(last, next_r, pltpu.roll(gr, SUBLANES - 1, 0))
            ni = jnp.where(last, next_i, pltpu.roll(gi, SUBLANES - 1, 0))
            xr, xi = xr_ref[sl, :], xi_ref[sl, :]
            sums[0] = sums[0] + (nr * xr + ni * xi)
            sums[1] = sums[1] + (ni * xr - nr * xi)

        first_r, first_i = _scan_chunk(dr_ref, di_ref, gr_ref, gi_ref, ar, -ai, cr, ci, reverse=True, visit=visit)
        dar_ref[...] += jnp.sum(sums[0], axis=0, keepdims=True)
        dai_ref[...] += jnp.sum(sums[1], axis=0, keepdims=True)
        cr_ref[...] = first_r
        ci_ref[...] = first_i

    return pl.pallas_call(
        body, name=name, grid=(S5_BLOCKS, nc),
        in_specs=[chan, op(0, c3.shape[1:]), op(S5_BLOCKS, c3.shape[1:]), blk, blk, par, par], out_specs=[blk, blk, par, par],
        out_shape=[jax.ShapeDtypeStruct((t, S5_WIDTH), F32)] * 2 + [jax.ShapeDtypeStruct((1, S5_WIDTH), F32)] * 2,
        scratch_shapes=[pltpu.VMEM((1, SCAN_LANES), F32)] * 2 + [pltpu.VMEM((rows, SCAN_LANES), F32)] * 2,
        compiler_params=_params(("parallel", "arbitrary")),
    )(dy, c3, c3, st_r, st_i, a_r, a_i)


def _s5_prep(lam_re, lam_im, log_dt, b_re, b_im, c_re, c_im, d_skip):
    dt = jnp.exp(log_dt)[:, None]
    mag = jnp.exp(lam_re * dt)
    ar, ai = mag * jnp.cos(lam_im * dt), mag * jnp.sin(lam_im * dt)
    den = lam_re * lam_re + lam_im * lam_im
    cr = ((ar - 1.0) * lam_re + ai * lam_im) / den
    ci = (ai * lam_re - (ar - 1.0) * lam_im) / den
    bbar_r = cr[..., None] * b_re - ci[..., None] * b_im
    bbar_i = cr[..., None] * b_im + ci[..., None] * b_re
    eye = jnp.eye(S5_PACK, dtype=F32)

    def bdiag(m):
        g, a, b = m.shape
        m4 = m.reshape(g // S5_PACK, S5_PACK, a, b)
        return (eye[None, :, None, :, None] * m4[:, :, :, None, :]).reshape(g // S5_PACK, S5_PACK * a, S5_PACK * b)

    b3 = jnp.concatenate([bdiag(bbar_r.transpose(0, 2, 1)), bdiag(bbar_i.transpose(0, 2, 1))], axis=0)
    c3 = jnp.concatenate([bdiag(c_re.transpose(0, 2, 1)), -bdiag(c_im.transpose(0, 2, 1))], axis=0)
    return b3, c3, d_skip.reshape(1, GROUP_WIDTH), ar.reshape(1, S5_WIDTH), ai.reshape(1, S5_WIDTH)


RET_UNROLL = 8


def _loop_unrolled(trips, body, init):
    factor = math.gcd(trips, RET_UNROLL)

    def several(i, carry):
        for u in range(factor):
            carry = body(i * factor + u, carry)
        return carry

    return lax.fori_loop(0, trips // factor, several, init)


def _ret_consts(lgam):
    c = RET_CHUNK
    r = lax.broadcasted_iota(jnp.int32, (c, c), 0)
    m = lax.broadcasted_iota(jnp.int32, (c, c), 1)
    rel = (r - m).astype(F32)
    decay = jnp.where(rel >= 0, jnp.exp(lgam * jnp.maximum(rel, 0.0)), 0.0)
    idx = lax.broadcasted_iota(jnp.int32, (c, 1), 0).astype(F32)
    zeta = jnp.exp(lgam * (c - 1.0 - idx))
    xi = jnp.exp(lgam * (idx + 1.0))
    return decay, zeta, xi, jnp.exp(lgam * c)


def _ret_specs(t):
    qk = lambda off: pl.BlockSpec((1, t, RET_QK), lambda h: (h + off, 0, 0))
    col = lambda off: pl.BlockSpec((t, RET_V), lambda h: (0, h + off))
    return qk, col


def _ret_fwd(qk, p_ret, lgam, name):
    t = qk.shape[1]
    nck = t // RET_CHUNK
    qk_spec, col = _ret_specs(t)

    def body(lg_ref, q_ref, k_ref, v_ref, g_ref, o_ref, y_ref):
        decay, zeta, xi, gam = _ret_consts(lg_ref[pl.program_id(0)])

        def step(n, state):
            sl = pl.ds(pl.multiple_of(n * RET_CHUNK, RET_CHUNK), RET_CHUNK)
            q, k, v = q_ref[0, sl, :], k_ref[0, sl, :] * (RET_QK ** -0.5), v_ref[sl, :]
            s = _dot(q, k, "nt") * decay
            o = _dot(s, v) + _dot(q, state) * xi
            o_ref[sl, :] = o
            y_ref[sl, :] = _ret_gate_fn(o, g_ref[sl, :]).astype(y_ref.dtype)
            return gam * state + _dot(k, zeta * v, "tn")

        _loop_unrolled(nck, step, jnp.zeros((RET_QK, RET_V), F32))

    return pl.pallas_call(
        body, name=name, grid=(RET_HEADS,), in_specs=[_SMEM, qk_spec(0), qk_spec(RET_HEADS), col(4), col(8)],
        out_specs=[col(0), col(0)],
        out_shape=[jax.ShapeDtypeStruct((t, GROUP_WIDTH), F32), jax.ShapeDtypeStruct((t, GROUP_WIDTH), BF16)],
        compiler_params=_params(("parallel",)),
    )(lgam, qk, qk, p_ret, p_ret)


def _ret_bwd(qk, p_ret, o_all, dy, lgam, name):
    t = qk.shape[1]
    nck = t // RET_CHUNK
    qk_spec, col = _ret_specs(t)
    gate_bwd = _vjp_block(_ret_gate_fn, 2)

    def body(lg_ref, q_ref, k_ref, v_ref, g_ref, o_ref, dy_ref, dq_ref, dk_ref, dv_ref, dg_ref, st_ref):
        decay, zeta, xi, gam = _ret_consts(lg_ref[pl.program_id(0)])
        scale = RET_QK ** -0.5

        def fstep(n, state):
            sl = pl.ds(pl.multiple_of(n * RET_CHUNK, RET_CHUNK), RET_CHUNK)
            st_ref[n] = state
            return gam * state + _dot(k_ref[0, sl, :] * scale, zeta * v_ref[sl, :], "tn")

        _loop_unrolled(nck, fstep, jnp.zeros((RET_QK, RET_V), F32))

        def bstep(r, grad_state):
            n = nck - 1 - r
            sl = pl.ds(pl.multiple_of(n * RET_CHUNK, RET_CHUNK), RET_CHUNK)
            q, k, v = q_ref[0, sl, :], k_ref[0, sl, :] * scale, v_ref[sl, :]
            d_o, dg = gate_bwd(o_ref[sl, :], g_ref[sl, :], dy_ref[sl, :])
            dg_ref[sl, :] = dg.astype(dg_ref.dtype)
            s = _dot(q, k, "nt") * decay
            ds = _dot(d_o, v, "nt") * decay
            xdo = xi * d_o
            dq_ref[0, sl, :] = _dot(ds, k) + _dot(xdo, st_ref[n], "nt")
            dk_ref[0, sl, :] = (_dot(ds, q, "tn") + _dot(zeta * v, grad_state, "nt")) * scale
            dv_ref[sl, :] = (_dot(s, d_o, "tn") + zeta * _dot(k, grad_state)).astype(dv_ref.dtype)
            return gam * grad_state + _dot(q, xdo, "tn")

        _loop_unrolled(nck, bstep, jnp.zeros((RET_QK, RET_V), F32))

    hd = pl.BlockSpec((1, t, RET_QK), lambda h: (h, 0, 0))
    return pl.pallas_call(
        body, name=name, grid=(RET_HEADS,),
        in_specs=[_SMEM, qk_spec(0), qk_spec(RET_HEADS), col(4), col(8), col(0), col(0)],
        out_specs=[hd, hd, col(0), col(0)],
        out_shape=[jax.ShapeDtypeStruct((RET_HEADS, t, RET_QK), F32)] * 2 + [jax.ShapeDtypeStruct((t, GROUP_WIDTH), BF16)] * 2,
        scratch_shapes=[pltpu.VMEM((nck, RET_QK, RET_V), F32)], compiler_params=_params(("parallel",)),
    )(lgam, qk, qk, p_ret, p_ret, o_all, dy)


SWA_GROUP = SWA_HEADS // SWA_KV_HEADS
SWA_SCALE = SWA_HD ** -0.5


def _swa_mask(n):
    rows = SWA_GROUP * WINDOW
    r = lax.broadcasted_iota(jnp.int32, (rows, 2 * WINDOW), 0) & (WINDOW - 1)
    j = lax.broadcasted_iota(jnp.int32, (rows, 2 * WINDOW), 1)
    dist = r + WINDOW - j
    return (dist >= 0) & (dist < WINDOW) & (n * WINDOW + j - WINDOW >= 0)


def _swa_sink_rows(sink_ref, kv):
    row = lax.broadcasted_iota(jnp.int32, (SWA_GROUP * WINDOW, 1), 0)
    sink = jnp.zeros((SWA_GROUP * WINDOW, 1), F32)
    for g in range(SWA_GROUP):
        sink = jnp.where(row >= g * WINDOW, sink_ref[kv * SWA_GROUP + g], sink)
    return sink


def _swa_pad_keys(n, k_ref, v_ref, kp_ref, vp_ref):
    @pl.when(n == 0)
    def _():
        zero = jnp.zeros((WINDOW, SWA_HD), F32)
        kp_ref[0:WINDOW, :] = zero
        vp_ref[0:WINDOW, :] = zero
        kp_ref[WINDOW:, :] = k_ref[0]
        vp_ref[WINDOW:, :] = v_ref[0]


SWA_STEP_BLOCKS = 4


def _swa_specs(t):
    per_step = math.gcd(t // WINDOW, SWA_STEP_BLOCKS)
    blk = lambda w: pl.BlockSpec((SWA_GROUP, per_step * WINDOW, w), lambda kv, n: (kv, n, 0))
    kspec = lambda off: pl.BlockSpec((1, t, SWA_HD), lambda kv, n: (SWA_HEADS + off + kv, 0, 0))
    return blk, kspec, per_step


def _swa_fwd(qkv, sinks, name):
    t = qkv.shape[1]
    rows = SWA_GROUP * WINDOW
    blk, kspec, per_step = _swa_specs(t)

    def body(sink_ref, q_ref, k_ref, v_ref, o_ref, lse_ref, kp_ref, vp_ref):
        kv, step = pl.program_id(0), pl.program_id(1)
        _swa_pad_keys(step, k_ref, v_ref, kp_ref, vp_ref)
        sink = _swa_sink_rows(sink_ref, kv)
        for u in range(per_step):
            n = step * per_step + u
            here = slice(u * WINDOW, (u + 1) * WINDOW)
            win = pl.ds(pl.multiple_of(n * WINDOW, WINDOW), 2 * WINDOW)
            s = _dot(q_ref[:, here, :].reshape(rows, SWA_HD), kp_ref[win, :], "nt") * SWA_SCALE
            s = jnp.where(_swa_mask(n), s, NEG)
            m = jnp.maximum(jnp.max(s, axis=-1, keepdims=True), sink)
            p = jnp.exp(s - m)
            den = jnp.sum(p, axis=-1, keepdims=True) + jnp.exp(sink - m)
            o_ref[:, here, :] = _dot(p / den, vp_ref[win, :]).reshape(SWA_GROUP, WINDOW, SWA_HD)
            lse_ref[:, here, :] = (m + jnp.log(den)).reshape(SWA_GROUP, WINDOW, 1)

    return pl.pallas_call(
        body, name=name, grid=(SWA_KV_HEADS, t // WINDOW // per_step),
        in_specs=[_SMEM, blk(SWA_HD), kspec(0), kspec(SWA_KV_HEADS)], out_specs=[blk(SWA_HD), blk(1)],
        out_shape=[jax.ShapeDtypeStruct((SWA_HEADS, t, SWA_HD), F32), jax.ShapeDtypeStruct((SWA_HEADS, t, 1), F32)],
        scratch_shapes=[pltpu.VMEM((t + WINDOW, SWA_HD), F32)] * 2, compiler_params=_params(("parallel", "arbitrary")),
    )(sinks, qkv, qkv, qkv)


def _swa_bwd(qkv, o, lse, d_o, sinks, name):
    t = qkv.shape[1]
    nb = t // WINDOW
    rows = SWA_GROUP * WINDOW
    blk, kspec, per_step = _swa_specs(t)

    def body(sink_ref, q_ref, k_ref, v_ref, o_ref, lse_ref, do_ref, dq_ref, dk_ref, dv_ref, dsink_ref,
             kp_ref, vp_ref, dkp_ref, dvp_ref):
        kv, step = pl.program_id(0), pl.program_id(1)
        _swa_pad_keys(step, k_ref, v_ref, kp_ref, vp_ref)

        @pl.when(step == 0)
        def _():
            dkp_ref[...] = jnp.zeros_like(dkp_ref)
            dvp_ref[...] = jnp.zeros_like(dvp_ref)
            dsink_ref[...] = jnp.zeros_like(dsink_ref)

        sink = _swa_sink_rows(sink_ref, kv)
        head = lax.broadcasted_iota(jnp.int32, (SWA_GROUP, 128), 0)
        acc = jnp.zeros((SWA_GROUP, 128), F32)
        for u in range(per_step):
            n = step * per_step + u
            here = slice(u * WINDOW, (u + 1) * WINDOW)
            win = pl.ds(pl.multiple_of(n * WINDOW, WINDOW), 2 * WINDOW)
            q, dout = q_ref[:, here, :].reshape(rows, SWA_HD), do_ref[:, here, :].reshape(rows, SWA_HD)
            lse_n = lse_ref[:, here, :].reshape(rows, 1)
            s = _dot(q, kp_ref[win, :], "nt") * SWA_SCALE
            s = jnp.where(_swa_mask(n), s, NEG)
            p = jnp.exp(s - lse_n)
            delta = jnp.sum(dout * o_ref[:, here, :].reshape(rows, SWA_HD), axis=-1, keepdims=True)
            ds = p * (_dot(dout, vp_ref[win, :], "nt") - delta)
            dq_ref[:, here, :] = (_dot(ds, kp_ref[win, :]) * SWA_SCALE).reshape(SWA_GROUP, WINDOW, SWA_HD).astype(dq_ref.dtype)
            dkp_ref[win, :] += _dot(ds, q, "tn") * SWA_SCALE
            dvp_ref[win, :] += _dot(p, dout, "tn")
            term = jnp.exp(sink - lse_n) * delta
            for g in range(SWA_GROUP):
                acc = jnp.where(head == g, acc + jnp.sum(term[g * WINDOW:(g + 1) * WINDOW], axis=0, keepdims=True), acc)
        dsink_ref[0] -= acc

        @pl.when(step == nb // per_step - 1)
        def _():
            dk_ref[0] = dkp_ref[WINDOW:, :].astype(dk_ref.dtype)
            dv_ref[0] = dvp_ref[WINDOW:, :].astype(dv_ref.dtype)

    kout = pl.BlockSpec((1, t, SWA_HD), lambda kv, n: (kv, 0, 0))
    dq, dk, dv, dsink = pl.pallas_call(
        body, name=name, grid=(SWA_KV_HEADS, nb // per_step),
        in_specs=[_SMEM, blk(SWA_HD), kspec(0), kspec(SWA_KV_HEADS), blk(SWA_HD), blk(1), blk(SWA_HD)],
        out_specs=[blk(SWA_HD), kout, kout, pl.BlockSpec((1, SWA_GROUP, 128), lambda kv, n: (kv, 0, 0))],
        out_shape=[jax.ShapeDtypeStruct((SWA_HEADS, t, SWA_HD), BF16), jax.ShapeDtypeStruct((SWA_KV_HEADS, t, SWA_HD), BF16),
                   jax.ShapeDtypeStruct((SWA_KV_HEADS, t, SWA_HD), BF16), jax.ShapeDtypeStruct((SWA_KV_HEADS, SWA_GROUP, 128), F32)],
        scratch_shapes=[pltpu.VMEM((t + WINDOW, SWA_HD), F32)] * 4, compiler_params=_params(("parallel", "arbitrary")),
    )(sinks, qkv, qkv, qkv, o, lse, d_o)
    return jnp.concatenate([dq, dk, dv], axis=0), dsink[:, :, 0].reshape(SWA_HEADS)


MLA_SCALE = (MLA_NOPE + MLA_ROPE) ** -0.5
MLA_TILE = 512
MLA_KEY_TILE = 512
MLA_BWD_TILE = 512


def _mla_diag(s):
    r = lax.broadcasted_iota(jnp.int32, s.shape, 0)
    c = lax.broadcasted_iota(jnp.int32, s.shape, 1)
    return jnp.where(c <= r, s, NEG)


def _mla_specs(t, tile):
    whole = lambda w, off: pl.BlockSpec((t, w), lambda h, i: (0, 2 * h + off))
    head = lambda w: pl.BlockSpec((1, t, w), lambda h, i: (h, 0, 0))
    key_rope = pl.BlockSpec((1, t, MLA_ROPE), lambda h, i: (MLA_HEADS, 0, 0))
    tile_of = lambda w: pl.BlockSpec((1, tile, w), lambda h, i: (h, i, 0))
    return whole, head, key_rope, tile_of


def _mla_attend(qn, rot, kv, name):
    t = qn.shape[1]
    tile = min(MLA_TILE, t)
    ktile = min(MLA_KEY_TILE, t)
    ratio = ktile // tile
    whole, head, key_rope, tile_of = _mla_specs(t, tile)

    def body(qn_ref, qr_ref, kn_ref, kr_ref, v_ref, o_ref, lse_ref, m_ref, l_ref, acc_ref):
        i = pl.program_id(1)
        qn_b, qr_b = qn_ref[0], qr_ref[0]

        def rows(j):
            return pl.ds(pl.multiple_of(j * ktile, ktile), ktile)

        def scores(j):
            return (_dot(qn_b, kn_ref[rows(j), :], "nt") + _dot(qr_b, kr_ref[0, rows(j), :], "nt")) * MLA_SCALE

        def causal(s, j):
            qpos = i * tile + lax.broadcasted_iota(jnp.int32, s.shape, 0)
            kpos = j * ktile + lax.broadcasted_iota(jnp.int32, s.shape, 1)
            return jnp.where(kpos <= qpos, s, NEG)

        def update(s, j):
            m_old = m_ref[...]
            m_new = jnp.maximum(m_old, jnp.max(s, axis=-1, keepdims=True))
            alpha = jnp.exp(m_old - m_new)
            p = jnp.exp(s - m_new)
            l_ref[...] = alpha * l_ref[...] + jnp.sum(p, axis=-1, keepdims=True)
            acc_ref[...] = alpha * acc_ref[...] + _dot(p, v_ref[rows(j), :])
            m_ref[...] = m_new

        m_ref[...] = jnp.full_like(m_ref, NEG)
        l_ref[...] = jnp.zeros_like(l_ref)
        acc_ref[...] = jnp.zeros_like(acc_ref)

        def step(j, s_cur):
            s_next = scores(j + 1)
            update(s_cur, j)
            return s_next

        last = i // ratio
        s_last = lax.fori_loop(0, last, step, scores(0))
        update(causal(s_last, last), last)
        o_ref[...] = acc_ref[...] / l_ref[...]
        lse_ref[0] = m_ref[...] + jnp.log(l_ref[...])

    return pl.pallas_call(
        body, name=name, grid=(MLA_HEADS, t // tile),
        in_specs=[tile_of(MLA_NOPE), tile_of(MLA_ROPE), whole(MLA_NOPE, 0), key_rope, whole(MLA_V, 1)],
        out_specs=[pl.BlockSpec((tile, MLA_V), lambda h, i: (i, h)), tile_of(1)],
        out_shape=[jax.ShapeDtypeStruct((t, GROUP_WIDTH), F32), jax.ShapeDtypeStruct((MLA_HEADS, t, 1), F32)],
        scratch_shapes=[pltpu.VMEM((tile, 1), F32), pltpu.VMEM((tile, 1), F32), pltpu.VMEM((tile, MLA_V), F32)],
        compiler_params=_params(("parallel", "parallel")),
    )(qn, rot, kv, rot, kv)


def _mla_attend_bwd(qn, rot, kv, lse, o, d_o, name):
    t = qn.shape[1]
    tile = min(MLA_BWD_TILE, t)
    nt = t // tile
    whole, head, key_rope, tile_of = _mla_specs(t, tile)

    def body(qn_ref, qr_ref, kn_ref, kr_ref, v_ref, lse_ref, o_ref, do_ref, dqn_ref, dqr_ref, dkn_ref, dv_ref, dkr_ref, dl_ref):
        j = pl.program_id(1)

        @pl.when(j == 0)
        def _():
            dqn_ref[...] = jnp.zeros_like(dqn_ref)
            dqr_ref[...] = jnp.zeros_like(dqr_ref)
            dl_ref[0] = jnp.sum(do_ref[...] * o_ref[...], axis=-1, keepdims=True)

        dkn_ref[...] = jnp.zeros_like(dkn_ref)
        dv_ref[...] = jnp.zeros_like(dv_ref)
        dkr_ref[...] = jnp.zeros_like(dkr_ref)
        kn_b, kr_b, v_b = kn_ref[...], kr_ref[0], v_ref[...]

        def block(i, diagonal):
            sl = pl.ds(pl.multiple_of(i * tile, tile), tile)
            qn_b, qr_b, dout = qn_ref[0, sl, :], qr_ref[0, sl, :], do_ref[sl, :]
            s = (_dot(qn_b, kn_b, "nt") + _dot(qr_b, kr_b, "nt")) * MLA_SCALE
            if diagonal:
                s = _mla_diag(s)
            p = jnp.exp(s - lse_ref[0, sl, :])
            ds = p * (_dot(dout, v_b, "nt") - dl_ref[0, sl, :]) * MLA_SCALE
            dv_ref[...] += _dot(p, dout, "tn")
            dkn_ref[...] += _dot(ds, qn_b, "tn")
            dkr_ref[0] += _dot(ds, qr_b, "tn")
            dqn_ref[0, sl, :] += _dot(ds, kn_b)
            dqr_ref[0, sl, :] += _dot(ds, kr_b)

        block(j, True)

        def step(i, carry):
            block(i, False)
            return carry

        lax.fori_loop(j + 1, nt, step, 0)

    key_tile = lambda w, off: pl.BlockSpec((tile, w), lambda h, j: (j, 2 * h + off))
    out_tile = pl.BlockSpec((tile, MLA_V), lambda h, j: (j, h))
    return pl.pallas_call(
        body, name=name, grid=(MLA_HEADS, nt),
        in_specs=[head(MLA_NOPE), head(MLA_ROPE), key_tile(MLA_NOPE, 0), pl.BlockSpec((1, tile, MLA_ROPE), lambda h, j: (MLA_HEADS, j, 0)),
                  key_tile(MLA_V, 1), head(1), pl.BlockSpec((t, MLA_V), lambda h, j: (0, h)), pl.BlockSpec((t, MLA_V), lambda h, j: (0, h))],
        out_specs=[head(MLA_NOPE), head(MLA_ROPE), out_tile, out_tile, tile_of(MLA_ROPE)],
        out_shape=[jax.ShapeDtypeStruct((MLA_HEADS, t, MLA_NOPE), F32), jax.ShapeDtypeStruct((MLA_HEADS, t, MLA_ROPE), F32),
                   jax.ShapeDtypeStruct((t, MLA_HEADS * MLA_NOPE), F32), jax.ShapeDtypeStruct((t, MLA_HEADS * MLA_V), F32),
                   jax.ShapeDtypeStruct((MLA_HEADS, t, MLA_ROPE), F32)],
        scratch_shapes=[pltpu.VMEM((1, t, 1), F32)], compiler_params=_params(("parallel", "arbitrary")),
    )(qn, rot, kv, rot, kv, lse, o, d_o)


def _place():
    return lax.axis_index("x"), lax.axis_index("y"), lax.axis_index("c")


def _all_gather(arrs, name):
    n = len(arrs)

    def body(*refs):
        x_refs, o_refs = refs[:n], refs[n:2 * n]
        send_sems, recv_sems, local_sems = refs[2 * n:]
        x, y, c = _place()
        me, sibling = (x, y, c), (x, y, 1 - c)
        chips = [(1 - x, y), (x, 1 - y), (1 - x, 1 - y)]

        def slot(a, p):
            return o_refs[a].at[4 * p[0] + 2 * p[1] + p[2]]

        def copy(a, k, block, to, src=None):
            return pltpu.make_async_remote_copy(
                src_ref=slot(a, block) if src is None else src, dst_ref=slot(a, block),
                send_sem=send_sems.at[a, k], recv_sem=recv_sems.at[a, k], device_id=to, device_id_type=MESH_ID)

        mine = [pltpu.make_async_copy(x_refs[a], slot(a, me), local_sems.at[a]) for a in range(n)]
        for cp in mine:
            cp.start()
        first = []
        for a in range(n):
            first.append(copy(a, 0, me, sibling, src=x_refs[a]))
            first += [copy(a, 1 + j, me, (*chip, c), src=x_refs[a]) for j, chip in enumerate(chips)]
        for cp in first:
            cp.start()
        passed = []
        for j, chip in enumerate(chips):
            for a in range(n):
                copy(a, 1 + j, (*chip, c), me).wait_recv()
                cp = copy(a, 4 + j, (*chip, c), sibling)
                cp.start()
                passed.append(cp)
        for a in range(n):
            copy(a, 0, sibling, me).wait_recv()
            for j, chip in enumerate(chips):
                copy(a, 4 + j, (*chip, 1 - c), me).wait_recv()
        for cp in first + passed:
            cp.wait_send()
        for cp in mine:
            cp.wait()

    return pl.pallas_call(
        body, name=name, in_specs=[_ANY] * n, out_specs=[_ANY] * n,
        out_shape=[jax.ShapeDtypeStruct((N_DEV,) + a.shape, a.dtype) for a in arrs],
        scratch_shapes=[pltpu.SemaphoreType.DMA((n, 7)), pltpu.SemaphoreType.DMA((n, 7)), pltpu.SemaphoreType.DMA((n,))],
    )(*arrs)


def _pass_to_sibling(arrs, name):
    n = len(arrs)

    def body(*refs):
        a_refs, o_refs = refs[:n], refs[n:2 * n]
        send_sems, recv_sems = refs[2 * n:]
        x, y, c = _place()
        chips = [(1 - x, y), (x, 1 - y), (1 - x, 1 - y)]
        slot = lambda px, py, pc: 4 * px + 2 * py + pc
        sends, recvs = [], []
        for a in range(n):
            for j, (px, py) in enumerate(chips):
                sends.append(pltpu.make_async_remote_copy(
                    src_ref=a_refs[a].at[slot(px, py, c)], dst_ref=o_refs[a].at[slot(px, py, c)], send_sem=send_sems.at[a, j],
                    recv_sem=recv_sems.at[a, j], device_id=(x, y, 1 - c), device_id_type=MESH_ID))
                recvs.append(pltpu.make_async_remote_copy(
                    src_ref=a_refs[a].at[slot(px, py, c)], dst_ref=o_refs[a].at[slot(px, py, 1 - c)], send_sem=send_sems.at[a, j],
                    recv_sem=recv_sems.at[a, j], device_id=(x, y, 1 - c), device_id_type=MESH_ID))
        for cp in sends:
            cp.start()
        for cp in sends:
            cp.wait_send()
        for cp in recvs:
            cp.wait_recv()

    return pl.pallas_call(
        body, name=name, in_specs=[_ANY] * n, out_specs=[_ANY] * n,
        out_shape=[jax.ShapeDtypeStruct(a.shape, a.dtype) for a in arrs], input_output_aliases={i: i for i in range(n)},
        scratch_shapes=[pltpu.SemaphoreType.DMA((n, 3)), pltpu.SemaphoreType.DMA((n, 3))],
    )(*arrs)


def _scatter_core(grads, name):
    n = len(grads)

    def body(*refs):
        g_refs, got_refs = refs[:n], refs[n:2 * n]
        send_sems, recv_sems = refs[2 * n:]
        x, y, c = _place()
        sends = [pltpu.make_async_remote_copy(
            src_ref=g_refs[a].at[2 * q + 1 - c], dst_ref=got_refs[a].at[q], send_sem=send_sems.at[a, q],
            recv_sem=recv_sems.at[a, q], device_id=(x, y, 1 - c), device_id_type=MESH_ID) for a in range(n) for q in range(4)]
        for cp in sends:
            cp.start()
        for cp in sends:
            cp.wait()

    return pl.pallas_call(
        body, name=name, in_specs=[_ANY] * n, out_specs=[_ANY] * n,
        out_shape=[jax.ShapeDtypeStruct((4,) + g.shape[1:], g.dtype) for g in grads],
        scratch_shapes=[pltpu.SemaphoreType.DMA((n, 4)), pltpu.SemaphoreType.DMA((n, 4))],
    )(*grads)


def _scatter_chips(parts, name):
    n = len(parts)

    def body(*refs):
        p_refs, o_refs = refs[:n], refs[n:2 * n]
        send_sems, recv_sems = refs[2 * n:]
        x, y, c = _place()
        chips = [(1 - x, y), (x, 1 - y), (1 - x, 1 - y)]
        sends = [pltpu.make_async_remote_copy(
            src_ref=p_refs[a].at[2 * px + py], dst_ref=o_refs[a].at[j], send_sem=send_sems.at[a, j],
            recv_sem=recv_sems.at[a, j], device_id=(px, py, c), device_id_type=MESH_ID)
            for a in range(n) for j, (px, py) in enumerate(chips)]
        for cp in sends:
            cp.start()
        for cp in sends:
            cp.wait()

    return pl.pallas_call(
        body, name=name, in_specs=[_ANY] * n, out_specs=[_ANY] * n,
        out_shape=[jax.ShapeDtypeStruct((3,) + p.shape[1:], p.dtype) for p in parts],
        scratch_shapes=[pltpu.SemaphoreType.DMA((n, 3)), pltpu.SemaphoreType.DMA((n, 3))],
    )(*parts)


GATHER_PEERS = [(0, 0, 1), (1, 0, 0), (0, 1, 0), (1, 1, 0)]
CHIP_PEERS = [(1, 0, 0), (0, 1, 0), (1, 1, 0)]
CORE_PEERS = [(0, 0, 1)] * 4
_HBM = pl.BlockSpec(memory_space=pltpu.HBM)
_SEM = pl.BlockSpec(memory_space=pltpu.SEMAPHORE)
_EFFECT = pltpu.SideEffectType.DATAFLOW_SIDE_EFFECTING


def _push_copies(src_refs, land_refs, send_sems, recv_sems, peers, src_of, slot_of):
    place = _place()
    flip = lambda v, f: 1 - v if f else v
    return [pltpu.make_async_remote_copy(
        src_ref=src_of(src_refs[a], k), dst_ref=land_refs[a].at[slot_of(k)], send_sem=send_sems[a], recv_sem=recv_sems[a],
        device_id=tuple(flip(v, f) for v, f in zip(place, peer)), device_id_type=MESH_ID)
        for a in range(len(src_refs)) for k, peer in enumerate(peers)]


def _push_start(srcs, land_shapes, peers, src_of, slot_of, name, after=None):
    n = len(srcs)
    extra = [] if after is None else [after]

    def body(*refs):
        src_refs, land_refs = refs[:n], refs[n:2 * n]
        refs = refs[2 * n + len(extra):]
        send_sems, recv_sems = refs[:n], refs[n:2 * n]
        token = refs[-1]
        for cp in _push_copies(src_refs, land_refs, send_sems, recv_sems, peers, src_of, slot_of):
            cp.start()
        token[...] = jnp.zeros_like(token)

    sems = [pltpu.SemaphoreType.DMA(())] * (2 * n)
    lands = [pltpu.with_memory_space_constraint(lax.empty(s.shape, s.dtype), pltpu.HBM) for s in land_shapes]
    res = pl.pallas_call(
        body, name=name, in_specs=[_HBM] * (2 * n) + [_ANY] * len(extra),
        out_specs=[_SEM] * (2 * n) + [_HBM] * (2 * n) + [pl.BlockSpec(memory_space=pltpu.VMEM)],
        out_shape=sems + [pltpu.HBM(s.shape, s.dtype) for s in srcs] + [pltpu.HBM(s.shape, s.dtype) for s in land_shapes]
        + [jax.ShapeDtypeStruct((8, 128), F32)],
        input_output_aliases={i: 2 * n + i for i in range(2 * n)},
        compiler_params=pltpu.CompilerParams(has_side_effects=_EFFECT),
    )(*[pltpu.with_memory_space_constraint(s, pltpu.HBM) for s in srcs], *lands, *extra)
    return list(res[:n]), list(res[n:2 * n]), list(res[2 * n:3 * n]), list(res[3 * n:4 * n]), res[-1]


def _push_wait(send_sems, recv_sems, srcs, lands, after, peers, src_of, slot_of, name):
    n = len(srcs)

    def body(*refs):
        src_refs, land_refs = refs[:n], refs[n:2 * n]
        s_sems, r_sems = refs[2 * n:3 * n], refs[3 * n:4 * n]
        copies = _push_copies(src_refs, land_refs, s_sems, r_sems, peers, src_of, slot_of)
        for cp in copies:
            cp.wait_send()
        for cp in copies:
            cp.wait_recv()

    res = pl.pallas_call(
        body, name=name, in_specs=[_HBM] * (2 * n) + [_SEM] * (2 * n) + [_ANY], out_specs=[_HBM] * (2 * n),
        out_shape=[pltpu.HBM(s.shape, s.dtype) for s in srcs] + [pltpu.HBM(s.shape, s.dtype) for s in lands],
        input_output_aliases={i: i for i in range(2 * n)},
        compiler_params=pltpu.CompilerParams(has_side_effects=_EFFECT),
    )(*srcs, *lands, *send_sems, *recv_sems, after)
    return list(res[:n]), list(res[n:])


def _pick_sum(picked, rest, index, pick_of, out_dtype, name):
    nq, r, cdim = rest.shape
    one = nq == 3
    tr = _row_tile(r, 512, 16)
    tc = 512 if cdim % 512 == 0 else cdim
    grid = (1 if one else nq, r // tr, cdim // tc)

    def body(i_ref, p_ref, r_ref, o_ref):
        acc = p_ref[0].astype(F32)
        if one:
            for j in range(3):
                acc = acc + r_ref[j].astype(F32)
            o_ref[...] = acc.astype(out_dtype)
        else:
            o_ref[0] = (acc + r_ref[0].astype(F32)).astype(out_dtype)

    spec = pltpu.PrefetchScalarGridSpec(
        num_scalar_prefetch=1, grid=grid,
        in_specs=[pl.BlockSpec((1, tr, tc), lambda q, i, j, i_ref: (pick_of(q, i_ref[0]), i, j)),
                  pl.BlockSpec((3, tr, tc), lambda q, i, j, i_ref: (0, i, j)) if one else pl.BlockSpec((1, tr, tc), lambda q, i, j, i_ref: (q, i, j))],
        out_specs=pl.BlockSpec((tr, tc), lambda q, i, j, i_ref: (i, j)) if one else pl.BlockSpec((1, tr, tc), lambda q, i, j, i_ref: (q, i, j)))
    return pl.pallas_call(
        body, name=name, grid_spec=spec,
        out_shape=jax.ShapeDtypeStruct((r, cdim) if one else (nq, r, cdim), out_dtype),
        compiler_params=_params(("parallel", "parallel", "parallel")),
    )(index.astype(jnp.int32).reshape(1), picked, rest)


def _adamw_fn(w, g, m, v):
    m = ADAM_B1 * m + (1.0 - ADAM_B1) * g
    v = ADAM_B2 * v + (1.0 - ADAM_B2) * jnp.square(g)
    m_hat = m / (1.0 - ADAM_B1 ** ADAM_STEP)
    v_hat = v / (1.0 - ADAM_B2 ** ADAM_STEP)
    delta = -ADAM_LR * (m_hat / (jnp.sqrt(v_hat) + ADAM_EPS) + ADAM_WD * w)
    return delta, m, v


def _as2d(a):
    return a.reshape(-1, a.shape[-1])


def _adamw_shard(w, g, m, v, name):
    shape = w.shape
    ins = [_as2d(a) for a in (w, g, m, v)]
    cols = ins[0].shape[1]
    outs = _ew(_adamw_fn, [(a, True) for a in ins], [(cols, F32, "tile")] * 3, 256, name)
    return [o.reshape(shape) for o in outs]


def _adamw_small(ws, gs, ms, vs, name):
    shapes = [w.shape for w in ws]
    flat = lambda a: a.reshape(-1, 128) if a.size % 128 == 0 else a.reshape(1, -1)
    ins = [flat(a) for grp in zip(ws, gs, ms, vs) for a in grp]
    k = len(ws)

    def fn(*vals):
        out = []
        for i in range(k):
            out += list(_adamw_fn(*vals[4 * i:4 * i + 4]))
        return out

    outs = _whole(fn, ins, [(ins[4 * (i // 3)].shape, F32) for i in range(3 * k)], name)
    deltas = [outs[3 * i].reshape(shapes[i]) for i in range(k)]
    new_m = [outs[3 * i + 1].reshape(shapes[i]) for i in range(k)]
    new_v = [outs[3 * i + 2].reshape(shapes[i]) for i in range(k)]
    return deltas, new_m, new_v


def _sum8(stacked, name):
    def fn(a):
        s = a[0:1]
        for i in range(1, N_DEV):
            s = s + a[i:i + 1]
        return s
    w = stacked.shape[1]
    tw = 8192
    if w % tw:
        return _whole(fn, [stacked], [((1, w), F32)], name)[0]

    def body(a_ref, o_ref):
        o_ref[...] = fn(a_ref[...])

    return pl.pallas_call(body, name=name, grid=(w // tw,), in_specs=[pl.BlockSpec((N_DEV, tw), lambda i: (0, i))],
                          out_specs=pl.BlockSpec((1, tw), lambda i: (0, i)), out_shape=jax.ShapeDtypeStruct((1, w), F32))(stacked)


ROWS = 512


def _split_heads(p, nh):
    t = p.shape[0]
    return p.reshape(t, nh, p.shape[1] // nh).transpose(1, 0, 2)


def _merge_heads(p):
    nh, t, d = p.shape
    return p.transpose(1, 0, 2).reshape(t, nh * d)


def _layer_fwd(h, mod, w, small, rope, l, late=None):
    sh1, sc1, gt1, sh2, sc2, gt2 = mod
    rope_ret, rope_mla = rope
    t = h.shape[0]
    nm = lambda s: f"l{l}_{s}"
    a1 = _ew(_norm_mod_fn, [(h, True), (small["norm1_g"], False), (sc1, False), (sh1, False)], [(D_MODEL, BF16, "tile")], ROWS, nm("norm1"))[0]
    p_s5 = _mm(a1, w["w_in_t"], "nt", 512, 512, 2048, name=nm("proj_s5"), n=512)
    p_ret = _mm(a1, w["w_in_t"], "nt", 512, 512, 2048, name=nm("proj_ret"), b_off=1, n=1536)
    p_swa = _mm(a1, w["w_in_t"], "nt", 512, 256, 2048, name=nm("proj_swa"), b_off=8, n=768)
    p_mla = _mm(a1, w["w_in_t"][2816:], "nt", 512, 576, 2048, name=nm("proj_mla"))
    b3, c3, dskip, a_r, a_i = small["s5"]
    st_r, st_i = _s5_scan_fwd(p_s5, b3, a_r, a_i, nm("s5_scan"))
    ypre = _mm_blocks(st_r, c3[:S5_BLOCKS], "nn", 512, pair=(st_i, c3[S5_BLOCKS:]), name=nm("s5_y"))
    z = _ew(_s5_act_fn, [(ypre, True), (p_s5, True), (dskip, False)], [(GROUP_WIDTH, F32, "tile")], ROWS, nm("s5_act"))[0]
    zz = _mm(z, w["glu_w"], "nn", 512, 512, 512, name=nm("s5_zz"))
    y_s5 = _ew(_s5_glu_fn, [(z, True), (zz, True), (small["s5_glu_b"], False)], [(GROUP_WIDTH, BF16, "tile")], ROWS, nm("s5_glu"))[0]
    qk_ret = _split_heads(_rope(p_ret[:, :2 * RET_HEADS * RET_QK], rope_ret, nm("ret_rope")), 2 * RET_HEADS)
    o_ret, y_ret = _ret_fwd(qk_ret, p_ret, small["ret_lgam"], nm("ret"))
    qkv_swa = _split_heads(p_swa, 12)
    o_swa, lse_swa = _swa_fwd(qkv_swa, small["swa_sinks"], nm("swa"))
    y_swa = _merge_heads(o_swa).astype(BF16)
    cq, ckv, kr = p_mla[:, :MLA_Q_RANK], p_mla[:, MLA_Q_RANK:MLA_Q_RANK + MLA_KV_RANK], p_mla[:, MLA_Q_RANK + MLA_KV_RANK:]
    cqn = _ew(_rms_gain_fn, [(cq, True), (small["mla_q_norm"], False)], [(MLA_Q_RANK, BF16, "tile")], ROWS, nm("mla_qnorm"))[0]
    ckvn = _ew(_rms_gain_fn, [(ckv, True), (small["mla_kv_norm"], False)], [(MLA_KV_RANK, BF16, "tile")], ROWS, nm("mla_kvnorm"))[0]
    q_full = _mm(cqn, w["w_uq_t"], "nt", 512, 768, 384, name=nm("mla_q"))
    kv_full = _mm(ckvn, w["w_ukv_t"], "nt", 512, 1024, 128, BF16, name=nm("mla_kv"))
    nq = q_full.shape[1]
    roped = _rope(jnp.concatenate([q_full, kr, jnp.zeros_like(kr)], axis=1), rope_mla, nm("mla_rope"), out_dtype=BF16)
    q4 = roped[:, :nq].reshape(t, MLA_HEADS, MLA_NOPE + MLA_ROPE)
    qn = q4[:, :, :MLA_NOPE].transpose(1, 0, 2)
    rot = jnp.concatenate([q4[:, :, MLA_NOPE:].transpose(1, 0, 2), roped[None, :, nq:nq + MLA_ROPE]], axis=0)
    o_mla, lse_mla = _mla_attend(qn, rot, kv_full, nm("mla"))
    cat = jnp.concatenate([y_s5, y_ret, y_swa, o_mla.astype(BF16)], axis=1)
    if late is not None:
        w = {**w, **late(lse_mla)}
    mixed, h1 = _mm(cat, w["w_out"], "nn", 512, 1024, 2048, name=nm("out_proj"), epi=lambda acc, hh, gt: (acc, _gate_add_fn(hh, acc, gt)),
                    epi_ins=[h, gt1], epi_outs=[F32, F32])
    a2 = _ew(_norm_mod_fn, [(h1, True), (small["norm2_g"], False), (sc2, False), (sh2, False)], [(D_MODEL, BF16, "tile")], ROWS, nm("norm2"))[0]
    hid, act = _mm(a2, w["w1_t"], "nt", 1024, 1024, 2048, name=nm("mlp1"), epi=lambda acc: (acc, _relu2_fn(acc)), epi_outs=[F32, BF16])
    mo = _mm(act, w["w2"], "nn", 1024, 1024, 2048, name=nm("mlp2"))
    h2 = _ew(_gate_add_fn, [(h1, True), (mo, True), (gt2, False)], [(D_MODEL, F32, "tile")], ROWS, nm("res2"))[0]
    saved = dict(w=w, h=h, a1=a1, p_s5=p_s5, p_ret=p_ret, st_r=st_r, st_i=st_i, ypre=ypre, z=z, zz=zz, qk_ret=qk_ret, o_ret=o_ret,
                 qkv_swa=qkv_swa, o_swa=o_swa, lse_swa=lse_swa, cq=cq, ckv=ckv, cqn=cqn, ckvn=ckvn, qn=qn, rot=rot,
                 kv_full=kv_full, o_mla=o_mla, lse_mla=lse_mla, cat=cat, mixed=mixed, h1=h1, a2=a2, hid=hid, act=act, mo=mo)
    return h2, saved


def _layer_bwd(dh2, mod, w, small, rope, s, l, after_mlp=None, after_out=None):
    sh1, sc1, gt1, sh2, sc2, gt2 = mod
    rope_ret, rope_mla = rope
    t = dh2.shape[0]
    nm = lambda n: f"l{l}_{n}_bwd"
    gb, gs = {}, {}
    row = (D_MODEL, F32, "acc")
    dmo, dgt2 = _ew(lambda d, y, gt: (d * gt, jnp.sum(d * y, axis=0, keepdims=True)),
                    [(dh2, True), (s["mo"], True), (gt2, False)], [(D_MODEL, BF16, "tile"), row], ROWS, nm("res2"))
    dhid = _mm(dmo, w["w2"], "nt", 1024, 1024, 2048, name=nm("mlp2_x"), epi=lambda acc, x: (acc * 2.0 * jnp.maximum(x, 0.0),),
               epi_ins=[s["hid"]], epi_outs=[BF16])[0]
    gb["w2"] = _mm(s["act"], dmo, "tn", 1024, 1024, 4096, BF16, name=nm("mlp2_w"))
    da2 = _mm(dhid, w["w1_t"], "nn", 1024, 1024, 2048, name=nm("mlp1_x"))
    gb["w1_t"] = _mm(dhid, s["a2"], "tn", 1024, 1024, 4096, BF16, name=nm("mlp1_w"))
    if after_mlp is not None:
        gt1 = gt1 + after_mlp(gb)

    def norm_bwd(hh, g, sc, sh, da, dres):
        dh_, dg, dsc, dsh = _vjp_block(_norm_mod_fn, 4)(hh, g, sc, sh, da)
        return dh_ + dres, dg, dsc, dsh

    dh1, gs["norm2_g"], dsc2, dsh2 = _ew(norm_bwd, [(s["h1"], True), (small["norm2_g"], False), (sc2, False), (sh2, False), (da2, True), (dh2, True)],
                                         [(D_MODEL, F32, "tile"), row, row, row], ROWS, nm("norm2"))
    dmixed, dgt1 = _ew(lambda d, y, gt: (d * gt, jnp.sum(d * y, axis=0, keepdims=True)),
                       [(dh1, True), (s["mixed"], True), (gt1, False)], [(D_MODEL, BF16, "tile"), row], ROWS, nm("res1"))
    dcat = _mm(dmixed, w["w_out"], "nt", 512, 1024, 2048, name=nm("out_proj_x"))
    gb["w_out"] = _mm(s["cat"], dmixed, "tn", 1024, 1024, 4096, BF16, name=nm("out_proj_w"))
    if after_out is not None:
        small = {**small, "s5_glu_b": small["s5_glu_b"] + after_out(gb)}
    dy_s5, dy_ret, dy_swa, dy_mla = (dcat[:, i * GROUP_WIDTH:(i + 1) * GROUP_WIDTH] for i in range(4))
    b3, c3, dskip, a_r, a_i = small["s5"]
    gw = (GROUP_WIDTH, F32, "tile")
    gacc = (GROUP_WIDTH, F32, "acc")
    dz_a, dzz, gs["s5_glu_b"] = _ew(_vjp_block(_s5_glu_fn, 3), [(s["z"], True), (s["zz"], True), (small["s5_glu_b"], False), (dy_s5, True)],
                                    [gw, gw, gacc], ROWS, nm("s5_glu"))
    dz_b = _mm(dzz, w["glu_w"], "nt", 512, 512, 512, name=nm("s5_zz_x"))
    gb["glu_w"] = _mm(s["z"], dzz, "tn", 512, 512, 1024, BF16, name=nm("s5_zz_w"))

    def act_bwd(ypre, u, dsk, dza, dzb):
        return _vjp_block(_s5_act_fn, 3)(ypre, u, dsk, dza + dzb)

    dypre, du_a, g_dskip = _ew(act_bwd, [(s["ypre"], True), (s["p_s5"], True), (dskip, False), (dz_a, True), (dz_b, True)],
                               [gw, gw, gacc], ROWS, nm("s5_act"))
    ch, st = S5_PACK * S5_CH, S5_PACK * S5_STATE
    g_c3 =jnp.concatenate([_mm_blocks_tn(s["st_r"], dypre, st, ch, lambda j: j, nm("s5_y_w_re")),
                            _mm_blocks_tn(s["st_i"], dypre, st, ch, lambda j: j, nm("s5_y_w_im"))], axis=0)
    dbu_r, dbu_i, g_ar, g_ai = _s5_scan_bwd(dypre, c3, s["st_r"], s["st_i"], a_r, a_i, nm("s5_scan"))
    du_b = _mm_blocks(dbu_r, b3[:S5_BLOCKS], "nt", 512, pair=(dbu_i, b3[S5_BLOCKS:]), name=nm("s5_bu_x"))
    g_b3 = jnp.concatenate([_mm_blocks_tn(s["p_s5"], dbu_r, ch, st, lambda j: j, nm("s5_bu_w_re")),
                            _mm_blocks_tn(s["p_s5"], dbu_i, ch, st, lambda j: j, nm("s5_bu_w_im"))], axis=0)
    gs["s5"] = (g_b3, g_c3, g_dskip, g_ar, g_ai)
    dqk_rot, dk_rot, dv_ret, dg_ret = _ret_bwd(s["qk_ret"], s["p_ret"], s["o_ret"], dy_ret, small["ret_lgam"], nm("ret"))
    dqk = _rope(_merge_heads(jnp.concatenate([dqk_rot, dk_rot], axis=0)), rope_ret, nm("ret_rope"), inverse=True, out_dtype=BF16)
    dqkv_swa, gs["swa_sinks"] = _swa_bwd(s["qkv_swa"], s["o_swa"], s["lse_swa"], _split_heads(dy_swa, SWA_HEADS), small["swa_sinks"], nm("swa"))
    dqn, dqr, dkn, dv_mla, dkr_heads = _mla_attend_bwd(s["qn"], s["rot"], s["kv_full"], s["lse_mla"], s["o_mla"], dy_mla, nm("mla_att"))
    dkv_full = jnp.stack([dkn.reshape(t, MLA_HEADS, MLA_NOPE), dv_mla.reshape(t, MLA_HEADS, MLA_V)], axis=2).reshape(t, 2 * MLA_HEADS * MLA_NOPE)
    dkr_rot = _ew(lambda a, b, c, d: a + b + c + d, [(dkr_heads[i], True) for i in range(MLA_HEADS)], [(MLA_ROPE, F32, "tile")], ROWS, nm("mla_dkr"))[0]
    nq = MLA_HEADS * (MLA_NOPE + MLA_ROPE)
    dq_rot = jnp.concatenate([dqn.transpose(1, 0, 2), dqr.transpose(1, 0, 2)], axis=2).reshape(t, nq)
    droped = _rope(jnp.concatenate([dq_rot, dkr_rot, jnp.zeros_like(dkr_rot)], axis=1), rope_mla, nm("mla_rope"), inverse=True, out_dtype=BF16)
    dq_full, dkr = droped[:, :nq], droped[:, nq:nq + MLA_ROPE]
    dcqn = _mm(dq_full, w["w_uq_t"], "nn", 512, 384, 768, name=nm("mla_q_x"))
    gb["w_uq_t"] = _mm(dq_full, s["cqn"], "tn", 768, 384, 1024, BF16, name=nm("mla_q_w"))
    dckvn = _mm(dkv_full, w["w_ukv_t"], "nn", 512, 128, 1024, name=nm("mla_kv_x"))
    gb["w_ukv_t"] = _mm(dkv_full, s["ckvn"], "tn", 1024, 128, 1024, BF16, name=nm("mla_kv_w"))
    dcq, gs["mla_q_norm"] = _ew(_vjp_block(_rms_gain_fn, 2), [(s["cq"], True), (small["mla_q_norm"], False), (dcqn, True)],
                                [(MLA_Q_RANK, BF16, "tile"), (MLA_Q_RANK, F32, "acc")], ROWS, nm("mla_qnorm"))
    dckv, gs["mla_kv_norm"] = _ew(_vjp_block(_rms_gain_fn, 2), [(s["ckv"], True), (small["mla_kv_norm"], False), (dckvn, True)],
                                  [(MLA_KV_RANK, BF16, "tile"), (MLA_KV_RANK, F32, "acc")], ROWS, nm("mla_kvnorm"))
    du = _ew(lambda a, b: a + b, [(du_a, True), (du_b, True)], [(GROUP_WIDTH, BF16, "tile")], ROWS, nm("s5_du"))[0]
    bf = lambda a: a.astype(BF16)
    dproj = jnp.concatenate([du, bf(dqk), bf(dv_ret), bf(dg_ret), bf(_merge_heads(dqkv_swa)), bf(dcq), bf(dckv), bf(dkr)], axis=1)
    da1 = _mm(dproj, w["w_in_t"], "nn", 512, 1024, N_IN, name=nm("proj_x"))
    gb["w_in_t"] = _mm(dproj, s["a1"], "tn", N_IN, 512, 2048, BF16, name=nm("proj_w"))
    dh, gs["norm1_g"], dsc1, dsh1 = _ew(norm_bwd, [(s["h"], True), (small["norm1_g"], False), (sc1, False), (sh1, False), (da1, True), (dh1, True)],
                                        [(D_MODEL, F32, "tile"), row, row, row], ROWS, nm("norm1"))
    dmod = jnp.concatenate([dsh1, dsc1, dgt1, dsh2, dsc2, dgt2], axis=1)
    return dh, gb, gs, dmod


BIG = ("w_in_t", "w1_t", "w_uq_t", "w_ukv_t", "w_out", "w2", "glu_w")
MLP_BIG = ("w1_t", "w2")
OUT_BIG = ("w_out",)
LATE_BIG = ("w_out", "w1_t", "w2")
S5_NAMES = ("s5_lambda_re", "s5_lambda_im", "s5_log_dt", "s5_b_re", "s5_b_im", "s5_c_re", "s5_c_im", "s5_d")


def kernel(x, c, norm1_g, norm2_g, ada_w, ada_b, w_in, s5_lambda_re, s5_lambda_im, s5_log_dt, s5_b_re, s5_b_im, s5_c_re, s5_c_im, s5_d, s5_glu_w, s5_glu_b, swa_sinks, mla_q_norm, mla_kv_norm, mla_w_uq, mla_w_ukv, w_out, mlp_w1, mlp_w2, final_norm_g, loss_target, m_norm1_g, m_norm2_g, m_ada_w, m_ada_b, m_w_in, m_s5_lambda_re, m_s5_lambda_im, m_s5_log_dt, m_s5_b_re, m_s5_b_im, m_s5_c_re, m_s5_c_im, m_s5_d, m_s5_glu_w, m_s5_glu_b, m_swa_sinks, m_mla_q_norm, m_mla_kv_norm, m_mla_w_uq, m_mla_w_ukv, m_w_out, m_mlp_w1, m_mlp_w2, m_final_norm_g, v_norm1_g, v_norm2_g, v_ada_w, v_ada_b, v_w_in, v_s5_lambda_re, v_s5_lambda_im, v_s5_log_dt, v_s5_b_re, v_s5_b_im, v_s5_c_re, v_s5_c_im, v_s5_d, v_s5_glu_w, v_s5_glu_b, v_swa_sinks, v_mla_q_norm, v_mla_kv_norm, v_mla_w_uq, v_mla_w_ukv, v_w_out, v_mlp_w1, v_mlp_w2, v_final_norm_g):
    names = ["norm1_g", "norm2_g", "ada_w", "ada_b", "w_in", "s5_lambda_re", "s5_lambda_im", "s5_log_dt", "s5_b_re", "s5_b_im",
             "s5_c_re", "s5_c_im", "s5_d", "s5_glu_w", "s5_glu_b", "swa_sinks", "mla_q_norm", "mla_kv_norm", "mla_w_uq",
             "mla_w_ukv", "w_out", "mlp_w1", "mlp_w2", "final_norm_g"]
    env = locals()
    wts = {n: env[n] for n in names}
    mom = {n: env["m_" + n] for n in names}
    var = {n: env["v_" + n] for n in names}
    t = x.shape[1]
    me = 4 * lax.axis_index("x") + 2 * lax.axis_index("y") + lax.axis_index("c")
    ret_lgam = jnp.log1p(-(2.0 ** (-5.0 - jnp.arange(RET_HEADS, dtype=F32))))

    tr = lambda a: a.transpose(0, 2, 1)
    shard = {"w_in_t": tr(w_in), "w1_t": tr(mlp_w1), "w_uq_t": tr(mla_w_uq), "w_ukv_t": tr(mla_w_ukv),
             "w_out": w_out, "w2": mlp_w2, "glu_w": s5_glu_w}
    to_send = [{k: shard[k][l].astype(BF16) for k in BIG} for l in range(DEPTH)]
    as_rows = lambda keys, arrs: {k: a.reshape(-1, shard[k].shape[2]) for k, a in zip(keys, arrs)}
    first = [k for k in BIG if k not in LATE_BIG]
    own_slot = lambda k: 4 * lax.axis_index("x") + 2 * lax.axis_index("y") + lax.axis_index("c")

    def gather_start(arrs, tag, after=None):
        return _push_start(arrs, [jax.ShapeDtypeStruct((N_DEV,) + a.shape, a.dtype) for a in arrs], GATHER_PEERS,
                           lambda ref, k: ref, own_slot, f"gather_weights_{tag}_start", after)

    def gather_finish(started, after, tag):
        sent, landed = _push_wait(started[0], started[1], started[2], started[3], after, GATHER_PEERS, lambda ref, k: ref, own_slot,
                                  f"gather_weights_{tag}_wait")
        with_own = [lax.dynamic_update_index_in_dim(full, own, me, 0) for full, own in zip(landed, sent)]
        return _pass_to_sibling(with_own, f"gather_weights_{tag}_pass")

    gather_first = gather_start([to_send[0][k] for k in first] + [c], "first")
    zero = gather_first[4][0, 0]
    rope = _rope_tables(t, zero)
    smalls, s5_pulls = [], []
    for l in range(DEPTH):
        s5_ops, pull = jax.vjp(_s5_prep, *[wts[n][l] + zero if n == "s5_log_dt" else wts[n][l] for n in S5_NAMES])
        s5_pulls.append(pull)
        smalls.append(dict(norm1_g=norm1_g[l][None], norm2_g=norm2_g[l][None], s5=s5_ops, s5_glu_b=s5_glu_b[l][None],
                           swa_sinks=swa_sinks[l], mla_q_norm=mla_q_norm[l][None], mla_kv_norm=mla_kv_norm[l][None], ret_lgam=ret_lgam))
    setup_done = (smalls[0]["s5"][3][0, 0] + smalls[1]["s5"][3][0, 0] + rope[0][0][0, 0] + rope[1][0][0, 0]).reshape(1, 1)
    gathered = gather_finish(gather_first, setup_done, "first")
    c_all = gathered[-1].reshape(N_DEV, D_MODEL)
    big = [as_rows(first, gathered[:len(first)]), None]

    c_act = _whole(lambda v: v * jax.nn.sigmoid(v), [c_all], [((N_DEV, D_MODEL), F32)], "cond_silu")[0]
    c_pad = jnp.concatenate([c_act, jnp.zeros((128 - N_DEV, D_MODEL), F32)], axis=0)
    cols = ada_w.shape[2]
    mod_part = [_mm(c_pad, ada_w[l], "nn", 128, cols, 512, name=f"l{l}_mod")[:N_DEV] for l in range(DEPTH)]
    mod_all = _all_gather([jnp.stack(mod_part)], "gather_mod")[0]
    gather0 = gather_start([to_send[0][k] for k in LATE_BIG], "l0", after=mod_all)
    gather1 = gather_start([to_send[1][k] for k in BIG], "l1", after=mod_all)
    mod_rows = lax.dynamic_index_in_dim(mod_all, me, axis=2, keepdims=False)
    mods = []
    for l in range(DEPTH):
        row = mod_rows[:, l].reshape(1, 6 * D_MODEL) + ada_b[l][None]
        if l == 0:
            row = row + (gather0[4][0, 0] + gather1[4][0, 0])
        mods.append([row[:, i * D_MODEL:(i + 1) * D_MODEL] for i in range(6)])

    h = x[0]
    saved = []
    for l in range(DEPTH):
        if l == 0:
            late = lambda after: as_rows(LATE_BIG, gather_finish(gather0, after, "l0"))
        else:
            big[1] = as_rows(BIG, gather_finish(gather1, h, "l1"))
            late = None
        h, s = _layer_fwd(h, mods[l], big[l], smalls[l], rope, l, late)
        big[l] = s.pop("w")
        saved.append(s)

    fg = final_norm_g[None]
    tgt = loss_target[0]
    def final_both(hh, g, tg):
        part, pull = jax.vjp(_final_fn, hh, g, tg)
        dh_, dg, _ = pull(jnp.ones((1, 1), F32))
        return dh_, dg, part

    dh, g_final, loss_local = _ew(final_both, [(h, True), (fg, False), (tgt, True)],
                                  [(D_MODEL, F32, "tile"), (D_MODEL, F32, "acc"), (1, F32, "acc")], ROWS, "loss")
    loss = lax.psum(loss_local[0, 0], ("x", "y", "c"))

    core, chip = lax.axis_index("c"), 2 * lax.axis_index("x") + lax.axis_index("y")

    def core_stage(g_layer, keys, tag):
        g_list = [g_layer[k].reshape(N_DEV, -1, g_layer[k].shape[1]) for k in keys]
        got = _scatter_core(g_list, f"scatter_core_{tag}")
        return [_pick_sum(g, o, core, lambda q, c_: 2 * q + c_, BF16, f"{tag}_core_sum_{k}") for k, g, o in zip(keys, g_list, got)]

    def their_block(ref, k):
        x_, y_ = lax.axis_index("x"), lax.axis_index("y")
        dx, dy, _ = CHIP_PEERS[k]
        return ref.at[2 * (1 - x_ if dx else x_) + (1 - y_ if dy else y_)]

    def chips_start(halves, tag, after=None):
        return _push_start(halves, [jax.ShapeDtypeStruct((3,) + a.shape[1:], a.dtype) for a in halves], CHIP_PEERS,
                           their_block, lambda k: k, f"scatter_chips_{tag}_start", after)

    def chips_wait(started, after, tag):
        return _push_wait(started[0], started[1], started[2], started[3], after, CHIP_PEERS, their_block, lambda k: k, f"scatter_chips_{tag}_wait")

    def sibling_block(ref, q):
        return ref.at[2 * q + 1 - lax.axis_index("c")]

    g_small, dmods = [None] * DEPTH, [None] * DEPTH
    dh, g_big1, g_small[1], dmods[1] = _layer_bwd(dh, mods[1], big[1], smalls[1], rope, saved[1], 1)
    g_list1 = [g_big1[k].reshape(N_DEV, -1, g_big1[k].shape[1]) for k in BIG]
    core1 = _push_start(g_list1, [jax.ShapeDtypeStruct((4,) + g.shape[1:], g.dtype) for g in g_list1], CORE_PEERS,
                        sibling_block, lambda q: q, "scatter_core_l1_start")
    mods0 = [m + core1[4][0, 0] for m in mods[0]]
    early = {}

    def after_mlp(gb):
        g_mine, got = _push_wait(core1[0], core1[1], core1[2], core1[3], gb["w1_t"], CORE_PEERS, sibling_block, lambda q: q,
                                 "scatter_core_l1_wait")
        halves = [_pick_sum(g, o, core, lambda q, c_: 2 * q + c_, BF16, f"l1_core_sum_{k}") for k, g, o in zip(BIG, g_mine, got)]
        early["l1"] = chips_start(halves, "l1")
        g_list = [gb[k].reshape(N_DEV, -1, gb[k].shape[1]) for k in MLP_BIG]
        early["mlp_core"] = _push_start(g_list, [jax.ShapeDtypeStruct((4,) + g.shape[1:], g.dtype) for g in g_list], CORE_PEERS,
                                        sibling_block, lambda q: q, "scatter_core_l0_mlp_start")
        return early["l1"][4][0, 0] + early["mlp_core"][4][0, 0]

    def after_out(gb):
        started = early["mlp_core"]
        g_mine, got = _push_wait(started[0], started[1], started[2], started[3], gb["w_out"], CORE_PEERS, sibling_block, lambda q: q,
                                 "scatter_core_l0_mlp_wait")
        halves = [_pick_sum(g, o, core, lambda q, c_: 2 * q + c_, BF16, f"l0_mlp_core_sum_{k}") for k, g, o in zip(MLP_BIG, g_mine, got)]
        early["mlp"] = chips_start(halves, "l0_mlp")
        early["out"] = chips_start(core_stage(gb, OUT_BIG, "l0_out"), "l0_out")
        return early["mlp"][4][0, 0] + early["out"][4][0, 0]

    dh, g_big0, g_small[0], dmods[0] = _layer_bwd(dh, mods0, big[0], smalls[0], rope, saved[0], 0, after_mlp=after_mlp, after_out=after_out)
    grad_x = dh[None]
    rest = [k for k in BIG if k not in MLP_BIG + OUT_BIG]
    halves_rest = core_stage(g_big0, rest, "l0_rest")

    small_parts = []
    for l in range(DEPTH):
        gs = g_small[l]
        s5g = s5_pulls[l](gs["s5"])
        small_parts += [gs["norm1_g"], gs["norm2_g"], *s5g, gs["s5_glu_b"], gs["swa_sinks"], gs["mla_q_norm"], gs["mla_kv_norm"]]
    small_parts += [g_final, *dmods]
    sizes = [int(np.prod(p.shape)) for p in small_parts]
    flat = jnp.concatenate([p.reshape(1, -1) for p in small_parts], axis=1)
    pad = (-flat.shape[1]) % 8192
    flat = jnp.pad(flat, ((0, 0), (0, pad)))
    flat_all = _all_gather([flat], "gather_small_grads")[0].reshape(N_DEV, -1)
    summed = _sum8(flat_all, "sum_small_grads")
    chips_rest = chips_start(halves_rest, "l0_rest", after=summed)
    rest_token = chips_rest[4]
    halves1, landed1 = chips_wait(early["l1"], rest_token, "l1")
    halves_mlp, landed_mlp = chips_wait(early["mlp"], rest_token, "l0_mlp")
    halves_out, landed_out = chips_wait(early["out"], rest_token, "l0_out")
    terms = {(1, k): pair for k, pair in zip(BIG, zip(halves1, landed1))}
    terms.update({(0, k): pair for k, pair in zip(MLP_BIG, zip(halves_mlp, landed_mlp))})
    terms.update({(0, k): pair for k, pair in zip(OUT_BIG, zip(halves_out, landed_out))})
    chip_sum = lambda l, k: _pick_sum(*terms[l, k], chip, lambda q, m_: m_, F32, f"l{l}_chip_sum_{k}")
    pieces, off = [], 0
    for sz in sizes:
        pieces.append(summed[0, off:off + sz])
        off += sz
    small_names = ["norm1_g", "norm2_g", *S5_NAMES, "s5_glu_b", "swa_sinks", "mla_q_norm", "mla_kv_norm"]
    per_layer = len(small_names)
    grads = {}
    for i, n in enumerate(small_names):
        grads[n] = jnp.stack([pieces[l * per_layer + i].reshape(wts[n].shape[1:]) for l in range(DEPTH)])
    grads["final_norm_g"] = pieces[DEPTH * per_layer]
    grads["ada_b"] = jnp.stack([pieces[DEPTH * per_layer + 1 + l] for l in range(DEPTH)])

    mod_off = sum(sizes[:DEPTH * per_layer + 1])
    dmod_all = flat_all[:, mod_off:mod_off + DEPTH * 6 * D_MODEL].reshape(N_DEV, DEPTH, N_DEV, cols)
    dmod_mine = lax.dynamic_index_in_dim(dmod_all, me, axis=2, keepdims=False).transpose(1, 0, 2)
    dmod_pad = jnp.concatenate([dmod_mine, jnp.zeros((DEPTH, 128 - N_DEV, cols), F32)], axis=1)
    grads["ada_w"] = jnp.stack([_mm(c_pad, dmod_pad[l], "tn", 512, cols, 128, name=f"l{l}_ada_w_grad") for l in range(DEPTH)])

    out_g, out_d, out_m, out_v = dict(grads), {}, {}, {}
    orig = {"w_in_t": "w_in", "w1_t": "mlp_w1", "w_uq_t": "mla_w_uq", "w_ukv_t": "mla_w_ukv", "w_out": "w_out", "w2": "mlp_w2", "glu_w": "s5_glu_w"}

    def update_big(keys):
        for k in keys:
            n = orig[k]
            out_g[n] = jnp.stack([chip_sum(l, k) for l in range(DEPTH)])
            if k.endswith("_t"):
                out_g[n] = tr(out_g[n])
            out_d[n], out_m[n], out_v[n] = _adamw_shard(wts[n], out_g[n], mom[n], var[n], f"adamw_{n}")

    update_big(MLP_BIG + OUT_BIG)
    out_d["ada_w"], out_m["ada_w"], out_v["ada_w"] = _adamw_shard(wts["ada_w"], out_g["ada_w"], mom["ada_w"], var["ada_w"], "adamw_ada_w")
    small_all = small_names + ["ada_b", "final_norm_g"]
    ds, ms, vs = _adamw_small([wts[n] for n in small_all], [grads[n] for n in small_all], [mom[n] for n in small_all],
                              [var[n] for n in small_all], "adamw_small")
    for n, d, m_, v_ in zip(small_all, ds, ms, vs):
        out_d[n], out_m[n], out_v[n] = d, m_, v_
    halves_rest, landed_rest = chips_wait(chips_rest, out_d["ada_w"], "l0_rest")
    terms.update({(0, k): pair for k, pair in zip(rest, zip(halves_rest, landed_rest))})
    update_big(rest)
    return (loss, grad_x, *[out_g[n] for n in names], *[out_d[n] for n in names], *[out_m[n] for n in names], *[out_v[n] for n in names])
```

```python
import functools
import math

import numpy as np
import jax
import jax.numpy as jnp
from jax import lax
from jax.experimental import pallas as pl
from jax.experimental.pallas import tpu as pltpu

F32 = jnp.float32
BF16 = jnp.bfloat16
_MXU_DTYPE = jnp.bfloat16

N_DEV = 8
D_MODEL = 2048
DEPTH = 2
GROUP_WIDTH = 512
D_FF = 8192
S5_CH, S5_GROUPS, S5_STATE = 16, 32, 64
S5_WIDTH = S5_GROUPS * S5_STATE
S5_PACK = 8
S5_BLOCKS = S5_GROUPS // S5_PACK
RET_HEADS, RET_QK, RET_V, RET_CHUNK = 4, 64, 128, 128
SWA_HD, SWA_HEADS, SWA_KV_HEADS, WINDOW = 64, 8, 2, 128
MLA_HEADS, MLA_Q_RANK, MLA_KV_RANK, MLA_NOPE, MLA_ROPE, MLA_V = 4, 384, 128, 128, 64, 128
ROPE_BASE = 10000.0
EPS = 1e-6
NEG = -1e30
N_IN = 3392
ADAM_LR, ADAM_B1, ADAM_B2, ADAM_EPS, ADAM_WD, ADAM_STEP = 0.001, 0.9, 0.999, 1e-08, 0.01, 10

VMEM_LIMIT_BYTES = 52 * 1024 * 1024
MESH_ID = pl.DeviceIdType.MESH
_ANY = pl.BlockSpec(memory_space=pl.ANY)
_SMEM = pl.BlockSpec(memory_space=pltpu.SMEM)


def _params(sem):
    return pltpu.CompilerParams(dimension_semantics=sem, vmem_limit_bytes=VMEM_LIMIT_BYTES)


_DIMS = {"nn": (((1,), (0,)), ((), ())), "nt": (((1,), (1,)), ((), ())), "tn": (((0,), (0,)), ((), ()))}


def _dot(a, b, mode="nn"):
    return lax.dot_general(a.astype(_MXU_DTYPE), b.astype(_MXU_DTYPE), _DIMS[mode], preferred_element_type=F32)


def _mm(a, b, mode, tm, tn, tk, out_dtype=F32, name="mm", b_off=0, n=None, pair=None, epi=None, epi_ins=(), epi_outs=None):
    if mode == "tn":
        kdim, m = a.shape
    else:
        m, kdim = a.shape
    if n is None:
        n = b.shape[0] if mode == "nt" else b.shape[1]
    tm, tn, tk = min(tm, m), min(tn, n), min(tk, kdim)
    assert m % tm == 0 and n % tn == 0 and kdim % tk == 0, (name, a.shape, b.shape, tm, tn, tk)
    nk = kdim // tk
    a_spec = pl.BlockSpec((tk, tm), lambda i, j, k: (k, i)) if mode == "tn" else pl.BlockSpec((tm, tk), lambda i, j, k: (i, k))
    if mode == "nt":
        b_spec = pl.BlockSpec((tn, tk), lambda i, j, k: (j + b_off, k))
    else:
        b_spec = pl.BlockSpec((tk, tn), lambda i, j, k: (k, j + b_off))
    o_spec = pl.BlockSpec((tm, tn), lambda i, j, k: (i, j))
    n_mm = 2 if pair is None else 4
    out_dtypes = [out_dtype] if epi is None else list(epi_outs)

    def body(*refs):
        ins, extra = refs[:n_mm], refs[n_mm:n_mm + len(epi_ins)]
        outs = refs[n_mm + len(epi_ins):n_mm + len(epi_ins) + len(out_dtypes)]
        part = _dot(ins[0][...], ins[1][...], mode)
        if pair is not None:
            part = part + _dot(ins[2][...], ins[3][...], mode)

        def finish(acc):
            vals = (acc,) if epi is None else epi(acc, *[r[...] for r in extra])
            for o_ref, v, dt in zip(outs, vals, out_dtypes):
                o_ref[...] = v.astype(dt)

        if nk == 1:
            finish(part)
        else:
            acc_ref = refs[-1]
            k = pl.program_id(2)

            @pl.when(k == 0)
            def _():
                acc_ref[...] = part

            @pl.when(k > 0)
            def _():
                acc_ref[...] += part

            @pl.when(k == nk - 1)
            def _():
                finish(acc_ref[...])

    operands = [a, b] + ([] if pair is None else list(pair)) + list(epi_ins)
    res = pl.pallas_call(
        body, name=name, grid=(m // tm, n // tn, nk),
        in_specs=[a_spec, b_spec] * (n_mm // 2) + [o_spec] * len(epi_ins),
        out_specs=[o_spec] * len(out_dtypes), out_shape=[jax.ShapeDtypeStruct((m, n), dt) for dt in out_dtypes],
        scratch_shapes=[] if nk == 1 else [pltpu.VMEM((tm, tn), F32)],
        compiler_params=_params(("parallel", "parallel", "arbitrary")),
    )(*operands)
    return res[0] if epi is None else res


def _mm_blocks(a, b, mode, tm, a_of=None, pair=None, name="mm_blocks"):
    a_of = a_of or (lambda j: j)
    m = a.shape[0]
    nj, kb, nb = b.shape
    a_w, o_w = (kb, nb) if mode == "nn" else (nb, kb)
    tm = min(tm, m)
    a_spec = pl.BlockSpec((tm, a_w), lambda i, j: (i, a_of(j)))
    b_spec = pl.BlockSpec((1, kb, nb), lambda i, j: (j, 0, 0))
    n_in = 2 if pair is None else 4

    def body(*refs):
        acc = _dot(refs[0][...], refs[1][0], mode)
        if pair is not None:
            acc = acc + _dot(refs[2][...], refs[3][0], mode)
        refs[n_in][...] = acc

    operands = [a, b] + ([] if pair is None else list(pair))
    return pl.pallas_call(
        body, name=name, grid=(m // tm, nj), in_specs=[a_spec, b_spec] * (n_in // 2),
        out_specs=pl.BlockSpec((tm, o_w), lambda i, j: (i, j)), out_shape=jax.ShapeDtypeStruct((m, nj * o_w), F32),
        compiler_params=_params(("parallel", "parallel")),
    )(*operands)


def _mm_blocks_tn(a, b, x, y, b_of, name):
    kdim = a.shape[0]
    nj = a.shape[1] // x

    def body(a_ref, b_ref, o_ref):
        o_ref[0] = _dot(a_ref[...], b_ref[...], "tn")

    return pl.pallas_call(
        body, name=name, grid=(nj,), in_specs=[pl.BlockSpec((kdim, x), lambda j: (0, j)), pl.BlockSpec((kdim, y), lambda j: (0, b_of(j)))],
        out_specs=pl.BlockSpec((1, x, y), lambda j: (j, 0, 0)), out_shape=jax.ShapeDtypeStruct((nj, x, y), F32),
        compiler_params=_params(("parallel",)),
    )(a, b)


SUBLANES = 8


def _row_tile(rows, target, mult=SUBLANES):
    best = None
    for cand in range(mult, min(rows, target) + 1, mult):
        if rows % cand == 0:
            best = cand
    return best or rows


def _ew(fn, ins, outs, tt, name):
    t = [a.shape[0] for a, tiled in ins if tiled][0]
    tt = _row_tile(t, tt)
    n_in = len(ins)
    in_specs = [pl.BlockSpec((tt, a.shape[1]), lambda i: (i, 0)) if tiled else pl.BlockSpec(a.shape, lambda i: (0, 0))
                for a, tiled in ins]
    out_specs, out_shapes = [], []
    for w, dt, kind in outs:
        if kind == "tile":
            out_specs.append(pl.BlockSpec((tt, w), lambda i: (i, 0)))
            out_shapes.append(jax.ShapeDtypeStruct((t, w), dt))
        else:
            out_specs.append(pl.BlockSpec((1, w), lambda i: (0, 0)))
            out_shapes.append(jax.ShapeDtypeStruct((1, w), F32))
    has_acc = any(kind == "acc" for _, _, kind in outs)

    def body(*refs):
        vals = fn(*[r[...] for r in refs[:n_in]])
        if not isinstance(vals, (tuple, list)):
            vals = (vals,)
        i = pl.program_id(0)
        for o_ref, v, (w, dt, kind) in zip(refs[n_in:], vals, outs):
            if kind == "tile":
                o_ref[...] = v.astype(dt)
            else:
                @pl.when(i == 0)
                def _(o_ref=o_ref, v=v):
                    o_ref[...] = v.astype(F32)

                @pl.when(i > 0)
                def _(o_ref=o_ref, v=v):
                    o_ref[...] += v.astype(F32)

    res = pl.pallas_call(
        body, name=name, grid=(t // tt,), in_specs=in_specs, out_specs=out_specs, out_shape=out_shapes,
        compiler_params=_params(("arbitrary" if has_acc else "parallel",)),
    )(*[a for a, _ in ins])
    return res


def _whole(fn, ins, outs, name):
    def body(*refs):
        vals = fn(*[r[...] for r in refs[:len(ins)]])
        if not isinstance(vals, (tuple, list)):
            vals = (vals,)
        for o_ref, v in zip(refs[len(ins):], vals):
            o_ref[...] = v.astype(o_ref.dtype)

    return pl.pallas_call(body, name=name, out_shape=[jax.ShapeDtypeStruct(s, dt) for s, dt in outs])(*ins)


def _rms(x):
    return x * lax.rsqrt(jnp.mean(x * x, axis=-1, keepdims=True) + EPS)


def _norm_mod_fn(h, g, sc, sh):
    return (_rms(h) * g) * (1.0 + sc) + sh


def _rms_gain_fn(x, g):
    return _rms(x) * g


def _gate_add_fn(h, y, gt):
    return h + gt * y


def _relu2_fn(x):
    return jnp.square(jnp.maximum(x, 0.0))


def _s5_act_fn(ypre, u, dskip):
    return jax.nn.gelu(ypre + dskip * u)


def _s5_glu_fn(z, zz, b):
    return z * jax.nn.sigmoid(zz + b)


def _ret_gate_fn(o, g):
    return _rms(o) * (g * jax.nn.sigmoid(g))


def _final_fn(h, g, tgt):
    err = _rms(h) * g - tgt
    return 0.5 * jnp.sum(jnp.mean(err * err, axis=-1, keepdims=True), axis=0, keepdims=True)


def _vjp_block(fn, n_args):
    def bwd(*vals):
        _, pull = jax.vjp(fn, *vals[:n_args])
        return pull(vals[n_args])
    return bwd


def _rope_tables(t, zero=0.0):
    d = RET_QK
    inv = ROPE_BASE ** (-jnp.arange(0, d, 2, dtype=F32) / d)
    ang = (jnp.arange(t, dtype=F32) + zero)[:, None] * inv[None, :]
    cos, sin = jnp.cos(ang), jnp.sin(ang)
    cos2, sin2 = jnp.concatenate([cos, cos], -1), jnp.concatenate([-sin, sin], -1)
    ret = (jnp.tile(cos2, (1, 8)), jnp.tile(sin2, (1, 8)))
    one, zero = jnp.ones((t, MLA_NOPE), F32), jnp.zeros((t, MLA_NOPE), F32)
    mla_c = jnp.concatenate([jnp.tile(jnp.concatenate([one, cos2], -1), (1, MLA_HEADS)), cos2, one[:, :d]], -1)
    mla_s = jnp.concatenate([jnp.tile(jnp.concatenate([zero, sin2], -1), (1, MLA_HEADS)), sin2, zero[:, :d]], -1)
    return ret, (mla_c, mla_s)


def _rope_fn(x, c, s, sign):
    w = x.shape[1]
    lane = lax.broadcasted_iota(jnp.int32, x.shape, 1)
    swapped = jnp.where((lane & 63) < 32, pltpu.roll(x, w - 32, 1), pltpu.roll(x, 32, 1))
    return x * c + swapped * (sign * s)


def _rope(x, tables, name, inverse=False, out_dtype=F32):
    c, s = tables
    fn = functools.partial(_rope_fn, sign=-1.0 if inverse else 1.0)
    return _ew(fn, [(x, True), (c, True), (s, True)], [(x.shape[1], out_dtype, "tile")], ROWS, name)[0]


SCAN_ROWS, SCAN_LANES = 256, 512


def _cmul(ar, ai, br, bi):
    return ar * br - ai * bi, ar * bi + ai * br


def _group_powers(ar, ai, reverse):
    shape = (SUBLANES, ar.shape[1])
    row = lax.broadcasted_iota(jnp.int32, shape, 0)
    pr, pi = ar, ai
    out_r, out_i = jnp.zeros(shape, F32), jnp.zeros(shape, F32)
    for e in range(1, SUBLANES + 1):
        hit = row == (SUBLANES - e if reverse else e - 1)
        out_r, out_i = jnp.where(hit, pr, out_r), jnp.where(hit, pi, out_i)
        if e < SUBLANES:
            pr, pi = _cmul(pr, pi, ar, ai)
    return out_r, out_i


def _scan_chunk(in_r_ref, in_i_ref, out_r_ref, out_i_ref, ar, ai, cr, ci, reverse, visit=None):
    rows, lanes = in_r_ref.shape
    sub = lax.broadcasted_iota(jnp.int32, (SUBLANES, lanes), 0)
    edge_r, edge_i = _group_powers(ar, ai, reverse)
    steps, pr, pi, k = [], ar, ai, 1
    while k < SUBLANES:
        steps.append((k, pr, pi))
        pr, pi = _cmul(pr, pi, pr, pi)
        k *= 2
    groups = range(rows // SUBLANES)
    for g in (reversed(groups) if reverse else groups):
        sl = slice(g * SUBLANES, (g + 1) * SUBLANES)
        xr, xi = in_r_ref[sl, :], in_i_ref[sl, :]
        for k, pr, pi in steps:
            shift = SUBLANES - k if reverse else k
            keep = sub < SUBLANES - k if reverse else sub >= k
            tr, ti = _cmul(pr, pi, pltpu.roll(xr, shift, 0), pltpu.roll(xi, shift, 0))
            xr, xi = xr + jnp.where(keep, tr, 0.0), xi + jnp.where(keep, ti, 0.0)
        tr, ti = _cmul(edge_r, edge_i, cr, ci)
        xr, xi = xr + tr, xi + ti
        out_r_ref[sl, :] = xr
        out_i_ref[sl, :] = xi
        if visit is not None:
            visit(sl, xr, xi, cr, ci)
        edge = slice(0, 1) if reverse else slice(SUBLANES - 1, SUBLANES)
        cr, ci = xr[edge, :], xi[edge, :]
    return cr, ci


def _s5_scan_specs(rows, row_block):
    assert SCAN_LANES == S5_PACK * S5_STATE
    chan = pl.BlockSpec((rows, S5_PACK * S5_CH), lambda j, i: (row_block(i), j))
    op = lambda off, shape: pl.BlockSpec((1,) + shape, lambda j, i: (j + off, 0, 0))
    blk = pl.BlockSpec((rows, SCAN_LANES), lambda j, i: (row_block(i), j))
    par = pl.BlockSpec((1, SCAN_LANES), lambda j, i: (0, j))
    return chan, op, blk, par


def _s5_scan_fwd(u, b3, a_r, a_i, name):
    t = u.shape[0]
    rows = min(SCAN_ROWS, t)
    chan, op, blk, par = _s5_scan_specs(rows, lambda i: i)

    def body(u_ref, b_re_ref, b_im_ref, ar_ref, ai_ref, or_ref, oi_ref, cr_ref, ci_ref, sr_ref, si_ref):
        i = pl.program_id(1)

        @pl.when(i == 0)
        def _():
            cr_ref[...] = jnp.zeros_like(cr_ref)
            ci_ref[...] = jnp.zeros_like(ci_ref)

        sr_ref[...] = _dot(u_ref[...], b_re_ref[0])
        si_ref[...] = _dot(u_ref[...], b_im_ref[0])
        ar, ai = ar_ref[...], ai_ref[...]
        cr, ci = _scan_chunk(sr_ref, si_ref, or_ref, oi_ref, ar, ai, cr_ref[...], ci_ref[...], reverse=False)
        cr_ref[...] = cr
        ci_ref[...] = ci

    st_r, st_i = pl.pallas_call(
        body, name=name, grid=(S5_BLOCKS, t // rows),
        in_specs=[chan, op(0, b3.shape[1:]), op(S5_BLOCKS, b3.shape[1:]), par, par], out_specs=[blk, blk],
        out_shape=[jax.ShapeDtypeStruct((t, S5_WIDTH), F32)] * 2,
        scratch_shapes=[pltpu.VMEM((1, SCAN_LANES), F32)] * 2 + [pltpu.VMEM((rows, SCAN_LANES), F32)] * 2,
        compiler_params=_params(("parallel", "arbitrary")),
    )(u, b3, b3, a_r, a_i)
    return st_r, st_i


def _s5_scan_bwd(dy, c3, st_r, st_i, a_r, a_i, name):
    t = dy.shape[0]
    rows = min(SCAN_ROWS, t)
    nc = t // rows
    chan, op, blk, par = _s5_scan_specs(rows, lambda i: nc - 1 - i)

    def body(dy_ref, c_re_ref, c_im_ref, xr_ref, xi_ref, ar_ref, ai_ref, gr_ref, gi_ref, dar_ref, dai_ref, cr_ref, ci_ref, dr_ref, di_ref):
        i = pl.program_id(1)
        dr_ref[...] = _dot(dy_ref[...], c_re_ref[0], "nt")
        di_ref[...] = _dot(dy_ref[...], c_im_ref[0], "nt")

        @pl.when(i == 0)
        def _():
            cr_ref[...] = jnp.zeros_like(cr_ref)
            ci_ref[...] = jnp.zeros_like(ci_ref)
            dar_ref[...] = jnp.zeros_like(dar_ref)
            dai_ref[...] = jnp.zeros_like(dai_ref)

        ar, ai = ar_ref[...], ai_ref[...]
        cr, ci = cr_ref[...], ci_ref[...]
        last = lax.broadcasted_iota(jnp.int32, (SUBLANES, SCAN_LANES), 0) == SUBLANES - 1
        sums = [jnp.zeros((SUBLANES, SCAN_LANES), F32), jnp.zeros((SUBLANES, SCAN_LANES), F32)]

        def visit(sl, gr, gi, next_r, next_i):
            nr = jnp.where(last, next_r, pltpu.roll(gr, SUBLANES - 1, 0))
            ni = jnp.where(last, next_i, pltpu.roll(gi, SUBLANES - 1, 0))
            xr, xi = xr_ref[sl, :], xi_ref[sl, :]
            sums[0] = sums[0] + (nr * xr + ni * xi)
            sums[1] = sums[1] + (ni * xr - nr * xi)

        first_r, first_i = _scan_chunk(dr_ref, di_ref, gr_ref, gi_ref, ar, -ai, cr, ci, reverse=True, visit=visit)
        dar_ref[...] += jnp.sum(sums[0], axis=0, keepdims=True)
        dai_ref[...] += jnp.sum(sums[1], axis=0, keepdims=True)
        cr_ref[...] = first_r
        ci_ref[...] = first_i

    return pl.pallas_call(
        body, name=name, grid=(S5_BLOCKS, nc),
        in_specs=[chan, op(0, c3.shape[1:]), op(S5_BLOCKS, c3.shape[1:]), blk, blk, par, par], out_specs=[blk, blk, par, par],
        out_shape=[jax.ShapeDtypeStruct((t, S5_WIDTH), F32)] * 2 + [jax.ShapeDtypeStruct((1, S5_WIDTH), F32)] * 2,
        scratch_shapes=[pltpu.VMEM((1, SCAN_LANES), F32)] * 2 + [pltpu.VMEM((rows, SCAN_LANES), F32)] * 2,
        compiler_params=_params(("parallel", "arbitrary")),
    )(dy, c3, c3, st_r, st_i, a_r, a_i)


def _s5_prep(lam_re, lam_im, log_dt, b_re, b_im, c_re, c_im, d_skip):
    dt = jnp.exp(log_dt)[:, None]
    mag = jnp.exp(lam_re * dt)
    ar, ai = mag * jnp.cos(lam_im * dt), mag * jnp.sin(lam_im * dt)
    den = lam_re * lam_re + lam_im * lam_im
    cr = ((ar - 1.0) * lam_re + ai * lam_im) / den
    ci = (ai * lam_re - (ar - 1.0) * lam_im) / den
    bbar_r = cr[..., None] * b_re - ci[..., None] * b_im
    bbar_i = cr[..., None] * b_im + ci[..., None] * b_re
    eye = jnp.eye(S5_PACK, dtype=F32)

    def bdiag(m):
        g, a, b = m.shape
        m4 = m.reshape(g // S5_PACK, S5_PACK, a, b)
        return (eye[None, :, None, :, None] * m4[:, :, :, None, :]).reshape(g // S5_PACK, S5_PACK * a, S5_PACK * b)

    b3 = jnp.concatenate([bdiag(bbar_r.transpose(0, 2, 1)), bdiag(bbar_i.transpose(0, 2, 1))], axis=0)
    c3 = jnp.concatenate([bdiag(c_re.transpose(0, 2, 1)), -bdiag(c_im.transpose(0, 2, 1))], axis=0)
    return b3, c3, d_skip.reshape(1, GROUP_WIDTH), ar.reshape(1, S5_WIDTH), ai.reshape(1, S5_WIDTH)


RET_UNROLL = 8


def _loop_unrolled(trips, body, init):
    factor = math.gcd(trips, RET_UNROLL)

    def several(i, carry):
        for u in range(factor):
            carry = body(i * factor + u, carry)
        return carry

    return lax.fori_loop(0, trips // factor, several, init)


def _ret_consts(lgam):
    c = RET_CHUNK
    r = lax.broadcasted_iota(jnp.int32, (c, c), 0)
    m = lax.broadcasted_iota(jnp.int32, (c, c), 1)
    rel = (r - m).astype(F32)
    decay = jnp.where(rel >= 0, jnp.exp(lgam * jnp.maximum(rel, 0.0)), 0.0)
    idx = lax.broadcasted_iota(jnp.int32, (c, 1), 0).astype(F32)
    zeta = jnp.exp(lgam * (c - 1.0 - idx))
    xi = jnp.exp(lgam * (idx + 1.0))
    return decay, zeta, xi, jnp.exp(lgam * c)


def _ret_specs(t):
    qk = lambda off: pl.BlockSpec((1, t, RET_QK), lambda h: (h + off, 0, 0))
    col = lambda off: pl.BlockSpec((t, RET_V), lambda h: (0, h + off))
    return qk, col


def _ret_fwd(qk, p_ret, lgam, name):
    t = qk.shape[1]
    nck = t // RET_CHUNK
    qk_spec, col = _ret_specs(t)

    def body(lg_ref, q_ref, k_ref, v_ref, g_ref, o_ref, y_ref):
        decay, zeta, xi, gam = _ret_consts(lg_ref[pl.program_id(0)])

        def step(n, state):
            sl = pl.ds(pl.multiple_of(n * RET_CHUNK, RET_CHUNK), RET_CHUNK)
            q, k, v = q_ref[0, sl, :], k_ref[0, sl, :] * (RET_QK ** -0.5), v_ref[sl, :]
            s = _dot(q, k, "nt") * decay
            o = _dot(s, v) + _dot(q, state) * xi
            o_ref[sl, :] = o
            y_ref[sl, :] = _ret_gate_fn(o, g_ref[sl, :]).astype(y_ref.dtype)
            return gam * state + _dot(k, zeta * v, "tn")

        _loop_unrolled(nck, step, jnp.zeros((RET_QK, RET_V), F32))

    return pl.pallas_call(
        body, name=name, grid=(RET_HEADS,), in_specs=[_SMEM, qk_spec(0), qk_spec(RET_HEADS), col(4), col(8)],
        out_specs=[col(0), col(0)],
        out_shape=[jax.ShapeDtypeStruct((t, GROUP_WIDTH), F32), jax.ShapeDtypeStruct((t, GROUP_WIDTH), BF16)],
        compiler_params=_params(("parallel",)),
    )(lgam, qk, qk, p_ret, p_ret)


def _ret_bwd(qk, p_ret, o_all, dy, lgam, name):
    t = qk.shape[1]
    nck = t // RET_CHUNK
    qk_spec, col = _ret_specs(t)
    gate_bwd = _vjp_block(_ret_gate_fn, 2)

    def body(lg_ref, q_ref, k_ref, v_ref, g_ref, o_ref, dy_ref, dq_ref, dk_ref, dv_ref, dg_ref, st_ref):
        decay, zeta, xi, gam = _ret_consts(lg_ref[pl.program_id(0)])
        scale = RET_QK ** -0.5

        def fstep(n, state):
            sl = pl.ds(pl.multiple_of(n * RET_CHUNK, RET_CHUNK), RET_CHUNK)
            st_ref[n] = state
            return gam * state + _dot(k_ref[0, sl, :] * scale, zeta * v_ref[sl, :], "tn")

        _loop_unrolled(nck, fstep, jnp.zeros((RET_QK, RET_V), F32))

        def bstep(r, grad_state):
            n = nck - 1 - r
            sl = pl.ds(pl.multiple_of(n * RET_CHUNK, RET_CHUNK), RET_CHUNK)
            q, k, v = q_ref[0, sl, :], k_ref[0, sl, :] * scale, v_ref[sl, :]
            d_o, dg = gate_bwd(o_ref[sl, :], g_ref[sl, :], dy_ref[sl, :])
            dg_ref[sl, :] = dg.astype(dg_ref.dtype)
            s = _dot(q, k, "nt") * decay
            ds = _dot(d_o, v, "nt") * decay
            xdo = xi * d_o
            dq_ref[0, sl, :] = _dot(ds, k) + _dot(xdo, st_ref[n], "nt")
            dk_ref[0, sl, :] = (_dot(ds, q, "tn") + _dot(zeta * v, grad_state, "nt")) * scale
            dv_ref[sl, :] = (_dot(s, d_o, "tn") + zeta * _dot(k, grad_state)).astype(dv_ref.dtype)
            return gam * grad_state + _dot(q, xdo, "tn")

        _loop_unrolled(nck, bstep, jnp.zeros((RET_QK, RET_V), F32))

    hd = pl.BlockSpec((1, t, RET_QK), lambda h: (h, 0, 0))
    return pl.pallas_call(
        body, name=name, grid=(RET_HEADS,),
        in_specs=[_SMEM, qk_spec(0), qk_spec(RET_HEADS), col(4), col(8), col(0), col(0)],
        out_specs=[hd, hd, col(0), col(0)],
        out_shape=[jax.ShapeDtypeStruct((RET_HEADS, t, RET_QK), F32)] * 2 + [jax.ShapeDtypeStruct((t, GROUP_WIDTH), BF16)] * 2,
        scratch_shapes=[pltpu.VMEM((nck, RET_QK, RET_V), F32)], compiler_params=_params(("parallel",)),
    )(lgam, qk, qk, p_ret, p_ret, o_all, dy)


SWA_GROUP = SWA_HEADS // SWA_KV_HEADS
SWA_SCALE = SWA_HD ** -0.5


def _swa_mask(n):
    rows = SWA_GROUP * WINDOW
    r = lax.broadcasted_iota(jnp.int32, (rows, 2 * WINDOW), 0) & (WINDOW - 1)
    j = lax.broadcasted_iota(jnp.int32, (rows, 2 * WINDOW), 1)
    dist = r + WINDOW - j
    return (dist >= 0) & (dist < WINDOW) & (n * WINDOW + j - WINDOW >= 0)


def _swa_sink_rows(sink_ref, kv):
    row = lax.broadcasted_iota(jnp.int32, (SWA_GROUP * WINDOW, 1), 0)
    sink = jnp.zeros((SWA_GROUP * WINDOW, 1), F32)
    for g in range(SWA_GROUP):
        sink = jnp.where(row >= g * WINDOW, sink_ref[kv * SWA_GROUP + g], sink)
    return sink


def _swa_pad_keys(n, k_ref, v_ref, kp_ref, vp_ref):
    @pl.when(n == 0)
    def _():
        zero = jnp.zeros((WINDOW, SWA_HD), F32)
        kp_ref[0:WINDOW, :] = zero
        vp_ref[0:WINDOW, :] = zero
        kp_ref[WINDOW:, :] = k_ref[0]
        vp_ref[WINDOW:, :] = v_ref[0]


SWA_STEP_BLOCKS = 4


def _swa_specs(t):
    per_step = math.gcd(t // WINDOW, SWA_STEP_BLOCKS)
    blk = lambda w: pl.BlockSpec((SWA_GROUP, per_step * WINDOW, w), lambda kv, n: (kv, n, 0))
    kspec = lambda off: pl.BlockSpec((1, t, SWA_HD), lambda kv, n: (SWA_HEADS + off + kv, 0, 0))
    return blk, kspec, per_step


def _swa_fwd(qkv, sinks, name):
    t = qkv.shape[1]
    rows = SWA_GROUP * WINDOW
    blk, kspec, per_step = _swa_specs(t)

    def body(sink_ref, q_ref, k_ref, v_ref, o_ref, lse_ref, kp_ref, vp_ref):
        kv, step = pl.program_id(0), pl.program_id(1)
        _swa_pad_keys(step, k_ref, v_ref, kp_ref, vp_ref)
        sink = _swa_sink_rows(sink_ref, kv)
        for u in range(per_step):
            n = step * per_step + u
            here = slice(u * WINDOW, (u + 1) * WINDOW)
            win = pl.ds(pl.multiple_of(n * WINDOW, WINDOW), 2 * WINDOW)
            s = _dot(q_ref[:, here, :].reshape(rows, SWA_HD), kp_ref[win, :], "nt") * SWA_SCALE
            s = jnp.where(_swa_mask(n), s, NEG)
            m = jnp.maximum(jnp.max(s, axis=-1, keepdims=True), sink)
            p = jnp.exp(s - m)
            den = jnp.sum(p, axis=-1, keepdims=True) + jnp.exp(sink - m)
            o_ref[:, here, :] = _dot(p / den, vp_ref[win, :]).reshape(SWA_GROUP, WINDOW, SWA_HD)
            lse_ref[:, here, :] = (m + jnp.log(den)).reshape(SWA_GROUP, WINDOW, 1)

    return pl.pallas_call(
        body, name=name, grid=(SWA_KV_HEADS, t // WINDOW // per_step),
        in_specs=[_SMEM, blk(SWA_HD), kspec(0), kspec(SWA_KV_HEADS)], out_specs=[blk(SWA_HD), blk(1)],
        out_shape=[jax.ShapeDtypeStruct((SWA_HEADS, t, SWA_HD), F32), jax.ShapeDtypeStruct((SWA_HEADS, t, 1), F32)],
        scratch_shapes=[pltpu.VMEM((t + WINDOW, SWA_HD), F32)] * 2, compiler_params=_params(("parallel", "arbitrary")),
    )(sinks, qkv, qkv, qkv)


def _swa_bwd(qkv, o, lse, d_o, sinks, name):
    t = qkv.shape[1]
    nb = t // WINDOW
    rows = SWA_GROUP * WINDOW
    blk, kspec, per_step = _swa_specs(t)

    def body(sink_ref, q_ref, k_ref, v_ref, o_ref, lse_ref, do_ref, dq_ref, dk_ref, dv_ref, dsink_ref,
             kp_ref, vp_ref, dkp_ref, dvp_ref):
        kv, step = pl.program_id(0), pl.program_id(1)
        _swa_pad_keys(step, k_ref, v_ref, kp_ref, vp_ref)

        @pl.when(step == 0)
        def _():
            dkp_ref[...] = jnp.zeros_like(dkp_ref)
            dvp_ref[...] = jnp.zeros_like(dvp_ref)
            dsink_ref[...] = jnp.zeros_like(dsink_ref)

        sink = _swa_sink_rows(sink_ref, kv)
        head = lax.broadcasted_iota(jnp.int32, (SWA_GROUP, 128), 0)
        acc = jnp.zeros((SWA_GROUP, 128), F32)
        for u in range(per_step):
            n = step * per_step + u
            here = slice(u * WINDOW, (u + 1) * WINDOW)
            win = pl.ds(pl.multiple_of(n * WINDOW, WINDOW), 2 * WINDOW)
            q, dout = q_ref[:, here, :].reshape(rows, SWA_HD), do_ref[:, here, :].reshape(rows, SWA_HD)
            lse_n = lse_ref[:, here, :].reshape(rows, 1)
            s = _dot(q, kp_ref[win, :], "nt") * SWA_SCALE
            s = jnp.where(_swa_mask(n), s, NEG)
            p = jnp.exp(s - lse_n)
            delta = jnp.sum(dout * o_ref[:, here, :].reshape(rows, SWA_HD), axis=-1, keepdims=True)
            ds = p * (_dot(dout, vp_ref[win, :], "nt") - delta)
            dq_ref[:, here, :] = (_dot(ds, kp_ref[win, :]) * SWA_SCALE).reshape(SWA_GROUP, WINDOW, SWA_HD).astype(dq_ref.dtype)
            dkp_ref[win, :] += _dot(ds, q, "tn") * SWA_SCALE
            dvp_ref[win, :] += _dot(p, dout, "tn")
            term = jnp.exp(sink - lse_n) * delta
            for g in range(SWA_GROUP):
                acc = jnp.where(head == g, acc + jnp.sum(term[g * WINDOW:(g + 1) * WINDOW], axis=0, keepdims=True), acc)
        dsink_ref[0] -= acc

        @pl.when(step == nb // per_step - 1)
        def _():
            dk_ref[0] = dkp_ref[WINDOW:, :].astype(dk_ref.dtype)
            dv_ref[0] = dvp_ref[WINDOW:, :].astype(dv_ref.dtype)

    kout = pl.BlockSpec((1, t, SWA_HD), lambda kv, n: (kv, 0, 0))
    dq, dk, dv, dsink = pl.pallas_call(
        body, name=name, grid=(SWA_KV_HEADS, nb // per_step),
        in_specs=[_SMEM, blk(SWA_HD), kspec(0), kspec(SWA_KV_HEADS), blk(SWA_HD), blk(1), blk(SWA_HD)],
        out_specs=[blk(SWA_HD), kout, kout, pl.BlockSpec((1, SWA_GROUP, 128), lambda kv, n: (kv, 0, 0))],
        out_shape=[jax.ShapeDtypeStruct((SWA_HEADS, t, SWA_HD), BF16), jax.ShapeDtypeStruct((SWA_KV_HEADS, t, SWA_HD), BF16),
                   jax.ShapeDtypeStruct((SWA_KV_HEADS, t, SWA_HD), BF16), jax.ShapeDtypeStruct((SWA_KV_HEADS, SWA_GROUP, 128), F32)],
        scratch_shapes=[pltpu.VMEM((t + WINDOW, SWA_HD), F32)] * 4, compiler_params=_params(("parallel", "arbitrary")),
    )(sinks, qkv, qkv, qkv, o, lse, d_o)
    return jnp.concatenate([dq, dk, dv], axis=0), dsink[:, :, 0].reshape(SWA_HEADS)


MLA_SCALE = (MLA_NOPE + MLA_ROPE) ** -0.5
MLA_TILE = 512
MLA_KEY_TILE = 512
MLA_BWD_TILE = 512


def _mla_diag(s):
    r = lax.broadcasted_iota(jnp.int32, s.shape, 0)
    c = lax.broadcasted_iota(jnp.int32, s.shape, 1)
    return jnp.where(c <= r, s, NEG)


def _mla_specs(t, tile):
    whole = lambda w, off: pl.BlockSpec((t, w), lambda h, i: (0, 2 * h + off))
    head = lambda w: pl.BlockSpec((1, t, w), lambda h, i: (h, 0, 0))
    key_rope = pl.BlockSpec((1, t, MLA_ROPE), lambda h, i: (MLA_HEADS, 0, 0))
    tile_of = lambda w: pl.BlockSpec((1, tile, w), lambda h, i: (h, i, 0))
    return whole, head, key_rope, tile_of


def _mla_attend(qn, rot, kv, name):
    t = qn.shape[1]
    tile = min(MLA_TILE, t)
    ktile = min(MLA_KEY_TILE, t)
    ratio = ktile // tile
    whole, head, key_rope, tile_of = _mla_specs(t, tile)

    def body(qn_ref, qr_ref, kn_ref, kr_ref, v_ref, o_ref, lse_ref, m_ref, l_ref, acc_ref):
        i = pl.program_id(1)
        qn_b, qr_b = qn_ref[0], qr_ref[0]

        def rows(j):
            return pl.ds(pl.multiple_of(j * ktile, ktile), ktile)

        def scores(j):
            return (_dot(qn_b, kn_ref[rows(j), :], "nt") + _dot(qr_b, kr_ref[0, rows(j), :], "nt")) * MLA_SCALE

        def causal(s, j):
            qpos = i * tile + lax.broadcasted_iota(jnp.int32, s.shape, 0)
            kpos = j * ktile + lax.broadcasted_iota(jnp.int32, s.shape, 1)
            return jnp.where(kpos <= qpos, s, NEG)

        def update(s, j):
            m_old = m_ref[...]
            m_new = jnp.maximum(m_old, jnp.max(s, axis=-1, keepdims=True))
            alpha = jnp.exp(m_old - m_new)
            p = jnp.exp(s - m_new)
            l_ref[...] = alpha * l_ref[...] + jnp.sum(p, axis=-1, keepdims=True)
            acc_ref[...] = alpha * acc_ref[...] + _dot(p, v_ref[rows(j), :])
            m_ref[...] = m_new

        m_ref[...] = jnp.full_like(m_ref, NEG)
        l_ref[...] = jnp.zeros_like(l_ref)
        acc_ref[...] = jnp.zeros_like(acc_ref)

        def step(j, s_cur):
            s_next = scores(j + 1)
            update(s_cur, j)
            return s_next

        last = i // ratio
        s_last = lax.fori_loop(0, last, step, scores(0))
        update(causal(s_last, last), last)
        o_ref[...] = acc_ref[...] / l_ref[...]
        lse_ref[0] = m_ref[...] + jnp.log(l_ref[...])

    return pl.pallas_call(
        body, name=name, grid=(MLA_HEADS, t // tile),
        in_specs=[tile_of(MLA_NOPE), tile_of(MLA_ROPE), whole(MLA_NOPE, 0), key_rope, whole(MLA_V, 1)],
        out_specs=[pl.BlockSpec((tile, MLA_V), lambda h, i: (i, h)), tile_of(1)],
        out_shape=[jax.ShapeDtypeStruct((t, GROUP_WIDTH), F32), jax.ShapeDtypeStruct((MLA_HEADS, t, 1), F32)],
        scratch_shapes=[pltpu.VMEM((tile, 1), F32), pltpu.VMEM((tile, 1), F32), pltpu.VMEM((tile, MLA_V), F32)],
        compiler_params=_params(("parallel", "parallel")),
    )(qn, rot, kv, rot, kv)


def _mla_attend_bwd(qn, rot, kv, lse, o, d_o, name):
    t = qn.shape[1]
    tile = min(MLA_BWD_TILE, t)
    nt = t // tile
    whole, head, key_rope, tile_of = _mla_specs(t, tile)

    def body(qn_ref, qr_ref, kn_ref, kr_ref, v_ref, lse_ref, o_ref, do_ref, dqn_ref, dqr_ref, dkn_ref, dv_ref, dkr_ref, dl_ref):
        j = pl.program_id(1)

        @pl.when(j == 0)
        def _():
            dqn_ref[...] = jnp.zeros_like(dqn_ref)
            dqr_ref[...] = jnp.zeros_like(dqr_ref)
            dl_ref[0] = jnp.sum(do_ref[...] * o_ref[...], axis=-1, keepdims=True)

        dkn_ref[...] = jnp.zeros_like(dkn_ref)
        dv_ref[...] = jnp.zeros_like(dv_ref)
        dkr_ref[...] = jnp.zeros_like(dkr_ref)
        kn_b, kr_b, v_b = kn_ref[...], kr_ref[0], v_ref[...]

        def block(i, diagonal):
            sl = pl.ds(pl.multiple_of(i * tile, tile), tile)
            qn_b, qr_b, dout = qn_ref[0, sl, :], qr_ref[0, sl, :], do_ref[sl, :]
            s = (_dot(qn_b, kn_b, "nt") + _dot(qr_b, kr_b, "nt")) * MLA_SCALE
            if diagonal:
                s = _mla_diag(s)
            p = jnp.exp(s - lse_ref[0, sl, :])
            ds = p * (_dot(dout, v_b, "nt") - dl_ref[0, sl, :]) * MLA_SCALE
            dv_ref[...] += _dot(p, dout, "tn")
            dkn_ref[...] += _dot(ds, qn_b, "tn")
            dkr_ref[0] += _dot(ds, qr_b, "tn")
            dqn_ref[0, sl, :] += _dot(ds, kn_b)
            dqr_ref[0, sl, :] += _dot(ds, kr_b)

        block(j, True)

        def step(i, carry):
            block(i, False)
            return carry

        lax.fori_loop(j + 1, nt, step, 0)

    key_tile = lambda w, off: pl.BlockSpec((tile, w), lambda h, j: (j, 2 * h + off))
    out_tile = pl.BlockSpec((tile, MLA_V), lambda h, j: (j, h))
    return pl.pallas_call(
        body, name=name, grid=(MLA_HEADS, nt),
        in_specs=[head(MLA_NOPE), head(MLA_ROPE), key_tile(MLA_NOPE, 0), pl.BlockSpec((1, tile, MLA_ROPE), lambda h, j: (MLA_HEADS, j, 0)),
                  key_tile(MLA_V, 1), head(1), pl.BlockSpec((t, MLA_V), lambda h, j: (0, h)), pl.BlockSpec((t, MLA_V), lambda h, j: (0, h))],
        out_specs=[head(MLA_NOPE), head(MLA_ROPE), out_tile, out_tile, tile_of(MLA_ROPE)],
        out_shape=[jax.ShapeDtypeStruct((MLA_HEADS, t, MLA_NOPE), F32), jax.ShapeDtypeStruct((MLA_HEADS, t, MLA_ROPE), F32),
                   jax.ShapeDtypeStruct((t, MLA_HEADS * MLA_NOPE), F32), jax.ShapeDtypeStruct((t, MLA_HEADS * MLA_V), F32),
                   jax.ShapeDtypeStruct((MLA_HEADS, t, MLA_ROPE), F32)],
        scratch_shapes=[pltpu.VMEM((1, t, 1), F32)], compiler_params=_params(("parallel", "arbitrary")),
    )(qn, rot, kv, rot, kv, lse, o, d_o)


def _place():
    return lax.axis_index("x"), lax.axis_index("y"), lax.axis_index("c")


def _all_gather(arrs, name):
    n = len(arrs)

    def body(*refs):
        x_refs, o_refs = refs[:n], refs[n:2 * n]
        send_sems, recv_sems, local_sems = refs[2 * n:]
        x, y, c = _place()
        me, sibling = (x, y, c), (x, y, 1 - c)
        chips = [(1 - x, y), (x, 1 - y), (1 - x, 1 - y)]

        def slot(a, p):
            return o_refs[a].at[4 * p[0] + 2 * p[1] + p[2]]

        def copy(a, k, block, to, src=None):
            return pltpu.make_async_remote_copy(
                src_ref=slot(a, block) if src is None else src, dst_ref=slot(a, block),
                send_sem=send_sems.at[a, k], recv_sem=recv_sems.at[a, k], device_id=to, device_id_type=MESH_ID)

        mine = [pltpu.make_async_copy(x_refs[a], slot(a, me), local_sems.at[a]) for a in range(n)]
        for cp in mine:
            cp.start()
        first = []
        for a in range(n):
            first.append(copy(a, 0, me, sibling, src=x_refs[a]))
            first += [copy(a, 1 + j, me, (*chip, c), src=x_refs[a]) for j, chip in enumerate(chips)]
        for cp in first:
            cp.start()
        passed = []
        for j, chip in enumerate(chips):
            for a in range(n):
                copy(a, 1 + j, (*chip, c), me).wait_recv()
                cp = copy(a, 4 + j, (*chip, c), sibling)
                cp.start()
                passed.append(cp)
        for a in range(n):
            copy(a, 0, sibling, me).wait_recv()
            for j, chip in enumerate(chips):
                copy(a, 4 + j, (*chip, 1 - c), me).wait_recv()
        for cp in first + passed:
            cp.wait_send()
        for cp in mine:
            cp.wait()

    return pl.pallas_call(
        body, name=name, in_specs=[_ANY] * n, out_specs=[_ANY] * n,
        out_shape=[jax.ShapeDtypeStruct((N_DEV,) + a.shape, a.dtype) for a in arrs],
        scratch_shapes=[pltpu.SemaphoreType.DMA((n, 7)), pltpu.SemaphoreType.DMA((n, 7)), pltpu.SemaphoreType.DMA((n,))],
    )(*arrs)


def _pass_to_sibling(arrs, name):
    n = len(arrs)

    def body(*refs):
        a_refs, o_refs = refs[:n], refs[n:2 * n]
        send_sems, recv_sems = refs[2 * n:]
        x, y, c = _place()
        chips = [(1 - x, y), (x, 1 - y), (1 - x, 1 - y)]
        slot = lambda px, py, pc: 4 * px + 2 * py + pc
        sends, recvs = [], []
        for a in range(n):
            for j, (px, py) in enumerate(chips):
                sends.append(pltpu.make_async_remote_copy(
                    src_ref=a_refs[a].at[slot(px, py, c)], dst_ref=o_refs[a].at[slot(px, py, c)], send_sem=send_sems.at[a, j],
                    recv_sem=recv_sems.at[a, j], device_id=(x, y, 1 - c), device_id_type=MESH_ID))
                recvs.append(pltpu.make_async_remote_copy(
                    src_ref=a_refs[a].at[slot(px, py, c)], dst_ref=o_refs[a].at[slot(px, py, 1 - c)], send_sem=send_sems.at[a, j],
                    recv_sem=recv_sems.at[a, j], device_id=(x, y, 1 - c), device_id_type=MESH_ID))
        for cp in sends:
            cp.start()
        for cp in sends:
            cp.wait_send()
        for cp in recvs:
            cp.wait_recv()

    return pl.pallas_call(
        body, name=name, in_specs=[_ANY] * n, out_specs=[_ANY] * n,
        out_shape=[jax.ShapeDtypeStruct(a.shape, a.dtype) for a in arrs], input_output_aliases={i: i for i in range(n)},
        scratch_shapes=[pltpu.SemaphoreType.DMA((n, 3)), pltpu.SemaphoreType.DMA((n, 3))],
    )(*arrs)


def _scatter_core(grads, name):
    n = len(grads)

    def body(*refs):
        g_refs, got_refs = refs[:n], refs[n:2 * n]
        send_sems, recv_sems = refs[2 * n:]
        x, y, c = _place()
        sends = [pltpu.make_async_remote_copy(
            src_ref=g_refs[a].at[2 * q + 1 - c], dst_ref=got_refs[a].at[q], send_sem=send_sems.at[a, q],
            recv_sem=recv_sems.at[a, q], device_id=(x, y, 1 - c), device_id_type=MESH_ID) for a in range(n) for q in range(4)]
        for cp in sends:
            cp.start()
        for cp in sends:
            cp.wait()

    return pl.pallas_call(
        body, name=name, in_specs=[_ANY] * n, out_specs=[_ANY] * n,
        out_shape=[jax.ShapeDtypeStruct((4,) + g.shape[1:], g.dtype) for g in grads],
        scratch_shapes=[pltpu.SemaphoreType.DMA((n, 4)), pltpu.SemaphoreType.DMA((n, 4))],
    )(*grads)


def _scatter_chips(parts, name):
    n = len(parts)

    def body(*refs):
        p_refs, o_refs = refs[:n], refs[n:2 * n]
        send_sems, recv_sems = refs[2 * n:]
        x, y, c = _place()
        chips = [(1 - x, y), (x, 1 - y), (1 - x, 1 - y)]
        sends = [pltpu.make_async_remote_copy(
            src_ref=p_refs[a].at[2 * px + py], dst_ref=o_refs[a].at[j], send_sem=send_sems.at[a, j],
            recv_sem=recv_sems.at[a, j], device_id=(px, py, c), device_id_type=MESH_ID)
            for a in range(n) for j, (px, py) in enumerate(chips)]
        for cp in sends:
            cp.start()
        for cp in sends:
            cp.wait()

    return pl.pallas_call(
        body, name=name, in_specs=[_ANY] * n, out_specs=[_ANY] * n,
        out_shape=[jax.ShapeDtypeStruct((3,) + p.shape[1:], p.dtype) for p in parts],
        scratch_shapes=[pltpu.SemaphoreType.DMA((n, 3)), pltpu.SemaphoreType.DMA((n, 3))],
    )(*parts)


GATHER_PEERS = [(0, 0, 1), (1, 0, 0), (0, 1, 0), (1, 1, 0)]
CHIP_PEERS = [(1, 0, 0), (0, 1, 0), (1, 1, 0)]
CORE_PEERS = [(0, 0, 1)] * 4
_HBM = pl.BlockSpec(memory_space=pltpu.HBM)
_SEM = pl.BlockSpec(memory_space=pltpu.SEMAPHORE)
_EFFECT = pltpu.SideEffectType.DATAFLOW_SIDE_EFFECTING


def _push_copies(src_refs, land_refs, send_sems, recv_sems, peers, src_of, slot_of):
    place = _place()
    flip = lambda v, f: 1 - v if f else v
    return [pltpu.make_async_remote_copy(
        src_ref=src_of(src_refs[a], k), dst_ref=land_refs[a].at[slot_of(k)], send_sem=send_sems[a], recv_sem=recv_sems[a],
        device_id=tuple(flip(v, f) for v, f in zip(place, peer)), device_id_type=MESH_ID)
        for a in range(len(src_refs)) for k, peer in enumerate(peers)]


def _push_start(srcs, land_shapes, peers, src_of, slot_of, name, after=None):
    n = len(srcs)
    extra = [] if after is None else [after]

    def body(*refs):
        src_refs, land_refs = refs[:n], refs[n:2 * n]
        refs = refs[2 * n + len(extra):]
        send_sems, recv_sems = refs[:n], refs[n:2 * n]
        token = refs[-1]
        for cp in _push_copies(src_refs, land_refs, send_sems, recv_sems, peers, src_of, slot_of):
            cp.start()
        token[...] = jnp.zeros_like(token)

    sems = [pltpu.SemaphoreType.DMA(())] * (2 * n)
    lands = [pltpu.with_memory_space_constraint(lax.empty(s.shape, s.dtype), pltpu.HBM) for s in land_shapes]
    res = pl.pallas_call(
        body, name=name, in_specs=[_HBM] * (2 * n) + [_ANY] * len(extra),
        out_specs=[_SEM] * (2 * n) + [_HBM] * (2 * n) + [pl.BlockSpec(memory_space=pltpu.VMEM)],
        out_shape=sems + [pltpu.HBM(s.shape, s.dtype) for s in srcs] + [pltpu.HBM(s.shape, s.dtype) for s in land_shapes]
        + [jax.ShapeDtypeStruct((8, 128), F32)],
        input_output_aliases={i: 2 * n + i for i in range(2 * n)},
        compiler_params=pltpu.CompilerParams(has_side_effects=_EFFECT),
    )(*[pltpu.with_memory_space_constraint(s, pltpu.HBM) for s in srcs], *lands, *extra)
    return list(res[:n]), list(res[n:2 * n]), list(res[2 * n:3 * n]), list(res[3 * n:4 * n]), res[-1]


def _push_wait(send_sems, recv_sems, srcs, lands, after, peers, src_of, slot_of, name):
    n = len(srcs)

    def body(*refs):
        src_refs, land_refs = refs[:n], refs[n:2 * n]
        s_sems, r_sems = refs[2 * n:3 * n], refs[3 * n:4 * n]
        copies = _push_copies(src_refs, land_refs, s_sems, r_sems, peers, src_of, slot_of)
        for cp in copies:
            cp.wait_send()
        for cp in copies:
            cp.wait_recv()

    res = pl.pallas_call(
        body, name=name, in_specs=[_HBM] * (2 * n) + [_SEM] * (2 * n) + [_ANY], out_specs=[_HBM] * (2 * n),
        out_shape=[pltpu.HBM(s.shape, s.dtype) for s in srcs] + [pltpu.HBM(s.shape, s.dtype) for s in lands],
        input_output_aliases={i: i for i in range(2 * n)},
        compiler_params=pltpu.CompilerParams(has_side_effects=_EFFECT),
    )(*srcs, *lands, *send_sems, *recv_sems, after)
    return list(res[:n]), list(res[n:])


def _pick_sum(picked, rest, index, pick_of, out_dtype, name):
    nq, r, cdim = rest.shape
    one = nq == 3
    tr = _row_tile(r, 512, 16)
    tc = 512 if cdim % 512 == 0 else cdim
    grid = (1 if one else nq, r // tr, cdim // tc)

    def body(i_ref, p_ref, r_ref, o_ref):
        acc = p_ref[0].astype(F32)
        if one:
            for j in range(3):
                acc = acc + r_ref[j].astype(F32)
            o_ref[...] = acc.astype(out_dtype)
        else:
            o_ref[0] = (acc + r_ref[0].astype(F32)).astype(out_dtype)

    spec = pltpu.PrefetchScalarGridSpec(
        num_scalar_prefetch=1, grid=grid,
        in_specs=[pl.BlockSpec((1, tr, tc), lambda q, i, j, i_ref: (pick_of(q, i_ref[0]), i, j)),
                  pl.BlockSpec((3, tr, tc), lambda q, i, j, i_ref: (0, i, j)) if one else pl.BlockSpec((1, tr, tc), lambda q, i, j, i_ref: (q, i, j))],
        out_specs=pl.BlockSpec((tr, tc), lambda q, i, j, i_ref: (i, j)) if one else pl.BlockSpec((1, tr, tc), lambda q, i, j, i_ref: (q, i, j)))
    return pl.pallas_call(
        body, name=name, grid_spec=spec,
        out_shape=jax.ShapeDtypeStruct((r, cdim) if one else (nq, r, cdim), out_dtype),
        compiler_params=_params(("parallel", "parallel", "parallel")),
    )(index.astype(jnp.int32).reshape(1), picked, rest)


def _adamw_fn(w, g, m, v):
    m = ADAM_B1 * m + (1.0 - ADAM_B1) * g
    v = ADAM_B2 * v + (1.0 - ADAM_B2) * jnp.square(g)
    m_hat = m / (1.0 - ADAM_B1 ** ADAM_STEP)
    v_hat = v / (1.0 - ADAM_B2 ** ADAM_STEP)
    delta = -ADAM_LR * (m_hat / (jnp.sqrt(v_hat) + ADAM_EPS) + ADAM_WD * w)
    return delta, m, v


def _as2d(a):
    return a.reshape(-1, a.shape[-1])


def _adamw_shard(w, g, m, v, name):
    shape = w.shape
    ins = [_as2d(a) for a in (w, g, m, v)]
    cols = ins[0].shape[1]
    outs = _ew(_adamw_fn, [(a, True) for a in ins], [(cols, F32, "tile")] * 3, 256, name)
    return [o.reshape(shape) for o in outs]


def _adamw_small(ws, gs, ms, vs, name):
    shapes = [w.shape for w in ws]
    flat = lambda a: a.reshape(-1, 128) if a.size % 128 == 0 else a.reshape(1, -1)
    ins = [flat(a) for grp in zip(ws, gs, ms, vs) for a in grp]
    k = len(ws)

    def fn(*vals):
        out = []
        for i in range(k):
            out += list(_adamw_fn(*vals[4 * i:4 * i + 4]))
        return out

    outs = _whole(fn, ins, [(ins[4 * (i // 3)].shape, F32) for i in range(3 * k)], name)
    deltas = [outs[3 * i].reshape(shapes[i]) for i in range(k)]
    new_m = [outs[3 * i + 1].reshape(shapes[i]) for i in range(k)]
    new_v = [outs[3 * i + 2].reshape(shapes[i]) for i in range(k)]
    return deltas, new_m, new_v


def _sum8(stacked, name):
    def fn(a):
        s = a[0:1]
        for i in range(1, N_DEV):
            s = s + a[i:i + 1]
        return s
    w = stacked.shape[1]
    tw = 8192
    if w % tw:
        return _whole(fn, [stacked], [((1, w), F32)], name)[0]

    def body(a_ref, o_ref):
        o_ref[...] = fn(a_ref[...])

    return pl.pallas_call(body, name=name, grid=(w // tw,), in_specs=[pl.BlockSpec((N_DEV, tw), lambda i: (0, i))],
                          out_specs=pl.BlockSpec((1, tw), lambda i: (0, i)), out_shape=jax.ShapeDtypeStruct((1, w), F32))(stacked)


ROWS = 512


def _split_heads(p, nh):
    t = p.shape[0]
    return p.reshape(t, nh, p.shape[1] // nh).transpose(1, 0, 2)


def _merge_heads(p):
    nh, t, d = p.shape
    return p.transpose(1, 0, 2).reshape(t, nh * d)


def _layer_fwd(h, mod, w, small, rope, l, late=None):
    sh1, sc1, gt1, sh2, sc2, gt2 = mod
    rope_ret, rope_mla = rope
    t = h.shape[0]
    nm = lambda s: f"l{l}_{s}"
    a1 = _ew(_norm_mod_fn, [(h, True), (small["norm1_g"], False), (sc1, False), (sh1, False)], [(D_MODEL, BF16, "tile")], ROWS, nm("norm1"))[0]
    p_s5 = _mm(a1, w["w_in_t"], "nt", 512, 512, 2048, name=nm("proj_s5"), n=512)
    p_ret = _mm(a1, w["w_in_t"], "nt", 512, 512, 2048, name=nm("proj_ret"), b_off=1, n=1536)
    p_swa = _mm(a1, w["w_in_t"], "nt", 512, 256, 2048, name=nm("proj_swa"), b_off=8, n=768)
    p_mla = _mm(a1, w["w_in_t"][2816:], "nt", 512, 576, 2048, name=nm("proj_mla"))
    b3, c3, dskip, a_r, a_i = small["s5"]
    st_r, st_i = _s5_scan_fwd(p_s5, b3, a_r, a_i, nm("s5_scan"))
    ypre = _mm_blocks(st_r, c3[:S5_BLOCKS], "nn", 512, pair=(st_i, c3[S5_BLOCKS:]), name=nm("s5_y"))
    z = _ew(_s5_act_fn, [(ypre, True), (p_s5, True), (dskip, False)], [(GROUP_WIDTH, F32, "tile")], ROWS, nm("s5_act"))[0]
    zz = _mm(z, w["glu_w"], "nn", 512, 512, 512, name=nm("s5_zz"))
    y_s5 = _ew(_s5_glu_fn, [(z, True), (zz, True), (small["s5_glu_b"], False)], [(GROUP_WIDTH, BF16, "tile")], ROWS, nm("s5_glu"))[0]
    qk_ret = _split_heads(_rope(p_ret[:, :2 * RET_HEADS * RET_QK], rope_ret, nm("ret_rope")), 2 * RET_HEADS)
    o_ret, y_ret = _ret_fwd(qk_ret, p_ret, small["ret_lgam"], nm("ret"))
    qkv_swa = _split_heads(p_swa, 12)
    o_swa, lse_swa = _swa_fwd(qkv_swa, small["swa_sinks"], nm("swa"))
    y_swa = _merge_heads(o_swa).astype(BF16)
    cq, ckv, kr = p_mla[:, :MLA_Q_RANK], p_mla[:, MLA_Q_RANK:MLA_Q_RANK + MLA_KV_RANK], p_mla[:, MLA_Q_RANK + MLA_KV_RANK:]
    cqn = _ew(_rms_gain_fn, [(cq, True), (small["mla_q_norm"], False)], [(MLA_Q_RANK, BF16, "tile")], ROWS, nm("mla_qnorm"))[0]
    ckvn = _ew(_rms_gain_fn, [(ckv, True), (small["mla_kv_norm"], False)], [(MLA_KV_RANK, BF16, "tile")], ROWS, nm("mla_kvnorm"))[0]
    q_full = _mm(cqn, w["w_uq_t"], "nt", 512, 768, 384, name=nm("mla_q"))
    kv_full = _mm(ckvn, w["w_ukv_t"], "nt", 512, 1024, 128, BF16, name=nm("mla_kv"))
    nq = q_full.shape[1]
    roped = _rope(jnp.concatenate([q_full, kr, jnp.zeros_like(kr)], axis=1), rope_mla, nm("mla_rope"), out_dtype=BF16)
    q4 = roped[:, :nq].reshape(t, MLA_HEADS, MLA_NOPE + MLA_ROPE)
    qn = q4[:, :, :MLA_NOPE].transpose(1, 0, 2)
    rot = jnp.concatenate([q4[:, :, MLA_NOPE:].transpose(1, 0, 2), roped[None, :, nq:nq + MLA_ROPE]], axis=0)
    o_mla, lse_mla = _mla_attend(qn, rot, kv_full, nm("mla"))
    cat = jnp.concatenate([y_s5, y_ret, y_swa, o_mla.astype(BF16)], axis=1)
    if late is not None:
        w = {**w, **late(lse_mla)}
    mixed = _mm(cat, w["w_out"], "nn", 512, 1024, 2048, name=nm("out_proj"))
    h1 = _ew(_gate_add_fn, [(h, True), (mixed, True), (gt1, False)], [(D_MODEL, F32, "tile")], ROWS, nm("res1"))[0]
    a2 = _ew(_norm_mod_fn, [(h1, True), (small["norm2_g"], False), (sc2, False), (sh2, False)], [(D_MODEL, BF16, "tile")], ROWS, nm("norm2"))[0]
    hid, act = _mm(a2, w["w1_t"], "nt", 1024, 1024, 2048, name=nm("mlp1"), epi=lambda acc: (acc, _relu2_fn(acc)), epi_outs=[F32, BF16])
    mo = _mm(act, w["w2"], "nn", 1024, 1024, 2048, name=nm("mlp2"))
    h2 = _ew(_gate_add_fn, [(h1, True), (mo, True), (gt2, False)], [(D_MODEL, F32, "tile")], ROWS, nm("res2"))[0]
    saved = dict(w=w, h=h, a1=a1, p_s5=p_s5, p_ret=p_ret, st_r=st_r, st_i=st_i, ypre=ypre, z=z, zz=zz, qk_ret=qk_ret, o_ret=o_ret,
                 qkv_swa=qkv_swa, o_swa=o_swa, lse_swa=lse_swa, cq=cq, ckv=ckv, cqn=cqn, ckvn=ckvn, qn=qn, rot=rot,
                 kv_full=kv_full, o_mla=o_mla, lse_mla=lse_mla, cat=cat, mixed=mixed, h1=h1, a2=a2, hid=hid, act=act, mo=mo)
    return h2, saved


def _layer_bwd(dh2, mod, w, small, rope, s, l, after_mlp=None, after_out=None):
    sh1, sc1, gt1, sh2, sc2, gt2 = mod
    rope_ret, rope_mla = rope
    t = dh2.shape[0]
    nm = lambda n: f"l{l}_{n}_bwd"
    gb, gs = {}, {}
    row = (D_MODEL, F32, "acc")
    dmo, dgt2 = _ew(lambda d, y, gt: (d * gt, jnp.sum(d * y, axis=0, keepdims=True)),
                    [(dh2, True), (s["mo"], True), (gt2, False)], [(D_MODEL, BF16, "tile"), row], ROWS, nm("res2"))
    dhid = _mm(dmo, w["w2"], "nt", 1024, 1024, 2048, name=nm("mlp2_x"), epi=lambda acc, x: (acc * 2.0 * jnp.maximum(x, 0.0),),
               epi_ins=[s["hid"]], epi_outs=[BF16])[0]
    gb["w2"] = _mm(s["act"], dmo, "tn", 1024, 1024, 4096, BF16, name=nm("mlp2_w"))
    da2 = _mm(dhid, w["w1_t"], "nn", 1024, 1024, 2048, name=nm("mlp1_x"))
    gb["w1_t"] = _mm(dhid, s["a2"], "tn", 1024, 1024, 4096, BF16, name=nm("mlp1_w"))
    if after_mlp is not None:
        gt1 = gt1 + after_mlp(gb)

    def norm_bwd(hh, g, sc, sh, da, dres):
        dh_, dg, dsc, dsh = _vjp_block(_norm_mod_fn, 4)(hh, g, sc, sh, da)
        return dh_ + dres, dg, dsc, dsh

    dh1, gs["norm2_g"], dsc2, dsh2 = _ew(norm_bwd, [(s["h1"], True), (small["norm2_g"], False), (sc2, False), (sh2, False), (da2, True), (dh2, True)],
                                         [(D_MODEL, F32, "tile"), row, row, row], ROWS, nm("norm2"))
    dmixed, dgt1 = _ew(lambda d, y, gt: (d * gt, jnp.sum(d * y, axis=0, keepdims=True)),
                       [(dh1, True), (s["mixed"], True), (gt1, False)], [(D_MODEL, BF16, "tile"), row], ROWS, nm("res1"))
    dcat = _mm(dmixed, w["w_out"], "nt", 512, 1024, 2048, name=nm("out_proj_x"))
    gb["w_out"] = _mm(s["cat"], dmixed, "tn", 1024, 1024, 4096, BF16, name=nm("out_proj_w"))
    if after_out is not None:
        small = {**small, "s5_glu_b": small["s5_glu_b"] + after_out(gb)}
    dy_s5, dy_ret, dy_swa, dy_mla = (dcat[:, i * GROUP_WIDTH:(i + 1) * GROUP_WIDTH] for i in range(4))
    b3, c3, dskip, a_r, a_i = small["s5"]
    gw = (GROUP_WIDTH, F32, "tile")
    gacc = (GROUP_WIDTH, F32, "acc")
    dz_a, dzz, gs["s5_glu_b"] = _ew(_vjp_block(_s5_glu_fn, 3), [(s["z"], True), (s["zz"], True), (small["s5_glu_b"], False), (dy_s5, True)],
                                    [gw, gw, gacc], ROWS, nm("s5_glu"))
    dz_b = _mm(dzz, w["glu_w"], "nt", 512, 512, 512, name=nm("s5_zz_x"))
    gb["glu_w"] = _mm(s["z"], dzz, "tn", 512, 512, 1024, BF16, name=nm("s5_zz_w"))

    def act_bwd(ypre, u, dsk, dza, dzb):
        return _vjp_block(_s5_act_fn, 3)(ypre, u, dsk, dza + dzb)

    dypre, du_a, g_dskip = _ew(act_bwd, [(s["ypre"], True), (s["p_s5"], True), (dskip, False), (dz_a, True), (dz_b, True)],
                               [gw, gw, gacc], ROWS, nm("s5_act"))
    ch, st = S5_PACK * S5_CH, S5_PACK * S5_STATE
    g_c3 =jnp.concatenate([_mm_blocks_tn(s["st_r"], dypre, st, ch, lambda j: j, nm("s5_y_w_re")),
                            _mm_blocks_tn(s["st_i"], dypre, st, ch, lambda j: j, nm("s5_y_w_im"))], axis=0)
    dbu_r, dbu_i, g_ar, g_ai = _s5_scan_bwd(dypre, c3, s["st_r"], s["st_i"], a_r, a_i, nm("s5_scan"))
    du_b = _mm_blocks(dbu_r, b3[:S5_BLOCKS], "nt", 512, pair=(dbu_i, b3[S5_BLOCKS:]), name=nm("s5_bu_x"))
    g_b3 = jnp.concatenate([_mm_blocks_tn(s["p_s5"], dbu_r, ch, st, lambda j: j, nm("s5_bu_w_re")),
                            _mm_blocks_tn(s["p_s5"], dbu_i, ch, st, lambda j: j, nm("s5_bu_w_im"))], axis=0)
    gs["s5"] = (g_b3, g_c3, g_dskip, g_ar, g_ai)
    dqk_rot, dk_rot, dv_ret, dg_ret = _ret_bwd(s["qk_ret"], s["p_ret"], s["o_ret"], dy_ret, small["ret_lgam"], nm("ret"))
    dqk = _rope(_merge_heads(jnp.concatenate([dqk_rot, dk_rot], axis=0)), rope_ret, nm("ret_rope"), inverse=True, out_dtype=BF16)
    dqkv_swa, gs["swa_sinks"] = _swa_bwd(s["qkv_swa"], s["o_swa"], s["lse_swa"], _split_heads(dy_swa, SWA_HEADS), small["swa_sinks"], nm("swa"))
    dqn, dqr, dkn, dv_mla, dkr_heads = _mla_attend_bwd(s["qn"], s["rot"], s["kv_full"], s["lse_mla"], s["o_mla"], dy_mla, nm("mla_att"))
    dkv_full = jnp.stack([dkn.reshape(t, MLA_HEADS, MLA_NOPE), dv_mla.reshape(t, MLA_HEADS, MLA_V)], axis=2).reshape(t, 2 * MLA_HEADS * MLA_NOPE)
    dkr_rot = _ew(lambda a, b, c, d: a + b + c + d, [(dkr_heads[i], True) for i in range(MLA_HEADS)], [(MLA_ROPE, F32, "tile")], ROWS, nm("mla_dkr"))[0]
    nq = MLA_HEADS * (MLA_NOPE + MLA_ROPE)
    dq_rot = jnp.concatenate([dqn.transpose(1, 0, 2), dqr.transpose(1, 0, 2)], axis=2).reshape(t, nq)
    droped = _rope(jnp.concatenate([dq_rot, dkr_rot, jnp.zeros_like(dkr_rot)], axis=1), rope_mla, nm("mla_rope"), inverse=True, out_dtype=BF16)
    dq_full, dkr = droped[:, :nq], droped[:, nq:nq + MLA_ROPE]
    dcqn = _mm(dq_full, w["w_uq_t"], "nn", 512, 384, 768, name=nm("mla_q_x"))
    gb["w_uq_t"] = _mm(dq_full, s["cqn"], "tn", 768, 384, 1024, BF16, name=nm("mla_q_w"))
    dckvn = _mm(dkv_full, w["w_ukv_t"], "nn", 512, 128, 1024, name=nm("mla_kv_x"))
    gb["w_ukv_t"] = _mm(dkv_full, s["ckvn"], "tn", 1024, 128, 1024, BF16, name=nm("mla_kv_w"))
    dcq, gs["mla_q_norm"] = _ew(_vjp_block(_rms_gain_fn, 2), [(s["cq"], True), (small["mla_q_norm"], False), (dcqn, True)],
                                [(MLA_Q_RANK, BF16, "tile"), (MLA_Q_RANK, F32, "acc")], ROWS, nm("mla_qnorm"))
    dckv, gs["mla_kv_norm"] = _ew(_vjp_block(_rms_gain_fn, 2), [(s["ckv"], True), (small["mla_kv_norm"], False), (dckvn, True)],
                                  [(MLA_KV_RANK, BF16, "tile"), (MLA_KV_RANK, F32, "acc")], ROWS, nm("mla_kvnorm"))
    du = _ew(lambda a, b: a + b, [(du_a, True), (du_b, True)], [(GROUP_WIDTH, BF16, "tile")], ROWS, nm("s5_du"))[0]
    bf = lambda a: a.astype(BF16)
    dproj = jnp.concatenate([du, bf(dqk), bf(dv_ret), bf(dg_ret), bf(_merge_heads(dqkv_swa)), bf(dcq), bf(dckv), bf(dkr)], axis=1)
    da1 = _mm(dproj, w["w_in_t"], "nn", 512, 1024, N_IN, name=nm("proj_x"))
    gb["w_in_t"] = _mm(dproj, s["a1"], "tn", N_IN, 512, 2048, BF16, name=nm("proj_w"))
    dh, gs["norm1_g"], dsc1, dsh1 = _ew(norm_bwd, [(s["h"], True), (small["norm1_g"], False), (sc1, False), (sh1, False), (da1, True), (dh1, True)],
                                        [(D_MODEL, F32, "tile"), row, row, row], ROWS, nm("norm1"))
    dmod = jnp.concatenate([dsh1, dsc1, dgt1, dsh2, dsc2, dgt2], axis=1)
    return dh, gb, gs, dmod


BIG = ("w_in_t", "w1_t", "w_uq_t", "w_ukv_t", "w_out", "w2", "glu_w")
MLP_BIG = ("w1_t", "w2")
OUT_BIG = ("w_out",)
LATE_BIG = ("w_out", "w1_t", "w2")
S5_NAMES = ("s5_lambda_re", "s5_lambda_im", "s5_log_dt", "s5_b_re", "s5_b_im", "s5_c_re", "s5_c_im", "s5_d")


def kernel(x, c, norm1_g, norm2_g, ada_w, ada_b, w_in, s5_lambda_re, s5_lambda_im, s5_log_dt, s5_b_re, s5_b_im, s5_c_re, s5_c_im, s5_d, s5_glu_w, s5_glu_b, swa_sinks, mla_q_norm, mla_kv_norm, mla_w_uq, mla_w_ukv, w_out, mlp_w1, mlp_w2, final_norm_g, loss_target, m_norm1_g, m_norm2_g, m_ada_w, m_ada_b, m_w_in, m_s5_lambda_re, m_s5_lambda_im, m_s5_log_dt, m_s5_b_re, m_s5_b_im, m_s5_c_re, m_s5_c_im, m_s5_d, m_s5_glu_w, m_s5_glu_b, m_swa_sinks, m_mla_q_norm, m_mla_kv_norm, m_mla_w_uq, m_mla_w_ukv, m_w_out, m_mlp_w1, m_mlp_w2, m_final_norm_g, v_norm1_g, v_norm2_g, v_ada_w, v_ada_b, v_w_in, v_s5_lambda_re, v_s5_lambda_im, v_s5_log_dt, v_s5_b_re, v_s5_b_im, v_s5_c_re, v_s5_c_im, v_s5_d, v_s5_glu_w, v_s5_glu_b, v_swa_sinks, v_mla_q_norm, v_mla_kv_norm, v_mla_w_uq, v_mla_w_ukv, v_w_out, v_mlp_w1, v_mlp_w2, v_final_norm_g):
    names = ["norm1_g", "norm2_g", "ada_w", "ada_b", "w_in", "s5_lambda_re", "s5_lambda_im", "s5_log_dt", "s5_b_re", "s5_b_im",
             "s5_c_re", "s5_c_im", "s5_d", "s5_glu_w", "s5_glu_b", "swa_sinks", "mla_q_norm", "mla_kv_norm", "mla_w_uq",
             "mla_w_ukv", "w_out", "mlp_w1", "mlp_w2", "final_norm_g"]
    env = locals()
    wts = {n: env[n] for n in names}
    mom = {n: env["m_" + n] for n in names}
    var = {n: env["v_" + n] for n in names}
    t = x.shape[1]
    me = 4 * lax.axis_index("x") + 2 * lax.axis_index("y") + lax.axis_index("c")
    ret_lgam = jnp.log1p(-(2.0 ** (-5.0 - jnp.arange(RET_HEADS, dtype=F32))))

    tr = lambda a: a.transpose(0, 2, 1)
    shard = {"w_in_t": tr(w_in), "w1_t": tr(mlp_w1), "w_uq_t": tr(mla_w_uq), "w_ukv_t": tr(mla_w_ukv),
             "w_out": w_out, "w2": mlp_w2, "glu_w": s5_glu_w}
    to_send = [{k: shard[k][l].astype(BF16) for k in BIG} for l in range(DEPTH)]
    as_rows = lambda keys, arrs: {k: a.reshape(-1, shard[k].shape[2]) for k, a in zip(keys, arrs)}
    first = [k for k in BIG if k not in LATE_BIG]
    own_slot = lambda k: 4 * lax.axis_index("x") + 2 * lax.axis_index("y") + lax.axis_index("c")

    def gather_start(arrs, tag, after=None):
        return _push_start(arrs, [jax.ShapeDtypeStruct((N_DEV,) + a.shape, a.dtype) for a in arrs], GATHER_PEERS,
                           lambda ref, k: ref, own_slot, f"gather_weights_{tag}_start", after)

    def gather_finish(started, after, tag):
        sent, landed = _push_wait(started[0], started[1], started[2], started[3], after, GATHER_PEERS, lambda ref, k: ref, own_slot,
                                  f"gather_weights_{tag}_wait")
        with_own = [lax.dynamic_update_index_in_dim(full, own, me, 0) for full, own in zip(landed, sent)]
        return _pass_to_sibling(with_own, f"gather_weights_{tag}_pass")

    gather_first = gather_start([to_send[0][k] for k in first] + [c], "first")
    zero = gather_first[4][0, 0]
    rope = _rope_tables(t, zero)
    smalls, s5_pulls = [], []
    for l in range(DEPTH):
        s5_ops, pull = jax.vjp(_s5_prep, *[wts[n][l] + zero if n == "s5_log_dt" else wts[n][l] for n in S5_NAMES])
        s5_pulls.append(pull)
        smalls.append(dict(norm1_g=norm1_g[l][None], norm2_g=norm2_g[l][None], s5=s5_ops, s5_glu_b=s5_glu_b[l][None],
                           swa_sinks=swa_sinks[l], mla_q_norm=mla_q_norm[l][None], mla_kv_norm=mla_kv_norm[l][None], ret_lgam=ret_lgam))
    setup_done = (smalls[0]["s5"][3][0, 0] + smalls[1]["s5"][3][0, 0] + rope[0][0][0, 0] + rope[1][0][0, 0]).reshape(1, 1)
    gathered = gather_finish(gather_first, setup_done, "first")
    c_all = gathered[-1].reshape(N_DEV, D_MODEL)
    big = [as_rows(first, gathered[:len(first)]), None]

    c_act = _whole(lambda v: v * jax.nn.sigmoid(v), [c_all], [((N_DEV, D_MODEL), F32)], "cond_silu")[0]
    c_pad = jnp.concatenate([c_act, jnp.zeros((128 - N_DEV, D_MODEL), F32)], axis=0)
    cols = ada_w.shape[2]
    mod_part = [_mm(c_pad, ada_w[l], "nn", 128, cols, 512, name=f"l{l}_mod")[:N_DEV] for l in range(DEPTH)]
    mod_all = _all_gather([jnp.stack(mod_part)], "gather_mod")[0]
    gather0 = gather_start([to_send[0][k] for k in LATE_BIG], "l0", after=mod_all)
    gather1 = gather_start([to_send[1][k] for k in BIG], "l1", after=gather0[4])
    mod_rows = lax.dynamic_index_in_dim(mod_all, me, axis=2, keepdims=False)
    mods = []
    for l in range(DEPTH):
        row = mod_rows[:, l].reshape(1, 6 * D_MODEL) + ada_b[l][None]
        if l == 0:
            row = row + (gather0[4][0, 0] + gather1[4][0, 0])
        mods.append([row[:, i * D_MODEL:(i + 1) * D_MODEL] for i in range(6)])

    h = x[0]
    saved = []
    for l in range(DEPTH):
        if l == 0:
            late = lambda after: as_rows(LATE_BIG, gather_finish(gather0, after, "l0"))
        else:
            big[1] = as_rows(BIG, gather_finish(gather1, h, "l1"))
            late = None
        h, s = _layer_fwd(h, mods[l], big[l], smalls[l], rope, l, late)
        big[l] = s.pop("w")
        saved.append(s)

    fg = final_norm_g[None]
    tgt = loss_target[0]
    def final_both(hh, g, tg):
        part, pull = jax.vjp(_final_fn, hh, g, tg)
        dh_, dg, _ = pull(jnp.ones((1, 1), F32))
        return dh_, dg, part

    dh, g_final, loss_local = _ew(final_both, [(h, True), (fg, False), (tgt, True)],
                                  [(D_MODEL, F32, "tile"), (D_MODEL, F32, "acc"), (1, F32, "acc")], ROWS, "loss")
    loss = lax.psum(loss_local[0, 0], ("x", "y", "c"))

    core, chip = lax.axis_index("c"), 2 * lax.axis_index("x") + lax.axis_index("y")

    def core_stage(g_layer, keys, tag):
        g_list = [g_layer[k].reshape(N_DEV, -1, g_layer[k].shape[1]) for k in keys]
        got = _scatter_core(g_list, f"scatter_core_{tag}")
        return [_pick_sum(g, o, core, lambda q, c_: 2 * q + c_, BF16, f"{tag}_core_sum_{k}") for k, g, o in zip(keys, g_list, got)]

    def their_block(ref, k):
        x_, y_ = lax.axis_index("x"), lax.axis_index("y")
        dx, dy, _ = CHIP_PEERS[k]
        return ref.at[2 * (1 - x_ if dx else x_) + (1 - y_ if dy else y_)]

    def chips_start(halves, tag, after=None):
        return _push_start(halves, [jax.ShapeDtypeStruct((3,) + a.shape[1:], a.dtype) for a in halves], CHIP_PEERS,
                           their_block, lambda k: k, f"scatter_chips_{tag}_start", after)

    def chips_wait(started, after, tag):
        return _push_wait(started[0], started[1], started[2], started[3], after, CHIP_PEERS, their_block, lambda k: k, f"scatter_chips_{tag}_wait")

    def sibling_block(ref, q):
        return ref.at[2 * q + 1 - lax.axis_index("c")]

    g_small, dmods = [None] * DEPTH, [None] * DEPTH
    dh, g_big1, g_small[1], dmods[1] = _layer_bwd(dh, mods[1], big[1], smalls[1], rope, saved[1], 1)
    g_list1 = [g_big1[k].reshape(N_DEV, -1, g_big1[k].shape[1]) for k in BIG]
    core1 = _push_start(g_list1, [jax.ShapeDtypeStruct((4,) + g.shape[1:], g.dtype) for g in g_list1], CORE_PEERS,
                        sibling_block, lambda q: q, "scatter_core_l1_start")
    mods0 = [m + core1[4][0, 0] for m in mods[0]]
    early = {}

    def after_mlp(gb):
        g_mine, got = _push_wait(core1[0], core1[1], core1[2], core1[3], gb["w1_t"], CORE_PEERS, sibling_block, lambda q: q,
                                 "scatter_core_l1_wait")
        halves = [_pick_sum(g, o, core, lambda q, c_: 2 * q + c_, BF16, f"l1_core_sum_{k}") for k, g, o in zip(BIG, g_mine, got)]
        early["l1"] = chips_start(halves, "l1")
        g_list = [gb[k].reshape(N_DEV, -1, gb[k].shape[1]) for k in MLP_BIG]
        early["mlp_core"] = _push_start(g_list, [jax.ShapeDtypeStruct((4,) + g.shape[1:], g.dtype) for g in g_list], CORE_PEERS,
                                        sibling_block, lambda q: q, "scatter_core_l0_mlp_start", early["l1"][4])
        return early["l1"][4][0, 0] + early["mlp_core"][4][0, 0]

    def after_out(gb):
        started = early["mlp_core"]
        g_mine, got = _push_wait(started[0], started[1], started[2], started[3], gb["w_out"], CORE_PEERS, sibling_block, lambda q: q,
                                 "scatter_core_l0_mlp_wait")
        halves = [_pick_sum(g, o, core, lambda q, c_: 2 * q + c_, BF16, f"l0_mlp_core_sum_{k}") for k, g, o in zip(MLP_BIG, g_mine, got)]
        early["mlp"] = chips_start(halves, "l0_mlp")
        early["out"] = chips_start(core_stage(gb, OUT_BIG, "l0_out"), "l0_out", early["mlp"][4])
        return early["mlp"][4][0, 0] + early["out"][4][0, 0]

    dh, g_big0, g_small[0], dmods[0] = _layer_bwd(dh, mods0, big[0], smalls[0], rope, saved[0], 0, after_mlp=after_mlp, after_out=after_out)
    grad_x = dh[None]
    rest = [k for k in BIG if k not in MLP_BIG + OUT_BIG]
    halves_rest = core_stage(g_big0, rest, "l0_rest")

    small_parts = []
    for l in range(DEPTH):
        gs = g_small[l]
        s5g = s5_pulls[l](gs["s5"])
        small_parts += [gs["norm1_g"], gs["norm2_g"], *s5g, gs["s5_glu_b"], gs["swa_sinks"], gs["mla_q_norm"], gs["mla_kv_norm"]]
    small_parts += [g_final, *dmods]
    sizes = [int(np.prod(p.shape)) for p in small_parts]
    flat = jnp.concatenate([p.reshape(1, -1) for p in small_parts], axis=1)
    pad = (-flat.shape[1]) % 8192
    flat = jnp.pad(flat, ((0, 0), (0, pad)))
    flat_all = _all_gather([flat], "gather_small_grads")[0].reshape(N_DEV, -1)
    summed = _sum8(flat_all, "sum_small_grads")
    chips_rest = chips_start(halves_rest, "l0_rest", after=summed)
    rest_token = chips_rest[4]
    halves1, landed1 = chips_wait(early["l1"], rest_token, "l1")
    halves_mlp, landed_mlp = chips_wait(early["mlp"], rest_token, "l0_mlp")
    halves_out, landed_out = chips_wait(early["out"], rest_token, "l0_out")
    terms = {(1, k): pair for k, pair in zip(BIG, zip(halves1, landed1))}
    terms.update({(0, k): pair for k, pair in zip(MLP_BIG, zip(halves_mlp, landed_mlp))})
    terms.update({(0, k): pair for k, pair in zip(OUT_BIG, zip(halves_out, landed_out))})
    chip_sum = lambda l, k: _pick_sum(*terms[l, k], chip, lambda q, m_: m_, F32, f"l{l}_chip_sum_{k}")
    pieces, off = [], 0
    for sz in sizes:
        pieces.append(summed[0, off:off + sz])
        off += sz
    small_names = ["norm1_g", "norm2_g", *S5_NAMES, "s5_glu_b", "swa_sinks", "mla_q_norm", "mla_kv_norm"]
    per_layer = len(small_names)
    grads = {}
    for i, n in enumerate(small_names):
        grads[n] = jnp.stack([pieces[l * per_layer + i].reshape(wts[n].shape[1:]) for l in range(DEPTH)])
    grads["final_norm_g"] = pieces[DEPTH * per_layer]
    grads["ada_b"] = jnp.stack([pieces[DEPTH * per_layer + 1 + l] for l in range(DEPTH)])

    mod_off = sum(sizes[:DEPTH * per_layer + 1])
    dmod_all = flat_all[:, mod_off:mod_off + DEPTH * 6 * D_MODEL].reshape(N_DEV, DEPTH, N_DEV, cols)
    dmod_mine = lax.dynamic_index_in_dim(dmod_all, me, axis=2, keepdims=False).transpose(1, 0, 2)
    dmod_pad = jnp.concatenate([dmod_mine, jnp.zeros((DEPTH, 128 - N_DEV, cols), F32)], axis=1)
    grads["ada_w"] = jnp.stack([_mm(c_pad, dmod_pad[l], "tn", 512, cols, 128, name=f"l{l}_ada_w_grad") for l in range(DEPTH)])

    out_g, out_d, out_m, out_v = dict(grads), {}, {}, {}
    orig = {"w_in_t": "w_in", "w1_t": "mlp_w1", "w_uq_t": "mla_w_uq", "w_ukv_t": "mla_w_ukv", "w_out": "w_out", "w2": "mlp_w2", "glu_w": "s5_glu_w"}

    def update_big(keys):
        for k in keys:
            n = orig[k]
            out_g[n] = jnp.stack([chip_sum(l, k) for l in range(DEPTH)])
            if k.endswith("_t"):
                out_g[n] = tr(out_g[n])
            out_d[n], out_m[n], out_v[n] = _adamw_shard(wts[n], out_g[n], mom[n], var[n], f"adamw_{n}")

    update_big(MLP_BIG + OUT_BIG)
    out_d["ada_w"], out_m["ada_w"], out_v["ada_w"] = _adamw_shard(wts["ada_w"], out_g["ada_w"], mom["ada_w"], var["ada_w"], "adamw_ada_w")
    small_all = small_names + ["ada_b", "final_norm_g"]
    ds, ms, vs = _adamw_small([wts[n] for n in small_all], [grads[n] for n in small_all], [mom[n] for n in small_all],
                              [var[n] for n in small_all], "adamw_small")
    for n, d, m_, v_ in zip(small_all, ds, ms, vs):
        out_d[n], out_m[n], out_v[n] = d, m_, v_
    halves_rest, landed_rest = chips_wait(chips_rest, out_d["ada_w"], "l0_rest")
    terms.update({(0, k): pair for k, pair in zip(rest, zip(halves_rest, landed_rest))})
    update_big(rest)
    return (loss, grad_x, *[out_g[n] for n in names], *[out_d[n] for n in names], *[out_m[n] for n in names], *[out_v[n] for n in names])
```
